```python
import math
import jax, jax.numpy as jnp
from jax import lax
import numpy as np

D_MODEL = 1024
BATCH = 8
SEQ = 8192
DEPTH = 2

N_META = 16
MLA_HEADS = 8
Q_LORA = 768
KV_LORA = 256
QK_NOPE = 128
QK_ROPE = 64
V_HEAD = 128
ROPE_THETA = 10000.0
Q_BLOCK = 128
NEG_INF = -1e30
SSD_INNER = 2 * D_MODEL
SSD_HEAD_DIM = 64
SSD_HEADS = SSD_INNER // SSD_HEAD_DIM
SSD_GROUPS = 4
SSD_HEADS_PER_GROUP = SSD_HEADS // SSD_GROUPS
SSD_STATE = 128
SSD_CONV = 4
SSD_CONV_DIM = SSD_INNER + 2 * SSD_GROUPS * SSD_STATE
CHUNK = 128
DT_MIN = 0.001
DT_MAX = 0.1
D_FF = 2816
FFN_CONV = 3
LN_EPS = 1e-5
RMS_EPS = 1e-6
DEEPNORM_ALPHA = (2 * DEPTH) ** 0.25
DEEPNORM_BETA = (8 * DEPTH) ** -0.25
IN_SIZES = (Q_LORA, KV_LORA, QK_ROPE, SSD_INNER, SSD_CONV_DIM, SSD_HEADS, D_MODEL, D_MODEL)
IN_COLS = sum(IN_SIZES)

kernel_name = "hybrid_mla_ssd_gated_deepnorm"


def layer_norm(x, g, b):
    xf = x.astype(jnp.float32)
    mu = jnp.mean(xf, axis=-1, keepdims=True)
    var = jnp.mean(jnp.square(xf - mu), axis=-1, keepdims=True)
    y = (xf - mu) * lax.rsqrt(var + LN_EPS) * g.astype(jnp.float32) + b.astype(jnp.float32)
    return y.astype(x.dtype)


def rms_norm(x, g):
    xf = x.astype(jnp.float32)
    y = xf * lax.rsqrt(jnp.mean(xf * xf, axis=-1, keepdims=True) + RMS_EPS) * g.astype(jnp.float32)
    return y.astype(x.dtype)


def causal_dwconv(x, w, b):
    k = w.shape[0]
    y = lax.conv_general_dilated(
        x, w[:, None, :].astype(x.dtype), window_strides=(1,), padding=((k - 1, 0),),
        dimension_numbers=("NWC", "WIO", "NWC"), feature_group_count=x.shape[-1])
    return y + b.astype(x.dtype)


def front_pad(t, pad):
    return jnp.pad(t, [(0, 0), (pad, 0)] + [(0, 0)] * (t.ndim - 2))


def rope_tables(length):
    inv_freq = 1.0 / (ROPE_THETA ** (jnp.arange(0, QK_ROPE, 2, dtype=jnp.float32) / QK_ROPE))
    ang = jnp.arange(length, dtype=jnp.float32)[:, None] * inv_freq[None, :]
    ang = jnp.concatenate([ang, ang], axis=-1)
    return jnp.cos(ang), jnp.sin(ang)


def apply_rope(x, cos, sin):
    xf = x.astype(jnp.float32)
    x1, x2 = jnp.split(xf, 2, axis=-1)
    rot = jnp.concatenate([-x2, x1], axis=-1)
    return (xf * cos + rot * sin).astype(x.dtype)


def mla_branch(q_lat, kv_lat, k_pe, cos, sin, q_norm_g, w_q_b, kv_norm_g, w_kv_b, w_o):
    b, l, _ = q_lat.shape
    q = (rms_norm(q_lat, q_norm_g) @ w_q_b).reshape(b, l, MLA_HEADS, QK_NOPE + QK_ROPE)
    q_nope = q[..., :QK_NOPE]
    q_pe = apply_rope(q[..., QK_NOPE:], cos[:, None, :], sin[:, None, :])
    kv = (rms_norm(kv_lat, kv_norm_g) @ w_kv_b).reshape(b, l, MLA_HEADS, QK_NOPE + V_HEAD)
    k_nope, v = kv[..., :QK_NOPE], kv[..., QK_NOPE:]
    k_pe = apply_rope(k_pe, cos, sin)
    pad = (-l) % Q_BLOCK
    q_nope, q_pe, k_nope, k_pe, v = [front_pad(t, pad) for t in (q_nope, q_pe, k_nope, k_pe, v)]
    lp = l + pad
    scale = (QK_NOPE + QK_ROPE) ** -0.5
    key_pos = jnp.arange(lp)

    def attend_block(blk):
        start = blk * Q_BLOCK
        qn = lax.dynamic_slice_in_dim(q_nope, start, Q_BLOCK, axis=1)
        qr = lax.dynamic_slice_in_dim(q_pe, start, Q_BLOCK, axis=1)
        s = (jnp.einsum("bqhd,bkhd->bhqk", qn, k_nope).astype(jnp.float32)
             + jnp.einsum("bqhr,bkr->bhqk", qr, k_pe).astype(jnp.float32))
        q_pos = start + jnp.arange(Q_BLOCK)
        visible = (key_pos[None, :] <= q_pos[:, None]) & (key_pos[None, :] >= pad)
        p = jax.nn.softmax(jnp.where(visible, s * scale, NEG_INF), axis=-1)
        return jnp.einsum("bhqk,bkhd->bqhd", p.astype(v.dtype), v)

    o = lax.map(attend_block, jnp.arange(lp // Q_BLOCK))
    o = jnp.moveaxis(o, 0, 1).reshape(b, lp, MLA_HEADS * V_HEAD)[:, pad:]
    return o @ w_o


def ssd_branch(z, xbc, dt_raw, conv_w, conv_b, dt_bias, a_log, d_skip, norm_g, w_o):
    b, l, _ = xbc.shape
    dtype = xbc.dtype
    f32 = jnp.float32
    xbc = jax.nn.silu(causal_dwconv(xbc, conv_w, conv_b)).astype(f32)
    xs, bm, cm = jnp.split(xbc, [SSD_INNER, SSD_INNER + SSD_GROUPS * SSD_STATE], axis=-1)
    dt = jax.nn.softplus(dt_raw.astype(f32) + dt_bias.astype(f32))
    a = -jnp.exp(a_log.astype(f32))
    pad = (-l) % CHUNK
    lp = l + pad
    nc = lp // CHUNK
    g, e = SSD_GROUPS, SSD_HEADS_PER_GROUP
    dt_c = front_pad(dt, pad).reshape(b, nc, CHUNK, g, e)
    x_c = front_pad(xs, pad).reshape(b, nc, CHUNK, g, e, SSD_HEAD_DIM) * dt_c[..., None]
    b_c = front_pad(bm, pad).reshape(b, nc, CHUNK, g, SSD_STATE)
    c_c = front_pad(cm, pad).reshape(b, nc, CHUNK, g, SSD_STATE)
    a_c = jnp.transpose(dt_c * a.reshape(g, e), (0, 3, 4, 1, 2))
    a_cs = jnp.cumsum(a_c, axis=-1)
    causal = jnp.tril(jnp.ones((CHUNK, CHUNK), dtype=bool))
    decay = jnp.exp(jnp.where(causal, a_cs[..., :, None] - a_cs[..., None, :], -jnp.inf))
    cb = jnp.einsum("bclgn,bcsgn->bgcls", c_c, b_c)
    y_diag = jnp.einsum("bgecls,bcsgep->bclgep", cb[:, :, None] * decay, x_c)
    decay_states = jnp.exp(a_cs[..., -1:] - a_cs)
    states = jnp.einsum("bclgn,bgecl,bclgep->cbgepn", b_c, decay_states, x_c)
    chunk_decay = jnp.moveaxis(jnp.exp(a_cs[..., -1]), -1, 0)

    def carry_state(h, inp):
        s_c, d_c = inp
        return d_c[..., None, None] * h + s_c, h

    h0 = jnp.zeros(states.shape[1:], f32)
    _, prev = lax.scan(carry_state, h0, (states, chunk_decay))
    y_off = jnp.einsum("bclgn,cbgepn,bgecl->bclgep", c_c, prev, jnp.exp(a_cs))
    y = (y_diag + y_off).reshape(b, lp, SSD_HEADS, SSD_HEAD_DIM)[:, pad:]
    y = y + xs.reshape(b, l, SSD_HEADS, SSD_HEAD_DIM) * d_skip.astype(f32)[:, None]
    gs = SSD_INNER // SSD_GROUPS
    y = y.reshape(b, l, SSD_GROUPS, gs) * jax.nn.silu(z.astype(f32)).reshape(b, l, SSD_GROUPS, gs)
    y = y * lax.rsqrt(jnp.mean(y * y, axis=-1, keepdims=True) + RMS_EPS)
    y = y.reshape(b, l, SSD_INNER) * norm_g.astype(f32)
    return y.astype(dtype) @ w_o


def conv_glu_ffn(h, w_up, conv_w, conv_b, w_down):
    u = causal_dwconv(h @ w_up, conv_w, conv_b)
    gate, val = jnp.split(u, 2, axis=-1)
    return (jax.nn.silu(gate) * val) @ w_down


def _fwd_setup_inputs(seed: int = 0) -> dict:
    key = jax.random.key(seed)
    ks = iter(jax.random.split(key, 40))
    nrm = lambda shape, scale: jax.random.normal(next(ks), shape, jnp.float32) * scale
    gain = lambda shape: 1.0 + nrm(shape, 0.02)
    bias = lambda shape: nrm(shape, 0.02)
    L = DEPTH
    x = nrm((BATCH, SEQ, D_MODEL), 1.0)
    meta_tokens = nrm((N_META, D_MODEL), 1.0)
    emb_ln_g = gain((D_MODEL,))
    emb_ln_b = bias((D_MODEL,))
    w_in = nrm((L, D_MODEL, IN_COLS), D_MODEL ** -0.5)
    q_norm_g = gain((L, Q_LORA))
    w_q_b = nrm((L, Q_LORA, MLA_HEADS * (QK_NOPE + QK_ROPE)), Q_LORA ** -0.5)
    kv_norm_g = gain((L, KV_LORA))
    w_kv_b = nrm((L, KV_LORA, MLA_HEADS * (QK_NOPE + V_HEAD)), KV_LORA ** -0.5)
    w_o_attn = nrm((L, MLA_HEADS * V_HEAD, D_MODEL), (MLA_HEADS * V_HEAD) ** -0.5)
    ssd_conv_w = nrm((L, SSD_CONV, SSD_CONV_DIM), SSD_CONV ** -0.5)
    ssd_conv_b = bias((L, SSD_CONV_DIM))
    u = jax.random.uniform(next(ks), (L, SSD_HEADS), jnp.float32)
    dt0 = jnp.exp(u * (math.log(DT_MAX) - math.log(DT_MIN)) + math.log(DT_MIN))
    dt_bias = dt0 + jnp.log(-jnp.expm1(-dt0))
    a_log = jnp.log(jax.random.uniform(next(ks), (L, SSD_HEADS), jnp.float32, 1.0, 16.0))
    d_skip = gain((L, SSD_HEADS))
    ssd_norm_g = gain((L, SSD_INNER))
    w_o_ssd = nrm((L, SSD_INNER, D_MODEL), SSD_INNER ** -0.5)
    w_out = nrm((L, D_MODEL, D_MODEL), DEEPNORM_BETA * D_MODEL ** -0.5)
    ln1_g = gain((L, D_MODEL))
    ln1_b = bias((L, D_MODEL))
    w_up = nrm((L, D_MODEL, 2 * D_FF), D_MODEL ** -0.5)
    ffn_conv_w = nrm((L, FFN_CONV, 2 * D_FF), FFN_CONV ** -0.5)
    ffn_conv_b = bias((L, 2 * D_FF))
    w_down = nrm((L, D_FF, D_MODEL), DEEPNORM_BETA * D_FF ** -0.5)
    ln2_g = gain((L, D_MODEL))
    ln2_b = bias((L, D_MODEL))
    return {"x": x, "meta_tokens": meta_tokens, "emb_ln_g": emb_ln_g, "emb_ln_b": emb_ln_b,
            "w_in": w_in, "q_norm_g": q_norm_g, "w_q_b": w_q_b, "kv_norm_g": kv_norm_g,
            "w_kv_b": w_kv_b, "w_o_attn": w_o_attn, "ssd_conv_w": ssd_conv_w, "ssd_conv_b": ssd_conv_b,
            "dt_bias": dt_bias, "a_log": a_log, "d_skip": d_skip, "ssd_norm_g": ssd_norm_g,
            "w_o_ssd": w_o_ssd, "w_out": w_out, "ln1_g": ln1_g, "ln1_b": ln1_b, "w_up": w_up,
            "ffn_conv_w": ffn_conv_w, "ffn_conv_b": ffn_conv_b, "w_down": w_down,
            "ln2_g": ln2_g, "ln2_b": ln2_b}


def _fwd_reference(x, meta_tokens, emb_ln_g, emb_ln_b, w_in, q_norm_g, w_q_b, kv_norm_g, w_kv_b,
              w_o_attn, ssd_conv_w, ssd_conv_b, dt_bias, a_log, d_skip, ssd_norm_g, w_o_ssd,
              w_out, ln1_g, ln1_b, w_up, ffn_conv_w, ffn_conv_b, w_down, ln2_g, ln2_b):
    b = x.shape[0]
    meta = jnp.broadcast_to(meta_tokens.astype(x.dtype)[None], (b, N_META, D_MODEL))
    h = layer_norm(jnp.concatenate([meta, x], axis=1), emb_ln_g, emb_ln_b)
    cos, sin = rope_tables(h.shape[1])
    splits = np.cumsum(IN_SIZES)[:-1].tolist()
    for i in range(DEPTH):
        proj = h @ w_in[i]
        q_lat, kv_lat, k_pe, z, xbc, dt_raw, g_attn, g_ssd = jnp.split(proj, splits, axis=-1)
        y_attn = mla_branch(q_lat, kv_lat, k_pe, cos, sin, q_norm_g[i], w_q_b[i],
                            kv_norm_g[i], w_kv_b[i], w_o_attn[i])
        y_ssd = ssd_branch(z, xbc, dt_raw, ssd_conv_w[i], ssd_conv_b[i], dt_bias[i], a_log[i],
                           d_skip[i], ssd_norm_g[i], w_o_ssd[i])
        mixed = jax.nn.sigmoid(g_attn) * y_attn + jax.nn.sigmoid(g_ssd) * y_ssd
        h = layer_norm(DEEPNORM_ALPHA * h + mixed @ w_out[i], ln1_g[i], ln1_b[i])
        ffn = conv_glu_ffn(h, w_up[i], ffn_conv_w[i], ffn_conv_b[i], w_down[i])
        h = layer_norm(DEEPNORM_ALPHA * h + ffn, ln2_g[i], ln2_b[i])
    return h[:, N_META:]


import jax as _jax
import jax.numpy as _jnp

TWIN_FORMAT = 'train_step'
FWD_PARAMS = ['x', 'meta_tokens', 'emb_ln_g', 'emb_ln_b', 'w_in', 'q_norm_g', 'w_q_b', 'kv_norm_g', 'w_kv_b', 'w_o_attn', 'ssd_conv_w', 'ssd_conv_b', 'dt_bias', 'a_log', 'd_skip', 'ssd_norm_g', 'w_o_ssd', 'w_out', 'ln1_g', 'ln1_b', 'w_up', 'ffn_conv_w', 'ffn_conv_b', 'w_down', 'ln2_g', 'ln2_b']
TWIN_WEIGHTS = ['meta_tokens', 'emb_ln_g', 'emb_ln_b', 'w_in', 'q_norm_g', 'w_q_b', 'kv_norm_g', 'w_kv_b', 'w_o_attn', 'ssd_conv_w', 'ssd_conv_b', 'dt_bias', 'a_log', 'd_skip', 'ssd_norm_g', 'w_o_ssd', 'w_out', 'ln1_g', 'ln1_b', 'w_up', 'ffn_conv_w', 'ffn_conv_b', 'w_down', 'ln2_g', 'ln2_b']
TWIN_DIFF_INPUT = 'x'
TWIN_INPUTS = ['x', 'meta_tokens', 'emb_ln_g', 'emb_ln_b', 'w_in', 'q_norm_g', 'w_q_b', 'kv_norm_g', 'w_kv_b', 'w_o_attn', 'ssd_conv_w', 'ssd_conv_b', 'dt_bias', 'a_log', 'd_skip', 'ssd_norm_g', 'w_o_ssd', 'w_out', 'ln1_g', 'ln1_b', 'w_up', 'ffn_conv_w', 'ffn_conv_b', 'w_down', 'ln2_g', 'ln2_b', 'loss_target', 'm_meta_tokens', 'm_emb_ln_g', 'm_emb_ln_b', 'm_w_in', 'm_q_norm_g', 'm_w_q_b', 'm_kv_norm_g', 'm_w_kv_b', 'm_w_o_attn', 'm_ssd_conv_w', 'm_ssd_conv_b', 'm_dt_bias', 'm_a_log', 'm_d_skip', 'm_ssd_norm_g', 'm_w_o_ssd', 'm_w_out', 'm_ln1_g', 'm_ln1_b', 'm_w_up', 'm_ffn_conv_w', 'm_ffn_conv_b', 'm_w_down', 'm_ln2_g', 'm_ln2_b', 'v_meta_tokens', 'v_emb_ln_g', 'v_emb_ln_b', 'v_w_in', 'v_q_norm_g', 'v_w_q_b', 'v_kv_norm_g', 'v_w_kv_b', 'v_w_o_attn', 'v_ssd_conv_w', 'v_ssd_conv_b', 'v_dt_bias', 'v_a_log', 'v_d_skip', 'v_ssd_norm_g', 'v_w_o_ssd', 'v_w_out', 'v_ln1_g', 'v_ln1_b', 'v_w_up', 'v_ffn_conv_w', 'v_ffn_conv_b', 'v_w_down', 'v_ln2_g', 'v_ln2_b']
TWIN_OUTPUTS = ['loss', 'grad_x', 'grad_meta_tokens', 'grad_emb_ln_g', 'grad_emb_ln_b', 'grad_w_in', 'grad_q_norm_g', 'grad_w_q_b', 'grad_kv_norm_g', 'grad_w_kv_b', 'grad_w_o_attn', 'grad_ssd_conv_w', 'grad_ssd_conv_b', 'grad_dt_bias', 'grad_a_log', 'grad_d_skip', 'grad_ssd_norm_g', 'grad_w_o_ssd', 'grad_w_out', 'grad_ln1_g', 'grad_ln1_b', 'grad_w_up', 'grad_ffn_conv_w', 'grad_ffn_conv_b', 'grad_w_down', 'grad_ln2_g', 'grad_ln2_b', 'delta_meta_tokens', 'delta_emb_ln_g', 'delta_emb_ln_b', 'delta_w_in', 'delta_q_norm_g', 'delta_w_q_b', 'delta_kv_norm_g', 'delta_w_kv_b', 'delta_w_o_attn', 'delta_ssd_conv_w', 'delta_ssd_conv_b', 'delta_dt_bias', 'delta_a_log', 'delta_d_skip', 'delta_ssd_norm_g', 'delta_w_o_ssd', 'delta_w_out', 'delta_ln1_g', 'delta_ln1_b', 'delta_w_up', 'delta_ffn_conv_w', 'delta_ffn_conv_b', 'delta_w_down', 'delta_ln2_g', 'delta_ln2_b', 'new_m_meta_tokens', 'new_m_emb_ln_g', 'new_m_emb_ln_b', 'new_m_w_in', 'new_m_q_norm_g', 'new_m_w_q_b', 'new_m_kv_norm_g', 'new_m_w_kv_b', 'new_m_w_o_attn', 'new_m_ssd_conv_w', 'new_m_ssd_conv_b', 'new_m_dt_bias', 'new_m_a_log', 'new_m_d_skip', 'new_m_ssd_norm_g', 'new_m_w_o_ssd', 'new_m_w_out', 'new_m_ln1_g', 'new_m_ln1_b', 'new_m_w_up', 'new_m_ffn_conv_w', 'new_m_ffn_conv_b', 'new_m_w_down', 'new_m_ln2_g', 'new_m_ln2_b', 'new_v_meta_tokens', 'new_v_emb_ln_g', 'new_v_emb_ln_b', 'new_v_w_in', 'new_v_q_norm_g', 'new_v_w_q_b', 'new_v_kv_norm_g', 'new_v_w_kv_b', 'new_v_w_o_attn', 'new_v_ssd_conv_w', 'new_v_ssd_conv_b', 'new_v_dt_bias', 'new_v_a_log', 'new_v_d_skip', 'new_v_ssd_norm_g', 'new_v_w_o_ssd', 'new_v_w_out', 'new_v_ln1_g', 'new_v_ln1_b', 'new_v_w_up', 'new_v_ffn_conv_w', 'new_v_ffn_conv_b', 'new_v_w_down', 'new_v_ln2_g', 'new_v_ln2_b']
TWIN_LEAF_KINDS = {'loss': 'loss', 'grad_x': 'grad_x', 'grad_meta_tokens': 'grad_w', 'grad_emb_ln_g': 'grad_w', 'grad_emb_ln_b': 'grad_w', 'grad_w_in': 'grad_w', 'grad_q_norm_g': 'grad_w', 'grad_w_q_b': 'grad_w', 'grad_kv_norm_g': 'grad_w', 'grad_w_kv_b': 'grad_w', 'grad_w_o_attn': 'grad_w', 'grad_ssd_conv_w': 'grad_w', 'grad_ssd_conv_b': 'grad_w', 'grad_dt_bias': 'grad_w', 'grad_a_log': 'grad_w', 'grad_d_skip': 'grad_w', 'grad_ssd_norm_g': 'grad_w', 'grad_w_o_ssd': 'grad_w', 'grad_w_out': 'grad_w', 'grad_ln1_g': 'grad_w', 'grad_ln1_b': 'grad_w', 'grad_w_up': 'grad_w', 'grad_ffn_conv_w': 'grad_w', 'grad_ffn_conv_b': 'grad_w', 'grad_w_down': 'grad_w', 'grad_ln2_g': 'grad_w', 'grad_ln2_b': 'grad_w', 'delta_meta_tokens': 'delta_w', 'delta_emb_ln_g': 'delta_w', 'delta_emb_ln_b': 'delta_w', 'delta_w_in': 'delta_w', 'delta_q_norm_g': 'delta_w', 'delta_w_q_b': 'delta_w', 'delta_kv_norm_g': 'delta_w', 'delta_w_kv_b': 'delta_w', 'delta_w_o_attn': 'delta_w', 'delta_ssd_conv_w': 'delta_w', 'delta_ssd_conv_b': 'delta_w', 'delta_dt_bias': 'delta_w', 'delta_a_log': 'delta_w', 'delta_d_skip': 'delta_w', 'delta_ssd_norm_g': 'delta_w', 'delta_w_o_ssd': 'delta_w', 'delta_w_out': 'delta_w', 'delta_ln1_g': 'delta_w', 'delta_ln1_b': 'delta_w', 'delta_w_up': 'delta_w', 'delta_ffn_conv_w': 'delta_w', 'delta_ffn_conv_b': 'delta_w', 'delta_w_down': 'delta_w', 'delta_ln2_g': 'delta_w', 'delta_ln2_b': 'delta_w', 'new_m_meta_tokens': 'new_m', 'new_m_emb_ln_g': 'new_m', 'new_m_emb_ln_b': 'new_m', 'new_m_w_in': 'new_m', 'new_m_q_norm_g': 'new_m', 'new_m_w_q_b': 'new_m', 'new_m_kv_norm_g': 'new_m', 'new_m_w_kv_b': 'new_m', 'new_m_w_o_attn': 'new_m', 'new_m_ssd_conv_w': 'new_m', 'new_m_ssd_conv_b': 'new_m', 'new_m_dt_bias': 'new_m', 'new_m_a_log': 'new_m', 'new_m_d_skip': 'new_m', 'new_m_ssd_norm_g': 'new_m', 'new_m_w_o_ssd': 'new_m', 'new_m_w_out': 'new_m', 'new_m_ln1_g': 'new_m', 'new_m_ln1_b': 'new_m', 'new_m_w_up': 'new_m', 'new_m_ffn_conv_w': 'new_m', 'new_m_ffn_conv_b': 'new_m', 'new_m_w_down': 'new_m', 'new_m_ln2_g': 'new_m', 'new_m_ln2_b': 'new_m', 'new_v_meta_tokens': 'new_v', 'new_v_emb_ln_g': 'new_v', 'new_v_emb_ln_b': 'new_v', 'new_v_w_in': 'new_v', 'new_v_q_norm_g': 'new_v', 'new_v_w_q_b': 'new_v', 'new_v_kv_norm_g': 'new_v', 'new_v_w_kv_b': 'new_v', 'new_v_w_o_attn': 'new_v', 'new_v_ssd_conv_w': 'new_v', 'new_v_ssd_conv_b': 'new_v', 'new_v_dt_bias': 'new_v', 'new_v_a_log': 'new_v', 'new_v_d_skip': 'new_v', 'new_v_ssd_norm_g': 'new_v', 'new_v_w_o_ssd': 'new_v', 'new_v_w_out': 'new_v', 'new_v_ln1_g': 'new_v', 'new_v_ln1_b': 'new_v', 'new_v_w_up': 'new_v', 'new_v_ffn_conv_w': 'new_v', 'new_v_ffn_conv_b': 'new_v', 'new_v_w_down': 'new_v', 'new_v_ln2_g': 'new_v', 'new_v_ln2_b': 'new_v'}


def _forward(args):
    return _fwd_reference(*[args[k] for k in FWD_PARAMS])


def _output_shape():
    def fwd():
        inp = _fwd_setup_inputs(0)
        return _fwd_reference(*[inp[k] for k in FWD_PARAMS])
    out = _jax.eval_shape(fwd)
    return out.shape, out.dtype

N_MICROBATCH = 1
ADAM_LR = 0.001
ADAM_B1 = 0.9
ADAM_B2 = 0.999
ADAM_EPS = 1e-08
ADAM_WD = 0.01
ADAM_STEP = 10
PER_EXAMPLE_BATCH_AXIS = {'x': 0, 'loss_target': 0}
SHARED_INPUTS = []
_WEIGHT_DTYPES = {'meta_tokens': _jnp.float32, 'emb_ln_g': _jnp.float32, 'emb_ln_b': _jnp.float32, 'w_in': _jnp.float32, 'q_norm_g': _jnp.float32, 'w_q_b': _jnp.float32, 'kv_norm_g': _jnp.float32, 'w_kv_b': _jnp.float32, 'w_o_attn': _jnp.float32, 'ssd_conv_w': _jnp.float32, 'ssd_conv_b': _jnp.float32, 'dt_bias': _jnp.float32, 'a_log': _jnp.float32, 'd_skip': _jnp.float32, 'ssd_norm_g': _jnp.float32, 'w_o_ssd': _jnp.float32, 'w_out': _jnp.float32, 'ln1_g': _jnp.float32, 'ln1_b': _jnp.float32, 'w_up': _jnp.float32, 'ffn_conv_w': _jnp.float32, 'ffn_conv_b': _jnp.float32, 'w_down': _jnp.float32, 'ln2_g': _jnp.float32, 'ln2_b': _jnp.float32}
MOMENT_SCALE = {'meta_tokens': 3.433945e-03, 'emb_ln_g': 1.928366e+00, 'emb_ln_b': 1.019925e+00, 'w_in': 2.754016e-02, 'q_norm_g': 9.265487e-03, 'w_q_b': 6.534023e-03, 'kv_norm_g': 2.343855e-02, 'w_kv_b': 8.019673e-03, 'w_o_attn': 9.213294e-03, 'ssd_conv_w': 3.206990e-02, 'ssd_conv_b': 5.363574e-02, 'dt_bias': 5.814196e-02, 'a_log': 1.149291e-01, 'd_skip': 1.955803e-01, 'ssd_norm_g': 3.856808e-02, 'w_o_ssd': 5.477569e-02, 'w_out': 1.089879e-01, 'ln1_g': 2.161153e+00, 'ln1_b': 1.028097e+00, 'w_up': 3.374514e-02, 'ffn_conv_w': 3.345199e-02, 'ffn_conv_b': 4.350349e-02, 'w_down': 1.101208e-01, 'ln2_g': 4.541847e+01, 'ln2_b': 2.594296e+00}


def _to_microbatches(a, axis):
    t = _jnp.moveaxis(a, axis, 0)
    t = t.reshape((N_MICROBATCH, t.shape[0] // N_MICROBATCH) + t.shape[1:])
    return _jnp.moveaxis(t, 1, axis + 1)


def setup_inputs(seed: int = 0) -> dict:
    inp = _fwd_setup_inputs(seed)
    key = _jax.random.fold_in(_jax.random.key(seed), 7919)
    shape, _ = _output_shape()
    out = dict(inp)
    out["loss_target"] = _jax.random.normal(_jax.random.fold_in(key, 0), shape, _jnp.float32)
    for i, name in enumerate(TWIN_WEIGHTS):
        w = inp[name].astype(_jnp.float32)
        if MOMENT_SCALE is None:
            s = _jnp.sqrt(_jnp.mean(_jnp.square(w)) + 1e-30)
        else:
            s = MOMENT_SCALE[name]
        km, kv = _jax.random.split(_jax.random.fold_in(key, i + 1))
        out[name] = w
        out["m_" + name] = s * _jax.random.normal(km, w.shape, _jnp.float32)
        out["v_" + name] = (s * s) * _jax.random.uniform(kv, w.shape, _jnp.float32, 0.5, 1.5)
    if N_MICROBATCH > 1:
        for name, axis in PER_EXAMPLE_BATCH_AXIS.items():
            out[name] = _to_microbatches(out[name], axis)
    return {'x': out['x'], 'meta_tokens': out['meta_tokens'], 'emb_ln_g': out['emb_ln_g'], 'emb_ln_b': out['emb_ln_b'], 'w_in': out['w_in'], 'q_norm_g': out['q_norm_g'], 'w_q_b': out['w_q_b'], 'kv_norm_g': out['kv_norm_g'], 'w_kv_b': out['w_kv_b'], 'w_o_attn': out['w_o_attn'], 'ssd_conv_w': out['ssd_conv_w'], 'ssd_conv_b': out['ssd_conv_b'], 'dt_bias': out['dt_bias'], 'a_log': out['a_log'], 'd_skip': out['d_skip'], 'ssd_norm_g': out['ssd_norm_g'], 'w_o_ssd': out['w_o_ssd'], 'w_out': out['w_out'], 'ln1_g': out['ln1_g'], 'ln1_b': out['ln1_b'], 'w_up': out['w_up'], 'ffn_conv_w': out['ffn_conv_w'], 'ffn_conv_b': out['ffn_conv_b'], 'w_down': out['w_down'], 'ln2_g': out['ln2_g'], 'ln2_b': out['ln2_b'], 'loss_target': out['loss_target'], 'm_meta_tokens': out['m_meta_tokens'], 'm_emb_ln_g': out['m_emb_ln_g'], 'm_emb_ln_b': out['m_emb_ln_b'], 'm_w_in': out['m_w_in'], 'm_q_norm_g': out['m_q_norm_g'], 'm_w_q_b': out['m_w_q_b'], 'm_kv_norm_g': out['m_kv_norm_g'], 'm_w_kv_b': out['m_w_kv_b'], 'm_w_o_attn': out['m_w_o_attn'], 'm_ssd_conv_w': out['m_ssd_conv_w'], 'm_ssd_conv_b': out['m_ssd_conv_b'], 'm_dt_bias': out['m_dt_bias'], 'm_a_log': out['m_a_log'], 'm_d_skip': out['m_d_skip'], 'm_ssd_norm_g': out['m_ssd_norm_g'], 'm_w_o_ssd': out['m_w_o_ssd'], 'm_w_out': out['m_w_out'], 'm_ln1_g': out['m_ln1_g'], 'm_ln1_b': out['m_ln1_b'], 'm_w_up': out['m_w_up'], 'm_ffn_conv_w': out['m_ffn_conv_w'], 'm_ffn_conv_b': out['m_ffn_conv_b'], 'm_w_down': out['m_w_down'], 'm_ln2_g': out['m_ln2_g'], 'm_ln2_b': out['m_ln2_b'], 'v_meta_tokens': out['v_meta_tokens'], 'v_emb_ln_g': out['v_emb_ln_g'], 'v_emb_ln_b': out['v_emb_ln_b'], 'v_w_in': out['v_w_in'], 'v_q_norm_g': out['v_q_norm_g'], 'v_w_q_b': out['v_w_q_b'], 'v_kv_norm_g': out['v_kv_norm_g'], 'v_w_kv_b': out['v_w_kv_b'], 'v_w_o_attn': out['v_w_o_attn'], 'v_ssd_conv_w': out['v_ssd_conv_w'], 'v_ssd_conv_b': out['v_ssd_conv_b'], 'v_dt_bias': out['v_dt_bias'], 'v_a_log': out['v_a_log'], 'v_d_skip': out['v_d_skip'], 'v_ssd_norm_g': out['v_ssd_norm_g'], 'v_w_o_ssd': out['v_w_o_ssd'], 'v_w_out': out['v_w_out'], 'v_ln1_g': out['v_ln1_g'], 'v_ln1_b': out['v_ln1_b'], 'v_w_up': out['v_w_up'], 'v_ffn_conv_w': out['v_ffn_conv_w'], 'v_ffn_conv_b': out['v_ffn_conv_b'], 'v_w_down': out['v_w_down'], 'v_ln2_g': out['v_ln2_g'], 'v_ln2_b': out['v_ln2_b']}


def _loss(weights, diff, rest, loss_target):
    with _jax.named_scope("forward"):
        args = {**rest, TWIN_DIFF_INPUT: diff, **{k: w.astype(_WEIGHT_DTYPES[k]) for k, w in weights.items()}}
        y = _forward(args)
    with _jax.named_scope("loss_head"):
        err = _jnp.square(y.astype(_jnp.float32) - loss_target)
        return 0.5 * _jnp.sum(_jnp.mean(err, axis=-1)) if err.ndim else 0.5 * err


def _adamw(w, g, m, v):
    m = ADAM_B1 * m + (1.0 - ADAM_B1) * g
    v = ADAM_B2 * v + (1.0 - ADAM_B2) * _jnp.square(g)
    m_hat = m / (1.0 - ADAM_B1 ** ADAM_STEP)
    v_hat = v / (1.0 - ADAM_B2 ** ADAM_STEP)
    delta = -ADAM_LR * (m_hat / (_jnp.sqrt(v_hat) + ADAM_EPS) + ADAM_WD * w)
    return delta, m, v


def reference(x, meta_tokens, emb_ln_g, emb_ln_b, w_in, q_norm_g, w_q_b, kv_norm_g, w_kv_b, w_o_attn, ssd_conv_w, ssd_conv_b, dt_bias, a_log, d_skip, ssd_norm_g, w_o_ssd, w_out, ln1_g, ln1_b, w_up, ffn_conv_w, ffn_conv_b, w_down, ln2_g, ln2_b, loss_target, m_meta_tokens, m_emb_ln_g, m_emb_ln_b, m_w_in, m_q_norm_g, m_w_q_b, m_kv_norm_g, m_w_kv_b, m_w_o_attn, m_ssd_conv_w, m_ssd_conv_b, m_dt_bias, m_a_log, m_d_skip, m_ssd_norm_g, m_w_o_ssd, m_w_out, m_ln1_g, m_ln1_b, m_w_up, m_ffn_conv_w, m_ffn_conv_b, m_w_down, m_ln2_g, m_ln2_b, v_meta_tokens, v_emb_ln_g, v_emb_ln_b, v_w_in, v_q_norm_g, v_w_q_b, v_kv_norm_g, v_w_kv_b, v_w_o_attn, v_ssd_conv_w, v_ssd_conv_b, v_dt_bias, v_a_log, v_d_skip, v_ssd_norm_g, v_w_o_ssd, v_w_out, v_ln1_g, v_ln1_b, v_w_up, v_ffn_conv_w, v_ffn_conv_b, v_w_down, v_ln2_g, v_ln2_b):
    given = dict(x=x, meta_tokens=meta_tokens, emb_ln_g=emb_ln_g, emb_ln_b=emb_ln_b, w_in=w_in, q_norm_g=q_norm_g, w_q_b=w_q_b, kv_norm_g=kv_norm_g, w_kv_b=w_kv_b, w_o_attn=w_o_attn, ssd_conv_w=ssd_conv_w, ssd_conv_b=ssd_conv_b, dt_bias=dt_bias, a_log=a_log, d_skip=d_skip, ssd_norm_g=ssd_norm_g, w_o_ssd=w_o_ssd, w_out=w_out, ln1_g=ln1_g, ln1_b=ln1_b, w_up=w_up, ffn_conv_w=ffn_conv_w, ffn_conv_b=ffn_conv_b, w_down=w_down, ln2_g=ln2_g, ln2_b=ln2_b, loss_target=loss_target, m_meta_tokens=m_meta_tokens, m_emb_ln_g=m_emb_ln_g, m_emb_ln_b=m_emb_ln_b, m_w_in=m_w_in, m_q_norm_g=m_q_norm_g, m_w_q_b=m_w_q_b, m_kv_norm_g=m_kv_norm_g, m_w_kv_b=m_w_kv_b, m_w_o_attn=m_w_o_attn, m_ssd_conv_w=m_ssd_conv_w, m_ssd_conv_b=m_ssd_conv_b, m_dt_bias=m_dt_bias, m_a_log=m_a_log, m_d_skip=m_d_skip, m_ssd_norm_g=m_ssd_norm_g, m_w_o_ssd=m_w_o_ssd, m_w_out=m_w_out, m_ln1_g=m_ln1_g, m_ln1_b=m_ln1_b, m_w_up=m_w_up, m_ffn_conv_w=m_ffn_conv_w, m_ffn_conv_b=m_ffn_conv_b, m_w_down=m_w_down, m_ln2_g=m_ln2_g, m_ln2_b=m_ln2_b, v_meta_tokens=v_meta_tokens, v_emb_ln_g=v_emb_ln_g, v_emb_ln_b=v_emb_ln_b, v_w_in=v_w_in, v_q_norm_g=v_q_norm_g, v_w_q_b=v_w_q_b, v_kv_norm_g=v_kv_norm_g, v_w_kv_b=v_w_kv_b, v_w_o_attn=v_w_o_attn, v_ssd_conv_w=v_ssd_conv_w, v_ssd_conv_b=v_ssd_conv_b, v_dt_bias=v_dt_bias, v_a_log=v_a_log, v_d_skip=v_d_skip, v_ssd_norm_g=v_ssd_norm_g, v_w_o_ssd=v_w_o_ssd, v_w_out=v_w_out, v_ln1_g=v_ln1_g, v_ln1_b=v_ln1_b, v_w_up=v_w_up, v_ffn_conv_w=v_ffn_conv_w, v_ffn_conv_b=v_ffn_conv_b, v_w_down=v_w_down, v_ln2_g=v_ln2_g, v_ln2_b=v_ln2_b)
    weights = {n: given[n] for n in TWIN_WEIGHTS}
    shared = {n: given[n] for n in SHARED_INPUTS}
    per_example = {n: given[n] for n in ['x']}
    grad_fn = _jax.value_and_grad(_loss, argnums=(0, 1))

    def one_microbatch(ex, loss_target):
        ex = dict(ex)
        diff = ex.pop(TWIN_DIFF_INPUT)
        return grad_fn(weights, diff, {**shared, **ex}, loss_target)

    if N_MICROBATCH == 1:
        loss, (grad_w, grad_x) = one_microbatch(per_example, given["loss_target"])
    else:
        def body(carry, xs):
            loss_sum, grad_sum = carry
            l_k, (gw_k, gx_k) = one_microbatch(xs[0], xs[1])
            with _jax.named_scope("update"):
                return (loss_sum + l_k, _jax.tree.map(_jnp.add, grad_sum, gw_k)), gx_k

        init = (_jnp.zeros((), _jnp.float32), _jax.tree.map(_jnp.zeros_like, weights))
        (loss, grad_w), grad_x = _jax.lax.scan(body, init, (per_example, given["loss_target"]))
    with _jax.named_scope("update"):
        delta_w, new_m, new_v = {}, {}, {}
        for n in TWIN_WEIGHTS:
            delta_w[n], new_m[n], new_v[n] = _adamw(weights[n], grad_w[n], given["m_" + n], given["v_" + n])
    return (loss, grad_x, *[grad_w[n] for n in TWIN_WEIGHTS], *[delta_w[n] for n in TWIN_WEIGHTS],
            *[new_m[n] for n in TWIN_WEIGHTS], *[new_v[n] for n in TWIN_WEIGHTS])
```

```python
import functools

import jax
import jax.numpy as jnp
import numpy as np
from jax import lax
from jax.experimental import pallas as pl
from jax.experimental.pallas import tpu as pltpu

F32 = jnp.float32
BF16 = jnp.bfloat16
HIGHEST = lax.Precision.HIGHEST

D_MODEL = 1024
DEPTH = 2
N_META = 16
HEADS = 8
Q_LORA = 768
KV_LORA = 256
QK_NOPE = 128
QK_ROPE = 64
V_HEAD = 128
ROPE_THETA = 10000.0
SSD_INNER = 2048
SSD_HEAD_DIM = 64
SSD_HEADS = 32
SSD_GROUPS = 4
SSD_STATE = 128
SSD_CONV = 4
SSD_CONV_DIM = SSD_INNER + 2 * SSD_GROUPS * SSD_STATE
CHUNK = 128
D_FF = 2816
FFN_CONV = 3
LN_EPS = 1e-5
RMS_EPS = 1e-6
ALPHA = (2 * DEPTH) ** 0.25
IN_SIZES = (Q_LORA, KV_LORA, QK_ROPE, SSD_INNER, SSD_CONV_DIM, SSD_HEADS, D_MODEL, D_MODEL)
ATT_SCALE = (QK_NOPE + QK_ROPE) ** -0.5
NEG_INF = -1e30
ADAM_LR, ADAM_B1, ADAM_B2, ADAM_EPS, ADAM_WD, ADAM_STEP = 0.001, 0.9, 0.999, 1e-08, 0.01, 10

LANES = 128
SUBLANES = 8
VMEM_BYTES = 64 * 1024 * 1024
N_DEV = 8

OQ, OKV, OZ, OXBC, OGA, OGS, OKPE, ODT = 0, 768, 1024, 3072, 6144, 7168, 8192, 8320
IN_PAD = 8448
QHEAD = 256

ROW_TILE = 640
MM_COL_TILE = 1408
MM_K_TILE = 1408
ATT_TILE = 640
ADAM_ROWS = 256


def _pick(n, target, q=LANES):
    assert n % q == 0, (n, q)
    units = n // q
    best = q
    for d in range(1, units + 1):
        if units % d == 0 and d * q <= target:
            best = d * q
    return best


def _params(sem, est_bytes):
    limit = int(min(VMEM_BYTES - (6 << 20), max(32 << 20, 2 * est_bytes + (8 << 20))))
    return pltpu.CompilerParams(dimension_semantics=sem, vmem_limit_bytes=limit)


def _nbytes(shape, dtype):
    return int(np.prod(shape)) * jnp.dtype(dtype).itemsize


def _mm(a, b, out_dtype, name, ta=False, add=None):
    if ta:
        K, M = a.shape
        tm = _pick(M, MM_COL_TILE)
        tk = _pick(K, ROW_TILE)
    else:
        M, K = a.shape
        tm = _pick(M, ROW_TILE)
        tk = _pick(K, MM_K_TILE)
    K2, N = b.shape
    assert K == K2
    tn = _pick(N, MM_COL_TILE)
    nk = K // tk
    dn = (((0,), (0,)), ((), ())) if ta else (((1,), (0,)), ((), ()))

    def body(*refs):
        if add is None:
            a_ref, b_ref, o_ref, acc = refs
        else:
            a_ref, b_ref, add_ref, o_ref, acc = refs
        k = pl.program_id(2)

        @pl.when(k == 0)
        def _():
            acc[...] = jnp.zeros_like(acc)

        acc[...] += lax.dot_general(a_ref[...].astype(BF16), b_ref[...].astype(BF16), dn,
                                    preferred_element_type=F32)

        @pl.when(k == nk - 1)
        def _():
            r = acc[...]
            if add is not None:
                r = r + add_ref[...].astype(F32)
            o_ref[...] = r.astype(out_dtype)

    if ta:
        a_spec = pl.BlockSpec((tk, tm), lambda i, j, k: (k, i))
    else:
        a_spec = pl.BlockSpec((tm, tk), lambda i, j, k: (i, k))
    in_specs = [a_spec, pl.BlockSpec((tk, tn), lambda i, j, k: (k, j))]
    args = [a, b]
    est = 2 * (tm * tk * a.dtype.itemsize + tk * tn * b.dtype.itemsize + tm * tn * 4) + tm * tn * 4
    if add is not None:
        in_specs.append(pl.BlockSpec((tm, tn), lambda i, j, k: (i, j)))
        args.append(add)
        est += 2 * tm * tn * 4
    return pl.pallas_call(
        body, name=name, grid=(M // tm, N // tn, nk), in_specs=in_specs,
        out_specs=pl.BlockSpec((tm, tn), lambda i, j, k: (i, j)),
        out_shape=jax.ShapeDtypeStruct((M, N), out_dtype),
        scratch_shapes=[pltpu.VMEM((tm, tn), F32)],
        compiler_params=_params(("parallel", "parallel", "arbitrary"), est),
    )(*args)


def _row(arr, bw=None, cb=0, grp=False, diff=True):
    return dict(arr=arr, bw=arr.shape[1] if bw is None else bw, cb=cb, grp=grp, diff=diff)


def _out(width, dtype, bw=None, grp=False):
    return dict(width=width, dtype=dtype, bw=width if bw is None else bw, grp=grp)


def _spec_rows(d, tm):
    return pl.BlockSpec((tm, d["bw"]), lambda g, i, cb=d["cb"], gr=d["grp"]: (i, cb + (g if gr else 0)))


def _spec_const(d):
    return pl.BlockSpec((d["arr"].shape[0], d["bw"]), lambda g, i, cb=d["cb"], gr=d["grp"]: (0, cb + (g if gr else 0)))


def _rw_fwd(fn, rows, consts, outs, name, ng=1, tm_target=LANES):
    Tp = rows[0]["arr"].shape[0]
    tm = _pick(Tp, tm_target)
    nr, ncst = len(rows), len(consts)

    def body(*refs):
        i = pl.program_id(1)
        rowidx = i * tm + lax.broadcasted_iota(jnp.int32, (tm, 1), 0)
        rv = [r[...].astype(F32) for r in refs[:nr]]
        cv = [c[...] for c in refs[nr:nr + ncst]]
        vals = fn(rowidx, *rv, *cv)
        for o, v in zip(refs[nr + ncst:], vals):
            o[...] = v.astype(o.dtype)

    est = sum(tm * d["bw"] * 4 for d in rows) + sum(tm * o["bw"] * 4 for o in outs)
    return pl.pallas_call(
        body, name=name, grid=(ng, Tp // tm),
        in_specs=[_spec_rows(d, tm) for d in rows] + [_spec_const(d) for d in consts],
        out_specs=[pl.BlockSpec((tm, o["bw"]), lambda g, i, gr=o["grp"]: (i, g if gr else 0)) for o in outs],
        out_shape=[jax.ShapeDtypeStruct((Tp, o["width"]), o["dtype"]) for o in outs],
        compiler_params=_params(("parallel", "parallel"), 3 * est),
    )(*[d["arr"] for d in rows], *[d["arr"] for d in consts])


def _rw_bwd(fn, rows, consts, cots, drow_dtypes, name, ng=1, tm_target=LANES):
    Tp = rows[0]["arr"].shape[0]
    tm = _pick(Tp, tm_target)
    nr, ncst, nct = len(rows), len(consts), len(cots)
    drows = [k for k, d in enumerate(rows) if d["diff"]]
    dcsts = [k for k, d in enumerate(consts) if d["diff"]]
    for k in drows:
        assert rows[k]["grp"] or ng == 1

    def body(*refs):
        g = pl.program_id(0)
        i = pl.program_id(1)
        rowidx = i * tm + lax.broadcasted_iota(jnp.int32, (tm, 1), 0)
        rv = [r[...].astype(F32) for r in refs[:nr]]
        cv = [c[...] for c in refs[nr:nr + ncst]]
        ct = tuple(r[...].astype(F32) for r in refs[nr + ncst:nr + ncst + nct])
        orefs = refs[nr + ncst + nct:]

        def f(*dargs):
            rr, cc = list(rv), list(cv)
            for k, v in zip(drows, dargs[:len(drows)]):
                rr[k] = v
            for k, v in zip(dcsts, dargs[len(drows):]):
                cc[k] = v
            return tuple(fn(rowidx, *rr, *cc))

        _, vjp = jax.vjp(f, *[rv[k] for k in drows], *[cv[k] for k in dcsts])
        grads = vjp(ct)
        for o, v in zip(orefs[:len(drows)], grads[:len(drows)]):
            o[...] = v.astype(o.dtype)
        for k, o, v in zip(dcsts, orefs[len(drows):], grads[len(drows):]):
            first = (i == 0) if consts[k]["grp"] else ((i == 0) & (g == 0))

            @pl.when(first)
            def _(o=o, v=v):
                o[...] = v

            @pl.when(jnp.logical_not(first))
            def _(o=o, v=v):
                o[...] += v

    out_specs, out_shape = [], []
    for k, dt in zip(drows, drow_dtypes):
        d = rows[k]
        out_specs.append(pl.BlockSpec((tm, d["bw"]), lambda g, i, gr=d["grp"]: (i, g if gr else 0)))
        out_shape.append(jax.ShapeDtypeStruct((Tp, d["bw"] * (ng if d["grp"] else 1)), dt))
    for k in dcsts:
        d = consts[k]
        r = d["arr"].shape[0]
        out_specs.append(pl.BlockSpec((r, d["bw"]), lambda g, i, gr=d["grp"]: (0, g if gr else 0)))
        out_shape.append(jax.ShapeDtypeStruct((r, d["bw"] * (ng if d["grp"] else 1)), F32))
    est = sum(tm * d["bw"] * 4 for d in rows) * 2 + sum(tm * d["bw"] * 4 for d in cots)
    res = pl.pallas_call(
        body, name=name, grid=(ng, Tp // tm),
        in_specs=[_spec_rows(d, tm) for d in rows] + [_spec_const(d) for d in consts] + [_spec_rows(d, tm) for d in cots],
        out_specs=out_specs, out_shape=out_shape,
        compiler_params=_params(("arbitrary", "arbitrary"), 3 * est),
    )(*[d["arr"] for d in rows], *[d["arr"] for d in consts], *[d["arr"] for d in cots])
    return list(res[:len(drows)]), list(res[len(drows):])


def _sigmoid(x):
    return 1.0 / (1.0 + jnp.exp(-x))


def _silu(x):
    return x * _sigmoid(x)


def _softplus(x):
    return jnp.maximum(x, 0.0) + jnp.log(1.0 + jnp.exp(-jnp.abs(x)))


def _layer_norm(x, g, b):
    mu = jnp.mean(x, axis=-1, keepdims=True)
    xc = x - mu
    var = jnp.mean(xc * xc, axis=-1, keepdims=True)
    return xc * lax.rsqrt(var + LN_EPS) * g + b


def _rms_norm(x, g):
    return x * lax.rsqrt(jnp.mean(x * x, axis=-1, keepdims=True) + RMS_EPS) * g


def _rope(r, cos, sin, rot):
    return r * cos + jnp.dot(r, rot, precision=HIGHEST, preferred_element_type=F32) * sin


def _make_stage_fns(npad):
    def fn_ln_masked(rowidx, x, g, b):
        return (jnp.where(rowidx >= npad, _layer_norm(x, g, b), 0.0),)

    def fn_in_post(rowidx, ql, kvl, kpe, dtr, cos, sin, rot, qg, kvg, dtb):
        qn = _rms_norm(ql, qg)
        kvn = _rms_norm(kvl, kvg)
        kr = _rope(kpe, cos, sin, rot)
        lane = lax.broadcasted_iota(jnp.int32, (1, LANES), 1)
        dt = jnp.where((rowidx >= npad) & (lane < SSD_HEADS), _softplus(dtr + dtb), 0.0)
        return qn, kvn, jnp.concatenate([kr] * HEADS, axis=1), dt

    def fn_q_post(rowidx, q, cos, sin, rot):
        rr = _rope(q[:, QK_NOPE:], cos, sin, rot)
        return (jnp.concatenate([q[:, :QK_NOPE], rr], axis=1) * ATT_SCALE,)

    def fn_gated_norm(rowidx, y, xs, z, dskip, g):
        v = (y + xs * dskip) * _silu(z)
        return (v * lax.rsqrt(jnp.mean(v * v, axis=-1, keepdims=True) + RMS_EPS) * g,)

    def fn_mix(rowidx, ga, gs, ya, ys):
        return (_sigmoid(ga) * ya + _sigmoid(gs) * ys,)

    def fn_res_ln(rowidx, h, r, g, b):
        return (jnp.where(rowidx >= npad, _layer_norm(ALPHA * h + r, g, b), 0.0),)

    def fn_glu(rowidx, u):
        return (_silu(u[:, :D_FF]) * u[:, D_FF:],)

    def fn_delta(rowidx, do, o):
        s = jnp.sum(do * o, axis=-1, keepdims=True)
        return (jnp.broadcast_to(s, do.shape),)

    return dict(ln=fn_ln_masked, in_post=fn_in_post, q_post=fn_q_post, gated=fn_gated_norm, mix=fn_mix,
                res_ln=fn_res_ln, glu=fn_glu, delta=fn_delta)


def _conv_tiles(Tp, C):
    return _pick(Tp, ROW_TILE), _pick(C, MM_COL_TILE)


def _conv_fwd(x, xoff, C, w8, b, K, act, npad, name):
    Tp = x.shape[0]
    tm, tc = _conv_tiles(Tp, C)
    assert xoff % tc == 0
    cb0 = xoff // tc
    rb = tm // SUBLANES

    def body(prev_ref, main_ref, w_ref, b_ref, o_ref):
        i = pl.program_id(1)
        main = main_ref[...].astype(F32)
        prev = jnp.where(i > 0, prev_ref[...].astype(F32), 0.0)
        ext = jnp.concatenate([prev, main], axis=0)
        acc = b_ref[...] + w_ref[K - 1:K, :] * main
        for k in range(K - 1):
            s = K - 1 - k
            acc = acc + w_ref[k:k + 1, :] * pltpu.roll(ext, s, 0)[SUBLANES:, :]
        if act:
            rowidx = i * tm + lax.broadcasted_iota(jnp.int32, (tm, 1), 0)
            acc = jnp.where(rowidx >= npad, _silu(acc), 0.0)
        o_ref[...] = acc.astype(o_ref.dtype)

    return pl.pallas_call(
        body, name=name, grid=(C // tc, Tp // tm),
        in_specs=[pl.BlockSpec((SUBLANES, tc), lambda g, i: (jnp.maximum(i * rb - 1, 0), cb0 + g)),
                  pl.BlockSpec((tm, tc), lambda g, i: (i, cb0 + g)),
                  pl.BlockSpec((SUBLANES, tc), lambda g, i: (0, g)),
                  pl.BlockSpec((1, tc), lambda g, i: (0, g))],
        out_specs=pl.BlockSpec((tm, tc), lambda g, i: (i, g)),
        out_shape=jax.ShapeDtypeStruct((Tp, C), F32),
        compiler_params=_params(("parallel", "parallel"), 8 * tm * tc * 4),
    )(x, x, w8, b)


def _conv_bwd(x, xoff, C, w8, b, dy, K, act, npad, name):
    Tp = x.shape[0]
    tm, tc = _conv_tiles(Tp, C)
    cb0 = xoff // tc
    rb = tm // SUBLANES
    ni = Tp // tm
    last_rb = Tp // SUBLANES - 1
    n = tm + 2 * SUBLANES

    def body(xp_ref, xm_ref, xn_ref, dym_ref, dyn_ref, w_ref, b_ref, dx_ref, dw_ref, db_ref):
        i = pl.program_id(1)
        prev = jnp.where(i > 0, xp_ref[...].astype(F32), 0.0)
        ext = jnp.concatenate([prev, xm_ref[...].astype(F32), xn_ref[...].astype(F32)], axis=0)
        dyn = jnp.where(i < ni - 1, dyn_ref[...].astype(F32), 0.0)
        dpre = jnp.concatenate([jnp.zeros((SUBLANES, tc), F32), dym_ref[...].astype(F32), dyn], axis=0)
        shifted = [ext if k == K - 1 else pltpu.roll(ext, K - 1 - k, 0) for k in range(K)]
        if act:
            pre = b_ref[...] + sum(w_ref[k:k + 1, :] * shifted[k] for k in range(K))
            rowidx = i * tm - SUBLANES + lax.broadcasted_iota(jnp.int32, (n, 1), 0)
            sg = _sigmoid(pre)
            dpre = jnp.where(rowidx >= npad, dpre * sg * (1.0 + pre * (1.0 - sg)), 0.0)
        dx = w_ref[K - 1:K, :] * dpre
        for k in range(K - 1):
            dx = dx + w_ref[k:k + 1, :] * pltpu.roll(dpre, n - (K - 1 - k), 0)
        dx_ref[...] = dx[SUBLANES:SUBLANES + tm, :].astype(dx_ref.dtype)

        @pl.when(i == 0)
        def _():
            dw_ref[...] = jnp.zeros_like(dw_ref)
            db_ref[...] = jnp.zeros_like(db_ref)

        dmain = dpre[SUBLANES:SUBLANES + tm, :]
        for k in range(K):
            dw_ref[k:k + 1, :] += jnp.sum(dmain * shifted[k][SUBLANES:SUBLANES + tm, :], axis=0, keepdims=True)
        db_ref[...] += jnp.sum(dmain, axis=0, keepdims=True)

    return pl.pallas_call(
        body, name=name, grid=(C // tc, ni),
        in_specs=[pl.BlockSpec((SUBLANES, tc), lambda g, i: (jnp.maximum(i * rb - 1, 0), cb0 + g)),
                  pl.BlockSpec((tm, tc), lambda g, i: (i, cb0 + g)),
                  pl.BlockSpec((SUBLANES, tc), lambda g, i: (jnp.minimum((i + 1) * rb, last_rb), cb0 + g)),
                  pl.BlockSpec((tm, tc), lambda g, i: (i, g)),
                  pl.BlockSpec((SUBLANES, tc), lambda g, i: (jnp.minimum((i + 1) * rb, last_rb), g)),
                  pl.BlockSpec((SUBLANES, tc), lambda g, i: (0, g)),
                  pl.BlockSpec((1, tc), lambda g, i: (0, g))],
        out_specs=[pl.BlockSpec((tm, tc), lambda g, i: (i, g)),
                   pl.BlockSpec((SUBLANES, tc), lambda g, i: (0, g)),
                   pl.BlockSpec((1, tc), lambda g, i: (0, g))],
        out_shape=[jax.ShapeDtypeStruct((Tp, C), BF16), jax.ShapeDtypeStruct((SUBLANES, C), F32),
                   jax.ShapeDtypeStruct((1, C), F32)],
        compiler_params=_params(("parallel", "arbitrary"), 14 * tm * tc * 4),
    )(x, x, x, dy, dy, w8, b)


def _flash_fwd(q, kv, kr8, npad, name):
    Tp = q.shape[0]
    t = _pick(Tp, ATT_TILE)
    nb = Tp // t
    nt = (((1,), (1,)), ((), ()))

    def body(q_ref, kn_ref, kr_ref, v_ref, o_ref, lse_ref, m_sc, l_sc, acc_sc):
        qi = pl.program_id(1)
        ki = pl.program_id(2)

        @pl.when(ki == 0)
        def _():
            m_sc[...] = jnp.full_like(m_sc, NEG_INF)
            l_sc[...] = jnp.zeros_like(l_sc)
            acc_sc[...] = jnp.zeros_like(acc_sc)

        def step(masked):
            k = jnp.concatenate([kn_ref[...], kr_ref[...]], axis=1)
            s = lax.dot_general(q_ref[...], k, nt, preferred_element_type=F32)
            if masked:
                row = qi * t + lax.broadcasted_iota(jnp.int32, (t, t), 0)
                col = ki * t + lax.broadcasted_iota(jnp.int32, (t, t), 1)
                s = jnp.where((col <= row) & (col >= npad), s, NEG_INF)
            m_prev = m_sc[...]
            m_new = jnp.maximum(m_prev, jnp.max(s, axis=-1, keepdims=True))
            p = jnp.exp(s - m_new)
            a = jnp.exp(m_prev - m_new)
            l_sc[...] = a * l_sc[...] + jnp.sum(p, axis=-1, keepdims=True)
            acc_sc[...] = a * acc_sc[...] + jnp.dot(p.astype(BF16), v_ref[...], preferred_element_type=F32)
            m_sc[...] = m_new

        need_mask = (ki == qi) | (ki == 0)

        @pl.when((ki <= qi) & need_mask)
        def _():
            step(True)

        @pl.when((ki <= qi) & jnp.logical_not(need_mask))
        def _():
            step(False)

        @pl.when(ki == qi)
        def _():
            l = l_sc[...]
            o_ref[...] = (acc_sc[...] / l).astype(o_ref.dtype)
            lse_ref[...] = jnp.broadcast_to(m_sc[...] + jnp.log(l), lse_ref.shape)

    kmap = lambda h, qi, ki: (jnp.minimum(ki, qi), h)
    return pl.pallas_call(
        body, name=name, grid=(HEADS, nb, nb),
        in_specs=[pl.BlockSpec((t, QHEAD), lambda h, qi, ki: (qi, h)),
                  pl.BlockSpec((t, QK_NOPE), kmap),
                  pl.BlockSpec((t, LANES), kmap),
                  pl.BlockSpec((t, V_HEAD), lambda h, qi, ki: (jnp.minimum(ki, qi), HEADS + h))],
        out_specs=[pl.BlockSpec((t, V_HEAD), lambda h, qi, ki: (qi, h)),
                   pl.BlockSpec((t, LANES), lambda h, qi, ki: (qi, h))],
        out_shape=[jax.ShapeDtypeStruct((Tp, HEADS * V_HEAD), F32), jax.ShapeDtypeStruct((Tp, HEADS * LANES), F32)],
        scratch_shapes=[pltpu.VMEM((t, 1), F32), pltpu.VMEM((t, 1), F32), pltpu.VMEM((t, V_HEAD), F32)],
        compiler_params=_params(("parallel", "parallel", "arbitrary"), 6 * t * t * 4),
    )(q, kv, kr8, kv)


def _flash_bwd(q, kv, kr8, do, lse, delta, npad, name):
    Tp = q.shape[0]
    t = _pick(Tp, ATT_TILE)
    nb = Tp // t
    nt = (((1,), (1,)), ((), ()))
    tn = (((0,), (0,)), ((), ()))

    def body(q_ref, kn_ref, kr_ref, v_ref, do_ref, lse_ref, dl_ref, dq_ref, dkn_ref, dkr_ref, dv_ref, dk_sc, dv_sc):
        ki = pl.program_id(1)
        qi = pl.program_id(2)

        @pl.when(qi == 0)
        def _():
            dk_sc[...] = jnp.zeros_like(dk_sc)
            dv_sc[...] = jnp.zeros_like(dv_sc)

        def step(masked):
            qv = q_ref[...]
            k = jnp.concatenate([kn_ref[...], kr_ref[...]], axis=1)
            s = lax.dot_general(qv, k, nt, preferred_element_type=F32)
            if masked:
                row = qi * t + lax.broadcasted_iota(jnp.int32, (t, t), 0)
                col = ki * t + lax.broadcasted_iota(jnp.int32, (t, t), 1)
                s = jnp.where((col <= row) & (col >= npad), s, NEG_INF)
            p = jnp.exp(s - lse_ref[:, :1])
            dob = do_ref[...].astype(BF16)
            dv_sc[...] += lax.dot_general(p.astype(BF16), dob, tn, preferred_element_type=F32)
            dp = lax.dot_general(dob, v_ref[...], nt, preferred_element_type=F32)
            ds = (p * (dp - dl_ref[:, :1])).astype(BF16)
            dk_sc[...] += lax.dot_general(ds, qv, tn, preferred_element_type=F32)
            dqc = jnp.dot(ds, k, preferred_element_type=F32)
            rows = pl.ds(pl.multiple_of(qi * t, t), t)

            @pl.when(ki == 0)
            def _():
                dq_ref[rows, :] = dqc

            @pl.when(ki > 0)
            def _():
                dq_ref[rows, :] += dqc

        need_mask = (ki == qi) | (ki == 0)

        @pl.when((qi >= ki) & need_mask)
        def _():
            step(True)

        @pl.when((qi >= ki) & jnp.logical_not(need_mask))
        def _():
            step(False)

        @pl.when(qi == nb - 1)
        def _():
            dkn_ref[...] = dk_sc[:, :QK_NOPE].astype(dkn_ref.dtype)
            dkr_ref[...] = dk_sc[:, QK_NOPE:].astype(dkr_ref.dtype)
            dv_ref[...] = dv_sc[...].astype(dv_ref.dtype)

    qmap = lambda h, ki, qi: (jnp.maximum(qi, ki), h)
    kmap = lambda h, ki, qi: (ki, h)
    est = 2 * Tp * QHEAD * 4 + 8 * t * t * 4
    return pl.pallas_call(
        body, name=name, grid=(HEADS, nb, nb),
        in_specs=[pl.BlockSpec((t, QHEAD), qmap),
                  pl.BlockSpec((t, QK_NOPE), kmap),
                  pl.BlockSpec((t, LANES), kmap),
                  pl.BlockSpec((t, V_HEAD), lambda h, ki, qi: (ki, HEADS + h)),
                  pl.BlockSpec((t, V_HEAD), qmap),
                  pl.BlockSpec((t, LANES), qmap),
                  pl.BlockSpec((t, LANES), qmap)],
        out_specs=[pl.BlockSpec((Tp, QHEAD), lambda h, ki, qi: (0, h)),
                   pl.BlockSpec((t, QK_NOPE), kmap),
                   pl.BlockSpec((t, LANES), kmap),
                   pl.BlockSpec((t, V_HEAD), kmap)],
        out_shape=[jax.ShapeDtypeStruct((Tp, HEADS * QHEAD), F32),
                   jax.ShapeDtypeStruct((Tp, HEADS * QK_NOPE), BF16),
                   jax.ShapeDtypeStruct((Tp, HEADS * LANES), F32),
                   jax.ShapeDtypeStruct((Tp, HEADS * V_HEAD), BF16)],
        scratch_shapes=[pltpu.VMEM((t, QHEAD), F32), pltpu.VMEM((t, V_HEAD), F32)],
        compiler_params=_params(("parallel", "arbitrary", "arbitrary"), est),
    )(q, kv, kr8, kv, do, lse, delta)


GW = SSD_INNER // SSD_GROUPS
PAIRS_PER_GROUP = GW // LANES
XB = SSD_INNER // GW
NT_DIMS = (((1,), (1,)), ((), ()))
TN_DIMS = (((0,), (0,)), ((), ()))


def _ssd_common(xs_ref, dt_ref, alog_ref, e_ref):
    a_neg = -jnp.exp(alog_ref[...])
    dt = dt_ref[...]
    li = lax.broadcasted_iota(jnp.int32, (CHUNK, CHUNK), 0)
    si = lax.broadcasted_iota(jnp.int32, (CHUNK, CHUNK), 1)
    tril = li >= si
    tri = tril.astype(F32)
    acs = jnp.dot(tri, dt * a_neg, precision=HIGHEST, preferred_element_type=F32)
    e = e_ref[...]
    dte = jnp.dot(dt, e, precision=HIGHEST, preferred_element_type=F32)
    acse = jnp.dot(acs, e, precision=HIGHEST, preferred_element_type=F32)
    x = xs_ref[...] * dte
    alast = acse[CHUNK - 1:CHUNK, :]
    return dict(a_neg=a_neg, dt=dt, tril=tril, tri=tri, acs=acs, acs_t=acs.T, e=e, dte=dte, acse=acse, x=x,
                p_e=jnp.exp(acse), w_e=jnp.exp(alast - acse), dl_e=jnp.exp(alast), li=li, si=si)


def _decay(cm, head):
    col = cm["acs"][:, head:head + 1]
    row = cm["acs_t"][head:head + 1, :]
    return jnp.exp(jnp.where(cm["tril"], col - row, -jnp.inf))


def _ssd_fwd(xbc, dt, alog, e, name):
    Tp = xbc.shape[0]
    nc = Tp // CHUNK

    def body(xs_ref, b_ref, c_ref, dt_ref, alog_ref, e_ref, y_ref, st_ref, st_sc):
        @pl.when(pl.program_id(0) == 0)
        def _():
            st_sc[...] = jnp.zeros_like(st_sc)

        cm = _ssd_common(xs_ref, dt_ref, alog_ref, e_ref)
        st_ref[0] = st_sc[...]
        lane = lax.broadcasted_iota(jnp.int32, (CHUNK, LANES), 1)
        for g in range(SSD_GROUPS):
            gs = slice(g * GW, (g + 1) * GW)
            cg = c_ref[:, g * SSD_STATE:(g + 1) * SSD_STATE].astype(BF16)
            bg = b_ref[:, g * SSD_STATE:(g + 1) * SSD_STATE].astype(BF16)
            cb = lax.dot_general(cg, bg, NT_DIMS, preferred_element_type=F32)
            stg = st_sc[:, gs]
            yoff = jnp.dot(cg, stg.astype(BF16), preferred_element_type=F32) * cm["p_e"][:, gs]
            xg = cm["x"][:, gs]
            for jp in range(PAIRS_PER_GROUP):
                j = g * PAIRS_PER_GROUP + jp
                xp = xg[:, jp * LANES:(jp + 1) * LANES].astype(BF16)
                ys = []
                for head in (2 * j, 2 * j + 1):
                    m = (cb * _decay(cm, head)).astype(BF16)
                    ys.append(jnp.dot(m, xp, preferred_element_type=F32))
                y_ref[:, j * LANES:(j + 1) * LANES] = (jnp.where(lane < SSD_HEAD_DIM, ys[0], ys[1])
                                                       + yoff[:, jp * LANES:(jp + 1) * LANES])
            snew = lax.dot_general(bg, (cm["w_e"][:, gs] * xg).astype(BF16), TN_DIMS, preferred_element_type=F32)
            st_sc[:, gs] = cm["dl_e"][:, gs] * stg + snew

    return pl.pallas_call(
        body, name=name, grid=(nc,),
        in_specs=[pl.BlockSpec((CHUNK, SSD_INNER), lambda c: (c, 0)),
                  pl.BlockSpec((CHUNK, GW), lambda c: (c, XB)),
                  pl.BlockSpec((CHUNK, GW), lambda c: (c, XB + 1)),
                  pl.BlockSpec((CHUNK, LANES), lambda c: (c, 0)),
                  pl.BlockSpec((1, LANES), lambda c: (0, 0)),
                  pl.BlockSpec((LANES, SSD_INNER), lambda c: (0, 0))],
        out_specs=[pl.BlockSpec((CHUNK, SSD_INNER), lambda c: (c, 0)),
                   pl.BlockSpec((1, SSD_STATE, SSD_INNER), lambda c: (c, 0, 0))],
        out_shape=[jax.ShapeDtypeStruct((Tp, SSD_INNER), F32), jax.ShapeDtypeStruct((nc, SSD_STATE, SSD_INNER), F32)],
        scratch_shapes=[pltpu.VMEM((SSD_STATE, SSD_INNER), F32)],
        compiler_params=_params(("arbitrary",), 24 * CHUNK * SSD_INNER * 4),
    )(xbc, xbc, xbc, dt, alog, e)


def _ssd_bwd(xbc, dt, alog, e, dy, dxs_skip, states, name):
    Tp = xbc.shape[0]
    nc = Tp // CHUNK
    rev = lambda c: nc - 1 - c

    def body(xs_ref, b_ref, c_ref, dt_ref, alog_ref, e_ref, dy_ref, skip_ref, st_ref,
             dxbc_ref, ddt_ref, dalog_ref, dst_sc, dx_sc, t_sc, tw_sc):
        @pl.when(pl.program_id(0) == 0)
        def _():
            dst_sc[...] = jnp.zeros_like(dst_sc)
            dalog_ref[...] = jnp.zeros_like(dalog_ref)

        cm = _ssd_common(xs_ref, dt_ref, alog_ref, e_ref)
        lane = lax.broadcasted_iota(jnp.int32, (CHUNK, LANES), 1)
        dacs_col = jnp.zeros((CHUNK, LANES), F32)
        dacs_row = jnp.zeros((LANES, CHUNK), F32)
        t_last = []
        for g in range(SSD_GROUPS):
            gs = slice(g * GW, (g + 1) * GW)
            cg = c_ref[:, g * SSD_STATE:(g + 1) * SSD_STATE].astype(BF16)
            bg = b_ref[:, g * SSD_STATE:(g + 1) * SSD_STATE].astype(BF16)
            stg = st_ref[0, :, gs]
            stg_b = stg.astype(BF16)
            dstg = dst_sc[:, gs]
            dstg_b = dstg.astype(BF16)
            xg = cm["x"][:, gs]
            dyg = dy_ref[:, gs]
            zg = jnp.dot(cg, stg_b, preferred_element_type=F32)
            dzg = dyg * cm["p_e"][:, gs]
            dzg_b = dzg.astype(BF16)
            dcg = lax.dot_general(dzg_b, stg_b, NT_DIMS, preferred_element_type=F32)
            dst_in = lax.dot_general(cg, dzg_b, TN_DIMS, preferred_element_type=F32)
            dst_in = dst_in + cm["dl_e"][:, gs] * dstg
            t_last.append(jnp.sum(dstg * stg * cm["dl_e"][:, gs], axis=0, keepdims=True))
            weg = cm["w_e"][:, gs]
            dbg = lax.dot_general((weg * xg).astype(BF16), dstg_b, NT_DIMS, preferred_element_type=F32)
            gg = jnp.dot(bg, dstg_b, preferred_element_type=F32)
            dxg = weg * gg
            tw_sc[:, gs] = xg * dxg
            t_sc[:, gs] = dzg * zg - xg * dxg
            cb = lax.dot_general(cg, bg, NT_DIMS, preferred_element_type=F32)
            dcb = jnp.zeros((CHUNK, CHUNK), F32)
            for jp in range(PAIRS_PER_GROUP):
                j = g * PAIRS_PER_GROUP + jp
                ps = slice(jp * LANES, (jp + 1) * LANES)
                xp = xg[:, ps].astype(BF16)
                dyp = dyg[:, ps]
                dxp = dxg[:, ps]
                for half, head in enumerate((2 * j, 2 * j + 1)):
                    lam = _decay(cm, head)
                    m32 = cb * lam
                    sel = (lane < SSD_HEAD_DIM) if half == 0 else (lane >= SSD_HEAD_DIM)
                    dye = jnp.where(sel, dyp, 0.0).astype(BF16)
                    dm = lax.dot_general(dye, xp, NT_DIMS, preferred_element_type=F32)
                    w = dm * m32
                    dacs_col = dacs_col + jnp.where(cm["si"] == head, jnp.sum(w, axis=1, keepdims=True), 0.0)
                    dacs_row = dacs_row + jnp.where(cm["li"] == head, jnp.sum(w, axis=0, keepdims=True), 0.0)
                    dcb = dcb + dm * lam
                    dxp = dxp + lax.dot_general(m32.astype(BF16), dye, TN_DIMS, preferred_element_type=F32)
                dx_sc[:, j * LANES:(j + 1) * LANES] = dxp
            dcb_b = dcb.astype(BF16)
            dcg = dcg + jnp.dot(dcb_b, bg, preferred_element_type=F32)
            dbg = dbg + lax.dot_general(dcb_b, cg, TN_DIMS, preferred_element_type=F32)
            dst_sc[:, gs] = dst_in
            dxbc_ref[:, SSD_INNER + g * SSD_STATE:SSD_INNER + (g + 1) * SSD_STATE] = dbg
            dxbc_ref[:, SSD_INNER + GW + g * SSD_STATE:SSD_INNER + GW + (g + 1) * SSD_STATE] = dcg
        e = cm["e"]
        dacs = lax.dot_general(t_sc[...], e, NT_DIMS, precision=HIGHEST, preferred_element_type=F32)
        dacs = dacs + dacs_col - dacs_row.T
        last_lane = jnp.concatenate(t_last, axis=1) + jnp.sum(tw_sc[...], axis=0, keepdims=True)
        last_head = lax.dot_general(jnp.broadcast_to(last_lane, (SUBLANES, SSD_INNER)), e, NT_DIMS,
                                    precision=HIGHEST, preferred_element_type=F32)[0:1, :]
        dacs = dacs + jnp.where(cm["li"] == CHUNK - 1, last_head, 0.0)
        da = lax.dot_general(cm["tri"], dacs, TN_DIMS, precision=HIGHEST, preferred_element_type=F32)
        dx_all = dx_sc[...]
        ddt = da * cm["a_neg"] + lax.dot_general(dx_all * xs_ref[...], e, NT_DIMS, precision=HIGHEST,
                                                 preferred_element_type=F32)
        ddt_ref[...] = ddt
        dxbc_ref[:, :SSD_INNER] = dx_all * cm["dte"] + skip_ref[...]
        dalog_ref[0:1, :] += jnp.sum(da * cm["dt"], axis=0, keepdims=True) * cm["a_neg"]

    return pl.pallas_call(
        body, name=name, grid=(nc,),
        in_specs=[pl.BlockSpec((CHUNK, SSD_INNER), lambda c: (rev(c), 0)),
                  pl.BlockSpec((CHUNK, GW), lambda c: (rev(c), XB)),
                  pl.BlockSpec((CHUNK, GW), lambda c: (rev(c), XB + 1)),
                  pl.BlockSpec((CHUNK, LANES), lambda c: (rev(c), 0)),
                  pl.BlockSpec((1, LANES), lambda c: (0, 0)),
                  pl.BlockSpec((LANES, SSD_INNER), lambda c: (0, 0)),
                  pl.BlockSpec((CHUNK, SSD_INNER), lambda c: (rev(c), 0)),
                  pl.BlockSpec((CHUNK, SSD_INNER), lambda c: (rev(c), 0)),
                  pl.BlockSpec((1, SSD_STATE, SSD_INNER), lambda c: (rev(c), 0, 0))],
        out_specs=[pl.BlockSpec((CHUNK, SSD_CONV_DIM), lambda c: (rev(c), 0)),
                   pl.BlockSpec((CHUNK, LANES), lambda c: (rev(c), 0)),
                   pl.BlockSpec((SUBLANES, LANES), lambda c: (0, 0))],
        out_shape=[jax.ShapeDtypeStruct((Tp, SSD_CONV_DIM), F32), jax.ShapeDtypeStruct((Tp, LANES), F32),
                   jax.ShapeDtypeStruct((SUBLANES, LANES), F32)],
        scratch_shapes=[pltpu.VMEM((SSD_STATE, SSD_INNER), F32), pltpu.VMEM((CHUNK, SSD_INNER), F32),
                        pltpu.VMEM((CHUNK, SSD_INNER), F32), pltpu.VMEM((CHUNK, SSD_INNER), F32)],
        compiler_params=_params(("arbitrary",), 32 * CHUNK * SSD_INNER * 4),
    )(xbc, xbc, xbc, dt, alog, e, dy, dxs_skip, states)


def _loss_head(h, target, name):
    Tp, d = h.shape
    nt = Tp // LANES

    def body(h_ref, t_ref, dh_ref, l_ref):
        real = pl.program_id(0) > 0
        err = jnp.where(real, h_ref[...] - t_ref[...], 0.0)
        dh_ref[...] = err * (1.0 / d)
        l_ref[...] = jnp.broadcast_to(0.5 * jnp.sum(err * err) * (1.0 / d), l_ref.shape)

    return pl.pallas_call(
        body, name=name, grid=(nt,),
        in_specs=[pl.BlockSpec((LANES, d), lambda i: (i, 0)),
                  pl.BlockSpec((LANES, d), lambda i: (jnp.maximum(i - 1, 0), 0))],
        out_specs=[pl.BlockSpec((LANES, d), lambda i: (i, 0)),
                   pl.BlockSpec((1, SUBLANES, LANES), lambda i: (i, 0, 0))],
        out_shape=[jax.ShapeDtypeStruct((Tp, d), F32), jax.ShapeDtypeStruct((nt, SUBLANES, LANES), F32)],
        compiler_params=_params(("parallel",), 8 * LANES * d * 4),
    )(h, target)


def _adamw(parts, w, m, v, name):
    R, C = w.shape
    tr = _pick(R, ADAM_ROWS, SUBLANES) if R > ADAM_ROWS else R
    c1 = 1.0 / (1.0 - ADAM_B1 ** ADAM_STEP)
    c2 = 1.0 / (1.0 - ADAM_B2 ** ADAM_STEP)

    def body(p_ref, w_ref, m_ref, v_ref, g_out, d_out, m_out, v_out):
        g = p_ref[0].astype(F32)
        for p in range(1, N_DEV):
            g = g + p_ref[p].astype(F32)
        m_new = ADAM_B1 * m_ref[...] + (1.0 - ADAM_B1) * g
        v_new = ADAM_B2 * v_ref[...] + (1.0 - ADAM_B2) * (g * g)
        g_out[...] = g
        m_out[...] = m_new
        v_out[...] = v_new
        d_out[...] = -ADAM_LR * ((m_new * c1) / (jnp.sqrt(v_new * c2) + ADAM_EPS) + ADAM_WD * w_ref[...])

    spec = pl.BlockSpec((tr, C), lambda i: (i, 0))
    est = N_DEV * tr * C * parts.dtype.itemsize + 7 * tr * C * 4
    return pl.pallas_call(
        body, name=name, grid=(R // tr,),
        in_specs=[pl.BlockSpec((N_DEV, tr, C), lambda i: (0, i, 0)), spec, spec, spec],
        out_specs=[spec] * 4, out_shape=[jax.ShapeDtypeStruct((R, C), F32)] * 4,
        compiler_params=_params(("parallel",), est),
    )(parts, w, m, v)


MESH_ID = pl.DeviceIdType.MESH
N_PEERS = N_DEV - 1


def _dev_index(p):
    return 4 * p[0] + 2 * p[1] + p[2]


def _allgather(wb, ws, name):
    def body(wb_ref, ws_ref, ob_ref, os_ref, send_sems, recv_sems, local_sems):
        x, y, c = lax.axis_index("x"), lax.axis_index("y"), lax.axis_index("c")
        me, sibling = (x, y, c), (x, y, 1 - c)
        chips = [(1 - x, y), (x, 1 - y), (1 - x, 1 - y)]
        for t, (src_ref, out_ref) in enumerate(((wb_ref, ob_ref), (ws_ref, os_ref))):
            def copy(k, block, to, src=None, t=t, out_ref=out_ref):
                slot = out_ref.at[_dev_index(block)]
                return pltpu.make_async_remote_copy(
                    src_ref=slot if src is None else src, dst_ref=slot,
                    send_sem=send_sems.at[t, k], recv_sem=recv_sems.at[t, k],
                    device_id=to, device_id_type=MESH_ID)

            mine = pltpu.make_async_copy(src_ref, out_ref.at[_dev_index(me)], local_sems.at[t])
            mine.start()
            first = [copy(0, me, sibling, src=src_ref)]
            first += [copy(1 + j, me, (*chip, c), src=src_ref) for j, chip in enumerate(chips)]
            for cp in first:
                cp.start()
            passed = [copy(4 + j, (*chip, c), sibling) for j, chip in enumerate(chips)]
            for j, chip in enumerate(chips):
                copy(1 + j, (*chip, c), me).wait_recv()
                passed[j].start()
            copy(0, sibling, me).wait_recv()
            for j, chip in enumerate(chips):
                copy(4 + j, (*chip, 1 - c), me).wait_recv()
            for cp in first + passed:
                cp.wait_send()
            mine.wait()

    any_spec = pl.BlockSpec(memory_space=pl.ANY)
    return pl.pallas_call(
        body, name=name, in_specs=[any_spec, any_spec], out_specs=[any_spec, any_spec],
        out_shape=[jax.ShapeDtypeStruct((N_DEV,) + wb.shape, wb.dtype),
                   jax.ShapeDtypeStruct((N_DEV,) + ws.shape, ws.dtype)],
        scratch_shapes=[pltpu.SemaphoreType.DMA((2, N_PEERS)), pltpu.SemaphoreType.DMA((2, N_PEERS)),
                        pltpu.SemaphoreType.DMA((2,))],
    )(wb, ws)


def _exchange_pieces(gb, gs, name):
    def body(gb_ref, gs_ref, ob_ref, os_ref, send_sems, recv_sems, local_sems):
        x, y, c = lax.axis_index("x"), lax.axis_index("y"), lax.axis_index("c")
        me = (x, y, c)
        peers = [(x, y, 1 - c), (1 - x, y, c), (x, 1 - y, c), (1 - x, 1 - y, c),
                 (1 - x, y, 1 - c), (x, 1 - y, 1 - c), (1 - x, 1 - y, 1 - c)]
        for t, (in_ref, out_ref) in enumerate(((gb_ref, ob_ref), (gs_ref, os_ref))):
            def copy(k, src_block, dst_block, to, t=t, in_ref=in_ref, out_ref=out_ref):
                return pltpu.make_async_remote_copy(
                    src_ref=in_ref.at[_dev_index(src_block)], dst_ref=out_ref.at[_dev_index(dst_block)],
                    send_sem=send_sems.at[t, k], recv_sem=recv_sems.at[t, k],
                    device_id=to, device_id_type=MESH_ID)

            mine = pltpu.make_async_copy(in_ref.at[_dev_index(me)], out_ref.at[_dev_index(me)], local_sems.at[t])
            mine.start()
            sends = [copy(k, p, me, p) for k, p in enumerate(peers)]
            for cp in sends:
                cp.start()
            for k, p in enumerate(peers):
                copy(k, p, p, me).wait_recv()
            for cp in sends:
                cp.wait_send()
            mine.wait()

    any_spec = pl.BlockSpec(memory_space=pl.ANY)
    return pl.pallas_call(
        body, name=name, in_specs=[any_spec, any_spec], out_specs=[any_spec, any_spec],
        out_shape=[jax.ShapeDtypeStruct(gb.shape, gb.dtype), jax.ShapeDtypeStruct(gs.shape, gs.dtype)],
        scratch_shapes=[pltpu.SemaphoreType.DMA((2, N_PEERS)), pltpu.SemaphoreType.DMA((2, N_PEERS)),
                        pltpu.SemaphoreType.DMA((2,))],
    )(gb, gs)


WEIGHTS = ['meta_tokens', 'emb_ln_g', 'emb_ln_b', 'w_in', 'q_norm_g', 'w_q_b', 'kv_norm_g', 'w_kv_b', 'w_o_attn',
           'ssd_conv_w', 'ssd_conv_b', 'dt_bias', 'a_log', 'd_skip', 'ssd_norm_g', 'w_o_ssd', 'w_out', 'ln1_g',
           'ln1_b', 'w_up', 'ffn_conv_w', 'ffn_conv_b', 'w_down', 'ln2_g', 'ln2_b']
BIG = {'w_in': 2, 'w_q_b': 2, 'w_kv_b': 2, 'w_o_attn': 1, 'w_o_ssd': 1, 'w_out': 1, 'w_up': 2, 'w_down': 1}
SMALL_SHARDED = {'meta_tokens': 1, 'ssd_conv_w': 2, 'ffn_conv_w': 2}
REPLICATED = [n for n in WEIGHTS if n not in BIG and n not in SMALL_SHARDED]
BIG_COLS = 1024
SMALL_COLS = LANES


def _flatten(arrs, cols, row_mult, lead=False):
    parts, offs, off = [], [], 0
    for a in arrs:
        a2 = a.reshape(N_DEV, -1) if lead else a.reshape(1, -1)
        n = a2.shape[1]
        pad = -n % cols
        parts.append(jnp.pad(a2, ((0, 0), (0, pad))))
        offs.append((off, n))
        off += n + pad
    rows = off // cols
    extra = (-rows % row_mult) * cols
    if extra:
        parts.append(jnp.zeros((parts[0].shape[0], extra), parts[0].dtype))
    flat = jnp.concatenate(parts, axis=1)
    flat = flat.reshape(flat.shape[0], -1, cols)
    return (flat if lead else flat[0]), offs


def _unflatten(flat, offs, shapes):
    f = flat.reshape(-1)
    return [f[o:o + n].reshape(s) for (o, n), s in zip(offs, shapes)]


def _to_pieces(g, axis):
    s = g.shape[axis] // N_DEV
    g = g.reshape(g.shape[:axis] + (N_DEV, s) + g.shape[axis + 1:])
    return jnp.moveaxis(g, axis, 0).reshape(N_DEV, -1)


def _from_pieces(p, shard_shape, axis):
    g = jnp.moveaxis(p.reshape((N_DEV,) + tuple(shard_shape)), 0, axis)
    sh = list(shard_shape)
    sh[axis] *= N_DEV
    return g.reshape(sh)


def _in_proj_pad(w):
    e = np.cumsum((0,) + IN_SIZES)
    ql, kvl, kpe, z, xbc, dt, ga, gs = [w[:, e[j]:e[j + 1]] for j in range(8)]
    zc = lambda n: jnp.zeros((w.shape[0], n), w.dtype)
    return jnp.concatenate([ql, kvl, z, xbc, ga, gs, kpe, zc(LANES - QK_ROPE), dt, zc(LANES - SSD_HEADS)], axis=1)


def _in_proj_unpad(d):
    seg = lambda o, n: d[:, o:o + n]
    return jnp.concatenate([seg(OQ, Q_LORA), seg(OKV, KV_LORA), seg(OKPE, QK_ROPE), seg(OZ, SSD_INNER),
                            seg(OXBC, SSD_CONV_DIM), seg(ODT, SSD_HEADS), seg(OGA, D_MODEL), seg(OGS, D_MODEL)], axis=1)


def _q_pad(w):
    w3 = w.reshape(Q_LORA, HEADS, QK_NOPE + QK_ROPE)
    return jnp.concatenate([w3, jnp.zeros((Q_LORA, HEADS, QHEAD - QK_NOPE - QK_ROPE), w.dtype)], axis=2).reshape(Q_LORA, HEADS * QHEAD)


def _q_unpad(d):
    return d.reshape(Q_LORA, HEADS, QHEAD)[:, :, :QK_NOPE + QK_ROPE].reshape(Q_LORA, HEADS * (QK_NOPE + QK_ROPE))


def _kv_perm(w):
    w3 = w.reshape(KV_LORA, HEADS, QK_NOPE + V_HEAD)
    return jnp.concatenate([w3[:, :, :QK_NOPE].reshape(KV_LORA, -1), w3[:, :, QK_NOPE:].reshape(KV_LORA, -1)], axis=1)


def _kv_unperm(d):
    kn = d[:, :HEADS * QK_NOPE].reshape(KV_LORA, HEADS, QK_NOPE)
    v = d[:, HEADS * QK_NOPE:].reshape(KV_LORA, HEADS, V_HEAD)
    return jnp.concatenate([kn, v], axis=2).reshape(KV_LORA, HEADS * (QK_NOPE + V_HEAD))


def _row_vec(v, width=None):
    v = v.reshape(1, -1).astype(F32)
    if width is not None and v.shape[1] < width:
        v = jnp.pad(v, ((0, 0), (0, width - v.shape[1])))
    return v


def _pad_rows8(w):
    return jnp.pad(w.astype(F32), ((0, SUBLANES - w.shape[0]), (0, 0)))


def _tables(Tp, npad):
    pos = jnp.maximum(jnp.arange(Tp, dtype=jnp.int32) - npad, 0).astype(F32)
    inv_freq = 1.0 / (ROPE_THETA ** (jnp.arange(0, QK_ROPE, 2, dtype=F32) / QK_ROPE))
    ang = pos[:, None] * inv_freq[None, :]
    ang = jnp.concatenate([ang, ang], axis=-1)
    zeros = jnp.zeros((Tp, LANES - QK_ROPE), F32)
    cos = jnp.concatenate([jnp.cos(ang), zeros], axis=1)
    sin = jnp.concatenate([jnp.sin(ang), zeros], axis=1)
    rot = np.zeros((LANES, LANES), np.float32)
    half = QK_ROPE // 2
    for i in range(half):
        rot[i + half, i] = -1.0
        rot[i, i + half] = 1.0
    expand = np.zeros((LANES, SSD_INNER), np.float32)
    for hd in range(SSD_HEADS):
        expand[hd, hd * SSD_HEAD_DIM:(hd + 1) * SSD_HEAD_DIM] = 1.0
    return cos, sin, jnp.asarray(rot), jnp.asarray(expand)


def _layer_rows(proj, tb):
    rows_a = [_row(proj, Q_LORA, OQ // Q_LORA), _row(proj, KV_LORA, OKV // KV_LORA), _row(proj, LANES, OKPE // LANES),
              _row(proj, LANES, ODT // LANES), _row(tb["cos"], diff=False), _row(tb["sin"], diff=False)]
    return rows_a


def _layer_fwd(h, P, tb, fns, npad):
    proj = _mm(h, P["w_in"], F32, "in_proj")
    rows_a = _layer_rows(proj, tb)
    consts_a = [_row(tb["rot"], diff=False), _row(P["q_norm_g"]), _row(P["kv_norm_g"]), _row(P["dt_bias"])]
    qn, kvn, kr8, dt = _rw_fwd(fns["in_post"], rows_a, consts_a,
                               [_out(Q_LORA, BF16), _out(KV_LORA, BF16), _out(HEADS * LANES, BF16), _out(LANES, F32)],
                               "in_post")
    q = _mm(qn, P["w_q"], F32, "q_proj")
    rows_q = [_row(q, QHEAD, 0, grp=True), _row(tb["cos"], diff=False), _row(tb["sin"], diff=False)]
    qr = _rw_fwd(fns["q_post"], rows_q, [_row(tb["rot"], diff=False)], [_out(HEADS * QHEAD, BF16, QHEAD, grp=True)],
                 "q_post", ng=HEADS)[0]
    kv = _mm(kvn, P["w_kv"], BF16, "kv_proj")
    o, lse = _flash_fwd(qr, kv, kr8, npad, "attn_fwd")
    ya = _mm(o, P["w_o_attn"], F32, "attn_out")
    xbc = _conv_fwd(proj, OXBC, SSD_CONV_DIM, P["ssd_conv_w"], P["ssd_conv_b"], SSD_CONV, True, npad, "ssd_conv")
    y, states = _ssd_fwd(xbc, dt, P["a_log"], tb["expand"], "ssd_fwd")
    rows_b = [_row(y, GW, 0, grp=True), _row(xbc, GW, 0, grp=True), _row(proj, GW, OZ // GW, grp=True)]
    consts_b = [_row(P["d_skip"], GW, 0, grp=True), _row(P["ssd_norm_g"], GW, 0, grp=True)]
    yn = _rw_fwd(fns["gated"], rows_b, consts_b, [_out(SSD_INNER, BF16, GW, grp=True)], "ssd_gate", ng=SSD_GROUPS)[0]
    ys = _mm(yn, P["w_o_ssd"], F32, "ssd_out")
    rows_c = [_row(proj, D_MODEL, OGA // D_MODEL), _row(proj, D_MODEL, OGS // D_MODEL), _row(ya), _row(ys)]
    mixed = _rw_fwd(fns["mix"], rows_c, [], [_out(D_MODEL, BF16)], "mix")[0]
    mo = _mm(mixed, P["w_out"], F32, "mix_out")
    consts_1 = [_row(P["ln1_g"]), _row(P["ln1_b"])]
    h1 = _rw_fwd(fns["res_ln"], [_row(h), _row(mo)], consts_1, [_out(D_MODEL, F32)], "ln1")[0]
    up = _mm(h1, P["w_up"], F32, "ffn_up")
    u = _conv_fwd(up, 0, 2 * D_FF, P["ffn_conv_w"], P["ffn_conv_b"], FFN_CONV, False, npad, "ffn_conv")
    act = _rw_fwd(fns["glu"], [_row(u)], [], [_out(D_FF, BF16)], "ffn_glu")[0]
    fo = _mm(act, P["w_down"], F32, "ffn_down")
    consts_2 = [_row(P["ln2_g"]), _row(P["ln2_b"])]
    h2 = _rw_fwd(fns["res_ln"], [_row(h1), _row(fo)], consts_2, [_out(D_MODEL, F32)], "ln2")[0]
    res = dict(h=h, proj=proj, qn=qn, kvn=kvn, kr8=kr8, dt=dt, q=q, qr=qr, kv=kv, o=o, lse=lse, ya=ya, xbc=xbc, y=y,
               states=states, yn=yn, ys=ys, mixed=mixed, mo=mo, h1=h1, up=up, u=u, act=act, fo=fo)
    return h2, res


def _layer_bwd(dh2, r, P, tb, fns, npad):
    g = {}
    consts_2 = [_row(P["ln2_g"]), _row(P["ln2_b"])]
    (dh1_a, dfo), (g["ln2_g"], g["ln2_b"]) = _rw_bwd(fns["res_ln"], [_row(r["h1"]), _row(r["fo"])], consts_2,
                                                     [_row(dh2)], [F32, BF16], "ln2_bwd")
    g["w_down"] = _mm(r["act"], dfo, F32, "dw_down", ta=True)
    dact = _mm(dfo, P["w_down_t"], F32, "d_act")
    (du,), _ = _rw_bwd(fns["glu"], [_row(r["u"])], [], [_row(dact)], [F32], "glu_bwd")
    dup, g["ffn_conv_w"], g["ffn_conv_b"] = _conv_bwd(r["up"], 0, 2 * D_FF, P["ffn_conv_w"], P["ffn_conv_b"], du,
                                                      FFN_CONV, False, npad, "ffn_conv_bwd")
    g["w_up"] = _mm(r["h1"], dup, F32, "dw_up", ta=True)
    dh1 = _mm(dup, P["w_up_t"], F32, "d_h1", add=dh1_a)
    consts_1 = [_row(P["ln1_g"]), _row(P["ln1_b"])]
    (dh_a, dmo), (g["ln1_g"], g["ln1_b"]) = _rw_bwd(fns["res_ln"], [_row(r["h"]), _row(r["mo"])], consts_1,
                                                    [_row(dh1)], [F32, BF16], "ln1_bwd")
    g["w_out"] = _mm(r["mixed"], dmo, F32, "dw_out", ta=True)
    dmixed = _mm(dmo, P["w_out_t"], F32, "d_mixed")
    proj = r["proj"]
    rows_c = [_row(proj, D_MODEL, OGA // D_MODEL), _row(proj, D_MODEL, OGS // D_MODEL), _row(r["ya"]), _row(r["ys"])]
    (dga, dgs, dya, dys), _ = _rw_bwd(fns["mix"], rows_c, [], [_row(dmixed)], [BF16] * 4, "mix_bwd")
    g["w_o_attn"] = _mm(r["o"], dya, F32, "dw_o_attn", ta=True)
    do = _mm(dya, P["w_o_attn_t"], F32, "d_o")
    g["w_o_ssd"] = _mm(r["yn"], dys, F32, "dw_o_ssd", ta=True)
    dyn = _mm(dys, P["w_o_ssd_t"], F32, "d_yn")
    rows_b = [_row(r["y"], GW, 0, grp=True), _row(r["xbc"], GW, 0, grp=True), _row(proj, GW, OZ // GW, grp=True)]
    consts_b = [_row(P["d_skip"], GW, 0, grp=True), _row(P["ssd_norm_g"], GW, 0, grp=True)]
    (dy, dxs_skip, dz), (g["d_skip"], g["ssd_norm_g"]) = _rw_bwd(
        fns["gated"], rows_b, consts_b, [_row(dyn, GW, 0, grp=True)], [F32, F32, BF16], "ssd_gate_bwd", ng=SSD_GROUPS)
    dxbc, ddt, g["a_log"] = _ssd_bwd(r["xbc"], r["dt"], P["a_log"], tb["expand"], dy, dxs_skip, r["states"], "ssd_bwd")
    dxbc_pre, g["ssd_conv_w"], g["ssd_conv_b"] = _conv_bwd(proj, OXBC, SSD_CONV_DIM, P["ssd_conv_w"], P["ssd_conv_b"],
                                                           dxbc, SSD_CONV, True, npad, "ssd_conv_bwd")
    delta = _rw_fwd(fns["delta"], [_row(do, V_HEAD, 0, grp=True), _row(r["o"], V_HEAD, 0, grp=True)], [],
                    [_out(HEADS * LANES, F32, LANES, grp=True)], "attn_delta", ng=HEADS)[0]
    dqr, dkn, dkr8, dv = _flash_bwd(r["qr"], r["kv"], r["kr8"], do, r["lse"], delta, npad, "attn_bwd")
    rows_q = [_row(r["q"], QHEAD, 0, grp=True), _row(tb["cos"], diff=False), _row(tb["sin"], diff=False)]
    (dq,), _ = _rw_bwd(fns["q_post"], rows_q, [_row(tb["rot"], diff=False)], [_row(dqr, QHEAD, 0, grp=True)], [BF16],
                       "q_post_bwd", ng=HEADS)
    g["w_q"] = _mm(r["qn"], dq, F32, "dw_q", ta=True)
    dqn = _mm(dq, P["w_q_t"], F32, "d_qn")
    dkv = jnp.concatenate([dkn, dv], axis=1)
    g["w_kv"] = _mm(r["kvn"], dkv, F32, "dw_kv", ta=True)
    dkvn = _mm(dkv, P["w_kv_t"], F32, "d_kvn")
    rows_a = _layer_rows(proj, tb)
    consts_a = [_row(tb["rot"], diff=False), _row(P["q_norm_g"]), _row(P["kv_norm_g"]), _row(P["dt_bias"])]
    (dql, dkvl, dkpe, ddtr), (g["q_norm_g"], g["kv_norm_g"], g["dt_bias"]) = _rw_bwd(
        fns["in_post"], rows_a, consts_a, [_row(dqn), _row(dkvn), _row(dkr8), _row(ddt)], [BF16] * 4, "in_post_bwd")
    dproj = jnp.concatenate([dql, dkvl, dz, dxbc_pre, dga, dgs, dkpe, ddtr], axis=1)
    g["w_in"] = _mm(r["h"], dproj, F32, "dw_in", ta=True)
    dh = _mm(dproj, P["w_in_t"], F32, "d_h", add=dh_a)
    return dh, g


def _layer_params(full, small, i):
    bf = lambda a: a.astype(BF16)
    P = {}
    P["w_in"] = _in_proj_pad(full["w_in"][i])
    P["w_q"] = _q_pad(full["w_q_b"][i])
    P["w_kv"] = _kv_perm(full["w_kv_b"][i])
    for n in ("w_o_attn", "w_o_ssd", "w_out", "w_up", "w_down"):
        P[n] = full[n][i]
    for n in ("w_in", "w_q", "w_kv", "w_o_attn", "w_o_ssd", "w_out", "w_up", "w_down"):
        P[n] = bf(P[n])
        P[n + "_t"] = P[n].T
    P["q_norm_g"] = _row_vec(small["q_norm_g"][i])
    P["kv_norm_g"] = _row_vec(small["kv_norm_g"][i])
    P["dt_bias"] = _row_vec(small["dt_bias"][i], LANES)
    P["a_log"] = _row_vec(small["a_log"][i], LANES)
    P["d_skip"] = _row_vec(jnp.repeat(small["d_skip"][i], SSD_HEAD_DIM))
    P["ssd_norm_g"] = _row_vec(small["ssd_norm_g"][i])
    P["ssd_conv_w"] = _pad_rows8(small["ssd_conv_w"][i])
    P["ssd_conv_b"] = _row_vec(small["ssd_conv_b"][i])
    P["ffn_conv_w"] = _pad_rows8(small["ffn_conv_w"][i])
    P["ffn_conv_b"] = _row_vec(small["ffn_conv_b"][i])
    for n in ("ln1_g", "ln1_b", "ln2_g", "ln2_b"):
        P[n] = _row_vec(small[n][i])
    return P


def _layer_grads_to_reference_layout(g):
    out = {}
    out["w_in"] = _in_proj_unpad(g["w_in"])
    out["w_q_b"] = _q_unpad(g["w_q"])
    out["w_kv_b"] = _kv_unperm(g["w_kv"])
    for n in ("w_o_attn", "w_o_ssd", "w_out", "w_up", "w_down"):
        out[n] = g[n]
    out["q_norm_g"] = g["q_norm_g"][0]
    out["kv_norm_g"] = g["kv_norm_g"][0]
    out["dt_bias"] = g["dt_bias"][0, :SSD_HEADS]
    out["a_log"] = g["a_log"][0, :SSD_HEADS]
    out["d_skip"] = g["d_skip"].reshape(SSD_HEADS, SSD_HEAD_DIM).sum(axis=1)
    out["ssd_norm_g"] = g["ssd_norm_g"][0]
    out["ssd_conv_w"] = g["ssd_conv_w"][:SSD_CONV]
    out["ssd_conv_b"] = g["ssd_conv_b"][0]
    out["ffn_conv_w"] = g["ffn_conv_w"][:FFN_CONV]
    out["ffn_conv_b"] = g["ffn_conv_b"][0]
    for n in ("ln1_g", "ln1_b", "ln2_g", "ln2_b"):
        out[n] = g[n][0]
    return out


def kernel(x, meta_tokens, emb_ln_g, emb_ln_b, w_in, q_norm_g, w_q_b, kv_norm_g, w_kv_b, w_o_attn, ssd_conv_w, ssd_conv_b, dt_bias, a_log, d_skip, ssd_norm_g, w_o_ssd, w_out, ln1_g, ln1_b, w_up, ffn_conv_w, ffn_conv_b, w_down, ln2_g, ln2_b, loss_target, m_meta_tokens, m_emb_ln_g, m_emb_ln_b, m_w_in, m_q_norm_g, m_w_q_b, m_kv_norm_g, m_w_kv_b, m_w_o_attn, m_ssd_conv_w, m_ssd_conv_b, m_dt_bias, m_a_log, m_d_skip, m_ssd_norm_g, m_w_o_ssd, m_w_out, m_ln1_g, m_ln1_b, m_w_up, m_ffn_conv_w, m_ffn_conv_b, m_w_down, m_ln2_g, m_ln2_b, v_meta_tokens, v_emb_ln_g, v_emb_ln_b, v_w_in, v_q_norm_g, v_w_q_b, v_kv_norm_g, v_w_kv_b, v_w_o_attn, v_ssd_conv_w, v_ssd_conv_b, v_dt_bias, v_a_log, v_d_skip, v_ssd_norm_g, v_w_o_ssd, v_w_out, v_ln1_g, v_ln1_b, v_w_up, v_ffn_conv_w, v_ffn_conv_b, v_w_down, v_ln2_g, v_ln2_b):
    given = dict(locals())
    w = {n: given[n] for n in WEIGHTS}
    m = {n: given["m_" + n] for n in WEIGHTS}
    v = {n: given["v_" + n] for n in WEIGHTS}
    seq = x.shape[1]
    assert x.shape[0] == 1 and seq % LANES == 0
    npad = LANES - N_META
    Tp = npad + N_META + seq
    depth = w_in.shape[0]

    big_names, small_names = list(BIG), list(SMALL_SHARDED)
    wb, offs_b = _flatten([w[n].astype(BF16) for n in big_names], BIG_COLS, 2 * SUBLANES)
    ws, offs_s = _flatten([w[n] for n in small_names], SMALL_COLS, SUBLANES)
    gb, gsm = _allgather(wb, ws, "weight_allgather")
    full = {}
    for n, (o, sz) in zip(big_names, offs_b):
        full[n] = _from_pieces(gb.reshape(N_DEV, -1)[:, o:o + sz], w[n].shape, BIG[n])
    small = {n: w[n] for n in REPLICATED}
    for n, (o, sz) in zip(small_names, offs_s):
        small[n] = _from_pieces(gsm.reshape(N_DEV, -1)[:, o:o + sz], w[n].shape, SMALL_SHARDED[n])

    fns = _make_stage_fns(npad)
    cos, sin, rot, expand = _tables(Tp, npad)
    tb = dict(cos=cos, sin=sin, rot=rot, expand=expand)
    top = jnp.pad(small["meta_tokens"], ((npad, 0), (0, 0)))
    hcat = jnp.concatenate([top, x[0]], axis=0)
    consts_e = [_row(_row_vec(w["emb_ln_g"])), _row(_row_vec(w["emb_ln_b"]))]
    h = _rw_fwd(fns["ln"], [_row(hcat)], consts_e, [_out(D_MODEL, F32)], "emb_ln")[0]
    layers = [_layer_params(full, small, i) for i in range(depth)]
    saved = []
    for i in range(depth):
        h, res = _layer_fwd(h, layers[i], tb, fns, npad)
        saved.append(res)
    dh, lparts = _loss_head(h, loss_target[0], "loss_head")
    loss = lax.psum(jnp.sum(lparts[:, 0, 0]), ("x", "y", "c"))

    lg = [None] * depth
    for i in reversed(range(depth)):
        dh, gi = _layer_bwd(dh, saved[i], layers[i], tb, fns, npad)
        lg[i] = _layer_grads_to_reference_layout(gi)
    (dhcat,), (d_emb_g, d_emb_b) = _rw_bwd(fns["ln"], [_row(hcat)], consts_e, [_row(dh)], [F32], "emb_ln_bwd")
    grad_x = dhcat[LANES:][None]
    local = {n: jnp.stack([lg[i][n] for i in range(depth)]) for n in lg[0]}
    local["meta_tokens"] = dhcat[npad:LANES]
    local["emb_ln_g"] = d_emb_g[0]
    local["emb_ln_b"] = d_emb_b[0]

    pb, poffs_b = _flatten([_to_pieces(local[n].astype(BF16), BIG[n]) for n in big_names], BIG_COLS, ADAM_ROWS, lead=True)
    sm_names = small_names + REPLICATED
    sm_pieces = [_to_pieces(local[n], SMALL_SHARDED[n]) for n in small_names]
    sm_pieces += [jnp.broadcast_to(local[n].reshape(1, -1), (N_DEV, local[n].size)) for n in REPLICATED]
    ps, poffs_s = _flatten(sm_pieces, SMALL_COLS, SUBLANES, lead=True)
    rb, rs = _exchange_pieces(pb, ps, "grad_exchange")
    outs = {}
    for names, recv, cols, mult in ((big_names, rb, BIG_COLS, ADAM_ROWS), (sm_names, rs, SMALL_COLS, SUBLANES)):
        wf, offs = _flatten([w[n] for n in names], cols, mult)
        mf, _ = _flatten([m[n] for n in names], cols, mult)
        vf, _ = _flatten([v[n] for n in names], cols, mult)
        res4 = _adamw(recv, wf, mf, vf, "adamw_big" if names is big_names else "adamw_small")
        shapes = [w[n].shape for n in names]
        for kind, flat in zip(("grad", "delta", "new_m", "new_v"), res4):
            for n, a in zip(names, _unflatten(flat, offs, shapes)):
                outs[kind + "_" + n] = a
    result = [loss, grad_x]
    for kind in ("grad", "delta", "new_m", "new_v"):
        result += [outs[kind + "_" + n] for n in WEIGHTS]
    return tuple(result)
```

```python
import functools

import jax
import jax.numpy as jnp
import numpy as np
from jax import lax
from jax.experimental import pallas as pl
from jax.experimental.pallas import tpu as pltpu

F32 = jnp.float32
BF16 = jnp.bfloat16
HIGHEST = lax.Precision.HIGHEST

D_MODEL = 1024
DEPTH = 2
N_META = 16
HEADS = 8
Q_LORA = 768
KV_LORA = 256
QK_NOPE = 128
QK_ROPE = 64
V_HEAD = 128
ROPE_THETA = 10000.0
SSD_INNER = 2048
SSD_HEAD_DIM = 64
SSD_HEADS = 32
SSD_GROUPS = 4
SSD_STATE = 128
SSD_CONV = 4
SSD_CONV_DIM = SSD_INNER + 2 * SSD_GROUPS * SSD_STATE
CHUNK = 128
D_FF = 2816
FFN_CONV = 3
LN_EPS = 1e-5
RMS_EPS = 1e-6
ALPHA = (2 * DEPTH) ** 0.25
IN_SIZES = (Q_LORA, KV_LORA, QK_ROPE, SSD_INNER, SSD_CONV_DIM, SSD_HEADS, D_MODEL, D_MODEL)
ATT_SCALE = (QK_NOPE + QK_ROPE) ** -0.5
NEG_INF = -1e30
ADAM_LR, ADAM_B1, ADAM_B2, ADAM_EPS, ADAM_WD, ADAM_STEP = 0.001, 0.9, 0.999, 1e-08, 0.01, 10

LANES = 128
SUBLANES = 8
VMEM_BYTES = 64 * 1024 * 1024
N_DEV = 8

OQ, OKV, OZ, OXBC, OGA, OGS, OKPE, ODT = 0, 768, 1024, 3072, 6144, 7168, 8192, 8320
IN_PAD = 8448
QHEAD = 256

ROW_TILE = 640
MM_COL_TILE = 1408
MM_K_TILE = 1408
ATT_TILE = 640
ATT_SUB = 128
BF16_ROWS = 16
ROW_BUDGET = 7 * 1024 * 1024
ADAM_ELEMS = 160 * 1024


def _pick(n, target, q=LANES):
    assert n % q == 0, (n, q)
    units = n // q
    best = q
    for d in range(1, units + 1):
        if units % d == 0 and d * q <= target:
            best = d * q
    return best


def _pick_rows(n, row_bytes):
    return _pick(n, max(BF16_ROWS, ROW_BUDGET // row_bytes), BF16_ROWS)


def _params(sem, est_bytes):
    limit = int(min(VMEM_BYTES - (6 << 20), max(32 << 20, 2 * est_bytes + (8 << 20))))
    return pltpu.CompilerParams(dimension_semantics=sem, vmem_limit_bytes=limit)


def _nbytes(shape, dtype):
    return int(np.prod(shape)) * jnp.dtype(dtype).itemsize


def _mm(a, b, out_dtype, name, ta=False, tb=False, add=None):
    assert not (ta and tb)
    if ta:
        K, M = a.shape
        tm = _pick(M, MM_COL_TILE)
        tk = _pick(K, ROW_TILE)
    else:
        M, K = a.shape
        tm = _pick(M, ROW_TILE)
        tk = _pick(K, MM_K_TILE)
    N, K2 = (b.shape if tb else b.shape[::-1])
    assert K == K2
    tn = _pick(N, MM_COL_TILE)
    nk = K // tk
    dn = (((0,), (0,)), ((), ())) if ta else ((((1,), (1,)), ((), ())) if tb else (((1,), (0,)), ((), ())))

    def body(*refs):
        if add is None:
            a_ref, b_ref, o_ref, acc = refs
        else:
            a_ref, b_ref, add_ref, o_ref, acc = refs
        k = pl.program_id(2)

        @pl.when(k == 0)
        def _():
            acc[...] = jnp.zeros_like(acc)

        acc[...] += lax.dot_general(a_ref[...].astype(BF16), b_ref[...].astype(BF16), dn,
                                    preferred_element_type=F32)

        @pl.when(k == nk - 1)
        def _():
            r = acc[...]
            if add is not None:
                r = r + add_ref[...].astype(F32)
            o_ref[...] = r.astype(out_dtype)

    if ta:
        a_spec = pl.BlockSpec((tk, tm), lambda i, j, k: (k, i))
    else:
        a_spec = pl.BlockSpec((tm, tk), lambda i, j, k: (i, k))
    b_spec = pl.BlockSpec((tn, tk), lambda i, j, k: (j, k)) if tb else pl.BlockSpec((tk, tn), lambda i, j, k: (k, j))
    in_specs = [a_spec, b_spec]
    args = [a, b]
    est = 2 * (tm * tk * a.dtype.itemsize + tk * tn * b.dtype.itemsize + tm * tn * 4) + tm * tn * 4
    if add is not None:
        in_specs.append(pl.BlockSpec((tm, tn), lambda i, j, k: (i, j)))
        args.append(add)
        est += 2 * tm * tn * 4
    return pl.pallas_call(
        body, name=name, grid=(M // tm, N // tn, nk), in_specs=in_specs,
        out_specs=pl.BlockSpec((tm, tn), lambda i, j, k: (i, j)),
        out_shape=jax.ShapeDtypeStruct((M, N), out_dtype),
        scratch_shapes=[pltpu.VMEM((tm, tn), F32)],
        compiler_params=_params(("parallel", "parallel", "arbitrary"), est),
    )(*args)


def _row(arr, bw=None, cb=0, grp=False, diff=True):
    return dict(arr=arr, bw=arr.shape[1] if bw is None else bw, cb=cb, grp=grp, diff=diff)


def _out(width, dtype, bw=None, grp=False):
    return dict(width=width, dtype=dtype, bw=width if bw is None else bw, grp=grp)


def _spec_rows(d, tm):
    return pl.BlockSpec((tm, d["bw"]), lambda g, i, cb=d["cb"], gr=d["grp"]: (i, cb + (g if gr else 0)))


def _spec_const(d):
    return pl.BlockSpec((d["arr"].shape[0], d["bw"]), lambda g, i, cb=d["cb"], gr=d["grp"]: (0, cb + (g if gr else 0)))


def _rw_fwd(fn, rows, consts, outs, name, ng=1):
    Tp = rows[0]["arr"].shape[0]
    tm = _pick_rows(Tp, 4 * (sum(d["bw"] for d in rows) + 2 * sum(o["bw"] for o in outs)))
    nr, ncst = len(rows), len(consts)

    def body(*refs):
        i = pl.program_id(1)
        rowidx = i * tm + lax.broadcasted_iota(jnp.int32, (tm, 1), 0)
        rv = [r[...].astype(F32) for r in refs[:nr]]
        cv = [c[...] for c in refs[nr:nr + ncst]]
        vals = fn(rowidx, *rv, *cv)
        for o, v in zip(refs[nr + ncst:], vals):
            o[...] = v.astype(o.dtype)

    est = sum(tm * d["bw"] * 4 for d in rows) + sum(tm * o["bw"] * 4 for o in outs)
    return pl.pallas_call(
        body, name=name, grid=(ng, Tp // tm),
        in_specs=[_spec_rows(d, tm) for d in rows] + [_spec_const(d) for d in consts],
        out_specs=[pl.BlockSpec((tm, o["bw"]), lambda g, i, gr=o["grp"]: (i, g if gr else 0)) for o in outs],
        out_shape=[jax.ShapeDtypeStruct((Tp, o["width"]), o["dtype"]) for o in outs],
        compiler_params=_params(("parallel", "parallel"), 3 * est),
    )(*[d["arr"] for d in rows], *[d["arr"] for d in consts])


def _rw_bwd(fn, rows, consts, cots, drow_dtypes, name, ng=1):
    Tp = rows[0]["arr"].shape[0]
    tm = _pick_rows(Tp, 4 * (3 * sum(d["bw"] for d in rows) + 2 * sum(d["bw"] for d in cots)))
    nr, ncst, nct = len(rows), len(consts), len(cots)
    drows = [k for k, d in enumerate(rows) if d["diff"]]
    dcsts = [k for k, d in enumerate(consts) if d["diff"]]
    for k in drows:
        assert rows[k]["grp"] or ng == 1

    def body(*refs):
        g = pl.program_id(0)
        i = pl.program_id(1)
        rowidx = i * tm + lax.broadcasted_iota(jnp.int32, (tm, 1), 0)
        rv = [r[...].astype(F32) for r in refs[:nr]]
        cv = [c[...] for c in refs[nr:nr + ncst]]
        ct = tuple(r[...].astype(F32) for r in refs[nr + ncst:nr + ncst + nct])
        orefs = refs[nr + ncst + nct:]

        def f(*dargs):
            rr, cc = list(rv), list(cv)
            for k, v in zip(drows, dargs[:len(drows)]):
                rr[k] = v
            for k, v in zip(dcsts, dargs[len(drows):]):
                cc[k] = v
            return tuple(fn(rowidx, *rr, *cc))

        _, vjp = jax.vjp(f, *[rv[k] for k in drows], *[cv[k] for k in dcsts])
        grads = vjp(ct)
        for o, v in zip(orefs[:len(drows)], grads[:len(drows)]):
            o[...] = v.astype(o.dtype)
        for k, o, v in zip(dcsts, orefs[len(drows):], grads[len(drows):]):
            first = (i == 0) if consts[k]["grp"] else ((i == 0) & (g == 0))

            @pl.when(first)
            def _(o=o, v=v):
                o[...] = v

            @pl.when(jnp.logical_not(first))
            def _(o=o, v=v):
                o[...] += v

    out_specs, out_shape = [], []
    for k, dt in zip(drows, drow_dtypes):
        d = rows[k]
        out_specs.append(pl.BlockSpec((tm, d["bw"]), lambda g, i, gr=d["grp"]: (i, g if gr else 0)))
        out_shape.append(jax.ShapeDtypeStruct((Tp, d["bw"] * (ng if d["grp"] else 1)), dt))
    for k in dcsts:
        d = consts[k]
        r = d["arr"].shape[0]
        out_specs.append(pl.BlockSpec((r, d["bw"]), lambda g, i, gr=d["grp"]: (0, g if gr else 0)))
        out_shape.append(jax.ShapeDtypeStruct((r, d["bw"] * (ng if d["grp"] else 1)), F32))
    est = sum(tm * d["bw"] * 4 for d in rows) * 2 + sum(tm * d["bw"] * 4 for d in cots)
    res = pl.pallas_call(
        body, name=name, grid=(ng, Tp // tm),
        in_specs=[_spec_rows(d, tm) for d in rows] + [_spec_const(d) for d in consts] + [_spec_rows(d, tm) for d in cots],
        out_specs=out_specs, out_shape=out_shape,
        compiler_params=_params(("arbitrary", "arbitrary"), 3 * est),
    )(*[d["arr"] for d in rows], *[d["arr"] for d in consts], *[d["arr"] for d in cots])
    return list(res[:len(drows)]), list(res[len(drows):])


def _sigmoid(x):
    return 1.0 / (1.0 + jnp.exp(-x))


def _silu(x):
    return x * _sigmoid(x)


def _softplus(x):
    return jnp.maximum(x, 0.0) + jnp.log(1.0 + jnp.exp(-jnp.abs(x)))


def _layer_norm(x, g, b):
    mu = jnp.mean(x, axis=-1, keepdims=True)
    xc = x - mu
    var = jnp.mean(xc * xc, axis=-1, keepdims=True)
    return xc * lax.rsqrt(var + LN_EPS) * g + b


def _rms_norm(x, g):
    return x * lax.rsqrt(jnp.mean(x * x, axis=-1, keepdims=True) + RMS_EPS) * g


def _rope(r, cos, sin, rot):
    return r * cos + jnp.dot(r, rot, precision=HIGHEST, preferred_element_type=F32) * sin


def _make_stage_fns(npad):
    def fn_ln_masked(rowidx, x, g, b):
        return (jnp.where(rowidx >= npad, _layer_norm(x, g, b), 0.0),)

    def fn_in_post(rowidx, ql, kvl, kpe, dtr, cos, sin, rot, qg, kvg, dtb):
        qn = _rms_norm(ql, qg)
        kvn = _rms_norm(kvl, kvg)
        kr = _rope(kpe, cos, sin, rot)
        lane = lax.broadcasted_iota(jnp.int32, (1, LANES), 1)
        dt = jnp.where((rowidx >= npad) & (lane < SSD_HEADS), _softplus(dtr + dtb), 0.0)
        return qn, kvn, jnp.concatenate([kr] * HEADS, axis=1), dt

    def fn_q_post(rowidx, q, cos, sin, rot):
        rr = _rope(q[:, QK_NOPE:], cos, sin, rot)
        return (jnp.concatenate([q[:, :QK_NOPE], rr], axis=1) * ATT_SCALE,)

    def fn_gated_norm(rowidx, y, xs, z, dskip, g):
        v = (y + xs * dskip) * _silu(z)
        return (v * lax.rsqrt(jnp.mean(v * v, axis=-1, keepdims=True) + RMS_EPS) * g,)

    def fn_mix(rowidx, ga, gs, ya, ys):
        return (_sigmoid(ga) * ya + _sigmoid(gs) * ys,)

    def fn_res_ln(rowidx, h, r, g, b):
        return (jnp.where(rowidx >= npad, _layer_norm(ALPHA * h + r, g, b), 0.0),)

    def fn_glu(rowidx, u):
        return (_silu(u[:, :D_FF]) * u[:, D_FF:],)

    def fn_delta(rowidx, do, o):
        s = jnp.sum(do * o, axis=-1, keepdims=True)
        return (jnp.broadcast_to(s, do.shape),)

    return dict(ln=fn_ln_masked, in_post=fn_in_post, q_post=fn_q_post, gated=fn_gated_norm, mix=fn_mix,
                res_ln=fn_res_ln, glu=fn_glu, delta=fn_delta)


def _conv_tiles(Tp, C):
    return _pick(Tp, ROW_TILE), _pick(C, MM_COL_TILE)


def _conv_fwd(x, xoff, C, w8, b, K, act, npad, name):
    Tp = x.shape[0]
    tm, tc = _conv_tiles(Tp, C)
    assert xoff % tc == 0
    cb0 = xoff // tc
    rb = tm // SUBLANES

    def body(prev_ref, main_ref, w_ref, b_ref, o_ref):
        i = pl.program_id(1)
        main = main_ref[...].astype(F32)
        prev = jnp.where(i > 0, prev_ref[...].astype(F32), 0.0)
        ext = jnp.concatenate([prev, main], axis=0)
        acc = b_ref[...] + w_ref[K - 1:K, :] * main
        for k in range(K - 1):
            s = K - 1 - k
            acc = acc + w_ref[k:k + 1, :] * pltpu.roll(ext, s, 0)[SUBLANES:, :]
        if act:
            rowidx = i * tm + lax.broadcasted_iota(jnp.int32, (tm, 1), 0)
            acc = jnp.where(rowidx >= npad, _silu(acc), 0.0)
        o_ref[...] = acc.astype(o_ref.dtype)

    return pl.pallas_call(
        body, name=name, grid=(C // tc, Tp // tm),
        in_specs=[pl.BlockSpec((SUBLANES, tc), lambda g, i: (jnp.maximum(i * rb - 1, 0), cb0 + g)),
                  pl.BlockSpec((tm, tc), lambda g, i: (i, cb0 + g)),
                  pl.BlockSpec((SUBLANES, tc), lambda g, i: (0, g)),
                  pl.BlockSpec((1, tc), lambda g, i: (0, g))],
        out_specs=pl.BlockSpec((tm, tc), lambda g, i: (i, g)),
        out_shape=jax.ShapeDtypeStruct((Tp, C), F32),
        compiler_params=_params(("parallel", "parallel"), 8 * tm * tc * 4),
    )(x, x, w8, b)


def _conv_bwd(x, xoff, C, w8, b, dy, K, act, npad, name):
    Tp = x.shape[0]
    tm, tc = _conv_tiles(Tp, C)
    cb0 = xoff // tc
    rb = tm // SUBLANES
    ni = Tp // tm
    last_rb = Tp // SUBLANES - 1
    n = tm + 2 * SUBLANES

    def body(xp_ref, xm_ref, xn_ref, dym_ref, dyn_ref, w_ref, b_ref, dx_ref, dw_ref, db_ref):
        i = pl.program_id(1)
        prev = jnp.where(i > 0, xp_ref[...].astype(F32), 0.0)
        ext = jnp.concatenate([prev, xm_ref[...].astype(F32), xn_ref[...].astype(F32)], axis=0)
        dyn = jnp.where(i < ni - 1, dyn_ref[...].astype(F32), 0.0)
        dpre = jnp.concatenate([jnp.zeros((SUBLANES, tc), F32), dym_ref[...].astype(F32), dyn], axis=0)
        shifted = [ext if k == K - 1 else pltpu.roll(ext, K - 1 - k, 0) for k in range(K)]
        if act:
            pre = b_ref[...] + sum(w_ref[k:k + 1, :] * shifted[k] for k in range(K))
            rowidx = i * tm - SUBLANES + lax.broadcasted_iota(jnp.int32, (n, 1), 0)
            sg = _sigmoid(pre)
            dpre = jnp.where(rowidx >= npad, dpre * sg * (1.0 + pre * (1.0 - sg)), 0.0)
        dx = w_ref[K - 1:K, :] * dpre
        for k in range(K - 1):
            dx = dx + w_ref[k:k + 1, :] * pltpu.roll(dpre, n - (K - 1 - k), 0)
        dx_ref[...] = dx[SUBLANES:SUBLANES + tm, :].astype(dx_ref.dtype)

        @pl.when(i == 0)
        def _():
            dw_ref[...] = jnp.zeros_like(dw_ref)
            db_ref[...] = jnp.zeros_like(db_ref)

        dmain = dpre[SUBLANES:SUBLANES + tm, :]
        for k in range(K):
            dw_ref[k:k + 1, :] += jnp.sum(dmain * shifted[k][SUBLANES:SUBLANES + tm, :], axis=0, keepdims=True)
        db_ref[...] += jnp.sum(dmain, axis=0, keepdims=True)

    return pl.pallas_call(
        body, name=name, grid=(C // tc, ni),
        in_specs=[pl.BlockSpec((SUBLANES, tc), lambda g, i: (jnp.maximum(i * rb - 1, 0), cb0 + g)),
                  pl.BlockSpec((tm, tc), lambda g, i: (i, cb0 + g)),
                  pl.BlockSpec((SUBLANES, tc), lambda g, i: (jnp.minimum((i + 1) * rb, last_rb), cb0 + g)),
                  pl.BlockSpec((tm, tc), lambda g, i: (i, g)),
                  pl.BlockSpec((SUBLANES, tc), lambda g, i: (jnp.minimum((i + 1) * rb, last_rb), g)),
                  pl.BlockSpec((SUBLANES, tc), lambda g, i: (0, g)),
                  pl.BlockSpec((1, tc), lambda g, i: (0, g))],
        out_specs=[pl.BlockSpec((tm, tc), lambda g, i: (i, g)),
                   pl.BlockSpec((SUBLANES, tc), lambda g, i: (0, g)),
                   pl.BlockSpec((1, tc), lambda g, i: (0, g))],
        out_shape=[jax.ShapeDtypeStruct((Tp, C), BF16), jax.ShapeDtypeStruct((SUBLANES, C), F32),
                   jax.ShapeDtypeStruct((1, C), F32)],
        compiler_params=_params(("parallel", "arbitrary"), 14 * tm * tc * 4),
    )(x, x, x, dy, dy, w8, b)


def _flash_fwd(q, kv, kr8, npad, name):
    Tp = q.shape[0]
    t = _pick(Tp, ATT_TILE)
    sub = min(t, ATT_SUB)
    nb = Tp // t
    nt = (((1,), (1,)), ((), ()))

    def body(q_ref, kn_ref, kr_ref, v_ref, o_ref, lse_ref, m_sc, l_sc, acc_sc):
        qi = pl.program_id(1)
        ki = pl.program_id(2)

        @pl.when(ki == 0)
        def _():
            m_sc[...] = jnp.full_like(m_sc, NEG_INF)
            l_sc[...] = jnp.zeros_like(l_sc)
            acc_sc[...] = jnp.zeros_like(acc_sc)

        def step(masked):
            k = jnp.concatenate([kn_ref[...], kr_ref[...]], axis=1)
            v = v_ref[...]
            for r in range(t // sub):
                rs = slice(r * sub, (r + 1) * sub)
                s = lax.dot_general(q_ref[rs, :], k, nt, preferred_element_type=F32)
                if masked:
                    row = qi * t + r * sub + lax.broadcasted_iota(jnp.int32, (sub, t), 0)
                    col = ki * t + lax.broadcasted_iota(jnp.int32, (sub, t), 1)
                    s = jnp.where((col <= row) & (col >= npad), s, NEG_INF)
                m_prev = m_sc[rs, :]
                m_new = jnp.maximum(m_prev, jnp.max(s, axis=-1, keepdims=True))
                p = jnp.exp(s - m_new)
                a = jnp.exp(m_prev - m_new)
                l_sc[rs, :] = a * l_sc[rs, :] + jnp.sum(p, axis=-1, keepdims=True)
                acc_sc[rs, :] = a * acc_sc[rs, :] + jnp.dot(p.astype(BF16), v, preferred_element_type=F32)
                m_sc[rs, :] = m_new

        need_mask = (ki == qi) | (ki == 0)

        @pl.when((ki <= qi) & need_mask)
        def _():
            step(True)

        @pl.when((ki <= qi) & jnp.logical_not(need_mask))
        def _():
            step(False)

        @pl.when(ki == qi)
        def _():
            l = l_sc[...]
            o_ref[...] = (acc_sc[...] / l).astype(o_ref.dtype)
            lse_ref[...] = jnp.broadcast_to(m_sc[...] + jnp.log(l), lse_ref.shape)

    kmap = lambda h, qi, ki: (jnp.minimum(ki, qi), h)
    return pl.pallas_call(
        body, name=name, grid=(HEADS, nb, nb),
        in_specs=[pl.BlockSpec((t, QHEAD), lambda h, qi, ki: (qi, h)),
                  pl.BlockSpec((t, QK_NOPE), kmap),
                  pl.BlockSpec((t, LANES), kmap),
                  pl.BlockSpec((t, V_HEAD), lambda h, qi, ki: (jnp.minimum(ki, qi), HEADS + h))],
        out_specs=[pl.BlockSpec((t, V_HEAD), lambda h, qi, ki: (qi, h)),
                   pl.BlockSpec((t, LANES), lambda h, qi, ki: (qi, h))],
        out_shape=[jax.ShapeDtypeStruct((Tp, HEADS * V_HEAD), F32), jax.ShapeDtypeStruct((Tp, HEADS * LANES), F32)],
        scratch_shapes=[pltpu.VMEM((t, 1), F32), pltpu.VMEM((t, 1), F32), pltpu.VMEM((t, V_HEAD), F32)],
        compiler_params=_params(("parallel", "parallel", "arbitrary"), 6 * t * t * 4),
    )(q, kv, kr8, kv)


def _flash_bwd(q, kv, kr8, do, lse, delta, npad, name):
    Tp = q.shape[0]
    t = _pick(Tp, ATT_TILE)
    sub = min(t, ATT_SUB)
    nb = Tp // t
    nt = (((1,), (1,)), ((), ()))
    tn = (((0,), (0,)), ((), ()))

    def body(q_ref, kn_ref, kr_ref, v_ref, do_ref, lse_ref, dl_ref, dq_ref, dkn_ref, dkr_ref, dv_ref, dk_sc, dv_sc):
        ki = pl.program_id(1)
        qi = pl.program_id(2)

        @pl.when(qi == 0)
        def _():
            dk_sc[...] = jnp.zeros_like(dk_sc)
            dv_sc[...] = jnp.zeros_like(dv_sc)

        def step(masked):
            k = jnp.concatenate([kn_ref[...], kr_ref[...]], axis=1)
            v = v_ref[...]
            dv_acc = jnp.zeros((t, V_HEAD), F32)
            dk_acc = jnp.zeros((t, QHEAD), F32)
            dq_parts = []
            for r in range(t // sub):
                rs = slice(r * sub, (r + 1) * sub)
                qv = q_ref[rs, :]
                s = lax.dot_general(qv, k, nt, preferred_element_type=F32)
                if masked:
                    row = qi * t + r * sub + lax.broadcasted_iota(jnp.int32, (sub, t), 0)
                    col = ki * t + lax.broadcasted_iota(jnp.int32, (sub, t), 1)
                    s = jnp.where((col <= row) & (col >= npad), s, NEG_INF)
                p = jnp.exp(s - lse_ref[rs, :1])
                dob = do_ref[rs, :].astype(BF16)
                dv_acc = dv_acc + lax.dot_general(p.astype(BF16), dob, tn, preferred_element_type=F32)
                dp = lax.dot_general(dob, v, nt, preferred_element_type=F32)
                ds = (p * (dp - dl_ref[rs, :1])).astype(BF16)
                dk_acc = dk_acc + lax.dot_general(ds, qv, tn, preferred_element_type=F32)
                dq_parts.append(jnp.dot(ds, k, preferred_element_type=F32))
            dv_sc[...] += dv_acc
            dk_sc[...] += dk_acc
            dqc = jnp.concatenate(dq_parts, axis=0)
            rows = pl.ds(pl.multiple_of(qi * t, t), t)

            @pl.when(ki == 0)
            def _():
                dq_ref[rows, :] = dqc

            @pl.when(ki > 0)
            def _():
                dq_ref[rows, :] += dqc

        need_mask = (ki == qi) | (ki == 0)

        @pl.when((qi >= ki) & need_mask)
        def _():
            step(True)

        @pl.when((qi >= ki) & jnp.logical_not(need_mask))
        def _():
            step(False)

        @pl.when(qi == nb - 1)
        def _():
            dkn_ref[...] = dk_sc[:, :QK_NOPE].astype(dkn_ref.dtype)
            dkr_ref[...] = dk_sc[:, QK_NOPE:].astype(dkr_ref.dtype)
            dv_ref[...] = dv_sc[...].astype(dv_ref.dtype)

    qmap = lambda h, ki, qi: (jnp.maximum(qi, ki), h)
    kmap = lambda h, ki, qi: (ki, h)
    est = 2 * Tp * QHEAD * 4 + 8 * t * t * 4
    return pl.pallas_call(
        body, name=name, grid=(HEADS, nb, nb),
        in_specs=[pl.BlockSpec((t, QHEAD), qmap),
                  pl.BlockSpec((t, QK_NOPE), kmap),
                  pl.BlockSpec((t, LANES), kmap),
                  pl.BlockSpec((t, V_HEAD), lambda h, ki, qi: (ki, HEADS + h)),
                  pl.BlockSpec((t, V_HEAD), qmap),
                  pl.BlockSpec((t, LANES), qmap),
                  pl.BlockSpec((t, LANES), qmap)],
        out_specs=[pl.BlockSpec((Tp, QHEAD), lambda h, ki, qi: (0, h)),
                   pl.BlockSpec((t, QK_NOPE), kmap),
                   pl.BlockSpec((t, LANES), kmap),
                   pl.BlockSpec((t, V_HEAD), kmap)],
        out_shape=[jax.ShapeDtypeStruct((Tp, HEADS * QHEAD), F32),
                   jax.ShapeDtypeStruct((Tp, HEADS * QK_NOPE), BF16),
                   jax.ShapeDtypeStruct((Tp, HEADS * LANES), F32),
                   jax.ShapeDtypeStruct((Tp, HEADS * V_HEAD), BF16)],
        scratch_shapes=[pltpu.VMEM((t, QHEAD), F32), pltpu.VMEM((t, V_HEAD), F32)],
        compiler_params=_params(("parallel", "arbitrary", "arbitrary"), est),
    )(q, kv, kr8, kv, do, lse, delta)


GW = SSD_INNER // SSD_GROUPS
PAIRS_PER_GROUP = GW // LANES
XB = SSD_INNER // GW
NT_DIMS = (((1,), (1,)), ((), ()))
TN_DIMS = (((0,), (0,)), ((), ()))


def _ssd_common(xs_ref, dt_ref, alog_ref, e_ref):
    a_neg = -jnp.exp(alog_ref[...])
    dt = dt_ref[...]
    li = lax.broadcasted_iota(jnp.int32, (CHUNK, CHUNK), 0)
    si = lax.broadcasted_iota(jnp.int32, (CHUNK, CHUNK), 1)
    tril = li >= si
    tri = tril.astype(F32)
    acs = jnp.dot(tri, dt * a_neg, precision=HIGHEST, preferred_element_type=F32)
    e = e_ref[...]
    dte = jnp.dot(dt, e, precision=HIGHEST, preferred_element_type=F32)
    acse = jnp.dot(acs, e, precision=HIGHEST, preferred_element_type=F32)
    x = xs_ref[...] * dte
    alast = acse[CHUNK - 1:CHUNK, :]
    return dict(a_neg=a_neg, dt=dt, tril=tril, tri=tri, acs=acs, acs_t=acs.T, e=e, dte=dte, acse=acse, x=x,
                p_e=jnp.exp(acse), w_e=jnp.exp(alast - acse), dl_e=jnp.exp(alast), li=li, si=si)


def _decay(cm, head):
    col = cm["acs"][:, head:head + 1]
    row = cm["acs_t"][head:head + 1, :]
    return jnp.exp(jnp.where(cm["tril"], col - row, -jnp.inf))


def _ssd_fwd(xbc, dt, alog, e, name):
    Tp = xbc.shape[0]
    nc = Tp // CHUNK

    def body(xs_ref, b_ref, c_ref, dt_ref, alog_ref, e_ref, y_ref, st_ref, st_sc):
        @pl.when(pl.program_id(0) == 0)
        def _():
            st_sc[...] = jnp.zeros_like(st_sc)

        cm = _ssd_common(xs_ref, dt_ref, alog_ref, e_ref)
        st_ref[0] = st_sc[...]
        lane = lax.broadcasted_iota(jnp.int32, (CHUNK, LANES), 1)
        for g in range(SSD_GROUPS):
            gs = slice(g * GW, (g + 1) * GW)
            cg = c_ref[:, g * SSD_STATE:(g + 1) * SSD_STATE].astype(BF16)
            bg = b_ref[:, g * SSD_STATE:(g + 1) * SSD_STATE].astype(BF16)
            cb = lax.dot_general(cg, bg, NT_DIMS, preferred_element_type=F32)
            stg = st_sc[:, gs]
            yoff = jnp.dot(cg, stg.astype(BF16), preferred_element_type=F32) * cm["p_e"][:, gs]
            xg = cm["x"][:, gs]
            for jp in range(PAIRS_PER_GROUP):
                j = g * PAIRS_PER_GROUP + jp
                xp = xg[:, jp * LANES:(jp + 1) * LANES].astype(BF16)
                ys = []
                for head in (2 * j, 2 * j + 1):
                    m = (cb * _decay(cm, head)).astype(BF16)
                    ys.append(jnp.dot(m, xp, preferred_element_type=F32))
                y_ref[:, j * LANES:(j + 1) * LANES] = (jnp.where(lane < SSD_HEAD_DIM, ys[0], ys[1])
                                                       + yoff[:, jp * LANES:(jp + 1) * LANES])
            snew = lax.dot_general(bg, (cm["w_e"][:, gs] * xg).astype(BF16), TN_DIMS, preferred_element_type=F32)
            st_sc[:, gs] = cm["dl_e"][:, gs] * stg + snew

    return pl.pallas_call(
        body, name=name, grid=(nc,),
        in_specs=[pl.BlockSpec((CHUNK, SSD_INNER), lambda c: (c, 0)),
                  pl.BlockSpec((CHUNK, GW), lambda c: (c, XB)),
                  pl.BlockSpec((CHUNK, GW), lambda c: (c, XB + 1)),
                  pl.BlockSpec((CHUNK, LANES), lambda c: (c, 0)),
                  pl.BlockSpec((1, LANES), lambda c: (0, 0)),
                  pl.BlockSpec((LANES, SSD_INNER), lambda c: (0, 0))],
        out_specs=[pl.BlockSpec((CHUNK, SSD_INNER), lambda c: (c, 0)),
                   pl.BlockSpec((1, SSD_STATE, SSD_INNER), lambda c: (c, 0, 0))],
        out_shape=[jax.ShapeDtypeStruct((Tp, SSD_INNER), F32), jax.ShapeDtypeStruct((nc, SSD_STATE, SSD_INNER), F32)],
        scratch_shapes=[pltpu.VMEM((SSD_STATE, SSD_INNER), F32)],
        compiler_params=_params(("arbitrary",), 24 * CHUNK * SSD_INNER * 4),
    )(xbc, xbc, xbc, dt, alog, e)


def _ssd_bwd(xbc, dt, alog, e, dy, dxs_skip, states, name):
    Tp = xbc.shape[0]
    nc = Tp // CHUNK
    rev = lambda c: nc - 1 - c

    def body(xs_ref, b_ref, c_ref, dt_ref, alog_ref, e_ref, dy_ref, skip_ref, st_ref,
             dxbc_ref, ddt_ref, dalog_ref, dst_sc, dx_sc, t_sc, tw_sc):
        @pl.when(pl.program_id(0) == 0)
        def _():
            dst_sc[...] = jnp.zeros_like(dst_sc)
            dalog_ref[...] = jnp.zeros_like(dalog_ref)

        cm = _ssd_common(xs_ref, dt_ref, alog_ref, e_ref)
        lane = lax.broadcasted_iota(jnp.int32, (CHUNK, LANES), 1)
        dacs_col = jnp.zeros((CHUNK, LANES), F32)
        dacs_row = jnp.zeros((LANES, CHUNK), F32)
        t_last = []
        for g in range(SSD_GROUPS):
            gs = slice(g * GW, (g + 1) * GW)
            cg = c_ref[:, g * SSD_STATE:(g + 1) * SSD_STATE].astype(BF16)
            bg = b_ref[:, g * SSD_STATE:(g + 1) * SSD_STATE].astype(BF16)
            stg = st_ref[0, :, gs]
            stg_b = stg.astype(BF16)
            dstg = dst_sc[:, gs]
            dstg_b = dstg.astype(BF16)
            xg = cm["x"][:, gs]
            dyg = dy_ref[:, gs]
            zg = jnp.dot(cg, stg_b, preferred_element_type=F32)
            dzg = dyg * cm["p_e"][:, gs]
            dzg_b = dzg.astype(BF16)
            dcg = lax.dot_general(dzg_b, stg_b, NT_DIMS, preferred_element_type=F32)
            dst_in = lax.dot_general(cg, dzg_b, TN_DIMS, preferred_element_type=F32)
            dst_in = dst_in + cm["dl_e"][:, gs] * dstg
            t_last.append(jnp.sum(dstg * stg * cm["dl_e"][:, gs], axis=0, keepdims=True))
            weg = cm["w_e"][:, gs]
            dbg = lax.dot_general((weg * xg).astype(BF16), dstg_b, NT_DIMS, preferred_element_type=F32)
            gg = jnp.dot(bg, dstg_b, preferred_element_type=F32)
            dxg = weg * gg
            tw_sc[:, gs] = xg * dxg
            t_sc[:, gs] = dzg * zg - xg * dxg
            cb = lax.dot_general(cg, bg, NT_DIMS, preferred_element_type=F32)
            dcb = jnp.zeros((CHUNK, CHUNK), F32)
            for jp in range(PAIRS_PER_GROUP):
                j = g * PAIRS_PER_GROUP + jp
                ps = slice(jp * LANES, (jp + 1) * LANES)
                xp = xg[:, ps].astype(BF16)
                dyp = dyg[:, ps]
                dxp = dxg[:, ps]
                for half, head in enumerate((2 * j, 2 * j + 1)):
                    lam = _decay(cm, head)
                    m32 = cb * lam
                    sel = (lane < SSD_HEAD_DIM) if half == 0 else (lane >= SSD_HEAD_DIM)
                    dye = jnp.where(sel, dyp, 0.0).astype(BF16)
                    dm = lax.dot_general(dye, xp, NT_DIMS, preferred_element_type=F32)
                    w = dm * m32
                    dacs_col = dacs_col + jnp.where(cm["si"] == head, jnp.sum(w, axis=1, keepdims=True), 0.0)
                    dacs_row = dacs_row + jnp.where(cm["li"] == head, jnp.sum(w, axis=0, keepdims=True), 0.0)
                    dcb = dcb + dm * lam
                    dxp = dxp + lax.dot_general(m32.astype(BF16), dye, TN_DIMS, preferred_element_type=F32)
                dx_sc[:, j * LANES:(j + 1) * LANES] = dxp
            dcb_b = dcb.astype(BF16)
            dcg = dcg + jnp.dot(dcb_b, bg, preferred_element_type=F32)
            dbg = dbg + lax.dot_general(dcb_b, cg, TN_DIMS, preferred_element_type=F32)
            dst_sc[:, gs] = dst_in
            dxbc_ref[:, SSD_INNER + g * SSD_STATE:SSD_INNER + (g + 1) * SSD_STATE] = dbg
            dxbc_ref[:, SSD_INNER + GW + g * SSD_STATE:SSD_INNER + GW + (g + 1) * SSD_STATE] = dcg
        e = cm["e"]
        dacs = lax.dot_general(t_sc[...], e, NT_DIMS, precision=HIGHEST, preferred_element_type=F32)
        dacs = dacs + dacs_col - dacs_row.T
        last_lane = jnp.concatenate(t_last, axis=1) + jnp.sum(tw_sc[...], axis=0, keepdims=True)
        last_head = lax.dot_general(jnp.broadcast_to(last_lane, (SUBLANES, SSD_INNER)), e, NT_DIMS,
                                    precision=HIGHEST, preferred_element_type=F32)[0:1, :]
        dacs = dacs + jnp.where(cm["li"] == CHUNK - 1, last_head, 0.0)
        da = lax.dot_general(cm["tri"], dacs, TN_DIMS, precision=HIGHEST, preferred_element_type=F32)
        dx_all = dx_sc[...]
        ddt = da * cm["a_neg"] + lax.dot_general(dx_all * xs_ref[...], e, NT_DIMS, precision=HIGHEST,
                                                 preferred_element_type=F32)
        ddt_ref[...] = ddt
        dxbc_ref[:, :SSD_INNER] = dx_all * cm["dte"] + skip_ref[...]
        dalog_ref[0:1, :] += jnp.sum(da * cm["dt"], axis=0, keepdims=True) * cm["a_neg"]

    return pl.pallas_call(
        body, name=name, grid=(nc,),
        in_specs=[pl.BlockSpec((CHUNK, SSD_INNER), lambda c: (rev(c), 0)),
                  pl.BlockSpec((CHUNK, GW), lambda c: (rev(c), XB)),
                  pl.BlockSpec((CHUNK, GW), lambda c: (rev(c), XB + 1)),
                  pl.BlockSpec((CHUNK, LANES), lambda c: (rev(c), 0)),
                  pl.BlockSpec((1, LANES), lambda c: (0, 0)),
                  pl.BlockSpec((LANES, SSD_INNER), lambda c: (0, 0)),
                  pl.BlockSpec((CHUNK, SSD_INNER), lambda c: (rev(c), 0)),
                  pl.BlockSpec((CHUNK, SSD_INNER), lambda c: (rev(c), 0)),
                  pl.BlockSpec((1, SSD_STATE, SSD_INNER), lambda c: (rev(c), 0, 0))],
        out_specs=[pl.BlockSpec((CHUNK, SSD_CONV_DIM), lambda c: (rev(c), 0)),
                   pl.BlockSpec((CHUNK, LANES), lambda c: (rev(c), 0)),
                   pl.BlockSpec((SUBLANES, LANES), lambda c: (0, 0))],
        out_shape=[jax.ShapeDtypeStruct((Tp, SSD_CONV_DIM), F32), jax.ShapeDtypeStruct((Tp, LANES), F32),
                   jax.ShapeDtypeStruct((SUBLANES, LANES), F32)],
        scratch_shapes=[pltpu.VMEM((SSD_STATE, SSD_INNER), F32), pltpu.VMEM((CHUNK, SSD_INNER), F32),
                        pltpu.VMEM((CHUNK, SSD_INNER), F32), pltpu.VMEM((CHUNK, SSD_INNER), F32)],
        compiler_params=_params(("arbitrary",), 32 * CHUNK * SSD_INNER * 4),
    )(xbc, xbc, xbc, dt, alog, e, dy, dxs_skip, states)


def _loss_head(h, target, name):
    Tp, d = h.shape
    nt = Tp // LANES

    def body(h_ref, t_ref, dh_ref, l_ref):
        real = pl.program_id(0) > 0
        err = jnp.where(real, h_ref[...] - t_ref[...], 0.0)
        dh_ref[...] = err * (1.0 / d)
        l_ref[...] = jnp.broadcast_to(0.5 * jnp.sum(err * err) * (1.0 / d), l_ref.shape)

    return pl.pallas_call(
        body, name=name, grid=(nt,),
        in_specs=[pl.BlockSpec((LANES, d), lambda i: (i, 0)),
                  pl.BlockSpec((LANES, d), lambda i: (jnp.maximum(i - 1, 0), 0))],
        out_specs=[pl.BlockSpec((LANES, d), lambda i: (i, 0)),
                   pl.BlockSpec((1, SUBLANES, LANES), lambda i: (i, 0, 0))],
        out_shape=[jax.ShapeDtypeStruct((Tp, d), F32), jax.ShapeDtypeStruct((nt, SUBLANES, LANES), F32)],
        compiler_params=_params(("parallel",), 8 * LANES * d * 4),
    )(h, target)


def _adamw(parts, w, m, v, name):
    shape = w.shape
    C = shape[-1]
    R = int(np.prod(shape[:-1]))
    parts, w, m, v = parts.reshape(N_DEV, R, C), w.reshape(R, C), m.reshape(R, C), v.reshape(R, C)
    lanes = -(-C // LANES) * LANES
    tr = _pick(R, max(BF16_ROWS, ADAM_ELEMS // lanes), BF16_ROWS) if R % BF16_ROWS == 0 else R
    c1 = 1.0 / (1.0 - ADAM_B1 ** ADAM_STEP)
    c2 = 1.0 / (1.0 - ADAM_B2 ** ADAM_STEP)

    def body(p_ref, w_ref, m_ref, v_ref, g_out, d_out, m_out, v_out):
        g = p_ref[0].astype(F32)
        for p in range(1, N_DEV):
            g = g + p_ref[p].astype(F32)
        m_new = ADAM_B1 * m_ref[...] + (1.0 - ADAM_B1) * g
        v_new = ADAM_B2 * v_ref[...] + (1.0 - ADAM_B2) * (g * g)
        g_out[...] = g
        m_out[...] = m_new
        v_out[...] = v_new
        d_out[...] = -ADAM_LR * ((m_new * c1) / (jnp.sqrt(v_new * c2) + ADAM_EPS) + ADAM_WD * w_ref[...])

    spec = pl.BlockSpec((tr, C), lambda i: (i, 0))
    est = N_DEV * tr * lanes * parts.dtype.itemsize + 7 * tr * lanes * 4
    res = pl.pallas_call(
        body, name=name, grid=(R // tr,),
        in_specs=[pl.BlockSpec((N_DEV, tr, C), lambda i: (0, i, 0)), spec, spec, spec],
        out_specs=[spec] * 4, out_shape=[jax.ShapeDtypeStruct((R, C), F32)] * 4,
        compiler_params=_params(("parallel",), est),
    )(parts, w, m, v)
    return [r.reshape(shape) for r in res]


MESH_ID = pl.DeviceIdType.MESH
N_PEERS = N_DEV - 1


def _dev_index(p):
    return 4 * p[0] + 2 * p[1] + p[2]


def _comm_call(body, name, arrs, out_shape):
    n = len(arrs)
    any_spec = pl.BlockSpec(memory_space=pl.ANY)
    return pl.pallas_call(
        functools.partial(body, n), name=name, in_specs=[any_spec] * n, out_specs=[any_spec] * n, out_shape=out_shape,
        scratch_shapes=[pltpu.SemaphoreType.DMA((n, N_PEERS)), pltpu.SemaphoreType.DMA((n, N_PEERS)),
                        pltpu.SemaphoreType.DMA((n,))],
    )(*arrs)


def _allgather(arrs, name):
    def body(n, *refs):
        src_refs, out_refs = refs[:n], refs[n:2 * n]
        send_sems, recv_sems, local_sems = refs[2 * n:]
        x, y, c = lax.axis_index("x"), lax.axis_index("y"), lax.axis_index("c")
        me, sibling = (x, y, c), (x, y, 1 - c)
        chips = [(1 - x, y), (x, 1 - y), (1 - x, 1 - y)]

        def copy(t, k, block, to, src=None):
            slot = out_refs[t].at[_dev_index(block)]
            return pltpu.make_async_remote_copy(
                src_ref=slot if src is None else src, dst_ref=slot,
                send_sem=send_sems.at[t, k], recv_sem=recv_sems.at[t, k],
                device_id=to, device_id_type=MESH_ID)

        sends, locals_ = [], []
        for t in range(n):
            mine = pltpu.make_async_copy(src_refs[t], out_refs[t].at[_dev_index(me)], local_sems.at[t])
            mine.start()
            locals_.append(mine)
            first = [copy(t, 0, me, sibling, src=src_refs[t])]
            first += [copy(t, 1 + j, me, (*chip, c), src=src_refs[t]) for j, chip in enumerate(chips)]
            for cp in first:
                cp.start()
            sends += first
        for j, chip in enumerate(chips):
            for t in range(n):
                copy(t, 1 + j, (*chip, c), me).wait_recv()
                passed = copy(t, 4 + j, (*chip, c), sibling)
                passed.start()
                sends.append(passed)
        for t in range(n):
            copy(t, 0, sibling, me).wait_recv()
            for j, chip in enumerate(chips):
                copy(t, 4 + j, (*chip, 1 - c), me).wait_recv()
        for cp in sends:
            cp.wait_send()
        for cp in locals_:
            cp.wait()

    return _comm_call(body, name, arrs, [jax.ShapeDtypeStruct((N_DEV,) + a.shape, a.dtype) for a in arrs])


def _exchange_pieces(arrs, name):
    def body(n, *refs):
        in_refs, out_refs = refs[:n], refs[n:2 * n]
        send_sems, recv_sems, local_sems = refs[2 * n:]
        x, y, c = lax.axis_index("x"), lax.axis_index("y"), lax.axis_index("c")
        me = (x, y, c)
        peers = [(x, y, 1 - c), (1 - x, y, c), (x, 1 - y, c), (1 - x, 1 - y, c),
                 (1 - x, y, 1 - c), (x, 1 - y, 1 - c), (1 - x, 1 - y, 1 - c)]

        def copy(t, k, src_block, dst_block, to):
            return pltpu.make_async_remote_copy(
                src_ref=in_refs[t].at[_dev_index(src_block)], dst_ref=out_refs[t].at[_dev_index(dst_block)],
                send_sem=send_sems.at[t, k], recv_sem=recv_sems.at[t, k],
                device_id=to, device_id_type=MESH_ID)

        sends, locals_ = [], []
        for t in range(n):
            mine = pltpu.make_async_copy(in_refs[t].at[_dev_index(me)], out_refs[t].at[_dev_index(me)], local_sems.at[t])
            mine.start()
            locals_.append(mine)
            for k, p in enumerate(peers):
                cp = copy(t, k, p, me, p)
                cp.start()
                sends.append(cp)
        for t in range(n):
            for k, p in enumerate(peers):
                copy(t, k, p, p, me).wait_recv()
        for cp in sends:
            cp.wait_send()
        for cp in locals_:
            cp.wait()

    return _comm_call(body, name, arrs, [jax.ShapeDtypeStruct(a.shape, a.dtype) for a in arrs])


WEIGHTS = ['meta_tokens', 'emb_ln_g', 'emb_ln_b', 'w_in', 'q_norm_g', 'w_q_b', 'kv_norm_g', 'w_kv_b', 'w_o_attn',
           'ssd_conv_w', 'ssd_conv_b', 'dt_bias', 'a_log', 'd_skip', 'ssd_norm_g', 'w_o_ssd', 'w_out', 'ln1_g',
           'ln1_b', 'w_up', 'ffn_conv_w', 'ffn_conv_b', 'w_down', 'ln2_g', 'ln2_b']
BIG = {'w_in': 2, 'w_q_b': 2, 'w_kv_b': 2, 'w_o_attn': 1, 'w_o_ssd': 1, 'w_out': 1, 'w_up': 2, 'w_down': 1}
SMALL_SHARDED = {'meta_tokens': 1, 'ssd_conv_w': 2, 'ffn_conv_w': 2}
REPLICATED = [n for n in WEIGHTS if n not in BIG and n not in SMALL_SHARDED]
BIG_COLS = 1024
SMALL_COLS = LANES


def _flatten(arrs, cols, row_mult, lead=False):
    parts, offs, off = [], [], 0
    for a in arrs:
        a2 = a.reshape(N_DEV, -1) if lead else a.reshape(1, -1)
        n = a2.shape[1]
        pad = -n % cols
        parts.append(jnp.pad(a2, ((0, 0), (0, pad))))
        offs.append((off, n))
        off += n + pad
    rows = off // cols
    extra = (-rows % row_mult) * cols
    if extra:
        parts.append(jnp.zeros((parts[0].shape[0], extra), parts[0].dtype))
    flat = jnp.concatenate(parts, axis=1)
    flat = flat.reshape(flat.shape[0], -1, cols)
    return (flat if lead else flat[0]), offs


def _unflatten(flat, offs, shapes):
    f = flat.reshape(-1)
    return [f[o:o + n].reshape(s) for (o, n), s in zip(offs, shapes)]


def _to_pieces(g, axis):
    s = g.shape[axis] // N_DEV
    g = g.reshape(g.shape[:axis] + (N_DEV, s) + g.shape[axis + 1:])
    return jnp.moveaxis(g, axis, 0).reshape(N_DEV, -1)


def _from_pieces(p, shard_shape, axis):
    g = jnp.moveaxis(p.reshape((N_DEV,) + tuple(shard_shape)), 0, axis)
    sh = list(shard_shape)
    sh[axis] *= N_DEV
    return g.reshape(sh)


def _in_proj_pad(w):
    e = np.cumsum((0,) + IN_SIZES)
    ql, kvl, kpe, z, xbc, dt, ga, gs = [w[:, e[j]:e[j + 1]] for j in range(8)]
    zc = lambda n: jnp.zeros((w.shape[0], n), w.dtype)
    return jnp.concatenate([ql, kvl, z, xbc, ga, gs, kpe, zc(LANES - QK_ROPE), dt, zc(LANES - SSD_HEADS)], axis=1)


def _in_proj_unpad(d):
    seg = lambda o, n: d[:, o:o + n]
    return jnp.concatenate([seg(OQ, Q_LORA), seg(OKV, KV_LORA), seg(OKPE, QK_ROPE), seg(OZ, SSD_INNER),
                            seg(OXBC, SSD_CONV_DIM), seg(ODT, SSD_HEADS), seg(OGA, D_MODEL), seg(OGS, D_MODEL)], axis=1)


def _q_pad(w):
    w3 = w.reshape(Q_LORA, HEADS, QK_NOPE + QK_ROPE)
    return jnp.concatenate([w3, jnp.zeros((Q_LORA, HEADS, QHEAD - QK_NOPE - QK_ROPE), w.dtype)], axis=2).reshape(Q_LORA, HEADS * QHEAD)


def _q_unpad(d):
    return d.reshape(Q_LORA, HEADS, QHEAD)[:, :, :QK_NOPE + QK_ROPE].reshape(Q_LORA, HEADS * (QK_NOPE + QK_ROPE))


def _kv_perm(w):
    w3 = w.reshape(KV_LORA, HEADS, QK_NOPE + V_HEAD)
    return jnp.concatenate([w3[:, :, :QK_NOPE].reshape(KV_LORA, -1), w3[:, :, QK_NOPE:].reshape(KV_LORA, -1)], axis=1)


def _kv_unperm(d):
    kn = d[:, :HEADS * QK_NOPE].reshape(KV_LORA, HEADS, QK_NOPE)
    v = d[:, HEADS * QK_NOPE:].reshape(KV_LORA, HEADS, V_HEAD)
    return jnp.concatenate([kn, v], axis=2).reshape(KV_LORA, HEADS * (QK_NOPE + V_HEAD))


def _row_vec(v, width=None):
    v = v.reshape(1, -1).astype(F32)
    if width is not None and v.shape[1] < width:
        v = jnp.pad(v, ((0, 0), (0, width - v.shape[1])))
    return v


def _pad_rows8(w):
    return jnp.pad(w.astype(F32), ((0, SUBLANES - w.shape[0]), (0, 0)))


def _tables(Tp, npad):
    pos = jnp.maximum(jnp.arange(Tp, dtype=jnp.int32) - npad, 0).astype(F32)
    inv_freq = 1.0 / (ROPE_THETA ** (jnp.arange(0, QK_ROPE, 2, dtype=F32) / QK_ROPE))
    ang = pos[:, None] * inv_freq[None, :]
    ang = jnp.concatenate([ang, ang], axis=-1)
    zeros = jnp.zeros((Tp, LANES - QK_ROPE), F32)
    cos = jnp.concatenate([jnp.cos(ang), zeros], axis=1)
    sin = jnp.concatenate([jnp.sin(ang), zeros], axis=1)
    rot = np.zeros((LANES, LANES), np.float32)
    half = QK_ROPE // 2
    for i in range(half):
        rot[i + half, i] = -1.0
        rot[i, i + half] = 1.0
    expand = np.zeros((LANES, SSD_INNER), np.float32)
    for hd in range(SSD_HEADS):
        expand[hd, hd * SSD_HEAD_DIM:(hd + 1) * SSD_HEAD_DIM] = 1.0
    return cos, sin, jnp.asarray(rot), jnp.asarray(expand)


def _layer_rows(proj, tb):
    rows_a = [_row(proj, Q_LORA, OQ // Q_LORA), _row(proj, KV_LORA, OKV // KV_LORA), _row(proj, LANES, OKPE // LANES),
              _row(proj, LANES, ODT // LANES), _row(tb["cos"], diff=False), _row(tb["sin"], diff=False)]
    return rows_a


def _layer_fwd(h, P, tb, fns, npad):
    proj = _mm(h, P["w_in"], F32, "in_proj")
    rows_a = _layer_rows(proj, tb)
    consts_a = [_row(tb["rot"], diff=False), _row(P["q_norm_g"]), _row(P["kv_norm_g"]), _row(P["dt_bias"])]
    qn, kvn, kr8, dt = _rw_fwd(fns["in_post"], rows_a, consts_a,
                               [_out(Q_LORA, BF16), _out(KV_LORA, BF16), _out(HEADS * LANES, BF16), _out(LANES, F32)],
                               "in_post")
    q = _mm(qn, P["w_q"], F32, "q_proj")
    rows_q = [_row(q, QHEAD, 0, grp=True), _row(tb["cos"], diff=False), _row(tb["sin"], diff=False)]
    qr = _rw_fwd(fns["q_post"], rows_q, [_row(tb["rot"], diff=False)], [_out(HEADS * QHEAD, BF16, QHEAD, grp=True)],
                 "q_post", ng=HEADS)[0]
    kv = _mm(kvn, P["w_kv"], BF16, "kv_proj")
    o, lse = _flash_fwd(qr, kv, kr8, npad, "attn_fwd")
    ya = _mm(o, P["w_o_attn"], F32, "attn_out")
    xbc = _conv_fwd(proj, OXBC, SSD_CONV_DIM, P["ssd_conv_w"], P["ssd_conv_b"], SSD_CONV, True, npad, "ssd_conv")
    y, states = _ssd_fwd(xbc, dt, P["a_log"], tb["expand"], "ssd_fwd")
    rows_b = [_row(y, GW, 0, grp=True), _row(xbc, GW, 0, grp=True), _row(proj, GW, OZ // GW, grp=True)]
    consts_b = [_row(P["d_skip"], GW, 0, grp=True), _row(P["ssd_norm_g"], GW, 0, grp=True)]
    yn = _rw_fwd(fns["gated"], rows_b, consts_b, [_out(SSD_INNER, BF16, GW, grp=True)], "ssd_gate", ng=SSD_GROUPS)[0]
    ys = _mm(yn, P["w_o_ssd"], F32, "ssd_out")
    rows_c = [_row(proj, D_MODEL, OGA // D_MODEL), _row(proj, D_MODEL, OGS // D_MODEL), _row(ya), _row(ys)]
    mixed = _rw_fwd(fns["mix"], rows_c, [], [_out(D_MODEL, BF16)], "mix")[0]
    mo = _mm(mixed, P["w_out"], F32, "mix_out")
    consts_1 = [_row(P["ln1_g"]), _row(P["ln1_b"])]
    h1 = _rw_fwd(fns["res_ln"], [_row(h), _row(mo)], consts_1, [_out(D_MODEL, F32)], "ln1")[0]
    up = _mm(h1, P["w_up"], F32, "ffn_up")
    u = _conv_fwd(up, 0, 2 * D_FF, P["ffn_conv_w"], P["ffn_conv_b"], FFN_CONV, False, npad, "ffn_conv")
    act = _rw_fwd(fns["glu"], [_row(u)], [], [_out(D_FF, BF16)], "ffn_glu")[0]
    fo = _mm(act, P["w_down"], F32, "ffn_down")
    consts_2 = [_row(P["ln2_g"]), _row(P["ln2_b"])]
    h2 = _rw_fwd(fns["res_ln"], [_row(h1), _row(fo)], consts_2, [_out(D_MODEL, F32)], "ln2")[0]
    res = dict(h=h, proj=proj, qn=qn, kvn=kvn, kr8=kr8, dt=dt, q=q, qr=qr, kv=kv, o=o, lse=lse, ya=ya, xbc=xbc, y=y,
               states=states, yn=yn, ys=ys, mixed=mixed, mo=mo, h1=h1, up=up, u=u, act=act, fo=fo)
    return h2, res


def _layer_bwd(dh2, r, P, tb, fns, npad):
    g = {}
    consts_2 = [_row(P["ln2_g"]), _row(P["ln2_b"])]
    (dh1_a, dfo), (g["ln2_g"], g["ln2_b"]) = _rw_bwd(fns["res_ln"], [_row(r["h1"]), _row(r["fo"])], consts_2,
                                                     [_row(dh2)], [F32, BF16], "ln2_bwd")
    g["w_down"] = _mm(r["act"], dfo, BF16, "dw_down", ta=True)
    dact = _mm(dfo, P["w_down"], F32, "d_act", tb=True)
    (du,), _ = _rw_bwd(fns["glu"], [_row(r["u"])], [], [_row(dact)], [F32], "glu_bwd")
    dup, g["ffn_conv_w"], g["ffn_conv_b"] = _conv_bwd(r["up"], 0, 2 * D_FF, P["ffn_conv_w"], P["ffn_conv_b"], du,
                                                      FFN_CONV, False, npad, "ffn_conv_bwd")
    g["w_up"] = _mm(r["h1"], dup, BF16, "dw_up", ta=True)
    dh1 = _mm(dup, P["w_up"], F32, "d_h1", tb=True, add=dh1_a)
    consts_1 = [_row(P["ln1_g"]), _row(P["ln1_b"])]
    (dh_a, dmo), (g["ln1_g"], g["ln1_b"]) = _rw_bwd(fns["res_ln"], [_row(r["h"]), _row(r["mo"])], consts_1,
                                                    [_row(dh1)], [F32, BF16], "ln1_bwd")
    g["w_out"] = _mm(r["mixed"], dmo, BF16, "dw_out", ta=True)
    dmixed = _mm(dmo, P["w_out"], F32, "d_mixed", tb=True)
    proj = r["proj"]
    rows_c = [_row(proj, D_MODEL, OGA // D_MODEL), _row(proj, D_MODEL, OGS // D_MODEL), _row(r["ya"]), _row(r["ys"])]
    (dga, dgs, dya, dys), _ = _rw_bwd(fns["mix"], rows_c, [], [_row(dmixed)], [BF16] * 4, "mix_bwd")
    g["w_o_attn"] = _mm(r["o"], dya, BF16, "dw_o_attn", ta=True)
    do = _mm(dya, P["w_o_attn"], F32, "d_o", tb=True)
    g["w_o_ssd"] = _mm(r["yn"], dys, BF16, "dw_o_ssd", ta=True)
    dyn = _mm(dys, P["w_o_ssd"], F32, "d_yn", tb=True)
    rows_b = [_row(r["y"], GW, 0, grp=True), _row(r["xbc"], GW, 0, grp=True), _row(proj, GW, OZ // GW, grp=True)]
    consts_b = [_row(P["d_skip"], GW, 0, grp=True), _row(P["ssd_norm_g"], GW, 0, grp=True)]
    (dy, dxs_skip, dz), (g["d_skip"], g["ssd_norm_g"]) = _rw_bwd(
        fns["gated"], rows_b, consts_b, [_row(dyn, GW, 0, grp=True)], [F32, F32, BF16], "ssd_gate_bwd", ng=SSD_GROUPS)
    dxbc, ddt, g["a_log"] = _ssd_bwd(r["xbc"], r["dt"], P["a_log"], tb["expand"], dy, dxs_skip, r["states"], "ssd_bwd")
    dxbc_pre, g["ssd_conv_w"], g["ssd_conv_b"] = _conv_bwd(proj, OXBC, SSD_CONV_DIM, P["ssd_conv_w"], P["ssd_conv_b"],
                                                           dxbc, SSD_CONV, True, npad, "ssd_conv_bwd")
    delta = _rw_fwd(fns["delta"], [_row(do, V_HEAD, 0, grp=True), _row(r["o"], V_HEAD, 0, grp=True)], [],
                    [_out(HEADS * LANES, F32, LANES, grp=True)], "attn_delta", ng=HEADS)[0]
    dqr, dkn, dkr8, dv = _flash_bwd(r["qr"], r["kv"], r["kr8"], do, r["lse"], delta, npad, "attn_bwd")
    rows_q = [_row(r["q"], QHEAD, 0, grp=True), _row(tb["cos"], diff=False), _row(tb["sin"], diff=False)]
    (dq,), _ = _rw_bwd(fns["q_post"], rows_q, [_row(tb["rot"], diff=False)], [_row(dqr, QHEAD, 0, grp=True)], [BF16],
                       "q_post_bwd", ng=HEADS)
    g["w_q"] = _mm(r["qn"], dq, BF16, "dw_q", ta=True)
    dqn = _mm(dq, P["w_q"], F32, "d_qn", tb=True)
    dkv = jnp.concatenate([dkn, dv], axis=1)
    g["w_kv"] = _mm(r["kvn"], dkv, BF16, "dw_kv", ta=True)
    dkvn = _mm(dkv, P["w_kv"], F32, "d_kvn", tb=True)
    rows_a = _layer_rows(proj, tb)
    consts_a = [_row(tb["rot"], diff=False), _row(P["q_norm_g"]), _row(P["kv_norm_g"]), _row(P["dt_bias"])]
    (dql, dkvl, dkpe, ddtr), (g["q_norm_g"], g["kv_norm_g"], g["dt_bias"]) = _rw_bwd(
        fns["in_post"], rows_a, consts_a, [_row(dqn), _row(dkvn), _row(dkr8), _row(ddt)], [BF16] * 4, "in_post_bwd")
    dproj = jnp.concatenate([dql, dkvl, dz, dxbc_pre, dga, dgs, dkpe, ddtr], axis=1)
    g["w_in"] = _mm(r["h"], dproj, BF16, "dw_in", ta=True)
    dh = _mm(dproj, P["w_in"], F32, "d_h", tb=True, add=dh_a)
    return dh, g


def _full_weight(g, axis, i):
    if axis == 1:
        return g[:, i].reshape(-1, g.shape[-1])
    return jnp.concatenate([g[p, i] for p in range(N_DEV)], axis=1)


def _grad_pieces(d, axis):
    if axis == 1:
        return d.reshape(N_DEV, -1, d.shape[1])
    return jnp.transpose(d.reshape(d.shape[0], N_DEV, -1), (1, 0, 2))


def _layer_params(gathered, small, i):
    full = {n: _full_weight(gathered[n], BIG[n], i) for n in BIG}
    P = {}
    P["w_in"] = _in_proj_pad(full["w_in"])
    P["w_q"] = _q_pad(full["w_q_b"])
    P["w_kv"] = _kv_perm(full["w_kv_b"])
    for n in ("w_o_attn", "w_o_ssd", "w_out", "w_up", "w_down"):
        P[n] = full[n]
    P["q_norm_g"] = _row_vec(small["q_norm_g"][i])
    P["kv_norm_g"] = _row_vec(small["kv_norm_g"][i])
    P["dt_bias"] = _row_vec(small["dt_bias"][i], LANES)
    P["a_log"] = _row_vec(small["a_log"][i], LANES)
    P["d_skip"] = _row_vec(jnp.repeat(small["d_skip"][i], SSD_HEAD_DIM))
    P["ssd_norm_g"] = _row_vec(small["ssd_norm_g"][i])
    P["ssd_conv_w"] = _pad_rows8(small["ssd_conv_w"][i])
    P["ssd_conv_b"] = _row_vec(small["ssd_conv_b"][i])
    P["ffn_conv_w"] = _pad_rows8(small["ffn_conv_w"][i])
    P["ffn_conv_b"] = _row_vec(small["ffn_conv_b"][i])
    for n in ("ln1_g", "ln1_b", "ln2_g", "ln2_b"):
        P[n] = _row_vec(small[n][i])
    return P


def _layer_grads_to_reference_layout(g):
    out = {}
    out["w_in"] = _in_proj_unpad(g["w_in"])
    out["w_q_b"] = _q_unpad(g["w_q"])
    out["w_kv_b"] = _kv_unperm(g["w_kv"])
    for n in ("w_o_attn", "w_o_ssd", "w_out", "w_up", "w_down"):
        out[n] = g[n]
    out["q_norm_g"] = g["q_norm_g"][0]
    out["kv_norm_g"] = g["kv_norm_g"][0]
    out["dt_bias"] = g["dt_bias"][0, :SSD_HEADS]
    out["a_log"] = g["a_log"][0, :SSD_HEADS]
    out["d_skip"] = g["d_skip"].reshape(SSD_HEADS, SSD_HEAD_DIM).sum(axis=1)
    out["ssd_norm_g"] = g["ssd_norm_g"][0]
    out["ssd_conv_w"] = g["ssd_conv_w"][:SSD_CONV]
    out["ssd_conv_b"] = g["ssd_conv_b"][0]
    out["ffn_conv_w"] = g["ffn_conv_w"][:FFN_CONV]
    out["ffn_conv_b"] = g["ffn_conv_b"][0]
    for n in ("ln1_g", "ln1_b", "ln2_g", "ln2_b"):
        out[n] = g[n][0]
    return out


def kernel(x, meta_tokens, emb_ln_g, emb_ln_b, w_in, q_norm_g, w_q_b, kv_norm_g, w_kv_b, w_o_attn, ssd_conv_w, ssd_conv_b, dt_bias, a_log, d_skip, ssd_norm_g, w_o_ssd, w_out, ln1_g, ln1_b, w_up, ffn_conv_w, ffn_conv_b, w_down, ln2_g, ln2_b, loss_target, m_meta_tokens, m_emb_ln_g, m_emb_ln_b, m_w_in, m_q_norm_g, m_w_q_b, m_kv_norm_g, m_w_kv_b, m_w_o_attn, m_ssd_conv_w, m_ssd_conv_b, m_dt_bias, m_a_log, m_d_skip, m_ssd_norm_g, m_w_o_ssd, m_w_out, m_ln1_g, m_ln1_b, m_w_up, m_ffn_conv_w, m_ffn_conv_b, m_w_down, m_ln2_g, m_ln2_b, v_meta_tokens, v_emb_ln_g, v_emb_ln_b, v_w_in, v_q_norm_g, v_w_q_b, v_kv_norm_g, v_w_kv_b, v_w_o_attn, v_ssd_conv_w, v_ssd_conv_b, v_dt_bias, v_a_log, v_d_skip, v_ssd_norm_g, v_w_o_ssd, v_w_out, v_ln1_g, v_ln1_b, v_w_up, v_ffn_conv_w, v_ffn_conv_b, v_w_down, v_ln2_g, v_ln2_b):
    given = dict(locals())
    w = {n: given[n] for n in WEIGHTS}
    m = {n: given["m_" + n] for n in WEIGHTS}
    v = {n: given["v_" + n] for n in WEIGHTS}
    seq = x.shape[1]
    assert x.shape[0] == 1 and seq % LANES == 0
    npad = LANES - N_META
    Tp = npad + N_META + seq
    depth = w_in.shape[0]

    big_names, small_names = list(BIG), list(SMALL_SHARDED)
    ws, offs_s = _flatten([w[n] for n in small_names], SMALL_COLS, SUBLANES)
    got = _allgather([w[n].astype(BF16) for n in big_names] + [ws], "weight_allgather")
    gathered = dict(zip(big_names, got[:-1]))
    gsm = got[-1]
    small = {n: w[n] for n in REPLICATED}
    for n, (o, sz) in zip(small_names, offs_s):
        small[n] = _from_pieces(gsm.reshape(N_DEV, -1)[:, o:o + sz], w[n].shape, SMALL_SHARDED[n])

    fns = _make_stage_fns(npad)
    cos, sin, rot, expand = _tables(Tp, npad)
    tb = dict(cos=cos, sin=sin, rot=rot, expand=expand)
    top = jnp.pad(small["meta_tokens"], ((npad, 0), (0, 0)))
    hcat = jnp.concatenate([top, x[0]], axis=0)
    consts_e = [_row(_row_vec(w["emb_ln_g"])), _row(_row_vec(w["emb_ln_b"]))]
    h = _rw_fwd(fns["ln"], [_row(hcat)], consts_e, [_out(D_MODEL, F32)], "emb_ln")[0]
    layers = [_layer_params(gathered, small, i) for i in range(depth)]
    saved = []
    for i in range(depth):
        h, res = _layer_fwd(h, layers[i], tb, fns, npad)
        saved.append(res)
    dh, lparts = _loss_head(h, loss_target[0], "loss_head")
    loss = lax.psum(jnp.sum(lparts[:, 0, 0]), ("x", "y", "c"))

    lg = [None] * depth
    for i in reversed(range(depth)):
        dh, gi = _layer_bwd(dh, saved[i], layers[i], tb, fns, npad)
        lg[i] = _layer_grads_to_reference_layout(gi)
    (dhcat,), (d_emb_g, d_emb_b) = _rw_bwd(fns["ln"], [_row(hcat)], consts_e, [_row(dh)], [F32], "emb_ln_bwd")
    grad_x = dhcat[LANES:][None]
    local = {n: jnp.stack([lg[i][n] for i in range(depth)]) for n in lg[0] if n not in BIG}
    local["meta_tokens"] = dhcat[npad:LANES]
    local["emb_ln_g"] = d_emb_g[0]
    local["emb_ln_b"] = d_emb_b[0]

    pieces = [jnp.stack([_grad_pieces(lg[i][n], BIG[n]) for i in range(depth)], axis=1).astype(BF16) for n in big_names]
    sm_names = small_names + REPLICATED
    sm_pieces = [_to_pieces(local[n], SMALL_SHARDED[n]) for n in small_names]
    sm_pieces += [jnp.broadcast_to(local[n].reshape(1, -1), (N_DEV, local[n].size)) for n in REPLICATED]
    ps, _ = _flatten(sm_pieces, SMALL_COLS, BF16_ROWS, lead=True)
    recv = _exchange_pieces(pieces + [ps], "grad_exchange")
    outs = {}
    kinds = ("grad", "delta", "new_m", "new_v")
    for n, r in zip(big_names, recv[:-1]):
        for kind, a in zip(kinds, _adamw(r, w[n], m[n], v[n], "adamw_" + n)):
            outs[kind + "_" + n] = a
    wf, offs = _flatten([w[n] for n in sm_names], SMALL_COLS, BF16_ROWS)
    mf, _ = _flatten([m[n] for n in sm_names], SMALL_COLS, BF16_ROWS)
    vf, _ = _flatten([v[n] for n in sm_names], SMALL_COLS, BF16_ROWS)
    shapes = [w[n].shape for n in sm_names]
    for kind, flat in zip(kinds, _adamw(recv[-1], wf, mf, vf, "adamw_small")):
        for n, a in zip(sm_names, _unflatten(flat, offs, shapes)):
            outs[kind + "_" + n] = a
    result = [loss, grad_x]
    for kind in ("grad", "delta", "new_m", "new_v"):
        result += [outs[kind + "_" + n] for n in WEIGHTS]
    return tuple(result)
```

```python
import functools

import jax
import jax.numpy as jnp
import numpy as np
from jax import lax
from jax.experimental import pallas as pl
from jax.experimental.pallas import tpu as pltpu

F32 = jnp.float32
BF16 = jnp.bfloat16
HIGHEST = lax.Precision.HIGHEST

D_MODEL = 1024
DEPTH = 2
N_META = 16
HEADS = 8
Q_LORA = 768
KV_LORA = 256
QK_NOPE = 128
QK_ROPE = 64
V_HEAD = 128
ROPE_THETA = 10000.0
SSD_INNER = 2048
SSD_HEAD_DIM = 64
SSD_HEADS = 32
SSD_GROUPS = 4
SSD_STATE = 128
SSD_CONV = 4
SSD_CONV_DIM = SSD_INNER + 2 * SSD_GROUPS * SSD_STATE
CHUNK = 128
D_FF = 2816
FFN_CONV = 3
LN_EPS = 1e-5
RMS_EPS = 1e-6
ALPHA = (2 * DEPTH) ** 0.25
IN_SIZES = (Q_LORA, KV_LORA, QK_ROPE, SSD_INNER, SSD_CONV_DIM, SSD_HEADS, D_MODEL, D_MODEL)
ATT_SCALE = (QK_NOPE + QK_ROPE) ** -0.5
NEG_INF = -1e30
ADAM_LR, ADAM_B1, ADAM_B2, ADAM_EPS, ADAM_WD, ADAM_STEP = 0.001, 0.9, 0.999, 1e-08, 0.01, 10

LANES = 128
SUBLANES = 8
VMEM_BYTES = 64 * 1024 * 1024
N_DEV = 8

OQ, OKV, OZ, OXBC, OGA, OGS, OKPE, ODT = 0, 768, 1024, 3072, 6144, 7168, 8192, 8320
IN_PAD = 8448
QHEAD = 256

ROW_TILE = 640
MM_COL_TILE = 1408
MM_K_TILE = 1408
ATT_TILE = 640
ATT_HEADS_PER_STEP = 2
BF16_ROWS = 16
ROW_BUDGET = 7 * 1024 * 1024
ADAM_ELEMS = 160 * 1024


def _pick(n, target, q=LANES):
    assert n % q == 0, (n, q)
    units = n // q
    best = q
    for d in range(1, units + 1):
        if units % d == 0 and d * q <= target:
            best = d * q
    return best


def _pick_rows(n, row_bytes):
    return _pick(n, max(BF16_ROWS, ROW_BUDGET // row_bytes), BF16_ROWS)


def _params(sem, est_bytes):
    limit = int(min(VMEM_BYTES - (6 << 20), max(32 << 20, 2 * est_bytes + (8 << 20))))
    return pltpu.CompilerParams(dimension_semantics=sem, vmem_limit_bytes=limit)


def _nbytes(shape, dtype):
    return int(np.prod(shape)) * jnp.dtype(dtype).itemsize


def _mm(a, b, out_dtype, name, ta=False, tb=False, add=None):
    assert not (ta and tb)
    if ta:
        K, M = a.shape
        tm = _pick(M, MM_COL_TILE)
        tk = _pick(K, ROW_TILE)
    else:
        M, K = a.shape
        tm = _pick(M, ROW_TILE)
        tk = _pick(K, MM_K_TILE)
    N, K2 = (b.shape if tb else b.shape[::-1])
    assert K == K2
    tn = _pick(N, MM_COL_TILE)
    nk = K // tk
    dn = (((0,), (0,)), ((), ())) if ta else ((((1,), (1,)), ((), ())) if tb else (((1,), (0,)), ((), ())))

    def body(*refs):
        if add is None:
            a_ref, b_ref, o_ref, acc = refs
        else:
            a_ref, b_ref, add_ref, o_ref, acc = refs
        k = pl.program_id(2)

        @pl.when(k == 0)
        def _():
            acc[...] = jnp.zeros_like(acc)

        acc[...] += lax.dot_general(a_ref[...].astype(BF16), b_ref[...].astype(BF16), dn,
                                    preferred_element_type=F32)

        @pl.when(k == nk - 1)
        def _():
            r = acc[...]
            if add is not None:
                r = r + add_ref[...].astype(F32)
            o_ref[...] = r.astype(out_dtype)

    if ta:
        a_spec = pl.BlockSpec((tk, tm), lambda i, j, k: (k, i))
    else:
        a_spec = pl.BlockSpec((tm, tk), lambda i, j, k: (i, k))
    b_spec = pl.BlockSpec((tn, tk), lambda i, j, k: (j, k)) if tb else pl.BlockSpec((tk, tn), lambda i, j, k: (k, j))
    in_specs = [a_spec, b_spec]
    args = [a, b]
    est = 2 * (tm * tk * a.dtype.itemsize + tk * tn * b.dtype.itemsize + tm * tn * 4) + tm * tn * 4
    if add is not None:
        in_specs.append(pl.BlockSpec((tm, tn), lambda i, j, k: (i, j)))
        args.append(add)
        est += 2 * tm * tn * 4
    return pl.pallas_call(
        body, name=name, grid=(M // tm, N // tn, nk), in_specs=in_specs,
        out_specs=pl.BlockSpec((tm, tn), lambda i, j, k: (i, j)),
        out_shape=jax.ShapeDtypeStruct((M, N), out_dtype),
        scratch_shapes=[pltpu.VMEM((tm, tn), F32)],
        compiler_params=_params(("parallel", "parallel", "arbitrary"), est),
    )(*args)


def _row(arr, bw=None, cb=0, grp=False, diff=True):
    return dict(arr=arr, bw=arr.shape[1] if bw is None else bw, cb=cb, grp=grp, diff=diff)


def _out(width, dtype, bw=None, grp=False):
    return dict(width=width, dtype=dtype, bw=width if bw is None else bw, grp=grp)


def _spec_rows(d, tm):
    return pl.BlockSpec((tm, d["bw"]), lambda g, i, cb=d["cb"], gr=d["grp"]: (i, cb + (g if gr else 0)))


def _spec_const(d):
    return pl.BlockSpec((d["arr"].shape[0], d["bw"]), lambda g, i, cb=d["cb"], gr=d["grp"]: (0, cb + (g if gr else 0)))


def _rw_fwd(fn, rows, consts, outs, name, ng=1):
    Tp = rows[0]["arr"].shape[0]
    tm = _pick_rows(Tp, 4 * (sum(d["bw"] for d in rows) + 2 * sum(o["bw"] for o in outs)))
    nr, ncst = len(rows), len(consts)

    def body(*refs):
        i = pl.program_id(1)
        rowidx = i * tm + lax.broadcasted_iota(jnp.int32, (tm, 1), 0)
        rv = [r[...].astype(F32) for r in refs[:nr]]
        cv = [c[...] for c in refs[nr:nr + ncst]]
        vals = fn(rowidx, *rv, *cv)
        for o, v in zip(refs[nr + ncst:], vals):
            o[...] = v.astype(o.dtype)

    est = sum(tm * d["bw"] * 4 for d in rows) + sum(tm * o["bw"] * 4 for o in outs)
    return pl.pallas_call(
        body, name=name, grid=(ng, Tp // tm),
        in_specs=[_spec_rows(d, tm) for d in rows] + [_spec_const(d) for d in consts],
        out_specs=[pl.BlockSpec((tm, o["bw"]), lambda g, i, gr=o["grp"]: (i, g if gr else 0)) for o in outs],
        out_shape=[jax.ShapeDtypeStruct((Tp, o["width"]), o["dtype"]) for o in outs],
        compiler_params=_params(("parallel", "parallel"), 3 * est),
    )(*[d["arr"] for d in rows], *[d["arr"] for d in consts])


def _rw_bwd(fn, rows, consts, cots, drow_dtypes, name, ng=1):
    Tp = rows[0]["arr"].shape[0]
    tm = _pick_rows(Tp, 4 * (3 * sum(d["bw"] for d in rows) + 2 * sum(d["bw"] for d in cots)))
    nr, ncst, nct = len(rows), len(consts), len(cots)
    drows = [k for k, d in enumerate(rows) if d["diff"]]
    dcsts = [k for k, d in enumerate(consts) if d["diff"]]
    for k in drows:
        assert rows[k]["grp"] or ng == 1

    def body(*refs):
        g = pl.program_id(0)
        i = pl.program_id(1)
        rowidx = i * tm + lax.broadcasted_iota(jnp.int32, (tm, 1), 0)
        rv = [r[...].astype(F32) for r in refs[:nr]]
        cv = [c[...] for c in refs[nr:nr + ncst]]
        ct = tuple(r[...].astype(F32) for r in refs[nr + ncst:nr + ncst + nct])
        orefs = refs[nr + ncst + nct:]

        def f(*dargs):
            rr, cc = list(rv), list(cv)
            for k, v in zip(drows, dargs[:len(drows)]):
                rr[k] = v
            for k, v in zip(dcsts, dargs[len(drows):]):
                cc[k] = v
            return tuple(fn(rowidx, *rr, *cc))

        _, vjp = jax.vjp(f, *[rv[k] for k in drows], *[cv[k] for k in dcsts])
        grads = vjp(ct)
        for o, v in zip(orefs[:len(drows)], grads[:len(drows)]):
            o[...] = v.astype(o.dtype)
        for k, o, v in zip(dcsts, orefs[len(drows):], grads[len(drows):]):
            first = (i == 0) if consts[k]["grp"] else ((i == 0) & (g == 0))

            @pl.when(first)
            def _(o=o, v=v):
                o[...] = v

            @pl.when(jnp.logical_not(first))
            def _(o=o, v=v):
                o[...] += v

    out_specs, out_shape = [], []
    for k, dt in zip(drows, drow_dtypes):
        d = rows[k]
        out_specs.append(pl.BlockSpec((tm, d["bw"]), lambda g, i, gr=d["grp"]: (i, g if gr else 0)))
        out_shape.append(jax.ShapeDtypeStruct((Tp, d["bw"] * (ng if d["grp"] else 1)), dt))
    for k in dcsts:
        d = consts[k]
        r = d["arr"].shape[0]
        out_specs.append(pl.BlockSpec((r, d["bw"]), lambda g, i, gr=d["grp"]: (0, g if gr else 0)))
        out_shape.append(jax.ShapeDtypeStruct((r, d["bw"] * (ng if d["grp"] else 1)), F32))
    est = sum(tm * d["bw"] * 4 for d in rows) * 2 + sum(tm * d["bw"] * 4 for d in cots)
    res = pl.pallas_call(
        body, name=name, grid=(ng, Tp // tm),
        in_specs=[_spec_rows(d, tm) for d in rows] + [_spec_const(d) for d in consts] + [_spec_rows(d, tm) for d in cots],
        out_specs=out_specs, out_shape=out_shape,
        compiler_params=_params(("arbitrary", "arbitrary"), 3 * est),
    )(*[d["arr"] for d in rows], *[d["arr"] for d in consts], *[d["arr"] for d in cots])
    return list(res[:len(drows)]), list(res[len(drows):])


def _sigmoid(x):
    return 1.0 / (1.0 + jnp.exp(-x))


def _silu(x):
    return x * _sigmoid(x)


def _softplus(x):
    return jnp.maximum(x, 0.0) + jnp.log(1.0 + jnp.exp(-jnp.abs(x)))


def _layer_norm(x, g, b):
    mu = jnp.mean(x, axis=-1, keepdims=True)
    xc = x - mu
    var = jnp.mean(xc * xc, axis=-1, keepdims=True)
    return xc * lax.rsqrt(var + LN_EPS) * g + b


def _rms_norm(x, g):
    return x * lax.rsqrt(jnp.mean(x * x, axis=-1, keepdims=True) + RMS_EPS) * g


def _rope(r, cos, sin, rot):
    return r * cos + jnp.dot(r, rot, precision=HIGHEST, preferred_element_type=F32) * sin


def _make_stage_fns(npad):
    def fn_ln_masked(rowidx, x, g, b):
        return (jnp.where(rowidx >= npad, _layer_norm(x, g, b), 0.0),)

    def fn_in_post(rowidx, ql, kvl, kpe, dtr, cos, sin, rot, qg, kvg, dtb):
        qn = _rms_norm(ql, qg)
        kvn = _rms_norm(kvl, kvg)
        kr = _rope(kpe, cos, sin, rot)
        lane = lax.broadcasted_iota(jnp.int32, (1, LANES), 1)
        dt = jnp.where((rowidx >= npad) & (lane < SSD_HEADS), _softplus(dtr + dtb), 0.0)
        return qn, kvn, jnp.concatenate([kr] * HEADS, axis=1), dt

    def fn_q_post(rowidx, q, cos, sin, rot):
        rr = _rope(q[:, QK_NOPE:], cos, sin, rot)
        return (jnp.concatenate([q[:, :QK_NOPE], rr], axis=1) * ATT_SCALE,)

    def fn_gated_norm(rowidx, y, xs, z, dskip, g):
        v = (y + xs * dskip) * _silu(z)
        return (v * lax.rsqrt(jnp.mean(v * v, axis=-1, keepdims=True) + RMS_EPS) * g,)

    def fn_mix(rowidx, ga, gs, ya, ys):
        return (_sigmoid(ga) * ya + _sigmoid(gs) * ys,)

    def fn_res_ln(rowidx, h, r, g, b):
        return (jnp.where(rowidx >= npad, _layer_norm(ALPHA * h + r, g, b), 0.0),)

    def fn_glu(rowidx, u):
        return (_silu(u[:, :D_FF]) * u[:, D_FF:],)

    def fn_delta(rowidx, do, o):
        s = jnp.sum(do * o, axis=-1, keepdims=True)
        return (jnp.broadcast_to(s, do.shape),)

    return dict(ln=fn_ln_masked, in_post=fn_in_post, q_post=fn_q_post, gated=fn_gated_norm, mix=fn_mix,
                res_ln=fn_res_ln, glu=fn_glu, delta=fn_delta)


def _conv_tiles(Tp, C):
    return _pick(Tp, ROW_TILE), _pick(C, MM_COL_TILE)


def _conv_fwd(x, xoff, C, w8, b, K, act, npad, name):
    Tp = x.shape[0]
    tm, tc = _conv_tiles(Tp, C)
    assert xoff % tc == 0
    cb0 = xoff // tc
    rb = tm // SUBLANES

    def body(prev_ref, main_ref, w_ref, b_ref, o_ref):
        i = pl.program_id(1)
        main = main_ref[...].astype(F32)
        prev = jnp.where(i > 0, prev_ref[...].astype(F32), 0.0)
        ext = jnp.concatenate([prev, main], axis=0)
        acc = b_ref[...] + w_ref[K - 1:K, :] * main
        for k in range(K - 1):
            s = K - 1 - k
            acc = acc + w_ref[k:k + 1, :] * pltpu.roll(ext, s, 0)[SUBLANES:, :]
        if act:
            rowidx = i * tm + lax.broadcasted_iota(jnp.int32, (tm, 1), 0)
            acc = jnp.where(rowidx >= npad, _silu(acc), 0.0)
        o_ref[...] = acc.astype(o_ref.dtype)

    return pl.pallas_call(
        body, name=name, grid=(C // tc, Tp // tm),
        in_specs=[pl.BlockSpec((SUBLANES, tc), lambda g, i: (jnp.maximum(i * rb - 1, 0), cb0 + g)),
                  pl.BlockSpec((tm, tc), lambda g, i: (i, cb0 + g)),
                  pl.BlockSpec((SUBLANES, tc), lambda g, i: (0, g)),
                  pl.BlockSpec((1, tc), lambda g, i: (0, g))],
        out_specs=pl.BlockSpec((tm, tc), lambda g, i: (i, g)),
        out_shape=jax.ShapeDtypeStruct((Tp, C), F32),
        compiler_params=_params(("parallel", "parallel"), 8 * tm * tc * 4),
    )(x, x, w8, b)


def _conv_bwd(x, xoff, C, w8, b, dy, K, act, npad, name):
    Tp = x.shape[0]
    tm, tc = _conv_tiles(Tp, C)
    cb0 = xoff // tc
    rb = tm // SUBLANES
    ni = Tp // tm
    last_rb = Tp // SUBLANES - 1
    n = tm + 2 * SUBLANES

    def body(xp_ref, xm_ref, xn_ref, dym_ref, dyn_ref, w_ref, b_ref, dx_ref, dw_ref, db_ref):
        i = pl.program_id(1)
        prev = jnp.where(i > 0, xp_ref[...].astype(F32), 0.0)
        ext = jnp.concatenate([prev, xm_ref[...].astype(F32), xn_ref[...].astype(F32)], axis=0)
        dyn = jnp.where(i < ni - 1, dyn_ref[...].astype(F32), 0.0)
        dpre = jnp.concatenate([jnp.zeros((SUBLANES, tc), F32), dym_ref[...].astype(F32), dyn], axis=0)
        shifted = [ext if k == K - 1 else pltpu.roll(ext, K - 1 - k, 0) for k in range(K)]
        if act:
            pre = b_ref[...] + sum(w_ref[k:k + 1, :] * shifted[k] for k in range(K))
            rowidx = i * tm - SUBLANES + lax.broadcasted_iota(jnp.int32, (n, 1), 0)
            sg = _sigmoid(pre)
            dpre = jnp.where(rowidx >= npad, dpre * sg * (1.0 + pre * (1.0 - sg)), 0.0)
        dx = w_ref[K - 1:K, :] * dpre
        for k in range(K - 1):
            dx = dx + w_ref[k:k + 1, :] * pltpu.roll(dpre, n - (K - 1 - k), 0)
        dx_ref[...] = dx[SUBLANES:SUBLANES + tm, :].astype(dx_ref.dtype)

        @pl.when(i == 0)
        def _():
            dw_ref[...] = jnp.zeros_like(dw_ref)
            db_ref[...] = jnp.zeros_like(db_ref)

        dmain = dpre[SUBLANES:SUBLANES + tm, :]
        for k in range(K):
            dw_ref[k:k + 1, :] += jnp.sum(dmain * shifted[k][SUBLANES:SUBLANES + tm, :], axis=0, keepdims=True)
        db_ref[...] += jnp.sum(dmain, axis=0, keepdims=True)

    return pl.pallas_call(
        body, name=name, grid=(C // tc, ni),
        in_specs=[pl.BlockSpec((SUBLANES, tc), lambda g, i: (jnp.maximum(i * rb - 1, 0), cb0 + g)),
                  pl.BlockSpec((tm, tc), lambda g, i: (i, cb0 + g)),
                  pl.BlockSpec((SUBLANES, tc), lambda g, i: (jnp.minimum((i + 1) * rb, last_rb), cb0 + g)),
                  pl.BlockSpec((tm, tc), lambda g, i: (i, g)),
                  pl.BlockSpec((SUBLANES, tc), lambda g, i: (jnp.minimum((i + 1) * rb, last_rb), g)),
                  pl.BlockSpec((SUBLANES, tc), lambda g, i: (0, g)),
                  pl.BlockSpec((1, tc), lambda g, i: (0, g))],
        out_specs=[pl.BlockSpec((tm, tc), lambda g, i: (i, g)),
                   pl.BlockSpec((SUBLANES, tc), lambda g, i: (0, g)),
                   pl.BlockSpec((1, tc), lambda g, i: (0, g))],
        out_shape=[jax.ShapeDtypeStruct((Tp, C), BF16), jax.ShapeDtypeStruct((SUBLANES, C), F32),
                   jax.ShapeDtypeStruct((1, C), F32)],
        compiler_params=_params(("parallel", "arbitrary"), 14 * tm * tc * 4),
    )(x, x, x, dy, dy, w8, b)


def _flash_fwd(q, kv, kr8, npad, name):
    Tp = q.shape[0]
    t = _pick(Tp, ATT_TILE)
    hp = ATT_HEADS_PER_STEP
    nb = Tp // t
    nt = (((1,), (1,)), ((), ()))

    def body(q_ref, kn_ref, kr_ref, v_ref, o_ref, lse_ref, m_sc, l_sc, acc_sc):
        qi = pl.program_id(1)
        ki = pl.program_id(2)

        @pl.when(ki == 0)
        def _():
            m_sc[...] = jnp.full_like(m_sc, NEG_INF)
            l_sc[...] = jnp.zeros_like(l_sc)
            acc_sc[...] = jnp.zeros_like(acc_sc)

        def step(masked):
            kr = kr_ref[...]
            if masked:
                row = qi * t + lax.broadcasted_iota(jnp.int32, (t, t), 0)
                col = ki * t + lax.broadcasted_iota(jnp.int32, (t, t), 1)
                visible = (col <= row) & (col >= npad)
            for hh in range(hp):
                k = jnp.concatenate([kn_ref[:, hh * QK_NOPE:(hh + 1) * QK_NOPE], kr], axis=1)
                s = lax.dot_general(q_ref[:, hh * QHEAD:(hh + 1) * QHEAD], k, nt, preferred_element_type=F32)
                if masked:
                    s = jnp.where(visible, s, NEG_INF)
                vs = slice(hh * V_HEAD, (hh + 1) * V_HEAD)
                m_prev = m_sc[hh]
                m_new = jnp.maximum(m_prev, jnp.max(s, axis=-1, keepdims=True))
                p = jnp.exp(s - m_new)
                a = jnp.exp(m_prev - m_new)
                l_sc[hh] = a * l_sc[hh] + jnp.sum(p, axis=-1, keepdims=True)
                acc_sc[:, vs] = a * acc_sc[:, vs] + jnp.dot(p.astype(BF16), v_ref[:, vs], preferred_element_type=F32)
                m_sc[hh] = m_new

        need_mask = (ki == qi) | (ki == 0)

        @pl.when((ki <= qi) & need_mask)
        def _():
            step(True)

        @pl.when((ki <= qi) & jnp.logical_not(need_mask))
        def _():
            step(False)

        @pl.when(ki == qi)
        def _():
            for hh in range(hp):
                vs = slice(hh * V_HEAD, (hh + 1) * V_HEAD)
                l = l_sc[hh]
                o_ref[:, vs] = (acc_sc[:, vs] / l).astype(o_ref.dtype)
                lse_ref[:, vs] = jnp.broadcast_to(m_sc[hh] + jnp.log(l), (t, LANES))

    kmin = lambda qi, ki: jnp.minimum(ki, qi)
    return pl.pallas_call(
        body, name=name, grid=(HEADS // hp, nb, nb),
        in_specs=[pl.BlockSpec((t, hp * QHEAD), lambda g, qi, ki: (qi, g)),
                  pl.BlockSpec((t, hp * QK_NOPE), lambda g, qi, ki: (kmin(qi, ki), g)),
                  pl.BlockSpec((t, LANES), lambda g, qi, ki: (kmin(qi, ki), 0)),
                  pl.BlockSpec((t, hp * V_HEAD), lambda g, qi, ki: (kmin(qi, ki), HEADS // hp + g))],
        out_specs=[pl.BlockSpec((t, hp * V_HEAD), lambda g, qi, ki: (qi, g)),
                   pl.BlockSpec((t, hp * LANES), lambda g, qi, ki: (qi, g))],
        out_shape=[jax.ShapeDtypeStruct((Tp, HEADS * V_HEAD), F32), jax.ShapeDtypeStruct((Tp, HEADS * LANES), F32)],
        scratch_shapes=[pltpu.VMEM((hp, t, 1), F32), pltpu.VMEM((hp, t, 1), F32), pltpu.VMEM((t, hp * V_HEAD), F32)],
        compiler_params=_params(("parallel", "parallel", "arbitrary"), 8 * hp * t * t * 4),
    )(q, kv, kr8, kv)


def _flash_bwd(q, kv, kr8, do, lse, delta, npad, name):
    Tp = q.shape[0]
    t = _pick(Tp, ATT_TILE)
    nb = Tp // t
    nt = (((1,), (1,)), ((), ()))
    tn = (((0,), (0,)), ((), ()))

    def body(q_ref, kn_ref, kr_ref, v_ref, do_ref, lse_ref, dl_ref, dq_ref, dkn_ref, dkr_ref, dv_ref, dk_sc, dv_sc):
        ki = pl.program_id(1)
        qi = pl.program_id(2)

        @pl.when(qi == 0)
        def _():
            dk_sc[...] = jnp.zeros_like(dk_sc)
            dv_sc[...] = jnp.zeros_like(dv_sc)

        def step(masked):
            qv = q_ref[...]
            k = jnp.concatenate([kn_ref[...], kr_ref[...]], axis=1)
            s = lax.dot_general(qv, k, nt, preferred_element_type=F32)
            if masked:
                row = qi * t + lax.broadcasted_iota(jnp.int32, (t, t), 0)
                col = ki * t + lax.broadcasted_iota(jnp.int32, (t, t), 1)
                s = jnp.where((col <= row) & (col >= npad), s, NEG_INF)
            p = jnp.exp(s - lse_ref[:, :1])
            dob = do_ref[...].astype(BF16)
            dv_sc[...] += lax.dot_general(p.astype(BF16), dob, tn, preferred_element_type=F32)
            dp = lax.dot_general(dob, v_ref[...], nt, preferred_element_type=F32)
            ds = (p * (dp - dl_ref[:, :1])).astype(BF16)
            dk_sc[...] += lax.dot_general(ds, qv, tn, preferred_element_type=F32)
            dqc = jnp.dot(ds, k, preferred_element_type=F32)
            rows = pl.ds(pl.multiple_of(qi * t, t), t)

            @pl.when(ki == 0)
            def _():
                dq_ref[rows, :] = dqc

            @pl.when(ki > 0)
            def _():
                dq_ref[rows, :] += dqc

        need_mask = (ki == qi) | (ki == 0)

        @pl.when((qi >= ki) & need_mask)
        def _():
            step(True)

        @pl.when((qi >= ki) & jnp.logical_not(need_mask))
        def _():
            step(False)

        @pl.when(qi == nb - 1)
        def _():
            dkn_ref[...] = dk_sc[:, :QK_NOPE].astype(dkn_ref.dtype)
            dkr_ref[...] = dk_sc[:, QK_NOPE:].astype(dkr_ref.dtype)
            dv_ref[...] = dv_sc[...].astype(dv_ref.dtype)

    qmap = lambda h, ki, qi: (jnp.maximum(qi, ki), h)
    kmap = lambda h, ki, qi: (ki, h)
    est = 2 * Tp * QHEAD * 4 + 8 * t * t * 4
    return pl.pallas_call(
        body, name=name, grid=(HEADS, nb, nb),
        in_specs=[pl.BlockSpec((t, QHEAD), qmap),
                  pl.BlockSpec((t, QK_NOPE), kmap),
                  pl.BlockSpec((t, LANES), kmap),
                  pl.BlockSpec((t, V_HEAD), lambda h, ki, qi: (ki, HEADS + h)),
                  pl.BlockSpec((t, V_HEAD), qmap),
                  pl.BlockSpec((t, LANES), qmap),
                  pl.BlockSpec((t, LANES), qmap)],
        out_specs=[pl.BlockSpec((Tp, QHEAD), lambda h, ki, qi: (0, h)),
                   pl.BlockSpec((t, QK_NOPE), kmap),
                   pl.BlockSpec((t, LANES), kmap),
                   pl.BlockSpec((t, V_HEAD), kmap)],
        out_shape=[jax.ShapeDtypeStruct((Tp, HEADS * QHEAD), F32),
                   jax.ShapeDtypeStruct((Tp, HEADS * QK_NOPE), BF16),
                   jax.ShapeDtypeStruct((Tp, HEADS * LANES), F32),
                   jax.ShapeDtypeStruct((Tp, HEADS * V_HEAD), BF16)],
        scratch_shapes=[pltpu.VMEM((t, QHEAD), F32), pltpu.VMEM((t, V_HEAD), F32)],
        compiler_params=_params(("parallel", "arbitrary", "arbitrary"), est),
    )(q, kv, kr8, kv, do, lse, delta)


GW = SSD_INNER // SSD_GROUPS
PAIRS_PER_GROUP = GW // LANES
XB = SSD_INNER // GW
NT_DIMS = (((1,), (1,)), ((), ()))
TN_DIMS = (((0,), (0,)), ((), ()))


def _ssd_common(xs_ref, dt_ref, alog_ref, e_ref):
    a_neg = -jnp.exp(alog_ref[...])
    dt = dt_ref[...]
    li = lax.broadcasted_iota(jnp.int32, (CHUNK, CHUNK), 0)
    si = lax.broadcasted_iota(jnp.int32, (CHUNK, CHUNK), 1)
    tril = li >= si
    tri = tril.astype(F32)
    acs = jnp.dot(tri, dt * a_neg, precision=HIGHEST, preferred_element_type=F32)
    e = e_ref[...]
    dte = jnp.dot(dt, e, precision=HIGHEST, preferred_element_type=F32)
    acse = jnp.dot(acs, e, precision=HIGHEST, preferred_element_type=F32)
    x = xs_ref[...] * dte
    alast = acse[CHUNK - 1:CHUNK, :]
    return dict(a_neg=a_neg, dt=dt, tril=tril, tri=tri, acs=acs, acs_t=acs.T, e=e, dte=dte, acse=acse, x=x,
                p_e=jnp.exp(acse), w_e=jnp.exp(alast - acse), dl_e=jnp.exp(alast), li=li, si=si)


def _decay(cm, head):
    col = cm["acs"][:, head:head + 1]
    row = cm["acs_t"][head:head + 1, :]
    return jnp.exp(jnp.where(cm["tril"], col - row, -jnp.inf))


def _ssd_fwd(xbc, dt, alog, e, name):
    Tp = xbc.shape[0]
    nc = Tp // CHUNK

    def body(xs_ref, b_ref, c_ref, dt_ref, alog_ref, e_ref, y_ref, st_ref, st_sc):
        @pl.when(pl.program_id(0) == 0)
        def _():
            st_sc[...] = jnp.zeros_like(st_sc)

        cm = _ssd_common(xs_ref, dt_ref, alog_ref, e_ref)
        st_ref[0] = st_sc[...]
        lane = lax.broadcasted_iota(jnp.int32, (CHUNK, LANES), 1)
        for g in range(SSD_GROUPS):
            gs = slice(g * GW, (g + 1) * GW)
            cg = c_ref[:, g * SSD_STATE:(g + 1) * SSD_STATE].astype(BF16)
            bg = b_ref[:, g * SSD_STATE:(g + 1) * SSD_STATE].astype(BF16)
            cb = lax.dot_general(cg, bg, NT_DIMS, preferred_element_type=F32)
            stg = st_sc[:, gs]
            yoff = jnp.dot(cg, stg.astype(BF16), preferred_element_type=F32) * cm["p_e"][:, gs]
            xg = cm["x"][:, gs]
            for jp in range(PAIRS_PER_GROUP):
                j = g * PAIRS_PER_GROUP + jp
                xp = xg[:, jp * LANES:(jp + 1) * LANES].astype(BF16)
                ys = []
                for head in (2 * j, 2 * j + 1):
                    m = (cb * _decay(cm, head)).astype(BF16)
                    ys.append(jnp.dot(m, xp, preferred_element_type=F32))
                y_ref[:, j * LANES:(j + 1) * LANES] = (jnp.where(lane < SSD_HEAD_DIM, ys[0], ys[1])
                                                       + yoff[:, jp * LANES:(jp + 1) * LANES])
            snew = lax.dot_general(bg, (cm["w_e"][:, gs] * xg).astype(BF16), TN_DIMS, preferred_element_type=F32)
            st_sc[:, gs] = cm["dl_e"][:, gs] * stg + snew

    return pl.pallas_call(
        body, name=name, grid=(nc,),
        in_specs=[pl.BlockSpec((CHUNK, SSD_INNER), lambda c: (c, 0)),
                  pl.BlockSpec((CHUNK, GW), lambda c: (c, XB)),
                  pl.BlockSpec((CHUNK, GW), lambda c: (c, XB + 1)),
                  pl.BlockSpec((CHUNK, LANES), lambda c: (c, 0)),
                  pl.BlockSpec((1, LANES), lambda c: (0, 0)),
                  pl.BlockSpec((LANES, SSD_INNER), lambda c: (0, 0))],
        out_specs=[pl.BlockSpec((CHUNK, SSD_INNER), lambda c: (c, 0)),
                   pl.BlockSpec((1, SSD_STATE, SSD_INNER), lambda c: (c, 0, 0))],
        out_shape=[jax.ShapeDtypeStruct((Tp, SSD_INNER), F32), jax.ShapeDtypeStruct((nc, SSD_STATE, SSD_INNER), F32)],
        scratch_shapes=[pltpu.VMEM((SSD_STATE, SSD_INNER), F32)],
        compiler_params=_params(("arbitrary",), 24 * CHUNK * SSD_INNER * 4),
    )(xbc, xbc, xbc, dt, alog, e)


def _ssd_bwd(xbc, dt, alog, e, dy, dxs_skip, states, name):
    Tp = xbc.shape[0]
    nc = Tp // CHUNK
    rev = lambda c: nc - 1 - c

    def body(xs_ref, b_ref, c_ref, dt_ref, alog_ref, e_ref, dy_ref, skip_ref, st_ref,
             dxbc_ref, ddt_ref, dalog_ref, dst_sc, dx_sc, t_sc, tw_sc):
        @pl.when(pl.program_id(0) == 0)
        def _():
            dst_sc[...] = jnp.zeros_like(dst_sc)
            dalog_ref[...] = jnp.zeros_like(dalog_ref)

        cm = _ssd_common(xs_ref, dt_ref, alog_ref, e_ref)
        lane = lax.broadcasted_iota(jnp.int32, (CHUNK, LANES), 1)
        dacs_col = jnp.zeros((CHUNK, LANES), F32)
        dacs_row = jnp.zeros((LANES, CHUNK), F32)
        t_last = []
        for g in range(SSD_GROUPS):
            gs = slice(g * GW, (g + 1) * GW)
            cg = c_ref[:, g * SSD_STATE:(g + 1) * SSD_STATE].astype(BF16)
            bg = b_ref[:, g * SSD_STATE:(g + 1) * SSD_STATE].astype(BF16)
            stg = st_ref[0, :, gs]
            stg_b = stg.astype(BF16)
            dstg = dst_sc[:, gs]
            dstg_b = dstg.astype(BF16)
            xg = cm["x"][:, gs]
            dyg = dy_ref[:, gs]
            zg = jnp.dot(cg, stg_b, preferred_element_type=F32)
            dzg = dyg * cm["p_e"][:, gs]
            dzg_b = dzg.astype(BF16)
            dcg = lax.dot_general(dzg_b, stg_b, NT_DIMS, preferred_element_type=F32)
            dst_in = lax.dot_general(cg, dzg_b, TN_DIMS, preferred_element_type=F32)
            dst_in = dst_in + cm["dl_e"][:, gs] * dstg
            t_last.append(jnp.sum(dstg * stg * cm["dl_e"][:, gs], axis=0, keepdims=True))
            weg = cm["w_e"][:, gs]
            dbg = lax.dot_general((weg * xg).astype(BF16), dstg_b, NT_DIMS, preferred_element_type=F32)
            gg = jnp.dot(bg, dstg_b, preferred_element_type=F32)
            dxg = weg * gg
            tw_sc[:, gs] = xg * dxg
            t_sc[:, gs] = dzg * zg - xg * dxg
            cb = lax.dot_general(cg, bg, NT_DIMS, preferred_element_type=F32)
            dcb = jnp.zeros((CHUNK, CHUNK), F32)
            for jp in range(PAIRS_PER_GROUP):
                j = g * PAIRS_PER_GROUP + jp
                ps = slice(jp * LANES, (jp + 1) * LANES)
                xp = xg[:, ps].astype(BF16)
                dyp = dyg[:, ps]
                dxp = dxg[:, ps]
                for half, head in enumerate((2 * j, 2 * j + 1)):
                    lam = _decay(cm, head)
                    m32 = cb * lam
                    sel = (lane < SSD_HEAD_DIM) if half == 0 else (lane >= SSD_HEAD_DIM)
                    dye = jnp.where(sel, dyp, 0.0).astype(BF16)
                    dm = lax.dot_general(dye, xp, NT_DIMS, preferred_element_type=F32)
                    w = dm * m32
                    dacs_col = dacs_col + jnp.where(cm["si"] == head, jnp.sum(w, axis=1, keepdims=True), 0.0)
                    dacs_row = dacs_row + jnp.where(cm["li"] == head, jnp.sum(w, axis=0, keepdims=True), 0.0)
                    dcb = dcb + dm * lam
                    dxp = dxp + lax.dot_general(m32.astype(BF16), dye, TN_DIMS, preferred_element_type=F32)
                dx_sc[:, j * LANES:(j + 1) * LANES] = dxp
            dcb_b = dcb.astype(BF16)
            dcg = dcg + jnp.dot(dcb_b, bg, preferred_element_type=F32)
            dbg = dbg + lax.dot_general(dcb_b, cg, TN_DIMS, preferred_element_type=F32)
            dst_sc[:, gs] = dst_in
            dxbc_ref[:, SSD_INNER + g * SSD_STATE:SSD_INNER + (g + 1) * SSD_STATE] = dbg
            dxbc_ref[:, SSD_INNER + GW + g * SSD_STATE:SSD_INNER + GW + (g + 1) * SSD_STATE] = dcg
        e = cm["e"]
        dacs = lax.dot_general(t_sc[...], e, NT_DIMS, precision=HIGHEST, preferred_element_type=F32)
        dacs = dacs + dacs_col - dacs_row.T
        last_lane = jnp.concatenate(t_last, axis=1) + jnp.sum(tw_sc[...], axis=0, keepdims=True)
        last_head = lax.dot_general(jnp.broadcast_to(last_lane, (SUBLANES, SSD_INNER)), e, NT_DIMS,
                                    precision=HIGHEST, preferred_element_type=F32)[0:1, :]
        dacs = dacs + jnp.where(cm["li"] == CHUNK - 1, last_head, 0.0)
        da = lax.dot_general(cm["tri"], dacs, TN_DIMS, precision=HIGHEST, preferred_element_type=F32)
        dx_all = dx_sc[...]
        ddt = da * cm["a_neg"] + lax.dot_general(dx_all * xs_ref[...], e, NT_DIMS, precision=HIGHEST,
                                                 preferred_element_type=F32)
        ddt_ref[...] = ddt
        dxbc_ref[:, :SSD_INNER] = dx_all * cm["dte"] + skip_ref[...]
        dalog_ref[0:1, :] += jnp.sum(da * cm["dt"], axis=0, keepdims=True) * cm["a_neg"]

    return pl.pallas_call(
        body, name=name, grid=(nc,),
        in_specs=[pl.BlockSpec((CHUNK, SSD_INNER), lambda c: (rev(c), 0)),
                  pl.BlockSpec((CHUNK, GW), lambda c: (rev(c), XB)),
                  pl.BlockSpec((CHUNK, GW), lambda c: (rev(c), XB + 1)),
                  pl.BlockSpec((CHUNK, LANES), lambda c: (rev(c), 0)),
                  pl.BlockSpec((1, LANES), lambda c: (0, 0)),
                  pl.BlockSpec((LANES, SSD_INNER), lambda c: (0, 0)),
                  pl.BlockSpec((CHUNK, SSD_INNER), lambda c: (rev(c), 0)),
                  pl.BlockSpec((CHUNK, SSD_INNER), lambda c: (rev(c), 0)),
                  pl.BlockSpec((1, SSD_STATE, SSD_INNER), lambda c: (rev(c), 0, 0))],
        out_specs=[pl.BlockSpec((CHUNK, SSD_CONV_DIM), lambda c: (rev(c), 0)),
                   pl.BlockSpec((CHUNK, LANES), lambda c: (rev(c), 0)),
                   pl.BlockSpec((SUBLANES, LANES), lambda c: (0, 0))],
        out_shape=[jax.ShapeDtypeStruct((Tp, SSD_CONV_DIM), F32), jax.ShapeDtypeStruct((Tp, LANES), F32),
                   jax.ShapeDtypeStruct((SUBLANES, LANES), F32)],
        scratch_shapes=[pltpu.VMEM((SSD_STATE, SSD_INNER), F32), pltpu.VMEM((CHUNK, SSD_INNER), F32),
                        pltpu.VMEM((CHUNK, SSD_INNER), F32), pltpu.VMEM((CHUNK, SSD_INNER), F32)],
        compiler_params=_params(("arbitrary",), 32 * CHUNK * SSD_INNER * 4),
    )(xbc, xbc, xbc, dt, alog, e, dy, dxs_skip, states)


def _loss_head(h, target, name):
    Tp, d = h.shape
    nt = Tp // LANES

    def body(h_ref, t_ref, dh_ref, l_ref):
        real = pl.program_id(0) > 0
        err = jnp.where(real, h_ref[...] - t_ref[...], 0.0)
        dh_ref[...] = err * (1.0 / d)
        l_ref[...] = jnp.broadcast_to(0.5 * jnp.sum(err * err) * (1.0 / d), l_ref.shape)

    return pl.pallas_call(
        body, name=name, grid=(nt,),
        in_specs=[pl.BlockSpec((LANES, d), lambda i: (i, 0)),
                  pl.BlockSpec((LANES, d), lambda i: (jnp.maximum(i - 1, 0), 0))],
        out_specs=[pl.BlockSpec((LANES, d), lambda i: (i, 0)),
                   pl.BlockSpec((1, SUBLANES, LANES), lambda i: (i, 0, 0))],
        out_shape=[jax.ShapeDtypeStruct((Tp, d), F32), jax.ShapeDtypeStruct((nt, SUBLANES, LANES), F32)],
        compiler_params=_params(("parallel",), 8 * LANES * d * 4),
    )(h, target)


def _adamw(parts, w, m, v, name):
    shape = w.shape
    C = shape[-1]
    R = int(np.prod(shape[:-1]))
    npart = parts.shape[0]
    parts, w, m, v = parts.reshape(npart, R, C), w.reshape(R, C), m.reshape(R, C), v.reshape(R, C)
    lanes = -(-C // LANES) * LANES
    tr = _pick(R, max(BF16_ROWS, ADAM_ELEMS // lanes), BF16_ROWS) if R % BF16_ROWS == 0 else R
    c1 = 1.0 / (1.0 - ADAM_B1 ** ADAM_STEP)
    c2 = 1.0 / (1.0 - ADAM_B2 ** ADAM_STEP)

    def body(p_ref, w_ref, m_ref, v_ref, g_out, d_out, m_out, v_out):
        g = p_ref[0].astype(F32)
        for p in range(1, npart):
            g = g + p_ref[p].astype(F32)
        m_new = ADAM_B1 * m_ref[...] + (1.0 - ADAM_B1) * g
        v_new = ADAM_B2 * v_ref[...] + (1.0 - ADAM_B2) * (g * g)
        g_out[...] = g
        m_out[...] = m_new
        v_out[...] = v_new
        d_out[...] = -ADAM_LR * ((m_new * c1) / (jnp.sqrt(v_new * c2) + ADAM_EPS) + ADAM_WD * w_ref[...])

    spec = pl.BlockSpec((tr, C), lambda i: (i, 0))
    est = npart * tr * lanes * parts.dtype.itemsize + 7 * tr * lanes * 4
    res = pl.pallas_call(
        body, name=name, grid=(R // tr,),
        in_specs=[pl.BlockSpec((npart, tr, C), lambda i: (0, i, 0)), spec, spec, spec],
        out_specs=[spec] * 4, out_shape=[jax.ShapeDtypeStruct((R, C), F32)] * 4,
        compiler_params=_params(("parallel",), est),
    )(parts, w, m, v)
    return [r.reshape(shape) for r in res]


MESH_ID = pl.DeviceIdType.MESH
N_PEERS = N_DEV - 1


def _dev_index(p):
    return 4 * p[0] + 2 * p[1] + p[2]


def _comm_call(body, name, arrs, out_shape, npairs):
    n = len(arrs)
    any_spec = pl.BlockSpec(memory_space=pl.ANY)
    return pl.pallas_call(
        functools.partial(body, n), name=name, in_specs=[any_spec] * n, out_specs=[any_spec] * n, out_shape=out_shape,
        scratch_shapes=[pltpu.SemaphoreType.DMA((n, npairs)), pltpu.SemaphoreType.DMA((n, npairs)),
                        pltpu.SemaphoreType.DMA((n,))],
    )(*arrs)


def _allgather(arrs, name):
    def body(n, *refs):
        src_refs, out_refs = refs[:n], refs[n:2 * n]
        send_sems, recv_sems, local_sems = refs[2 * n:]
        x, y, c = lax.axis_index("x"), lax.axis_index("y"), lax.axis_index("c")
        me, sibling = (x, y, c), (x, y, 1 - c)
        chips = [(1 - x, y), (x, 1 - y), (1 - x, 1 - y)]

        def copy(t, k, block, to, src=None):
            slot = out_refs[t].at[_dev_index(block)]
            return pltpu.make_async_remote_copy(
                src_ref=slot if src is None else src, dst_ref=slot,
                send_sem=send_sems.at[t, k], recv_sem=recv_sems.at[t, k],
                device_id=to, device_id_type=MESH_ID)

        sends, locals_ = [], []
        for t in range(n):
            mine = pltpu.make_async_copy(src_refs[t], out_refs[t].at[_dev_index(me)], local_sems.at[t])
            mine.start()
            locals_.append(mine)
            first = [copy(t, 0, me, sibling, src=src_refs[t])]
            first += [copy(t, 1 + j, me, (*chip, c), src=src_refs[t]) for j, chip in enumerate(chips)]
            for cp in first:
                cp.start()
            sends += first
        for j, chip in enumerate(chips):
            for t in range(n):
                copy(t, 1 + j, (*chip, c), me).wait_recv()
                passed = copy(t, 4 + j, (*chip, c), sibling)
                passed.start()
                sends.append(passed)
        for t in range(n):
            copy(t, 0, sibling, me).wait_recv()
            for j, chip in enumerate(chips):
                copy(t, 4 + j, (*chip, 1 - c), me).wait_recv()
        for cp in sends:
            cp.wait_send()
        for cp in locals_:
            cp.wait()

    return _comm_call(body, name, arrs, [jax.ShapeDtypeStruct((N_DEV,) + a.shape, a.dtype) for a in arrs], N_PEERS)


N_CHIPS = N_DEV // 2
CHIPS = [(0, 0), (0, 1), (1, 0), (1, 1)]


def _sibling_exchange(arrs, name):
    def body(n, *refs):
        in_refs, out_refs = refs[:n], refs[n:2 * n]
        send_sems, recv_sems, _ = refs[2 * n:]
        x, y, c = lax.axis_index("x"), lax.axis_index("y"), lax.axis_index("c")
        sibling = (x, y, 1 - c)

        def copy(t, j):
            return pltpu.make_async_remote_copy(
                src_ref=in_refs[t].at[_dev_index((*CHIPS[j], 1 - c))], dst_ref=out_refs[t].at[j],
                send_sem=send_sems.at[t, j], recv_sem=recv_sems.at[t, j],
                device_id=sibling, device_id_type=MESH_ID)

        copies = [copy(t, j) for t in range(n) for j in range(N_CHIPS)]
        for cp in copies:
            cp.start()
        for cp in copies:
            cp.wait_recv()
        for cp in copies:
            cp.wait_send()

    return _comm_call(body, name, arrs, [jax.ShapeDtypeStruct((N_CHIPS,) + a.shape[1:], a.dtype) for a in arrs], N_CHIPS)


def _chip_exchange(arrs, name):
    def body(n, *refs):
        in_refs, out_refs = refs[:n], refs[n:2 * n]
        send_sems, recv_sems, local_sems = refs[2 * n:]
        x, y, c = lax.axis_index("x"), lax.axis_index("y"), lax.axis_index("c")
        mine = 2 * x + y
        peers = [(1 - x, y), (x, 1 - y), (1 - x, 1 - y)]

        def copy(t, k, src_chip, dst_chip, to):
            return pltpu.make_async_remote_copy(
                src_ref=in_refs[t].at[src_chip], dst_ref=out_refs[t].at[dst_chip],
                send_sem=send_sems.at[t, k], recv_sem=recv_sems.at[t, k],
                device_id=(*to, c), device_id_type=MESH_ID)

        sends, locals_ = [], []
        for t in range(n):
            own = pltpu.make_async_copy(in_refs[t].at[mine], out_refs[t].at[mine], local_sems.at[t])
            own.start()
            locals_.append(own)
            for k, p in enumerate(peers):
                cp = copy(t, k, 2 * p[0] + p[1], mine, p)
                cp.start()
                sends.append(cp)
        for t in range(n):
            for k, p in enumerate(peers):
                copy(t, k, mine, 2 * p[0] + p[1], p).wait_recv()
        for cp in sends:
            cp.wait_send()
        for cp in locals_:
            cp.wait()

    return _comm_call(body, name, arrs, [jax.ShapeDtypeStruct(a.shape, a.dtype) for a in arrs], N_CHIPS - 1)


def _add_pairs(a, b, name):
    shape = a.shape
    C = shape[-1]
    R = int(np.prod(shape[:-1]))
    lanes = -(-C // LANES) * LANES
    tr = _pick(R, max(BF16_ROWS, 2 * ADAM_ELEMS // lanes), BF16_ROWS) if R % BF16_ROWS == 0 else R

    def body(a_ref, b_ref, o_ref):
        o_ref[...] = (a_ref[...].astype(F32) + b_ref[...].astype(F32)).astype(o_ref.dtype)

    spec = pl.BlockSpec((tr, C), lambda i: (i, 0))
    return pl.pallas_call(
        body, name=name, grid=(R // tr,), in_specs=[spec, spec], out_specs=spec,
        out_shape=jax.ShapeDtypeStruct((R, C), a.dtype),
        compiler_params=_params(("parallel",), 3 * tr * lanes * 4),
    )(a.reshape(R, C), b.reshape(R, C)).reshape(shape)


WEIGHTS = ['meta_tokens', 'emb_ln_g', 'emb_ln_b', 'w_in', 'q_norm_g', 'w_q_b', 'kv_norm_g', 'w_kv_b', 'w_o_attn',
           'ssd_conv_w', 'ssd_conv_b', 'dt_bias', 'a_log', 'd_skip', 'ssd_norm_g', 'w_o_ssd', 'w_out', 'ln1_g',
           'ln1_b', 'w_up', 'ffn_conv_w', 'ffn_conv_b', 'w_down', 'ln2_g', 'ln2_b']
BIG = {'w_in': 2, 'w_q_b': 2, 'w_kv_b': 2, 'w_o_attn': 1, 'w_o_ssd': 1, 'w_out': 1, 'w_up': 2, 'w_down': 1}
SMALL_SHARDED = {'meta_tokens': 1, 'ssd_conv_w': 2, 'ffn_conv_w': 2}
REPLICATED = [n for n in WEIGHTS if n not in BIG and n not in SMALL_SHARDED]
BIG_COLS = 1024
SMALL_COLS = LANES


def _flatten(arrs, cols, row_mult, lead=False):
    parts, offs, off = [], [], 0
    for a in arrs:
        a2 = a.reshape(N_DEV, -1) if lead else a.reshape(1, -1)
        n = a2.shape[1]
        pad = -n % cols
        parts.append(jnp.pad(a2, ((0, 0), (0, pad))))
        offs.append((off, n))
        off += n + pad
    rows = off // cols
    extra = (-rows % row_mult) * cols
    if extra:
        parts.append(jnp.zeros((parts[0].shape[0], extra), parts[0].dtype))
    flat = jnp.concatenate(parts, axis=1)
    flat = flat.reshape(flat.shape[0], -1, cols)
    return (flat if lead else flat[0]), offs


def _unflatten(flat, offs, shapes):
    f = flat.reshape(-1)
    return [f[o:o + n].reshape(s) for (o, n), s in zip(offs, shapes)]


def _to_pieces(g, axis):
    s = g.shape[axis] // N_DEV
    g = g.reshape(g.shape[:axis] + (N_DEV, s) + g.shape[axis + 1:])
    return jnp.moveaxis(g, axis, 0).reshape(N_DEV, -1)


def _from_pieces(p, shard_shape, axis):
    g = jnp.moveaxis(p.reshape((N_DEV,) + tuple(shard_shape)), 0, axis)
    sh = list(shard_shape)
    sh[axis] *= N_DEV
    return g.reshape(sh)


def _in_proj_pad(w):
    e = np.cumsum((0,) + IN_SIZES)
    ql, kvl, kpe, z, xbc, dt, ga, gs = [w[:, e[j]:e[j + 1]] for j in range(8)]
    zc = lambda n: jnp.zeros((w.shape[0], n), w.dtype)
    return jnp.concatenate([ql, kvl, z, xbc, ga, gs, kpe, zc(LANES - QK_ROPE), dt, zc(LANES - SSD_HEADS)], axis=1)


def _in_proj_unpad(d):
    seg = lambda o, n: d[:, o:o + n]
    return jnp.concatenate([seg(OQ, Q_LORA), seg(OKV, KV_LORA), seg(OKPE, QK_ROPE), seg(OZ, SSD_INNER),
                            seg(OXBC, SSD_CONV_DIM), seg(ODT, SSD_HEADS), seg(OGA, D_MODEL), seg(OGS, D_MODEL)], axis=1)


def _q_pad(w):
    w3 = w.reshape(Q_LORA, HEADS, QK_NOPE + QK_ROPE)
    return jnp.concatenate([w3, jnp.zeros((Q_LORA, HEADS, QHEAD - QK_NOPE - QK_ROPE), w.dtype)], axis=2).reshape(Q_LORA, HEADS * QHEAD)


def _q_unpad(d):
    return d.reshape(Q_LORA, HEADS, QHEAD)[:, :, :QK_NOPE + QK_ROPE].reshape(Q_LORA, HEADS * (QK_NOPE + QK_ROPE))


def _kv_perm(w):
    w3 = w.reshape(KV_LORA, HEADS, QK_NOPE + V_HEAD)
    return jnp.concatenate([w3[:, :, :QK_NOPE].reshape(KV_LORA, -1), w3[:, :, QK_NOPE:].reshape(KV_LORA, -1)], axis=1)


def _kv_unperm(d):
    kn = d[:, :HEADS * QK_NOPE].reshape(KV_LORA, HEADS, QK_NOPE)
    v = d[:, HEADS * QK_NOPE:].reshape(KV_LORA, HEADS, V_HEAD)
    return jnp.concatenate([kn, v], axis=2).reshape(KV_LORA, HEADS * (QK_NOPE + V_HEAD))


def _row_vec(v, width=None):
    v = v.reshape(1, -1).astype(F32)
    if width is not None and v.shape[1] < width:
        v = jnp.pad(v, ((0, 0), (0, width - v.shape[1])))
    return v


def _pad_rows8(w):
    return jnp.pad(w.astype(F32), ((0, SUBLANES - w.shape[0]), (0, 0)))


def _tables(Tp, npad):
    pos = jnp.maximum(jnp.arange(Tp, dtype=jnp.int32) - npad, 0).astype(F32)
    inv_freq = 1.0 / (ROPE_THETA ** (jnp.arange(0, QK_ROPE, 2, dtype=F32) / QK_ROPE))
    ang = pos[:, None] * inv_freq[None, :]
    ang = jnp.concatenate([ang, ang], axis=-1)
    zeros = jnp.zeros((Tp, LANES - QK_ROPE), F32)
    cos = jnp.concatenate([jnp.cos(ang), zeros], axis=1)
    sin = jnp.concatenate([jnp.sin(ang), zeros], axis=1)
    rot = np.zeros((LANES, LANES), np.float32)
    half = QK_ROPE // 2
    for i in range(half):
        rot[i + half, i] = -1.0
        rot[i, i + half] = 1.0
    expand = np.zeros((LANES, SSD_INNER), np.float32)
    for hd in range(SSD_HEADS):
        expand[hd, hd * SSD_HEAD_DIM:(hd + 1) * SSD_HEAD_DIM] = 1.0
    return cos, sin, jnp.asarray(rot), jnp.asarray(expand)


def _layer_rows(proj, tb):
    rows_a = [_row(proj, Q_LORA, OQ // Q_LORA), _row(proj, KV_LORA, OKV // KV_LORA), _row(proj, LANES, OKPE // LANES),
              _row(proj, LANES, ODT // LANES), _row(tb["cos"], diff=False), _row(tb["sin"], diff=False)]
    return rows_a


def _layer_fwd(h, P, tb, fns, npad):
    proj = _mm(h, P["w_in"], F32, "in_proj")
    rows_a = _layer_rows(proj, tb)
    consts_a = [_row(tb["rot"], diff=False), _row(P["q_norm_g"]), _row(P["kv_norm_g"]), _row(P["dt_bias"])]
    qn, kvn, kr8, dt = _rw_fwd(fns["in_post"], rows_a, consts_a,
                               [_out(Q_LORA, BF16), _out(KV_LORA, BF16), _out(HEADS * LANES, BF16), _out(LANES, F32)],
                               "in_post")
    q = _mm(qn, P["w_q"], F32, "q_proj")
    rows_q = [_row(q, QHEAD, 0, grp=True), _row(tb["cos"], diff=False), _row(tb["sin"], diff=False)]
    qr = _rw_fwd(fns["q_post"], rows_q, [_row(tb["rot"], diff=False)], [_out(HEADS * QHEAD, BF16, QHEAD, grp=True)],
                 "q_post", ng=HEADS)[0]
    kv = _mm(kvn, P["w_kv"], BF16, "kv_proj")
    o, lse = _flash_fwd(qr, kv, kr8, npad, "attn_fwd")
    ya = _mm(o, P["w_o_attn"], F32, "attn_out")
    xbc = _conv_fwd(proj, OXBC, SSD_CONV_DIM, P["ssd_conv_w"], P["ssd_conv_b"], SSD_CONV, True, npad, "ssd_conv")
    y, states = _ssd_fwd(xbc, dt, P["a_log"], tb["expand"], "ssd_fwd")
    rows_b = [_row(y, GW, 0, grp=True), _row(xbc, GW, 0, grp=True), _row(proj, GW, OZ // GW, grp=True)]
    consts_b = [_row(P["d_skip"], GW, 0, grp=True), _row(P["ssd_norm_g"], GW, 0, grp=True)]
    yn = _rw_fwd(fns["gated"], rows_b, consts_b, [_out(SSD_INNER, BF16, GW, grp=True)], "ssd_gate", ng=SSD_GROUPS)[0]
    ys = _mm(yn, P["w_o_ssd"], F32, "ssd_out")
    rows_c = [_row(proj, D_MODEL, OGA // D_MODEL), _row(proj, D_MODEL, OGS // D_MODEL), _row(ya), _row(ys)]
    mixed = _rw_fwd(fns["mix"], rows_c, [], [_out(D_MODEL, BF16)], "mix")[0]
    mo = _mm(mixed, P["w_out"], F32, "mix_out")
    consts_1 = [_row(P["ln1_g"]), _row(P["ln1_b"])]
    h1 = _rw_fwd(fns["res_ln"], [_row(h), _row(mo)], consts_1, [_out(D_MODEL, F32)], "ln1")[0]
    up = _mm(h1, P["w_up"], F32, "ffn_up")
    u = _conv_fwd(up, 0, 2 * D_FF, P["ffn_conv_w"], P["ffn_conv_b"], FFN_CONV, False, npad, "ffn_conv")
    act = _rw_fwd(fns["glu"], [_row(u)], [], [_out(D_FF, BF16)], "ffn_glu")[0]
    fo = _mm(act, P["w_down"], F32, "ffn_down")
    consts_2 = [_row(P["ln2_g"]), _row(P["ln2_b"])]
    h2 = _rw_fwd(fns["res_ln"], [_row(h1), _row(fo)], consts_2, [_out(D_MODEL, F32)], "ln2")[0]
    res = dict(h=h, proj=proj, qn=qn, kvn=kvn, kr8=kr8, dt=dt, q=q, qr=qr, kv=kv, o=o, lse=lse, ya=ya, xbc=xbc, y=y,
               states=states, yn=yn, ys=ys, mixed=mixed, mo=mo, h1=h1, up=up, u=u, act=act, fo=fo)
    return h2, res


def _layer_bwd(dh2, r, P, tb, fns, npad):
    g = {}
    consts_2 = [_row(P["ln2_g"]), _row(P["ln2_b"])]
    (dh1_a, dfo), (g["ln2_g"], g["ln2_b"]) = _rw_bwd(fns["res_ln"], [_row(r["h1"]), _row(r["fo"])], consts_2,
                                                     [_row(dh2)], [F32, BF16], "ln2_bwd")
    g["w_down"] = _mm(r["act"], dfo, BF16, "dw_down", ta=True)
    dact = _mm(dfo, P["w_down"], F32, "d_act", tb=True)
    (du,), _ = _rw_bwd(fns["glu"], [_row(r["u"])], [], [_row(dact)], [F32], "glu_bwd")
    dup, g["ffn_conv_w"], g["ffn_conv_b"] = _conv_bwd(r["up"], 0, 2 * D_FF, P["ffn_conv_w"], P["ffn_conv_b"], du,
                                                      FFN_CONV, False, npad, "ffn_conv_bwd")
    g["w_up"] = _mm(r["h1"], dup, BF16, "dw_up", ta=True)
    dh1 = _mm(dup, P["w_up"], F32, "d_h1", tb=True, add=dh1_a)
    consts_1 = [_row(P["ln1_g"]), _row(P["ln1_b"])]
    (dh_a, dmo), (g["ln1_g"], g["ln1_b"]) = _rw_bwd(fns["res_ln"], [_row(r["h"]), _row(r["mo"])], consts_1,
                                                    [_row(dh1)], [F32, BF16], "ln1_bwd")
    g["w_out"] = _mm(r["mixed"], dmo, BF16, "dw_out", ta=True)
    dmixed = _mm(dmo, P["w_out"], F32, "d_mixed", tb=True)
    proj = r["proj"]
    rows_c = [_row(proj, D_MODEL, OGA // D_MODEL), _row(proj, D_MODEL, OGS // D_MODEL), _row(r["ya"]), _row(r["ys"])]
    (dga, dgs, dya, dys), _ = _rw_bwd(fns["mix"], rows_c, [], [_row(dmixed)], [BF16] * 4, "mix_bwd")
    g["w_o_attn"] = _mm(r["o"], dya, BF16, "dw_o_attn", ta=True)
    do = _mm(dya, P["w_o_attn"], F32, "d_o", tb=True)
    g["w_o_ssd"] = _mm(r["yn"], dys, BF16, "dw_o_ssd", ta=True)
    dyn = _mm(dys, P["w_o_ssd"], F32, "d_yn", tb=True)
    rows_b = [_row(r["y"], GW, 0, grp=True), _row(r["xbc"], GW, 0, grp=True), _row(proj, GW, OZ // GW, grp=True)]
    consts_b = [_row(P["d_skip"], GW, 0, grp=True), _row(P["ssd_norm_g"], GW, 0, grp=True)]
    (dy, dxs_skip, dz), (g["d_skip"], g["ssd_norm_g"]) = _rw_bwd(
        fns["gated"], rows_b, consts_b, [_row(dyn, GW, 0, grp=True)], [F32, F32, BF16], "ssd_gate_bwd", ng=SSD_GROUPS)
    dxbc, ddt, g["a_log"] = _ssd_bwd(r["xbc"], r["dt"], P["a_log"], tb["expand"], dy, dxs_skip, r["states"], "ssd_bwd")
    dxbc_pre, g["ssd_conv_w"], g["ssd_conv_b"] = _conv_bwd(proj, OXBC, SSD_CONV_DIM, P["ssd_conv_w"], P["ssd_conv_b"],
                                                           dxbc, SSD_CONV, True, npad, "ssd_conv_bwd")
    delta = _rw_fwd(fns["delta"], [_row(do, V_HEAD, 0, grp=True), _row(r["o"], V_HEAD, 0, grp=True)], [],
                    [_out(HEADS * LANES, F32, LANES, grp=True)], "attn_delta", ng=HEADS)[0]
    dqr, dkn, dkr8, dv = _flash_bwd(r["qr"], r["kv"], r["kr8"], do, r["lse"], delta, npad, "attn_bwd")
    rows_q = [_row(r["q"], QHEAD, 0, grp=True), _row(tb["cos"], diff=False), _row(tb["sin"], diff=False)]
    (dq,), _ = _rw_bwd(fns["q_post"], rows_q, [_row(tb["rot"], diff=False)], [_row(dqr, QHEAD, 0, grp=True)], [BF16],
                       "q_post_bwd", ng=HEADS)
    g["w_q"] = _mm(r["qn"], dq, BF16, "dw_q", ta=True)
    dqn = _mm(dq, P["w_q"], F32, "d_qn", tb=True)
    dkv = jnp.concatenate([dkn, dv], axis=1)
    g["w_kv"] = _mm(r["kvn"], dkv, BF16, "dw_kv", ta=True)
    dkvn = _mm(dkv, P["w_kv"], F32, "d_kvn", tb=True)
    rows_a = _layer_rows(proj, tb)
    consts_a = [_row(tb["rot"], diff=False), _row(P["q_norm_g"]), _row(P["kv_norm_g"]), _row(P["dt_bias"])]
    (dql, dkvl, dkpe, ddtr), (g["q_norm_g"], g["kv_norm_g"], g["dt_bias"]) = _rw_bwd(
        fns["in_post"], rows_a, consts_a, [_row(dqn), _row(dkvn), _row(dkr8), _row(ddt)], [BF16] * 4, "in_post_bwd")
    dproj = jnp.concatenate([dql, dkvl, dz, dxbc_pre, dga, dgs, dkpe, ddtr], axis=1)
    g["w_in"] = _mm(r["h"], dproj, BF16, "dw_in", ta=True)
    dh = _mm(dproj, P["w_in"], F32, "d_h", tb=True, add=dh_a)
    return dh, g


def _full_weight(g, axis, i):
    if axis == 1:
        return g[:, i].reshape(-1, g.shape[-1])
    return jnp.concatenate([g[p, i] for p in range(N_DEV)], axis=1)


def _grad_pieces(d, axis):
    if axis == 1:
        return d.reshape(N_DEV, -1, d.shape[1])
    return jnp.transpose(d.reshape(d.shape[0], N_DEV, -1), (1, 0, 2))


def _layer_params(gathered, small, i):
    full = {n: _full_weight(gathered[n], BIG[n], i) for n in BIG}
    P = {}
    P["w_in"] = _in_proj_pad(full["w_in"])
    P["w_q"] = _q_pad(full["w_q_b"])
    P["w_kv"] = _kv_perm(full["w_kv_b"])
    for n in ("w_o_attn", "w_o_ssd", "w_out", "w_up", "w_down"):
        P[n] = full[n]
    P["q_norm_g"] = _row_vec(small["q_norm_g"][i])
    P["kv_norm_g"] = _row_vec(small["kv_norm_g"][i])
    P["dt_bias"] = _row_vec(small["dt_bias"][i], LANES)
    P["a_log"] = _row_vec(small["a_log"][i], LANES)
    P["d_skip"] = _row_vec(jnp.repeat(small["d_skip"][i], SSD_HEAD_DIM))
    P["ssd_norm_g"] = _row_vec(small["ssd_norm_g"][i])
    P["ssd_conv_w"] = _pad_rows8(small["ssd_conv_w"][i])
    P["ssd_conv_b"] = _row_vec(small["ssd_conv_b"][i])
    P["ffn_conv_w"] = _pad_rows8(small["ffn_conv_w"][i])
    P["ffn_conv_b"] = _row_vec(small["ffn_conv_b"][i])
    for n in ("ln1_g", "ln1_b", "ln2_g", "ln2_b"):
        P[n] = _row_vec(small[n][i])
    return P


def _layer_grads_to_reference_layout(g):
    out = {}
    out["w_in"] = _in_proj_unpad(g["w_in"])
    out["w_q_b"] = _q_unpad(g["w_q"])
    out["w_kv_b"] = _kv_unperm(g["w_kv"])
    for n in ("w_o_attn", "w_o_ssd", "w_out", "w_up", "w_down"):
        out[n] = g[n]
    out["q_norm_g"] = g["q_norm_g"][0]
    out["kv_norm_g"] = g["kv_norm_g"][0]
    out["dt_bias"] = g["dt_bias"][0, :SSD_HEADS]
    out["a_log"] = g["a_log"][0, :SSD_HEADS]
    out["d_skip"] = g["d_skip"].reshape(SSD_HEADS, SSD_HEAD_DIM).sum(axis=1)
    out["ssd_norm_g"] = g["ssd_norm_g"][0]
    out["ssd_conv_w"] = g["ssd_conv_w"][:SSD_CONV]
    out["ssd_conv_b"] = g["ssd_conv_b"][0]
    out["ffn_conv_w"] = g["ffn_conv_w"][:FFN_CONV]
    out["ffn_conv_b"] = g["ffn_conv_b"][0]
    for n in ("ln1_g", "ln1_b", "ln2_g", "ln2_b"):
        out[n] = g[n][0]
    return out


def kernel(x, meta_tokens, emb_ln_g, emb_ln_b, w_in, q_norm_g, w_q_b, kv_norm_g, w_kv_b, w_o_attn, ssd_conv_w, ssd_conv_b, dt_bias, a_log, d_skip, ssd_norm_g, w_o_ssd, w_out, ln1_g, ln1_b, w_up, ffn_conv_w, ffn_conv_b, w_down, ln2_g, ln2_b, loss_target, m_meta_tokens, m_emb_ln_g, m_emb_ln_b, m_w_in, m_q_norm_g, m_w_q_b, m_kv_norm_g, m_w_kv_b, m_w_o_attn, m_ssd_conv_w, m_ssd_conv_b, m_dt_bias, m_a_log, m_d_skip, m_ssd_norm_g, m_w_o_ssd, m_w_out, m_ln1_g, m_ln1_b, m_w_up, m_ffn_conv_w, m_ffn_conv_b, m_w_down, m_ln2_g, m_ln2_b, v_meta_tokens, v_emb_ln_g, v_emb_ln_b, v_w_in, v_q_norm_g, v_w_q_b, v_kv_norm_g, v_w_kv_b, v_w_o_attn, v_ssd_conv_w, v_ssd_conv_b, v_dt_bias, v_a_log, v_d_skip, v_ssd_norm_g, v_w_o_ssd, v_w_out, v_ln1_g, v_ln1_b, v_w_up, v_ffn_conv_w, v_ffn_conv_b, v_w_down, v_ln2_g, v_ln2_b):
    given = dict(locals())
    w = {n: given[n] for n in WEIGHTS}
    m = {n: given["m_" + n] for n in WEIGHTS}
    v = {n: given["v_" + n] for n in WEIGHTS}
    seq = x.shape[1]
    assert x.shape[0] == 1 and seq % LANES == 0
    npad = LANES - N_META
    Tp = npad + N_META + seq
    depth = w_in.shape[0]

    big_names, small_names = list(BIG), list(SMALL_SHARDED)
    ws, offs_s = _flatten([w[n] for n in small_names], SMALL_COLS, SUBLANES)
    got = _allgather([w[n].astype(BF16) for n in big_names] + [ws], "weight_allgather")
    gathered = dict(zip(big_names, got[:-1]))
    gsm = got[-1]
    small = {n: w[n] for n in REPLICATED}
    for n, (o, sz) in zip(small_names, offs_s):
        small[n] = _from_pieces(gsm.reshape(N_DEV, -1)[:, o:o + sz], w[n].shape, SMALL_SHARDED[n])

    fns = _make_stage_fns(npad)
    cos, sin, rot, expand = _tables(Tp, npad)
    tb = dict(cos=cos, sin=sin, rot=rot, expand=expand)
    top = jnp.pad(small["meta_tokens"], ((npad, 0), (0, 0)))
    hcat = jnp.concatenate([top, x[0]], axis=0)
    consts_e = [_row(_row_vec(w["emb_ln_g"])), _row(_row_vec(w["emb_ln_b"]))]
    h = _rw_fwd(fns["ln"], [_row(hcat)], consts_e, [_out(D_MODEL, F32)], "emb_ln")[0]
    layers = [_layer_params(gathered, small, i) for i in range(depth)]
    saved = []
    for i in range(depth):
        h, res = _layer_fwd(h, layers[i], tb, fns, npad)
        saved.append(res)
    dh, lparts = _loss_head(h, loss_target[0], "loss_head")
    loss = lax.psum(jnp.sum(lparts[:, 0, 0]), ("x", "y", "c"))

    lg = [None] * depth
    for i in reversed(range(depth)):
        dh, gi = _layer_bwd(dh, saved[i], layers[i], tb, fns, npad)
        lg[i] = _layer_grads_to_reference_layout(gi)
    (dhcat,), (d_emb_g, d_emb_b) = _rw_bwd(fns["ln"], [_row(hcat)], consts_e, [_row(dh)], [F32], "emb_ln_bwd")
    grad_x = dhcat[LANES:][None]
    local = {n: jnp.stack([lg[i][n] for i in range(depth)]) for n in lg[0] if n not in BIG}
    local["meta_tokens"] = dhcat[npad:LANES]
    local["emb_ln_g"] = d_emb_g[0]
    local["emb_ln_b"] = d_emb_b[0]

    pieces = [jnp.stack([_grad_pieces(lg[i][n], BIG[n]) for i in range(depth)], axis=1).astype(BF16) for n in big_names]
    sm_names = small_names + REPLICATED
    sm_pieces = [_to_pieces(local[n], SMALL_SHARDED[n]) for n in small_names]
    sm_pieces += [jnp.broadcast_to(local[n].reshape(1, -1), (N_DEV, local[n].size)) for n in REPLICATED]
    ps, _ = _flatten(sm_pieces, SMALL_COLS, BF16_ROWS, lead=True)
    pieces = pieces + [ps]
    core = lax.axis_index("c")
    from_sibling = _sibling_exchange(pieces, "grad_exchange_cores")
    chip_sums = []
    for k, (p, r) in enumerate(zip(pieces, from_sibling)):
        own = lax.dynamic_index_in_dim(p.reshape((N_CHIPS, 2) + p.shape[1:]), core, axis=1, keepdims=False)
        chip_sums.append(_add_pairs(own, r, "grad_chip_sum_%d" % k))
    recv = _chip_exchange(chip_sums, "grad_exchange_chips")
    outs = {}
    kinds = ("grad", "delta", "new_m", "new_v")
    for n, r in zip(big_names, recv[:-1]):
        for kind, a in zip(kinds, _adamw(r, w[n], m[n], v[n], "adamw_" + n)):
            outs[kind + "_" + n] = a
    wf, offs = _flatten([w[n] for n in sm_names], SMALL_COLS, BF16_ROWS)
    mf, _ = _flatten([m[n] for n in sm_names], SMALL_COLS, BF16_ROWS)
    vf, _ = _flatten([v[n] for n in sm_names], SMALL_COLS, BF16_ROWS)
    shapes = [w[n].shape for n in sm_names]
    for kind, flat in zip(kinds, _adamw(recv[-1], wf, mf, vf, "adamw_small")):
        for n, a in zip(sm_names, _unflatten(flat, offs, shapes)):
            outs[kind + "_" + n] = a
    result = [loss, grad_x]
    for kind in ("grad", "delta", "new_m", "new_v"):
        result += [outs[kind + "_" + n] for n in WEIGHTS]
    return tuple(result)
```

```python
import functools

import jax
import jax.numpy as jnp
import numpy as np
from jax import lax
from jax.experimental import pallas as pl
from jax.experimental.pallas import tpu as pltpu

F32 = jnp.float32
BF16 = jnp.bfloat16
HIGHEST = lax.Precision.HIGHEST
SSD_PREC = lax.Precision.HIGH

D_MODEL = 1024
DEPTH = 2
N_META = 16
HEADS = 8
Q_LORA = 768
KV_LORA = 256
QK_NOPE = 128
QK_ROPE = 64
V_HEAD = 128
ROPE_THETA = 10000.0
SSD_INNER = 2048
SSD_HEAD_DIM = 64
SSD_HEADS = 32
SSD_GROUPS = 4
SSD_STATE = 128
SSD_CONV = 4
SSD_CONV_DIM = SSD_INNER + 2 * SSD_GROUPS * SSD_STATE
CHUNK = 128
D_FF = 2816
FFN_CONV = 3
LN_EPS = 1e-5
RMS_EPS = 1e-6
ALPHA = (2 * DEPTH) ** 0.25
IN_SIZES = (Q_LORA, KV_LORA, QK_ROPE, SSD_INNER, SSD_CONV_DIM, SSD_HEADS, D_MODEL, D_MODEL)
ATT_SCALE = (QK_NOPE + QK_ROPE) ** -0.5
NEG_INF = -1e30
ADAM_LR, ADAM_B1, ADAM_B2, ADAM_EPS, ADAM_WD, ADAM_STEP = 0.001, 0.9, 0.999, 1e-08, 0.01, 10

LANES = 128
SUBLANES = 8
VMEM_BYTES = 64 * 1024 * 1024
N_DEV = 8

OQ, OKV, OZ, OXBC, OGA, OGS, OKPE, ODT = 0, 768, 1024, 3072, 6144, 7168, 8192, 8320
IN_PAD = 8448
QHEAD = 256

ROW_TILE = 640
MM_COL_TILE = 1408
MM_K_TILE = 2816
MM_TOKEN_K_TILE = 1664
ATT_TILE = 640
ATT_HEADS_PER_STEP = 4
BF16_ROWS = 16
ROW_BUDGET = 7 * 1024 * 1024
ADAM_ELEMS = 160 * 1024


def _pick(n, target, q=LANES):
    assert n % q == 0, (n, q)
    units = n // q
    best = q
    for d in range(1, units + 1):
        if units % d == 0 and d * q <= target:
            best = d * q
    return best


def _pick_rows(n, row_bytes):
    return _pick(n, max(BF16_ROWS, ROW_BUDGET // row_bytes), BF16_ROWS)


def _params(sem, est_bytes):
    limit = int(min(VMEM_BYTES - (6 << 20), max(32 << 20, 2 * est_bytes + (8 << 20))))
    return pltpu.CompilerParams(dimension_semantics=sem, vmem_limit_bytes=limit)


def _nbytes(shape, dtype):
    return int(np.prod(shape)) * jnp.dtype(dtype).itemsize


def _mm(a, b, out_dtype, name, ta=False, tb=False, add=None):
    assert not (ta and tb)
    if ta:
        K, M = a.shape
        tm = _pick(M, MM_COL_TILE)
        tk = _pick(K, MM_TOKEN_K_TILE)
    else:
        M, K = a.shape
        tm = _pick(M, ROW_TILE)
        tk = _pick(K, MM_K_TILE)
    N, K2 = (b.shape if tb else b.shape[::-1])
    assert K == K2
    tn = _pick(N, MM_COL_TILE)
    nk = K // tk
    dn = (((0,), (0,)), ((), ())) if ta else ((((1,), (1,)), ((), ())) if tb else (((1,), (0,)), ((), ())))

    def body(*refs):
        a_ref, b_ref = refs[:2]
        add_ref = refs[2] if add is not None else None
        o_ref = refs[2 + (add is not None)]
        d = lax.dot_general(a_ref[...].astype(BF16), b_ref[...].astype(BF16), dn, preferred_element_type=F32)

        def finish(r):
            if add is not None:
                r = r + add_ref[...].astype(F32)
            o_ref[...] = r.astype(out_dtype)

        if nk == 1:
            finish(d)
            return
        acc = refs[-1]
        k = pl.program_id(2)

        @pl.when(k == 0)
        def _():
            acc[...] = d

        @pl.when((k > 0) & (k < nk - 1))
        def _():
            acc[...] += d

        @pl.when(k == nk - 1)
        def _():
            finish(acc[...] + d)

    if ta:
        a_spec = pl.BlockSpec((tk, tm), lambda i, j, k: (k, i))
    else:
        a_spec = pl.BlockSpec((tm, tk), lambda i, j, k: (i, k))
    b_spec = pl.BlockSpec((tn, tk), lambda i, j, k: (j, k)) if tb else pl.BlockSpec((tk, tn), lambda i, j, k: (k, j))
    in_specs = [a_spec, b_spec]
    args = [a, b]
    est = 2 * (tm * tk * a.dtype.itemsize + tk * tn * b.dtype.itemsize + tm * tn * 4) + tm * tn * 4
    if add is not None:
        in_specs.append(pl.BlockSpec((tm, tn), lambda i, j, k: (i, j)))
        args.append(add)
        est += 2 * tm * tn * 4
    return pl.pallas_call(
        body, name=name, grid=(M // tm, N // tn, nk), in_specs=in_specs,
        out_specs=pl.BlockSpec((tm, tn), lambda i, j, k: (i, j)),
        out_shape=jax.ShapeDtypeStruct((M, N), out_dtype),
        scratch_shapes=[pltpu.VMEM((tm, tn), F32)] if nk > 1 else [],
        compiler_params=_params(("parallel", "parallel", "arbitrary"), est),
    )(*args)


def _row(arr, bw=None, cb=0, grp=False, diff=True):
    return dict(arr=arr, bw=arr.shape[1] if bw is None else bw, cb=cb, grp=grp, diff=diff)


def _out(width, dtype, bw=None, grp=False):
    return dict(width=width, dtype=dtype, bw=width if bw is None else bw, grp=grp)


def _spec_rows(d, tm):
    return pl.BlockSpec((tm, d["bw"]), lambda g, i, cb=d["cb"], gr=d["grp"]: (i, cb + (g if gr else 0)))


def _spec_const(d):
    return pl.BlockSpec((d["arr"].shape[0], d["bw"]), lambda g, i, cb=d["cb"], gr=d["grp"]: (0, cb + (g if gr else 0)))


def _rw_fwd(fn, rows, consts, outs, name, ng=1):
    Tp = rows[0]["arr"].shape[0]
    tm = _pick_rows(Tp, 4 * (sum(d["bw"] for d in rows) + 2 * sum(o["bw"] for o in outs)))
    nr, ncst = len(rows), len(consts)

    def body(*refs):
        i = pl.program_id(1)
        rowidx = i * tm + lax.broadcasted_iota(jnp.int32, (tm, 1), 0)
        rv = [r[...].astype(F32) for r in refs[:nr]]
        cv = [c[...] for c in refs[nr:nr + ncst]]
        vals = fn(rowidx, *rv, *cv)
        for o, v in zip(refs[nr + ncst:], vals):
            o[...] = v.astype(o.dtype)

    est = sum(tm * d["bw"] * 4 for d in rows) + sum(tm * o["bw"] * 4 for o in outs)
    return pl.pallas_call(
        body, name=name, grid=(ng, Tp // tm),
        in_specs=[_spec_rows(d, tm) for d in rows] + [_spec_const(d) for d in consts],
        out_specs=[pl.BlockSpec((tm, o["bw"]), lambda g, i, gr=o["grp"]: (i, g if gr else 0)) for o in outs],
        out_shape=[jax.ShapeDtypeStruct((Tp, o["width"]), o["dtype"]) for o in outs],
        compiler_params=_params(("parallel", "parallel"), 3 * est),
    )(*[d["arr"] for d in rows], *[d["arr"] for d in consts])


def _rw_bwd(fn, rows, consts, cots, drow_dtypes, name, ng=1):
    Tp = rows[0]["arr"].shape[0]
    tm = _pick_rows(Tp, 4 * (3 * sum(d["bw"] for d in rows) + 2 * sum(d["bw"] for d in cots)))
    nr, ncst, nct = len(rows), len(consts), len(cots)
    drows = [k for k, d in enumerate(rows) if d["diff"]]
    dcsts = [k for k, d in enumerate(consts) if d["diff"]]
    for k in drows:
        assert rows[k]["grp"] or ng == 1

    def body(*refs):
        g = pl.program_id(0)
        i = pl.program_id(1)
        rowidx = i * tm + lax.broadcasted_iota(jnp.int32, (tm, 1), 0)
        rv = [r[...].astype(F32) for r in refs[:nr]]
        cv = [c[...] for c in refs[nr:nr + ncst]]
        ct = tuple(r[...].astype(F32) for r in refs[nr + ncst:nr + ncst + nct])
        orefs = refs[nr + ncst + nct:]

        def f(*dargs):
            rr, cc = list(rv), list(cv)
            for k, v in zip(drows, dargs[:len(drows)]):
                rr[k] = v
            for k, v in zip(dcsts, dargs[len(drows):]):
                cc[k] = v
            return tuple(fn(rowidx, *rr, *cc))

        _, vjp = jax.vjp(f, *[rv[k] for k in drows], *[cv[k] for k in dcsts])
        grads = vjp(ct)
        for o, v in zip(orefs[:len(drows)], grads[:len(drows)]):
            o[...] = v.astype(o.dtype)
        for k, o, v in zip(dcsts, orefs[len(drows):], grads[len(drows):]):
            first = (i == 0) if consts[k]["grp"] else ((i == 0) & (g == 0))

            @pl.when(first)
            def _(o=o, v=v):
                o[...] = v

            @pl.when(jnp.logical_not(first))
            def _(o=o, v=v):
                o[...] += v

    out_specs, out_shape = [], []
    for k, dt in zip(drows, drow_dtypes):
        d = rows[k]
        out_specs.append(pl.BlockSpec((tm, d["bw"]), lambda g, i, gr=d["grp"]: (i, g if gr else 0)))
        out_shape.append(jax.ShapeDtypeStruct((Tp, d["bw"] * (ng if d["grp"] else 1)), dt))
    for k in dcsts:
        d = consts[k]
        r = d["arr"].shape[0]
        out_specs.append(pl.BlockSpec((r, d["bw"]), lambda g, i, gr=d["grp"]: (0, g if gr else 0)))
        out_shape.append(jax.ShapeDtypeStruct((r, d["bw"] * (ng if d["grp"] else 1)), F32))
    est = sum(tm * d["bw"] * 4 for d in rows) * 2 + sum(tm * d["bw"] * 4 for d in cots)
    res = pl.pallas_call(
        body, name=name, grid=(ng, Tp // tm),
        in_specs=[_spec_rows(d, tm) for d in rows] + [_spec_const(d) for d in consts] + [_spec_rows(d, tm) for d in cots],
        out_specs=out_specs, out_shape=out_shape,
        compiler_params=_params(("arbitrary", "arbitrary"), 3 * est),
    )(*[d["arr"] for d in rows], *[d["arr"] for d in consts], *[d["arr"] for d in cots])
    return list(res[:len(drows)]), list(res[len(drows):])


def _sigmoid(x):
    return 1.0 / (1.0 + jnp.exp(-x))


def _silu(x):
    return x * _sigmoid(x)


def _softplus(x):
    return jnp.maximum(x, 0.0) + jnp.log(1.0 + jnp.exp(-jnp.abs(x)))


def _layer_norm(x, g, b):
    mu = jnp.mean(x, axis=-1, keepdims=True)
    xc = x - mu
    var = jnp.mean(xc * xc, axis=-1, keepdims=True)
    return xc * lax.rsqrt(var + LN_EPS) * g + b


def _rms_norm(x, g):
    return x * lax.rsqrt(jnp.mean(x * x, axis=-1, keepdims=True) + RMS_EPS) * g


def _rope(r, cos, sin, rot):
    return r * cos + jnp.dot(r, rot, precision=HIGHEST, preferred_element_type=F32) * sin


def _make_stage_fns(npad):
    def fn_ln_masked(rowidx, x, g, b):
        return (jnp.where(rowidx >= npad, _layer_norm(x, g, b), 0.0),)

    def fn_in_post(rowidx, ql, kvl, kpe, dtr, cos, sin, rot, qg, kvg, dtb):
        qn = _rms_norm(ql, qg)
        kvn = _rms_norm(kvl, kvg)
        kr = _rope(kpe, cos, sin, rot)
        lane = lax.broadcasted_iota(jnp.int32, (1, LANES), 1)
        dt = jnp.where((rowidx >= npad) & (lane < SSD_HEADS), _softplus(dtr + dtb), 0.0)
        return qn, kvn, jnp.concatenate([kr] * HEADS, axis=1), dt

    def fn_q_post(rowidx, q, cos, sin, rot):
        rr = _rope(q[:, QK_NOPE:], cos, sin, rot)
        return (jnp.concatenate([q[:, :QK_NOPE], rr], axis=1) * ATT_SCALE,)

    def fn_gated_norm(rowidx, y, xs, z, dskip, g):
        v = (y + xs * dskip) * _silu(z)
        return (v * lax.rsqrt(jnp.mean(v * v, axis=-1, keepdims=True) + RMS_EPS) * g,)

    def fn_mix(rowidx, ga, gs, ya, ys):
        return (_sigmoid(ga) * ya + _sigmoid(gs) * ys,)

    def fn_res_ln(rowidx, h, r, g, b):
        return (jnp.where(rowidx >= npad, _layer_norm(ALPHA * h + r, g, b), 0.0),)

    def fn_glu(rowidx, u):
        return (_silu(u[:, :D_FF]) * u[:, D_FF:],)

    return dict(ln=fn_ln_masked, in_post=fn_in_post, q_post=fn_q_post, gated=fn_gated_norm, mix=fn_mix,
                res_ln=fn_res_ln, glu=fn_glu)


def _conv_tiles(Tp, C):
    return _pick(Tp, ROW_TILE), _pick(C, MM_COL_TILE)


def _conv_fwd(x, xoff, C, w8, b, K, act, npad, name):
    Tp = x.shape[0]
    tm, tc = _conv_tiles(Tp, C)
    assert xoff % tc == 0
    cb0 = xoff // tc
    rb = tm // SUBLANES

    def body(prev_ref, main_ref, w_ref, b_ref, o_ref):
        i = pl.program_id(1)
        main = main_ref[...].astype(F32)
        prev = jnp.where(i > 0, prev_ref[...].astype(F32), 0.0)
        ext = jnp.concatenate([prev, main], axis=0)
        acc = b_ref[...] + w_ref[K - 1:K, :] * main
        for k in range(K - 1):
            s = K - 1 - k
            acc = acc + w_ref[k:k + 1, :] * pltpu.roll(ext, s, 0)[SUBLANES:, :]
        if act:
            rowidx = i * tm + lax.broadcasted_iota(jnp.int32, (tm, 1), 0)
            acc = jnp.where(rowidx >= npad, _silu(acc), 0.0)
        o_ref[...] = acc.astype(o_ref.dtype)

    return pl.pallas_call(
        body, name=name, grid=(C // tc, Tp // tm),
        in_specs=[pl.BlockSpec((SUBLANES, tc), lambda g, i: (jnp.maximum(i * rb - 1, 0), cb0 + g)),
                  pl.BlockSpec((tm, tc), lambda g, i: (i, cb0 + g)),
                  pl.BlockSpec((SUBLANES, tc), lambda g, i: (0, g)),
                  pl.BlockSpec((1, tc), lambda g, i: (0, g))],
        out_specs=pl.BlockSpec((tm, tc), lambda g, i: (i, g)),
        out_shape=jax.ShapeDtypeStruct((Tp, C), F32),
        compiler_params=_params(("parallel", "parallel"), 8 * tm * tc * 4),
    )(x, x, w8, b)


def _conv_bwd(x, xoff, C, w8, b, dy, K, act, npad, name):
    Tp = x.shape[0]
    tm, tc = _conv_tiles(Tp, C)
    cb0 = xoff // tc
    rb = tm // SUBLANES
    ni = Tp // tm
    last_rb = Tp // SUBLANES - 1
    n = tm + 2 * SUBLANES

    def body(xp_ref, xm_ref, xn_ref, dym_ref, dyn_ref, w_ref, b_ref, dx_ref, dw_ref, db_ref):
        i = pl.program_id(1)
        prev = jnp.where(i > 0, xp_ref[...].astype(F32), 0.0)
        ext = jnp.concatenate([prev, xm_ref[...].astype(F32), xn_ref[...].astype(F32)], axis=0)
        dyn = jnp.where(i < ni - 1, dyn_ref[...].astype(F32), 0.0)
        dpre = jnp.concatenate([jnp.zeros((SUBLANES, tc), F32), dym_ref[...].astype(F32), dyn], axis=0)
        shifted = [ext if k == K - 1 else pltpu.roll(ext, K - 1 - k, 0) for k in range(K)]
        if act:
            pre = b_ref[...] + sum(w_ref[k:k + 1, :] * shifted[k] for k in range(K))
            rowidx = i * tm - SUBLANES + lax.broadcasted_iota(jnp.int32, (n, 1), 0)
            sg = _sigmoid(pre)
            dpre = jnp.where(rowidx >= npad, dpre * sg * (1.0 + pre * (1.0 - sg)), 0.0)
        dx = w_ref[K - 1:K, :] * dpre
        for k in range(K - 1):
            dx = dx + w_ref[k:k + 1, :] * pltpu.roll(dpre, n - (K - 1 - k), 0)
        dx_ref[...] = dx[SUBLANES:SUBLANES + tm, :].astype(dx_ref.dtype)

        @pl.when(i == 0)
        def _():
            dw_ref[...] = jnp.zeros_like(dw_ref)
            db_ref[...] = jnp.zeros_like(db_ref)

        dmain = dpre[SUBLANES:SUBLANES + tm, :]
        for k in range(K):
            dw_ref[k:k + 1, :] += jnp.sum(dmain * shifted[k][SUBLANES:SUBLANES + tm, :], axis=0, keepdims=True)
        db_ref[...] += jnp.sum(dmain, axis=0, keepdims=True)

    return pl.pallas_call(
        body, name=name, grid=(C // tc, ni),
        in_specs=[pl.BlockSpec((SUBLANES, tc), lambda g, i: (jnp.maximum(i * rb - 1, 0), cb0 + g)),
                  pl.BlockSpec((tm, tc), lambda g, i: (i, cb0 + g)),
                  pl.BlockSpec((SUBLANES, tc), lambda g, i: (jnp.minimum((i + 1) * rb, last_rb), cb0 + g)),
                  pl.BlockSpec((tm, tc), lambda g, i: (i, g)),
                  pl.BlockSpec((SUBLANES, tc), lambda g, i: (jnp.minimum((i + 1) * rb, last_rb), g)),
                  pl.BlockSpec((SUBLANES, tc), lambda g, i: (0, g)),
                  pl.BlockSpec((1, tc), lambda g, i: (0, g))],
        out_specs=[pl.BlockSpec((tm, tc), lambda g, i: (i, g)),
                   pl.BlockSpec((SUBLANES, tc), lambda g, i: (0, g)),
                   pl.BlockSpec((1, tc), lambda g, i: (0, g))],
        out_shape=[jax.ShapeDtypeStruct((Tp, C), BF16), jax.ShapeDtypeStruct((SUBLANES, C), F32),
                   jax.ShapeDtypeStruct((1, C), F32)],
        compiler_params=_params(("parallel", "arbitrary"), 14 * tm * tc * 4),
    )(x, x, x, dy, dy, w8, b)


def _flash_fwd(q, kv, kr8, npad, name):
    Tp = q.shape[0]
    t = _pick(Tp, ATT_TILE)
    hp = ATT_HEADS_PER_STEP
    nb = Tp // t
    nt = (((1,), (1,)), ((), ()))
    tn = (((0,), (0,)), ((), ()))

    def body(q_ref, kn_ref, kr_ref, v_ref, o_ref, lse_ref, m_sc, l_sc, acc_sc):
        qi = pl.program_id(1)
        ki = pl.program_id(2)

        @pl.when(ki == 0)
        def _():
            m_sc[...] = jnp.full_like(m_sc, NEG_INF)
            l_sc[...] = jnp.zeros_like(l_sc)
            acc_sc[...] = jnp.zeros_like(acc_sc)

        def step(masked):
            kr = kr_ref[...]
            if masked:
                key = ki * t + lax.broadcasted_iota(jnp.int32, (t, t), 0)
                qry = qi * t + lax.broadcasted_iota(jnp.int32, (t, t), 1)
                visible = (key <= qry) & (key >= npad)
            for hh in range(hp):
                k = jnp.concatenate([kn_ref[:, hh * QK_NOPE:(hh + 1) * QK_NOPE], kr], axis=1)
                st = lax.dot_general(k, q_ref[:, hh * QHEAD:(hh + 1) * QHEAD], nt, preferred_element_type=F32)
                if masked:
                    st = jnp.where(visible, st, NEG_INF)
                vs = slice(hh * V_HEAD, (hh + 1) * V_HEAD)
                m_prev = m_sc[hh]
                m_new = jnp.maximum(m_prev, jnp.max(st, axis=0, keepdims=True))
                pt = jnp.exp(st - m_new)
                a = jnp.exp(m_prev - m_new)
                l_sc[hh] = a * l_sc[hh] + jnp.sum(pt, axis=0, keepdims=True)
                acc_sc[vs, :] = a * acc_sc[vs, :] + lax.dot_general(v_ref[:, vs], pt.astype(BF16), tn,
                                                                    preferred_element_type=F32)
                m_sc[hh] = m_new

        need_mask = (ki == qi) | (ki == 0)

        @pl.when((ki <= qi) & need_mask)
        def _():
            step(True)

        @pl.when((ki <= qi) & jnp.logical_not(need_mask))
        def _():
            step(False)

        @pl.when(ki == qi)
        def _():
            for hh in range(hp):
                vs = slice(hh * V_HEAD, (hh + 1) * V_HEAD)
                l = l_sc[hh]
                o_ref[:, vs] = (acc_sc[vs, :] / l).T.astype(o_ref.dtype)
                lse_ref[hh * SUBLANES:(hh + 1) * SUBLANES, :] = jnp.broadcast_to(m_sc[hh] + jnp.log(l), (SUBLANES, t))

    kmin = lambda qi, ki: jnp.minimum(ki, qi)
    return pl.pallas_call(
        body, name=name, grid=(HEADS // hp, nb, nb),
        in_specs=[pl.BlockSpec((t, hp * QHEAD), lambda g, qi, ki: (qi, g)),
                  pl.BlockSpec((t, hp * QK_NOPE), lambda g, qi, ki: (kmin(qi, ki), g)),
                  pl.BlockSpec((t, LANES), lambda g, qi, ki: (kmin(qi, ki), 0)),
                  pl.BlockSpec((t, hp * V_HEAD), lambda g, qi, ki: (kmin(qi, ki), HEADS // hp + g))],
        out_specs=[pl.BlockSpec((t, hp * V_HEAD), lambda g, qi, ki: (qi, g)),
                   pl.BlockSpec((hp * SUBLANES, t), lambda g, qi, ki: (g, qi))],
        out_shape=[jax.ShapeDtypeStruct((Tp, HEADS * V_HEAD), F32), jax.ShapeDtypeStruct((HEADS * SUBLANES, Tp), F32)],
        scratch_shapes=[pltpu.VMEM((hp, 1, t), F32), pltpu.VMEM((hp, 1, t), F32), pltpu.VMEM((hp * V_HEAD, t), F32)],
        compiler_params=_params(("parallel", "parallel", "arbitrary"), 8 * hp * t * t * 4),
    )(q, kv, kr8, kv)


def _attn_delta(do, o, name):
    Tp = do.shape[0]
    tm = _pick(Tp, MM_TOKEN_K_TILE)

    def body(do_ref, o_ref, d_ref):
        prod = do_ref[...] * o_ref[...]
        ones = jnp.ones((SUBLANES, V_HEAD), F32)
        d_ref[...] = lax.dot_general(ones, prod, (((1,), (1,)), ((), ())), precision=HIGHEST,
                                     preferred_element_type=F32)

    return pl.pallas_call(
        body, name=name, grid=(HEADS, Tp // tm),
        in_specs=[pl.BlockSpec((tm, V_HEAD), lambda h, i: (i, h)), pl.BlockSpec((tm, V_HEAD), lambda h, i: (i, h))],
        out_specs=pl.BlockSpec((SUBLANES, tm), lambda h, i: (h, i)),
        out_shape=jax.ShapeDtypeStruct((HEADS * SUBLANES, Tp), F32),
        compiler_params=_params(("parallel", "parallel"), 4 * tm * V_HEAD * 4),
    )(do, o)


def _flash_bwd(q, kv, kr8, do, lse, delta, npad, name):
    Tp = q.shape[0]
    t = _pick(Tp, ATT_TILE)
    nb = Tp // t
    nt = (((1,), (1,)), ((), ()))
    tn = (((0,), (0,)), ((), ()))

    def body(q_ref, kn_ref, kr_ref, v_ref, do_ref, lse_ref, dl_ref, dq_ref, dkn_ref, dkr_ref, dv_ref, dk_sc, dv_sc):
        ki = pl.program_id(1)
        qi = pl.program_id(2)

        @pl.when(qi == 0)
        def _():
            dk_sc[...] = jnp.zeros_like(dk_sc)
            dv_sc[...] = jnp.zeros_like(dv_sc)

        def step(masked):
            qv = q_ref[...]
            k = jnp.concatenate([kn_ref[...], kr_ref[...]], axis=1)
            st = lax.dot_general(k, qv, nt, preferred_element_type=F32)
            if masked:
                key = ki * t + lax.broadcasted_iota(jnp.int32, (t, t), 0)
                qry = qi * t + lax.broadcasted_iota(jnp.int32, (t, t), 1)
                st = jnp.where((key <= qry) & (key >= npad), st, NEG_INF)
            pt = jnp.exp(st - lse_ref[0:1, :])
            dob = do_ref[...].astype(BF16)
            dv_sc[...] += jnp.dot(pt.astype(BF16), dob, preferred_element_type=F32)
            dpt = lax.dot_general(v_ref[...], dob, nt, preferred_element_type=F32)
            dst = (pt * (dpt - dl_ref[0:1, :])).astype(BF16)
            dk_sc[...] += jnp.dot(dst, qv, preferred_element_type=F32)
            dqc = lax.dot_general(dst, k, tn, preferred_element_type=F32)
            rows = pl.ds(pl.multiple_of(qi * t, t), t)

            @pl.when(ki == 0)
            def _():
                dq_ref[rows, :] = dqc

            @pl.when(ki > 0)
            def _():
                dq_ref[rows, :] += dqc

        need_mask = (ki == qi) | (ki == 0)

        @pl.when((qi >= ki) & need_mask)
        def _():
            step(True)

        @pl.when((qi >= ki) & jnp.logical_not(need_mask))
        def _():
            step(False)

        @pl.when(qi == nb - 1)
        def _():
            dkn_ref[...] = dk_sc[:, :QK_NOPE].astype(dkn_ref.dtype)
            dkr_ref[...] = dk_sc[:, QK_NOPE:].astype(dkr_ref.dtype)
            dv_ref[...] = dv_sc[...].astype(dv_ref.dtype)

    qmap = lambda h, ki, qi: (jnp.maximum(qi, ki), h)
    kmap = lambda h, ki, qi: (ki, h)
    est = 2 * Tp * QHEAD * 4 + 8 * t * t * 4
    return pl.pallas_call(
        body, name=name, grid=(HEADS, nb, nb),
        in_specs=[pl.BlockSpec((t, QHEAD), qmap),
                  pl.BlockSpec((t, QK_NOPE), kmap),
                  pl.BlockSpec((t, LANES), kmap),
                  pl.BlockSpec((t, V_HEAD), lambda h, ki, qi: (ki, HEADS + h)),
                  pl.BlockSpec((t, V_HEAD), qmap),
                  pl.BlockSpec((SUBLANES, t), lambda h, ki, qi: (h, jnp.maximum(qi, ki))),
                  pl.BlockSpec((SUBLANES, t), lambda h, ki, qi: (h, jnp.maximum(qi, ki)))],
        out_specs=[pl.BlockSpec((Tp, QHEAD), lambda h, ki, qi: (0, h)),
                   pl.BlockSpec((t, QK_NOPE), kmap),
                   pl.BlockSpec((t, LANES), kmap),
                   pl.BlockSpec((t, V_HEAD), kmap)],
        out_shape=[jax.ShapeDtypeStruct((Tp, HEADS * QHEAD), F32),
                   jax.ShapeDtypeStruct((Tp, HEADS * QK_NOPE), BF16),
                   jax.ShapeDtypeStruct((Tp, HEADS * LANES), F32),
                   jax.ShapeDtypeStruct((Tp, HEADS * V_HEAD), BF16)],
        scratch_shapes=[pltpu.VMEM((t, QHEAD), F32), pltpu.VMEM((t, V_HEAD), F32)],
        compiler_params=_params(("parallel", "arbitrary", "arbitrary"), est),
    )(q, kv, kr8, kv, do, lse, delta)


GW = SSD_INNER // SSD_GROUPS
PAIRS_PER_GROUP = GW // LANES
XB = SSD_INNER // GW
NT_DIMS = (((1,), (1,)), ((), ()))
TN_DIMS = (((0,), (0,)), ((), ()))


def _ssd_common(xs_ref, dt_ref, alog_ref, e_ref):
    a_neg = -jnp.exp(alog_ref[...])
    dt = dt_ref[...]
    li = lax.broadcasted_iota(jnp.int32, (CHUNK, CHUNK), 0)
    si = lax.broadcasted_iota(jnp.int32, (CHUNK, CHUNK), 1)
    tril = li >= si
    tri = tril.astype(F32)
    acs = jnp.dot(tri, dt * a_neg, precision=SSD_PREC, preferred_element_type=F32)
    e = e_ref[...]
    dte = jnp.dot(dt, e, precision=SSD_PREC, preferred_element_type=F32)
    acse = jnp.dot(acs, e, precision=SSD_PREC, preferred_element_type=F32)
    x = xs_ref[...] * dte
    alast = acse[CHUNK - 1:CHUNK, :]
    return dict(a_neg=a_neg, dt=dt, tril=tril, tri=tri, acs=acs, acs_t=acs.T, e=e, dte=dte, acse=acse, x=x,
                p_e=jnp.exp(acse), w_e=jnp.exp(alast - acse), dl_e=jnp.exp(alast), li=li, si=si)


def _decay(cm, head):
    col = cm["acs"][:, head:head + 1]
    row = cm["acs_t"][head:head + 1, :]
    return jnp.exp(jnp.where(cm["tril"], col - row, -jnp.inf))


def _ssd_fwd(xbc, dt, alog, e, name):
    Tp = xbc.shape[0]
    nc = Tp // CHUNK

    def body(xs_ref, b_ref, c_ref, dt_ref, alog_ref, e_ref, y_ref, st_ref, st_sc):
        @pl.when(pl.program_id(0) == 0)
        def _():
            st_sc[...] = jnp.zeros_like(st_sc)

        cm = _ssd_common(xs_ref, dt_ref, alog_ref, e_ref)
        st_ref[0] = st_sc[...]
        lane = lax.broadcasted_iota(jnp.int32, (CHUNK, LANES), 1)
        for g in range(SSD_GROUPS):
            gs = slice(g * GW, (g + 1) * GW)
            cg = c_ref[:, g * SSD_STATE:(g + 1) * SSD_STATE].astype(BF16)
            bg = b_ref[:, g * SSD_STATE:(g + 1) * SSD_STATE].astype(BF16)
            cb = lax.dot_general(cg, bg, NT_DIMS, preferred_element_type=F32)
            stg = st_sc[:, gs]
            yoff = jnp.dot(cg, stg.astype(BF16), preferred_element_type=F32) * cm["p_e"][:, gs]
            xg = cm["x"][:, gs]
            for jp in range(PAIRS_PER_GROUP):
                j = g * PAIRS_PER_GROUP + jp
                xp = xg[:, jp * LANES:(jp + 1) * LANES].astype(BF16)
                ys = []
                for head in (2 * j, 2 * j + 1):
                    m = (cb * _decay(cm, head)).astype(BF16)
                    ys.append(jnp.dot(m, xp, preferred_element_type=F32))
                y_ref[:, j * LANES:(j + 1) * LANES] = (jnp.where(lane < SSD_HEAD_DIM, ys[0], ys[1])
                                                       + yoff[:, jp * LANES:(jp + 1) * LANES])
            snew = lax.dot_general(bg, (cm["w_e"][:, gs] * xg).astype(BF16), TN_DIMS, preferred_element_type=F32)
            st_sc[:, gs] = cm["dl_e"][:, gs] * stg + snew

    return pl.pallas_call(
        body, name=name, grid=(nc,),
        in_specs=[pl.BlockSpec((CHUNK, SSD_INNER), lambda c: (c, 0)),
                  pl.BlockSpec((CHUNK, GW), lambda c: (c, XB)),
                  pl.BlockSpec((CHUNK, GW), lambda c: (c, XB + 1)),
                  pl.BlockSpec((CHUNK, LANES), lambda c: (c, 0)),
                  pl.BlockSpec((1, LANES), lambda c: (0, 0)),
                  pl.BlockSpec((LANES, SSD_INNER), lambda c: (0, 0))],
        out_specs=[pl.BlockSpec((CHUNK, SSD_INNER), lambda c: (c, 0)),
                   pl.BlockSpec((1, SSD_STATE, SSD_INNER), lambda c: (c, 0, 0))],
        out_shape=[jax.ShapeDtypeStruct((Tp, SSD_INNER), F32), jax.ShapeDtypeStruct((nc, SSD_STATE, SSD_INNER), F32)],
        scratch_shapes=[pltpu.VMEM((SSD_STATE, SSD_INNER), F32)],
        compiler_params=_params(("arbitrary",), 24 * CHUNK * SSD_INNER * 4),
    )(xbc, xbc, xbc, dt, alog, e)


def _ssd_bwd(xbc, dt, alog, e, dy, dxs_skip, states, name):
    Tp = xbc.shape[0]
    nc = Tp // CHUNK
    rev = lambda c: nc - 1 - c

    def body(xs_ref, b_ref, c_ref, dt_ref, alog_ref, e_ref, dy_ref, skip_ref, st_ref,
             dxbc_ref, ddt_ref, dalog_ref, dst_sc, dx_sc, t_sc, tw_sc):
        @pl.when(pl.program_id(0) == 0)
        def _():
            dst_sc[...] = jnp.zeros_like(dst_sc)
            dalog_ref[...] = jnp.zeros_like(dalog_ref)

        cm = _ssd_common(xs_ref, dt_ref, alog_ref, e_ref)
        lane = lax.broadcasted_iota(jnp.int32, (CHUNK, LANES), 1)
        dacs_col = jnp.zeros((CHUNK, LANES), F32)
        dacs_row = jnp.zeros((LANES, CHUNK), F32)
        t_last = []
        for g in range(SSD_GROUPS):
            gs = slice(g * GW, (g + 1) * GW)
            cg = c_ref[:, g * SSD_STATE:(g + 1) * SSD_STATE].astype(BF16)
            bg = b_ref[:, g * SSD_STATE:(g + 1) * SSD_STATE].astype(BF16)
            stg = st_ref[0, :, gs]
            stg_b = stg.astype(BF16)
            dstg = dst_sc[:, gs]
            dstg_b = dstg.astype(BF16)
            xg = cm["x"][:, gs]
            dyg = dy_ref[:, gs]
            zg = jnp.dot(cg, stg_b, preferred_element_type=F32)
            dzg = dyg * cm["p_e"][:, gs]
            dzg_b = dzg.astype(BF16)
            dcg = lax.dot_general(dzg_b, stg_b, NT_DIMS, preferred_element_type=F32)
            dst_in = lax.dot_general(cg, dzg_b, TN_DIMS, preferred_element_type=F32)
            dst_in = dst_in + cm["dl_e"][:, gs] * dstg
            t_last.append(jnp.sum(dstg * stg * cm["dl_e"][:, gs], axis=0, keepdims=True))
            weg = cm["w_e"][:, gs]
            dbg = lax.dot_general((weg * xg).astype(BF16), dstg_b, NT_DIMS, preferred_element_type=F32)
            gg = jnp.dot(bg, dstg_b, preferred_element_type=F32)
            dxg = weg * gg
            tw_sc[:, gs] = xg * dxg
            t_sc[:, gs] = dzg * zg - xg * dxg
            cb = lax.dot_general(cg, bg, NT_DIMS, preferred_element_type=F32)
            dcb = jnp.zeros((CHUNK, CHUNK), F32)
            for jp in range(PAIRS_PER_GROUP):
                j = g * PAIRS_PER_GROUP + jp
                ps = slice(jp * LANES, (jp + 1) * LANES)
                xp = xg[:, ps].astype(BF16)
                dyp = dyg[:, ps]
                dxp = dxg[:, ps]
                for half, head in enumerate((2 * j, 2 * j + 1)):
                    lam = _decay(cm, head)
                    m32 = cb * lam
                    sel = (lane < SSD_HEAD_DIM) if half == 0 else (lane >= SSD_HEAD_DIM)
                    dye = jnp.where(sel, dyp, 0.0).astype(BF16)
                    dm = lax.dot_general(dye, xp, NT_DIMS, preferred_element_type=F32)
                    w = dm * m32
                    dacs_col = dacs_col + jnp.where(cm["si"] == head, jnp.sum(w, axis=1, keepdims=True), 0.0)
                    dacs_row = dacs_row + jnp.where(cm["li"] == head, jnp.sum(w, axis=0, keepdims=True), 0.0)
                    dcb = dcb + dm * lam
                    dxp = dxp + lax.dot_general(m32.astype(BF16), dye, TN_DIMS, preferred_element_type=F32)
                dx_sc[:, j * LANES:(j + 1) * LANES] = dxp
            dcb_b = dcb.astype(BF16)
            dcg = dcg + jnp.dot(dcb_b, bg, preferred_element_type=F32)
            dbg = dbg + lax.dot_general(dcb_b, cg, TN_DIMS, preferred_element_type=F32)
            dst_sc[:, gs] = dst_in
            dxbc_ref[:, SSD_INNER + g * SSD_STATE:SSD_INNER + (g + 1) * SSD_STATE] = dbg
            dxbc_ref[:, SSD_INNER + GW + g * SSD_STATE:SSD_INNER + GW + (g + 1) * SSD_STATE] = dcg
        e = cm["e"]
        dacs = lax.dot_general(t_sc[...], e, NT_DIMS, precision=SSD_PREC, preferred_element_type=F32)
        dacs = dacs + dacs_col - dacs_row.T
        last_lane = jnp.concatenate(t_last, axis=1) + jnp.sum(tw_sc[...], axis=0, keepdims=True)
        last_head = lax.dot_general(jnp.broadcast_to(last_lane, (SUBLANES, SSD_INNER)), e, NT_DIMS,
                                    precision=SSD_PREC, preferred_element_type=F32)[0:1, :]
        dacs = dacs + jnp.where(cm["li"] == CHUNK - 1, last_head, 0.0)
        da = lax.dot_general(cm["tri"], dacs, TN_DIMS, precision=SSD_PREC, preferred_element_type=F32)
        dx_all = dx_sc[...]
        ddt = da * cm["a_neg"] + lax.dot_general(dx_all * xs_ref[...], e, NT_DIMS, precision=SSD_PREC,
                                                 preferred_element_type=F32)
        ddt_ref[...] = ddt
        dxbc_ref[:, :SSD_INNER] = dx_all * cm["dte"] + skip_ref[...]
        dalog_ref[0:1, :] += jnp.sum(da * cm["dt"], axis=0, keepdims=True) * cm["a_neg"]

    return pl.pallas_call(
        body, name=name, grid=(nc,),
        in_specs=[pl.BlockSpec((CHUNK, SSD_INNER), lambda c: (rev(c), 0)),
                  pl.BlockSpec((CHUNK, GW), lambda c: (rev(c), XB)),
                  pl.BlockSpec((CHUNK, GW), lambda c: (rev(c), XB + 1)),
                  pl.BlockSpec((CHUNK, LANES), lambda c: (rev(c), 0)),
                  pl.BlockSpec((1, LANES), lambda c: (0, 0)),
                  pl.BlockSpec((LANES, SSD_INNER), lambda c: (0, 0)),
                  pl.BlockSpec((CHUNK, SSD_INNER), lambda c: (rev(c), 0)),
                  pl.BlockSpec((CHUNK, SSD_INNER), lambda c: (rev(c), 0)),
                  pl.BlockSpec((1, SSD_STATE, SSD_INNER), lambda c: (rev(c), 0, 0))],
        out_specs=[pl.BlockSpec((CHUNK, SSD_CONV_DIM), lambda c: (rev(c), 0)),
                   pl.BlockSpec((CHUNK, LANES), lambda c: (rev(c), 0)),
                   pl.BlockSpec((SUBLANES, LANES), lambda c: (0, 0))],
        out_shape=[jax.ShapeDtypeStruct((Tp, SSD_CONV_DIM), F32), jax.ShapeDtypeStruct((Tp, LANES), F32),
                   jax.ShapeDtypeStruct((SUBLANES, LANES), F32)],
        scratch_shapes=[pltpu.VMEM((SSD_STATE, SSD_INNER), F32), pltpu.VMEM((CHUNK, SSD_INNER), F32),
                        pltpu.VMEM((CHUNK, SSD_INNER), F32), pltpu.VMEM((CHUNK, SSD_INNER), F32)],
        compiler_params=_params(("arbitrary",), 32 * CHUNK * SSD_INNER * 4),
    )(xbc, xbc, xbc, dt, alog, e, dy, dxs_skip, states)


def _loss_head(h, target, name):
    Tp, d = h.shape
    nt = Tp // LANES

    def body(h_ref, t_ref, dh_ref, l_ref):
        real = pl.program_id(0) > 0
        err = jnp.where(real, h_ref[...] - t_ref[...], 0.0)
        dh_ref[...] = err * (1.0 / d)
        l_ref[...] = jnp.broadcast_to(0.5 * jnp.sum(err * err) * (1.0 / d), l_ref.shape)

    return pl.pallas_call(
        body, name=name, grid=(nt,),
        in_specs=[pl.BlockSpec((LANES, d), lambda i: (i, 0)),
                  pl.BlockSpec((LANES, d), lambda i: (jnp.maximum(i - 1, 0), 0))],
        out_specs=[pl.BlockSpec((LANES, d), lambda i: (i, 0)),
                   pl.BlockSpec((1, SUBLANES, LANES), lambda i: (i, 0, 0))],
        out_shape=[jax.ShapeDtypeStruct((Tp, d), F32), jax.ShapeDtypeStruct((nt, SUBLANES, LANES), F32)],
        compiler_params=_params(("parallel",), 8 * LANES * d * 4),
    )(h, target)


def _adamw(parts, w, m, v, name):
    shape = w.shape
    C = shape[-1]
    R = int(np.prod(shape[:-1]))
    npart = parts.shape[0]
    parts, w, m, v = parts.reshape(npart, R, C), w.reshape(R, C), m.reshape(R, C), v.reshape(R, C)
    lanes = -(-C // LANES) * LANES
    tr = _pick(R, max(BF16_ROWS, ADAM_ELEMS // lanes), BF16_ROWS) if R % BF16_ROWS == 0 else R
    c1 = 1.0 / (1.0 - ADAM_B1 ** ADAM_STEP)
    c2 = 1.0 / (1.0 - ADAM_B2 ** ADAM_STEP)

    def body(p_ref, w_ref, m_ref, v_ref, g_out, d_out, m_out, v_out):
        g = p_ref[0].astype(F32)
        for p in range(1, npart):
            g = g + p_ref[p].astype(F32)
        m_new = ADAM_B1 * m_ref[...] + (1.0 - ADAM_B1) * g
        v_new = ADAM_B2 * v_ref[...] + (1.0 - ADAM_B2) * (g * g)
        g_out[...] = g
        m_out[...] = m_new
        v_out[...] = v_new
        d_out[...] = -ADAM_LR * ((m_new * c1) / (jnp.sqrt(v_new * c2) + ADAM_EPS) + ADAM_WD * w_ref[...])

    spec = pl.BlockSpec((tr, C), lambda i: (i, 0))
    est = npart * tr * lanes * parts.dtype.itemsize + 7 * tr * lanes * 4
    res = pl.pallas_call(
        body, name=name, grid=(R // tr,),
        in_specs=[pl.BlockSpec((npart, tr, C), lambda i: (0, i, 0)), spec, spec, spec],
        out_specs=[spec] * 4, out_shape=[jax.ShapeDtypeStruct((R, C), F32)] * 4,
        compiler_params=_params(("parallel",), est),
    )(parts, w, m, v)
    return [r.reshape(shape) for r in res]


MESH_ID = pl.DeviceIdType.MESH
N_PEERS = N_DEV - 1


def _dev_index(p):
    return 4 * p[0] + 2 * p[1] + p[2]


def _comm_call(body, name, arrs, out_shape, npairs):
    n = len(arrs)
    any_spec = pl.BlockSpec(memory_space=pl.ANY)
    return pl.pallas_call(
        functools.partial(body, n), name=name, in_specs=[any_spec] * n, out_specs=[any_spec] * n, out_shape=out_shape,
        scratch_shapes=[pltpu.SemaphoreType.DMA((n, npairs)), pltpu.SemaphoreType.DMA((n, npairs)),
                        pltpu.SemaphoreType.DMA((n,))],
    )(*arrs)


def _allgather(arrs, name):
    def body(n, *refs):
        src_refs, out_refs = refs[:n], refs[n:2 * n]
        send_sems, recv_sems, local_sems = refs[2 * n:]
        x, y, c = lax.axis_index("x"), lax.axis_index("y"), lax.axis_index("c")
        me, sibling = (x, y, c), (x, y, 1 - c)
        chips = [(1 - x, y), (x, 1 - y), (1 - x, 1 - y)]

        def copy(t, k, block, to, src=None):
            slot = out_refs[t].at[_dev_index(block)]
            return pltpu.make_async_remote_copy(
                src_ref=slot if src is None else src, dst_ref=slot,
                send_sem=send_sems.at[t, k], recv_sem=recv_sems.at[t, k],
                device_id=to, device_id_type=MESH_ID)

        sends, locals_ = [], []
        for t in range(n):
            mine = pltpu.make_async_copy(src_refs[t], out_refs[t].at[_dev_index(me)], local_sems.at[t])
            mine.start()
            locals_.append(mine)
            first = [copy(t, 0, me, sibling, src=src_refs[t])]
            first += [copy(t, 1 + j, me, (*chip, c), src=src_refs[t]) for j, chip in enumerate(chips)]
            for cp in first:
                cp.start()
            sends += first
        for j, chip in enumerate(chips):
            for t in range(n):
                copy(t, 1 + j, (*chip, c), me).wait_recv()
                passed = copy(t, 4 + j, (*chip, c), sibling)
                passed.start()
                sends.append(passed)
        for t in range(n):
            copy(t, 0, sibling, me).wait_recv()
            for j, chip in enumerate(chips):
                copy(t, 4 + j, (*chip, 1 - c), me).wait_recv()
        for cp in sends:
            cp.wait_send()
        for cp in locals_:
            cp.wait()

    return _comm_call(body, name, arrs, [jax.ShapeDtypeStruct((N_DEV,) + a.shape, a.dtype) for a in arrs], N_PEERS)


N_CHIPS = N_DEV // 2
CHIPS = [(0, 0), (0, 1), (1, 0), (1, 1)]


def _sibling_exchange(arrs, name):
    def body(n, *refs):
        in_refs, out_refs = refs[:n], refs[n:2 * n]
        send_sems, recv_sems, _ = refs[2 * n:]
        x, y, c = lax.axis_index("x"), lax.axis_index("y"), lax.axis_index("c")
        sibling = (x, y, 1 - c)

        def copy(t, j):
            return pltpu.make_async_remote_copy(
                src_ref=in_refs[t].at[_dev_index((*CHIPS[j], 1 - c))], dst_ref=out_refs[t].at[j],
                send_sem=send_sems.at[t, j], recv_sem=recv_sems.at[t, j],
                device_id=sibling, device_id_type=MESH_ID)

        copies = [copy(t, j) for t in range(n) for j in range(N_CHIPS)]
        for cp in copies:
            cp.start()
        for cp in copies:
            cp.wait_recv()
        for cp in copies:
            cp.wait_send()

    return _comm_call(body, name, arrs, [jax.ShapeDtypeStruct((N_CHIPS,) + a.shape[1:], a.dtype) for a in arrs], N_CHIPS)


def _chip_exchange(arrs, name):
    def body(n, *refs):
        in_refs, out_refs = refs[:n], refs[n:2 * n]
        send_sems, recv_sems, local_sems = refs[2 * n:]
        x, y, c = lax.axis_index("x"), lax.axis_index("y"), lax.axis_index("c")
        mine = 2 * x + y
        peers = [(1 - x, y), (x, 1 - y), (1 - x, 1 - y)]

        def copy(t, k, src_chip, dst_chip, to):
            return pltpu.make_async_remote_copy(
                src_ref=in_refs[t].at[src_chip], dst_ref=out_refs[t].at[dst_chip],
                send_sem=send_sems.at[t, k], recv_sem=recv_sems.at[t, k],
                device_id=(*to, c), device_id_type=MESH_ID)

        sends, locals_ = [], []
        for t in range(n):
            own = pltpu.make_async_copy(in_refs[t].at[mine], out_refs[t].at[mine], local_sems.at[t])
            own.start()
            locals_.append(own)
            for k, p in enumerate(peers):
                cp = copy(t, k, 2 * p[0] + p[1], mine, p)
                cp.start()
                sends.append(cp)
        for t in range(n):
            for k, p in enumerate(peers):
                copy(t, k, mine, 2 * p[0] + p[1], p).wait_recv()
        for cp in sends:
            cp.wait_send()
        for cp in locals_:
            cp.wait()

    return _comm_call(body, name, arrs, [jax.ShapeDtypeStruct(a.shape, a.dtype) for a in arrs], N_CHIPS - 1)


def _add_pairs(a, b, name):
    shape = a.shape
    C = shape[-1]
    R = int(np.prod(shape[:-1]))
    lanes = -(-C // LANES) * LANES
    tr = _pick(R, max(BF16_ROWS, 2 * ADAM_ELEMS // lanes), BF16_ROWS) if R % BF16_ROWS == 0 else R

    def body(a_ref, b_ref, o_ref):
        o_ref[...] = (a_ref[...].astype(F32) + b_ref[...].astype(F32)).astype(o_ref.dtype)

    spec = pl.BlockSpec((tr, C), lambda i: (i, 0))
    return pl.pallas_call(
        body, name=name, grid=(R // tr,), in_specs=[spec, spec], out_specs=spec,
        out_shape=jax.ShapeDtypeStruct((R, C), a.dtype),
        compiler_params=_params(("parallel",), 3 * tr * lanes * 4),
    )(a.reshape(R, C), b.reshape(R, C)).reshape(shape)


WEIGHTS = ['meta_tokens', 'emb_ln_g', 'emb_ln_b', 'w_in', 'q_norm_g', 'w_q_b', 'kv_norm_g', 'w_kv_b', 'w_o_attn',
           'ssd_conv_w', 'ssd_conv_b', 'dt_bias', 'a_log', 'd_skip', 'ssd_norm_g', 'w_o_ssd', 'w_out', 'ln1_g',
           'ln1_b', 'w_up', 'ffn_conv_w', 'ffn_conv_b', 'w_down', 'ln2_g', 'ln2_b']
BIG = {'w_in': 2, 'w_q_b': 2, 'w_kv_b': 2, 'w_o_attn': 1, 'w_o_ssd': 1, 'w_out': 1, 'w_up': 2, 'w_down': 1}
SMALL_SHARDED = {'meta_tokens': 1, 'ssd_conv_w': 2, 'ffn_conv_w': 2}
REPLICATED = [n for n in WEIGHTS if n not in BIG and n not in SMALL_SHARDED]
BIG_COLS = 1024
SMALL_COLS = LANES


def _flatten(arrs, cols, row_mult, lead=False):
    parts, offs, off = [], [], 0
    for a in arrs:
        a2 = a.reshape(N_DEV, -1) if lead else a.reshape(1, -1)
        n = a2.shape[1]
        pad = -n % cols
        parts.append(jnp.pad(a2, ((0, 0), (0, pad))))
        offs.append((off, n))
        off += n + pad
    rows = off // cols
    extra = (-rows % row_mult) * cols
    if extra:
        parts.append(jnp.zeros((parts[0].shape[0], extra), parts[0].dtype))
    flat = jnp.concatenate(parts, axis=1)
    flat = flat.reshape(flat.shape[0], -1, cols)
    return (flat if lead else flat[0]), offs


def _unflatten(flat, offs, shapes):
    f = flat.reshape(-1)
    return [f[o:o + n].reshape(s) for (o, n), s in zip(offs, shapes)]


def _to_pieces(g, axis):
    s = g.shape[axis] // N_DEV
    g = g.reshape(g.shape[:axis] + (N_DEV, s) + g.shape[axis + 1:])
    return jnp.moveaxis(g, axis, 0).reshape(N_DEV, -1)


def _from_pieces(p, shard_shape, axis):
    g = jnp.moveaxis(p.reshape((N_DEV,) + tuple(shard_shape)), 0, axis)
    sh = list(shard_shape)
    sh[axis] *= N_DEV
    return g.reshape(sh)


def _in_proj_pad(w):
    e = np.cumsum((0,) + IN_SIZES)
    ql, kvl, kpe, z, xbc, dt, ga, gs = [w[:, e[j]:e[j + 1]] for j in range(8)]
    zc = lambda n: jnp.zeros((w.shape[0], n), w.dtype)
    return jnp.concatenate([ql, kvl, z, xbc, ga, gs, kpe, zc(LANES - QK_ROPE), dt, zc(LANES - SSD_HEADS)], axis=1)


def _in_proj_unpad(d):
    seg = lambda o, n: d[:, o:o + n]
    return jnp.concatenate([seg(OQ, Q_LORA), seg(OKV, KV_LORA), seg(OKPE, QK_ROPE), seg(OZ, SSD_INNER),
                            seg(OXBC, SSD_CONV_DIM), seg(ODT, SSD_HEADS), seg(OGA, D_MODEL), seg(OGS, D_MODEL)], axis=1)


def _q_pad(w):
    w3 = w.reshape(Q_LORA, HEADS, QK_NOPE + QK_ROPE)
    return jnp.concatenate([w3, jnp.zeros((Q_LORA, HEADS, QHEAD - QK_NOPE - QK_ROPE), w.dtype)], axis=2).reshape(Q_LORA, HEADS * QHEAD)


def _q_unpad(d):
    return d.reshape(Q_LORA, HEADS, QHEAD)[:, :, :QK_NOPE + QK_ROPE].reshape(Q_LORA, HEADS * (QK_NOPE + QK_ROPE))


def _kv_perm(w):
    w3 = w.reshape(KV_LORA, HEADS, QK_NOPE + V_HEAD)
    return jnp.concatenate([w3[:, :, :QK_NOPE].reshape(KV_LORA, -1), w3[:, :, QK_NOPE:].reshape(KV_LORA, -1)], axis=1)


def _kv_unperm(d):
    kn = d[:, :HEADS * QK_NOPE].reshape(KV_LORA, HEADS, QK_NOPE)
    v = d[:, HEADS * QK_NOPE:].reshape(KV_LORA, HEADS, V_HEAD)
    return jnp.concatenate([kn, v], axis=2).reshape(KV_LORA, HEADS * (QK_NOPE + V_HEAD))


def _row_vec(v, width=None):
    v = v.reshape(1, -1).astype(F32)
    if width is not None and v.shape[1] < width:
        v = jnp.pad(v, ((0, 0), (0, width - v.shape[1])))
    return v


def _pad_rows8(w):
    return jnp.pad(w.astype(F32), ((0, SUBLANES - w.shape[0]), (0, 0)))


def _tables(Tp, npad):
    pos = jnp.maximum(jnp.arange(Tp, dtype=jnp.int32) - npad, 0).astype(F32)
    inv_freq = 1.0 / (ROPE_THETA ** (jnp.arange(0, QK_ROPE, 2, dtype=F32) / QK_ROPE))
    ang = pos[:, None] * inv_freq[None, :]
    ang = jnp.concatenate([ang, ang], axis=-1)
    zeros = jnp.zeros((Tp, LANES - QK_ROPE), F32)
    cos = jnp.concatenate([jnp.cos(ang), zeros], axis=1)
    sin = jnp.concatenate([jnp.sin(ang), zeros], axis=1)
    rot = np.zeros((LANES, LANES), np.float32)
    half = QK_ROPE // 2
    for i in range(half):
        rot[i + half, i] = -1.0
        rot[i, i + half] = 1.0
    expand = np.zeros((LANES, SSD_INNER), np.float32)
    for hd in range(SSD_HEADS):
        expand[hd, hd * SSD_HEAD_DIM:(hd + 1) * SSD_HEAD_DIM] = 1.0
    return cos, sin, jnp.asarray(rot), jnp.asarray(expand)


def _layer_rows(proj, tb):
    rows_a = [_row(proj, Q_LORA, OQ // Q_LORA), _row(proj, KV_LORA, OKV // KV_LORA), _row(proj, LANES, OKPE // LANES),
              _row(proj, LANES, ODT // LANES), _row(tb["cos"], diff=False), _row(tb["sin"], diff=False)]
    return rows_a


def _layer_fwd(h, P, tb, fns, npad):
    proj = _mm(h, P["w_in"], F32, "in_proj")
    rows_a = _layer_rows(proj, tb)
    consts_a = [_row(tb["rot"], diff=False), _row(P["q_norm_g"]), _row(P["kv_norm_g"]), _row(P["dt_bias"])]
    qn, kvn, kr8, dt = _rw_fwd(fns["in_post"], rows_a, consts_a,
                               [_out(Q_LORA, BF16), _out(KV_LORA, BF16), _out(HEADS * LANES, BF16), _out(LANES, F32)],
                               "in_post")
    q = _mm(qn, P["w_q"], F32, "q_proj")
    rows_q = [_row(q, QHEAD, 0, grp=True), _row(tb["cos"], diff=False), _row(tb["sin"], diff=False)]
    qr = _rw_fwd(fns["q_post"], rows_q, [_row(tb["rot"], diff=False)], [_out(HEADS * QHEAD, BF16, QHEAD, grp=True)],
                 "q_post", ng=HEADS)[0]
    kv = _mm(kvn, P["w_kv"], BF16, "kv_proj")
    o, lse = _flash_fwd(qr, kv, kr8, npad, "attn_fwd")
    ya = _mm(o, P["w_o_attn"], F32, "attn_out")
    xbc = _conv_fwd(proj, OXBC, SSD_CONV_DIM, P["ssd_conv_w"], P["ssd_conv_b"], SSD_CONV, True, npad, "ssd_conv")
    y, states = _ssd_fwd(xbc, dt, P["a_log"], tb["expand"], "ssd_fwd")
    rows_b = [_row(y, GW, 0, grp=True), _row(xbc, GW, 0, grp=True), _row(proj, GW, OZ // GW, grp=True)]
    consts_b = [_row(P["d_skip"], GW, 0, grp=True), _row(P["ssd_norm_g"], GW, 0, grp=True)]
    yn = _rw_fwd(fns["gated"], rows_b, consts_b, [_out(SSD_INNER, BF16, GW, grp=True)], "ssd_gate", ng=SSD_GROUPS)[0]
    ys = _mm(yn, P["w_o_ssd"], F32, "ssd_out")
    rows_c = [_row(proj, D_MODEL, OGA // D_MODEL), _row(proj, D_MODEL, OGS // D_MODEL), _row(ya), _row(ys)]
    mixed = _rw_fwd(fns["mix"], rows_c, [], [_out(D_MODEL, BF16)], "mix")[0]
    mo = _mm(mixed, P["w_out"], F32, "mix_out")
    consts_1 = [_row(P["ln1_g"]), _row(P["ln1_b"])]
    h1 = _rw_fwd(fns["res_ln"], [_row(h), _row(mo)], consts_1, [_out(D_MODEL, F32)], "ln1")[0]
    up = _mm(h1, P["w_up"], F32, "ffn_up")
    u = _conv_fwd(up, 0, 2 * D_FF, P["ffn_conv_w"], P["ffn_conv_b"], FFN_CONV, False, npad, "ffn_conv")
    act = _rw_fwd(fns["glu"], [_row(u)], [], [_out(D_FF, BF16)], "ffn_glu")[0]
    fo = _mm(act, P["w_down"], F32, "ffn_down")
    consts_2 = [_row(P["ln2_g"]), _row(P["ln2_b"])]
    h2 = _rw_fwd(fns["res_ln"], [_row(h1), _row(fo)], consts_2, [_out(D_MODEL, F32)], "ln2")[0]
    res = dict(h=h, proj=proj, qn=qn, kvn=kvn, kr8=kr8, dt=dt, q=q, qr=qr, kv=kv, o=o, lse=lse, ya=ya, xbc=xbc, y=y,
               states=states, yn=yn, ys=ys, mixed=mixed, mo=mo, h1=h1, up=up, u=u, act=act, fo=fo)
    return h2, res


def _layer_bwd(dh2, r, P, tb, fns, npad):
    g = {}
    consts_2 = [_row(P["ln2_g"]), _row(P["ln2_b"])]
    (dh1_a, dfo), (g["ln2_g"], g["ln2_b"]) = _rw_bwd(fns["res_ln"], [_row(r["h1"]), _row(r["fo"])], consts_2,
                                                     [_row(dh2)], [F32, BF16], "ln2_bwd")
    g["w_down"] = _mm(r["act"], dfo, BF16, "dw_down", ta=True)
    dact = _mm(dfo, P["w_down"], F32, "d_act", tb=True)
    (du,), _ = _rw_bwd(fns["glu"], [_row(r["u"])], [], [_row(dact)], [F32], "glu_bwd")
    dup, g["ffn_conv_w"], g["ffn_conv_b"] = _conv_bwd(r["up"], 0, 2 * D_FF, P["ffn_conv_w"], P["ffn_conv_b"], du,
                                                      FFN_CONV, False, npad, "ffn_conv_bwd")
    g["w_up"] = _mm(r["h1"], dup, BF16, "dw_up", ta=True)
    dh1 = _mm(dup, P["w_up"], F32, "d_h1", tb=True, add=dh1_a)
    consts_1 = [_row(P["ln1_g"]), _row(P["ln1_b"])]
    (dh_a, dmo), (g["ln1_g"], g["ln1_b"]) = _rw_bwd(fns["res_ln"], [_row(r["h"]), _row(r["mo"])], consts_1,
                                                    [_row(dh1)], [F32, BF16], "ln1_bwd")
    g["w_out"] = _mm(r["mixed"], dmo, BF16, "dw_out", ta=True)
    dmixed = _mm(dmo, P["w_out"], F32, "d_mixed", tb=True)
    proj = r["proj"]
    rows_c = [_row(proj, D_MODEL, OGA // D_MODEL), _row(proj, D_MODEL, OGS // D_MODEL), _row(r["ya"]), _row(r["ys"])]
    (dga, dgs, dya, dys), _ = _rw_bwd(fns["mix"], rows_c, [], [_row(dmixed)], [BF16] * 4, "mix_bwd")
    g["w_o_attn"] = _mm(r["o"], dya, BF16, "dw_o_attn", ta=True)
    do = _mm(dya, P["w_o_attn"], F32, "d_o", tb=True)
    g["w_o_ssd"] = _mm(r["yn"], dys, BF16, "dw_o_ssd", ta=True)
    dyn = _mm(dys, P["w_o_ssd"], F32, "d_yn", tb=True)
    rows_b = [_row(r["y"], GW, 0, grp=True), _row(r["xbc"], GW, 0, grp=True), _row(proj, GW, OZ // GW, grp=True)]
    consts_b = [_row(P["d_skip"], GW, 0, grp=True), _row(P["ssd_norm_g"], GW, 0, grp=True)]
    (dy, dxs_skip, dz), (g["d_skip"], g["ssd_norm_g"]) = _rw_bwd(
        fns["gated"], rows_b, consts_b, [_row(dyn, GW, 0, grp=True)], [F32, F32, BF16], "ssd_gate_bwd", ng=SSD_GROUPS)
    dxbc, ddt, g["a_log"] = _ssd_bwd(r["xbc"], r["dt"], P["a_log"], tb["expand"], dy, dxs_skip, r["states"], "ssd_bwd")
    dxbc_pre, g["ssd_conv_w"], g["ssd_conv_b"] = _conv_bwd(proj, OXBC, SSD_CONV_DIM, P["ssd_conv_w"], P["ssd_conv_b"],
                                                           dxbc, SSD_CONV, True, npad, "ssd_conv_bwd")
    delta = _attn_delta(do, r["o"], "attn_delta")
    dqr, dkn, dkr8, dv = _flash_bwd(r["qr"], r["kv"], r["kr8"], do, r["lse"], delta, npad, "attn_bwd")
    rows_q = [_row(r["q"], QHEAD, 0, grp=True), _row(tb["cos"], diff=False), _row(tb["sin"], diff=False)]
    (dq,), _ = _rw_bwd(fns["q_post"], rows_q, [_row(tb["rot"], diff=False)], [_row(dqr, QHEAD, 0, grp=True)], [BF16],
                       "q_post_bwd", ng=HEADS)
    g["w_q"] = _mm(r["qn"], dq, BF16, "dw_q", ta=True)
    dqn = _mm(dq, P["w_q"], F32, "d_qn", tb=True)
    dkv = jnp.concatenate([dkn, dv], axis=1)
    g["w_kv"] = _mm(r["kvn"], dkv, BF16, "dw_kv", ta=True)
    dkvn = _mm(dkv, P["w_kv"], F32, "d_kvn", tb=True)
    rows_a = _layer_rows(proj, tb)
    consts_a = [_row(tb["rot"], diff=False), _row(P["q_norm_g"]), _row(P["kv_norm_g"]), _row(P["dt_bias"])]
    (dql, dkvl, dkpe, ddtr), (g["q_norm_g"], g["kv_norm_g"], g["dt_bias"]) = _rw_bwd(
        fns["in_post"], rows_a, consts_a, [_row(dqn), _row(dkvn), _row(dkr8), _row(ddt)], [BF16] * 4, "in_post_bwd")
    dproj = jnp.concatenate([dql, dkvl, dz, dxbc_pre, dga, dgs, dkpe, ddtr], axis=1)
    g["w_in"] = _mm(r["h"], dproj, BF16, "dw_in", ta=True)
    dh = _mm(dproj, P["w_in"], F32, "d_h", tb=True, add=dh_a)
    return dh, g


def _full_weight(g, axis, i):
    if axis == 1:
        return g[:, i].reshape(-1, g.shape[-1])
    return jnp.concatenate([g[p, i] for p in range(N_DEV)], axis=1)


def _grad_pieces(d, axis):
    if axis == 1:
        return d.reshape(N_DEV, -1, d.shape[1])
    return jnp.transpose(d.reshape(d.shape[0], N_DEV, -1), (1, 0, 2))


def _layer_params(gathered, small, i):
    full = {n: _full_weight(gathered[n], BIG[n], i) for n in BIG}
    P = {}
    P["w_in"] = _in_proj_pad(full["w_in"])
    P["w_q"] = _q_pad(full["w_q_b"])
    P["w_kv"] = _kv_perm(full["w_kv_b"])
    for n in ("w_o_attn", "w_o_ssd", "w_out", "w_up", "w_down"):
        P[n] = full[n]
    P["q_norm_g"] = _row_vec(small["q_norm_g"][i])
    P["kv_norm_g"] = _row_vec(small["kv_norm_g"][i])
    P["dt_bias"] = _row_vec(small["dt_bias"][i], LANES)
    P["a_log"] = _row_vec(small["a_log"][i], LANES)
    P["d_skip"] = _row_vec(jnp.repeat(small["d_skip"][i], SSD_HEAD_DIM))
    P["ssd_norm_g"] = _row_vec(small["ssd_norm_g"][i])
    P["ssd_conv_w"] = _pad_rows8(small["ssd_conv_w"][i])
    P["ssd_conv_b"] = _row_vec(small["ssd_conv_b"][i])
    P["ffn_conv_w"] = _pad_rows8(small["ffn_conv_w"][i])
    P["ffn_conv_b"] = _row_vec(small["ffn_conv_b"][i])
    for n in ("ln1_g", "ln1_b", "ln2_g", "ln2_b"):
        P[n] = _row_vec(small[n][i])
    return P


def _layer_grads_to_reference_layout(g):
    out = {}
    out["w_in"] = _in_proj_unpad(g["w_in"])
    out["w_q_b"] = _q_unpad(g["w_q"])
    out["w_kv_b"] = _kv_unperm(g["w_kv"])
    for n in ("w_o_attn", "w_o_ssd", "w_out", "w_up", "w_down"):
        out[n] = g[n]
    out["q_norm_g"] = g["q_norm_g"][0]
    out["kv_norm_g"] = g["kv_norm_g"][0]
    out["dt_bias"] = g["dt_bias"][0, :SSD_HEADS]
    out["a_log"] = g["a_log"][0, :SSD_HEADS]
    out["d_skip"] = g["d_skip"].reshape(SSD_HEADS, SSD_HEAD_DIM).sum(axis=1)
    out["ssd_norm_g"] = g["ssd_norm_g"][0]
    out["ssd_conv_w"] = g["ssd_conv_w"][:SSD_CONV]
    out["ssd_conv_b"] = g["ssd_conv_b"][0]
    out["ffn_conv_w"] = g["ffn_conv_w"][:FFN_CONV]
    out["ffn_conv_b"] = g["ffn_conv_b"][0]
    for n in ("ln1_g", "ln1_b", "ln2_g", "ln2_b"):
        out[n] = g[n][0]
    return out


def kernel(x, meta_tokens, emb_ln_g, emb_ln_b, w_in, q_norm_g, w_q_b, kv_norm_g, w_kv_b, w_o_attn, ssd_conv_w, ssd_conv_b, dt_bias, a_log, d_skip, ssd_norm_g, w_o_ssd, w_out, ln1_g, ln1_b, w_up, ffn_conv_w, ffn_conv_b, w_down, ln2_g, ln2_b, loss_target, m_meta_tokens, m_emb_ln_g, m_emb_ln_b, m_w_in, m_q_norm_g, m_w_q_b, m_kv_norm_g, m_w_kv_b, m_w_o_attn, m_ssd_conv_w, m_ssd_conv_b, m_dt_bias, m_a_log, m_d_skip, m_ssd_norm_g, m_w_o_ssd, m_w_out, m_ln1_g, m_ln1_b, m_w_up, m_ffn_conv_w, m_ffn_conv_b, m_w_down, m_ln2_g, m_ln2_b, v_meta_tokens, v_emb_ln_g, v_emb_ln_b, v_w_in, v_q_norm_g, v_w_q_b, v_kv_norm_g, v_w_kv_b, v_w_o_attn, v_ssd_conv_w, v_ssd_conv_b, v_dt_bias, v_a_log, v_d_skip, v_ssd_norm_g, v_w_o_ssd, v_w_out, v_ln1_g, v_ln1_b, v_w_up, v_ffn_conv_w, v_ffn_conv_b, v_w_down, v_ln2_g, v_ln2_b):
    given = dict(locals())
    w = {n: given[n] for n in WEIGHTS}
    m = {n: given["m_" + n] for n in WEIGHTS}
    v = {n: given["v_" + n] for n in WEIGHTS}
    seq = x.shape[1]
    assert x.shape[0] == 1 and seq % LANES == 0
    npad = LANES - N_META
    Tp = npad + N_META + seq
    depth = w_in.shape[0]

    big_names, small_names = list(BIG), list(SMALL_SHARDED)
    ws, offs_s = _flatten([w[n] for n in small_names], SMALL_COLS, SUBLANES)
    got = _allgather([w[n].astype(BF16) for n in big_names] + [ws], "weight_allgather")
    gathered = dict(zip(big_names, got[:-1]))
    gsm = got[-1]
    small = {n: w[n] for n in REPLICATED}
    for n, (o, sz) in zip(small_names, offs_s):
        small[n] = _from_pieces(gsm.reshape(N_DEV, -1)[:, o:o + sz], w[n].shape, SMALL_SHARDED[n])

    fns = _make_stage_fns(npad)
    cos, sin, rot, expand = _tables(Tp, npad)
    tb = dict(cos=cos, sin=sin, rot=rot, expand=expand)
    top = jnp.pad(small["meta_tokens"], ((npad, 0), (0, 0)))
    hcat = jnp.concatenate([top, x[0]], axis=0)
    consts_e = [_row(_row_vec(w["emb_ln_g"])), _row(_row_vec(w["emb_ln_b"]))]
    h = _rw_fwd(fns["ln"], [_row(hcat)], consts_e, [_out(D_MODEL, F32)], "emb_ln")[0]
    layers = [_layer_params(gathered, small, i) for i in range(depth)]
    saved = []
    for i in range(depth):
        h, res = _layer_fwd(h, layers[i], tb, fns, npad)
        saved.append(res)
    dh, lparts = _loss_head(h, loss_target[0], "loss_head")
    loss = lax.psum(jnp.sum(lparts[:, 0, 0]), ("x", "y", "c"))

    lg = [None] * depth
    for i in reversed(range(depth)):
        dh, gi = _layer_bwd(dh, saved[i], layers[i], tb, fns, npad)
        lg[i] = _layer_grads_to_reference_layout(gi)
    (dhcat,), (d_emb_g, d_emb_b) = _rw_bwd(fns["ln"], [_row(hcat)], consts_e, [_row(dh)], [F32], "emb_ln_bwd")
    grad_x = dhcat[LANES:][None]
    local = {n: jnp.stack([lg[i][n] for i in range(depth)]) for n in lg[0] if n not in BIG}
    local["meta_tokens"] = dhcat[npad:LANES]
    local["emb_ln_g"] = d_emb_g[0]
    local["emb_ln_b"] = d_emb_b[0]

    pieces = [jnp.stack([_grad_pieces(lg[i][n], BIG[n]) for i in range(depth)], axis=1).astype(BF16) for n in big_names]
    sm_names = small_names + REPLICATED
    sm_pieces = [_to_pieces(local[n], SMALL_SHARDED[n]) for n in small_names]
    sm_pieces += [jnp.broadcast_to(local[n].reshape(1, -1), (N_DEV, local[n].size)) for n in REPLICATED]
    ps, _ = _flatten(sm_pieces, SMALL_COLS, BF16_ROWS, lead=True)
    pieces = pieces + [ps]
    core = lax.axis_index("c")
    from_sibling = _sibling_exchange(pieces, "grad_exchange_cores")
    chip_sums = []
    for k, (p, r) in enumerate(zip(pieces, from_sibling)):
        own = lax.dynamic_index_in_dim(p.reshape((N_CHIPS, 2) + p.shape[1:]), core, axis=1, keepdims=False)
        chip_sums.append(_add_pairs(own, r, "grad_chip_sum_%d" % k))
    recv = _chip_exchange(chip_sums, "grad_exchange_chips")
    outs = {}
    kinds = ("grad", "delta", "new_m", "new_v")
    for n, r in zip(big_names, recv[:-1]):
        for kind, a in zip(kinds, _adamw(r, w[n], m[n], v[n], "adamw_" + n)):
            outs[kind + "_" + n] = a
    wf, offs = _flatten([w[n] for n in sm_names], SMALL_COLS, BF16_ROWS)
    mf, _ = _flatten([m[n] for n in sm_names], SMALL_COLS, BF16_ROWS)
    vf, _ = _flatten([v[n] for n in sm_names], SMALL_COLS, BF16_ROWS)
    shapes = [w[n].shape for n in sm_names]
    for kind, flat in zip(kinds, _adamw(recv[-1], wf, mf, vf, "adamw_small")):
        for n, a in zip(sm_names, _unflatten(flat, offs, shapes)):
            outs[kind + "_" + n] = a
    result = [loss, grad_x]
    for kind in ("grad", "delta", "new_m", "new_v"):
        result += [outs[kind + "_" + n] for n in WEIGHTS]
    return tuple(result)
```

```python
import functools

import jax
import jax.numpy as jnp
import numpy as np
from jax import lax
from jax.experimental import pallas as pl
from jax.experimental.pallas import tpu as pltpu

F32 = jnp.float32
BF16 = jnp.bfloat16
HIGHEST = lax.Precision.HIGHEST
SSD_PREC = lax.Precision.HIGH

D_MODEL = 1024
DEPTH = 2
N_META = 16
HEADS = 8
Q_LORA = 768
KV_LORA = 256
QK_NOPE = 128
QK_ROPE = 64
V_HEAD = 128
ROPE_THETA = 10000.0
SSD_INNER = 2048
SSD_HEAD_DIM = 64
SSD_HEADS = 32
SSD_GROUPS = 4
SSD_STATE = 128
SSD_CONV = 4
SSD_CONV_DIM = SSD_INNER + 2 * SSD_GROUPS * SSD_STATE
CHUNK = 128
D_FF = 2816
FFN_CONV = 3
LN_EPS = 1e-5
RMS_EPS = 1e-6
ALPHA = (2 * DEPTH) ** 0.25
IN_SIZES = (Q_LORA, KV_LORA, QK_ROPE, SSD_INNER, SSD_CONV_DIM, SSD_HEADS, D_MODEL, D_MODEL)
ATT_SCALE = (QK_NOPE + QK_ROPE) ** -0.5
NEG_INF = -1e30
ADAM_LR, ADAM_B1, ADAM_B2, ADAM_EPS, ADAM_WD, ADAM_STEP = 0.001, 0.9, 0.999, 1e-08, 0.01, 10

LANES = 128
SUBLANES = 8
VMEM_BYTES = 64 * 1024 * 1024
N_DEV = 8

OQ, OKV, OZ, OXBC, OGA, OGS, OKPE, ODT = 0, 768, 1024, 3072, 6144, 7168, 8192, 8320
IN_PAD = 8448
QHEAD = 256

ROW_TILE = 640
MM_COL_TILE = 1408
MM_K_TILE = 2816
MM_TOKEN_K_TILE = 1664
ATT_TILE = 640
ATT_HEADS_PER_STEP = 4
BF16_ROWS = 16
ROW_BUDGET = 7 * 1024 * 1024
ADAM_ELEMS = 160 * 1024


def _pick(n, target, q=LANES):
    assert n % q == 0, (n, q)
    units = n // q
    best = q
    for d in range(1, units + 1):
        if units % d == 0 and d * q <= target:
            best = d * q
    return best


def _pick_rows(n, row_bytes):
    return _pick(n, max(BF16_ROWS, ROW_BUDGET // row_bytes), BF16_ROWS)


def _params(sem, est_bytes):
    limit = int(min(VMEM_BYTES - (6 << 20), max(32 << 20, 2 * est_bytes + (8 << 20))))
    return pltpu.CompilerParams(dimension_semantics=sem, vmem_limit_bytes=limit)


def _nbytes(shape, dtype):
    return int(np.prod(shape)) * jnp.dtype(dtype).itemsize


def _mm(a, b, out_dtype, name, ta=False, tb=False, add=None):
    assert not (ta and tb)
    if ta:
        K, M = a.shape
        tm = _pick(M, MM_COL_TILE)
        tk = _pick(K, MM_TOKEN_K_TILE)
    else:
        M, K = a.shape
        tm = _pick(M, ROW_TILE)
        tk = _pick(K, MM_K_TILE)
    N, K2 = (b.shape if tb else b.shape[::-1])
    assert K == K2
    tn = _pick(N, MM_COL_TILE)
    nk = K // tk
    dn = (((0,), (0,)), ((), ())) if ta else ((((1,), (1,)), ((), ())) if tb else (((1,), (0,)), ((), ())))

    def body(*refs):
        a_ref, b_ref = refs[:2]
        add_ref = refs[2] if add is not None else None
        o_ref = refs[2 + (add is not None)]
        d = lax.dot_general(a_ref[...].astype(BF16), b_ref[...].astype(BF16), dn, preferred_element_type=F32)

        def finish(r):
            if add is not None:
                r = r + add_ref[...].astype(F32)
            o_ref[...] = r.astype(out_dtype)

        if nk == 1:
            finish(d)
            return
        acc = refs[-1]
        k = pl.program_id(2)

        @pl.when(k == 0)
        def _():
            acc[...] = d

        @pl.when((k > 0) & (k < nk - 1))
        def _():
            acc[...] += d

        @pl.when(k == nk - 1)
        def _():
            finish(acc[...] + d)

    if ta:
        a_spec = pl.BlockSpec((tk, tm), lambda i, j, k: (k, i))
    else:
        a_spec = pl.BlockSpec((tm, tk), lambda i, j, k: (i, k))
    b_spec = pl.BlockSpec((tn, tk), lambda i, j, k: (j, k)) if tb else pl.BlockSpec((tk, tn), lambda i, j, k: (k, j))
    in_specs = [a_spec, b_spec]
    args = [a, b]
    est = 2 * (tm * tk * a.dtype.itemsize + tk * tn * b.dtype.itemsize + tm * tn * 4) + tm * tn * 4
    if add is not None:
        in_specs.append(pl.BlockSpec((tm, tn), lambda i, j, k: (i, j)))
        args.append(add)
        est += 2 * tm * tn * 4
    return pl.pallas_call(
        body, name=name, grid=(M // tm, N // tn, nk), in_specs=in_specs,
        out_specs=pl.BlockSpec((tm, tn), lambda i, j, k: (i, j)),
        out_shape=jax.ShapeDtypeStruct((M, N), out_dtype),
        scratch_shapes=[pltpu.VMEM((tm, tn), F32)] if nk > 1 else [],
        compiler_params=_params(("parallel", "parallel", "arbitrary"), est),
    )(*args)


def _row(arr, bw=None, cb=0, grp=False, diff=True):
    return dict(arr=arr, bw=arr.shape[1] if bw is None else bw, cb=cb, grp=grp, diff=diff)


def _out(width, dtype, bw=None, grp=False):
    return dict(width=width, dtype=dtype, bw=width if bw is None else bw, grp=grp)


def _spec_rows(d, tm):
    return pl.BlockSpec((tm, d["bw"]), lambda g, i, cb=d["cb"], gr=d["grp"]: (i, cb + (g if gr else 0)))


def _spec_const(d):
    return pl.BlockSpec((d["arr"].shape[0], d["bw"]), lambda g, i, cb=d["cb"], gr=d["grp"]: (0, cb + (g if gr else 0)))


def _rw_fwd(fn, rows, consts, outs, name, ng=1):
    Tp = rows[0]["arr"].shape[0]
    tm = _pick_rows(Tp, 4 * (sum(d["bw"] for d in rows) + 2 * sum(o["bw"] for o in outs)))
    nr, ncst = len(rows), len(consts)

    def body(*refs):
        i = pl.program_id(1)
        rowidx = i * tm + lax.broadcasted_iota(jnp.int32, (tm, 1), 0)
        rv = [r[...].astype(F32) for r in refs[:nr]]
        cv = [c[...] for c in refs[nr:nr + ncst]]
        vals = fn(rowidx, *rv, *cv)
        for o, v in zip(refs[nr + ncst:], vals):
            o[...] = v.astype(o.dtype)

    est = sum(tm * d["bw"] * 4 for d in rows) + sum(tm * o["bw"] * 4 for o in outs)
    return pl.pallas_call(
        body, name=name, grid=(ng, Tp // tm),
        in_specs=[_spec_rows(d, tm) for d in rows] + [_spec_const(d) for d in consts],
        out_specs=[pl.BlockSpec((tm, o["bw"]), lambda g, i, gr=o["grp"]: (i, g if gr else 0)) for o in outs],
        out_shape=[jax.ShapeDtypeStruct((Tp, o["width"]), o["dtype"]) for o in outs],
        compiler_params=_params(("parallel", "parallel"), 3 * est),
    )(*[d["arr"] for d in rows], *[d["arr"] for d in consts])


def _rw_bwd(fn, rows, consts, cots, drow_dtypes, name, ng=1):
    Tp = rows[0]["arr"].shape[0]
    tm = _pick_rows(Tp, 4 * (3 * sum(d["bw"] for d in rows) + 2 * sum(d["bw"] for d in cots)))
    nr, ncst, nct = len(rows), len(consts), len(cots)
    drows = [k for k, d in enumerate(rows) if d["diff"]]
    dcsts = [k for k, d in enumerate(consts) if d["diff"]]
    for k in drows:
        assert rows[k]["grp"] or ng == 1

    def body(*refs):
        g = pl.program_id(0)
        i = pl.program_id(1)
        rowidx = i * tm + lax.broadcasted_iota(jnp.int32, (tm, 1), 0)
        rv = [r[...].astype(F32) for r in refs[:nr]]
        cv = [c[...] for c in refs[nr:nr + ncst]]
        ct = tuple(r[...].astype(F32) for r in refs[nr + ncst:nr + ncst + nct])
        orefs = refs[nr + ncst + nct:]

        def f(*dargs):
            rr, cc = list(rv), list(cv)
            for k, v in zip(drows, dargs[:len(drows)]):
                rr[k] = v
            for k, v in zip(dcsts, dargs[len(drows):]):
                cc[k] = v
            return tuple(fn(rowidx, *rr, *cc))

        _, vjp = jax.vjp(f, *[rv[k] for k in drows], *[cv[k] for k in dcsts])
        grads = vjp(ct)
        for o, v in zip(orefs[:len(drows)], grads[:len(drows)]):
            o[...] = v.astype(o.dtype)
        for k, o, v in zip(dcsts, orefs[len(drows):], grads[len(drows):]):
            first = (i == 0) if consts[k]["grp"] else ((i == 0) & (g == 0))

            @pl.when(first)
            def _(o=o, v=v):
                o[...] = v

            @pl.when(jnp.logical_not(first))
            def _(o=o, v=v):
                o[...] += v

    out_specs, out_shape = [], []
    for k, dt in zip(drows, drow_dtypes):
        d = rows[k]
        out_specs.append(pl.BlockSpec((tm, d["bw"]), lambda g, i, gr=d["grp"]: (i, g if gr else 0)))
        out_shape.append(jax.ShapeDtypeStruct((Tp, d["bw"] * (ng if d["grp"] else 1)), dt))
    for k in dcsts:
        d = consts[k]
        r = d["arr"].shape[0]
        out_specs.append(pl.BlockSpec((r, d["bw"]), lambda g, i, gr=d["grp"]: (0, g if gr else 0)))
        out_shape.append(jax.ShapeDtypeStruct((r, d["bw"] * (ng if d["grp"] else 1)), F32))
    est = sum(tm * d["bw"] * 4 for d in rows) * 2 + sum(tm * d["bw"] * 4 for d in cots)
    res = pl.pallas_call(
        body, name=name, grid=(ng, Tp // tm),
        in_specs=[_spec_rows(d, tm) for d in rows] + [_spec_const(d) for d in consts] + [_spec_rows(d, tm) for d in cots],
        out_specs=out_specs, out_shape=out_shape,
        compiler_params=_params(("arbitrary", "arbitrary"), 3 * est),
    )(*[d["arr"] for d in rows], *[d["arr"] for d in consts], *[d["arr"] for d in cots])
    return list(res[:len(drows)]), list(res[len(drows):])


def _sigmoid(x):
    return 0.5 * jnp.tanh(0.5 * x) + 0.5


def _silu(x):
    return x * _sigmoid(x)


def _softplus(x):
    return jnp.maximum(x, 0.0) + jnp.log(1.0 + jnp.exp(-jnp.abs(x)))


def _layer_norm(x, g, b):
    mu = jnp.mean(x, axis=-1, keepdims=True)
    xc = x - mu
    var = jnp.mean(xc * xc, axis=-1, keepdims=True)
    return xc * lax.rsqrt(var + LN_EPS) * g + b


def _rms_norm(x, g):
    return x * lax.rsqrt(jnp.mean(x * x, axis=-1, keepdims=True) + RMS_EPS) * g


def _rope(r, cos, sin, rot):
    return r * cos + jnp.dot(r, rot, precision=HIGHEST, preferred_element_type=F32) * sin


def _make_stage_fns(npad):
    def fn_ln_masked(rowidx, x, g, b):
        return (jnp.where(rowidx >= npad, _layer_norm(x, g, b), 0.0),)

    def fn_in_post(rowidx, ql, kvl, kpe, dtr, cos, sin, rot, qg, kvg, dtb):
        qn = _rms_norm(ql, qg)
        kvn = _rms_norm(kvl, kvg)
        kr = _rope(kpe, cos, sin, rot)
        lane = lax.broadcasted_iota(jnp.int32, (1, LANES), 1)
        dt = jnp.where((rowidx >= npad) & (lane < SSD_HEADS), _softplus(dtr + dtb), 0.0)
        return qn, kvn, jnp.concatenate([kr] * HEADS, axis=1), dt

    def fn_q_post(rowidx, q, cos, sin, rot):
        rr = _rope(q[:, QK_NOPE:], cos, sin, rot)
        return (jnp.concatenate([q[:, :QK_NOPE], rr], axis=1) * ATT_SCALE,)

    def fn_gated_norm(rowidx, y, xs, z, dskip, g):
        v = (y + xs * dskip) * _silu(z)
        return (v * lax.rsqrt(jnp.mean(v * v, axis=-1, keepdims=True) + RMS_EPS) * g,)

    def fn_mix(rowidx, ga, gs, ya, ys):
        return (_sigmoid(ga) * ya + _sigmoid(gs) * ys,)

    def fn_res_ln(rowidx, h, r, g, b):
        return (jnp.where(rowidx >= npad, _layer_norm(ALPHA * h + r, g, b), 0.0),)

    def fn_glu(rowidx, u):
        return (_silu(u[:, :D_FF]) * u[:, D_FF:],)

    return dict(ln=fn_ln_masked, in_post=fn_in_post, q_post=fn_q_post, gated=fn_gated_norm, mix=fn_mix,
                res_ln=fn_res_ln, glu=fn_glu)


def _conv_tiles(Tp, C):
    return _pick(Tp, ROW_TILE), _pick(C, MM_COL_TILE)


def _conv_fwd(x, xoff, C, w8, b, K, act, npad, name):
    Tp = x.shape[0]
    tm, tc = _conv_tiles(Tp, C)
    assert xoff % tc == 0
    cb0 = xoff // tc
    rb = tm // SUBLANES

    def body(prev_ref, main_ref, w_ref, b_ref, o_ref):
        i = pl.program_id(1)
        main = main_ref[...].astype(F32)
        prev = jnp.where(i > 0, prev_ref[...].astype(F32), 0.0)
        ext = jnp.concatenate([prev, main], axis=0)
        acc = b_ref[...] + w_ref[K - 1:K, :] * main
        for k in range(K - 1):
            s = K - 1 - k
            acc = acc + w_ref[k:k + 1, :] * pltpu.roll(ext, s, 0)[SUBLANES:, :]
        if act:
            rowidx = i * tm + lax.broadcasted_iota(jnp.int32, (tm, 1), 0)
            acc = jnp.where(rowidx >= npad, _silu(acc), 0.0)
        o_ref[...] = acc.astype(o_ref.dtype)

    return pl.pallas_call(
        body, name=name, grid=(C // tc, Tp // tm),
        in_specs=[pl.BlockSpec((SUBLANES, tc), lambda g, i: (jnp.maximum(i * rb - 1, 0), cb0 + g)),
                  pl.BlockSpec((tm, tc), lambda g, i: (i, cb0 + g)),
                  pl.BlockSpec((SUBLANES, tc), lambda g, i: (0, g)),
                  pl.BlockSpec((1, tc), lambda g, i: (0, g))],
        out_specs=pl.BlockSpec((tm, tc), lambda g, i: (i, g)),
        out_shape=jax.ShapeDtypeStruct((Tp, C), F32),
        compiler_params=_params(("parallel", "parallel"), 8 * tm * tc * 4),
    )(x, x, w8, b)


def _conv_bwd(x, xoff, C, w8, b, dy, K, act, npad, name):
    Tp = x.shape[0]
    tm, tc = _conv_tiles(Tp, C)
    cb0 = xoff // tc
    rb = tm // SUBLANES
    ni = Tp // tm
    last_rb = Tp // SUBLANES - 1
    n = tm + 2 * SUBLANES

    def body(xp_ref, xm_ref, xn_ref, dym_ref, dyn_ref, w_ref, b_ref, dx_ref, dw_ref, db_ref):
        i = pl.program_id(1)
        prev = jnp.where(i > 0, xp_ref[...].astype(F32), 0.0)
        ext = jnp.concatenate([prev, xm_ref[...].astype(F32), xn_ref[...].astype(F32)], axis=0)
        dyn = jnp.where(i < ni - 1, dyn_ref[...].astype(F32), 0.0)
        dpre = jnp.concatenate([jnp.zeros((SUBLANES, tc), F32), dym_ref[...].astype(F32), dyn], axis=0)
        shifted = [ext if k == K - 1 else pltpu.roll(ext, K - 1 - k, 0) for k in range(K)]
        if act:
            pre = b_ref[...] + sum(w_ref[k:k + 1, :] * shifted[k] for k in range(K))
            rowidx = i * tm - SUBLANES + lax.broadcasted_iota(jnp.int32, (n, 1), 0)
            sg = _sigmoid(pre)
            dpre = jnp.where(rowidx >= npad, dpre * sg * (1.0 + pre * (1.0 - sg)), 0.0)
        dx = w_ref[K - 1:K, :] * dpre
        for k in range(K - 1):
            dx = dx + w_ref[k:k + 1, :] * pltpu.roll(dpre, n - (K - 1 - k), 0)
        dx_ref[...] = dx[SUBLANES:SUBLANES + tm, :].astype(dx_ref.dtype)

        @pl.when(i == 0)
        def _():
            dw_ref[...] = jnp.zeros_like(dw_ref)
            db_ref[...] = jnp.zeros_like(db_ref)

        dmain = dpre[SUBLANES:SUBLANES + tm, :]
        for k in range(K):
            dw_ref[k:k + 1, :] += jnp.sum(dmain * shifted[k][SUBLANES:SUBLANES + tm, :], axis=0, keepdims=True)
        db_ref[...] += jnp.sum(dmain, axis=0, keepdims=True)

    return pl.pallas_call(
        body, name=name, grid=(C // tc, ni),
        in_specs=[pl.BlockSpec((SUBLANES, tc), lambda g, i: (jnp.maximum(i * rb - 1, 0), cb0 + g)),
                  pl.BlockSpec((tm, tc), lambda g, i: (i, cb0 + g)),
                  pl.BlockSpec((SUBLANES, tc), lambda g, i: (jnp.minimum((i + 1) * rb, last_rb), cb0 + g)),
                  pl.BlockSpec((tm, tc), lambda g, i: (i, g)),
                  pl.BlockSpec((SUBLANES, tc), lambda g, i: (jnp.minimum((i + 1) * rb, last_rb), g)),
                  pl.BlockSpec((SUBLANES, tc), lambda g, i: (0, g)),
                  pl.BlockSpec((1, tc), lambda g, i: (0, g))],
        out_specs=[pl.BlockSpec((tm, tc), lambda g, i: (i, g)),
                   pl.BlockSpec((SUBLANES, tc), lambda g, i: (0, g)),
                   pl.BlockSpec((1, tc), lambda g, i: (0, g))],
        out_shape=[jax.ShapeDtypeStruct((Tp, C), BF16), jax.ShapeDtypeStruct((SUBLANES, C), F32),
                   jax.ShapeDtypeStruct((1, C), F32)],
        compiler_params=_params(("parallel", "arbitrary"), 14 * tm * tc * 4),
    )(x, x, x, dy, dy, w8, b)


def _split_refs(refs, n_in, n_out, n_scratch, nbg):
    cuts = np.cumsum([0, n_in, nbg, n_out, nbg, n_scratch])
    return tuple(refs[a:b] for a, b in zip(cuts[:-1], cuts[1:])) + (refs[cuts[-1]:],)


def _flash_fwd(q, kv, kr8, npad, name, bg=None):
    Tp = q.shape[0]
    t = _pick(Tp, ATT_TILE)
    hp = ATT_HEADS_PER_STEP
    nb = Tp // t
    ng = HEADS // hp
    nbg = bg.n if bg else 0
    nt = (((1,), (1,)), ((), ()))
    tn = (((0,), (0,)), ((), ()))

    def body(*refs):
        (q_ref, kn_ref, kr_ref, v_ref), bg_in, (o_ref, lse_ref), bg_out, (m_sc, l_sc, acc_sc), bg_sems = _split_refs(
            refs, 4, 2, 3, nbg)
        g = pl.program_id(0)
        qi = pl.program_id(1)
        ki = pl.program_id(2)
        if bg:
            @pl.when((g == 0) & (qi == 0) & (ki == 0))
            def _():
                bg.start(bg_in, bg_out, bg_sems)

        @pl.when(ki == 0)
        def _():
            m_sc[...] = jnp.full_like(m_sc, NEG_INF)
            l_sc[...] = jnp.zeros_like(l_sc)
            acc_sc[...] = jnp.zeros_like(acc_sc)

        def step(masked):
            kr = kr_ref[...]
            if masked:
                key = ki * t + lax.broadcasted_iota(jnp.int32, (t, t), 0)
                qry = qi * t + lax.broadcasted_iota(jnp.int32, (t, t), 1)
                visible = (key <= qry) & (key >= npad)
            for hh in range(hp):
                k = jnp.concatenate([kn_ref[:, hh * QK_NOPE:(hh + 1) * QK_NOPE], kr], axis=1)
                st = lax.dot_general(k, q_ref[:, hh * QHEAD:(hh + 1) * QHEAD], nt, preferred_element_type=F32)
                if masked:
                    st = jnp.where(visible, st, NEG_INF)
                vs = slice(hh * V_HEAD, (hh + 1) * V_HEAD)
                m_prev = m_sc[hh]
                m_new = jnp.maximum(m_prev, jnp.max(st, axis=0, keepdims=True))
                pt = jnp.exp(st - m_new)
                a = jnp.exp(m_prev - m_new)
                l_sc[hh] = a * l_sc[hh] + jnp.sum(pt, axis=0, keepdims=True)
                acc_sc[vs, :] = a * acc_sc[vs, :] + lax.dot_general(v_ref[:, vs], pt.astype(BF16), tn,
                                                                    preferred_element_type=F32)
                m_sc[hh] = m_new

        need_mask = (ki == qi) | (ki == 0)

        @pl.when((ki <= qi) & need_mask)
        def _():
            step(True)

        @pl.when((ki <= qi) & jnp.logical_not(need_mask))
        def _():
            step(False)

        @pl.when(ki == qi)
        def _():
            for hh in range(hp):
                vs = slice(hh * V_HEAD, (hh + 1) * V_HEAD)
                l = l_sc[hh]
                o_ref[:, vs] = (acc_sc[vs, :] / l).T.astype(o_ref.dtype)
                lse_ref[hh * SUBLANES:(hh + 1) * SUBLANES, :] = jnp.broadcast_to(m_sc[hh] + jnp.log(l), (SUBLANES, t))

        if bg:
            @pl.when((g == ng - 1) & (qi == nb - 1) & (ki == nb - 1))
            def _():
                bg.wait(bg_in, bg_out, bg_sems)

    kmin = lambda qi, ki: jnp.minimum(ki, qi)
    return pl.pallas_call(
        body, name=name, grid=(ng, nb, nb),
        in_specs=[pl.BlockSpec((t, hp * QHEAD), lambda g, qi, ki: (qi, g)),
                  pl.BlockSpec((t, hp * QK_NOPE), lambda g, qi, ki: (kmin(qi, ki), g)),
                  pl.BlockSpec((t, LANES), lambda g, qi, ki: (kmin(qi, ki), 0)),
                  pl.BlockSpec((t, hp * V_HEAD), lambda g, qi, ki: (kmin(qi, ki), ng + g))] + (bg.specs if bg else []),
        out_specs=[pl.BlockSpec((t, hp * V_HEAD), lambda g, qi, ki: (qi, g)),
                   pl.BlockSpec((hp * SUBLANES, t), lambda g, qi, ki: (g, qi))] + (bg.specs if bg else []),
        out_shape=[jax.ShapeDtypeStruct((Tp, HEADS * V_HEAD), F32), jax.ShapeDtypeStruct((HEADS * SUBLANES, Tp), F32)]
        + (bg.out_shape if bg else []),
        scratch_shapes=[pltpu.VMEM((hp, 1, t), F32), pltpu.VMEM((hp, 1, t), F32), pltpu.VMEM((hp * V_HEAD, t), F32)]
        + (bg.scratch if bg else []),
        compiler_params=_params(("arbitrary",) * 3 if bg else ("parallel", "parallel", "arbitrary"), 8 * hp * t * t * 4),
    )(q, kv, kr8, kv, *(bg.arrs if bg else []))


def _attn_delta(do, o, name):
    Tp = do.shape[0]
    tm = _pick(Tp, MM_TOKEN_K_TILE)

    def body(do_ref, o_ref, d_ref):
        prod = do_ref[...] * o_ref[...]
        ones = jnp.ones((SUBLANES, V_HEAD), F32)
        d_ref[...] = lax.dot_general(ones, prod, (((1,), (1,)), ((), ())), precision=HIGHEST,
                                     preferred_element_type=F32)

    return pl.pallas_call(
        body, name=name, grid=(HEADS, Tp // tm),
        in_specs=[pl.BlockSpec((tm, V_HEAD), lambda h, i: (i, h)), pl.BlockSpec((tm, V_HEAD), lambda h, i: (i, h))],
        out_specs=pl.BlockSpec((SUBLANES, tm), lambda h, i: (h, i)),
        out_shape=jax.ShapeDtypeStruct((HEADS * SUBLANES, Tp), F32),
        compiler_params=_params(("parallel", "parallel"), 4 * tm * V_HEAD * 4),
    )(do, o)


def _flash_bwd(q, kv, kr8, do, lse, delta, npad, name, bg=None):
    Tp = q.shape[0]
    t = _pick(Tp, ATT_TILE)
    nb = Tp // t
    nbg = bg.n if bg else 0
    nt = (((1,), (1,)), ((), ()))
    tn = (((0,), (0,)), ((), ()))

    def body(*refs):
        ((q_ref, kn_ref, kr_ref, v_ref, do_ref, lse_ref, dl_ref), bg_in, (dq_ref, dkn_ref, dkr_ref, dv_ref), bg_out,
         (dk_sc, dv_sc), bg_sems) = _split_refs(refs, 7, 4, 2, nbg)
        h = pl.program_id(0)
        ki = pl.program_id(1)
        qi = pl.program_id(2)
        if bg:
            @pl.when((h == 0) & (ki == 0) & (qi == 0))
            def _():
                bg.start(bg_in, bg_out, bg_sems)

        @pl.when(qi == 0)
        def _():
            dk_sc[...] = jnp.zeros_like(dk_sc)
            dv_sc[...] = jnp.zeros_like(dv_sc)

        def step(masked):
            qv = q_ref[...]
            k = jnp.concatenate([kn_ref[...], kr_ref[...]], axis=1)
            st = lax.dot_general(k, qv, nt, preferred_element_type=F32)
            if masked:
                key = ki * t + lax.broadcasted_iota(jnp.int32, (t, t), 0)
                qry = qi * t + lax.broadcasted_iota(jnp.int32, (t, t), 1)
                st = jnp.where((key <= qry) & (key >= npad), st, NEG_INF)
            pt = jnp.exp(st - lse_ref[0:1, :])
            dob = do_ref[...].astype(BF16)
            dv_sc[...] += jnp.dot(pt.astype(BF16), dob, preferred_element_type=F32)
            dpt = lax.dot_general(v_ref[...], dob, nt, preferred_element_type=F32)
            dst = (pt * (dpt - dl_ref[0:1, :])).astype(BF16)
            dk_sc[...] += jnp.dot(dst, qv, preferred_element_type=F32)
            dqc = lax.dot_general(dst, k, tn, preferred_element_type=F32)
            rows = pl.ds(pl.multiple_of(qi * t, t), t)

            @pl.when(ki == 0)
            def _():
                dq_ref[rows, :] = dqc

            @pl.when(ki > 0)
            def _():
                dq_ref[rows, :] += dqc

        need_mask = (ki == qi) | (ki == 0)

        @pl.when((qi >= ki) & need_mask)
        def _():
            step(True)

        @pl.when((qi >= ki) & jnp.logical_not(need_mask))
        def _():
            step(False)

        @pl.when(qi == nb - 1)
        def _():
            dkn_ref[...] = dk_sc[:, :QK_NOPE].astype(dkn_ref.dtype)
            dkr_ref[...] = dk_sc[:, QK_NOPE:].astype(dkr_ref.dtype)
            dv_ref[...] = dv_sc[...].astype(dv_ref.dtype)

        if bg:
            @pl.when((h == HEADS - 1) & (ki == nb - 1) & (qi == nb - 1))
            def _():
                bg.wait(bg_in, bg_out, bg_sems)

    qmap = lambda h, ki, qi: (jnp.maximum(qi, ki), h)
    kmap = lambda h, ki, qi: (ki, h)
    est = 2 * Tp * QHEAD * 4 + 8 * t * t * 4
    return pl.pallas_call(
        body, name=name, grid=(HEADS, nb, nb),
        in_specs=[pl.BlockSpec((t, QHEAD), qmap),
                  pl.BlockSpec((t, QK_NOPE), kmap),
                  pl.BlockSpec((t, LANES), kmap),
                  pl.BlockSpec((t, V_HEAD), lambda h, ki, qi: (ki, HEADS + h)),
                  pl.BlockSpec((t, V_HEAD), qmap),
                  pl.BlockSpec((SUBLANES, t), lambda h, ki, qi: (h, jnp.maximum(qi, ki))),
                  pl.BlockSpec((SUBLANES, t), lambda h, ki, qi: (h, jnp.maximum(qi, ki)))] + (bg.specs if bg else []),
        out_specs=[pl.BlockSpec((Tp, QHEAD), lambda h, ki, qi: (0, h)),
                   pl.BlockSpec((t, QK_NOPE), kmap),
                   pl.BlockSpec((t, LANES), kmap),
                   pl.BlockSpec((t, V_HEAD), kmap)] + (bg.specs if bg else []),
        out_shape=[jax.ShapeDtypeStruct((Tp, HEADS * QHEAD), F32),
                   jax.ShapeDtypeStruct((Tp, HEADS * QK_NOPE), BF16),
                   jax.ShapeDtypeStruct((Tp, HEADS * LANES), F32),
                   jax.ShapeDtypeStruct((Tp, HEADS * V_HEAD), BF16)] + (bg.out_shape if bg else []),
        scratch_shapes=[pltpu.VMEM((t, QHEAD), F32), pltpu.VMEM((t, V_HEAD), F32)] + (bg.scratch if bg else []),
        compiler_params=_params(("arbitrary",) * 3 if bg else ("parallel", "arbitrary", "arbitrary"), est),
    )(q, kv, kr8, kv, do, lse, delta, *(bg.arrs if bg else []))


GW = SSD_INNER // SSD_GROUPS
PAIRS_PER_GROUP = GW // LANES
XB = SSD_INNER // GW
NT_DIMS = (((1,), (1,)), ((), ()))
TN_DIMS = (((0,), (0,)), ((), ()))


def _ssd_common(xs_ref, dt_ref, alog_ref, e_ref):
    a_neg = -jnp.exp(alog_ref[...])
    dt = dt_ref[...]
    li = lax.broadcasted_iota(jnp.int32, (CHUNK, CHUNK), 0)
    si = lax.broadcasted_iota(jnp.int32, (CHUNK, CHUNK), 1)
    tril = li >= si
    tri = tril.astype(F32)
    acs = jnp.dot(tri, dt * a_neg, precision=SSD_PREC, preferred_element_type=F32)
    e = e_ref[...]
    dte = jnp.dot(dt, e, precision=SSD_PREC, preferred_element_type=F32)
    acse = jnp.dot(acs, e, precision=SSD_PREC, preferred_element_type=F32)
    x = xs_ref[...] * dte
    alast = acse[CHUNK - 1:CHUNK, :]
    return dict(a_neg=a_neg, dt=dt, tril=tril, tri=tri, acs=acs, acs_t=acs.T, e=e, dte=dte, acse=acse, x=x,
                p_e=jnp.exp(acse), w_e=jnp.exp(alast - acse), dl_e=jnp.exp(alast), li=li, si=si)


def _decay(cm, head):
    col = cm["acs"][:, head:head + 1]
    row = cm["acs_t"][head:head + 1, :]
    return jnp.exp(jnp.where(cm["tril"], col - row, -jnp.inf))


def _ssd_fwd(xbc, dt, alog, e, name):
    Tp = xbc.shape[0]
    nc = Tp // CHUNK

    def body(xs_ref, b_ref, c_ref, dt_ref, alog_ref, e_ref, y_ref, st_ref, st_sc):
        @pl.when(pl.program_id(0) == 0)
        def _():
            st_sc[...] = jnp.zeros_like(st_sc)

        cm = _ssd_common(xs_ref, dt_ref, alog_ref, e_ref)
        st_ref[0] = st_sc[...]
        lane = lax.broadcasted_iota(jnp.int32, (CHUNK, LANES), 1)
        for g in range(SSD_GROUPS):
            gs = slice(g * GW, (g + 1) * GW)
            cg = c_ref[:, g * SSD_STATE:(g + 1) * SSD_STATE].astype(BF16)
            bg = b_ref[:, g * SSD_STATE:(g + 1) * SSD_STATE].astype(BF16)
            cb = lax.dot_general(cg, bg, NT_DIMS, preferred_element_type=F32)
            stg = st_sc[:, gs]
            yoff = jnp.dot(cg, stg.astype(BF16), preferred_element_type=F32) * cm["p_e"][:, gs]
            xg = cm["x"][:, gs]
            for jp in range(PAIRS_PER_GROUP):
                j = g * PAIRS_PER_GROUP + jp
                xp = xg[:, jp * LANES:(jp + 1) * LANES].astype(BF16)
                ys = []
                for head in (2 * j, 2 * j + 1):
                    m = (cb * _decay(cm, head)).astype(BF16)
                    ys.append(jnp.dot(m, xp, preferred_element_type=F32))
                y_ref[:, j * LANES:(j + 1) * LANES] = (jnp.where(lane < SSD_HEAD_DIM, ys[0], ys[1])
                                                       + yoff[:, jp * LANES:(jp + 1) * LANES])
            snew = lax.dot_general(bg, (cm["w_e"][:, gs] * xg).astype(BF16), TN_DIMS, preferred_element_type=F32)
            st_sc[:, gs] = cm["dl_e"][:, gs] * stg + snew

    return pl.pallas_call(
        body, name=name, grid=(nc,),
        in_specs=[pl.BlockSpec((CHUNK, SSD_INNER), lambda c: (c, 0)),
                  pl.BlockSpec((CHUNK, GW), lambda c: (c, XB)),
                  pl.BlockSpec((CHUNK, GW), lambda c: (c, XB + 1)),
                  pl.BlockSpec((CHUNK, LANES), lambda c: (c, 0)),
                  pl.BlockSpec((1, LANES), lambda c: (0, 0)),
                  pl.BlockSpec((LANES, SSD_INNER), lambda c: (0, 0))],
        out_specs=[pl.BlockSpec((CHUNK, SSD_INNER), lambda c: (c, 0)),
                   pl.BlockSpec((1, SSD_STATE, SSD_INNER), lambda c: (c, 0, 0))],
        out_shape=[jax.ShapeDtypeStruct((Tp, SSD_INNER), F32), jax.ShapeDtypeStruct((nc, SSD_STATE, SSD_INNER), F32)],
        scratch_shapes=[pltpu.VMEM((SSD_STATE, SSD_INNER), F32)],
        compiler_params=_params(("arbitrary",), 24 * CHUNK * SSD_INNER * 4),
    )(xbc, xbc, xbc, dt, alog, e)


def _ssd_bwd(xbc, dt, alog, e, dy, dxs_skip, states, name):
    Tp = xbc.shape[0]
    nc = Tp // CHUNK
    rev = lambda c: nc - 1 - c

    def body(xs_ref, b_ref, c_ref, dt_ref, alog_ref, e_ref, dy_ref, skip_ref, st_ref,
             dxbc_ref, ddt_ref, dalog_ref, dst_sc, dx_sc, t_sc, tw_sc):
        @pl.when(pl.program_id(0) == 0)
        def _():
            dst_sc[...] = jnp.zeros_like(dst_sc)
            dalog_ref[...] = jnp.zeros_like(dalog_ref)

        cm = _ssd_common(xs_ref, dt_ref, alog_ref, e_ref)
        lane = lax.broadcasted_iota(jnp.int32, (CHUNK, LANES), 1)
        dacs_col = jnp.zeros((CHUNK, LANES), F32)
        dacs_row = jnp.zeros((LANES, CHUNK), F32)
        t_last = []
        for g in range(SSD_GROUPS):
            gs = slice(g * GW, (g + 1) * GW)
            cg = c_ref[:, g * SSD_STATE:(g + 1) * SSD_STATE].astype(BF16)
            bg = b_ref[:, g * SSD_STATE:(g + 1) * SSD_STATE].astype(BF16)
            stg = st_ref[0, :, gs]
            stg_b = stg.astype(BF16)
            dstg = dst_sc[:, gs]
            dstg_b = dstg.astype(BF16)
            xg = cm["x"][:, gs]
            dyg = dy_ref[:, gs]
            zg = jnp.dot(cg, stg_b, preferred_element_type=F32)
            dzg = dyg * cm["p_e"][:, gs]
            dzg_b = dzg.astype(BF16)
            dcg = lax.dot_general(dzg_b, stg_b, NT_DIMS, preferred_element_type=F32)
            dst_in = lax.dot_general(cg, dzg_b, TN_DIMS, preferred_element_type=F32)
            dst_in = dst_in + cm["dl_e"][:, gs] * dstg
            t_last.append(jnp.sum(dstg * stg * cm["dl_e"][:, gs], axis=0, keepdims=True))
            weg = cm["w_e"][:, gs]
            dbg = lax.dot_general((weg * xg).astype(BF16), dstg_b, NT_DIMS, preferred_element_type=F32)
            gg = jnp.dot(bg, dstg_b, preferred_element_type=F32)
            dxg = weg * gg
            tw_sc[:, gs] = xg * dxg
            t_sc[:, gs] = dzg * zg - xg * dxg
            cb = lax.dot_general(cg, bg, NT_DIMS, preferred_element_type=F32)
            dcb = jnp.zeros((CHUNK, CHUNK), F32)
            for jp in range(PAIRS_PER_GROUP):
                j = g * PAIRS_PER_GROUP + jp
                ps = slice(jp * LANES, (jp + 1) * LANES)
                xp = xg[:, ps].astype(BF16)
                dyp = dyg[:, ps]
                dxp = dxg[:, ps]
                for half, head in enumerate((2 * j, 2 * j + 1)):
                    lam = _decay(cm, head)
                    m32 = cb * lam
                    sel = (lane < SSD_HEAD_DIM) if half == 0 else (lane >= SSD_HEAD_DIM)
                    dye = jnp.where(sel, dyp, 0.0).astype(BF16)
                    dm = lax.dot_general(dye, xp, NT_DIMS, preferred_element_type=F32)
                    w = dm * m32
                    dacs_col = dacs_col + jnp.where(cm["si"] == head, jnp.sum(w, axis=1, keepdims=True), 0.0)
                    dacs_row = dacs_row + jnp.where(cm["li"] == head, jnp.sum(w, axis=0, keepdims=True), 0.0)
                    dcb = dcb + dm * lam
                    dxp = dxp + lax.dot_general(m32.astype(BF16), dye, TN_DIMS, preferred_element_type=F32)
                dx_sc[:, j * LANES:(j + 1) * LANES] = dxp
            dcb_b = dcb.astype(BF16)
            dcg = dcg + jnp.dot(dcb_b, bg, preferred_element_type=F32)
            dbg = dbg + lax.dot_general(dcb_b, cg, TN_DIMS, preferred_element_type=F32)
            dst_sc[:, gs] = dst_in
            dxbc_ref[:, SSD_INNER + g * SSD_STATE:SSD_INNER + (g + 1) * SSD_STATE] = dbg
            dxbc_ref[:, SSD_INNER + GW + g * SSD_STATE:SSD_INNER + GW + (g + 1) * SSD_STATE] = dcg
        e = cm["e"]
        dacs = lax.dot_general(t_sc[...], e, NT_DIMS, precision=SSD_PREC, preferred_element_type=F32)
        dacs = dacs + dacs_col - dacs_row.T
        last_lane = jnp.concatenate(t_last, axis=1) + jnp.sum(tw_sc[...], axis=0, keepdims=True)
        last_head = lax.dot_general(jnp.broadcast_to(last_lane, (SUBLANES, SSD_INNER)), e, NT_DIMS,
                                    precision=SSD_PREC, preferred_element_type=F32)[0:1, :]
        dacs = dacs + jnp.where(cm["li"] == CHUNK - 1, last_head, 0.0)
        da = lax.dot_general(cm["tri"], dacs, TN_DIMS, precision=SSD_PREC, preferred_element_type=F32)
        dx_all = dx_sc[...]
        ddt = da * cm["a_neg"] + lax.dot_general(dx_all * xs_ref[...], e, NT_DIMS, precision=SSD_PREC,
                                                 preferred_element_type=F32)
        ddt_ref[...] = ddt
        dxbc_ref[:, :SSD_INNER] = dx_all * cm["dte"] + skip_ref[...]
        dalog_ref[0:1, :] += jnp.sum(da * cm["dt"], axis=0, keepdims=True) * cm["a_neg"]

    return pl.pallas_call(
        body, name=name, grid=(nc,),
        in_specs=[pl.BlockSpec((CHUNK, SSD_INNER), lambda c: (rev(c), 0)),
                  pl.BlockSpec((CHUNK, GW), lambda c: (rev(c), XB)),
                  pl.BlockSpec((CHUNK, GW), lambda c: (rev(c), XB + 1)),
                  pl.BlockSpec((CHUNK, LANES), lambda c: (rev(c), 0)),
                  pl.BlockSpec((1, LANES), lambda c: (0, 0)),
                  pl.BlockSpec((LANES, SSD_INNER), lambda c: (0, 0)),
                  pl.BlockSpec((CHUNK, SSD_INNER), lambda c: (rev(c), 0)),
                  pl.BlockSpec((CHUNK, SSD_INNER), lambda c: (rev(c), 0)),
                  pl.BlockSpec((1, SSD_STATE, SSD_INNER), lambda c: (rev(c), 0, 0))],
        out_specs=[pl.BlockSpec((CHUNK, SSD_CONV_DIM), lambda c: (rev(c), 0)),
                   pl.BlockSpec((CHUNK, LANES), lambda c: (rev(c), 0)),
                   pl.BlockSpec((SUBLANES, LANES), lambda c: (0, 0))],
        out_shape=[jax.ShapeDtypeStruct((Tp, SSD_CONV_DIM), F32), jax.ShapeDtypeStruct((Tp, LANES), F32),
                   jax.ShapeDtypeStruct((SUBLANES, LANES), F32)],
        scratch_shapes=[pltpu.VMEM((SSD_STATE, SSD_INNER), F32), pltpu.VMEM((CHUNK, SSD_INNER), F32),
                        pltpu.VMEM((CHUNK, SSD_INNER), F32), pltpu.VMEM((CHUNK, SSD_INNER), F32)],
        compiler_params=_params(("arbitrary",), 32 * CHUNK * SSD_INNER * 4),
    )(xbc, xbc, xbc, dt, alog, e, dy, dxs_skip, states)


def _loss_head(h, target, name):
    Tp, d = h.shape
    nt = Tp // LANES

    def body(h_ref, t_ref, dh_ref, l_ref):
        real = pl.program_id(0) > 0
        err = jnp.where(real, h_ref[...] - t_ref[...], 0.0)
        dh_ref[...] = err * (1.0 / d)
        l_ref[...] = jnp.broadcast_to(0.5 * jnp.sum(err * err) * (1.0 / d), l_ref.shape)

    return pl.pallas_call(
        body, name=name, grid=(nt,),
        in_specs=[pl.BlockSpec((LANES, d), lambda i: (i, 0)),
                  pl.BlockSpec((LANES, d), lambda i: (jnp.maximum(i - 1, 0), 0))],
        out_specs=[pl.BlockSpec((LANES, d), lambda i: (i, 0)),
                   pl.BlockSpec((1, SUBLANES, LANES), lambda i: (i, 0, 0))],
        out_shape=[jax.ShapeDtypeStruct((Tp, d), F32), jax.ShapeDtypeStruct((nt, SUBLANES, LANES), F32)],
        compiler_params=_params(("parallel",), 8 * LANES * d * 4),
    )(h, target)


def _adamw(parts, w, m, v, name):
    shape = w.shape
    C = shape[-1]
    R = int(np.prod(shape[:-1]))
    npart = parts.shape[0]
    parts, w, m, v = parts.reshape(npart, R, C), w.reshape(R, C), m.reshape(R, C), v.reshape(R, C)
    lanes = -(-C // LANES) * LANES
    tr = _pick(R, max(BF16_ROWS, ADAM_ELEMS // lanes), BF16_ROWS) if R % BF16_ROWS == 0 else R
    c1 = 1.0 / (1.0 - ADAM_B1 ** ADAM_STEP)
    c2 = 1.0 / (1.0 - ADAM_B2 ** ADAM_STEP)

    def body(p_ref, w_ref, m_ref, v_ref, g_out, d_out, m_out, v_out):
        g = p_ref[0].astype(F32)
        for p in range(1, npart):
            g = g + p_ref[p].astype(F32)
        m_new = ADAM_B1 * m_ref[...] + (1.0 - ADAM_B1) * g
        v_new = ADAM_B2 * v_ref[...] + (1.0 - ADAM_B2) * (g * g)
        g_out[...] = g
        m_out[...] = m_new
        v_out[...] = v_new
        d_out[...] = -ADAM_LR * ((m_new * c1) / (jnp.sqrt(v_new * c2) + ADAM_EPS) + ADAM_WD * w_ref[...])

    spec = pl.BlockSpec((tr, C), lambda i: (i, 0))
    est = npart * tr * lanes * parts.dtype.itemsize + 7 * tr * lanes * 4
    res = pl.pallas_call(
        body, name=name, grid=(R // tr,),
        in_specs=[pl.BlockSpec((npart, tr, C), lambda i: (0, i, 0)), spec, spec, spec],
        out_specs=[spec] * 4, out_shape=[jax.ShapeDtypeStruct((R, C), F32)] * 4,
        compiler_params=_params(("parallel",), est),
    )(parts, w, m, v)
    return [r.reshape(shape) for r in res]


MESH_ID = pl.DeviceIdType.MESH
N_PEERS = N_DEV - 1


def _dev_index(p):
    return 4 * p[0] + 2 * p[1] + p[2]


class _Background:
    def __init__(self, kind, arrs):
        self.kind, self.arrs, self.n = kind, list(arrs), len(arrs)
        self.npairs = N_PEERS if kind == "gather" else N_CHIPS - 1
        lead = (N_DEV,) if kind == "gather" else ()
        self.out_shape = [jax.ShapeDtypeStruct(lead + a.shape, a.dtype) for a in self.arrs]
        self.specs = [pl.BlockSpec(memory_space=pl.ANY)] * self.n
        self.scratch = [pltpu.SemaphoreType.DMA((self.n, self.npairs)), pltpu.SemaphoreType.DMA((self.n, self.npairs)),
                        pltpu.SemaphoreType.DMA((self.n,))]

    def copies(self, in_refs, out_refs, sems):
        send_sems, recv_sems, local_sems = sems
        x, y, c = lax.axis_index("x"), lax.axis_index("y"), lax.axis_index("c")
        sends, recvs, locals_ = [], [], []

        def remote(t, k, src, dst, to):
            return pltpu.make_async_remote_copy(src_ref=src, dst_ref=dst, send_sem=send_sems.at[t, k],
                                                recv_sem=recv_sems.at[t, k], device_id=to, device_id_type=MESH_ID)

        if self.kind == "gather":
            me = _dev_index((x, y, c))
            peers = [(x, y, 1 - c), (1 - x, y, c), (x, 1 - y, c), (1 - x, 1 - y, c),
                     (1 - x, y, 1 - c), (x, 1 - y, 1 - c), (1 - x, 1 - y, 1 - c)]
            for t in range(self.n):
                locals_.append(pltpu.make_async_copy(in_refs[t], out_refs[t].at[me], local_sems.at[t]))
                for k, p in enumerate(peers):
                    sends.append(remote(t, k, in_refs[t], out_refs[t].at[me], p))
                    recvs.append(remote(t, k, in_refs[t], out_refs[t].at[_dev_index(p)], p))
        else:
            mine = 2 * x + y
            peers = [(1 - x, y), (x, 1 - y), (1 - x, 1 - y)]
            for t in range(self.n):
                locals_.append(pltpu.make_async_copy(in_refs[t].at[mine], out_refs[t].at[mine], local_sems.at[t]))
                for k, p in enumerate(peers):
                    theirs = 2 * p[0] + p[1]
                    sends.append(remote(t, k, in_refs[t].at[theirs], out_refs[t].at[mine], (*p, c)))
                    recvs.append(remote(t, k, in_refs[t].at[mine], out_refs[t].at[theirs], (*p, c)))
        return sends, recvs, locals_

    def start(self, in_refs, out_refs, sems):
        sends, _, locals_ = self.copies(in_refs, out_refs, sems)
        for cp in locals_ + sends:
            cp.start()

    def wait(self, in_refs, out_refs, sems):
        sends, recvs, locals_ = self.copies(in_refs, out_refs, sems)
        for cp in recvs:
            cp.wait_recv()
        for cp in sends:
            cp.wait_send()
        for cp in locals_:
            cp.wait()


def _comm_call(body, name, arrs, out_shape, npairs):
    n = len(arrs)
    any_spec = pl.BlockSpec(memory_space=pl.ANY)
    return pl.pallas_call(
        functools.partial(body, n), name=name, in_specs=[any_spec] * n, out_specs=[any_spec] * n, out_shape=out_shape,
        scratch_shapes=[pltpu.SemaphoreType.DMA((n, npairs)), pltpu.SemaphoreType.DMA((n, npairs)),
                        pltpu.SemaphoreType.DMA((n,))],
    )(*arrs)


def _allgather(arrs, name):
    def body(n, *refs):
        src_refs, out_refs = refs[:n], refs[n:2 * n]
        send_sems, recv_sems, local_sems = refs[2 * n:]
        x, y, c = lax.axis_index("x"), lax.axis_index("y"), lax.axis_index("c")
        me, sibling = (x, y, c), (x, y, 1 - c)
        chips = [(1 - x, y), (x, 1 - y), (1 - x, 1 - y)]

        def copy(t, k, block, to, src=None):
            slot = out_refs[t].at[_dev_index(block)]
            return pltpu.make_async_remote_copy(
                src_ref=slot if src is None else src, dst_ref=slot,
                send_sem=send_sems.at[t, k], recv_sem=recv_sems.at[t, k],
                device_id=to, device_id_type=MESH_ID)

        sends, locals_ = [], []
        for t in range(n):
            mine = pltpu.make_async_copy(src_refs[t], out_refs[t].at[_dev_index(me)], local_sems.at[t])
            mine.start()
            locals_.append(mine)
            first = [copy(t, 0, me, sibling, src=src_refs[t])]
            first += [copy(t, 1 + j, me, (*chip, c), src=src_refs[t]) for j, chip in enumerate(chips)]
            for cp in first:
                cp.start()
            sends += first
        for j, chip in enumerate(chips):
            for t in range(n):
                copy(t, 1 + j, (*chip, c), me).wait_recv()
                passed = copy(t, 4 + j, (*chip, c), sibling)
                passed.start()
                sends.append(passed)
        for t in range(n):
            copy(t, 0, sibling, me).wait_recv()
            for j, chip in enumerate(chips):
                copy(t, 4 + j, (*chip, 1 - c), me).wait_recv()
        for cp in sends:
            cp.wait_send()
        for cp in locals_:
            cp.wait()

    return _comm_call(body, name, arrs, [jax.ShapeDtypeStruct((N_DEV,) + a.shape, a.dtype) for a in arrs], N_PEERS)


N_CHIPS = N_DEV // 2
CHIPS = [(0, 0), (0, 1), (1, 0), (1, 1)]


def _sibling_exchange(arrs, name):
    def body(n, *refs):
        in_refs, out_refs = refs[:n], refs[n:2 * n]
        send_sems, recv_sems, _ = refs[2 * n:]
        x, y, c = lax.axis_index("x"), lax.axis_index("y"), lax.axis_index("c")
        sibling = (x, y, 1 - c)

        def copy(t, j):
            return pltpu.make_async_remote_copy(
                src_ref=in_refs[t].at[_dev_index((*CHIPS[j], 1 - c))], dst_ref=out_refs[t].at[j],
                send_sem=send_sems.at[t, j], recv_sem=recv_sems.at[t, j],
                device_id=sibling, device_id_type=MESH_ID)

        copies = [copy(t, j) for t in range(n) for j in range(N_CHIPS)]
        for cp in copies:
            cp.start()
        for cp in copies:
            cp.wait_recv()
        for cp in copies:
            cp.wait_send()

    return _comm_call(body, name, arrs, [jax.ShapeDtypeStruct((N_CHIPS,) + a.shape[1:], a.dtype) for a in arrs], N_CHIPS)


def _chip_exchange(arrs, name):
    def body(n, *refs):
        in_refs, out_refs = refs[:n], refs[n:2 * n]
        send_sems, recv_sems, local_sems = refs[2 * n:]
        x, y, c = lax.axis_index("x"), lax.axis_index("y"), lax.axis_index("c")
        mine = 2 * x + y
        peers = [(1 - x, y), (x, 1 - y), (1 - x, 1 - y)]

        def copy(t, k, src_chip, dst_chip, to):
            return pltpu.make_async_remote_copy(
                src_ref=in_refs[t].at[src_chip], dst_ref=out_refs[t].at[dst_chip],
                send_sem=send_sems.at[t, k], recv_sem=recv_sems.at[t, k],
                device_id=(*to, c), device_id_type=MESH_ID)

        sends, locals_ = [], []
        for t in range(n):
            own = pltpu.make_async_copy(in_refs[t].at[mine], out_refs[t].at[mine], local_sems.at[t])
            own.start()
            locals_.append(own)
            for k, p in enumerate(peers):
                cp = copy(t, k, 2 * p[0] + p[1], mine, p)
                cp.start()
                sends.append(cp)
        for t in range(n):
            for k, p in enumerate(peers):
                copy(t, k, mine, 2 * p[0] + p[1], p).wait_recv()
        for cp in sends:
            cp.wait_send()
        for cp in locals_:
            cp.wait()

    return _comm_call(body, name, arrs, [jax.ShapeDtypeStruct(a.shape, a.dtype) for a in arrs], N_CHIPS - 1)


def _add_pairs(a, b, name):
    shape = a.shape
    C = shape[-1]
    R = int(np.prod(shape[:-1]))
    lanes = -(-C // LANES) * LANES
    tr = _pick(R, max(BF16_ROWS, 2 * ADAM_ELEMS // lanes), BF16_ROWS) if R % BF16_ROWS == 0 else R

    def body(a_ref, b_ref, o_ref):
        o_ref[...] = (a_ref[...].astype(F32) + b_ref[...].astype(F32)).astype(o_ref.dtype)

    spec = pl.BlockSpec((tr, C), lambda i: (i, 0))
    return pl.pallas_call(
        body, name=name, grid=(R // tr,), in_specs=[spec, spec], out_specs=spec,
        out_shape=jax.ShapeDtypeStruct((R, C), a.dtype),
        compiler_params=_params(("parallel",), 3 * tr * lanes * 4),
    )(a.reshape(R, C), b.reshape(R, C)).reshape(shape)


WEIGHTS = ['meta_tokens', 'emb_ln_g', 'emb_ln_b', 'w_in', 'q_norm_g', 'w_q_b', 'kv_norm_g', 'w_kv_b', 'w_o_attn',
           'ssd_conv_w', 'ssd_conv_b', 'dt_bias', 'a_log', 'd_skip', 'ssd_norm_g', 'w_o_ssd', 'w_out', 'ln1_g',
           'ln1_b', 'w_up', 'ffn_conv_w', 'ffn_conv_b', 'w_down', 'ln2_g', 'ln2_b']
BIG = {'w_in': 2, 'w_q_b': 2, 'w_kv_b': 2, 'w_o_attn': 1, 'w_o_ssd': 1, 'w_out': 1, 'w_up': 2, 'w_down': 1}
SMALL_SHARDED = {'meta_tokens': 1, 'ssd_conv_w': 2, 'ffn_conv_w': 2}
REPLICATED = [n for n in WEIGHTS if n not in BIG and n not in SMALL_SHARDED]
BIG_COLS = 1024
SMALL_COLS = LANES


def _flatten(arrs, cols, row_mult, lead=False):
    parts, offs, off = [], [], 0
    for a in arrs:
        a2 = a.reshape(N_DEV, -1) if lead else a.reshape(1, -1)
        n = a2.shape[1]
        pad = -n % cols
        parts.append(jnp.pad(a2, ((0, 0), (0, pad))))
        offs.append((off, n))
        off += n + pad
    rows = off // cols
    extra = (-rows % row_mult) * cols
    if extra:
        parts.append(jnp.zeros((parts[0].shape[0], extra), parts[0].dtype))
    flat = jnp.concatenate(parts, axis=1)
    flat = flat.reshape(flat.shape[0], -1, cols)
    return (flat if lead else flat[0]), offs


def _unflatten(flat, offs, shapes):
    f = flat.reshape(-1)
    return [f[o:o + n].reshape(s) for (o, n), s in zip(offs, shapes)]


def _to_pieces(g, axis):
    s = g.shape[axis] // N_DEV
    g = g.reshape(g.shape[:axis] + (N_DEV, s) + g.shape[axis + 1:])
    return jnp.moveaxis(g, axis, 0).reshape(N_DEV, -1)


def _from_pieces(p, shard_shape, axis):
    g = jnp.moveaxis(p.reshape((N_DEV,) + tuple(shard_shape)), 0, axis)
    sh = list(shard_shape)
    sh[axis] *= N_DEV
    return g.reshape(sh)


def _in_proj_pad(w):
    e = np.cumsum((0,) + IN_SIZES)
    ql, kvl, kpe, z, xbc, dt, ga, gs = [w[:, e[j]:e[j + 1]] for j in range(8)]
    zc = lambda n: jnp.zeros((w.shape[0], n), w.dtype)
    return jnp.concatenate([ql, kvl, z, xbc, ga, gs, kpe, zc(LANES - QK_ROPE), dt, zc(LANES - SSD_HEADS)], axis=1)


def _in_proj_unpad(d):
    seg = lambda o, n: d[:, o:o + n]
    return jnp.concatenate([seg(OQ, Q_LORA), seg(OKV, KV_LORA), seg(OKPE, QK_ROPE), seg(OZ, SSD_INNER),
                            seg(OXBC, SSD_CONV_DIM), seg(ODT, SSD_HEADS), seg(OGA, D_MODEL), seg(OGS, D_MODEL)], axis=1)


def _q_pad(w):
    w3 = w.reshape(Q_LORA, HEADS, QK_NOPE + QK_ROPE)
    return jnp.concatenate([w3, jnp.zeros((Q_LORA, HEADS, QHEAD - QK_NOPE - QK_ROPE), w.dtype)], axis=2).reshape(Q_LORA, HEADS * QHEAD)


def _q_unpad(d):
    return d.reshape(Q_LORA, HEADS, QHEAD)[:, :, :QK_NOPE + QK_ROPE].reshape(Q_LORA, HEADS * (QK_NOPE + QK_ROPE))


def _kv_perm(w):
    w3 = w.reshape(KV_LORA, HEADS, QK_NOPE + V_HEAD)
    return jnp.concatenate([w3[:, :, :QK_NOPE].reshape(KV_LORA, -1), w3[:, :, QK_NOPE:].reshape(KV_LORA, -1)], axis=1)


def _kv_unperm(d):
    kn = d[:, :HEADS * QK_NOPE].reshape(KV_LORA, HEADS, QK_NOPE)
    v = d[:, HEADS * QK_NOPE:].reshape(KV_LORA, HEADS, V_HEAD)
    return jnp.concatenate([kn, v], axis=2).reshape(KV_LORA, HEADS * (QK_NOPE + V_HEAD))


def _row_vec(v, width=None):
    v = v.reshape(1, -1).astype(F32)
    if width is not None and v.shape[1] < width:
        v = jnp.pad(v, ((0, 0), (0, width - v.shape[1])))
    return v


def _pad_rows8(w):
    return jnp.pad(w.astype(F32), ((0, SUBLANES - w.shape[0]), (0, 0)))


def _tables(Tp, npad):
    pos = jnp.maximum(jnp.arange(Tp, dtype=jnp.int32) - npad, 0).astype(F32)
    inv_freq = 1.0 / (ROPE_THETA ** (jnp.arange(0, QK_ROPE, 2, dtype=F32) / QK_ROPE))
    ang = pos[:, None] * inv_freq[None, :]
    ang = jnp.concatenate([ang, ang], axis=-1)
    zeros = jnp.zeros((Tp, LANES - QK_ROPE), F32)
    cos = jnp.concatenate([jnp.cos(ang), zeros], axis=1)
    sin = jnp.concatenate([jnp.sin(ang), zeros], axis=1)
    rot = np.zeros((LANES, LANES), np.float32)
    half = QK_ROPE // 2
    for i in range(half):
        rot[i + half, i] = -1.0
        rot[i, i + half] = 1.0
    expand = np.zeros((LANES, SSD_INNER), np.float32)
    for hd in range(SSD_HEADS):
        expand[hd, hd * SSD_HEAD_DIM:(hd + 1) * SSD_HEAD_DIM] = 1.0
    return cos, sin, jnp.asarray(rot), jnp.asarray(expand)


def _layer_rows(proj, tb):
    rows_a = [_row(proj, Q_LORA, OQ // Q_LORA), _row(proj, KV_LORA, OKV // KV_LORA), _row(proj, LANES, OKPE // LANES),
              _row(proj, LANES, ODT // LANES), _row(tb["cos"], diff=False), _row(tb["sin"], diff=False)]
    return rows_a


def _layer_fwd(h, P, tb, fns, npad, bg=None):
    proj = _mm(h, P["w_in"], F32, "in_proj")
    rows_a = _layer_rows(proj, tb)
    consts_a = [_row(tb["rot"], diff=False), _row(P["q_norm_g"]), _row(P["kv_norm_g"]), _row(P["dt_bias"])]
    qn, kvn, kr8, dt = _rw_fwd(fns["in_post"], rows_a, consts_a,
                               [_out(Q_LORA, BF16), _out(KV_LORA, BF16), _out(HEADS * LANES, BF16), _out(LANES, F32)],
                               "in_post")
    q = _mm(qn, P["w_q"], F32, "q_proj")
    rows_q = [_row(q, QHEAD, 0, grp=True), _row(tb["cos"], diff=False), _row(tb["sin"], diff=False)]
    qr = _rw_fwd(fns["q_post"], rows_q, [_row(tb["rot"], diff=False)], [_out(HEADS * QHEAD, BF16, QHEAD, grp=True)],
                 "q_post", ng=HEADS)[0]
    kv = _mm(kvn, P["w_kv"], BF16, "kv_proj")
    o, lse, *carried = _flash_fwd(qr, kv, kr8, npad, "attn_fwd_gather" if bg else "attn_fwd", bg=bg)
    ya = _mm(o, P["w_o_attn"], F32, "attn_out")
    xbc = _conv_fwd(proj, OXBC, SSD_CONV_DIM, P["ssd_conv_w"], P["ssd_conv_b"], SSD_CONV, True, npad, "ssd_conv")
    y, states = _ssd_fwd(xbc, dt, P["a_log"], tb["expand"], "ssd_fwd")
    rows_b = [_row(y, GW, 0, grp=True), _row(xbc, GW, 0, grp=True), _row(proj, GW, OZ // GW, grp=True)]
    consts_b = [_row(P["d_skip"], GW, 0, grp=True), _row(P["ssd_norm_g"], GW, 0, grp=True)]
    yn = _rw_fwd(fns["gated"], rows_b, consts_b, [_out(SSD_INNER, BF16, GW, grp=True)], "ssd_gate", ng=SSD_GROUPS)[0]
    ys = _mm(yn, P["w_o_ssd"], F32, "ssd_out")
    rows_c = [_row(proj, D_MODEL, OGA // D_MODEL), _row(proj, D_MODEL, OGS // D_MODEL), _row(ya), _row(ys)]
    mixed = _rw_fwd(fns["mix"], rows_c, [], [_out(D_MODEL, BF16)], "mix")[0]
    mo = _mm(mixed, P["w_out"], F32, "mix_out")
    consts_1 = [_row(P["ln1_g"]), _row(P["ln1_b"])]
    h1 = _rw_fwd(fns["res_ln"], [_row(h), _row(mo)], consts_1, [_out(D_MODEL, F32)], "ln1")[0]
    up = _mm(h1, P["w_up"], F32, "ffn_up")
    u = _conv_fwd(up, 0, 2 * D_FF, P["ffn_conv_w"], P["ffn_conv_b"], FFN_CONV, False, npad, "ffn_conv")
    act = _rw_fwd(fns["glu"], [_row(u)], [], [_out(D_FF, BF16)], "ffn_glu")[0]
    fo = _mm(act, P["w_down"], F32, "ffn_down")
    consts_2 = [_row(P["ln2_g"]), _row(P["ln2_b"])]
    h2 = _rw_fwd(fns["res_ln"], [_row(h1), _row(fo)], consts_2, [_out(D_MODEL, F32)], "ln2")[0]
    res = dict(h=h, proj=proj, qn=qn, kvn=kvn, kr8=kr8, dt=dt, q=q, qr=qr, kv=kv, o=o, lse=lse, ya=ya, xbc=xbc, y=y,
               states=states, yn=yn, ys=ys, mixed=mixed, mo=mo, h1=h1, up=up, u=u, act=act, fo=fo)
    return h2, res, carried


def _layer_bwd(dh2, r, P, tb, fns, npad, bg=None):
    g = {}
    consts_2 = [_row(P["ln2_g"]), _row(P["ln2_b"])]
    (dh1_a, dfo), (g["ln2_g"], g["ln2_b"]) = _rw_bwd(fns["res_ln"], [_row(r["h1"]), _row(r["fo"])], consts_2,
                                                     [_row(dh2)], [F32, BF16], "ln2_bwd")
    g["w_down"] = _mm(r["act"], dfo, BF16, "dw_down", ta=True)
    dact = _mm(dfo, P["w_down"], F32, "d_act", tb=True)
    (du,), _ = _rw_bwd(fns["glu"], [_row(r["u"])], [], [_row(dact)], [F32], "glu_bwd")
    dup, g["ffn_conv_w"], g["ffn_conv_b"] = _conv_bwd(r["up"], 0, 2 * D_FF, P["ffn_conv_w"], P["ffn_conv_b"], du,
                                                      FFN_CONV, False, npad, "ffn_conv_bwd")
    g["w_up"] = _mm(r["h1"], dup, BF16, "dw_up", ta=True)
    dh1 = _mm(dup, P["w_up"], F32, "d_h1", tb=True, add=dh1_a)
    consts_1 = [_row(P["ln1_g"]), _row(P["ln1_b"])]
    (dh_a, dmo), (g["ln1_g"], g["ln1_b"]) = _rw_bwd(fns["res_ln"], [_row(r["h"]), _row(r["mo"])], consts_1,
                                                    [_row(dh1)], [F32, BF16], "ln1_bwd")
    g["w_out"] = _mm(r["mixed"], dmo, BF16, "dw_out", ta=True)
    dmixed = _mm(dmo, P["w_out"], F32, "d_mixed", tb=True)
    proj = r["proj"]
    rows_c = [_row(proj, D_MODEL, OGA // D_MODEL), _row(proj, D_MODEL, OGS // D_MODEL), _row(r["ya"]), _row(r["ys"])]
    (dga, dgs, dya, dys), _ = _rw_bwd(fns["mix"], rows_c, [], [_row(dmixed)], [BF16] * 4, "mix_bwd")
    g["w_o_attn"] = _mm(r["o"], dya, BF16, "dw_o_attn", ta=True)
    do = _mm(dya, P["w_o_attn"], F32, "d_o", tb=True)
    g["w_o_ssd"] = _mm(r["yn"], dys, BF16, "dw_o_ssd", ta=True)
    dyn = _mm(dys, P["w_o_ssd"], F32, "d_yn", tb=True)
    rows_b = [_row(r["y"], GW, 0, grp=True), _row(r["xbc"], GW, 0, grp=True), _row(proj, GW, OZ // GW, grp=True)]
    consts_b = [_row(P["d_skip"], GW, 0, grp=True), _row(P["ssd_norm_g"], GW, 0, grp=True)]
    (dy, dxs_skip, dz), (g["d_skip"], g["ssd_norm_g"]) = _rw_bwd(
        fns["gated"], rows_b, consts_b, [_row(dyn, GW, 0, grp=True)], [F32, F32, BF16], "ssd_gate_bwd", ng=SSD_GROUPS)
    dxbc, ddt, g["a_log"] = _ssd_bwd(r["xbc"], r["dt"], P["a_log"], tb["expand"], dy, dxs_skip, r["states"], "ssd_bwd")
    dxbc_pre, g["ssd_conv_w"], g["ssd_conv_b"] = _conv_bwd(proj, OXBC, SSD_CONV_DIM, P["ssd_conv_w"], P["ssd_conv_b"],
                                                           dxbc, SSD_CONV, True, npad, "ssd_conv_bwd")
    delta = _attn_delta(do, r["o"], "attn_delta")
    dqr, dkn, dkr8, dv, *carried = _flash_bwd(r["qr"], r["kv"], r["kr8"], do, r["lse"], delta, npad,
                                              "attn_bwd_exchange" if bg else "attn_bwd", bg=bg)
    rows_q = [_row(r["q"], QHEAD, 0, grp=True), _row(tb["cos"], diff=False), _row(tb["sin"], diff=False)]
    (dq,), _ = _rw_bwd(fns["q_post"], rows_q, [_row(tb["rot"], diff=False)], [_row(dqr, QHEAD, 0, grp=True)], [BF16],
                       "q_post_bwd", ng=HEADS)
    g["w_q"] = _mm(r["qn"], dq, BF16, "dw_q", ta=True)
    dqn = _mm(dq, P["w_q"], F32, "d_qn", tb=True)
    dkv = jnp.concatenate([dkn, dv], axis=1)
    g["w_kv"] = _mm(r["kvn"], dkv, BF16, "dw_kv", ta=True)
    dkvn = _mm(dkv, P["w_kv"], F32, "d_kvn", tb=True)
    rows_a = _layer_rows(proj, tb)
    consts_a = [_row(tb["rot"], diff=False), _row(P["q_norm_g"]), _row(P["kv_norm_g"]), _row(P["dt_bias"])]
    (dql, dkvl, dkpe, ddtr), (g["q_norm_g"], g["kv_norm_g"], g["dt_bias"]) = _rw_bwd(
        fns["in_post"], rows_a, consts_a, [_row(dqn), _row(dkvn), _row(dkr8), _row(ddt)], [BF16] * 4, "in_post_bwd")
    dproj = jnp.concatenate([dql, dkvl, dz, dxbc_pre, dga, dgs, dkpe, ddtr], axis=1)
    g["w_in"] = _mm(r["h"], dproj, BF16, "dw_in", ta=True)
    dh = _mm(dproj, P["w_in"], F32, "d_h", tb=True, add=dh_a)
    return dh, g, carried


def _full_weight(g, axis):
    if axis == 1:
        return g.reshape(-1, g.shape[-1])
    return jnp.concatenate([g[p] for p in range(N_DEV)], axis=1)


def _grad_pieces(d, axis):
    if axis == 1:
        return d.reshape(N_DEV, -1, d.shape[1])
    return jnp.transpose(d.reshape(d.shape[0], N_DEV, -1), (1, 0, 2))


def _layer_params(gathered, small, i):
    full = {n: _full_weight(gathered[n], BIG[n]) for n in BIG}
    P = {}
    P["w_in"] = _in_proj_pad(full["w_in"])
    P["w_q"] = _q_pad(full["w_q_b"])
    P["w_kv"] = _kv_perm(full["w_kv_b"])
    for n in ("w_o_attn", "w_o_ssd", "w_out", "w_up", "w_down"):
        P[n] = full[n]
    P["q_norm_g"] = _row_vec(small["q_norm_g"][i])
    P["kv_norm_g"] = _row_vec(small["kv_norm_g"][i])
    P["dt_bias"] = _row_vec(small["dt_bias"][i], LANES)
    P["a_log"] = _row_vec(small["a_log"][i], LANES)
    P["d_skip"] = _row_vec(jnp.repeat(small["d_skip"][i], SSD_HEAD_DIM))
    P["ssd_norm_g"] = _row_vec(small["ssd_norm_g"][i])
    P["ssd_conv_w"] = _pad_rows8(small["ssd_conv_w"][i])
    P["ssd_conv_b"] = _row_vec(small["ssd_conv_b"][i])
    P["ffn_conv_w"] = _pad_rows8(small["ffn_conv_w"][i])
    P["ffn_conv_b"] = _row_vec(small["ffn_conv_b"][i])
    for n in ("ln1_g", "ln1_b", "ln2_g", "ln2_b"):
        P[n] = _row_vec(small[n][i])
    return P


def _layer_grads_to_reference_layout(g):
    out = {}
    out["w_in"] = _in_proj_unpad(g["w_in"])
    out["w_q_b"] = _q_unpad(g["w_q"])
    out["w_kv_b"] = _kv_unperm(g["w_kv"])
    for n in ("w_o_attn", "w_o_ssd", "w_out", "w_up", "w_down"):
        out[n] = g[n]
    out["q_norm_g"] = g["q_norm_g"][0]
    out["kv_norm_g"] = g["kv_norm_g"][0]
    out["dt_bias"] = g["dt_bias"][0, :SSD_HEADS]
    out["a_log"] = g["a_log"][0, :SSD_HEADS]
    out["d_skip"] = g["d_skip"].reshape(SSD_HEADS, SSD_HEAD_DIM).sum(axis=1)
    out["ssd_norm_g"] = g["ssd_norm_g"][0]
    out["ssd_conv_w"] = g["ssd_conv_w"][:SSD_CONV]
    out["ssd_conv_b"] = g["ssd_conv_b"][0]
    out["ffn_conv_w"] = g["ffn_conv_w"][:FFN_CONV]
    out["ffn_conv_b"] = g["ffn_conv_b"][0]
    for n in ("ln1_g", "ln1_b", "ln2_g", "ln2_b"):
        out[n] = g[n][0]
    return out


def kernel(x, meta_tokens, emb_ln_g, emb_ln_b, w_in, q_norm_g, w_q_b, kv_norm_g, w_kv_b, w_o_attn, ssd_conv_w, ssd_conv_b, dt_bias, a_log, d_skip, ssd_norm_g, w_o_ssd, w_out, ln1_g, ln1_b, w_up, ffn_conv_w, ffn_conv_b, w_down, ln2_g, ln2_b, loss_target, m_meta_tokens, m_emb_ln_g, m_emb_ln_b, m_w_in, m_q_norm_g, m_w_q_b, m_kv_norm_g, m_w_kv_b, m_w_o_attn, m_ssd_conv_w, m_ssd_conv_b, m_dt_bias, m_a_log, m_d_skip, m_ssd_norm_g, m_w_o_ssd, m_w_out, m_ln1_g, m_ln1_b, m_w_up, m_ffn_conv_w, m_ffn_conv_b, m_w_down, m_ln2_g, m_ln2_b, v_meta_tokens, v_emb_ln_g, v_emb_ln_b, v_w_in, v_q_norm_g, v_w_q_b, v_kv_norm_g, v_w_kv_b, v_w_o_attn, v_ssd_conv_w, v_ssd_conv_b, v_dt_bias, v_a_log, v_d_skip, v_ssd_norm_g, v_w_o_ssd, v_w_out, v_ln1_g, v_ln1_b, v_w_up, v_ffn_conv_w, v_ffn_conv_b, v_w_down, v_ln2_g, v_ln2_b):
    given = dict(locals())
    w = {n: given[n] for n in WEIGHTS}
    m = {n: given["m_" + n] for n in WEIGHTS}
    v = {n: given["v_" + n] for n in WEIGHTS}
    seq = x.shape[1]
    assert x.shape[0] == 1 and seq % LANES == 0
    npad = LANES - N_META
    Tp = npad + N_META + seq
    depth = w_in.shape[0]

    big_names, small_names = list(BIG), list(SMALL_SHARDED)
    ws, offs_s = _flatten([w[n] for n in small_names], SMALL_COLS, SUBLANES)
    shards = [[w[n][i].astype(BF16) for n in big_names] for i in range(depth)]
    got = _allgather(shards[0] + [ws], "weight_allgather")
    gathered = dict(zip(big_names, got[:-1]))
    gsm = got[-1]
    small = {n: w[n] for n in REPLICATED}
    for n, (o, sz) in zip(small_names, offs_s):
        small[n] = _from_pieces(gsm.reshape(N_DEV, -1)[:, o:o + sz], w[n].shape, SMALL_SHARDED[n])

    fns = _make_stage_fns(npad)
    cos, sin, rot, expand = _tables(Tp, npad)
    tb = dict(cos=cos, sin=sin, rot=rot, expand=expand)
    top = jnp.pad(small["meta_tokens"], ((npad, 0), (0, 0)))
    hcat = jnp.concatenate([top, x[0]], axis=0)
    consts_e = [_row(_row_vec(w["emb_ln_g"])), _row(_row_vec(w["emb_ln_b"]))]
    h = _rw_fwd(fns["ln"], [_row(hcat)], consts_e, [_out(D_MODEL, F32)], "emb_ln")[0]
    layers, saved = [], []
    for i in range(depth):
        layers.append(_layer_params(gathered, small, i))
        bg = _Background("gather", shards[i + 1]) if i + 1 < depth else None
        h, res, carried = _layer_fwd(h, layers[i], tb, fns, npad, bg=bg)
        gathered = dict(zip(big_names, carried))
        saved.append(res)
    dh, lparts = _loss_head(h, loss_target[0], "loss_head")
    loss = lax.psum(jnp.sum(lparts[:, 0, 0]), ("x", "y", "c"))

    core = lax.axis_index("c")

    def chip_partials(pieces, tag):
        from_sibling = _sibling_exchange(pieces, "grad_exchange_cores_" + tag)
        sums = []
        for k, (p, r) in enumerate(zip(pieces, from_sibling)):
            own = lax.dynamic_index_in_dim(p.reshape((N_CHIPS, 2) + p.shape[1:]), core, axis=1, keepdims=False)
            sums.append(_add_pairs(own, r, "grad_chip_sum_%s_%d" % (tag, k)))
        return sums

    lg, recv_big, pending = [None] * depth, [None] * depth, None
    for i in reversed(range(depth)):
        bg = _Background("chips", pending) if pending is not None else None
        dh, gi, carried = _layer_bwd(dh, saved[i], layers[i], tb, fns, npad, bg=bg)
        if bg:
            recv_big[i + 1] = carried
        lg[i] = _layer_grads_to_reference_layout(gi)
        pending = None
        if i > 0:
            pending = chip_partials([_grad_pieces(lg[i][n], BIG[n]).astype(BF16) for n in big_names], "l%d" % i)
    (dhcat,), (d_emb_g, d_emb_b) = _rw_bwd(fns["ln"], [_row(hcat)], consts_e, [_row(dh)], [F32], "emb_ln_bwd")
    grad_x = dhcat[LANES:][None]
    local = {n: jnp.stack([lg[i][n] for i in range(depth)]) for n in lg[0] if n not in BIG}
    local["meta_tokens"] = dhcat[npad:LANES]
    local["emb_ln_g"] = d_emb_g[0]
    local["emb_ln_b"] = d_emb_b[0]

    sm_names = small_names + REPLICATED
    sm_pieces = [_to_pieces(local[n], SMALL_SHARDED[n]) for n in small_names]
    sm_pieces += [jnp.broadcast_to(local[n].reshape(1, -1), (N_DEV, local[n].size)) for n in REPLICATED]
    ps, _ = _flatten(sm_pieces, SMALL_COLS, BF16_ROWS, lead=True)
    pieces = [_grad_pieces(lg[0][n], BIG[n]).astype(BF16) for n in big_names] + [ps]
    recv = _chip_exchange(chip_partials(pieces, "l0"), "grad_exchange_chips")
    recv_big[0] = recv[:-1]
    outs = {}
    kinds = ("grad", "delta", "new_m", "new_v")
    for k, n in enumerate(big_names):
        parts = jnp.stack([recv_big[i][k] for i in range(depth)], axis=1)
        for kind, a in zip(kinds, _adamw(parts, w[n], m[n], v[n], "adamw_" + n)):
            outs[kind + "_" + n] = a
    wf, offs = _flatten([w[n] for n in sm_names], SMALL_COLS, BF16_ROWS)
    mf, _ = _flatten([m[n] for n in sm_names], SMALL_COLS, BF16_ROWS)
    vf, _ = _flatten([v[n] for n in sm_names], SMALL_COLS, BF16_ROWS)
    shapes = [w[n].shape for n in sm_names]
    for kind, flat in zip(kinds, _adamw(recv[-1], wf, mf, vf, "adamw_small")):
        for n, a in zip(sm_names, _unflatten(flat, offs, shapes)):
            outs[kind + "_" + n] = a
    result = [loss, grad_x]
    for kind in ("grad", "delta", "new_m", "new_v"):
        result += [outs[kind + "_" + n] for n in WEIGHTS]
    return tuple(result)
```

```python
import functools

import jax
import jax.numpy as jnp
import numpy as np
from jax import lax
from jax.experimental import pallas as pl
from jax.experimental.pallas import tpu as pltpu

F32 = jnp.float32
BF16 = jnp.bfloat16
HIGHEST = lax.Precision.HIGHEST
SSD_PREC = lax.Precision.HIGH

D_MODEL = 1024
DEPTH = 2
N_META = 16
HEADS = 8
Q_LORA = 768
KV_LORA = 256
QK_NOPE = 128
QK_ROPE = 64
V_HEAD = 128
ROPE_THETA = 10000.0
SSD_INNER = 2048
SSD_HEAD_DIM = 64
SSD_HEADS = 32
SSD_GROUPS = 4
SSD_STATE = 128
SSD_CONV = 4
SSD_CONV_DIM = SSD_INNER + 2 * SSD_GROUPS * SSD_STATE
CHUNK = 128
D_FF = 2816
FFN_CONV = 3
LN_EPS = 1e-5
RMS_EPS = 1e-6
ALPHA = (2 * DEPTH) ** 0.25
IN_SIZES = (Q_LORA, KV_LORA, QK_ROPE, SSD_INNER, SSD_CONV_DIM, SSD_HEADS, D_MODEL, D_MODEL)
ATT_SCALE = (QK_NOPE + QK_ROPE) ** -0.5
NEG_INF = -1e30
ADAM_LR, ADAM_B1, ADAM_B2, ADAM_EPS, ADAM_WD, ADAM_STEP = 0.001, 0.9, 0.999, 1e-08, 0.01, 10

LANES = 128
SUBLANES = 8
VMEM_BYTES = 64 * 1024 * 1024
N_DEV = 8

OQ, OKV, OZ, OXBC, OGA, OGS, OKPE, ODT = 0, 768, 1024, 3072, 6144, 7168, 8192, 8320
IN_PAD = 8448
QHEAD = 256

ROW_TILE = 640
MM_COL_TILE = 1408
MM_ROW_TILE = 1664
MM_VMEM_BUDGET = 46 * 1024 * 1024
MM_K_TILE = 2816
MM_TOKEN_K_TILE = 1664
ATT_TILE = 640
ATT_HEADS_PER_STEP = 4
BF16_ROWS = 16
ROW_BUDGET = 7 * 1024 * 1024
ADAM_ELEMS = 160 * 1024


def _pick(n, target, q=LANES):
    assert n % q == 0, (n, q)
    units = n // q
    best = q
    for d in range(1, units + 1):
        if units % d == 0 and d * q <= target:
            best = d * q
    return best


def _pick_rows(n, row_bytes):
    return _pick(n, max(BF16_ROWS, ROW_BUDGET // row_bytes), BF16_ROWS)


def _params(sem, est_bytes):
    limit = int(min(VMEM_BYTES - (6 << 20), max(32 << 20, 2 * est_bytes + (8 << 20))))
    return pltpu.CompilerParams(dimension_semantics=sem, vmem_limit_bytes=limit)


def _nbytes(shape, dtype):
    return int(np.prod(shape)) * jnp.dtype(dtype).itemsize


def _mm(a, b, out_dtype, name, ta=False, tb=False, add=None):
    assert not (ta and tb)
    if ta:
        K, M = a.shape
        tm = _pick(M, MM_COL_TILE)
        tk = _pick(K, MM_TOKEN_K_TILE)
    else:
        M, K = a.shape
        tk = _pick(K, MM_K_TILE)
    N, K2 = (b.shape if tb else b.shape[::-1])
    assert K == K2
    tn = _pick(N, MM_COL_TILE)
    nk = K // tk

    def vmem_estimate(tm):
        e = 2 * (tm * tk * a.dtype.itemsize + tk * tn * b.dtype.itemsize + tm * tn * jnp.dtype(out_dtype).itemsize)
        e += tm * tn * 4 + tm * tk * 2
        return e + (tm * tn * 4 if nk > 1 else 0) + (2 * tm * tn * 4 if add is not None else 0)

    if not ta:
        tm = _pick(M, MM_ROW_TILE, BF16_ROWS)
        while vmem_estimate(tm) > MM_VMEM_BUDGET and tm > BF16_ROWS:
            tm = _pick(M, tm - BF16_ROWS, BF16_ROWS)
    dn = (((0,), (0,)), ((), ())) if ta else ((((1,), (1,)), ((), ())) if tb else (((1,), (0,)), ((), ())))

    def body(*refs):
        a_ref, b_ref = refs[:2]
        add_ref = refs[2] if add is not None else None
        o_ref = refs[2 + (add is not None)]
        d = lax.dot_general(a_ref[...].astype(BF16), b_ref[...].astype(BF16), dn, preferred_element_type=F32)

        def finish(r):
            if add is not None:
                r = r + add_ref[...].astype(F32)
            o_ref[...] = r.astype(out_dtype)

        if nk == 1:
            finish(d)
            return
        acc = refs[-1]
        k = pl.program_id(2)

        @pl.when(k == 0)
        def _():
            acc[...] = d

        @pl.when((k > 0) & (k < nk - 1))
        def _():
            acc[...] += d

        @pl.when(k == nk - 1)
        def _():
            finish(acc[...] + d)

    if ta:
        a_spec = pl.BlockSpec((tk, tm), lambda i, j, k: (k, i))
    else:
        a_spec = pl.BlockSpec((tm, tk), lambda i, j, k: (i, k))
    b_spec = pl.BlockSpec((tn, tk), lambda i, j, k: (j, k)) if tb else pl.BlockSpec((tk, tn), lambda i, j, k: (k, j))
    in_specs = [a_spec, b_spec]
    args = [a, b]
    est = vmem_estimate(tm)
    if add is not None:
        in_specs.append(pl.BlockSpec((tm, tn), lambda i, j, k: (i, j)))
        args.append(add)
    return pl.pallas_call(
        body, name=name, grid=(M // tm, N // tn, nk), in_specs=in_specs,
        out_specs=pl.BlockSpec((tm, tn), lambda i, j, k: (i, j)),
        out_shape=jax.ShapeDtypeStruct((M, N), out_dtype),
        scratch_shapes=[pltpu.VMEM((tm, tn), F32)] if nk > 1 else [],
        compiler_params=_params(("parallel", "parallel", "arbitrary"), est),
    )(*args)


def _row(arr, bw=None, cb=0, grp=False, diff=True):
    return dict(arr=arr, bw=arr.shape[1] if bw is None else bw, cb=cb, grp=grp, diff=diff)


def _out(width, dtype, bw=None, grp=False):
    return dict(width=width, dtype=dtype, bw=width if bw is None else bw, grp=grp)


def _spec_rows(d, tm):
    return pl.BlockSpec((tm, d["bw"]), lambda g, i, cb=d["cb"], gr=d["grp"]: (i, cb + (g if gr else 0)))


def _spec_const(d):
    return pl.BlockSpec((d["arr"].shape[0], d["bw"]), lambda g, i, cb=d["cb"], gr=d["grp"]: (0, cb + (g if gr else 0)))


def _rw_fwd(fn, rows, consts, outs, name, ng=1):
    Tp = rows[0]["arr"].shape[0]
    tm = _pick_rows(Tp, 4 * (sum(d["bw"] for d in rows) + 2 * sum(o["bw"] for o in outs)))
    nr, ncst = len(rows), len(consts)

    def body(*refs):
        i = pl.program_id(1)
        rowidx = i * tm + lax.broadcasted_iota(jnp.int32, (tm, 1), 0)
        rv = [r[...].astype(F32) for r in refs[:nr]]
        cv = [c[...] for c in refs[nr:nr + ncst]]
        vals = fn(rowidx, *rv, *cv)
        for o, v in zip(refs[nr + ncst:], vals):
            o[...] = v.astype(o.dtype)

    est = sum(tm * d["bw"] * 4 for d in rows) + sum(tm * o["bw"] * 4 for o in outs)
    return pl.pallas_call(
        body, name=name, grid=(ng, Tp // tm),
        in_specs=[_spec_rows(d, tm) for d in rows] + [_spec_const(d) for d in consts],
        out_specs=[pl.BlockSpec((tm, o["bw"]), lambda g, i, gr=o["grp"]: (i, g if gr else 0)) for o in outs],
        out_shape=[jax.ShapeDtypeStruct((Tp, o["width"]), o["dtype"]) for o in outs],
        compiler_params=_params(("parallel", "parallel"), 3 * est),
    )(*[d["arr"] for d in rows], *[d["arr"] for d in consts])


def _rw_bwd(fn, rows, consts, cots, drow_dtypes, name, ng=1):
    Tp = rows[0]["arr"].shape[0]
    tm = _pick_rows(Tp, 4 * (3 * sum(d["bw"] for d in rows) + 2 * sum(d["bw"] for d in cots)))
    nr, ncst, nct = len(rows), len(consts), len(cots)
    drows = [k for k, d in enumerate(rows) if d["diff"]]
    dcsts = [k for k, d in enumerate(consts) if d["diff"]]
    for k in drows:
        assert rows[k]["grp"] or ng == 1

    def body(*refs):
        g = pl.program_id(0)
        i = pl.program_id(1)
        rowidx = i * tm + lax.broadcasted_iota(jnp.int32, (tm, 1), 0)
        rv = [r[...].astype(F32) for r in refs[:nr]]
        cv = [c[...] for c in refs[nr:nr + ncst]]
        ct = tuple(r[...].astype(F32) for r in refs[nr + ncst:nr + ncst + nct])
        orefs = refs[nr + ncst + nct:]

        def f(*dargs):
            rr, cc = list(rv), list(cv)
            for k, v in zip(drows, dargs[:len(drows)]):
                rr[k] = v
            for k, v in zip(dcsts, dargs[len(drows):]):
                cc[k] = v
            return tuple(fn(rowidx, *rr, *cc))

        _, vjp = jax.vjp(f, *[rv[k] for k in drows], *[cv[k] for k in dcsts])
        grads = vjp(ct)
        for o, v in zip(orefs[:len(drows)], grads[:len(drows)]):
            o[...] = v.astype(o.dtype)
        for k, o, v in zip(dcsts, orefs[len(drows):], grads[len(drows):]):
            first = (i == 0) if consts[k]["grp"] else ((i == 0) & (g == 0))

            @pl.when(first)
            def _(o=o, v=v):
                o[...] = v

            @pl.when(jnp.logical_not(first))
            def _(o=o, v=v):
                o[...] += v

    out_specs, out_shape = [], []
    for k, dt in zip(drows, drow_dtypes):
        d = rows[k]
        out_specs.append(pl.BlockSpec((tm, d["bw"]), lambda g, i, gr=d["grp"]: (i, g if gr else 0)))
        out_shape.append(jax.ShapeDtypeStruct((Tp, d["bw"] * (ng if d["grp"] else 1)), dt))
    for k in dcsts:
        d = consts[k]
        r = d["arr"].shape[0]
        out_specs.append(pl.BlockSpec((r, d["bw"]), lambda g, i, gr=d["grp"]: (0, g if gr else 0)))
        out_shape.append(jax.ShapeDtypeStruct((r, d["bw"] * (ng if d["grp"] else 1)), F32))
    est = sum(tm * d["bw"] * 4 for d in rows) * 2 + sum(tm * d["bw"] * 4 for d in cots)
    res = pl.pallas_call(
        body, name=name, grid=(ng, Tp // tm),
        in_specs=[_spec_rows(d, tm) for d in rows] + [_spec_const(d) for d in consts] + [_spec_rows(d, tm) for d in cots],
        out_specs=out_specs, out_shape=out_shape,
        compiler_params=_params(("arbitrary", "arbitrary"), 3 * est),
    )(*[d["arr"] for d in rows], *[d["arr"] for d in consts], *[d["arr"] for d in cots])
    return list(res[:len(drows)]), list(res[len(drows):])


def _sigmoid(x):
    return 0.5 * jnp.tanh(0.5 * x) + 0.5


def _silu(x):
    return x * _sigmoid(x)


def _softplus(x):
    return jnp.maximum(x, 0.0) + jnp.log(1.0 + jnp.exp(-jnp.abs(x)))


def _layer_norm(x, g, b):
    mu = jnp.mean(x, axis=-1, keepdims=True)
    xc = x - mu
    var = jnp.mean(xc * xc, axis=-1, keepdims=True)
    return xc * lax.rsqrt(var + LN_EPS) * g + b


def _rms_norm(x, g):
    return x * lax.rsqrt(jnp.mean(x * x, axis=-1, keepdims=True) + RMS_EPS) * g


def _rope(r, cos, sin, rot):
    return r * cos + jnp.dot(r, rot, precision=HIGHEST, preferred_element_type=F32) * sin


def _make_stage_fns(npad):
    def fn_ln_masked(rowidx, x, g, b):
        return (jnp.where(rowidx >= npad, _layer_norm(x, g, b), 0.0),)

    def fn_in_post(rowidx, ql, kvl, kpe, dtr, cos, sin, rot, qg, kvg, dtb):
        qn = _rms_norm(ql, qg)
        kvn = _rms_norm(kvl, kvg)
        kr = _rope(kpe, cos, sin, rot)
        lane = lax.broadcasted_iota(jnp.int32, (1, LANES), 1)
        dt = jnp.where((rowidx >= npad) & (lane < SSD_HEADS), _softplus(dtr + dtb), 0.0)
        return qn, kvn, jnp.concatenate([kr] * HEADS, axis=1), dt

    def fn_q_post(rowidx, q, cos, sin, rot):
        rr = _rope(q[:, QK_NOPE:], cos, sin, rot)
        return (jnp.concatenate([q[:, :QK_NOPE], rr], axis=1) * ATT_SCALE,)

    def fn_gated_norm(rowidx, y, xs, z, dskip, g):
        v = (y + xs * dskip) * _silu(z)
        return (v * lax.rsqrt(jnp.mean(v * v, axis=-1, keepdims=True) + RMS_EPS) * g,)

    def fn_mix(rowidx, ga, gs, ya, ys):
        return (_sigmoid(ga) * ya + _sigmoid(gs) * ys,)

    def fn_res_ln(rowidx, h, r, g, b):
        return (jnp.where(rowidx >= npad, _layer_norm(ALPHA * h + r, g, b), 0.0),)

    def fn_glu(rowidx, u):
        return (_silu(u[:, :D_FF]) * u[:, D_FF:],)

    return dict(ln=fn_ln_masked, in_post=fn_in_post, q_post=fn_q_post, gated=fn_gated_norm, mix=fn_mix,
                res_ln=fn_res_ln, glu=fn_glu)


def _conv_tiles(Tp, C):
    return _pick(Tp, ROW_TILE), _pick(C, MM_COL_TILE)


def _conv_fwd(x, xoff, C, w8, b, K, act, npad, name):
    Tp = x.shape[0]
    tm, tc = _conv_tiles(Tp, C)
    assert xoff % tc == 0
    cb0 = xoff // tc
    rb = tm // SUBLANES

    def body(prev_ref, main_ref, w_ref, b_ref, o_ref):
        i = pl.program_id(1)
        main = main_ref[...].astype(F32)
        prev = jnp.where(i > 0, prev_ref[...].astype(F32), 0.0)
        ext = jnp.concatenate([prev, main], axis=0)
        acc = b_ref[...] + w_ref[K - 1:K, :] * main
        for k in range(K - 1):
            s = K - 1 - k
            acc = acc + w_ref[k:k + 1, :] * pltpu.roll(ext, s, 0)[SUBLANES:, :]
        if act:
            rowidx = i * tm + lax.broadcasted_iota(jnp.int32, (tm, 1), 0)
            acc = jnp.where(rowidx >= npad, _silu(acc), 0.0)
        o_ref[...] = acc.astype(o_ref.dtype)

    return pl.pallas_call(
        body, name=name, grid=(C // tc, Tp // tm),
        in_specs=[pl.BlockSpec((SUBLANES, tc), lambda g, i: (jnp.maximum(i * rb - 1, 0), cb0 + g)),
                  pl.BlockSpec((tm, tc), lambda g, i: (i, cb0 + g)),
                  pl.BlockSpec((SUBLANES, tc), lambda g, i: (0, g)),
                  pl.BlockSpec((1, tc), lambda g, i: (0, g))],
        out_specs=pl.BlockSpec((tm, tc), lambda g, i: (i, g)),
        out_shape=jax.ShapeDtypeStruct((Tp, C), F32),
        compiler_params=_params(("parallel", "parallel"), 8 * tm * tc * 4),
    )(x, x, w8, b)


def _conv_bwd(x, xoff, C, w8, b, dy, K, act, npad, name):
    Tp = x.shape[0]
    tm, tc = _conv_tiles(Tp, C)
    cb0 = xoff // tc
    rb = tm // SUBLANES
    ni = Tp // tm
    last_rb = Tp // SUBLANES - 1
    n = tm + 2 * SUBLANES

    def body(xp_ref, xm_ref, xn_ref, dym_ref, dyn_ref, w_ref, b_ref, dx_ref, dw_ref, db_ref):
        i = pl.program_id(1)
        prev = jnp.where(i > 0, xp_ref[...].astype(F32), 0.0)
        ext = jnp.concatenate([prev, xm_ref[...].astype(F32), xn_ref[...].astype(F32)], axis=0)
        dyn = jnp.where(i < ni - 1, dyn_ref[...].astype(F32), 0.0)
        dpre = jnp.concatenate([jnp.zeros((SUBLANES, tc), F32), dym_ref[...].astype(F32), dyn], axis=0)
        shifted = [ext if k == K - 1 else pltpu.roll(ext, K - 1 - k, 0) for k in range(K)]
        if act:
            pre = b_ref[...] + sum(w_ref[k:k + 1, :] * shifted[k] for k in range(K))
            rowidx = i * tm - SUBLANES + lax.broadcasted_iota(jnp.int32, (n, 1), 0)
            sg = _sigmoid(pre)
            dpre = jnp.where(rowidx >= npad, dpre * sg * (1.0 + pre * (1.0 - sg)), 0.0)
        dx = w_ref[K - 1:K, :] * dpre
        for k in range(K - 1):
            dx = dx + w_ref[k:k + 1, :] * pltpu.roll(dpre, n - (K - 1 - k), 0)
        dx_ref[...] = dx[SUBLANES:SUBLANES + tm, :].astype(dx_ref.dtype)

        @pl.when(i == 0)
        def _():
            dw_ref[...] = jnp.zeros_like(dw_ref)
            db_ref[...] = jnp.zeros_like(db_ref)

        dmain = dpre[SUBLANES:SUBLANES + tm, :]
        for k in range(K):
            dw_ref[k:k + 1, :] += jnp.sum(dmain * shifted[k][SUBLANES:SUBLANES + tm, :], axis=0, keepdims=True)
        db_ref[...] += jnp.sum(dmain, axis=0, keepdims=True)

    return pl.pallas_call(
        body, name=name, grid=(C // tc, ni),
        in_specs=[pl.BlockSpec((SUBLANES, tc), lambda g, i: (jnp.maximum(i * rb - 1, 0), cb0 + g)),
                  pl.BlockSpec((tm, tc), lambda g, i: (i, cb0 + g)),
                  pl.BlockSpec((SUBLANES, tc), lambda g, i: (jnp.minimum((i + 1) * rb, last_rb), cb0 + g)),
                  pl.BlockSpec((tm, tc), lambda g, i: (i, g)),
                  pl.BlockSpec((SUBLANES, tc), lambda g, i: (jnp.minimum((i + 1) * rb, last_rb), g)),
                  pl.BlockSpec((SUBLANES, tc), lambda g, i: (0, g)),
                  pl.BlockSpec((1, tc), lambda g, i: (0, g))],
        out_specs=[pl.BlockSpec((tm, tc), lambda g, i: (i, g)),
                   pl.BlockSpec((SUBLANES, tc), lambda g, i: (0, g)),
                   pl.BlockSpec((1, tc), lambda g, i: (0, g))],
        out_shape=[jax.ShapeDtypeStruct((Tp, C), BF16), jax.ShapeDtypeStruct((SUBLANES, C), F32),
                   jax.ShapeDtypeStruct((1, C), F32)],
        compiler_params=_params(("parallel", "arbitrary"), 14 * tm * tc * 4),
    )(x, x, x, dy, dy, w8, b)


def _split_refs(refs, n_in, n_out, n_scratch, nbg):
    cuts = np.cumsum([0, n_in, nbg, n_out, nbg, n_scratch])
    return tuple(refs[a:b] for a, b in zip(cuts[:-1], cuts[1:])) + (refs[cuts[-1]:],)


def _flash_fwd(q, kv, kr8, npad, name, bg=None):
    Tp = q.shape[0]
    t = _pick(Tp, ATT_TILE)
    hp = ATT_HEADS_PER_STEP
    nb = Tp // t
    ng = HEADS // hp
    nbg = bg.n if bg else 0
    nt = (((1,), (1,)), ((), ()))
    tn = (((0,), (0,)), ((), ()))

    def body(*refs):
        (q_ref, kn_ref, kr_ref, v_ref), bg_in, (o_ref, lse_ref), bg_out, (m_sc, l_sc, acc_sc), bg_sems = _split_refs(
            refs, 4, 2, 3, nbg)
        g = pl.program_id(0)
        qi = pl.program_id(1)
        ki = pl.program_id(2)
        if bg:
            @pl.when((g == 0) & (qi == 0) & (ki == 0))
            def _():
                bg.start(bg_in, bg_out, bg_sems)

        @pl.when(ki == 0)
        def _():
            m_sc[...] = jnp.full_like(m_sc, NEG_INF)
            l_sc[...] = jnp.zeros_like(l_sc)
            acc_sc[...] = jnp.zeros_like(acc_sc)

        def step(masked):
            kr = kr_ref[...]
            if masked:
                key = ki * t + lax.broadcasted_iota(jnp.int32, (t, t), 0)
                qry = qi * t + lax.broadcasted_iota(jnp.int32, (t, t), 1)
                visible = (key <= qry) & (key >= npad)
            for hh in range(hp):
                k = jnp.concatenate([kn_ref[:, hh * QK_NOPE:(hh + 1) * QK_NOPE], kr], axis=1)
                st = lax.dot_general(k, q_ref[:, hh * QHEAD:(hh + 1) * QHEAD], nt, preferred_element_type=F32)
                if masked:
                    st = jnp.where(visible, st, NEG_INF)
                vs = slice(hh * V_HEAD, (hh + 1) * V_HEAD)
                m_prev = m_sc[hh]
                m_new = jnp.maximum(m_prev, jnp.max(st, axis=0, keepdims=True))
                pt = jnp.exp(st - m_new)
                a = jnp.exp(m_prev - m_new)
                l_sc[hh] = a * l_sc[hh] + jnp.sum(pt, axis=0, keepdims=True)
                acc_sc[vs, :] = a * acc_sc[vs, :] + lax.dot_general(v_ref[:, vs], pt.astype(BF16), tn,
                                                                    preferred_element_type=F32)
                m_sc[hh] = m_new

        need_mask = (ki == qi) | (ki == 0)

        @pl.when((ki <= qi) & need_mask)
        def _():
            step(True)

        @pl.when((ki <= qi) & jnp.logical_not(need_mask))
        def _():
            step(False)

        @pl.when(ki == qi)
        def _():
            for hh in range(hp):
                vs = slice(hh * V_HEAD, (hh + 1) * V_HEAD)
                l = l_sc[hh]
                o_ref[:, vs] = (acc_sc[vs, :] / l).T.astype(o_ref.dtype)
                lse_ref[hh * SUBLANES:(hh + 1) * SUBLANES, :] = jnp.broadcast_to(m_sc[hh] + jnp.log(l), (SUBLANES, t))

        if bg:
            @pl.when((g == ng - 1) & (qi == nb - 1) & (ki == nb - 1))
            def _():
                bg.wait(bg_in, bg_out, bg_sems)

    kmin = lambda qi, ki: jnp.minimum(ki, qi)
    return pl.pallas_call(
        body, name=name, grid=(ng, nb, nb),
        in_specs=[pl.BlockSpec((t, hp * QHEAD), lambda g, qi, ki: (qi, g)),
                  pl.BlockSpec((t, hp * QK_NOPE), lambda g, qi, ki: (kmin(qi, ki), g)),
                  pl.BlockSpec((t, LANES), lambda g, qi, ki: (kmin(qi, ki), 0)),
                  pl.BlockSpec((t, hp * V_HEAD), lambda g, qi, ki: (kmin(qi, ki), ng + g))] + (bg.specs if bg else []),
        out_specs=[pl.BlockSpec((t, hp * V_HEAD), lambda g, qi, ki: (qi, g)),
                   pl.BlockSpec((hp * SUBLANES, t), lambda g, qi, ki: (g, qi))] + (bg.specs if bg else []),
        out_shape=[jax.ShapeDtypeStruct((Tp, HEADS * V_HEAD), F32), jax.ShapeDtypeStruct((HEADS * SUBLANES, Tp), F32)]
        + (bg.out_shape if bg else []),
        scratch_shapes=[pltpu.VMEM((hp, 1, t), F32), pltpu.VMEM((hp, 1, t), F32), pltpu.VMEM((hp * V_HEAD, t), F32)]
        + (bg.scratch if bg else []),
        compiler_params=_params(("arbitrary",) * 3 if bg else ("parallel", "parallel", "arbitrary"), 8 * hp * t * t * 4),
    )(q, kv, kr8, kv, *(bg.arrs if bg else []))


def _attn_delta(do, o, name):
    Tp = do.shape[0]
    tm = _pick(Tp, MM_TOKEN_K_TILE)

    def body(do_ref, o_ref, d_ref):
        prod = do_ref[...] * o_ref[...]
        ones = jnp.ones((SUBLANES, V_HEAD), F32)
        d_ref[...] = lax.dot_general(ones, prod, (((1,), (1,)), ((), ())), precision=HIGHEST,
                                     preferred_element_type=F32)

    return pl.pallas_call(
        body, name=name, grid=(HEADS, Tp // tm),
        in_specs=[pl.BlockSpec((tm, V_HEAD), lambda h, i: (i, h)), pl.BlockSpec((tm, V_HEAD), lambda h, i: (i, h))],
        out_specs=pl.BlockSpec((SUBLANES, tm), lambda h, i: (h, i)),
        out_shape=jax.ShapeDtypeStruct((HEADS * SUBLANES, Tp), F32),
        compiler_params=_params(("parallel", "parallel"), 4 * tm * V_HEAD * 4),
    )(do, o)


def _flash_bwd(q, kv, kr8, do, lse, delta, npad, name, bg=None):
    Tp = q.shape[0]
    t = _pick(Tp, ATT_TILE)
    nb = Tp // t
    nbg = bg.n if bg else 0
    nt = (((1,), (1,)), ((), ()))
    tn = (((0,), (0,)), ((), ()))

    def body(*refs):
        ((q_ref, kn_ref, kr_ref, v_ref, do_ref, lse_ref, dl_ref), bg_in, (dq_ref, dkn_ref, dkr_ref, dv_ref), bg_out,
         (dk_sc, dv_sc), bg_sems) = _split_refs(refs, 7, 4, 2, nbg)
        h = pl.program_id(0)
        ki = pl.program_id(1)
        qi = pl.program_id(2)
        if bg:
            @pl.when((h == 0) & (ki == 0) & (qi == 0))
            def _():
                bg.start(bg_in, bg_out, bg_sems)

        @pl.when(qi == 0)
        def _():
            dk_sc[...] = jnp.zeros_like(dk_sc)
            dv_sc[...] = jnp.zeros_like(dv_sc)

        def step(masked):
            qv = q_ref[...]
            k = jnp.concatenate([kn_ref[...], kr_ref[...]], axis=1)
            st = lax.dot_general(k, qv, nt, preferred_element_type=F32)
            if masked:
                key = ki * t + lax.broadcasted_iota(jnp.int32, (t, t), 0)
                qry = qi * t + lax.broadcasted_iota(jnp.int32, (t, t), 1)
                st = jnp.where((key <= qry) & (key >= npad), st, NEG_INF)
            pt = jnp.exp(st - lse_ref[0:1, :])
            dob = do_ref[...].astype(BF16)
            dv_sc[...] += jnp.dot(pt.astype(BF16), dob, preferred_element_type=F32)
            dpt = lax.dot_general(v_ref[...], dob, nt, preferred_element_type=F32)
            dst = (pt * (dpt - dl_ref[0:1, :])).astype(BF16)
            dk_sc[...] += jnp.dot(dst, qv, preferred_element_type=F32)
            dqc = lax.dot_general(dst, k, tn, preferred_element_type=F32)
            rows = pl.ds(pl.multiple_of(qi * t, t), t)

            @pl.when(ki == 0)
            def _():
                dq_ref[rows, :] = dqc

            @pl.when(ki > 0)
            def _():
                dq_ref[rows, :] += dqc

        need_mask = (ki == qi) | (ki == 0)

        @pl.when((qi >= ki) & need_mask)
        def _():
            step(True)

        @pl.when((qi >= ki) & jnp.logical_not(need_mask))
        def _():
            step(False)

        @pl.when(qi == nb - 1)
        def _():
            dkn_ref[...] = dk_sc[:, :QK_NOPE].astype(dkn_ref.dtype)
            dkr_ref[...] = dk_sc[:, QK_NOPE:].astype(dkr_ref.dtype)
            dv_ref[...] = dv_sc[...].astype(dv_ref.dtype)

        if bg:
            @pl.when((h == HEADS - 1) & (ki == nb - 1) & (qi == nb - 1))
            def _():
                bg.wait(bg_in, bg_out, bg_sems)

    qmap = lambda h, ki, qi: (jnp.maximum(qi, ki), h)
    kmap = lambda h, ki, qi: (ki, h)
    est = 2 * Tp * QHEAD * 4 + 8 * t * t * 4
    return pl.pallas_call(
        body, name=name, grid=(HEADS, nb, nb),
        in_specs=[pl.BlockSpec((t, QHEAD), qmap),
                  pl.BlockSpec((t, QK_NOPE), kmap),
                  pl.BlockSpec((t, LANES), kmap),
                  pl.BlockSpec((t, V_HEAD), lambda h, ki, qi: (ki, HEADS + h)),
                  pl.BlockSpec((t, V_HEAD), qmap),
                  pl.BlockSpec((SUBLANES, t), lambda h, ki, qi: (h, jnp.maximum(qi, ki))),
                  pl.BlockSpec((SUBLANES, t), lambda h, ki, qi: (h, jnp.maximum(qi, ki)))] + (bg.specs if bg else []),
        out_specs=[pl.BlockSpec((Tp, QHEAD), lambda h, ki, qi: (0, h)),
                   pl.BlockSpec((t, QK_NOPE), kmap),
                   pl.BlockSpec((t, LANES), kmap),
                   pl.BlockSpec((t, V_HEAD), kmap)] + (bg.specs if bg else []),
        out_shape=[jax.ShapeDtypeStruct((Tp, HEADS * QHEAD), F32),
                   jax.ShapeDtypeStruct((Tp, HEADS * QK_NOPE), BF16),
                   jax.ShapeDtypeStruct((Tp, HEADS * LANES), F32),
                   jax.ShapeDtypeStruct((Tp, HEADS * V_HEAD), BF16)] + (bg.out_shape if bg else []),
        scratch_shapes=[pltpu.VMEM((t, QHEAD), F32), pltpu.VMEM((t, V_HEAD), F32)] + (bg.scratch if bg else []),
        compiler_params=_params(("arbitrary",) * 3 if bg else ("parallel", "arbitrary", "arbitrary"), est),
    )(q, kv, kr8, kv, do, lse, delta, *(bg.arrs if bg else []))


GW = SSD_INNER // SSD_GROUPS
PAIRS_PER_GROUP = GW // LANES
XB = SSD_INNER // GW
NT_DIMS = (((1,), (1,)), ((), ()))
TN_DIMS = (((0,), (0,)), ((), ()))


def _ssd_common(xs_ref, dt_ref, alog_ref, e_ref):
    a_neg = -jnp.exp(alog_ref[...])
    dt = dt_ref[...]
    li = lax.broadcasted_iota(jnp.int32, (CHUNK, CHUNK), 0)
    si = lax.broadcasted_iota(jnp.int32, (CHUNK, CHUNK), 1)
    tril = li >= si
    tri = tril.astype(F32)
    acs = jnp.dot(tri, dt * a_neg, precision=SSD_PREC, preferred_element_type=F32)
    e = e_ref[...]
    dte = jnp.dot(dt, e, precision=SSD_PREC, preferred_element_type=F32)
    acse = jnp.dot(acs, e, precision=SSD_PREC, preferred_element_type=F32)
    x = xs_ref[...] * dte
    alast = acse[CHUNK - 1:CHUNK, :]
    return dict(a_neg=a_neg, dt=dt, tril=tril, tri=tri, acs=acs, acs_t=acs.T, e=e, dte=dte, acse=acse, x=x,
                p_e=jnp.exp(acse), w_e=jnp.exp(alast - acse), dl_e=jnp.exp(alast), li=li, si=si)


def _decay(cm, head):
    col = cm["acs"][:, head:head + 1]
    row = cm["acs_t"][head:head + 1, :]
    return jnp.exp(jnp.where(cm["tril"], col - row, -jnp.inf))


def _ssd_fwd(xbc, dt, alog, e, name):
    Tp = xbc.shape[0]
    nc = Tp // CHUNK

    def body(xs_ref, b_ref, c_ref, dt_ref, alog_ref, e_ref, y_ref, st_ref, st_sc):
        @pl.when(pl.program_id(0) == 0)
        def _():
            st_sc[...] = jnp.zeros_like(st_sc)

        cm = _ssd_common(xs_ref, dt_ref, alog_ref, e_ref)
        st_ref[0] = st_sc[...]
        lane = lax.broadcasted_iota(jnp.int32, (CHUNK, LANES), 1)
        for g in range(SSD_GROUPS):
            gs = slice(g * GW, (g + 1) * GW)
            cg = c_ref[:, g * SSD_STATE:(g + 1) * SSD_STATE].astype(BF16)
            bg = b_ref[:, g * SSD_STATE:(g + 1) * SSD_STATE].astype(BF16)
            cb = lax.dot_general(cg, bg, NT_DIMS, preferred_element_type=F32)
            stg = st_sc[:, gs]
            yoff = jnp.dot(cg, stg.astype(BF16), preferred_element_type=F32) * cm["p_e"][:, gs]
            xg = cm["x"][:, gs]
            for jp in range(PAIRS_PER_GROUP):
                j = g * PAIRS_PER_GROUP + jp
                xp = xg[:, jp * LANES:(jp + 1) * LANES].astype(BF16)
                ys = []
                for head in (2 * j, 2 * j + 1):
                    m = (cb * _decay(cm, head)).astype(BF16)
                    ys.append(jnp.dot(m, xp, preferred_element_type=F32))
                y_ref[:, j * LANES:(j + 1) * LANES] = (jnp.where(lane < SSD_HEAD_DIM, ys[0], ys[1])
                                                       + yoff[:, jp * LANES:(jp + 1) * LANES])
            snew = lax.dot_general(bg, (cm["w_e"][:, gs] * xg).astype(BF16), TN_DIMS, preferred_element_type=F32)
            st_sc[:, gs] = cm["dl_e"][:, gs] * stg + snew

    return pl.pallas_call(
        body, name=name, grid=(nc,),
        in_specs=[pl.BlockSpec((CHUNK, SSD_INNER), lambda c: (c, 0)),
                  pl.BlockSpec((CHUNK, GW), lambda c: (c, XB)),
                  pl.BlockSpec((CHUNK, GW), lambda c: (c, XB + 1)),
                  pl.BlockSpec((CHUNK, LANES), lambda c: (c, 0)),
                  pl.BlockSpec((1, LANES), lambda c: (0, 0)),
                  pl.BlockSpec((LANES, SSD_INNER), lambda c: (0, 0))],
        out_specs=[pl.BlockSpec((CHUNK, SSD_INNER), lambda c: (c, 0)),
                   pl.BlockSpec((1, SSD_STATE, SSD_INNER), lambda c: (c, 0, 0))],
        out_shape=[jax.ShapeDtypeStruct((Tp, SSD_INNER), F32), jax.ShapeDtypeStruct((nc, SSD_STATE, SSD_INNER), F32)],
        scratch_shapes=[pltpu.VMEM((SSD_STATE, SSD_INNER), F32)],
        compiler_params=_params(("arbitrary",), 24 * CHUNK * SSD_INNER * 4),
    )(xbc, xbc, xbc, dt, alog, e)


def _ssd_bwd(xbc, dt, alog, e, dy, dxs_skip, states, name):
    Tp = xbc.shape[0]
    nc = Tp // CHUNK
    rev = lambda c: nc - 1 - c

    def body(xs_ref, b_ref, c_ref, dt_ref, alog_ref, e_ref, dy_ref, skip_ref, st_ref,
             dxbc_ref, ddt_ref, dalog_ref, dst_sc, dx_sc, t_sc, tw_sc):
        @pl.when(pl.program_id(0) == 0)
        def _():
            dst_sc[...] = jnp.zeros_like(dst_sc)
            dalog_ref[...] = jnp.zeros_like(dalog_ref)

        cm = _ssd_common(xs_ref, dt_ref, alog_ref, e_ref)
        lane = lax.broadcasted_iota(jnp.int32, (CHUNK, LANES), 1)
        dacs_col = jnp.zeros((CHUNK, LANES), F32)
        dacs_row = jnp.zeros((LANES, CHUNK), F32)
        t_last = []
        for g in range(SSD_GROUPS):
            gs = slice(g * GW, (g + 1) * GW)
            cg = c_ref[:, g * SSD_STATE:(g + 1) * SSD_STATE].astype(BF16)
            bg = b_ref[:, g * SSD_STATE:(g + 1) * SSD_STATE].astype(BF16)
            stg = st_ref[0, :, gs]
            stg_b = stg.astype(BF16)
            dstg = dst_sc[:, gs]
            dstg_b = dstg.astype(BF16)
            xg = cm["x"][:, gs]
            dyg = dy_ref[:, gs]
            zg = jnp.dot(cg, stg_b, preferred_element_type=F32)
            dzg = dyg * cm["p_e"][:, gs]
            dzg_b = dzg.astype(BF16)
            dcg = lax.dot_general(dzg_b, stg_b, NT_DIMS, preferred_element_type=F32)
            dst_in = lax.dot_general(cg, dzg_b, TN_DIMS, preferred_element_type=F32)
            dst_in = dst_in + cm["dl_e"][:, gs] * dstg
            t_last.append(jnp.sum(dstg * stg * cm["dl_e"][:, gs], axis=0, keepdims=True))
            weg = cm["w_e"][:, gs]
            dbg = lax.dot_general((weg * xg).astype(BF16), dstg_b, NT_DIMS, preferred_element_type=F32)
            gg = jnp.dot(bg, dstg_b, preferred_element_type=F32)
            dxg = weg * gg
            tw_sc[:, gs] = xg * dxg
            t_sc[:, gs] = dzg * zg - xg * dxg
            cb = lax.dot_general(cg, bg, NT_DIMS, preferred_element_type=F32)
            dcb = jnp.zeros((CHUNK, CHUNK), F32)
            for jp in range(PAIRS_PER_GROUP):
                j = g * PAIRS_PER_GROUP + jp
                ps = slice(jp * LANES, (jp + 1) * LANES)
                xp = xg[:, ps].astype(BF16)
                dyp = dyg[:, ps]
                dxp = dxg[:, ps]
                for half, head in enumerate((2 * j, 2 * j + 1)):
                    lam = _decay(cm, head)
                    m32 = cb * lam
                    sel = (lane < SSD_HEAD_DIM) if half == 0 else (lane >= SSD_HEAD_DIM)
                    dye = jnp.where(sel, dyp, 0.0).astype(BF16)
                    dm = lax.dot_general(dye, xp, NT_DIMS, preferred_element_type=F32)
                    w = dm * m32
                    dacs_col = dacs_col + jnp.where(cm["si"] == head, jnp.sum(w, axis=1, keepdims=True), 0.0)
                    dacs_row = dacs_row + jnp.where(cm["li"] == head, jnp.sum(w, axis=0, keepdims=True), 0.0)
                    dcb = dcb + dm * lam
                    dxp = dxp + lax.dot_general(m32.astype(BF16), dye, TN_DIMS, preferred_element_type=F32)
                dx_sc[:, j * LANES:(j + 1) * LANES] = dxp
            dcb_b = dcb.astype(BF16)
            dcg = dcg + jnp.dot(dcb_b, bg, preferred_element_type=F32)
            dbg = dbg + lax.dot_general(dcb_b, cg, TN_DIMS, preferred_element_type=F32)
            dst_sc[:, gs] = dst_in
            dxbc_ref[:, SSD_INNER + g * SSD_STATE:SSD_INNER + (g + 1) * SSD_STATE] = dbg
            dxbc_ref[:, SSD_INNER + GW + g * SSD_STATE:SSD_INNER + GW + (g + 1) * SSD_STATE] = dcg
        e = cm["e"]
        dacs = lax.dot_general(t_sc[...], e, NT_DIMS, precision=SSD_PREC, preferred_element_type=F32)
        dacs = dacs + dacs_col - dacs_row.T
        last_lane = jnp.concatenate(t_last, axis=1) + jnp.sum(tw_sc[...], axis=0, keepdims=True)
        last_head = lax.dot_general(jnp.broadcast_to(last_lane, (SUBLANES, SSD_INNER)), e, NT_DIMS,
                                    precision=SSD_PREC, preferred_element_type=F32)[0:1, :]
        dacs = dacs + jnp.where(cm["li"] == CHUNK - 1, last_head, 0.0)
        da = lax.dot_general(cm["tri"], dacs, TN_DIMS, precision=SSD_PREC, preferred_element_type=F32)
        dx_all = dx_sc[...]
        ddt = da * cm["a_neg"] + lax.dot_general(dx_all * xs_ref[...], e, NT_DIMS, precision=SSD_PREC,
                                                 preferred_element_type=F32)
        ddt_ref[...] = ddt
        dxbc_ref[:, :SSD_INNER] = dx_all * cm["dte"] + skip_ref[...]
        dalog_ref[0:1, :] += jnp.sum(da * cm["dt"], axis=0, keepdims=True) * cm["a_neg"]

    return pl.pallas_call(
        body, name=name, grid=(nc,),
        in_specs=[pl.BlockSpec((CHUNK, SSD_INNER), lambda c: (rev(c), 0)),
                  pl.BlockSpec((CHUNK, GW), lambda c: (rev(c), XB)),
                  pl.BlockSpec((CHUNK, GW), lambda c: (rev(c), XB + 1)),
                  pl.BlockSpec((CHUNK, LANES), lambda c: (rev(c), 0)),
                  pl.BlockSpec((1, LANES), lambda c: (0, 0)),
                  pl.BlockSpec((LANES, SSD_INNER), lambda c: (0, 0)),
                  pl.BlockSpec((CHUNK, SSD_INNER), lambda c: (rev(c), 0)),
                  pl.BlockSpec((CHUNK, SSD_INNER), lambda c: (rev(c), 0)),
                  pl.BlockSpec((1, SSD_STATE, SSD_INNER), lambda c: (rev(c), 0, 0))],
        out_specs=[pl.BlockSpec((CHUNK, SSD_CONV_DIM), lambda c: (rev(c), 0)),
                   pl.BlockSpec((CHUNK, LANES), lambda c: (rev(c), 0)),
                   pl.BlockSpec((SUBLANES, LANES), lambda c: (0, 0))],
        out_shape=[jax.ShapeDtypeStruct((Tp, SSD_CONV_DIM), F32), jax.ShapeDtypeStruct((Tp, LANES), F32),
                   jax.ShapeDtypeStruct((SUBLANES, LANES), F32)],
        scratch_shapes=[pltpu.VMEM((SSD_STATE, SSD_INNER), F32), pltpu.VMEM((CHUNK, SSD_INNER), F32),
                        pltpu.VMEM((CHUNK, SSD_INNER), F32), pltpu.VMEM((CHUNK, SSD_INNER), F32)],
        compiler_params=_params(("arbitrary",), 32 * CHUNK * SSD_INNER * 4),
    )(xbc, xbc, xbc, dt, alog, e, dy, dxs_skip, states)


def _loss_head(h, target, name):
    Tp, d = h.shape
    nt = Tp // LANES

    def body(h_ref, t_ref, dh_ref, l_ref):
        real = pl.program_id(0) > 0
        err = jnp.where(real, h_ref[...] - t_ref[...], 0.0)
        dh_ref[...] = err * (1.0 / d)
        l_ref[...] = jnp.broadcast_to(0.5 * jnp.sum(err * err) * (1.0 / d), l_ref.shape)

    return pl.pallas_call(
        body, name=name, grid=(nt,),
        in_specs=[pl.BlockSpec((LANES, d), lambda i: (i, 0)),
                  pl.BlockSpec((LANES, d), lambda i: (jnp.maximum(i - 1, 0), 0))],
        out_specs=[pl.BlockSpec((LANES, d), lambda i: (i, 0)),
                   pl.BlockSpec((1, SUBLANES, LANES), lambda i: (i, 0, 0))],
        out_shape=[jax.ShapeDtypeStruct((Tp, d), F32), jax.ShapeDtypeStruct((nt, SUBLANES, LANES), F32)],
        compiler_params=_params(("parallel",), 8 * LANES * d * 4),
    )(h, target)


def _adamw(parts, w, m, v, name):
    shape = w.shape
    C = shape[-1]
    R = int(np.prod(shape[:-1]))
    npart = parts.shape[0]
    parts, w, m, v = parts.reshape(npart, R, C), w.reshape(R, C), m.reshape(R, C), v.reshape(R, C)
    lanes = -(-C // LANES) * LANES
    tr = _pick(R, max(BF16_ROWS, ADAM_ELEMS // lanes), BF16_ROWS) if R % BF16_ROWS == 0 else R
    c1 = 1.0 / (1.0 - ADAM_B1 ** ADAM_STEP)
    c2 = 1.0 / (1.0 - ADAM_B2 ** ADAM_STEP)

    def body(p_ref, w_ref, m_ref, v_ref, g_out, d_out, m_out, v_out):
        g = p_ref[0].astype(F32)
        for p in range(1, npart):
            g = g + p_ref[p].astype(F32)
        m_new = ADAM_B1 * m_ref[...] + (1.0 - ADAM_B1) * g
        v_new = ADAM_B2 * v_ref[...] + (1.0 - ADAM_B2) * (g * g)
        g_out[...] = g
        m_out[...] = m_new
        v_out[...] = v_new
        d_out[...] = -ADAM_LR * ((m_new * c1) / (jnp.sqrt(v_new * c2) + ADAM_EPS) + ADAM_WD * w_ref[...])

    spec = pl.BlockSpec((tr, C), lambda i: (i, 0))
    est = npart * tr * lanes * parts.dtype.itemsize + 7 * tr * lanes * 4
    res = pl.pallas_call(
        body, name=name, grid=(R // tr,),
        in_specs=[pl.BlockSpec((npart, tr, C), lambda i: (0, i, 0)), spec, spec, spec],
        out_specs=[spec] * 4, out_shape=[jax.ShapeDtypeStruct((R, C), F32)] * 4,
        compiler_params=_params(("parallel",), est),
    )(parts, w, m, v)
    return [r.reshape(shape) for r in res]


MESH_ID = pl.DeviceIdType.MESH
N_PEERS = N_DEV - 1


def _dev_index(p):
    return 4 * p[0] + 2 * p[1] + p[2]


class _Background:
    def __init__(self, kind, arrs):
        self.kind, self.arrs, self.n = kind, list(arrs), len(arrs)
        self.npairs = N_PEERS if kind == "gather" else N_CHIPS - 1
        lead = (N_DEV,) if kind == "gather" else ()
        self.out_shape = [jax.ShapeDtypeStruct(lead + a.shape, a.dtype) for a in self.arrs]
        self.specs = [pl.BlockSpec(memory_space=pl.ANY)] * self.n
        self.scratch = [pltpu.SemaphoreType.DMA((self.n, self.npairs)), pltpu.SemaphoreType.DMA((self.n, self.npairs)),
                        pltpu.SemaphoreType.DMA((self.n,))]

    def copies(self, in_refs, out_refs, sems):
        send_sems, recv_sems, local_sems = sems
        x, y, c = lax.axis_index("x"), lax.axis_index("y"), lax.axis_index("c")
        sends, recvs, locals_ = [], [], []

        def remote(t, k, src, dst, to):
            return pltpu.make_async_remote_copy(src_ref=src, dst_ref=dst, send_sem=send_sems.at[t, k],
                                                recv_sem=recv_sems.at[t, k], device_id=to, device_id_type=MESH_ID)

        if self.kind == "gather":
            me = _dev_index((x, y, c))
            peers = [(x, y, 1 - c), (1 - x, y, c), (x, 1 - y, c), (1 - x, 1 - y, c),
                     (1 - x, y, 1 - c), (x, 1 - y, 1 - c), (1 - x, 1 - y, 1 - c)]
            for t in range(self.n):
                locals_.append(pltpu.make_async_copy(in_refs[t], out_refs[t].at[me], local_sems.at[t]))
                for k, p in enumerate(peers):
                    sends.append(remote(t, k, in_refs[t], out_refs[t].at[me], p))
                    recvs.append(remote(t, k, in_refs[t], out_refs[t].at[_dev_index(p)], p))
        else:
            mine = 2 * x + y
            peers = [(1 - x, y), (x, 1 - y), (1 - x, 1 - y)]
            for t in range(self.n):
                locals_.append(pltpu.make_async_copy(in_refs[t].at[mine], out_refs[t].at[mine], local_sems.at[t]))
                for k, p in enumerate(peers):
                    theirs = 2 * p[0] + p[1]
                    sends.append(remote(t, k, in_refs[t].at[theirs], out_refs[t].at[mine], (*p, c)))
                    recvs.append(remote(t, k, in_refs[t].at[mine], out_refs[t].at[theirs], (*p, c)))
        return sends, recvs, locals_

    def start(self, in_refs, out_refs, sems):
        sends, _, locals_ = self.copies(in_refs, out_refs, sems)
        for cp in locals_ + sends:
            cp.start()

    def wait(self, in_refs, out_refs, sems):
        sends, recvs, locals_ = self.copies(in_refs, out_refs, sems)
        for cp in recvs:
            cp.wait_recv()
        for cp in sends:
            cp.wait_send()
        for cp in locals_:
            cp.wait()


def _comm_call(body, name, arrs, out_shape, npairs):
    n = len(arrs)
    any_spec = pl.BlockSpec(memory_space=pl.ANY)
    return pl.pallas_call(
        functools.partial(body, n), name=name, in_specs=[any_spec] * n, out_specs=[any_spec] * n, out_shape=out_shape,
        scratch_shapes=[pltpu.SemaphoreType.DMA((n, npairs)), pltpu.SemaphoreType.DMA((n, npairs)),
                        pltpu.SemaphoreType.DMA((n,))],
    )(*arrs)


def _allgather(arrs, name):
    def body(n, *refs):
        src_refs, out_refs = refs[:n], refs[n:2 * n]
        send_sems, recv_sems, local_sems = refs[2 * n:]
        x, y, c = lax.axis_index("x"), lax.axis_index("y"), lax.axis_index("c")
        me, sibling = (x, y, c), (x, y, 1 - c)
        chips = [(1 - x, y), (x, 1 - y), (1 - x, 1 - y)]

        def copy(t, k, block, to, src=None):
            slot = out_refs[t].at[_dev_index(block)]
            return pltpu.make_async_remote_copy(
                src_ref=slot if src is None else src, dst_ref=slot,
                send_sem=send_sems.at[t, k], recv_sem=recv_sems.at[t, k],
                device_id=to, device_id_type=MESH_ID)

        sends, locals_ = [], []
        for t in range(n):
            mine = pltpu.make_async_copy(src_refs[t], out_refs[t].at[_dev_index(me)], local_sems.at[t])
            mine.start()
            locals_.append(mine)
            first = [copy(t, 0, me, sibling, src=src_refs[t])]
            first += [copy(t, 1 + j, me, (*chip, c), src=src_refs[t]) for j, chip in enumerate(chips)]
            for cp in first:
                cp.start()
            sends += first
        for j, chip in enumerate(chips):
            for t in range(n):
                copy(t, 1 + j, (*chip, c), me).wait_recv()
                passed = copy(t, 4 + j, (*chip, c), sibling)
                passed.start()
                sends.append(passed)
        for t in range(n):
            copy(t, 0, sibling, me).wait_recv()
            for j, chip in enumerate(chips):
                copy(t, 4 + j, (*chip, 1 - c), me).wait_recv()
        for cp in sends:
            cp.wait_send()
        for cp in locals_:
            cp.wait()

    return _comm_call(body, name, arrs, [jax.ShapeDtypeStruct((N_DEV,) + a.shape, a.dtype) for a in arrs], N_PEERS)


N_CHIPS = N_DEV // 2
CHIPS = [(0, 0), (0, 1), (1, 0), (1, 1)]


def _sibling_exchange(arrs, name):
    def body(n, *refs):
        in_refs, out_refs = refs[:n], refs[n:2 * n]
        send_sems, recv_sems, _ = refs[2 * n:]
        x, y, c = lax.axis_index("x"), lax.axis_index("y"), lax.axis_index("c")
        sibling = (x, y, 1 - c)

        def copy(t, j):
            return pltpu.make_async_remote_copy(
                src_ref=in_refs[t].at[_dev_index((*CHIPS[j], 1 - c))], dst_ref=out_refs[t].at[j],
                send_sem=send_sems.at[t, j], recv_sem=recv_sems.at[t, j],
                device_id=sibling, device_id_type=MESH_ID)

        copies = [copy(t, j) for t in range(n) for j in range(N_CHIPS)]
        for cp in copies:
            cp.start()
        for cp in copies:
            cp.wait_recv()
        for cp in copies:
            cp.wait_send()

    return _comm_call(body, name, arrs, [jax.ShapeDtypeStruct((N_CHIPS,) + a.shape[1:], a.dtype) for a in arrs], N_CHIPS)


def _chip_exchange(arrs, name):
    def body(n, *refs):
        in_refs, out_refs = refs[:n], refs[n:2 * n]
        send_sems, recv_sems, local_sems = refs[2 * n:]
        x, y, c = lax.axis_index("x"), lax.axis_index("y"), lax.axis_index("c")
        mine = 2 * x + y
        peers = [(1 - x, y), (x, 1 - y), (1 - x, 1 - y)]

        def copy(t, k, src_chip, dst_chip, to):
            return pltpu.make_async_remote_copy(
                src_ref=in_refs[t].at[src_chip], dst_ref=out_refs[t].at[dst_chip],
                send_sem=send_sems.at[t, k], recv_sem=recv_sems.at[t, k],
                device_id=(*to, c), device_id_type=MESH_ID)

        sends, locals_ = [], []
        for t in range(n):
            own = pltpu.make_async_copy(in_refs[t].at[mine], out_refs[t].at[mine], local_sems.at[t])
            own.start()
            locals_.append(own)
            for k, p in enumerate(peers):
                cp = copy(t, k, 2 * p[0] + p[1], mine, p)
                cp.start()
                sends.append(cp)
        for t in range(n):
            for k, p in enumerate(peers):
                copy(t, k, mine, 2 * p[0] + p[1], p).wait_recv()
        for cp in sends:
            cp.wait_send()
        for cp in locals_:
            cp.wait()

    return _comm_call(body, name, arrs, [jax.ShapeDtypeStruct(a.shape, a.dtype) for a in arrs], N_CHIPS - 1)


def _add_pairs(a, b, name):
    shape = a.shape
    C = shape[-1]
    R = int(np.prod(shape[:-1]))
    lanes = -(-C // LANES) * LANES
    tr = _pick(R, max(BF16_ROWS, 2 * ADAM_ELEMS // lanes), BF16_ROWS) if R % BF16_ROWS == 0 else R

    def body(a_ref, b_ref, o_ref):
        o_ref[...] = (a_ref[...].astype(F32) + b_ref[...].astype(F32)).astype(o_ref.dtype)

    spec = pl.BlockSpec((tr, C), lambda i: (i, 0))
    return pl.pallas_call(
        body, name=name, grid=(R // tr,), in_specs=[spec, spec], out_specs=spec,
        out_shape=jax.ShapeDtypeStruct((R, C), a.dtype),
        compiler_params=_params(("parallel",), 3 * tr * lanes * 4),
    )(a.reshape(R, C), b.reshape(R, C)).reshape(shape)


WEIGHTS = ['meta_tokens', 'emb_ln_g', 'emb_ln_b', 'w_in', 'q_norm_g', 'w_q_b', 'kv_norm_g', 'w_kv_b', 'w_o_attn',
           'ssd_conv_w', 'ssd_conv_b', 'dt_bias', 'a_log', 'd_skip', 'ssd_norm_g', 'w_o_ssd', 'w_out', 'ln1_g',
           'ln1_b', 'w_up', 'ffn_conv_w', 'ffn_conv_b', 'w_down', 'ln2_g', 'ln2_b']
BIG = {'w_in': 2, 'w_q_b': 2, 'w_kv_b': 2, 'w_o_attn': 1, 'w_o_ssd': 1, 'w_out': 1, 'w_up': 2, 'w_down': 1}
SMALL_SHARDED = {'meta_tokens': 1, 'ssd_conv_w': 2, 'ffn_conv_w': 2}
REPLICATED = [n for n in WEIGHTS if n not in BIG and n not in SMALL_SHARDED]
BIG_COLS = 1024
SMALL_COLS = LANES


def _flatten(arrs, cols, row_mult, lead=False):
    parts, offs, off = [], [], 0
    for a in arrs:
        a2 = a.reshape(N_DEV, -1) if lead else a.reshape(1, -1)
        n = a2.shape[1]
        pad = -n % cols
        parts.append(jnp.pad(a2, ((0, 0), (0, pad))))
        offs.append((off, n))
        off += n + pad
    rows = off // cols
    extra = (-rows % row_mult) * cols
    if extra:
        parts.append(jnp.zeros((parts[0].shape[0], extra), parts[0].dtype))
    flat = jnp.concatenate(parts, axis=1)
    flat = flat.reshape(flat.shape[0], -1, cols)
    return (flat if lead else flat[0]), offs


def _unflatten(flat, offs, shapes):
    f = flat.reshape(-1)
    return [f[o:o + n].reshape(s) for (o, n), s in zip(offs, shapes)]


def _to_pieces(g, axis):
    s = g.shape[axis] // N_DEV
    g = g.reshape(g.shape[:axis] + (N_DEV, s) + g.shape[axis + 1:])
    return jnp.moveaxis(g, axis, 0).reshape(N_DEV, -1)


def _from_pieces(p, shard_shape, axis):
    g = jnp.moveaxis(p.reshape((N_DEV,) + tuple(shard_shape)), 0, axis)
    sh = list(shard_shape)
    sh[axis] *= N_DEV
    return g.reshape(sh)


def _in_proj_pad(w):
    e = np.cumsum((0,) + IN_SIZES)
    ql, kvl, kpe, z, xbc, dt, ga, gs = [w[:, e[j]:e[j + 1]] for j in range(8)]
    zc = lambda n: jnp.zeros((w.shape[0], n), w.dtype)
    return jnp.concatenate([ql, kvl, z, xbc, ga, gs, kpe, zc(LANES - QK_ROPE), dt, zc(LANES - SSD_HEADS)], axis=1)


def _in_proj_unpad(d):
    seg = lambda o, n: d[:, o:o + n]
    return jnp.concatenate([seg(OQ, Q_LORA), seg(OKV, KV_LORA), seg(OKPE, QK_ROPE), seg(OZ, SSD_INNER),
                            seg(OXBC, SSD_CONV_DIM), seg(ODT, SSD_HEADS), seg(OGA, D_MODEL), seg(OGS, D_MODEL)], axis=1)


def _q_pad(w):
    w3 = w.reshape(Q_LORA, HEADS, QK_NOPE + QK_ROPE)
    return jnp.concatenate([w3, jnp.zeros((Q_LORA, HEADS, QHEAD - QK_NOPE - QK_ROPE), w.dtype)], axis=2).reshape(Q_LORA, HEADS * QHEAD)


def _q_unpad(d):
    return d.reshape(Q_LORA, HEADS, QHEAD)[:, :, :QK_NOPE + QK_ROPE].reshape(Q_LORA, HEADS * (QK_NOPE + QK_ROPE))


def _kv_perm(w):
    w3 = w.reshape(KV_LORA, HEADS, QK_NOPE + V_HEAD)
    return jnp.concatenate([w3[:, :, :QK_NOPE].reshape(KV_LORA, -1), w3[:, :, QK_NOPE:].reshape(KV_LORA, -1)], axis=1)


def _kv_unperm(d):
    kn = d[:, :HEADS * QK_NOPE].reshape(KV_LORA, HEADS, QK_NOPE)
    v = d[:, HEADS * QK_NOPE:].reshape(KV_LORA, HEADS, V_HEAD)
    return jnp.concatenate([kn, v], axis=2).reshape(KV_LORA, HEADS * (QK_NOPE + V_HEAD))


def _row_vec(v, width=None):
    v = v.reshape(1, -1).astype(F32)
    if width is not None and v.shape[1] < width:
        v = jnp.pad(v, ((0, 0), (0, width - v.shape[1])))
    return v


def _pad_rows8(w):
    return jnp.pad(w.astype(F32), ((0, SUBLANES - w.shape[0]), (0, 0)))


def _tables(Tp, npad):
    pos = jnp.maximum(jnp.arange(Tp, dtype=jnp.int32) - npad, 0).astype(F32)
    inv_freq = 1.0 / (ROPE_THETA ** (jnp.arange(0, QK_ROPE, 2, dtype=F32) / QK_ROPE))
    ang = pos[:, None] * inv_freq[None, :]
    ang = jnp.concatenate([ang, ang], axis=-1)
    zeros = jnp.zeros((Tp, LANES - QK_ROPE), F32)
    cos = jnp.concatenate([jnp.cos(ang), zeros], axis=1)
    sin = jnp.concatenate([jnp.sin(ang), zeros], axis=1)
    rot = np.zeros((LANES, LANES), np.float32)
    half = QK_ROPE // 2
    for i in range(half):
        rot[i + half, i] = -1.0
        rot[i, i + half] = 1.0
    expand = np.zeros((LANES, SSD_INNER), np.float32)
    for hd in range(SSD_HEADS):
        expand[hd, hd * SSD_HEAD_DIM:(hd + 1) * SSD_HEAD_DIM] = 1.0
    return cos, sin, jnp.asarray(rot), jnp.asarray(expand)


def _layer_rows(proj, tb):
    rows_a = [_row(proj, Q_LORA, OQ // Q_LORA), _row(proj, KV_LORA, OKV // KV_LORA), _row(proj, LANES, OKPE // LANES),
              _row(proj, LANES, ODT // LANES), _row(tb["cos"], diff=False), _row(tb["sin"], diff=False)]
    return rows_a


def _layer_fwd(h, h_bf, P, tb, fns, npad, bg=None):
    both = [_out(D_MODEL, F32), _out(D_MODEL, BF16)]
    res_ln_twice = lambda *a: fns["res_ln"](*a) * 2
    proj = _mm(h_bf, P["w_in"], F32, "in_proj")
    rows_a = _layer_rows(proj, tb)
    consts_a = [_row(tb["rot"], diff=False), _row(P["q_norm_g"]), _row(P["kv_norm_g"]), _row(P["dt_bias"])]
    qn, kvn, kr8, dt = _rw_fwd(fns["in_post"], rows_a, consts_a,
                               [_out(Q_LORA, BF16), _out(KV_LORA, BF16), _out(HEADS * LANES, BF16), _out(LANES, F32)],
                               "in_post")
    q = _mm(qn, P["w_q"], F32, "q_proj")
    rows_q = [_row(q, QHEAD, 0, grp=True), _row(tb["cos"], diff=False), _row(tb["sin"], diff=False)]
    qr = _rw_fwd(fns["q_post"], rows_q, [_row(tb["rot"], diff=False)], [_out(HEADS * QHEAD, BF16, QHEAD, grp=True)],
                 "q_post", ng=HEADS)[0]
    kv = _mm(kvn, P["w_kv"], BF16, "kv_proj")
    o, lse, *carried = _flash_fwd(qr, kv, kr8, npad, "attn_fwd_gather" if bg else "attn_fwd", bg=bg)
    ya = _mm(o, P["w_o_attn"], F32, "attn_out")
    xbc = _conv_fwd(proj, OXBC, SSD_CONV_DIM, P["ssd_conv_w"], P["ssd_conv_b"], SSD_CONV, True, npad, "ssd_conv")
    y, states = _ssd_fwd(xbc, dt, P["a_log"], tb["expand"], "ssd_fwd")
    rows_b = [_row(y, GW, 0, grp=True), _row(xbc, GW, 0, grp=True), _row(proj, GW, OZ // GW, grp=True)]
    consts_b = [_row(P["d_skip"], GW, 0, grp=True), _row(P["ssd_norm_g"], GW, 0, grp=True)]
    yn = _rw_fwd(fns["gated"], rows_b, consts_b, [_out(SSD_INNER, BF16, GW, grp=True)], "ssd_gate", ng=SSD_GROUPS)[0]
    ys = _mm(yn, P["w_o_ssd"], F32, "ssd_out")
    rows_c = [_row(proj, D_MODEL, OGA // D_MODEL), _row(proj, D_MODEL, OGS // D_MODEL), _row(ya), _row(ys)]
    mixed = _rw_fwd(fns["mix"], rows_c, [], [_out(D_MODEL, BF16)], "mix")[0]
    mo = _mm(mixed, P["w_out"], F32, "mix_out")
    consts_1 = [_row(P["ln1_g"]), _row(P["ln1_b"])]
    h1, h1_bf = _rw_fwd(res_ln_twice, [_row(h), _row(mo)], consts_1, both, "ln1")
    up = _mm(h1_bf, P["w_up"], F32, "ffn_up")
    u = _conv_fwd(up, 0, 2 * D_FF, P["ffn_conv_w"], P["ffn_conv_b"], FFN_CONV, False, npad, "ffn_conv")
    act = _rw_fwd(fns["glu"], [_row(u)], [], [_out(D_FF, BF16)], "ffn_glu")[0]
    fo = _mm(act, P["w_down"], F32, "ffn_down")
    consts_2 = [_row(P["ln2_g"]), _row(P["ln2_b"])]
    h2, h2_bf = _rw_fwd(res_ln_twice, [_row(h1), _row(fo)], consts_2, both, "ln2")
    res = dict(h=h, h_bf=h_bf, proj=proj, qn=qn, kvn=kvn, kr8=kr8, dt=dt, q=q, qr=qr, kv=kv, o=o, lse=lse, ya=ya,
               xbc=xbc, y=y, states=states, yn=yn, ys=ys, mixed=mixed, mo=mo, h1=h1, h1_bf=h1_bf, up=up, u=u, act=act,
               fo=fo)
    return h2, h2_bf, res, carried


def _layer_bwd(dh2, r, P, tb, fns, npad, bg=None):
    g = {}
    consts_2 = [_row(P["ln2_g"]), _row(P["ln2_b"])]
    (dh1_a, dfo), (g["ln2_g"], g["ln2_b"]) = _rw_bwd(fns["res_ln"], [_row(r["h1"]), _row(r["fo"])], consts_2,
                                                     [_row(dh2)], [F32, BF16], "ln2_bwd")
    g["w_down"] = _mm(r["act"], dfo, BF16, "dw_down", ta=True)
    dact = _mm(dfo, P["w_down"], F32, "d_act", tb=True)
    (du,), _ = _rw_bwd(fns["glu"], [_row(r["u"])], [], [_row(dact)], [F32], "glu_bwd")
    dup, g["ffn_conv_w"], g["ffn_conv_b"] = _conv_bwd(r["up"], 0, 2 * D_FF, P["ffn_conv_w"], P["ffn_conv_b"], du,
                                                      FFN_CONV, False, npad, "ffn_conv_bwd")
    g["w_up"] = _mm(r["h1_bf"], dup, BF16, "dw_up", ta=True)
    dh1 = _mm(dup, P["w_up"], F32, "d_h1", tb=True, add=dh1_a)
    consts_1 = [_row(P["ln1_g"]), _row(P["ln1_b"])]
    (dh_a, dmo), (g["ln1_g"], g["ln1_b"]) = _rw_bwd(fns["res_ln"], [_row(r["h"]), _row(r["mo"])], consts_1,
                                                    [_row(dh1)], [F32, BF16], "ln1_bwd")
    g["w_out"] = _mm(r["mixed"], dmo, BF16, "dw_out", ta=True)
    dmixed = _mm(dmo, P["w_out"], F32, "d_mixed", tb=True)
    proj = r["proj"]
    rows_c = [_row(proj, D_MODEL, OGA // D_MODEL), _row(proj, D_MODEL, OGS // D_MODEL), _row(r["ya"]), _row(r["ys"])]
    (dga, dgs, dya, dys), _ = _rw_bwd(fns["mix"], rows_c, [], [_row(dmixed)], [BF16] * 4, "mix_bwd")
    g["w_o_attn"] = _mm(r["o"], dya, BF16, "dw_o_attn", ta=True)
    do = _mm(dya, P["w_o_attn"], F32, "d_o", tb=True)
    g["w_o_ssd"] = _mm(r["yn"], dys, BF16, "dw_o_ssd", ta=True)
    dyn = _mm(dys, P["w_o_ssd"], F32, "d_yn", tb=True)
    rows_b = [_row(r["y"], GW, 0, grp=True), _row(r["xbc"], GW, 0, grp=True), _row(proj, GW, OZ // GW, grp=True)]
    consts_b = [_row(P["d_skip"], GW, 0, grp=True), _row(P["ssd_norm_g"], GW, 0, grp=True)]
    (dy, dxs_skip, dz), (g["d_skip"], g["ssd_norm_g"]) = _rw_bwd(
        fns["gated"], rows_b, consts_b, [_row(dyn, GW, 0, grp=True)], [F32, F32, BF16], "ssd_gate_bwd", ng=SSD_GROUPS)
    dxbc, ddt, g["a_log"] = _ssd_bwd(r["xbc"], r["dt"], P["a_log"], tb["expand"], dy, dxs_skip, r["states"], "ssd_bwd")
    dxbc_pre, g["ssd_conv_w"], g["ssd_conv_b"] = _conv_bwd(proj, OXBC, SSD_CONV_DIM, P["ssd_conv_w"], P["ssd_conv_b"],
                                                           dxbc, SSD_CONV, True, npad, "ssd_conv_bwd")
    delta = _attn_delta(do, r["o"], "attn_delta")
    dqr, dkn, dkr8, dv, *carried = _flash_bwd(r["qr"], r["kv"], r["kr8"], do, r["lse"], delta, npad,
                                              "attn_bwd_exchange" if bg else "attn_bwd", bg=bg)
    rows_q = [_row(r["q"], QHEAD, 0, grp=True), _row(tb["cos"], diff=False), _row(tb["sin"], diff=False)]
    (dq,), _ = _rw_bwd(fns["q_post"], rows_q, [_row(tb["rot"], diff=False)], [_row(dqr, QHEAD, 0, grp=True)], [BF16],
                       "q_post_bwd", ng=HEADS)
    g["w_q"] = _mm(r["qn"], dq, BF16, "dw_q", ta=True)
    dqn = _mm(dq, P["w_q"], F32, "d_qn", tb=True)
    dkv = jnp.concatenate([dkn, dv], axis=1)
    g["w_kv"] = _mm(r["kvn"], dkv, BF16, "dw_kv", ta=True)
    dkvn = _mm(dkv, P["w_kv"], F32, "d_kvn", tb=True)
    rows_a = _layer_rows(proj, tb)
    consts_a = [_row(tb["rot"], diff=False), _row(P["q_norm_g"]), _row(P["kv_norm_g"]), _row(P["dt_bias"])]
    (dql, dkvl, dkpe, ddtr), (g["q_norm_g"], g["kv_norm_g"], g["dt_bias"]) = _rw_bwd(
        fns["in_post"], rows_a, consts_a, [_row(dqn), _row(dkvn), _row(dkr8), _row(ddt)], [BF16] * 4, "in_post_bwd")
    dproj = jnp.concatenate([dql, dkvl, dz, dxbc_pre, dga, dgs, dkpe, ddtr], axis=1)
    g["w_in"] = _mm(r["h_bf"], dproj, BF16, "dw_in", ta=True)
    dh = _mm(dproj, P["w_in"], F32, "d_h", tb=True, add=dh_a)
    return dh, g, carried


def _full_weight(g, axis):
    if axis == 1:
        return g.reshape(-1, g.shape[-1])
    return jnp.concatenate([g[p] for p in range(N_DEV)], axis=1)


def _grad_pieces(d, axis):
    if axis == 1:
        return d.reshape(N_DEV, -1, d.shape[1])
    return jnp.transpose(d.reshape(d.shape[0], N_DEV, -1), (1, 0, 2))


def _layer_params(gathered, small, i):
    full = {n: _full_weight(gathered[n], BIG[n]) for n in BIG}
    P = {}
    P["w_in"] = _in_proj_pad(full["w_in"])
    P["w_q"] = _q_pad(full["w_q_b"])
    P["w_kv"] = _kv_perm(full["w_kv_b"])
    for n in ("w_o_attn", "w_o_ssd", "w_out", "w_up", "w_down"):
        P[n] = full[n]
    P["q_norm_g"] = _row_vec(small["q_norm_g"][i])
    P["kv_norm_g"] = _row_vec(small["kv_norm_g"][i])
    P["dt_bias"] = _row_vec(small["dt_bias"][i], LANES)
    P["a_log"] = _row_vec(small["a_log"][i], LANES)
    P["d_skip"] = _row_vec(jnp.repeat(small["d_skip"][i], SSD_HEAD_DIM))
    P["ssd_norm_g"] = _row_vec(small["ssd_norm_g"][i])
    P["ssd_conv_w"] = _pad_rows8(small["ssd_conv_w"][i])
    P["ssd_conv_b"] = _row_vec(small["ssd_conv_b"][i])
    P["ffn_conv_w"] = _pad_rows8(small["ffn_conv_w"][i])
    P["ffn_conv_b"] = _row_vec(small["ffn_conv_b"][i])
    for n in ("ln1_g", "ln1_b", "ln2_g", "ln2_b"):
        P[n] = _row_vec(small[n][i])
    return P


def _layer_grads_to_reference_layout(g):
    out = {}
    out["w_in"] = _in_proj_unpad(g["w_in"])
    out["w_q_b"] = _q_unpad(g["w_q"])
    out["w_kv_b"] = _kv_unperm(g["w_kv"])
    for n in ("w_o_attn", "w_o_ssd", "w_out", "w_up", "w_down"):
        out[n] = g[n]
    out["q_norm_g"] = g["q_norm_g"][0]
    out["kv_norm_g"] = g["kv_norm_g"][0]
    out["dt_bias"] = g["dt_bias"][0, :SSD_HEADS]
    out["a_log"] = g["a_log"][0, :SSD_HEADS]
    out["d_skip"] = g["d_skip"].reshape(SSD_HEADS, SSD_HEAD_DIM).sum(axis=1)
    out["ssd_norm_g"] = g["ssd_norm_g"][0]
    out["ssd_conv_w"] = g["ssd_conv_w"][:SSD_CONV]
    out["ssd_conv_b"] = g["ssd_conv_b"][0]
    out["ffn_conv_w"] = g["ffn_conv_w"][:FFN_CONV]
    out["ffn_conv_b"] = g["ffn_conv_b"][0]
    for n in ("ln1_g", "ln1_b", "ln2_g", "ln2_b"):
        out[n] = g[n][0]
    return out


def kernel(x, meta_tokens, emb_ln_g, emb_ln_b, w_in, q_norm_g, w_q_b, kv_norm_g, w_kv_b, w_o_attn, ssd_conv_w, ssd_conv_b, dt_bias, a_log, d_skip, ssd_norm_g, w_o_ssd, w_out, ln1_g, ln1_b, w_up, ffn_conv_w, ffn_conv_b, w_down, ln2_g, ln2_b, loss_target, m_meta_tokens, m_emb_ln_g, m_emb_ln_b, m_w_in, m_q_norm_g, m_w_q_b, m_kv_norm_g, m_w_kv_b, m_w_o_attn, m_ssd_conv_w, m_ssd_conv_b, m_dt_bias, m_a_log, m_d_skip, m_ssd_norm_g, m_w_o_ssd, m_w_out, m_ln1_g, m_ln1_b, m_w_up, m_ffn_conv_w, m_ffn_conv_b, m_w_down, m_ln2_g, m_ln2_b, v_meta_tokens, v_emb_ln_g, v_emb_ln_b, v_w_in, v_q_norm_g, v_w_q_b, v_kv_norm_g, v_w_kv_b, v_w_o_attn, v_ssd_conv_w, v_ssd_conv_b, v_dt_bias, v_a_log, v_d_skip, v_ssd_norm_g, v_w_o_ssd, v_w_out, v_ln1_g, v_ln1_b, v_w_up, v_ffn_conv_w, v_ffn_conv_b, v_w_down, v_ln2_g, v_ln2_b):
    given = dict(locals())
    w = {n: given[n] for n in WEIGHTS}
    m = {n: given["m_" + n] for n in WEIGHTS}
    v = {n: given["v_" + n] for n in WEIGHTS}
    seq = x.shape[1]
    assert x.shape[0] == 1 and seq % LANES == 0
    npad = LANES - N_META
    Tp = npad + N_META + seq
    depth = w_in.shape[0]

    big_names, small_names = list(BIG), list(SMALL_SHARDED)
    ws, offs_s = _flatten([w[n] for n in small_names], SMALL_COLS, SUBLANES)
    shards = [[w[n][i].astype(BF16) for n in big_names] for i in range(depth)]
    got = _allgather(shards[0] + [ws], "weight_allgather")
    gathered = dict(zip(big_names, got[:-1]))
    gsm = got[-1]
    small = {n: w[n] for n in REPLICATED}
    for n, (o, sz) in zip(small_names, offs_s):
        small[n] = _from_pieces(gsm.reshape(N_DEV, -1)[:, o:o + sz], w[n].shape, SMALL_SHARDED[n])

    fns = _make_stage_fns(npad)
    cos, sin, rot, expand = _tables(Tp, npad)
    tb = dict(cos=cos, sin=sin, rot=rot, expand=expand)
    top = jnp.pad(small["meta_tokens"], ((npad, 0), (0, 0)))
    hcat = jnp.concatenate([top, x[0]], axis=0)
    consts_e = [_row(_row_vec(w["emb_ln_g"])), _row(_row_vec(w["emb_ln_b"]))]
    h, h_bf = _rw_fwd(lambda *a: fns["ln"](*a) * 2, [_row(hcat)], consts_e, [_out(D_MODEL, F32), _out(D_MODEL, BF16)],
                      "emb_ln")
    layers, saved = [], []
    for i in range(depth):
        layers.append(_layer_params(gathered, small, i))
        bg = _Background("gather", shards[i + 1]) if i + 1 < depth else None
        h, h_bf, res, carried = _layer_fwd(h, h_bf, layers[i], tb, fns, npad, bg=bg)
        gathered = dict(zip(big_names, carried))
        saved.append(res)
    dh, lparts = _loss_head(h, loss_target[0], "loss_head")
    loss = lax.psum(jnp.sum(lparts[:, 0, 0]), ("x", "y", "c"))

    core = lax.axis_index("c")

    def chip_partials(pieces, tag):
        from_sibling = _sibling_exchange(pieces, "grad_exchange_cores_" + tag)
        sums = []
        for k, (p, r) in enumerate(zip(pieces, from_sibling)):
            own = lax.dynamic_index_in_dim(p.reshape((N_CHIPS, 2) + p.shape[1:]), core, axis=1, keepdims=False)
            sums.append(_add_pairs(own, r, "grad_chip_sum_%s_%d" % (tag, k)))
        return sums

    lg, recv_big, pending = [None] * depth, [None] * depth, None
    for i in reversed(range(depth)):
        bg = _Background("chips", pending) if pending is not None else None
        dh, gi, carried = _layer_bwd(dh, saved[i], layers[i], tb, fns, npad, bg=bg)
        if bg:
            recv_big[i + 1] = carried
        lg[i] = _layer_grads_to_reference_layout(gi)
        pending = None
        if i > 0:
            pending = chip_partials([_grad_pieces(lg[i][n], BIG[n]).astype(BF16) for n in big_names], "l%d" % i)
    (dhcat,), (d_emb_g, d_emb_b) = _rw_bwd(fns["ln"], [_row(hcat)], consts_e, [_row(dh)], [F32], "emb_ln_bwd")
    grad_x = dhcat[LANES:][None]
    local = {n: jnp.stack([lg[i][n] for i in range(depth)]) for n in lg[0] if n not in BIG}
    local["meta_tokens"] = dhcat[npad:LANES]
    local["emb_ln_g"] = d_emb_g[0]
    local["emb_ln_b"] = d_emb_b[0]

    sm_names = small_names + REPLICATED
    sm_pieces = [_to_pieces(local[n], SMALL_SHARDED[n]) for n in small_names]
    sm_pieces += [jnp.broadcast_to(local[n].reshape(1, -1), (N_DEV, local[n].size)) for n in REPLICATED]
    ps, _ = _flatten(sm_pieces, SMALL_COLS, BF16_ROWS, lead=True)
    pieces = [_grad_pieces(lg[0][n], BIG[n]).astype(BF16) for n in big_names] + [ps]
    recv = _chip_exchange(chip_partials(pieces, "l0"), "grad_exchange_chips")
    recv_big[0] = recv[:-1]
    outs = {}
    kinds = ("grad", "delta", "new_m", "new_v")
    for k, n in enumerate(big_names):
        parts = jnp.stack([recv_big[i][k] for i in range(depth)], axis=1)
        for kind, a in zip(kinds, _adamw(parts, w[n], m[n], v[n], "adamw_" + n)):
            outs[kind + "_" + n] = a
    wf, offs = _flatten([w[n] for n in sm_names], SMALL_COLS, BF16_ROWS)
    mf, _ = _flatten([m[n] for n in sm_names], SMALL_COLS, BF16_ROWS)
    vf, _ = _flatten([v[n] for n in sm_names], SMALL_COLS, BF16_ROWS)
    shapes = [w[n].shape for n in sm_names]
    for kind, flat in zip(kinds, _adamw(recv[-1], wf, mf, vf, "adamw_small")):
        for n, a in zip(sm_names, _unflatten(flat, offs, shapes)):
            outs[kind + "_" + n] = a
    result = [loss, grad_x]
    for kind in ("grad", "delta", "new_m", "new_v"):
        result += [outs[kind + "_" + n] for n in WEIGHTS]
    return tuple(result)
```

```python
import functools

import jax
import jax.numpy as jnp
import numpy as np
from jax import lax
from jax.experimental import pallas as pl
from jax.experimental.pallas import tpu as pltpu

F32 = jnp.float32
BF16 = jnp.bfloat16
HIGHEST = lax.Precision.HIGHEST
SSD_PREC = lax.Precision.HIGH

D_MODEL = 1024
DEPTH = 2
N_META = 16
HEADS = 8
Q_LORA = 768
KV_LORA = 256
QK_NOPE = 128
QK_ROPE = 64
V_HEAD = 128
ROPE_THETA = 10000.0
SSD_INNER = 2048
SSD_HEAD_DIM = 64
SSD_HEADS = 32
SSD_GROUPS = 4
SSD_STATE = 128
SSD_CONV = 4
SSD_CONV_DIM = SSD_INNER + 2 * SSD_GROUPS * SSD_STATE
CHUNK = 128
D_FF = 2816
FFN_CONV = 3
LN_EPS = 1e-5
RMS_EPS = 1e-6
ALPHA = (2 * DEPTH) ** 0.25
IN_SIZES = (Q_LORA, KV_LORA, QK_ROPE, SSD_INNER, SSD_CONV_DIM, SSD_HEADS, D_MODEL, D_MODEL)
ATT_SCALE = (QK_NOPE + QK_ROPE) ** -0.5
NEG_INF = -1e30
ADAM_LR, ADAM_B1, ADAM_B2, ADAM_EPS, ADAM_WD, ADAM_STEP = 0.001, 0.9, 0.999, 1e-08, 0.01, 10

LANES = 128
SUBLANES = 8
VMEM_BYTES = 64 * 1024 * 1024
N_DEV = 8

OQ, OKV, OZ, OXBC, OGA, OGS, OKPE, ODT = 0, 768, 1024, 3072, 6144, 7168, 8192, 8320
IN_PAD = 8448
QHEAD = 256

ROW_TILE = 640
MM_COL_TILE = 1408
MM_ROW_TILE = 1664
MM_VMEM_BUDGET = 46 * 1024 * 1024
MM_K_TILE = 2816
MM_TOKEN_K_TILE = 1664
ATT_TILE = 640
ATT_HEADS_PER_STEP = 4
BF16_ROWS = 16
HALO = BF16_ROWS
ROW_BUDGET = 7 * 1024 * 1024
ADAM_ELEMS = 160 * 1024


def _pick(n, target, q=LANES):
    assert n % q == 0, (n, q)
    units = n // q
    best = q
    for d in range(1, units + 1):
        if units % d == 0 and d * q <= target:
            best = d * q
    return best


def _pick_rows(n, row_bytes):
    return _pick(n, max(BF16_ROWS, ROW_BUDGET // row_bytes), BF16_ROWS)


def _params(sem, est_bytes):
    limit = int(min(VMEM_BYTES - (6 << 20), max(32 << 20, 2 * est_bytes + (8 << 20))))
    return pltpu.CompilerParams(dimension_semantics=sem, vmem_limit_bytes=limit)


def _nbytes(shape, dtype):
    return int(np.prod(shape)) * jnp.dtype(dtype).itemsize


def _mm(a, b, out_dtype, name, ta=False, tb=False, add=None):
    assert not (ta and tb)
    if ta:
        K, M = a.shape
        tm = _pick(M, MM_COL_TILE)
        tk = _pick(K, MM_TOKEN_K_TILE)
    else:
        M, K = a.shape
        tk = _pick(K, MM_K_TILE)
    N, K2 = (b.shape if tb else b.shape[::-1])
    assert K == K2
    tn = _pick(N, MM_COL_TILE)
    nk = K // tk

    def vmem_estimate(tm):
        e = 2 * (tm * tk * a.dtype.itemsize + tk * tn * b.dtype.itemsize + tm * tn * jnp.dtype(out_dtype).itemsize)
        e += tm * tn * 4 + tm * tk * 2
        return e + (tm * tn * 4 if nk > 1 else 0) + (2 * tm * tn * 4 if add is not None else 0)

    if not ta:
        tm = _pick(M, MM_ROW_TILE, BF16_ROWS)
        while vmem_estimate(tm) > MM_VMEM_BUDGET and tm > BF16_ROWS:
            tm = _pick(M, tm - BF16_ROWS, BF16_ROWS)
    dn = (((0,), (0,)), ((), ())) if ta else ((((1,), (1,)), ((), ())) if tb else (((1,), (0,)), ((), ())))

    def body(*refs):
        a_ref, b_ref = refs[:2]
        add_ref = refs[2] if add is not None else None
        o_ref = refs[2 + (add is not None)]
        d = lax.dot_general(a_ref[...].astype(BF16), b_ref[...].astype(BF16), dn, preferred_element_type=F32)

        def finish(r):
            if add is not None:
                r = r + add_ref[...].astype(F32)
            o_ref[...] = r.astype(out_dtype)

        if nk == 1:
            finish(d)
            return
        acc = refs[-1]
        k = pl.program_id(2)

        @pl.when(k == 0)
        def _():
            acc[...] = d

        @pl.when((k > 0) & (k < nk - 1))
        def _():
            acc[...] += d

        @pl.when(k == nk - 1)
        def _():
            finish(acc[...] + d)

    if ta:
        a_spec = pl.BlockSpec((tk, tm), lambda i, j, k: (k, i))
    else:
        a_spec = pl.BlockSpec((tm, tk), lambda i, j, k: (i, k))
    b_spec = pl.BlockSpec((tn, tk), lambda i, j, k: (j, k)) if tb else pl.BlockSpec((tk, tn), lambda i, j, k: (k, j))
    in_specs = [a_spec, b_spec]
    args = [a, b]
    est = vmem_estimate(tm)
    if add is not None:
        in_specs.append(pl.BlockSpec((tm, tn), lambda i, j, k: (i, j)))
        args.append(add)
    return pl.pallas_call(
        body, name=name, grid=(M // tm, N // tn, nk), in_specs=in_specs,
        out_specs=pl.BlockSpec((tm, tn), lambda i, j, k: (i, j)),
        out_shape=jax.ShapeDtypeStruct((M, N), out_dtype),
        scratch_shapes=[pltpu.VMEM((tm, tn), F32)] if nk > 1 else [],
        compiler_params=_params(("parallel", "parallel", "arbitrary"), est),
    )(*args)


def _row(arr, bw=None, cb=0, grp=False, diff=True):
    return dict(arr=arr, bw=arr.shape[1] if bw is None else bw, cb=cb, grp=grp, diff=diff)


def _out(width, dtype, bw=None, grp=False):
    return dict(width=width, dtype=dtype, bw=width if bw is None else bw, grp=grp)


def _spec_rows(d, tm):
    return pl.BlockSpec((tm, d["bw"]), lambda g, i, cb=d["cb"], gr=d["grp"]: (i, cb + (g if gr else 0)))


def _spec_const(d):
    return pl.BlockSpec((d["arr"].shape[0], d["bw"]), lambda g, i, cb=d["cb"], gr=d["grp"]: (0, cb + (g if gr else 0)))


def _rw_fwd(fn, rows, consts, outs, name, ng=1):
    Tp = rows[0]["arr"].shape[0]
    tm = _pick_rows(Tp, 4 * (sum(d["bw"] for d in rows) + 2 * sum(o["bw"] for o in outs)))
    nr, ncst = len(rows), len(consts)

    def body(*refs):
        i = pl.program_id(1)
        rowidx = i * tm + lax.broadcasted_iota(jnp.int32, (tm, 1), 0)
        rv = [r[...].astype(F32) for r in refs[:nr]]
        cv = [c[...] for c in refs[nr:nr + ncst]]
        vals = fn(rowidx, *rv, *cv)
        for o, v in zip(refs[nr + ncst:], vals):
            o[...] = v.astype(o.dtype)

    est = sum(tm * d["bw"] * 4 for d in rows) + sum(tm * o["bw"] * 4 for o in outs)
    return pl.pallas_call(
        body, name=name, grid=(ng, Tp // tm),
        in_specs=[_spec_rows(d, tm) for d in rows] + [_spec_const(d) for d in consts],
        out_specs=[pl.BlockSpec((tm, o["bw"]), lambda g, i, gr=o["grp"]: (i, g if gr else 0)) for o in outs],
        out_shape=[jax.ShapeDtypeStruct((Tp, o["width"]), o["dtype"]) for o in outs],
        compiler_params=_params(("parallel", "parallel"), 3 * est),
    )(*[d["arr"] for d in rows], *[d["arr"] for d in consts])


def _rw_bwd(fn, rows, consts, cots, drow_dtypes, name, ng=1):
    Tp = rows[0]["arr"].shape[0]
    tm = _pick_rows(Tp, 4 * (3 * sum(d["bw"] for d in rows) + 2 * sum(d["bw"] for d in cots)))
    nr, ncst, nct = len(rows), len(consts), len(cots)
    drows = [k for k, d in enumerate(rows) if d["diff"]]
    dcsts = [k for k, d in enumerate(consts) if d["diff"]]
    for k in drows:
        assert rows[k]["grp"] or ng == 1

    def body(*refs):
        g = pl.program_id(0)
        i = pl.program_id(1)
        rowidx = i * tm + lax.broadcasted_iota(jnp.int32, (tm, 1), 0)
        rv = [r[...].astype(F32) for r in refs[:nr]]
        cv = [c[...] for c in refs[nr:nr + ncst]]
        ct = tuple(r[...].astype(F32) for r in refs[nr + ncst:nr + ncst + nct])
        orefs = refs[nr + ncst + nct:]

        def f(*dargs):
            rr, cc = list(rv), list(cv)
            for k, v in zip(drows, dargs[:len(drows)]):
                rr[k] = v
            for k, v in zip(dcsts, dargs[len(drows):]):
                cc[k] = v
            return tuple(fn(rowidx, *rr, *cc))

        _, vjp = jax.vjp(f, *[rv[k] for k in drows], *[cv[k] for k in dcsts])
        grads = vjp(ct)
        for o, v in zip(orefs[:len(drows)], grads[:len(drows)]):
            o[...] = v.astype(o.dtype)
        for k, o, v in zip(dcsts, orefs[len(drows):], grads[len(drows):]):
            first = (i == 0) if consts[k]["grp"] else ((i == 0) & (g == 0))

            @pl.when(first)
            def _(o=o, v=v):
                o[...] = v

            @pl.when(jnp.logical_not(first))
            def _(o=o, v=v):
                o[...] += v

    out_specs, out_shape = [], []
    for k, dt in zip(drows, drow_dtypes):
        d = rows[k]
        out_specs.append(pl.BlockSpec((tm, d["bw"]), lambda g, i, gr=d["grp"]: (i, g if gr else 0)))
        out_shape.append(jax.ShapeDtypeStruct((Tp, d["bw"] * (ng if d["grp"] else 1)), dt))
    for k in dcsts:
        d = consts[k]
        r = d["arr"].shape[0]
        out_specs.append(pl.BlockSpec((r, d["bw"]), lambda g, i, gr=d["grp"]: (0, g if gr else 0)))
        out_shape.append(jax.ShapeDtypeStruct((r, d["bw"] * (ng if d["grp"] else 1)), F32))
    est = sum(tm * d["bw"] * 4 for d in rows) * 2 + sum(tm * d["bw"] * 4 for d in cots)
    res = pl.pallas_call(
        body, name=name, grid=(ng, Tp // tm),
        in_specs=[_spec_rows(d, tm) for d in rows] + [_spec_const(d) for d in consts] + [_spec_rows(d, tm) for d in cots],
        out_specs=out_specs, out_shape=out_shape,
        compiler_params=_params(("arbitrary", "arbitrary"), 3 * est),
    )(*[d["arr"] for d in rows], *[d["arr"] for d in consts], *[d["arr"] for d in cots])
    return list(res[:len(drows)]), list(res[len(drows):])


def _sigmoid(x):
    return 0.5 * jnp.tanh(0.5 * x) + 0.5


def _silu(x):
    return x * _sigmoid(x)


def _softplus(x):
    return jnp.maximum(x, 0.0) + jnp.log(1.0 + jnp.exp(-jnp.abs(x)))


def _layer_norm(x, g, b):
    mu = jnp.mean(x, axis=-1, keepdims=True)
    xc = x - mu
    var = jnp.mean(xc * xc, axis=-1, keepdims=True)
    return xc * lax.rsqrt(var + LN_EPS) * g + b


def _rms_norm(x, g):
    return x * lax.rsqrt(jnp.mean(x * x, axis=-1, keepdims=True) + RMS_EPS) * g


def _rope(r, cos, sin, rot):
    return r * cos + jnp.dot(r, rot, precision=HIGHEST, preferred_element_type=F32) * sin


def _make_stage_fns(npad):
    def fn_ln_masked(rowidx, x, g, b):
        return (jnp.where(rowidx >= npad, _layer_norm(x, g, b), 0.0),)

    def fn_in_post(rowidx, ql, kvl, kpe, dtr, cos, sin, rot, qg, kvg, dtb):
        qn = _rms_norm(ql, qg)
        kvn = _rms_norm(kvl, kvg)
        kr = _rope(kpe, cos, sin, rot)
        lane = lax.broadcasted_iota(jnp.int32, (1, LANES), 1)
        dt = jnp.where((rowidx >= npad) & (lane < SSD_HEADS), _softplus(dtr + dtb), 0.0)
        return qn, kvn, jnp.concatenate([kr] * HEADS, axis=1), dt

    def fn_q_post(rowidx, q, cos, sin, rot):
        rr = _rope(q[:, QK_NOPE:], cos, sin, rot)
        return (jnp.concatenate([q[:, :QK_NOPE], rr], axis=1) * ATT_SCALE,)

    def fn_gated_norm(rowidx, y, xs, z, dskip, g):
        v = (y + xs * dskip) * _silu(z)
        return (v * lax.rsqrt(jnp.mean(v * v, axis=-1, keepdims=True) + RMS_EPS) * g,)

    def fn_mix(rowidx, ga, gs, ya, ys):
        return (_sigmoid(ga) * ya + _sigmoid(gs) * ys,)

    def fn_res_ln(rowidx, h, r, g, b):
        return (jnp.where(rowidx >= npad, _layer_norm(ALPHA * h + r, g, b), 0.0),)

    def fn_glu(rowidx, u):
        return (_silu(u[:, :D_FF]) * u[:, D_FF:],)

    return dict(ln=fn_ln_masked, in_post=fn_in_post, q_post=fn_q_post, gated=fn_gated_norm, mix=fn_mix,
                res_ln=fn_res_ln, glu=fn_glu)


def _conv_tiles(Tp, C):
    return _pick(Tp, ROW_TILE), _pick(C, MM_COL_TILE)


def _conv_fwd(x, xoff, C, w8, b, K, act, npad, name, out_dtype=F32):
    Tp = x.shape[0]
    tm, tc = _conv_tiles(Tp, C)
    assert xoff % tc == 0
    cb0 = xoff // tc
    rb = tm // HALO

    def body(prev_ref, main_ref, w_ref, b_ref, o_ref):
        i = pl.program_id(1)
        main = main_ref[...].astype(F32)
        prev = jnp.where(i > 0, prev_ref[...].astype(F32), 0.0)
        ext = jnp.concatenate([prev, main], axis=0)
        acc = b_ref[...] + w_ref[K - 1:K, :] * main
        for k in range(K - 1):
            s = K - 1 - k
            acc = acc + w_ref[k:k + 1, :] * pltpu.roll(ext, s, 0)[HALO:, :]
        if act:
            rowidx = i * tm + lax.broadcasted_iota(jnp.int32, (tm, 1), 0)
            acc = jnp.where(rowidx >= npad, _silu(acc), 0.0)
        o_ref[...] = acc.astype(o_ref.dtype)

    return pl.pallas_call(
        body, name=name, grid=(C // tc, Tp // tm),
        in_specs=[pl.BlockSpec((HALO, tc), lambda g, i: (jnp.maximum(i * rb - 1, 0), cb0 + g)),
                  pl.BlockSpec((tm, tc), lambda g, i: (i, cb0 + g)),
                  pl.BlockSpec((SUBLANES, tc), lambda g, i: (0, g)),
                  pl.BlockSpec((1, tc), lambda g, i: (0, g))],
        out_specs=pl.BlockSpec((tm, tc), lambda g, i: (i, g)),
        out_shape=jax.ShapeDtypeStruct((Tp, C), out_dtype),
        compiler_params=_params(("parallel", "parallel"), 8 * tm * tc * 4),
    )(x, x, w8, b)


def _conv_bwd(x, xoff, C, w8, b, dy, K, act, npad, name):
    Tp = x.shape[0]
    tm, tc = _conv_tiles(Tp, C)
    cb0 = xoff // tc
    rb = tm // HALO
    ni = Tp // tm
    last_rb = Tp // HALO - 1
    n = tm + 2 * HALO

    def body(xp_ref, xm_ref, xn_ref, dym_ref, dyn_ref, w_ref, b_ref, dx_ref, dw_ref, db_ref):
        i = pl.program_id(1)
        prev = jnp.where(i > 0, xp_ref[...].astype(F32), 0.0)
        ext = jnp.concatenate([prev, xm_ref[...].astype(F32), xn_ref[...].astype(F32)], axis=0)
        dyn = jnp.where(i < ni - 1, dyn_ref[...].astype(F32), 0.0)
        dpre = jnp.concatenate([jnp.zeros((HALO, tc), F32), dym_ref[...].astype(F32), dyn], axis=0)
        shifted = [ext if k == K - 1 else pltpu.roll(ext, K - 1 - k, 0) for k in range(K)]
        if act:
            pre = b_ref[...] + sum(w_ref[k:k + 1, :] * shifted[k] for k in range(K))
            rowidx = i * tm - HALO + lax.broadcasted_iota(jnp.int32, (n, 1), 0)
            sg = _sigmoid(pre)
            dpre = jnp.where(rowidx >= npad, dpre * sg * (1.0 + pre * (1.0 - sg)), 0.0)
        dx = w_ref[K - 1:K, :] * dpre
        for k in range(K - 1):
            dx = dx + w_ref[k:k + 1, :] * pltpu.roll(dpre, n - (K - 1 - k), 0)
        dx_ref[...] = dx[HALO:HALO + tm, :].astype(dx_ref.dtype)

        @pl.when(i == 0)
        def _():
            dw_ref[...] = jnp.zeros_like(dw_ref)
            db_ref[...] = jnp.zeros_like(db_ref)

        dmain = dpre[HALO:HALO + tm, :]
        for k in range(K):
            dw_ref[k:k + 1, :] += jnp.sum(dmain * shifted[k][HALO:HALO + tm, :], axis=0, keepdims=True)
        db_ref[...] += jnp.sum(dmain, axis=0, keepdims=True)

    return pl.pallas_call(
        body, name=name, grid=(C // tc, ni),
        in_specs=[pl.BlockSpec((HALO, tc), lambda g, i: (jnp.maximum(i * rb - 1, 0), cb0 + g)),
                  pl.BlockSpec((tm, tc), lambda g, i: (i, cb0 + g)),
                  pl.BlockSpec((HALO, tc), lambda g, i: (jnp.minimum((i + 1) * rb, last_rb), cb0 + g)),
                  pl.BlockSpec((tm, tc), lambda g, i: (i, g)),
                  pl.BlockSpec((HALO, tc), lambda g, i: (jnp.minimum((i + 1) * rb, last_rb), g)),
                  pl.BlockSpec((SUBLANES, tc), lambda g, i: (0, g)),
                  pl.BlockSpec((1, tc), lambda g, i: (0, g))],
        out_specs=[pl.BlockSpec((tm, tc), lambda g, i: (i, g)),
                   pl.BlockSpec((SUBLANES, tc), lambda g, i: (0, g)),
                   pl.BlockSpec((1, tc), lambda g, i: (0, g))],
        out_shape=[jax.ShapeDtypeStruct((Tp, C), BF16), jax.ShapeDtypeStruct((SUBLANES, C), F32),
                   jax.ShapeDtypeStruct((1, C), F32)],
        compiler_params=_params(("parallel", "arbitrary"), 14 * tm * tc * 4),
    )(x, x, x, dy, dy, w8, b)


def _split_refs(refs, n_in, n_out, n_scratch, nbg):
    cuts = np.cumsum([0, n_in, nbg, n_out, nbg, n_scratch])
    return tuple(refs[a:b] for a, b in zip(cuts[:-1], cuts[1:])) + (refs[cuts[-1]:],)


def _flash_fwd(q, kv, kr8, npad, name, bg=None):
    Tp = q.shape[0]
    t = _pick(Tp, ATT_TILE)
    hp = ATT_HEADS_PER_STEP
    nb = Tp // t
    ng = HEADS // hp
    nbg = bg.n if bg else 0
    nt = (((1,), (1,)), ((), ()))
    tn = (((0,), (0,)), ((), ()))

    def body(*refs):
        (q_ref, kn_ref, kr_ref, v_ref), bg_in, (o_ref, lse_ref), bg_out, (m_sc, l_sc, acc_sc), bg_sems = _split_refs(
            refs, 4, 2, 3, nbg)
        g = pl.program_id(0)
        qi = pl.program_id(1)
        ki = pl.program_id(2)
        if bg:
            @pl.when((g == 0) & (qi == 0) & (ki == 0))
            def _():
                bg.start(bg_in, bg_out, bg_sems)

        @pl.when(ki == 0)
        def _():
            m_sc[...] = jnp.full_like(m_sc, NEG_INF)
            l_sc[...] = jnp.zeros_like(l_sc)
            acc_sc[...] = jnp.zeros_like(acc_sc)

        def step(masked):
            kr = kr_ref[...]
            if masked:
                key = ki * t + lax.broadcasted_iota(jnp.int32, (t, t), 0)
                qry = qi * t + lax.broadcasted_iota(jnp.int32, (t, t), 1)
                visible = (key <= qry) & (key >= npad)
            for hh in range(hp):
                k = jnp.concatenate([kn_ref[:, hh * QK_NOPE:(hh + 1) * QK_NOPE], kr], axis=1)
                st = lax.dot_general(k, q_ref[:, hh * QHEAD:(hh + 1) * QHEAD], nt, preferred_element_type=F32)
                if masked:
                    st = jnp.where(visible, st, NEG_INF)
                vs = slice(hh * V_HEAD, (hh + 1) * V_HEAD)
                m_prev = m_sc[hh]
                m_new = jnp.maximum(m_prev, jnp.max(st, axis=0, keepdims=True))
                pt = jnp.exp(st - m_new)
                a = jnp.exp(m_prev - m_new)
                l_sc[hh] = a * l_sc[hh] + jnp.sum(pt, axis=0, keepdims=True)
                acc_sc[vs, :] = a * acc_sc[vs, :] + lax.dot_general(v_ref[:, vs], pt.astype(BF16), tn,
                                                                    preferred_element_type=F32)
                m_sc[hh] = m_new

        need_mask = (ki == qi) | (ki == 0)

        @pl.when((ki <= qi) & need_mask)
        def _():
            step(True)

        @pl.when((ki <= qi) & jnp.logical_not(need_mask))
        def _():
            step(False)

        @pl.when(ki == qi)
        def _():
            for hh in range(hp):
                vs = slice(hh * V_HEAD, (hh + 1) * V_HEAD)
                l = l_sc[hh]
                o_ref[:, vs] = (acc_sc[vs, :] / l).T.astype(o_ref.dtype)
                lse_ref[hh * SUBLANES:(hh + 1) * SUBLANES, :] = jnp.broadcast_to(m_sc[hh] + jnp.log(l), (SUBLANES, t))

        if bg:
            @pl.when((g == ng - 1) & (qi == nb - 1) & (ki == nb - 1))
            def _():
                bg.wait(bg_in, bg_out, bg_sems)

    kmin = lambda qi, ki: jnp.minimum(ki, qi)
    return pl.pallas_call(
        body, name=name, grid=(ng, nb, nb),
        in_specs=[pl.BlockSpec((t, hp * QHEAD), lambda g, qi, ki: (qi, g)),
                  pl.BlockSpec((t, hp * QK_NOPE), lambda g, qi, ki: (kmin(qi, ki), g)),
                  pl.BlockSpec((t, LANES), lambda g, qi, ki: (kmin(qi, ki), 0)),
                  pl.BlockSpec((t, hp * V_HEAD), lambda g, qi, ki: (kmin(qi, ki), ng + g))] + (bg.specs if bg else []),
        out_specs=[pl.BlockSpec((t, hp * V_HEAD), lambda g, qi, ki: (qi, g)),
                   pl.BlockSpec((hp * SUBLANES, t), lambda g, qi, ki: (g, qi))] + (bg.specs if bg else []),
        out_shape=[jax.ShapeDtypeStruct((Tp, HEADS * V_HEAD), F32), jax.ShapeDtypeStruct((HEADS * SUBLANES, Tp), F32)]
        + (bg.out_shape if bg else []),
        scratch_shapes=[pltpu.VMEM((hp, 1, t), F32), pltpu.VMEM((hp, 1, t), F32), pltpu.VMEM((hp * V_HEAD, t), F32)]
        + (bg.scratch if bg else []),
        compiler_params=_params(("arbitrary",) * 3 if bg else ("parallel", "parallel", "arbitrary"), 8 * hp * t * t * 4),
    )(q, kv, kr8, kv, *(bg.arrs if bg else []))


def _attn_delta(do, o, name):
    Tp = do.shape[0]
    tm = _pick(Tp, MM_TOKEN_K_TILE)

    def body(do_ref, o_ref, d_ref):
        prod = do_ref[...] * o_ref[...]
        ones = jnp.ones((SUBLANES, V_HEAD), F32)
        d_ref[...] = lax.dot_general(ones, prod, (((1,), (1,)), ((), ())), precision=HIGHEST,
                                     preferred_element_type=F32)

    return pl.pallas_call(
        body, name=name, grid=(HEADS, Tp // tm),
        in_specs=[pl.BlockSpec((tm, V_HEAD), lambda h, i: (i, h)), pl.BlockSpec((tm, V_HEAD), lambda h, i: (i, h))],
        out_specs=pl.BlockSpec((SUBLANES, tm), lambda h, i: (h, i)),
        out_shape=jax.ShapeDtypeStruct((HEADS * SUBLANES, Tp), F32),
        compiler_params=_params(("parallel", "parallel"), 4 * tm * V_HEAD * 4),
    )(do, o)


def _flash_bwd(q, kv, kr8, do, lse, delta, npad, name, bg=None):
    Tp = q.shape[0]
    t = _pick(Tp, ATT_TILE)
    nb = Tp // t
    nbg = bg.n if bg else 0
    nt = (((1,), (1,)), ((), ()))
    tn = (((0,), (0,)), ((), ()))

    def body(*refs):
        ((q_ref, kn_ref, kr_ref, v_ref, do_ref, lse_ref, dl_ref), bg_in, (dq_ref, dkn_ref, dkr_ref, dv_ref), bg_out,
         (dk_sc, dv_sc), bg_sems) = _split_refs(refs, 7, 4, 2, nbg)
        h = pl.program_id(0)
        ki = pl.program_id(1)
        qi = pl.program_id(2)
        if bg:
            @pl.when((h == 0) & (ki == 0) & (qi == 0))
            def _():
                bg.start(bg_in, bg_out, bg_sems)

        @pl.when(qi == 0)
        def _():
            dk_sc[...] = jnp.zeros_like(dk_sc)
            dv_sc[...] = jnp.zeros_like(dv_sc)

        def step(masked):
            qv = q_ref[...]
            k = jnp.concatenate([kn_ref[...], kr_ref[...]], axis=1)
            st = lax.dot_general(k, qv, nt, preferred_element_type=F32)
            if masked:
                key = ki * t + lax.broadcasted_iota(jnp.int32, (t, t), 0)
                qry = qi * t + lax.broadcasted_iota(jnp.int32, (t, t), 1)
                st = jnp.where((key <= qry) & (key >= npad), st, NEG_INF)
            pt = jnp.exp(st - lse_ref[0:1, :])
            dob = do_ref[...].astype(BF16)
            dv_sc[...] += jnp.dot(pt.astype(BF16), dob, preferred_element_type=F32)
            dpt = lax.dot_general(v_ref[...], dob, nt, preferred_element_type=F32)
            dst = (pt * (dpt - dl_ref[0:1, :])).astype(BF16)
            dk_sc[...] += jnp.dot(dst, qv, preferred_element_type=F32)
            dqc = lax.dot_general(dst, k, tn, preferred_element_type=F32)
            rows = pl.ds(pl.multiple_of(qi * t, t), t)

            @pl.when(ki == 0)
            def _():
                dq_ref[rows, :] = dqc

            @pl.when(ki > 0)
            def _():
                dq_ref[rows, :] += dqc

        need_mask = (ki == qi) | (ki == 0)

        @pl.when((qi >= ki) & need_mask)
        def _():
            step(True)

        @pl.when((qi >= ki) & jnp.logical_not(need_mask))
        def _():
            step(False)

        @pl.when(qi == nb - 1)
        def _():
            dkn_ref[...] = dk_sc[:, :QK_NOPE].astype(dkn_ref.dtype)
            dkr_ref[...] = dk_sc[:, QK_NOPE:].astype(dkr_ref.dtype)
            dv_ref[...] = dv_sc[...].astype(dv_ref.dtype)

        if bg:
            @pl.when((h == HEADS - 1) & (ki == nb - 1) & (qi == nb - 1))
            def _():
                bg.wait(bg_in, bg_out, bg_sems)

    qmap = lambda h, ki, qi: (jnp.maximum(qi, ki), h)
    kmap = lambda h, ki, qi: (ki, h)
    est = 2 * Tp * QHEAD * 4 + 8 * t * t * 4
    return pl.pallas_call(
        body, name=name, grid=(HEADS, nb, nb),
        in_specs=[pl.BlockSpec((t, QHEAD), qmap),
                  pl.BlockSpec((t, QK_NOPE), kmap),
                  pl.BlockSpec((t, LANES), kmap),
                  pl.BlockSpec((t, V_HEAD), lambda h, ki, qi: (ki, HEADS + h)),
                  pl.BlockSpec((t, V_HEAD), qmap),
                  pl.BlockSpec((SUBLANES, t), lambda h, ki, qi: (h, jnp.maximum(qi, ki))),
                  pl.BlockSpec((SUBLANES, t), lambda h, ki, qi: (h, jnp.maximum(qi, ki)))] + (bg.specs if bg else []),
        out_specs=[pl.BlockSpec((Tp, QHEAD), lambda h, ki, qi: (0, h)),
                   pl.BlockSpec((t, QK_NOPE), kmap),
                   pl.BlockSpec((t, LANES), kmap),
                   pl.BlockSpec((t, V_HEAD), kmap)] + (bg.specs if bg else []),
        out_shape=[jax.ShapeDtypeStruct((Tp, HEADS * QHEAD), F32),
                   jax.ShapeDtypeStruct((Tp, HEADS * QK_NOPE), BF16),
                   jax.ShapeDtypeStruct((Tp, HEADS * LANES), F32),
                   jax.ShapeDtypeStruct((Tp, HEADS * V_HEAD), BF16)] + (bg.out_shape if bg else []),
        scratch_shapes=[pltpu.VMEM((t, QHEAD), F32), pltpu.VMEM((t, V_HEAD), F32)] + (bg.scratch if bg else []),
        compiler_params=_params(("arbitrary",) * 3 if bg else ("parallel", "arbitrary", "arbitrary"), est),
    )(q, kv, kr8, kv, do, lse, delta, *(bg.arrs if bg else []))


GW = SSD_INNER // SSD_GROUPS
PAIRS_PER_GROUP = GW // LANES
XB = SSD_INNER // GW
NT_DIMS = (((1,), (1,)), ((), ()))
TN_DIMS = (((0,), (0,)), ((), ()))


def _ssd_common(xs_ref, dt_ref, alog_ref, e_ref):
    a_neg = -jnp.exp(alog_ref[...])
    dt = dt_ref[...]
    li = lax.broadcasted_iota(jnp.int32, (CHUNK, CHUNK), 0)
    si = lax.broadcasted_iota(jnp.int32, (CHUNK, CHUNK), 1)
    tril = li >= si
    tri = tril.astype(F32)
    acs = jnp.dot(tri, dt * a_neg, precision=SSD_PREC, preferred_element_type=F32)
    e = e_ref[...]
    dte = jnp.dot(dt, e, precision=SSD_PREC, preferred_element_type=F32)
    acse = jnp.dot(acs, e, precision=SSD_PREC, preferred_element_type=F32)
    x = xs_ref[...] * dte
    alast = acse[CHUNK - 1:CHUNK, :]
    return dict(a_neg=a_neg, dt=dt, tril=tril, tri=tri, acs=acs, acs_t=acs.T, e=e, dte=dte, acse=acse, x=x,
                p_e=jnp.exp(acse), w_e=jnp.exp(alast - acse), dl_e=jnp.exp(alast), li=li, si=si)


def _decay(cm, head):
    col = cm["acs"][:, head:head + 1]
    row = cm["acs_t"][head:head + 1, :]
    return jnp.exp(jnp.where(cm["tril"], col - row, -jnp.inf))


def _ssd_fwd(xbc, dt, alog, e, name):
    Tp = xbc.shape[0]
    nc = Tp // CHUNK

    def body(xs_ref, b_ref, c_ref, dt_ref, alog_ref, e_ref, y_ref, st_ref, st_sc):
        @pl.when(pl.program_id(0) == 0)
        def _():
            st_sc[...] = jnp.zeros_like(st_sc)

        cm = _ssd_common(xs_ref, dt_ref, alog_ref, e_ref)
        st_ref[0] = st_sc[...]
        lane = lax.broadcasted_iota(jnp.int32, (CHUNK, LANES), 1)
        for g in range(SSD_GROUPS):
            gs = slice(g * GW, (g + 1) * GW)
            cg = c_ref[:, g * SSD_STATE:(g + 1) * SSD_STATE].astype(BF16)
            bg = b_ref[:, g * SSD_STATE:(g + 1) * SSD_STATE].astype(BF16)
            cb = lax.dot_general(cg, bg, NT_DIMS, preferred_element_type=F32)
            stg = st_sc[:, gs]
            yoff = jnp.dot(cg, stg.astype(BF16), preferred_element_type=F32) * cm["p_e"][:, gs]
            xg = cm["x"][:, gs]
            for jp in range(PAIRS_PER_GROUP):
                j = g * PAIRS_PER_GROUP + jp
                xp = xg[:, jp * LANES:(jp + 1) * LANES].astype(BF16)
                ys = []
                for head in (2 * j, 2 * j + 1):
                    m = (cb * _decay(cm, head)).astype(BF16)
                    ys.append(jnp.dot(m, xp, preferred_element_type=F32))
                y_ref[:, j * LANES:(j + 1) * LANES] = (jnp.where(lane < SSD_HEAD_DIM, ys[0], ys[1])
                                                       + yoff[:, jp * LANES:(jp + 1) * LANES])
            snew = lax.dot_general(bg, (cm["w_e"][:, gs] * xg).astype(BF16), TN_DIMS, preferred_element_type=F32)
            st_sc[:, gs] = cm["dl_e"][:, gs] * stg + snew

    return pl.pallas_call(
        body, name=name, grid=(nc,),
        in_specs=[pl.BlockSpec((CHUNK, SSD_INNER), lambda c: (c, 0)),
                  pl.BlockSpec((CHUNK, GW), lambda c: (c, XB)),
                  pl.BlockSpec((CHUNK, GW), lambda c: (c, XB + 1)),
                  pl.BlockSpec((CHUNK, LANES), lambda c: (c, 0)),
                  pl.BlockSpec((1, LANES), lambda c: (0, 0)),
                  pl.BlockSpec((LANES, SSD_INNER), lambda c: (0, 0))],
        out_specs=[pl.BlockSpec((CHUNK, SSD_INNER), lambda c: (c, 0)),
                   pl.BlockSpec((1, SSD_STATE, SSD_INNER), lambda c: (c, 0, 0))],
        out_shape=[jax.ShapeDtypeStruct((Tp, SSD_INNER), F32), jax.ShapeDtypeStruct((nc, SSD_STATE, SSD_INNER), F32)],
        scratch_shapes=[pltpu.VMEM((SSD_STATE, SSD_INNER), F32)],
        compiler_params=_params(("arbitrary",), 24 * CHUNK * SSD_INNER * 4),
    )(xbc, xbc, xbc, dt, alog, e)


def _ssd_bwd(xbc, dt, alog, e, dy, dxs_skip, states, name):
    Tp = xbc.shape[0]
    nc = Tp // CHUNK
    rev = lambda c: nc - 1 - c

    def body(xs_ref, b_ref, c_ref, dt_ref, alog_ref, e_ref, dy_ref, skip_ref, st_ref,
             dxbc_ref, ddt_ref, dalog_ref, dst_sc, dx_sc, t_sc, tw_sc):
        @pl.when(pl.program_id(0) == 0)
        def _():
            dst_sc[...] = jnp.zeros_like(dst_sc)
            dalog_ref[...] = jnp.zeros_like(dalog_ref)

        cm = _ssd_common(xs_ref, dt_ref, alog_ref, e_ref)
        lane = lax.broadcasted_iota(jnp.int32, (CHUNK, LANES), 1)
        dacs_col = jnp.zeros((CHUNK, LANES), F32)
        dacs_row = jnp.zeros((LANES, CHUNK), F32)
        t_last = []
        for g in range(SSD_GROUPS):
            gs = slice(g * GW, (g + 1) * GW)
            cg = c_ref[:, g * SSD_STATE:(g + 1) * SSD_STATE].astype(BF16)
            bg = b_ref[:, g * SSD_STATE:(g + 1) * SSD_STATE].astype(BF16)
            stg = st_ref[0, :, gs]
            stg_b = stg.astype(BF16)
            dstg = dst_sc[:, gs]
            dstg_b = dstg.astype(BF16)
            xg = cm["x"][:, gs]
            dyg = dy_ref[:, gs]
            zg = jnp.dot(cg, stg_b, preferred_element_type=F32)
            dzg = dyg * cm["p_e"][:, gs]
            dzg_b = dzg.astype(BF16)
            dcg = lax.dot_general(dzg_b, stg_b, NT_DIMS, preferred_element_type=F32)
            dst_in = lax.dot_general(cg, dzg_b, TN_DIMS, preferred_element_type=F32)
            dst_in = dst_in + cm["dl_e"][:, gs] * dstg
            t_last.append(jnp.sum(dstg * stg * cm["dl_e"][:, gs], axis=0, keepdims=True))
            weg = cm["w_e"][:, gs]
            dbg = lax.dot_general((weg * xg).astype(BF16), dstg_b, NT_DIMS, preferred_element_type=F32)
            gg = jnp.dot(bg, dstg_b, preferred_element_type=F32)
            dxg = weg * gg
            tw_sc[:, gs] = xg * dxg
            t_sc[:, gs] = dzg * zg - xg * dxg
            cb = lax.dot_general(cg, bg, NT_DIMS, preferred_element_type=F32)
            dcb = jnp.zeros((CHUNK, CHUNK), F32)
            for jp in range(PAIRS_PER_GROUP):
                j = g * PAIRS_PER_GROUP + jp
                ps = slice(jp * LANES, (jp + 1) * LANES)
                xp = xg[:, ps].astype(BF16)
                dyp = dyg[:, ps]
                dxp = dxg[:, ps]
                for half, head in enumerate((2 * j, 2 * j + 1)):
                    lam = _decay(cm, head)
                    m32 = cb * lam
                    sel = (lane < SSD_HEAD_DIM) if half == 0 else (lane >= SSD_HEAD_DIM)
                    dye = jnp.where(sel, dyp, 0.0).astype(BF16)
                    dm = lax.dot_general(dye, xp, NT_DIMS, preferred_element_type=F32)
                    w = dm * m32
                    dacs_col = dacs_col + jnp.where(cm["si"] == head, jnp.sum(w, axis=1, keepdims=True), 0.0)
                    dacs_row = dacs_row + jnp.where(cm["li"] == head, jnp.sum(w, axis=0, keepdims=True), 0.0)
                    dcb = dcb + dm * lam
                    dxp = dxp + lax.dot_general(m32.astype(BF16), dye, TN_DIMS, preferred_element_type=F32)
                dx_sc[:, j * LANES:(j + 1) * LANES] = dxp
            dcb_b = dcb.astype(BF16)
            dcg = dcg + jnp.dot(dcb_b, bg, preferred_element_type=F32)
            dbg = dbg + lax.dot_general(dcb_b, cg, TN_DIMS, preferred_element_type=F32)
            dst_sc[:, gs] = dst_in
            dxbc_ref[:, SSD_INNER + g * SSD_STATE:SSD_INNER + (g + 1) * SSD_STATE] = dbg
            dxbc_ref[:, SSD_INNER + GW + g * SSD_STATE:SSD_INNER + GW + (g + 1) * SSD_STATE] = dcg
        e = cm["e"]
        dacs = lax.dot_general(t_sc[...], e, NT_DIMS, precision=SSD_PREC, preferred_element_type=F32)
        dacs = dacs + dacs_col - dacs_row.T
        last_lane = jnp.concatenate(t_last, axis=1) + jnp.sum(tw_sc[...], axis=0, keepdims=True)
        last_head = lax.dot_general(jnp.broadcast_to(last_lane, (SUBLANES, SSD_INNER)), e, NT_DIMS,
                                    precision=SSD_PREC, preferred_element_type=F32)[0:1, :]
        dacs = dacs + jnp.where(cm["li"] == CHUNK - 1, last_head, 0.0)
        da = lax.dot_general(cm["tri"], dacs, TN_DIMS, precision=SSD_PREC, preferred_element_type=F32)
        dx_all = dx_sc[...]
        ddt = da * cm["a_neg"] + lax.dot_general(dx_all * xs_ref[...], e, NT_DIMS, precision=SSD_PREC,
                                                 preferred_element_type=F32)
        ddt_ref[...] = ddt
        dxbc_ref[:, :SSD_INNER] = dx_all * cm["dte"] + skip_ref[...]
        dalog_ref[0:1, :] += jnp.sum(da * cm["dt"], axis=0, keepdims=True) * cm["a_neg"]

    return pl.pallas_call(
        body, name=name, grid=(nc,),
        in_specs=[pl.BlockSpec((CHUNK, SSD_INNER), lambda c: (rev(c), 0)),
                  pl.BlockSpec((CHUNK, GW), lambda c: (rev(c), XB)),
                  pl.BlockSpec((CHUNK, GW), lambda c: (rev(c), XB + 1)),
                  pl.BlockSpec((CHUNK, LANES), lambda c: (rev(c), 0)),
                  pl.BlockSpec((1, LANES), lambda c: (0, 0)),
                  pl.BlockSpec((LANES, SSD_INNER), lambda c: (0, 0)),
                  pl.BlockSpec((CHUNK, SSD_INNER), lambda c: (rev(c), 0)),
                  pl.BlockSpec((CHUNK, SSD_INNER), lambda c: (rev(c), 0)),
                  pl.BlockSpec((1, SSD_STATE, SSD_INNER), lambda c: (rev(c), 0, 0))],
        out_specs=[pl.BlockSpec((CHUNK, SSD_CONV_DIM), lambda c: (rev(c), 0)),
                   pl.BlockSpec((CHUNK, LANES), lambda c: (rev(c), 0)),
                   pl.BlockSpec((SUBLANES, LANES), lambda c: (0, 0))],
        out_shape=[jax.ShapeDtypeStruct((Tp, SSD_CONV_DIM), F32), jax.ShapeDtypeStruct((Tp, LANES), F32),
                   jax.ShapeDtypeStruct((SUBLANES, LANES), F32)],
        scratch_shapes=[pltpu.VMEM((SSD_STATE, SSD_INNER), F32), pltpu.VMEM((CHUNK, SSD_INNER), F32),
                        pltpu.VMEM((CHUNK, SSD_INNER), F32), pltpu.VMEM((CHUNK, SSD_INNER), F32)],
        compiler_params=_params(("arbitrary",), 32 * CHUNK * SSD_INNER * 4),
    )(xbc, xbc, xbc, dt, alog, e, dy, dxs_skip, states)


def _loss_head(h, target, name):
    Tp, d = h.shape
    nt = Tp // LANES

    def body(h_ref, t_ref, dh_ref, l_ref):
        real = pl.program_id(0) > 0
        err = jnp.where(real, h_ref[...] - t_ref[...], 0.0)
        dh_ref[...] = err * (1.0 / d)
        l_ref[...] = jnp.broadcast_to(0.5 * jnp.sum(err * err) * (1.0 / d), l_ref.shape)

    return pl.pallas_call(
        body, name=name, grid=(nt,),
        in_specs=[pl.BlockSpec((LANES, d), lambda i: (i, 0)),
                  pl.BlockSpec((LANES, d), lambda i: (jnp.maximum(i - 1, 0), 0))],
        out_specs=[pl.BlockSpec((LANES, d), lambda i: (i, 0)),
                   pl.BlockSpec((1, SUBLANES, LANES), lambda i: (i, 0, 0))],
        out_shape=[jax.ShapeDtypeStruct((Tp, d), F32), jax.ShapeDtypeStruct((nt, SUBLANES, LANES), F32)],
        compiler_params=_params(("parallel",), 8 * LANES * d * 4),
    )(h, target)


def _adamw(parts, w, m, v, name):
    shape = w.shape
    C = shape[-1]
    R = int(np.prod(shape[:-1]))
    npart = parts.shape[0]
    parts, w, m, v = parts.reshape(npart, R, C), w.reshape(R, C), m.reshape(R, C), v.reshape(R, C)
    lanes = -(-C // LANES) * LANES
    tr = _pick(R, max(BF16_ROWS, ADAM_ELEMS // lanes), BF16_ROWS) if R % BF16_ROWS == 0 else R
    c1 = 1.0 / (1.0 - ADAM_B1 ** ADAM_STEP)
    c2 = 1.0 / (1.0 - ADAM_B2 ** ADAM_STEP)

    def body(p_ref, w_ref, m_ref, v_ref, g_out, d_out, m_out, v_out):
        g = p_ref[0].astype(F32)
        for p in range(1, npart):
            g = g + p_ref[p].astype(F32)
        m_new = ADAM_B1 * m_ref[...] + (1.0 - ADAM_B1) * g
        v_new = ADAM_B2 * v_ref[...] + (1.0 - ADAM_B2) * (g * g)
        g_out[...] = g
        m_out[...] = m_new
        v_out[...] = v_new
        d_out[...] = -ADAM_LR * ((m_new * c1) / (jnp.sqrt(v_new * c2) + ADAM_EPS) + ADAM_WD * w_ref[...])

    spec = pl.BlockSpec((tr, C), lambda i: (i, 0))
    est = npart * tr * lanes * parts.dtype.itemsize + 7 * tr * lanes * 4
    res = pl.pallas_call(
        body, name=name, grid=(R // tr,),
        in_specs=[pl.BlockSpec((npart, tr, C), lambda i: (0, i, 0)), spec, spec, spec],
        out_specs=[spec] * 4, out_shape=[jax.ShapeDtypeStruct((R, C), F32)] * 4,
        compiler_params=_params(("parallel",), est),
    )(parts, w, m, v)
    return [r.reshape(shape) for r in res]


MESH_ID = pl.DeviceIdType.MESH
N_PEERS = N_DEV - 1


def _dev_index(p):
    return 4 * p[0] + 2 * p[1] + p[2]


class _Background:
    def __init__(self, kind, arrs):
        self.kind, self.arrs, self.n = kind, list(arrs), len(arrs)
        self.npairs = N_PEERS if kind == "gather" else N_CHIPS - 1
        lead = (N_DEV,) if kind == "gather" else ()
        self.out_shape = [jax.ShapeDtypeStruct(lead + a.shape, a.dtype) for a in self.arrs]
        self.specs = [pl.BlockSpec(memory_space=pl.ANY)] * self.n
        self.scratch = [pltpu.SemaphoreType.DMA((self.n, self.npairs)), pltpu.SemaphoreType.DMA((self.n, self.npairs)),
                        pltpu.SemaphoreType.DMA((self.n,))]

    def copies(self, in_refs, out_refs, sems):
        send_sems, recv_sems, local_sems = sems
        x, y, c = lax.axis_index("x"), lax.axis_index("y"), lax.axis_index("c")
        sends, recvs, locals_ = [], [], []

        def remote(t, k, src, dst, to):
            return pltpu.make_async_remote_copy(src_ref=src, dst_ref=dst, send_sem=send_sems.at[t, k],
                                                recv_sem=recv_sems.at[t, k], device_id=to, device_id_type=MESH_ID)

        if self.kind == "gather":
            me = _dev_index((x, y, c))
            peers = [(x, y, 1 - c), (1 - x, y, c), (x, 1 - y, c), (1 - x, 1 - y, c),
                     (1 - x, y, 1 - c), (x, 1 - y, 1 - c), (1 - x, 1 - y, 1 - c)]
            for t in range(self.n):
                locals_.append(pltpu.make_async_copy(in_refs[t], out_refs[t].at[me], local_sems.at[t]))
                for k, p in enumerate(peers):
                    sends.append(remote(t, k, in_refs[t], out_refs[t].at[me], p))
                    recvs.append(remote(t, k, in_refs[t], out_refs[t].at[_dev_index(p)], p))
        else:
            mine = 2 * x + y
            peers = [(1 - x, y), (x, 1 - y), (1 - x, 1 - y)]
            for t in range(self.n):
                locals_.append(pltpu.make_async_copy(in_refs[t].at[mine], out_refs[t].at[mine], local_sems.at[t]))
                for k, p in enumerate(peers):
                    theirs = 2 * p[0] + p[1]
                    sends.append(remote(t, k, in_refs[t].at[theirs], out_refs[t].at[mine], (*p, c)))
                    recvs.append(remote(t, k, in_refs[t].at[mine], out_refs[t].at[theirs], (*p, c)))
        return sends, recvs, locals_

    def start(self, in_refs, out_refs, sems):
        sends, _, locals_ = self.copies(in_refs, out_refs, sems)
        for cp in locals_ + sends:
            cp.start()

    def wait(self, in_refs, out_refs, sems):
        sends, recvs, locals_ = self.copies(in_refs, out_refs, sems)
        for cp in recvs:
            cp.wait_recv()
        for cp in sends:
            cp.wait_send()
        for cp in locals_:
            cp.wait()


def _comm_call(body, name, arrs, out_shape, npairs):
    n = len(arrs)
    any_spec = pl.BlockSpec(memory_space=pl.ANY)
    return pl.pallas_call(
        functools.partial(body, n), name=name, in_specs=[any_spec] * n, out_specs=[any_spec] * n, out_shape=out_shape,
        scratch_shapes=[pltpu.SemaphoreType.DMA((n, npairs)), pltpu.SemaphoreType.DMA((n, npairs)),
                        pltpu.SemaphoreType.DMA((n,))],
    )(*arrs)


def _allgather(arrs, name):
    def body(n, *refs):
        src_refs, out_refs = refs[:n], refs[n:2 * n]
        send_sems, recv_sems, local_sems = refs[2 * n:]
        x, y, c = lax.axis_index("x"), lax.axis_index("y"), lax.axis_index("c")
        me, sibling = (x, y, c), (x, y, 1 - c)
        chips = [(1 - x, y), (x, 1 - y), (1 - x, 1 - y)]

        def copy(t, k, block, to, src=None):
            slot = out_refs[t].at[_dev_index(block)]
            return pltpu.make_async_remote_copy(
                src_ref=slot if src is None else src, dst_ref=slot,
                send_sem=send_sems.at[t, k], recv_sem=recv_sems.at[t, k],
                device_id=to, device_id_type=MESH_ID)

        sends, locals_ = [], []
        for t in range(n):
            mine = pltpu.make_async_copy(src_refs[t], out_refs[t].at[_dev_index(me)], local_sems.at[t])
            mine.start()
            locals_.append(mine)
            first = [copy(t, 0, me, sibling, src=src_refs[t])]
            first += [copy(t, 1 + j, me, (*chip, c), src=src_refs[t]) for j, chip in enumerate(chips)]
            for cp in first:
                cp.start()
            sends += first
        for j, chip in enumerate(chips):
            for t in range(n):
                copy(t, 1 + j, (*chip, c), me).wait_recv()
                passed = copy(t, 4 + j, (*chip, c), sibling)
                passed.start()
                sends.append(passed)
        for t in range(n):
            copy(t, 0, sibling, me).wait_recv()
            for j, chip in enumerate(chips):
                copy(t, 4 + j, (*chip, 1 - c), me).wait_recv()
        for cp in sends:
            cp.wait_send()
        for cp in locals_:
            cp.wait()

    return _comm_call(body, name, arrs, [jax.ShapeDtypeStruct((N_DEV,) + a.shape, a.dtype) for a in arrs], N_PEERS)


N_CHIPS = N_DEV // 2
CHIPS = [(0, 0), (0, 1), (1, 0), (1, 1)]


def _sibling_exchange(arrs, name):
    def body(n, *refs):
        in_refs, out_refs = refs[:n], refs[n:2 * n]
        send_sems, recv_sems, _ = refs[2 * n:]
        x, y, c = lax.axis_index("x"), lax.axis_index("y"), lax.axis_index("c")
        sibling = (x, y, 1 - c)

        def copy(t, j):
            return pltpu.make_async_remote_copy(
                src_ref=in_refs[t].at[_dev_index((*CHIPS[j], 1 - c))], dst_ref=out_refs[t].at[j],
                send_sem=send_sems.at[t, j], recv_sem=recv_sems.at[t, j],
                device_id=sibling, device_id_type=MESH_ID)

        copies = [copy(t, j) for t in range(n) for j in range(N_CHIPS)]
        for cp in copies:
            cp.start()
        for cp in copies:
            cp.wait_recv()
        for cp in copies:
            cp.wait_send()

    return _comm_call(body, name, arrs, [jax.ShapeDtypeStruct((N_CHIPS,) + a.shape[1:], a.dtype) for a in arrs], N_CHIPS)


def _chip_exchange(arrs, name):
    def body(n, *refs):
        in_refs, out_refs = refs[:n], refs[n:2 * n]
        send_sems, recv_sems, local_sems = refs[2 * n:]
        x, y, c = lax.axis_index("x"), lax.axis_index("y"), lax.axis_index("c")
        mine = 2 * x + y
        peers = [(1 - x, y), (x, 1 - y), (1 - x, 1 - y)]

        def copy(t, k, src_chip, dst_chip, to):
            return pltpu.make_async_remote_copy(
                src_ref=in_refs[t].at[src_chip], dst_ref=out_refs[t].at[dst_chip],
                send_sem=send_sems.at[t, k], recv_sem=recv_sems.at[t, k],
                device_id=(*to, c), device_id_type=MESH_ID)

        sends, locals_ = [], []
        for t in range(n):
            own = pltpu.make_async_copy(in_refs[t].at[mine], out_refs[t].at[mine], local_sems.at[t])
            own.start()
            locals_.append(own)
            for k, p in enumerate(peers):
                cp = copy(t, k, 2 * p[0] + p[1], mine, p)
                cp.start()
                sends.append(cp)
        for t in range(n):
            for k, p in enumerate(peers):
                copy(t, k, mine, 2 * p[0] + p[1], p).wait_recv()
        for cp in sends:
            cp.wait_send()
        for cp in locals_:
            cp.wait()

    return _comm_call(body, name, arrs, [jax.ShapeDtypeStruct(a.shape, a.dtype) for a in arrs], N_CHIPS - 1)


def _add_pairs(a, b, name):
    shape = a.shape
    C = shape[-1]
    R = int(np.prod(shape[:-1]))
    lanes = -(-C // LANES) * LANES
    tr = _pick(R, max(BF16_ROWS, 2 * ADAM_ELEMS // lanes), BF16_ROWS) if R % BF16_ROWS == 0 else R

    def body(a_ref, b_ref, o_ref):
        o_ref[...] = (a_ref[...].astype(F32) + b_ref[...].astype(F32)).astype(o_ref.dtype)

    spec = pl.BlockSpec((tr, C), lambda i: (i, 0))
    return pl.pallas_call(
        body, name=name, grid=(R // tr,), in_specs=[spec, spec], out_specs=spec,
        out_shape=jax.ShapeDtypeStruct((R, C), a.dtype),
        compiler_params=_params(("parallel",), 3 * tr * lanes * 4),
    )(a.reshape(R, C), b.reshape(R, C)).reshape(shape)


WEIGHTS = ['meta_tokens', 'emb_ln_g', 'emb_ln_b', 'w_in', 'q_norm_g', 'w_q_b', 'kv_norm_g', 'w_kv_b', 'w_o_attn',
           'ssd_conv_w', 'ssd_conv_b', 'dt_bias', 'a_log', 'd_skip', 'ssd_norm_g', 'w_o_ssd', 'w_out', 'ln1_g',
           'ln1_b', 'w_up', 'ffn_conv_w', 'ffn_conv_b', 'w_down', 'ln2_g', 'ln2_b']
BIG = {'w_in': 2, 'w_q_b': 2, 'w_kv_b': 2, 'w_o_attn': 1, 'w_o_ssd': 1, 'w_out': 1, 'w_up': 2, 'w_down': 1}
SMALL_SHARDED = {'meta_tokens': 1, 'ssd_conv_w': 2, 'ffn_conv_w': 2}
REPLICATED = [n for n in WEIGHTS if n not in BIG and n not in SMALL_SHARDED]
BIG_COLS = 1024
SMALL_COLS = LANES


def _flatten(arrs, cols, row_mult, lead=False):
    parts, offs, off = [], [], 0
    for a in arrs:
        a2 = a.reshape(N_DEV, -1) if lead else a.reshape(1, -1)
        n = a2.shape[1]
        pad = -n % cols
        parts.append(jnp.pad(a2, ((0, 0), (0, pad))))
        offs.append((off, n))
        off += n + pad
    rows = off // cols
    extra = (-rows % row_mult) * cols
    if extra:
        parts.append(jnp.zeros((parts[0].shape[0], extra), parts[0].dtype))
    flat = jnp.concatenate(parts, axis=1)
    flat = flat.reshape(flat.shape[0], -1, cols)
    return (flat if lead else flat[0]), offs


def _unflatten(flat, offs, shapes):
    f = flat.reshape(-1)
    return [f[o:o + n].reshape(s) for (o, n), s in zip(offs, shapes)]


def _to_pieces(g, axis):
    s = g.shape[axis] // N_DEV
    g = g.reshape(g.shape[:axis] + (N_DEV, s) + g.shape[axis + 1:])
    return jnp.moveaxis(g, axis, 0).reshape(N_DEV, -1)


def _from_pieces(p, shard_shape, axis):
    g = jnp.moveaxis(p.reshape((N_DEV,) + tuple(shard_shape)), 0, axis)
    sh = list(shard_shape)
    sh[axis] *= N_DEV
    return g.reshape(sh)


def _in_proj_pad(w):
    e = np.cumsum((0,) + IN_SIZES)
    ql, kvl, kpe, z, xbc, dt, ga, gs = [w[:, e[j]:e[j + 1]] for j in range(8)]
    zc = lambda n: jnp.zeros((w.shape[0], n), w.dtype)
    return jnp.concatenate([ql, kvl, z, xbc, ga, gs, kpe, zc(LANES - QK_ROPE), dt, zc(LANES - SSD_HEADS)], axis=1)


def _in_proj_unpad(d):
    seg = lambda o, n: d[:, o:o + n]
    return jnp.concatenate([seg(OQ, Q_LORA), seg(OKV, KV_LORA), seg(OKPE, QK_ROPE), seg(OZ, SSD_INNER),
                            seg(OXBC, SSD_CONV_DIM), seg(ODT, SSD_HEADS), seg(OGA, D_MODEL), seg(OGS, D_MODEL)], axis=1)


def _q_pad(w):
    w3 = w.reshape(Q_LORA, HEADS, QK_NOPE + QK_ROPE)
    return jnp.concatenate([w3, jnp.zeros((Q_LORA, HEADS, QHEAD - QK_NOPE - QK_ROPE), w.dtype)], axis=2).reshape(Q_LORA, HEADS * QHEAD)


def _q_unpad(d):
    return d.reshape(Q_LORA, HEADS, QHEAD)[:, :, :QK_NOPE + QK_ROPE].reshape(Q_LORA, HEADS * (QK_NOPE + QK_ROPE))


def _kv_perm(w):
    w3 = w.reshape(KV_LORA, HEADS, QK_NOPE + V_HEAD)
    return jnp.concatenate([w3[:, :, :QK_NOPE].reshape(KV_LORA, -1), w3[:, :, QK_NOPE:].reshape(KV_LORA, -1)], axis=1)


def _kv_unperm(d):
    kn = d[:, :HEADS * QK_NOPE].reshape(KV_LORA, HEADS, QK_NOPE)
    v = d[:, HEADS * QK_NOPE:].reshape(KV_LORA, HEADS, V_HEAD)
    return jnp.concatenate([kn, v], axis=2).reshape(KV_LORA, HEADS * (QK_NOPE + V_HEAD))


def _row_vec(v, width=None):
    v = v.reshape(1, -1).astype(F32)
    if width is not None and v.shape[1] < width:
        v = jnp.pad(v, ((0, 0), (0, width - v.shape[1])))
    return v


def _pad_rows8(w):
    return jnp.pad(w.astype(F32), ((0, SUBLANES - w.shape[0]), (0, 0)))


def _tables(Tp, npad):
    pos = jnp.maximum(jnp.arange(Tp, dtype=jnp.int32) - npad, 0).astype(F32)
    inv_freq = 1.0 / (ROPE_THETA ** (jnp.arange(0, QK_ROPE, 2, dtype=F32) / QK_ROPE))
    ang = pos[:, None] * inv_freq[None, :]
    ang = jnp.concatenate([ang, ang], axis=-1)
    zeros = jnp.zeros((Tp, LANES - QK_ROPE), F32)
    cos = jnp.concatenate([jnp.cos(ang), zeros], axis=1)
    sin = jnp.concatenate([jnp.sin(ang), zeros], axis=1)
    rot = np.zeros((LANES, LANES), np.float32)
    half = QK_ROPE // 2
    for i in range(half):
        rot[i + half, i] = -1.0
        rot[i, i + half] = 1.0
    expand = np.zeros((LANES, SSD_INNER), np.float32)
    for hd in range(SSD_HEADS):
        expand[hd, hd * SSD_HEAD_DIM:(hd + 1) * SSD_HEAD_DIM] = 1.0
    return cos, sin, jnp.asarray(rot), jnp.asarray(expand)


def _layer_rows(proj, tb):
    rows_a = [_row(proj, Q_LORA, OQ // Q_LORA), _row(proj, KV_LORA, OKV // KV_LORA), _row(proj, LANES, OKPE // LANES),
              _row(proj, LANES, ODT // LANES), _row(tb["cos"], diff=False), _row(tb["sin"], diff=False)]
    return rows_a


def _layer_fwd(h, h_bf, P, tb, fns, npad, bg=None):
    both = [_out(D_MODEL, F32), _out(D_MODEL, BF16)]
    res_ln_twice = lambda *a: fns["res_ln"](*a) * 2
    proj = _mm(h_bf, P["w_in"], F32, "in_proj")
    rows_a = _layer_rows(proj, tb)
    consts_a = [_row(tb["rot"], diff=False), _row(P["q_norm_g"]), _row(P["kv_norm_g"]), _row(P["dt_bias"])]
    qn, kvn, kr8, dt = _rw_fwd(fns["in_post"], rows_a, consts_a,
                               [_out(Q_LORA, BF16), _out(KV_LORA, BF16), _out(HEADS * LANES, BF16), _out(LANES, F32)],
                               "in_post")
    q = _mm(qn, P["w_q"], F32, "q_proj")
    rows_q = [_row(q, QHEAD, 0, grp=True), _row(tb["cos"], diff=False), _row(tb["sin"], diff=False)]
    qr = _rw_fwd(fns["q_post"], rows_q, [_row(tb["rot"], diff=False)], [_out(HEADS * QHEAD, BF16, QHEAD, grp=True)],
                 "q_post", ng=HEADS)[0]
    kv = _mm(kvn, P["w_kv"], BF16, "kv_proj")
    o, lse, *carried = _flash_fwd(qr, kv, kr8, npad, "attn_fwd_gather" if bg else "attn_fwd", bg=bg)
    ya = _mm(o, P["w_o_attn"], F32, "attn_out")
    xbc = _conv_fwd(proj, OXBC, SSD_CONV_DIM, P["ssd_conv_w"], P["ssd_conv_b"], SSD_CONV, True, npad, "ssd_conv")
    y, states = _ssd_fwd(xbc, dt, P["a_log"], tb["expand"], "ssd_fwd")
    rows_b = [_row(y, GW, 0, grp=True), _row(xbc, GW, 0, grp=True), _row(proj, GW, OZ // GW, grp=True)]
    consts_b = [_row(P["d_skip"], GW, 0, grp=True), _row(P["ssd_norm_g"], GW, 0, grp=True)]
    yn = _rw_fwd(fns["gated"], rows_b, consts_b, [_out(SSD_INNER, BF16, GW, grp=True)], "ssd_gate", ng=SSD_GROUPS)[0]
    ys = _mm(yn, P["w_o_ssd"], F32, "ssd_out")
    rows_c = [_row(proj, D_MODEL, OGA // D_MODEL), _row(proj, D_MODEL, OGS // D_MODEL), _row(ya), _row(ys)]
    mixed = _rw_fwd(fns["mix"], rows_c, [], [_out(D_MODEL, BF16)], "mix")[0]
    mo = _mm(mixed, P["w_out"], F32, "mix_out")
    consts_1 = [_row(P["ln1_g"]), _row(P["ln1_b"])]
    h1, h1_bf = _rw_fwd(res_ln_twice, [_row(h), _row(mo)], consts_1, both, "ln1")
    up = _mm(h1_bf, P["w_up"], BF16, "ffn_up")
    u = _conv_fwd(up, 0, 2 * D_FF, P["ffn_conv_w"], P["ffn_conv_b"], FFN_CONV, False, npad, "ffn_conv", BF16)
    act = _rw_fwd(fns["glu"], [_row(u)], [], [_out(D_FF, BF16)], "ffn_glu")[0]
    fo = _mm(act, P["w_down"], F32, "ffn_down")
    consts_2 = [_row(P["ln2_g"]), _row(P["ln2_b"])]
    h2, h2_bf = _rw_fwd(res_ln_twice, [_row(h1), _row(fo)], consts_2, both, "ln2")
    res = dict(h=h, h_bf=h_bf, proj=proj, qn=qn, kvn=kvn, kr8=kr8, dt=dt, q=q, qr=qr, kv=kv, o=o, lse=lse, ya=ya,
               xbc=xbc, y=y, states=states, yn=yn, ys=ys, mixed=mixed, mo=mo, h1=h1, h1_bf=h1_bf, up=up, u=u, act=act,
               fo=fo)
    return h2, h2_bf, res, carried


def _layer_bwd(dh2, r, P, tb, fns, npad, bg=None):
    g = {}
    consts_2 = [_row(P["ln2_g"]), _row(P["ln2_b"])]
    (dh1_a, dfo), (g["ln2_g"], g["ln2_b"]) = _rw_bwd(fns["res_ln"], [_row(r["h1"]), _row(r["fo"])], consts_2,
                                                     [_row(dh2)], [F32, BF16], "ln2_bwd")
    g["w_down"] = _mm(r["act"], dfo, BF16, "dw_down", ta=True)
    dact = _mm(dfo, P["w_down"], BF16, "d_act", tb=True)
    (du,), _ = _rw_bwd(fns["glu"], [_row(r["u"])], [], [_row(dact)], [BF16], "glu_bwd")
    dup, g["ffn_conv_w"], g["ffn_conv_b"] = _conv_bwd(r["up"], 0, 2 * D_FF, P["ffn_conv_w"], P["ffn_conv_b"], du,
                                                      FFN_CONV, False, npad, "ffn_conv_bwd")
    g["w_up"] = _mm(r["h1_bf"], dup, BF16, "dw_up", ta=True)
    dh1 = _mm(dup, P["w_up"], F32, "d_h1", tb=True, add=dh1_a)
    consts_1 = [_row(P["ln1_g"]), _row(P["ln1_b"])]
    (dh_a, dmo), (g["ln1_g"], g["ln1_b"]) = _rw_bwd(fns["res_ln"], [_row(r["h"]), _row(r["mo"])], consts_1,
                                                    [_row(dh1)], [F32, BF16], "ln1_bwd")
    g["w_out"] = _mm(r["mixed"], dmo, BF16, "dw_out", ta=True)
    dmixed = _mm(dmo, P["w_out"], F32, "d_mixed", tb=True)
    proj = r["proj"]
    rows_c = [_row(proj, D_MODEL, OGA // D_MODEL), _row(proj, D_MODEL, OGS // D_MODEL), _row(r["ya"]), _row(r["ys"])]
    (dga, dgs, dya, dys), _ = _rw_bwd(fns["mix"], rows_c, [], [_row(dmixed)], [BF16] * 4, "mix_bwd")
    g["w_o_attn"] = _mm(r["o"], dya, BF16, "dw_o_attn", ta=True)
    do = _mm(dya, P["w_o_attn"], F32, "d_o", tb=True)
    g["w_o_ssd"] = _mm(r["yn"], dys, BF16, "dw_o_ssd", ta=True)
    dyn = _mm(dys, P["w_o_ssd"], F32, "d_yn", tb=True)
    rows_b = [_row(r["y"], GW, 0, grp=True), _row(r["xbc"], GW, 0, grp=True), _row(proj, GW, OZ // GW, grp=True)]
    consts_b = [_row(P["d_skip"], GW, 0, grp=True), _row(P["ssd_norm_g"], GW, 0, grp=True)]
    (dy, dxs_skip, dz), (g["d_skip"], g["ssd_norm_g"]) = _rw_bwd(
        fns["gated"], rows_b, consts_b, [_row(dyn, GW, 0, grp=True)], [F32, F32, BF16], "ssd_gate_bwd", ng=SSD_GROUPS)
    dxbc, ddt, g["a_log"] = _ssd_bwd(r["xbc"], r["dt"], P["a_log"], tb["expand"], dy, dxs_skip, r["states"], "ssd_bwd")
    dxbc_pre, g["ssd_conv_w"], g["ssd_conv_b"] = _conv_bwd(proj, OXBC, SSD_CONV_DIM, P["ssd_conv_w"], P["ssd_conv_b"],
                                                           dxbc, SSD_CONV, True, npad, "ssd_conv_bwd")
    delta = _attn_delta(do, r["o"], "attn_delta")
    dqr, dkn, dkr8, dv, *carried = _flash_bwd(r["qr"], r["kv"], r["kr8"], do, r["lse"], delta, npad,
                                              "attn_bwd_exchange" if bg else "attn_bwd", bg=bg)
    rows_q = [_row(r["q"], QHEAD, 0, grp=True), _row(tb["cos"], diff=False), _row(tb["sin"], diff=False)]
    (dq,), _ = _rw_bwd(fns["q_post"], rows_q, [_row(tb["rot"], diff=False)], [_row(dqr, QHEAD, 0, grp=True)], [BF16],
                       "q_post_bwd", ng=HEADS)
    g["w_q"] = _mm(r["qn"], dq, BF16, "dw_q", ta=True)
    dqn = _mm(dq, P["w_q"], F32, "d_qn", tb=True)
    dkv = jnp.concatenate([dkn, dv], axis=1)
    g["w_kv"] = _mm(r["kvn"], dkv, BF16, "dw_kv", ta=True)
    dkvn = _mm(dkv, P["w_kv"], F32, "d_kvn", tb=True)
    rows_a = _layer_rows(proj, tb)
    consts_a = [_row(tb["rot"], diff=False), _row(P["q_norm_g"]), _row(P["kv_norm_g"]), _row(P["dt_bias"])]
    (dql, dkvl, dkpe, ddtr), (g["q_norm_g"], g["kv_norm_g"], g["dt_bias"]) = _rw_bwd(
        fns["in_post"], rows_a, consts_a, [_row(dqn), _row(dkvn), _row(dkr8), _row(ddt)], [BF16] * 4, "in_post_bwd")
    dproj = jnp.concatenate([dql, dkvl, dz, dxbc_pre, dga, dgs, dkpe, ddtr], axis=1)
    g["w_in"] = _mm(r["h_bf"], dproj, BF16, "dw_in", ta=True)
    dh = _mm(dproj, P["w_in"], F32, "d_h", tb=True, add=dh_a)
    return dh, g, carried


def _full_weight(g, axis):
    if axis == 1:
        return g.reshape(-1, g.shape[-1])
    return jnp.concatenate([g[p] for p in range(N_DEV)], axis=1)


def _grad_pieces(d, axis):
    if axis == 1:
        return d.reshape(N_DEV, -1, d.shape[1])
    return jnp.transpose(d.reshape(d.shape[0], N_DEV, -1), (1, 0, 2))


def _layer_params(gathered, small, i):
    full = {n: _full_weight(gathered[n], BIG[n]) for n in BIG}
    P = {}
    P["w_in"] = _in_proj_pad(full["w_in"])
    P["w_q"] = _q_pad(full["w_q_b"])
    P["w_kv"] = _kv_perm(full["w_kv_b"])
    for n in ("w_o_attn", "w_o_ssd", "w_out", "w_up", "w_down"):
        P[n] = full[n]
    P["q_norm_g"] = _row_vec(small["q_norm_g"][i])
    P["kv_norm_g"] = _row_vec(small["kv_norm_g"][i])
    P["dt_bias"] = _row_vec(small["dt_bias"][i], LANES)
    P["a_log"] = _row_vec(small["a_log"][i], LANES)
    P["d_skip"] = _row_vec(jnp.repeat(small["d_skip"][i], SSD_HEAD_DIM))
    P["ssd_norm_g"] = _row_vec(small["ssd_norm_g"][i])
    P["ssd_conv_w"] = _pad_rows8(small["ssd_conv_w"][i])
    P["ssd_conv_b"] = _row_vec(small["ssd_conv_b"][i])
    P["ffn_conv_w"] = _pad_rows8(small["ffn_conv_w"][i])
    P["ffn_conv_b"] = _row_vec(small["ffn_conv_b"][i])
    for n in ("ln1_g", "ln1_b", "ln2_g", "ln2_b"):
        P[n] = _row_vec(small[n][i])
    return P


def _layer_grads_to_reference_layout(g):
    out = {}
    out["w_in"] = _in_proj_unpad(g["w_in"])
    out["w_q_b"] = _q_unpad(g["w_q"])
    out["w_kv_b"] = _kv_unperm(g["w_kv"])
    for n in ("w_o_attn", "w_o_ssd", "w_out", "w_up", "w_down"):
        out[n] = g[n]
    out["q_norm_g"] = g["q_norm_g"][0]
    out["kv_norm_g"] = g["kv_norm_g"][0]
    out["dt_bias"] = g["dt_bias"][0, :SSD_HEADS]
    out["a_log"] = g["a_log"][0, :SSD_HEADS]
    out["d_skip"] = g["d_skip"].reshape(SSD_HEADS, SSD_HEAD_DIM).sum(axis=1)
    out["ssd_norm_g"] = g["ssd_norm_g"][0]
    out["ssd_conv_w"] = g["ssd_conv_w"][:SSD_CONV]
    out["ssd_conv_b"] = g["ssd_conv_b"][0]
    out["ffn_conv_w"] = g["ffn_conv_w"][:FFN_CONV]
    out["ffn_conv_b"] = g["ffn_conv_b"][0]
    for n in ("ln1_g", "ln1_b", "ln2_g", "ln2_b"):
        out[n] = g[n][0]
    return out


def kernel(x, meta_tokens, emb_ln_g, emb_ln_b, w_in, q_norm_g, w_q_b, kv_norm_g, w_kv_b, w_o_attn, ssd_conv_w, ssd_conv_b, dt_bias, a_log, d_skip, ssd_norm_g, w_o_ssd, w_out, ln1_g, ln1_b, w_up, ffn_conv_w, ffn_conv_b, w_down, ln2_g, ln2_b, loss_target, m_meta_tokens, m_emb_ln_g, m_emb_ln_b, m_w_in, m_q_norm_g, m_w_q_b, m_kv_norm_g, m_w_kv_b, m_w_o_attn, m_ssd_conv_w, m_ssd_conv_b, m_dt_bias, m_a_log, m_d_skip, m_ssd_norm_g, m_w_o_ssd, m_w_out, m_ln1_g, m_ln1_b, m_w_up, m_ffn_conv_w, m_ffn_conv_b, m_w_down, m_ln2_g, m_ln2_b, v_meta_tokens, v_emb_ln_g, v_emb_ln_b, v_w_in, v_q_norm_g, v_w_q_b, v_kv_norm_g, v_w_kv_b, v_w_o_attn, v_ssd_conv_w, v_ssd_conv_b, v_dt_bias, v_a_log, v_d_skip, v_ssd_norm_g, v_w_o_ssd, v_w_out, v_ln1_g, v_ln1_b, v_w_up, v_ffn_conv_w, v_ffn_conv_b, v_w_down, v_ln2_g, v_ln2_b):
    given = dict(locals())
    w = {n: given[n] for n in WEIGHTS}
    m = {n: given["m_" + n] for n in WEIGHTS}
    v = {n: given["v_" + n] for n in WEIGHTS}
    seq = x.shape[1]
    assert x.shape[0] == 1 and seq % LANES == 0
    npad = LANES - N_META
    Tp = npad + N_META + seq
    depth = w_in.shape[0]

    big_names, small_names = list(BIG), list(SMALL_SHARDED)
    ws, offs_s = _flatten([w[n] for n in small_names], SMALL_COLS, SUBLANES)
    shards = [[w[n][i].astype(BF16) for n in big_names] for i in range(depth)]
    got = _allgather(shards[0] + [ws], "weight_allgather")
    gathered = dict(zip(big_names, got[:-1]))
    gsm = got[-1]
    small = {n: w[n] for n in REPLICATED}
    for n, (o, sz) in zip(small_names, offs_s):
        small[n] = _from_pieces(gsm.reshape(N_DEV, -1)[:, o:o + sz], w[n].shape, SMALL_SHARDED[n])

    fns = _make_stage_fns(npad)
    cos, sin, rot, expand = _tables(Tp, npad)
    tb = dict(cos=cos, sin=sin, rot=rot, expand=expand)
    top = jnp.pad(small["meta_tokens"], ((npad, 0), (0, 0)))
    hcat = jnp.concatenate([top, x[0]], axis=0)
    consts_e = [_row(_row_vec(w["emb_ln_g"])), _row(_row_vec(w["emb_ln_b"]))]
    h, h_bf = _rw_fwd(lambda *a: fns["ln"](*a) * 2, [_row(hcat)], consts_e, [_out(D_MODEL, F32), _out(D_MODEL, BF16)],
                      "emb_ln")
    layers, saved = [], []
    for i in range(depth):
        layers.append(_layer_params(gathered, small, i))
        bg = _Background("gather", shards[i + 1]) if i + 1 < depth else None
        h, h_bf, res, carried = _layer_fwd(h, h_bf, layers[i], tb, fns, npad, bg=bg)
        gathered = dict(zip(big_names, carried))
        saved.append(res)
    dh, lparts = _loss_head(h, loss_target[0], "loss_head")
    loss = lax.psum(jnp.sum(lparts[:, 0, 0]), ("x", "y", "c"))

    core = lax.axis_index("c")

    def chip_partials(pieces, tag):
        from_sibling = _sibling_exchange(pieces, "grad_exchange_cores_" + tag)
        sums = []
        for k, (p, r) in enumerate(zip(pieces, from_sibling)):
            own = lax.dynamic_index_in_dim(p.reshape((N_CHIPS, 2) + p.shape[1:]), core, axis=1, keepdims=False)
            sums.append(_add_pairs(own, r, "grad_chip_sum_%s_%d" % (tag, k)))
        return sums

    lg, recv_big, pending = [None] * depth, [None] * depth, None
    for i in reversed(range(depth)):
        bg = _Background("chips", pending) if pending is not None else None
        dh, gi, carried = _layer_bwd(dh, saved[i], layers[i], tb, fns, npad, bg=bg)
        if bg:
            recv_big[i + 1] = carried
        lg[i] = _layer_grads_to_reference_layout(gi)
        pending = None
        if i > 0:
            pending = chip_partials([_grad_pieces(lg[i][n], BIG[n]).astype(BF16) for n in big_names], "l%d" % i)
    (dhcat,), (d_emb_g, d_emb_b) = _rw_bwd(fns["ln"], [_row(hcat)], consts_e, [_row(dh)], [F32], "emb_ln_bwd")
    grad_x = dhcat[LANES:][None]
    local = {n: jnp.stack([lg[i][n] for i in range(depth)]) for n in lg[0] if n not in BIG}
    local["meta_tokens"] = dhcat[npad:LANES]
    local["emb_ln_g"] = d_emb_g[0]
    local["emb_ln_b"] = d_emb_b[0]

    sm_names = small_names + REPLICATED
    sm_pieces = [_to_pieces(local[n], SMALL_SHARDED[n]) for n in small_names]
    sm_pieces += [jnp.broadcast_to(local[n].reshape(1, -1), (N_DEV, local[n].size)) for n in REPLICATED]
    ps, _ = _flatten(sm_pieces, SMALL_COLS, BF16_ROWS, lead=True)
    pieces = [_grad_pieces(lg[0][n], BIG[n]).astype(BF16) for n in big_names] + [ps]
    recv = _chip_exchange(chip_partials(pieces, "l0"), "grad_exchange_chips")
    recv_big[0] = recv[:-1]
    outs = {}
    kinds = ("grad", "delta", "new_m", "new_v")
    for k, n in enumerate(big_names):
        parts = jnp.stack([recv_big[i][k] for i in range(depth)], axis=1)
        for kind, a in zip(kinds, _adamw(parts, w[n], m[n], v[n], "adamw_" + n)):
            outs[kind + "_" + n] = a
    wf, offs = _flatten([w[n] for n in sm_names], SMALL_COLS, BF16_ROWS)
    mf, _ = _flatten([m[n] for n in sm_names], SMALL_COLS, BF16_ROWS)
    vf, _ = _flatten([v[n] for n in sm_names], SMALL_COLS, BF16_ROWS)
    shapes = [w[n].shape for n in sm_names]
    for kind, flat in zip(kinds, _adamw(recv[-1], wf, mf, vf, "adamw_small")):
        for n, a in zip(sm_names, _unflatten(flat, offs, shapes)):
            outs[kind + "_" + n] = a
    result = [loss, grad_x]
    for kind in ("grad", "delta", "new_m", "new_v"):
        result += [outs[kind + "_" + n] for n in WEIGHTS]
    return tuple(result)
```

```python
import functools

import jax
import jax.numpy as jnp
import numpy as np
from jax import lax
from jax.experimental import pallas as pl
from jax.experimental.pallas import tpu as pltpu

F32 = jnp.float32
BF16 = jnp.bfloat16
HIGHEST = lax.Precision.HIGHEST
SSD_PREC = lax.Precision.HIGH

D_MODEL = 1024
DEPTH = 2
N_META = 16
HEADS = 8
Q_LORA = 768
KV_LORA = 256
QK_NOPE = 128
QK_ROPE = 64
V_HEAD = 128
ROPE_THETA = 10000.0
SSD_INNER = 2048
SSD_HEAD_DIM = 64
SSD_HEADS = 32
SSD_GROUPS = 4
SSD_STATE = 128
SSD_CONV = 4
SSD_CONV_DIM = SSD_INNER + 2 * SSD_GROUPS * SSD_STATE
CHUNK = 128
D_FF = 2816
FFN_CONV = 3
LN_EPS = 1e-5
RMS_EPS = 1e-6
ALPHA = (2 * DEPTH) ** 0.25
IN_SIZES = (Q_LORA, KV_LORA, QK_ROPE, SSD_INNER, SSD_CONV_DIM, SSD_HEADS, D_MODEL, D_MODEL)
ATT_SCALE = (QK_NOPE + QK_ROPE) ** -0.5
NEG_INF = -1e30
ADAM_LR, ADAM_B1, ADAM_B2, ADAM_EPS, ADAM_WD, ADAM_STEP = 0.001, 0.9, 0.999, 1e-08, 0.01, 10

LANES = 128
SUBLANES = 8
VMEM_BYTES = 64 * 1024 * 1024
N_DEV = 8

OQ, OKV, OZ, OXBC, OGA, OGS, OKPE, ODT = 0, 768, 1024, 3072, 6144, 7168, 8192, 8320
IN_PAD = 8448
QHEAD = 256

ROW_TILE = 640
MM_COL_TILE = 1408
MM_ROW_TILE = 1664
MM_VMEM_BUDGET = 46 * 1024 * 1024
MM_K_TILE = 2816
MM_TOKEN_K_TILE = 1664
ATT_TILE = 640
ATT_HEADS_PER_STEP = 4
BF16_ROWS = 16
HALO = BF16_ROWS
ROW_BUDGET = 7 * 1024 * 1024
ADAM_ELEMS = 160 * 1024


def _pick(n, target, q=LANES):
    assert n % q == 0, (n, q)
    units = n // q
    best = q
    for d in range(1, units + 1):
        if units % d == 0 and d * q <= target:
            best = d * q
    return best


def _pick_rows(n, row_bytes):
    return _pick(n, max(BF16_ROWS, ROW_BUDGET // row_bytes), BF16_ROWS)


def _params(sem, est_bytes):
    limit = int(min(VMEM_BYTES - (6 << 20), max(32 << 20, 2 * est_bytes + (8 << 20))))
    return pltpu.CompilerParams(dimension_semantics=sem, vmem_limit_bytes=limit)


def _nbytes(shape, dtype):
    return int(np.prod(shape)) * jnp.dtype(dtype).itemsize


def _mm(a, b, out_dtype, name, ta=False, tb=False, add=None):
    assert not (ta and tb)
    if ta:
        K, M = a.shape
        tm = _pick(M, MM_COL_TILE)
        tk = _pick(K, MM_TOKEN_K_TILE)
    else:
        M, K = a.shape
        tk = _pick(K, MM_K_TILE)
    N, K2 = (b.shape if tb else b.shape[::-1])
    assert K == K2
    tn = _pick(N, MM_COL_TILE)
    nk = K // tk

    def vmem_estimate(tm):
        e = 2 * (tm * tk * a.dtype.itemsize + tk * tn * b.dtype.itemsize + tm * tn * jnp.dtype(out_dtype).itemsize)
        e += tm * tn * 4 + tm * tk * 2
        return e + (tm * tn * 4 if nk > 1 else 0) + (2 * tm * tn * 4 if add is not None else 0)

    if not ta:
        tm = _pick(M, MM_ROW_TILE, BF16_ROWS)
        while vmem_estimate(tm) > MM_VMEM_BUDGET and tm > BF16_ROWS:
            tm = _pick(M, tm - BF16_ROWS, BF16_ROWS)
    dn = (((0,), (0,)), ((), ())) if ta else ((((1,), (1,)), ((), ())) if tb else (((1,), (0,)), ((), ())))

    def body(*refs):
        a_ref, b_ref = refs[:2]
        add_ref = refs[2] if add is not None else None
        o_ref = refs[2 + (add is not None)]
        d = lax.dot_general(a_ref[...].astype(BF16), b_ref[...].astype(BF16), dn, preferred_element_type=F32)

        def finish(r):
            if add is not None:
                r = r + add_ref[...].astype(F32)
            o_ref[...] = r.astype(out_dtype)

        if nk == 1:
            finish(d)
            return
        acc = refs[-1]
        k = pl.program_id(2)

        @pl.when(k == 0)
        def _():
            acc[...] = d

        @pl.when((k > 0) & (k < nk - 1))
        def _():
            acc[...] += d

        @pl.when(k == nk - 1)
        def _():
            finish(acc[...] + d)

    if ta:
        a_spec = pl.BlockSpec((tk, tm), lambda i, j, k: (k, i))
    else:
        a_spec = pl.BlockSpec((tm, tk), lambda i, j, k: (i, k))
    b_spec = pl.BlockSpec((tn, tk), lambda i, j, k: (j, k)) if tb else pl.BlockSpec((tk, tn), lambda i, j, k: (k, j))
    in_specs = [a_spec, b_spec]
    args = [a, b]
    est = vmem_estimate(tm)
    if add is not None:
        in_specs.append(pl.BlockSpec((tm, tn), lambda i, j, k: (i, j)))
        args.append(add)
    return pl.pallas_call(
        body, name=name, grid=(M // tm, N // tn, nk), in_specs=in_specs,
        out_specs=pl.BlockSpec((tm, tn), lambda i, j, k: (i, j)),
        out_shape=jax.ShapeDtypeStruct((M, N), out_dtype),
        scratch_shapes=[pltpu.VMEM((tm, tn), F32)] if nk > 1 else [],
        compiler_params=_params(("parallel", "parallel", "arbitrary"), est),
    )(*args)


def _row(arr, bw=None, cb=0, grp=False, diff=True):
    return dict(arr=arr, bw=arr.shape[1] if bw is None else bw, cb=cb, grp=grp, diff=diff)


def _out(width, dtype, bw=None, grp=False):
    return dict(width=width, dtype=dtype, bw=width if bw is None else bw, grp=grp)


def _spec_rows(d, tm):
    return pl.BlockSpec((tm, d["bw"]), lambda g, i, cb=d["cb"], gr=d["grp"]: (i, cb + (g if gr else 0)))


def _spec_const(d):
    return pl.BlockSpec((d["arr"].shape[0], d["bw"]), lambda g, i, cb=d["cb"], gr=d["grp"]: (0, cb + (g if gr else 0)))


def _rw_fwd(fn, rows, consts, outs, name, ng=1):
    Tp = rows[0]["arr"].shape[0]
    tm = _pick_rows(Tp, 4 * (sum(d["bw"] for d in rows) + 2 * sum(o["bw"] for o in outs)))
    nr, ncst = len(rows), len(consts)

    def body(*refs):
        i = pl.program_id(1)
        rowidx = i * tm + lax.broadcasted_iota(jnp.int32, (tm, 1), 0)
        rv = [r[...].astype(F32) for r in refs[:nr]]
        cv = [c[...] for c in refs[nr:nr + ncst]]
        vals = fn(rowidx, *rv, *cv)
        for o, v in zip(refs[nr + ncst:], vals):
            o[...] = v.astype(o.dtype)

    est = sum(tm * d["bw"] * 4 for d in rows) + sum(tm * o["bw"] * 4 for o in outs)
    return pl.pallas_call(
        body, name=name, grid=(ng, Tp // tm),
        in_specs=[_spec_rows(d, tm) for d in rows] + [_spec_const(d) for d in consts],
        out_specs=[pl.BlockSpec((tm, o["bw"]), lambda g, i, gr=o["grp"]: (i, g if gr else 0)) for o in outs],
        out_shape=[jax.ShapeDtypeStruct((Tp, o["width"]), o["dtype"]) for o in outs],
        compiler_params=_params(("parallel", "parallel"), 3 * est),
    )(*[d["arr"] for d in rows], *[d["arr"] for d in consts])


def _rw_bwd(fn, rows, consts, cots, drow_dtypes, name, ng=1):
    Tp = rows[0]["arr"].shape[0]
    tm = _pick_rows(Tp, 4 * (3 * sum(d["bw"] for d in rows) + 2 * sum(d["bw"] for d in cots)))
    nr, ncst, nct = len(rows), len(consts), len(cots)
    drows = [k for k, d in enumerate(rows) if d["diff"]]
    dcsts = [k for k, d in enumerate(consts) if d["diff"]]
    for k in drows:
        assert rows[k]["grp"] or ng == 1

    def body(*refs):
        g = pl.program_id(0)
        i = pl.program_id(1)
        rowidx = i * tm + lax.broadcasted_iota(jnp.int32, (tm, 1), 0)
        rv = [r[...].astype(F32) for r in refs[:nr]]
        cv = [c[...] for c in refs[nr:nr + ncst]]
        ct = tuple(r[...].astype(F32) for r in refs[nr + ncst:nr + ncst + nct])
        orefs = refs[nr + ncst + nct:]

        def f(*dargs):
            rr, cc = list(rv), list(cv)
            for k, v in zip(drows, dargs[:len(drows)]):
                rr[k] = v
            for k, v in zip(dcsts, dargs[len(drows):]):
                cc[k] = v
            return tuple(fn(rowidx, *rr, *cc))

        _, vjp = jax.vjp(f, *[rv[k] for k in drows], *[cv[k] for k in dcsts])
        grads = vjp(ct)
        for o, v in zip(orefs[:len(drows)], grads[:len(drows)]):
            o[...] = v.astype(o.dtype)
        for k, o, v in zip(dcsts, orefs[len(drows):], grads[len(drows):]):
            first = (i == 0) if consts[k]["grp"] else ((i == 0) & (g == 0))

            @pl.when(first)
            def _(o=o, v=v):
                o[...] = v

            @pl.when(jnp.logical_not(first))
            def _(o=o, v=v):
                o[...] += v

    out_specs, out_shape = [], []
    for k, dt in zip(drows, drow_dtypes):
        d = rows[k]
        out_specs.append(pl.BlockSpec((tm, d["bw"]), lambda g, i, gr=d["grp"]: (i, g if gr else 0)))
        out_shape.append(jax.ShapeDtypeStruct((Tp, d["bw"] * (ng if d["grp"] else 1)), dt))
    for k in dcsts:
        d = consts[k]
        r = d["arr"].shape[0]
        out_specs.append(pl.BlockSpec((r, d["bw"]), lambda g, i, gr=d["grp"]: (0, g if gr else 0)))
        out_shape.append(jax.ShapeDtypeStruct((r, d["bw"] * (ng if d["grp"] else 1)), F32))
    est = sum(tm * d["bw"] * 4 for d in rows) * 2 + sum(tm * d["bw"] * 4 for d in cots)
    res = pl.pallas_call(
        body, name=name, grid=(ng, Tp // tm),
        in_specs=[_spec_rows(d, tm) for d in rows] + [_spec_const(d) for d in consts] + [_spec_rows(d, tm) for d in cots],
        out_specs=out_specs, out_shape=out_shape,
        compiler_params=_params(("arbitrary", "arbitrary"), 3 * est),
    )(*[d["arr"] for d in rows], *[d["arr"] for d in consts], *[d["arr"] for d in cots])
    return list(res[:len(drows)]), list(res[len(drows):])


def _sigmoid(x):
    return 0.5 * jnp.tanh(0.5 * x) + 0.5


def _silu(x):
    return x * _sigmoid(x)


def _softplus(x):
    return jnp.maximum(x, 0.0) + jnp.log(1.0 + jnp.exp(-jnp.abs(x)))


def _layer_norm(x, g, b):
    mu = jnp.mean(x, axis=-1, keepdims=True)
    xc = x - mu
    var = jnp.mean(xc * xc, axis=-1, keepdims=True)
    return xc * lax.rsqrt(var + LN_EPS) * g + b


def _rms_norm(x, g):
    return x * lax.rsqrt(jnp.mean(x * x, axis=-1, keepdims=True) + RMS_EPS) * g


def _rope(r, cos, sin, rot):
    return r * cos + jnp.dot(r, rot, precision=HIGHEST, preferred_element_type=F32) * sin


def _make_stage_fns(npad):
    def fn_ln_masked(rowidx, x, g, b):
        return (jnp.where(rowidx >= npad, _layer_norm(x, g, b), 0.0),)

    def fn_in_post(rowidx, ql, kvl, kpe, dtr, cos, sin, rot, qg, kvg, dtb):
        qn = _rms_norm(ql, qg)
        kvn = _rms_norm(kvl, kvg)
        kr = _rope(kpe, cos, sin, rot)
        lane = lax.broadcasted_iota(jnp.int32, (1, LANES), 1)
        dt = jnp.where((rowidx >= npad) & (lane < SSD_HEADS), _softplus(dtr + dtb), 0.0)
        return qn, kvn, jnp.concatenate([kr] * HEADS, axis=1), dt

    def fn_q_post(rowidx, q, cos, sin, rot):
        rr = _rope(q[:, QK_NOPE:], cos, sin, rot)
        return (jnp.concatenate([q[:, :QK_NOPE], rr], axis=1) * ATT_SCALE,)

    def fn_gated_norm(rowidx, y, xs, z, dskip, g):
        v = (y + xs * dskip) * _silu(z)
        return (v * lax.rsqrt(jnp.mean(v * v, axis=-1, keepdims=True) + RMS_EPS) * g,)

    def fn_mix(rowidx, ga, gs, ya, ys):
        return (_sigmoid(ga) * ya + _sigmoid(gs) * ys,)

    def fn_res_ln(rowidx, h, r, g, b):
        return (jnp.where(rowidx >= npad, _layer_norm(ALPHA * h + r, g, b), 0.0),)

    def fn_glu(rowidx, u):
        return (_silu(u[:, :D_FF]) * u[:, D_FF:],)

    return dict(ln=fn_ln_masked, in_post=fn_in_post, q_post=fn_q_post, gated=fn_gated_norm, mix=fn_mix,
                res_ln=fn_res_ln, glu=fn_glu)


def _conv_tiles(Tp, C):
    return _pick(Tp, ROW_TILE), _pick(C, MM_COL_TILE)


def _conv_fwd(x, xoff, C, w8, b, K, act, npad, name, out_dtype=F32):
    Tp = x.shape[0]
    tm, tc = _conv_tiles(Tp, C)
    assert xoff % tc == 0
    cb0 = xoff // tc
    rb = tm // HALO

    def body(prev_ref, main_ref, w_ref, b_ref, o_ref):
        i = pl.program_id(1)
        main = main_ref[...].astype(F32)
        prev = jnp.where(i > 0, prev_ref[...].astype(F32), 0.0)
        ext = jnp.concatenate([prev, main], axis=0)
        acc = b_ref[...] + w_ref[K - 1:K, :] * main
        for k in range(K - 1):
            s = K - 1 - k
            acc = acc + w_ref[k:k + 1, :] * pltpu.roll(ext, s, 0)[HALO:, :]
        if act:
            rowidx = i * tm + lax.broadcasted_iota(jnp.int32, (tm, 1), 0)
            acc = jnp.where(rowidx >= npad, _silu(acc), 0.0)
        o_ref[...] = acc.astype(o_ref.dtype)

    return pl.pallas_call(
        body, name=name, grid=(C // tc, Tp // tm),
        in_specs=[pl.BlockSpec((HALO, tc), lambda g, i: (jnp.maximum(i * rb - 1, 0), cb0 + g)),
                  pl.BlockSpec((tm, tc), lambda g, i: (i, cb0 + g)),
                  pl.BlockSpec((SUBLANES, tc), lambda g, i: (0, g)),
                  pl.BlockSpec((1, tc), lambda g, i: (0, g))],
        out_specs=pl.BlockSpec((tm, tc), lambda g, i: (i, g)),
        out_shape=jax.ShapeDtypeStruct((Tp, C), out_dtype),
        compiler_params=_params(("parallel", "parallel"), 8 * tm * tc * 4),
    )(x, x, w8, b)


def _conv_bwd(x, xoff, C, w8, b, dy, K, act, npad, name):
    Tp = x.shape[0]
    tm, tc = _conv_tiles(Tp, C)
    cb0 = xoff // tc
    rb = tm // HALO
    ni = Tp // tm
    last_rb = Tp // HALO - 1
    n = tm + 2 * HALO

    def body(xp_ref, xm_ref, xn_ref, dym_ref, dyn_ref, w_ref, b_ref, dx_ref, dw_ref, db_ref):
        i = pl.program_id(1)
        prev = jnp.where(i > 0, xp_ref[...].astype(F32), 0.0)
        ext = jnp.concatenate([prev, xm_ref[...].astype(F32), xn_ref[...].astype(F32)], axis=0)
        dyn = jnp.where(i < ni - 1, dyn_ref[...].astype(F32), 0.0)
        dpre = jnp.concatenate([jnp.zeros((HALO, tc), F32), dym_ref[...].astype(F32), dyn], axis=0)
        shifted = [ext if k == K - 1 else pltpu.roll(ext, K - 1 - k, 0) for k in range(K)]
        if act:
            pre = b_ref[...] + sum(w_ref[k:k + 1, :] * shifted[k] for k in range(K))
            rowidx = i * tm - HALO + lax.broadcasted_iota(jnp.int32, (n, 1), 0)
            sg = _sigmoid(pre)
            dpre = jnp.where(rowidx >= npad, dpre * sg * (1.0 + pre * (1.0 - sg)), 0.0)
        dx = w_ref[K - 1:K, :] * dpre
        for k in range(K - 1):
            dx = dx + w_ref[k:k + 1, :] * pltpu.roll(dpre, n - (K - 1 - k), 0)
        dx_ref[...] = dx[HALO:HALO + tm, :].astype(dx_ref.dtype)

        @pl.when(i == 0)
        def _():
            dw_ref[...] = jnp.zeros_like(dw_ref)
            db_ref[...] = jnp.zeros_like(db_ref)

        dmain = dpre[HALO:HALO + tm, :]
        for k in range(K):
            dw_ref[k:k + 1, :] += jnp.sum(dmain * shifted[k][HALO:HALO + tm, :], axis=0, keepdims=True)
        db_ref[...] += jnp.sum(dmain, axis=0, keepdims=True)

    return pl.pallas_call(
        body, name=name, grid=(C // tc, ni),
        in_specs=[pl.BlockSpec((HALO, tc), lambda g, i: (jnp.maximum(i * rb - 1, 0), cb0 + g)),
                  pl.BlockSpec((tm, tc), lambda g, i: (i, cb0 + g)),
                  pl.BlockSpec((HALO, tc), lambda g, i: (jnp.minimum((i + 1) * rb, last_rb), cb0 + g)),
                  pl.BlockSpec((tm, tc), lambda g, i: (i, g)),
                  pl.BlockSpec((HALO, tc), lambda g, i: (jnp.minimum((i + 1) * rb, last_rb), g)),
                  pl.BlockSpec((SUBLANES, tc), lambda g, i: (0, g)),
                  pl.BlockSpec((1, tc), lambda g, i: (0, g))],
        out_specs=[pl.BlockSpec((tm, tc), lambda g, i: (i, g)),
                   pl.BlockSpec((SUBLANES, tc), lambda g, i: (0, g)),
                   pl.BlockSpec((1, tc), lambda g, i: (0, g))],
        out_shape=[jax.ShapeDtypeStruct((Tp, C), BF16), jax.ShapeDtypeStruct((SUBLANES, C), F32),
                   jax.ShapeDtypeStruct((1, C), F32)],
        compiler_params=_params(("parallel", "arbitrary"), 14 * tm * tc * 4),
    )(x, x, x, dy, dy, w8, b)


def _split_refs(refs, n_in, n_out, n_scratch, nbg):
    cuts = np.cumsum([0, n_in, nbg, n_out, nbg, n_scratch])
    return tuple(refs[a:b] for a, b in zip(cuts[:-1], cuts[1:])) + (refs[cuts[-1]:],)


def _flash_fwd(q, kv, kr8, npad, name, bg=None):
    Tp = q.shape[0]
    t = _pick(Tp, ATT_TILE)
    hp = ATT_HEADS_PER_STEP
    nb = Tp // t
    ng = HEADS // hp
    nbg = bg.n if bg else 0
    nt = (((1,), (1,)), ((), ()))
    tn = (((0,), (0,)), ((), ()))

    def body(*refs):
        (q_ref, kn_ref, kr_ref, v_ref), bg_in, (o_ref, lse_ref), bg_out, (m_sc, l_sc, acc_sc), bg_sems = _split_refs(
            refs, 4, 2, 3, nbg)
        g = pl.program_id(0)
        qi = pl.program_id(1)
        ki = pl.program_id(2)
        if bg:
            @pl.when((g == 0) & (qi == 0) & (ki == 0))
            def _():
                bg.start(bg_in, bg_out, bg_sems)

        @pl.when(ki == 0)
        def _():
            m_sc[...] = jnp.full_like(m_sc, NEG_INF)
            l_sc[...] = jnp.zeros_like(l_sc)
            acc_sc[...] = jnp.zeros_like(acc_sc)

        def step(masked):
            kr = kr_ref[...]
            if masked:
                key = ki * t + lax.broadcasted_iota(jnp.int32, (t, t), 0)
                qry = qi * t + lax.broadcasted_iota(jnp.int32, (t, t), 1)
                visible = (key <= qry) & (key >= npad)
            for hh in range(hp):
                k = jnp.concatenate([kn_ref[:, hh * QK_NOPE:(hh + 1) * QK_NOPE], kr], axis=1)
                st = lax.dot_general(k, q_ref[:, hh * QHEAD:(hh + 1) * QHEAD], nt, preferred_element_type=F32)
                if masked:
                    st = jnp.where(visible, st, NEG_INF)
                vs = slice(hh * V_HEAD, (hh + 1) * V_HEAD)
                m_prev = m_sc[hh]
                m_new = jnp.maximum(m_prev, jnp.max(st, axis=0, keepdims=True))
                pt = jnp.exp(st - m_new)
                a = jnp.exp(m_prev - m_new)
                l_sc[hh] = a * l_sc[hh] + jnp.sum(pt, axis=0, keepdims=True)
                acc_sc[vs, :] = a * acc_sc[vs, :] + lax.dot_general(v_ref[:, vs], pt.astype(BF16), tn,
                                                                    preferred_element_type=F32)
                m_sc[hh] = m_new

        need_mask = (ki == qi) | (ki == 0)

        @pl.when((ki <= qi) & need_mask)
        def _():
            step(True)

        @pl.when((ki <= qi) & jnp.logical_not(need_mask))
        def _():
            step(False)

        @pl.when(ki == qi)
        def _():
            for hh in range(hp):
                vs = slice(hh * V_HEAD, (hh + 1) * V_HEAD)
                l = l_sc[hh]
                o_ref[:, vs] = (acc_sc[vs, :] / l).T.astype(o_ref.dtype)
                lse_ref[hh * SUBLANES:(hh + 1) * SUBLANES, :] = jnp.broadcast_to(m_sc[hh] + jnp.log(l), (SUBLANES, t))

        if bg:
            @pl.when((g == ng - 1) & (qi == nb - 1) & (ki == nb - 1))
            def _():
                bg.wait(bg_in, bg_out, bg_sems)

    kmin = lambda qi, ki: jnp.minimum(ki, qi)
    return pl.pallas_call(
        body, name=name, grid=(ng, nb, nb),
        in_specs=[pl.BlockSpec((t, hp * QHEAD), lambda g, qi, ki: (qi, g)),
                  pl.BlockSpec((t, hp * QK_NOPE), lambda g, qi, ki: (kmin(qi, ki), g)),
                  pl.BlockSpec((t, LANES), lambda g, qi, ki: (kmin(qi, ki), 0)),
                  pl.BlockSpec((t, hp * V_HEAD), lambda g, qi, ki: (kmin(qi, ki), ng + g))] + (bg.specs if bg else []),
        out_specs=[pl.BlockSpec((t, hp * V_HEAD), lambda g, qi, ki: (qi, g)),
                   pl.BlockSpec((hp * SUBLANES, t), lambda g, qi, ki: (g, qi))] + (bg.specs if bg else []),
        out_shape=[jax.ShapeDtypeStruct((Tp, HEADS * V_HEAD), F32), jax.ShapeDtypeStruct((HEADS * SUBLANES, Tp), F32)]
        + (bg.out_shape if bg else []),
        scratch_shapes=[pltpu.VMEM((hp, 1, t), F32), pltpu.VMEM((hp, 1, t), F32), pltpu.VMEM((hp * V_HEAD, t), F32)]
        + (bg.scratch if bg else []),
        compiler_params=_params(("arbitrary",) * 3 if bg else ("parallel", "parallel", "arbitrary"), 8 * hp * t * t * 4),
    )(q, kv, kr8, kv, *(bg.arrs if bg else []))


def _attn_delta(do, o, name):
    Tp = do.shape[0]
    tm = _pick(Tp, MM_TOKEN_K_TILE)

    def body(do_ref, o_ref, d_ref):
        prod = do_ref[...] * o_ref[...]
        ones = jnp.ones((SUBLANES, V_HEAD), F32)
        d_ref[...] = lax.dot_general(ones, prod, (((1,), (1,)), ((), ())), precision=HIGHEST,
                                     preferred_element_type=F32)

    return pl.pallas_call(
        body, name=name, grid=(HEADS, Tp // tm),
        in_specs=[pl.BlockSpec((tm, V_HEAD), lambda h, i: (i, h)), pl.BlockSpec((tm, V_HEAD), lambda h, i: (i, h))],
        out_specs=pl.BlockSpec((SUBLANES, tm), lambda h, i: (h, i)),
        out_shape=jax.ShapeDtypeStruct((HEADS * SUBLANES, Tp), F32),
        compiler_params=_params(("parallel", "parallel"), 4 * tm * V_HEAD * 4),
    )(do, o)


def _flash_bwd(q, kv, kr8, do, lse, delta, npad, name, bg=None):
    Tp = q.shape[0]
    t = _pick(Tp, ATT_TILE)
    nb = Tp // t
    nbg = bg.n if bg else 0
    nt = (((1,), (1,)), ((), ()))
    tn = (((0,), (0,)), ((), ()))

    def body(*refs):
        ((q_ref, kn_ref, kr_ref, v_ref, do_ref, lse_ref, dl_ref), bg_in, (dq_ref, dkn_ref, dkr_ref, dv_ref), bg_out,
         (dk_sc, dv_sc), bg_sems) = _split_refs(refs, 7, 4, 2, nbg)
        h = pl.program_id(0)
        ki = pl.program_id(1)
        qi = pl.program_id(2)
        if bg:
            @pl.when((h == 0) & (ki == 0) & (qi == 0))
            def _():
                bg.start(bg_in, bg_out, bg_sems)

        @pl.when(qi == 0)
        def _():
            dk_sc[...] = jnp.zeros_like(dk_sc)
            dv_sc[...] = jnp.zeros_like(dv_sc)

        def step(masked):
            qv = q_ref[...]
            k = jnp.concatenate([kn_ref[...], kr_ref[...]], axis=1)
            st = lax.dot_general(k, qv, nt, preferred_element_type=F32)
            if masked:
                key = ki * t + lax.broadcasted_iota(jnp.int32, (t, t), 0)
                qry = qi * t + lax.broadcasted_iota(jnp.int32, (t, t), 1)
                st = jnp.where((key <= qry) & (key >= npad), st, NEG_INF)
            pt = jnp.exp(st - lse_ref[0:1, :])
            dob = do_ref[...].astype(BF16)
            dv_sc[...] += jnp.dot(pt.astype(BF16), dob, preferred_element_type=F32)
            dpt = lax.dot_general(v_ref[...], dob, nt, preferred_element_type=F32)
            dst = (pt * (dpt - dl_ref[0:1, :])).astype(BF16)
            dk_sc[...] += jnp.dot(dst, qv, preferred_element_type=F32)
            dqc = lax.dot_general(dst, k, tn, preferred_element_type=F32)
            rows = pl.ds(pl.multiple_of(qi * t, t), t)

            @pl.when(ki == 0)
            def _():
                dq_ref[rows, :] = dqc

            @pl.when(ki > 0)
            def _():
                dq_ref[rows, :] += dqc

        need_mask = (ki == qi) | (ki == 0)

        @pl.when((qi >= ki) & need_mask)
        def _():
            step(True)

        @pl.when((qi >= ki) & jnp.logical_not(need_mask))
        def _():
            step(False)

        @pl.when(qi == nb - 1)
        def _():
            dkn_ref[...] = dk_sc[:, :QK_NOPE].astype(dkn_ref.dtype)
            dkr_ref[...] = dk_sc[:, QK_NOPE:].astype(dkr_ref.dtype)
            dv_ref[...] = dv_sc[...].astype(dv_ref.dtype)

        if bg:
            @pl.when((h == HEADS - 1) & (ki == nb - 1) & (qi == nb - 1))
            def _():
                bg.wait(bg_in, bg_out, bg_sems)

    qmap = lambda h, ki, qi: (jnp.maximum(qi, ki), h)
    kmap = lambda h, ki, qi: (ki, h)
    est = 2 * Tp * QHEAD * 4 + 8 * t * t * 4
    return pl.pallas_call(
        body, name=name, grid=(HEADS, nb, nb),
        in_specs=[pl.BlockSpec((t, QHEAD), qmap),
                  pl.BlockSpec((t, QK_NOPE), kmap),
                  pl.BlockSpec((t, LANES), kmap),
                  pl.BlockSpec((t, V_HEAD), lambda h, ki, qi: (ki, HEADS + h)),
                  pl.BlockSpec((t, V_HEAD), qmap),
                  pl.BlockSpec((SUBLANES, t), lambda h, ki, qi: (h, jnp.maximum(qi, ki))),
                  pl.BlockSpec((SUBLANES, t), lambda h, ki, qi: (h, jnp.maximum(qi, ki)))] + (bg.specs if bg else []),
        out_specs=[pl.BlockSpec((Tp, QHEAD), lambda h, ki, qi: (0, h)),
                   pl.BlockSpec((t, QK_NOPE), kmap),
                   pl.BlockSpec((t, LANES), kmap),
                   pl.BlockSpec((t, V_HEAD), kmap)] + (bg.specs if bg else []),
        out_shape=[jax.ShapeDtypeStruct((Tp, HEADS * QHEAD), F32),
                   jax.ShapeDtypeStruct((Tp, HEADS * QK_NOPE), BF16),
                   jax.ShapeDtypeStruct((Tp, HEADS * LANES), F32),
                   jax.ShapeDtypeStruct((Tp, HEADS * V_HEAD), BF16)] + (bg.out_shape if bg else []),
        scratch_shapes=[pltpu.VMEM((t, QHEAD), F32), pltpu.VMEM((t, V_HEAD), F32)] + (bg.scratch if bg else []),
        compiler_params=_params(("arbitrary",) * 3 if bg else ("parallel", "arbitrary", "arbitrary"), est),
    )(q, kv, kr8, kv, do, lse, delta, *(bg.arrs if bg else []))


GW = SSD_INNER // SSD_GROUPS
PAIRS_PER_GROUP = GW // LANES
XB = SSD_INNER // GW
NT_DIMS = (((1,), (1,)), ((), ()))
TN_DIMS = (((0,), (0,)), ((), ()))


def _ssd_common(xs_ref, dt_ref, alog_ref, e_ref):
    a_neg = -jnp.exp(alog_ref[...])
    dt = dt_ref[...]
    li = lax.broadcasted_iota(jnp.int32, (CHUNK, CHUNK), 0)
    si = lax.broadcasted_iota(jnp.int32, (CHUNK, CHUNK), 1)
    tril = li >= si
    tri = tril.astype(F32)
    acs = jnp.dot(tri, dt * a_neg, precision=SSD_PREC, preferred_element_type=F32)
    e = e_ref[...]
    dte = jnp.dot(dt, e, precision=SSD_PREC, preferred_element_type=F32)
    acse = jnp.dot(acs, e, precision=SSD_PREC, preferred_element_type=F32)
    x = xs_ref[...] * dte
    alast = acse[CHUNK - 1:CHUNK, :]
    return dict(a_neg=a_neg, dt=dt, tril=tril, tri=tri, acs=acs, acs_t=acs.T, e=e, dte=dte, acse=acse, x=x,
                p_e=jnp.exp(acse), w_e=jnp.exp(alast - acse), dl_e=jnp.exp(alast), li=li, si=si)


def _decay(cm, head):
    col = cm["acs"][:, head:head + 1]
    row = cm["acs_t"][head:head + 1, :]
    return jnp.exp(jnp.where(cm["tril"], col - row, -jnp.inf))


def _ssd_fwd(xbc, dt, alog, e, name):
    Tp = xbc.shape[0]
    nc = Tp // CHUNK

    def body(xs_ref, b_ref, c_ref, dt_ref, alog_ref, e_ref, y_ref, st_ref, st_sc):
        @pl.when(pl.program_id(0) == 0)
        def _():
            st_sc[...] = jnp.zeros_like(st_sc)

        cm = _ssd_common(xs_ref, dt_ref, alog_ref, e_ref)
        st_ref[0] = st_sc[...]
        lane = lax.broadcasted_iota(jnp.int32, (CHUNK, LANES), 1)
        for g in range(SSD_GROUPS):
            gs = slice(g * GW, (g + 1) * GW)
            cg = c_ref[:, g * SSD_STATE:(g + 1) * SSD_STATE].astype(BF16)
            bg = b_ref[:, g * SSD_STATE:(g + 1) * SSD_STATE].astype(BF16)
            cb = lax.dot_general(cg, bg, NT_DIMS, preferred_element_type=F32)
            stg = st_sc[:, gs]
            yoff = jnp.dot(cg, stg.astype(BF16), preferred_element_type=F32) * cm["p_e"][:, gs]
            xg = cm["x"][:, gs]
            for jp in range(PAIRS_PER_GROUP):
                j = g * PAIRS_PER_GROUP + jp
                xp = xg[:, jp * LANES:(jp + 1) * LANES].astype(BF16)
                ys = []
                for head in (2 * j, 2 * j + 1):
                    m = (cb * _decay(cm, head)).astype(BF16)
                    ys.append(jnp.dot(m, xp, preferred_element_type=F32))
                y_ref[:, j * LANES:(j + 1) * LANES] = (jnp.where(lane < SSD_HEAD_DIM, ys[0], ys[1])
                                                       + yoff[:, jp * LANES:(jp + 1) * LANES])
            snew = lax.dot_general(bg, (cm["w_e"][:, gs] * xg).astype(BF16), TN_DIMS, preferred_element_type=F32)
            st_sc[:, gs] = cm["dl_e"][:, gs] * stg + snew

    return pl.pallas_call(
        body, name=name, grid=(nc,),
        in_specs=[pl.BlockSpec((CHUNK, SSD_INNER), lambda c: (c, 0)),
                  pl.BlockSpec((CHUNK, GW), lambda c: (c, XB)),
                  pl.BlockSpec((CHUNK, GW), lambda c: (c, XB + 1)),
                  pl.BlockSpec((CHUNK, LANES), lambda c: (c, 0)),
                  pl.BlockSpec((1, LANES), lambda c: (0, 0)),
                  pl.BlockSpec((LANES, SSD_INNER), lambda c: (0, 0))],
        out_specs=[pl.BlockSpec((CHUNK, SSD_INNER), lambda c: (c, 0)),
                   pl.BlockSpec((1, SSD_STATE, SSD_INNER), lambda c: (c, 0, 0))],
        out_shape=[jax.ShapeDtypeStruct((Tp, SSD_INNER), F32), jax.ShapeDtypeStruct((nc, SSD_STATE, SSD_INNER), F32)],
        scratch_shapes=[pltpu.VMEM((SSD_STATE, SSD_INNER), F32)],
        compiler_params=_params(("arbitrary",), 24 * CHUNK * SSD_INNER * 4),
    )(xbc, xbc, xbc, dt, alog, e)


def _ssd_bwd(xbc, dt, alog, e, dy, dxs_skip, states, name):
    Tp = xbc.shape[0]
    nc = Tp // CHUNK
    rev = lambda c: nc - 1 - c

    def body(xs_ref, b_ref, c_ref, dt_ref, alog_ref, e_ref, dy_ref, skip_ref, st_ref,
             dxbc_ref, ddt_ref, dalog_ref, dst_sc, dx_sc, t_sc, tw_sc):
        @pl.when(pl.program_id(0) == 0)
        def _():
            dst_sc[...] = jnp.zeros_like(dst_sc)
            dalog_ref[...] = jnp.zeros_like(dalog_ref)

        cm = _ssd_common(xs_ref, dt_ref, alog_ref, e_ref)
        lane = lax.broadcasted_iota(jnp.int32, (CHUNK, LANES), 1)
        dacs_col = jnp.zeros((CHUNK, LANES), F32)
        dacs_row = jnp.zeros((LANES, CHUNK), F32)
        t_last = []
        for g in range(SSD_GROUPS):
            gs = slice(g * GW, (g + 1) * GW)
            cg = c_ref[:, g * SSD_STATE:(g + 1) * SSD_STATE].astype(BF16)
            bg = b_ref[:, g * SSD_STATE:(g + 1) * SSD_STATE].astype(BF16)
            stg = st_ref[0, :, gs]
            stg_b = stg.astype(BF16)
            dstg = dst_sc[:, gs]
            dstg_b = dstg.astype(BF16)
            xg = cm["x"][:, gs]
            dyg = dy_ref[:, gs]
            zg = jnp.dot(cg, stg_b, preferred_element_type=F32)
            dzg = dyg * cm["p_e"][:, gs]
            dzg_b = dzg.astype(BF16)
            dcg = lax.dot_general(dzg_b, stg_b, NT_DIMS, preferred_element_type=F32)
            dst_in = lax.dot_general(cg, dzg_b, TN_DIMS, preferred_element_type=F32)
            dst_in = dst_in + cm["dl_e"][:, gs] * dstg
            t_last.append(jnp.sum(dstg * stg * cm["dl_e"][:, gs], axis=0, keepdims=True))
            weg = cm["w_e"][:, gs]
            dbg = lax.dot_general((weg * xg).astype(BF16), dstg_b, NT_DIMS, preferred_element_type=F32)
            gg = jnp.dot(bg, dstg_b, preferred_element_type=F32)
            dxg = weg * gg
            tw_sc[:, gs] = xg * dxg
            t_sc[:, gs] = dzg * zg - xg * dxg
            cb = lax.dot_general(cg, bg, NT_DIMS, preferred_element_type=F32)
            dcb = jnp.zeros((CHUNK, CHUNK), F32)
            for jp in range(PAIRS_PER_GROUP):
                j = g * PAIRS_PER_GROUP + jp
                ps = slice(jp * LANES, (jp + 1) * LANES)
                xp = xg[:, ps].astype(BF16)
                dyp = dyg[:, ps]
                dxp = dxg[:, ps]
                for half, head in enumerate((2 * j, 2 * j + 1)):
                    lam = _decay(cm, head)
                    m32 = cb * lam
                    sel = (lane < SSD_HEAD_DIM) if half == 0 else (lane >= SSD_HEAD_DIM)
                    dye = jnp.where(sel, dyp, 0.0).astype(BF16)
                    dm = lax.dot_general(dye, xp, NT_DIMS, preferred_element_type=F32)
                    w = dm * m32
                    dacs_col = dacs_col + jnp.where(cm["si"] == head, jnp.sum(w, axis=1, keepdims=True), 0.0)
                    dacs_row = dacs_row + jnp.where(cm["li"] == head, jnp.sum(w, axis=0, keepdims=True), 0.0)
                    dcb = dcb + dm * lam
                    dxp = dxp + lax.dot_general(m32.astype(BF16), dye, TN_DIMS, preferred_element_type=F32)
                dx_sc[:, j * LANES:(j + 1) * LANES] = dxp
            dcb_b = dcb.astype(BF16)
            dcg = dcg + jnp.dot(dcb_b, bg, preferred_element_type=F32)
            dbg = dbg + lax.dot_general(dcb_b, cg, TN_DIMS, preferred_element_type=F32)
            dst_sc[:, gs] = dst_in
            dxbc_ref[:, SSD_INNER + g * SSD_STATE:SSD_INNER + (g + 1) * SSD_STATE] = dbg
            dxbc_ref[:, SSD_INNER + GW + g * SSD_STATE:SSD_INNER + GW + (g + 1) * SSD_STATE] = dcg
        e = cm["e"]
        dacs = lax.dot_general(t_sc[...], e, NT_DIMS, precision=SSD_PREC, preferred_element_type=F32)
        dacs = dacs + dacs_col - dacs_row.T
        last_lane = jnp.concatenate(t_last, axis=1) + jnp.sum(tw_sc[...], axis=0, keepdims=True)
        last_head = lax.dot_general(jnp.broadcast_to(last_lane, (SUBLANES, SSD_INNER)), e, NT_DIMS,
                                    precision=SSD_PREC, preferred_element_type=F32)[0:1, :]
        dacs = dacs + jnp.where(cm["li"] == CHUNK - 1, last_head, 0.0)
        da = lax.dot_general(cm["tri"], dacs, TN_DIMS, precision=SSD_PREC, preferred_element_type=F32)
        dx_all = dx_sc[...]
        ddt = da * cm["a_neg"] + lax.dot_general(dx_all * xs_ref[...], e, NT_DIMS, precision=SSD_PREC,
                                                 preferred_element_type=F32)
        ddt_ref[...] = ddt
        dxbc_ref[:, :SSD_INNER] = dx_all * cm["dte"] + skip_ref[...]
        dalog_ref[0:1, :] += jnp.sum(da * cm["dt"], axis=0, keepdims=True) * cm["a_neg"]

    return pl.pallas_call(
        body, name=name, grid=(nc,),
        in_specs=[pl.BlockSpec((CHUNK, SSD_INNER), lambda c: (rev(c), 0)),
                  pl.BlockSpec((CHUNK, GW), lambda c: (rev(c), XB)),
                  pl.BlockSpec((CHUNK, GW), lambda c: (rev(c), XB + 1)),
                  pl.BlockSpec((CHUNK, LANES), lambda c: (rev(c), 0)),
                  pl.BlockSpec((1, LANES), lambda c: (0, 0)),
                  pl.BlockSpec((LANES, SSD_INNER), lambda c: (0, 0)),
                  pl.BlockSpec((CHUNK, SSD_INNER), lambda c: (rev(c), 0)),
                  pl.BlockSpec((CHUNK, SSD_INNER), lambda c: (rev(c), 0)),
                  pl.BlockSpec((1, SSD_STATE, SSD_INNER), lambda c: (rev(c), 0, 0))],
        out_specs=[pl.BlockSpec((CHUNK, SSD_CONV_DIM), lambda c: (rev(c), 0)),
                   pl.BlockSpec((CHUNK, LANES), lambda c: (rev(c), 0)),
                   pl.BlockSpec((SUBLANES, LANES), lambda c: (0, 0))],
        out_shape=[jax.ShapeDtypeStruct((Tp, SSD_CONV_DIM), F32), jax.ShapeDtypeStruct((Tp, LANES), F32),
                   jax.ShapeDtypeStruct((SUBLANES, LANES), F32)],
        scratch_shapes=[pltpu.VMEM((SSD_STATE, SSD_INNER), F32), pltpu.VMEM((CHUNK, SSD_INNER), F32),
                        pltpu.VMEM((CHUNK, SSD_INNER), F32), pltpu.VMEM((CHUNK, SSD_INNER), F32)],
        compiler_params=_params(("arbitrary",), 32 * CHUNK * SSD_INNER * 4),
    )(xbc, xbc, xbc, dt, alog, e, dy, dxs_skip, states)


def _loss_head(h, target, name):
    Tp, d = h.shape
    nt = Tp // LANES

    def body(h_ref, t_ref, dh_ref, l_ref):
        real = pl.program_id(0) > 0
        err = jnp.where(real, h_ref[...] - t_ref[...], 0.0)
        dh_ref[...] = err * (1.0 / d)
        l_ref[...] = jnp.broadcast_to(0.5 * jnp.sum(err * err) * (1.0 / d), l_ref.shape)

    return pl.pallas_call(
        body, name=name, grid=(nt,),
        in_specs=[pl.BlockSpec((LANES, d), lambda i: (i, 0)),
                  pl.BlockSpec((LANES, d), lambda i: (jnp.maximum(i - 1, 0), 0))],
        out_specs=[pl.BlockSpec((LANES, d), lambda i: (i, 0)),
                   pl.BlockSpec((1, SUBLANES, LANES), lambda i: (i, 0, 0))],
        out_shape=[jax.ShapeDtypeStruct((Tp, d), F32), jax.ShapeDtypeStruct((nt, SUBLANES, LANES), F32)],
        compiler_params=_params(("parallel",), 8 * LANES * d * 4),
    )(h, target)


def _adamw(parts, w, m, v, name):
    shape = w.shape
    C = shape[-1]
    R = int(np.prod(shape[:-1]))
    npart = parts.shape[0]
    parts, w, m, v = parts.reshape(npart, R, C), w.reshape(R, C), m.reshape(R, C), v.reshape(R, C)
    lanes = -(-C // LANES) * LANES
    tr = _pick(R, max(BF16_ROWS, ADAM_ELEMS // lanes), BF16_ROWS) if R % BF16_ROWS == 0 else R
    c1 = 1.0 / (1.0 - ADAM_B1 ** ADAM_STEP)
    c2 = 1.0 / (1.0 - ADAM_B2 ** ADAM_STEP)

    def body(p_ref, w_ref, m_ref, v_ref, g_out, d_out, m_out, v_out):
        g = p_ref[0].astype(F32)
        for p in range(1, npart):
            g = g + p_ref[p].astype(F32)
        m_new = ADAM_B1 * m_ref[...] + (1.0 - ADAM_B1) * g
        v_new = ADAM_B2 * v_ref[...] + (1.0 - ADAM_B2) * (g * g)
        g_out[...] = g
        m_out[...] = m_new
        v_out[...] = v_new
        d_out[...] = -ADAM_LR * ((m_new * c1) / (jnp.sqrt(v_new * c2) + ADAM_EPS) + ADAM_WD * w_ref[...])

    spec = pl.BlockSpec((tr, C), lambda i: (i, 0))
    est = npart * tr * lanes * parts.dtype.itemsize + 7 * tr * lanes * 4
    res = pl.pallas_call(
        body, name=name, grid=(R // tr,),
        in_specs=[pl.BlockSpec((npart, tr, C), lambda i: (0, i, 0)), spec, spec, spec],
        out_specs=[spec] * 4, out_shape=[jax.ShapeDtypeStruct((R, C), F32)] * 4,
        compiler_params=_params(("parallel",), est),
    )(parts, w, m, v)
    return [r.reshape(shape) for r in res]


MESH_ID = pl.DeviceIdType.MESH
N_PEERS = N_DEV - 1


def _dev_index(p):
    return 4 * p[0] + 2 * p[1] + p[2]


class _Background:
    def __init__(self, kind, arrs):
        self.kind, self.arrs, self.n = kind, list(arrs), len(arrs)
        self.npairs = N_PEERS if kind == "gather" else N_CHIPS - 1
        lead = (N_DEV,) if kind == "gather" else ()
        self.out_shape = [jax.ShapeDtypeStruct(lead + a.shape, a.dtype) for a in self.arrs]
        self.specs = [pl.BlockSpec(memory_space=pl.ANY)] * self.n
        self.scratch = [pltpu.SemaphoreType.DMA((self.n, self.npairs)), pltpu.SemaphoreType.DMA((self.n, self.npairs)),
                        pltpu.SemaphoreType.DMA((self.n,))]

    def copies(self, in_refs, out_refs, sems):
        send_sems, recv_sems, local_sems = sems
        x, y, c = lax.axis_index("x"), lax.axis_index("y"), lax.axis_index("c")
        sends, recvs, locals_ = [], [], []

        def remote(t, k, src, dst, to):
            return pltpu.make_async_remote_copy(src_ref=src, dst_ref=dst, send_sem=send_sems.at[t, k],
                                                recv_sem=recv_sems.at[t, k], device_id=to, device_id_type=MESH_ID)

        if self.kind == "gather":
            me = _dev_index((x, y, c))
            peers = [(x, y, 1 - c), (1 - x, y, c), (x, 1 - y, c), (1 - x, 1 - y, c),
                     (1 - x, y, 1 - c), (x, 1 - y, 1 - c), (1 - x, 1 - y, 1 - c)]
            for t in range(self.n):
                locals_.append(pltpu.make_async_copy(in_refs[t], out_refs[t].at[me], local_sems.at[t]))
                for k, p in enumerate(peers):
                    sends.append(remote(t, k, in_refs[t], out_refs[t].at[me], p))
                    recvs.append(remote(t, k, in_refs[t], out_refs[t].at[_dev_index(p)], p))
        else:
            mine = 2 * x + y
            peers = [(1 - x, y), (x, 1 - y), (1 - x, 1 - y)]
            for t in range(self.n):
                locals_.append(pltpu.make_async_copy(in_refs[t].at[mine], out_refs[t].at[mine], local_sems.at[t]))
                for k, p in enumerate(peers):
                    theirs = 2 * p[0] + p[1]
                    sends.append(remote(t, k, in_refs[t].at[theirs], out_refs[t].at[mine], (*p, c)))
                    recvs.append(remote(t, k, in_refs[t].at[mine], out_refs[t].at[theirs], (*p, c)))
        return sends, recvs, locals_

    def start(self, in_refs, out_refs, sems):
        sends, _, locals_ = self.copies(in_refs, out_refs, sems)
        for cp in locals_ + sends:
            cp.start()

    def wait(self, in_refs, out_refs, sems):
        sends, recvs, locals_ = self.copies(in_refs, out_refs, sems)
        for cp in recvs:
            cp.wait_recv()
        for cp in sends:
            cp.wait_send()
        for cp in locals_:
            cp.wait()


def _comm_call(body, name, arrs, out_shape, npairs):
    n = len(arrs)
    any_spec = pl.BlockSpec(memory_space=pl.ANY)
    return pl.pallas_call(
        functools.partial(body, n), name=name, in_specs=[any_spec] * n, out_specs=[any_spec] * n, out_shape=out_shape,
        scratch_shapes=[pltpu.SemaphoreType.DMA((n, npairs)), pltpu.SemaphoreType.DMA((n, npairs)),
                        pltpu.SemaphoreType.DMA((n,))],
    )(*arrs)


def _allgather(arrs, name):
    def body(n, *refs):
        src_refs, out_refs = refs[:n], refs[n:2 * n]
        send_sems, recv_sems, local_sems = refs[2 * n:]
        x, y, c = lax.axis_index("x"), lax.axis_index("y"), lax.axis_index("c")
        me, sibling = (x, y, c), (x, y, 1 - c)
        chips = [(1 - x, y), (x, 1 - y), (1 - x, 1 - y)]

        def copy(t, k, block, to, src=None):
            slot = out_refs[t].at[_dev_index(block)]
            return pltpu.make_async_remote_copy(
                src_ref=slot if src is None else src, dst_ref=slot,
                send_sem=send_sems.at[t, k], recv_sem=recv_sems.at[t, k],
                device_id=to, device_id_type=MESH_ID)

        sends, locals_ = [], []
        for t in range(n):
            mine = pltpu.make_async_copy(src_refs[t], out_refs[t].at[_dev_index(me)], local_sems.at[t])
            mine.start()
            locals_.append(mine)
            first = [copy(t, 0, me, sibling, src=src_refs[t])]
            first += [copy(t, 1 + j, me, (*chip, c), src=src_refs[t]) for j, chip in enumerate(chips)]
            for cp in first:
                cp.start()
            sends += first
        for j, chip in enumerate(chips):
            for t in range(n):
                copy(t, 1 + j, (*chip, c), me).wait_recv()
                passed = copy(t, 4 + j, (*chip, c), sibling)
                passed.start()
                sends.append(passed)
        for t in range(n):
            copy(t, 0, sibling, me).wait_recv()
            for j, chip in enumerate(chips):
                copy(t, 4 + j, (*chip, 1 - c), me).wait_recv()
        for cp in sends:
            cp.wait_send()
        for cp in locals_:
            cp.wait()

    return _comm_call(body, name, arrs, [jax.ShapeDtypeStruct((N_DEV,) + a.shape, a.dtype) for a in arrs], N_PEERS)


N_CHIPS = N_DEV // 2
CHIPS = [(0, 0), (0, 1), (1, 0), (1, 1)]


def _sibling_exchange(arrs, name):
    def body(n, *refs):
        in_refs, out_refs = refs[:n], refs[n:2 * n]
        send_sems, recv_sems, _ = refs[2 * n:]
        x, y, c = lax.axis_index("x"), lax.axis_index("y"), lax.axis_index("c")
        sibling = (x, y, 1 - c)

        def copy(t, j):
            return pltpu.make_async_remote_copy(
                src_ref=in_refs[t].at[_dev_index((*CHIPS[j], 1 - c))], dst_ref=out_refs[t].at[j],
                send_sem=send_sems.at[t, j], recv_sem=recv_sems.at[t, j],
                device_id=sibling, device_id_type=MESH_ID)

        copies = [copy(t, j) for t in range(n) for j in range(N_CHIPS)]
        for cp in copies:
            cp.start()
        for cp in copies:
            cp.wait_recv()
        for cp in copies:
            cp.wait_send()

    return _comm_call(body, name, arrs, [jax.ShapeDtypeStruct((N_CHIPS,) + a.shape[1:], a.dtype) for a in arrs], N_CHIPS)


def _chip_exchange(arrs, name):
    def body(n, *refs):
        in_refs, out_refs = refs[:n], refs[n:2 * n]
        send_sems, recv_sems, local_sems = refs[2 * n:]
        x, y, c = lax.axis_index("x"), lax.axis_index("y"), lax.axis_index("c")
        mine = 2 * x + y
        peers = [(1 - x, y), (x, 1 - y), (1 - x, 1 - y)]

        def copy(t, k, src_chip, dst_chip, to):
            return pltpu.make_async_remote_copy(
                src_ref=in_refs[t].at[src_chip], dst_ref=out_refs[t].at[dst_chip],
                send_sem=send_sems.at[t, k], recv_sem=recv_sems.at[t, k],
                device_id=(*to, c), device_id_type=MESH_ID)

        sends, locals_ = [], []
        for t in range(n):
            own = pltpu.make_async_copy(in_refs[t].at[mine], out_refs[t].at[mine], local_sems.at[t])
            own.start()
            locals_.append(own)
            for k, p in enumerate(peers):
                cp = copy(t, k, 2 * p[0] + p[1], mine, p)
                cp.start()
                sends.append(cp)
        for t in range(n):
            for k, p in enumerate(peers):
                copy(t, k, mine, 2 * p[0] + p[1], p).wait_recv()
        for cp in sends:
            cp.wait_send()
        for cp in locals_:
            cp.wait()

    return _comm_call(body, name, arrs, [jax.ShapeDtypeStruct(a.shape, a.dtype) for a in arrs], N_CHIPS - 1)


def _add_pairs(a, b, name):
    shape = a.shape
    C = shape[-1]
    R = int(np.prod(shape[:-1]))
    lanes = -(-C // LANES) * LANES
    tr = _pick(R, max(BF16_ROWS, 2 * ADAM_ELEMS // lanes), BF16_ROWS) if R % BF16_ROWS == 0 else R

    def body(a_ref, b_ref, o_ref):
        o_ref[...] = (a_ref[...].astype(F32) + b_ref[...].astype(F32)).astype(o_ref.dtype)

    spec = pl.BlockSpec((tr, C), lambda i: (i, 0))
    return pl.pallas_call(
        body, name=name, grid=(R // tr,), in_specs=[spec, spec], out_specs=spec,
        out_shape=jax.ShapeDtypeStruct((R, C), a.dtype),
        compiler_params=_params(("parallel",), 3 * tr * lanes * 4),
    )(a.reshape(R, C), b.reshape(R, C)).reshape(shape)


WEIGHTS = ['meta_tokens', 'emb_ln_g', 'emb_ln_b', 'w_in', 'q_norm_g', 'w_q_b', 'kv_norm_g', 'w_kv_b', 'w_o_attn',
           'ssd_conv_w', 'ssd_conv_b', 'dt_bias', 'a_log', 'd_skip', 'ssd_norm_g', 'w_o_ssd', 'w_out', 'ln1_g',
           'ln1_b', 'w_up', 'ffn_conv_w', 'ffn_conv_b', 'w_down', 'ln2_g', 'ln2_b']
BIG = {'w_in': 2, 'w_q_b': 2, 'w_kv_b': 2, 'w_o_attn': 1, 'w_o_ssd': 1, 'w_out': 1, 'w_up': 2, 'w_down': 1}
SMALL_SHARDED = {'meta_tokens': 1, 'ssd_conv_w': 2, 'ffn_conv_w': 2}
REPLICATED = [n for n in WEIGHTS if n not in BIG and n not in SMALL_SHARDED]
FIRST_USED = ['w_in', 'w_q_b', 'w_kv_b']
AFTER_ATTENTION = [n for n in BIG if n not in FIRST_USED]
BIG_COLS = 1024
SMALL_COLS = LANES


def _flatten(arrs, cols, row_mult, lead=False):
    parts, offs, off = [], [], 0
    for a in arrs:
        a2 = a.reshape(N_DEV, -1) if lead else a.reshape(1, -1)
        n = a2.shape[1]
        pad = -n % cols
        parts.append(jnp.pad(a2, ((0, 0), (0, pad))))
        offs.append((off, n))
        off += n + pad
    rows = off // cols
    extra = (-rows % row_mult) * cols
    if extra:
        parts.append(jnp.zeros((parts[0].shape[0], extra), parts[0].dtype))
    flat = jnp.concatenate(parts, axis=1)
    flat = flat.reshape(flat.shape[0], -1, cols)
    return (flat if lead else flat[0]), offs


def _unflatten(flat, offs, shapes):
    f = flat.reshape(-1)
    return [f[o:o + n].reshape(s) for (o, n), s in zip(offs, shapes)]


def _to_pieces(g, axis):
    s = g.shape[axis] // N_DEV
    g = g.reshape(g.shape[:axis] + (N_DEV, s) + g.shape[axis + 1:])
    return jnp.moveaxis(g, axis, 0).reshape(N_DEV, -1)


def _from_pieces(p, shard_shape, axis):
    g = jnp.moveaxis(p.reshape((N_DEV,) + tuple(shard_shape)), 0, axis)
    sh = list(shard_shape)
    sh[axis] *= N_DEV
    return g.reshape(sh)


def _in_proj_pad(w):
    e = np.cumsum((0,) + IN_SIZES)
    ql, kvl, kpe, z, xbc, dt, ga, gs = [w[:, e[j]:e[j + 1]] for j in range(8)]
    zc = lambda n: jnp.zeros((w.shape[0], n), w.dtype)
    return jnp.concatenate([ql, kvl, z, xbc, ga, gs, kpe, zc(LANES - QK_ROPE), dt, zc(LANES - SSD_HEADS)], axis=1)


def _in_proj_unpad(d):
    seg = lambda o, n: d[:, o:o + n]
    return jnp.concatenate([seg(OQ, Q_LORA), seg(OKV, KV_LORA), seg(OKPE, QK_ROPE), seg(OZ, SSD_INNER),
                            seg(OXBC, SSD_CONV_DIM), seg(ODT, SSD_HEADS), seg(OGA, D_MODEL), seg(OGS, D_MODEL)], axis=1)


def _q_pad(w):
    w3 = w.reshape(Q_LORA, HEADS, QK_NOPE + QK_ROPE)
    return jnp.concatenate([w3, jnp.zeros((Q_LORA, HEADS, QHEAD - QK_NOPE - QK_ROPE), w.dtype)], axis=2).reshape(Q_LORA, HEADS * QHEAD)


def _q_unpad(d):
    return d.reshape(Q_LORA, HEADS, QHEAD)[:, :, :QK_NOPE + QK_ROPE].reshape(Q_LORA, HEADS * (QK_NOPE + QK_ROPE))


def _kv_perm(w):
    w3 = w.reshape(KV_LORA, HEADS, QK_NOPE + V_HEAD)
    return jnp.concatenate([w3[:, :, :QK_NOPE].reshape(KV_LORA, -1), w3[:, :, QK_NOPE:].reshape(KV_LORA, -1)], axis=1)


def _kv_unperm(d):
    kn = d[:, :HEADS * QK_NOPE].reshape(KV_LORA, HEADS, QK_NOPE)
    v = d[:, HEADS * QK_NOPE:].reshape(KV_LORA, HEADS, V_HEAD)
    return jnp.concatenate([kn, v], axis=2).reshape(KV_LORA, HEADS * (QK_NOPE + V_HEAD))


def _row_vec(v, width=None):
    v = v.reshape(1, -1).astype(F32)
    if width is not None and v.shape[1] < width:
        v = jnp.pad(v, ((0, 0), (0, width - v.shape[1])))
    return v


def _pad_rows8(w):
    return jnp.pad(w.astype(F32), ((0, SUBLANES - w.shape[0]), (0, 0)))


def _tables(Tp, npad):
    pos = jnp.maximum(jnp.arange(Tp, dtype=jnp.int32) - npad, 0).astype(F32)
    inv_freq = 1.0 / (ROPE_THETA ** (jnp.arange(0, QK_ROPE, 2, dtype=F32) / QK_ROPE))
    ang = pos[:, None] * inv_freq[None, :]
    ang = jnp.concatenate([ang, ang], axis=-1)
    zeros = jnp.zeros((Tp, LANES - QK_ROPE), F32)
    cos = jnp.concatenate([jnp.cos(ang), zeros], axis=1)
    sin = jnp.concatenate([jnp.sin(ang), zeros], axis=1)
    rot = np.zeros((LANES, LANES), np.float32)
    half = QK_ROPE // 2
    for i in range(half):
        rot[i + half, i] = -1.0
        rot[i, i + half] = 1.0
    expand = np.zeros((LANES, SSD_INNER), np.float32)
    for hd in range(SSD_HEADS):
        expand[hd, hd * SSD_HEAD_DIM:(hd + 1) * SSD_HEAD_DIM] = 1.0
    return cos, sin, jnp.asarray(rot), jnp.asarray(expand)


def _layer_rows(proj, tb):
    rows_a = [_row(proj, Q_LORA, OQ // Q_LORA), _row(proj, KV_LORA, OKV // KV_LORA), _row(proj, LANES, OKPE // LANES),
              _row(proj, LANES, ODT // LANES), _row(tb["cos"], diff=False), _row(tb["sin"], diff=False)]
    return rows_a


def _layer_fwd(h, h_bf, P, tb, fns, npad, bg=None, on_carried=None):
    both = [_out(D_MODEL, F32), _out(D_MODEL, BF16)]
    res_ln_twice = lambda *a: fns["res_ln"](*a) * 2
    proj = _mm(h_bf, P["w_in"], F32, "in_proj")
    rows_a = _layer_rows(proj, tb)
    consts_a = [_row(tb["rot"], diff=False), _row(P["q_norm_g"]), _row(P["kv_norm_g"]), _row(P["dt_bias"])]
    qn, kvn, kr8, dt = _rw_fwd(fns["in_post"], rows_a, consts_a,
                               [_out(Q_LORA, BF16), _out(KV_LORA, BF16), _out(HEADS * LANES, BF16), _out(LANES, F32)],
                               "in_post")
    q = _mm(qn, P["w_q"], F32, "q_proj")
    rows_q = [_row(q, QHEAD, 0, grp=True), _row(tb["cos"], diff=False), _row(tb["sin"], diff=False)]
    qr = _rw_fwd(fns["q_post"], rows_q, [_row(tb["rot"], diff=False)], [_out(HEADS * QHEAD, BF16, QHEAD, grp=True)],
                 "q_post", ng=HEADS)[0]
    kv = _mm(kvn, P["w_kv"], BF16, "kv_proj")
    o, lse, *carried = _flash_fwd(qr, kv, kr8, npad, "attn_fwd_gather" if bg else "attn_fwd", bg=bg)
    if on_carried is not None:
        carried = on_carried(P, carried)
    ya = _mm(o, P["w_o_attn"], F32, "attn_out")
    xbc = _conv_fwd(proj, OXBC, SSD_CONV_DIM, P["ssd_conv_w"], P["ssd_conv_b"], SSD_CONV, True, npad, "ssd_conv")
    y, states = _ssd_fwd(xbc, dt, P["a_log"], tb["expand"], "ssd_fwd")
    rows_b = [_row(y, GW, 0, grp=True), _row(xbc, GW, 0, grp=True), _row(proj, GW, OZ // GW, grp=True)]
    consts_b = [_row(P["d_skip"], GW, 0, grp=True), _row(P["ssd_norm_g"], GW, 0, grp=True)]
    yn = _rw_fwd(fns["gated"], rows_b, consts_b, [_out(SSD_INNER, BF16, GW, grp=True)], "ssd_gate", ng=SSD_GROUPS)[0]
    ys = _mm(yn, P["w_o_ssd"], F32, "ssd_out")
    rows_c = [_row(proj, D_MODEL, OGA // D_MODEL), _row(proj, D_MODEL, OGS // D_MODEL), _row(ya), _row(ys)]
    mixed = _rw_fwd(fns["mix"], rows_c, [], [_out(D_MODEL, BF16)], "mix")[0]
    mo = _mm(mixed, P["w_out"], F32, "mix_out")
    consts_1 = [_row(P["ln1_g"]), _row(P["ln1_b"])]
    h1, h1_bf = _rw_fwd(res_ln_twice, [_row(h), _row(mo)], consts_1, both, "ln1")
    up = _mm(h1_bf, P["w_up"], BF16, "ffn_up")
    u = _conv_fwd(up, 0, 2 * D_FF, P["ffn_conv_w"], P["ffn_conv_b"], FFN_CONV, False, npad, "ffn_conv", BF16)
    act = _rw_fwd(fns["glu"], [_row(u)], [], [_out(D_FF, BF16)], "ffn_glu")[0]
    fo = _mm(act, P["w_down"], F32, "ffn_down")
    consts_2 = [_row(P["ln2_g"]), _row(P["ln2_b"])]
    h2, h2_bf = _rw_fwd(res_ln_twice, [_row(h1), _row(fo)], consts_2, both, "ln2")
    res = dict(h=h, h_bf=h_bf, proj=proj, qn=qn, kvn=kvn, kr8=kr8, dt=dt, q=q, qr=qr, kv=kv, o=o, lse=lse, ya=ya,
               xbc=xbc, y=y, states=states, yn=yn, ys=ys, mixed=mixed, mo=mo, h1=h1, h1_bf=h1_bf, up=up, u=u, act=act,
               fo=fo)
    return h2, h2_bf, res, carried


def _layer_bwd(dh2, r, P, tb, fns, npad, bg=None, before_attn=None):
    g = {}
    consts_2 = [_row(P["ln2_g"]), _row(P["ln2_b"])]
    (dh1_a, dfo), (g["ln2_g"], g["ln2_b"]) = _rw_bwd(fns["res_ln"], [_row(r["h1"]), _row(r["fo"])], consts_2,
                                                     [_row(dh2)], [F32, BF16], "ln2_bwd")
    g["w_down"] = _mm(r["act"], dfo, BF16, "dw_down", ta=True)
    dact = _mm(dfo, P["w_down"], BF16, "d_act", tb=True)
    (du,), _ = _rw_bwd(fns["glu"], [_row(r["u"])], [], [_row(dact)], [BF16], "glu_bwd")
    dup, g["ffn_conv_w"], g["ffn_conv_b"] = _conv_bwd(r["up"], 0, 2 * D_FF, P["ffn_conv_w"], P["ffn_conv_b"], du,
                                                      FFN_CONV, False, npad, "ffn_conv_bwd")
    g["w_up"] = _mm(r["h1_bf"], dup, BF16, "dw_up", ta=True)
    dh1 = _mm(dup, P["w_up"], F32, "d_h1", tb=True, add=dh1_a)
    consts_1 = [_row(P["ln1_g"]), _row(P["ln1_b"])]
    (dh_a, dmo), (g["ln1_g"], g["ln1_b"]) = _rw_bwd(fns["res_ln"], [_row(r["h"]), _row(r["mo"])], consts_1,
                                                    [_row(dh1)], [F32, BF16], "ln1_bwd")
    g["w_out"] = _mm(r["mixed"], dmo, BF16, "dw_out", ta=True)
    dmixed = _mm(dmo, P["w_out"], F32, "d_mixed", tb=True)
    proj = r["proj"]
    rows_c = [_row(proj, D_MODEL, OGA // D_MODEL), _row(proj, D_MODEL, OGS // D_MODEL), _row(r["ya"]), _row(r["ys"])]
    (dga, dgs, dya, dys), _ = _rw_bwd(fns["mix"], rows_c, [], [_row(dmixed)], [BF16] * 4, "mix_bwd")
    g["w_o_attn"] = _mm(r["o"], dya, BF16, "dw_o_attn", ta=True)
    do = _mm(dya, P["w_o_attn"], F32, "d_o", tb=True)
    g["w_o_ssd"] = _mm(r["yn"], dys, BF16, "dw_o_ssd", ta=True)
    dyn = _mm(dys, P["w_o_ssd"], F32, "d_yn", tb=True)
    rows_b = [_row(r["y"], GW, 0, grp=True), _row(r["xbc"], GW, 0, grp=True), _row(proj, GW, OZ // GW, grp=True)]
    consts_b = [_row(P["d_skip"], GW, 0, grp=True), _row(P["ssd_norm_g"], GW, 0, grp=True)]
    (dy, dxs_skip, dz), (g["d_skip"], g["ssd_norm_g"]) = _rw_bwd(
        fns["gated"], rows_b, consts_b, [_row(dyn, GW, 0, grp=True)], [F32, F32, BF16], "ssd_gate_bwd", ng=SSD_GROUPS)
    dxbc, ddt, g["a_log"] = _ssd_bwd(r["xbc"], r["dt"], P["a_log"], tb["expand"], dy, dxs_skip, r["states"], "ssd_bwd")
    dxbc_pre, g["ssd_conv_w"], g["ssd_conv_b"] = _conv_bwd(proj, OXBC, SSD_CONV_DIM, P["ssd_conv_w"], P["ssd_conv_b"],
                                                           dxbc, SSD_CONV, True, npad, "ssd_conv_bwd")
    delta = _attn_delta(do, r["o"], "attn_delta")
    if before_attn is not None:
        bg = before_attn(g)
    dqr, dkn, dkr8, dv, *carried = _flash_bwd(r["qr"], r["kv"], r["kr8"], do, r["lse"], delta, npad,
                                              "attn_bwd_exchange" if bg else "attn_bwd", bg=bg)
    rows_q = [_row(r["q"], QHEAD, 0, grp=True), _row(tb["cos"], diff=False), _row(tb["sin"], diff=False)]
    (dq,), _ = _rw_bwd(fns["q_post"], rows_q, [_row(tb["rot"], diff=False)], [_row(dqr, QHEAD, 0, grp=True)], [BF16],
                       "q_post_bwd", ng=HEADS)
    g["w_q"] = _mm(r["qn"], dq, BF16, "dw_q", ta=True)
    dqn = _mm(dq, P["w_q"], F32, "d_qn", tb=True)
    dkv = jnp.concatenate([dkn, dv], axis=1)
    g["w_kv"] = _mm(r["kvn"], dkv, BF16, "dw_kv", ta=True)
    dkvn = _mm(dkv, P["w_kv"], F32, "d_kvn", tb=True)
    rows_a = _layer_rows(proj, tb)
    consts_a = [_row(tb["rot"], diff=False), _row(P["q_norm_g"]), _row(P["kv_norm_g"]), _row(P["dt_bias"])]
    (dql, dkvl, dkpe, ddtr), (g["q_norm_g"], g["kv_norm_g"], g["dt_bias"]) = _rw_bwd(
        fns["in_post"], rows_a, consts_a, [_row(dqn), _row(dkvn), _row(dkr8), _row(ddt)], [BF16] * 4, "in_post_bwd")
    dproj = jnp.concatenate([dql, dkvl, dz, dxbc_pre, dga, dgs, dkpe, ddtr], axis=1)
    g["w_in"] = _mm(r["h_bf"], dproj, BF16, "dw_in", ta=True)
    dh = _mm(dproj, P["w_in"], F32, "d_h", tb=True, add=dh_a)
    return dh, g, carried


def _full_weight(g, axis):
    if axis == 1:
        return g.reshape(-1, g.shape[-1])
    return jnp.concatenate([g[p] for p in range(N_DEV)], axis=1)


def _grad_pieces(d, axis):
    if axis == 1:
        return d.reshape(N_DEV, -1, d.shape[1])
    return jnp.transpose(d.reshape(d.shape[0], N_DEV, -1), (1, 0, 2))


def _big_params(gathered):
    prep = {"w_in": ("w_in", _in_proj_pad), "w_q_b": ("w_q", _q_pad), "w_kv_b": ("w_kv", _kv_perm)}
    P = {}
    for n, g in gathered.items():
        key, fn = prep.get(n, (n, lambda a: a))
        P[key] = fn(_full_weight(g, BIG[n]))
    return P


def _layer_params(gathered, small, i):
    P = _big_params(gathered)
    P["q_norm_g"] = _row_vec(small["q_norm_g"][i])
    P["kv_norm_g"] = _row_vec(small["kv_norm_g"][i])
    P["dt_bias"] = _row_vec(small["dt_bias"][i], LANES)
    P["a_log"] = _row_vec(small["a_log"][i], LANES)
    P["d_skip"] = _row_vec(jnp.repeat(small["d_skip"][i], SSD_HEAD_DIM))
    P["ssd_norm_g"] = _row_vec(small["ssd_norm_g"][i])
    P["ssd_conv_w"] = _pad_rows8(small["ssd_conv_w"][i])
    P["ssd_conv_b"] = _row_vec(small["ssd_conv_b"][i])
    P["ffn_conv_w"] = _pad_rows8(small["ffn_conv_w"][i])
    P["ffn_conv_b"] = _row_vec(small["ffn_conv_b"][i])
    for n in ("ln1_g", "ln1_b", "ln2_g", "ln2_b"):
        P[n] = _row_vec(small[n][i])
    return P


def _layer_grads_to_reference_layout(g):
    out = {}
    out["w_in"] = _in_proj_unpad(g["w_in"])
    out["w_q_b"] = _q_unpad(g["w_q"])
    out["w_kv_b"] = _kv_unperm(g["w_kv"])
    for n in ("w_o_attn", "w_o_ssd", "w_out", "w_up", "w_down"):
        out[n] = g[n]
    out["q_norm_g"] = g["q_norm_g"][0]
    out["kv_norm_g"] = g["kv_norm_g"][0]
    out["dt_bias"] = g["dt_bias"][0, :SSD_HEADS]
    out["a_log"] = g["a_log"][0, :SSD_HEADS]
    out["d_skip"] = g["d_skip"].reshape(SSD_HEADS, SSD_HEAD_DIM).sum(axis=1)
    out["ssd_norm_g"] = g["ssd_norm_g"][0]
    out["ssd_conv_w"] = g["ssd_conv_w"][:SSD_CONV]
    out["ssd_conv_b"] = g["ssd_conv_b"][0]
    out["ffn_conv_w"] = g["ffn_conv_w"][:FFN_CONV]
    out["ffn_conv_b"] = g["ffn_conv_b"][0]
    for n in ("ln1_g", "ln1_b", "ln2_g", "ln2_b"):
        out[n] = g[n][0]
    return out


def kernel(x, meta_tokens, emb_ln_g, emb_ln_b, w_in, q_norm_g, w_q_b, kv_norm_g, w_kv_b, w_o_attn, ssd_conv_w, ssd_conv_b, dt_bias, a_log, d_skip, ssd_norm_g, w_o_ssd, w_out, ln1_g, ln1_b, w_up, ffn_conv_w, ffn_conv_b, w_down, ln2_g, ln2_b, loss_target, m_meta_tokens, m_emb_ln_g, m_emb_ln_b, m_w_in, m_q_norm_g, m_w_q_b, m_kv_norm_g, m_w_kv_b, m_w_o_attn, m_ssd_conv_w, m_ssd_conv_b, m_dt_bias, m_a_log, m_d_skip, m_ssd_norm_g, m_w_o_ssd, m_w_out, m_ln1_g, m_ln1_b, m_w_up, m_ffn_conv_w, m_ffn_conv_b, m_w_down, m_ln2_g, m_ln2_b, v_meta_tokens, v_emb_ln_g, v_emb_ln_b, v_w_in, v_q_norm_g, v_w_q_b, v_kv_norm_g, v_w_kv_b, v_w_o_attn, v_ssd_conv_w, v_ssd_conv_b, v_dt_bias, v_a_log, v_d_skip, v_ssd_norm_g, v_w_o_ssd, v_w_out, v_ln1_g, v_ln1_b, v_w_up, v_ffn_conv_w, v_ffn_conv_b, v_w_down, v_ln2_g, v_ln2_b):
    given = dict(locals())
    w = {n: given[n] for n in WEIGHTS}
    m = {n: given["m_" + n] for n in WEIGHTS}
    v = {n: given["v_" + n] for n in WEIGHTS}
    seq = x.shape[1]
    assert x.shape[0] == 1 and seq % LANES == 0
    npad = LANES - N_META
    Tp = npad + N_META + seq
    depth = w_in.shape[0]

    big_names, small_names = list(BIG), list(SMALL_SHARDED)
    ws, offs_s = _flatten([w[n] for n in small_names], SMALL_COLS, SUBLANES)
    shards = [{n: w[n][i].astype(BF16) for n in big_names} for i in range(depth)]
    got = _allgather([shards[0][n] for n in FIRST_USED] + [ws], "weight_allgather")
    gathered = dict(zip(FIRST_USED, got[:-1]))
    gsm = got[-1]
    small = {n: w[n] for n in REPLICATED}
    for n, (o, sz) in zip(small_names, offs_s):
        small[n] = _from_pieces(gsm.reshape(N_DEV, -1)[:, o:o + sz], w[n].shape, SMALL_SHARDED[n])

    fns = _make_stage_fns(npad)
    cos, sin, rot, expand = _tables(Tp, npad)
    tb = dict(cos=cos, sin=sin, rot=rot, expand=expand)
    top = jnp.pad(small["meta_tokens"], ((npad, 0), (0, 0)))
    hcat = jnp.concatenate([top, x[0]], axis=0)
    consts_e = [_row(_row_vec(w["emb_ln_g"])), _row(_row_vec(w["emb_ln_b"]))]
    h, h_bf = _rw_fwd(lambda *a: fns["ln"](*a) * 2, [_row(hcat)], consts_e, [_out(D_MODEL, F32), _out(D_MODEL, BF16)],
                      "emb_ln")
    layers, saved = [], []
    for i in range(depth):
        layers.append(_layer_params(gathered, small, i))
        late = AFTER_ATTENTION if i == 0 else []
        nxt = big_names if i + 1 < depth else []
        arrs = [shards[i][n] for n in late] + [shards[i + 1][n] for n in nxt]

        def on_carried(P, carried, late=late):
            P.update(_big_params(dict(zip(late, carried[:len(late)]))))
            return carried[len(late):]

        h, h_bf, res, carried = _layer_fwd(h, h_bf, layers[i], tb, fns, npad,
                                           bg=_Background("gather", arrs) if arrs else None, on_carried=on_carried)
        gathered = dict(zip(nxt, carried))
        saved.append(res)
    dh, lparts = _loss_head(h, loss_target[0], "loss_head")
    loss = lax.psum(jnp.sum(lparts[:, 0, 0]), ("x", "y", "c"))

    core = lax.axis_index("c")

    def chip_partials(pieces, tag):
        from_sibling = _sibling_exchange(pieces, "grad_exchange_cores_" + tag)
        sums = []
        for k, (p, r) in enumerate(zip(pieces, from_sibling)):
            own = lax.dynamic_index_in_dim(p.reshape((N_CHIPS, 2) + p.shape[1:]), core, axis=1, keepdims=False)
            sums.append(_add_pairs(own, r, "grad_chip_sum_%s_%d" % (tag, k)))
        return sums

    lg, recv_big, pending = [None] * depth, [None] * depth, []
    for i in reversed(range(depth)):
        early = AFTER_ATTENTION if i == 0 else []

        def before_attn(g, pending=pending, early=early, i=i):
            sums = pending + (chip_partials([_grad_pieces(g[n], BIG[n]) for n in early], "l%d_early" % i) if early else [])
            return _Background("chips", sums) if sums else None

        dh, gi, carried = _layer_bwd(dh, saved[i], layers[i], tb, fns, npad, before_attn=before_attn)
        if pending:
            recv_big[i + 1] = dict(zip(big_names, carried[:len(pending)]))
        recv_big[i] = dict(zip(early, carried[len(pending):]))
        lg[i] = _layer_grads_to_reference_layout(gi)
        pending = []
        if i > 0:
            pending = chip_partials([_grad_pieces(lg[i][n], BIG[n]) for n in big_names], "l%d" % i)
    (dhcat,), (d_emb_g, d_emb_b) = _rw_bwd(fns["ln"], [_row(hcat)], consts_e, [_row(dh)], [F32], "emb_ln_bwd")
    grad_x = dhcat[LANES:][None]
    local = {n: jnp.stack([lg[i][n] for i in range(depth)]) for n in lg[0] if n not in BIG}
    local["meta_tokens"] = dhcat[npad:LANES]
    local["emb_ln_g"] = d_emb_g[0]
    local["emb_ln_b"] = d_emb_b[0]

    sm_names = small_names + REPLICATED
    sm_pieces = [_to_pieces(local[n], SMALL_SHARDED[n]) for n in small_names]
    sm_pieces += [jnp.broadcast_to(local[n].reshape(1, -1), (N_DEV, local[n].size)) for n in REPLICATED]
    ps, _ = _flatten(sm_pieces, SMALL_COLS, BF16_ROWS, lead=True)
    pieces = [_grad_pieces(lg[0][n], BIG[n]) for n in FIRST_USED] + [ps]
    recv = _chip_exchange(chip_partials(pieces, "l0"), "grad_exchange_chips")
    recv_big[0].update(zip(FIRST_USED, recv[:-1]))
    outs = {}
    kinds = ("grad", "delta", "new_m", "new_v")
    for n in big_names:
        parts = jnp.stack([recv_big[i][n] for i in range(depth)], axis=1)
        for kind, a in zip(kinds, _adamw(parts, w[n], m[n], v[n], "adamw_" + n)):
            outs[kind + "_" + n] = a
    wf, offs = _flatten([w[n] for n in sm_names], SMALL_COLS, BF16_ROWS)
    mf, _ = _flatten([m[n] for n in sm_names], SMALL_COLS, BF16_ROWS)
    vf, _ = _flatten([v[n] for n in sm_names], SMALL_COLS, BF16_ROWS)
    shapes = [w[n].shape for n in sm_names]
    for kind, flat in zip(kinds, _adamw(recv[-1], wf, mf, vf, "adamw_small")):
        for n, a in zip(sm_names, _unflatten(flat, offs, shapes)):
            outs[kind + "_" + n] = a
    result = [loss, grad_x]
    for kind in ("grad", "delta", "new_m", "new_v"):
        result += [outs[kind + "_" + n] for n in WEIGHTS]
    return tuple(result)
```

```python
import functools

import jax
import jax.numpy as jnp
import numpy as np
from jax import lax
from jax.experimental import pallas as pl
from jax.experimental.pallas import tpu as pltpu

F32 = jnp.float32
BF16 = jnp.bfloat16
HIGHEST = lax.Precision.HIGHEST
SSD_PREC = lax.Precision.HIGH

D_MODEL = 1024
DEPTH = 2
N_META = 16
HEADS = 8
Q_LORA = 768
KV_LORA = 256
QK_NOPE = 128
QK_ROPE = 64
V_HEAD = 128
ROPE_THETA = 10000.0
SSD_INNER = 2048
SSD_HEAD_DIM = 64
SSD_HEADS = 32
SSD_GROUPS = 4
SSD_STATE = 128
SSD_CONV = 4
SSD_CONV_DIM = SSD_INNER + 2 * SSD_GROUPS * SSD_STATE
CHUNK = 128
D_FF = 2816
FFN_CONV = 3
LN_EPS = 1e-5
RMS_EPS = 1e-6
ALPHA = (2 * DEPTH) ** 0.25
IN_SIZES = (Q_LORA, KV_LORA, QK_ROPE, SSD_INNER, SSD_CONV_DIM, SSD_HEADS, D_MODEL, D_MODEL)
ATT_SCALE = (QK_NOPE + QK_ROPE) ** -0.5
NEG_INF = -1e30
ADAM_LR, ADAM_B1, ADAM_B2, ADAM_EPS, ADAM_WD, ADAM_STEP = 0.001, 0.9, 0.999, 1e-08, 0.01, 10

LANES = 128
SUBLANES = 8
VMEM_BYTES = 64 * 1024 * 1024
N_DEV = 8

OQ, OKV, OZ, OXBC, OGA, OGS, OKPE, ODT = 0, 768, 1024, 3072, 6144, 7168, 8192, 8320
IN_PAD = 8448
QHEAD = 256

ROW_TILE = 640
MM_COL_TILE = 1408
MM_ROW_TILE = 1664
MM_VMEM_BUDGET = 46 * 1024 * 1024
MM_K_TILE = 2816
MM_TOKEN_K_TILE = 1664
ATT_TILE = 640
ATT_HEADS_PER_STEP = 4
BF16_ROWS = 16
HALO = BF16_ROWS
ROW_BUDGET = 7 * 1024 * 1024
ADAM_ELEMS = 160 * 1024


def _pick(n, target, q=LANES):
    assert n % q == 0, (n, q)
    units = n // q
    best = q
    for d in range(1, units + 1):
        if units % d == 0 and d * q <= target:
            best = d * q
    return best


def _pick_rows(n, row_bytes):
    return _pick(n, max(BF16_ROWS, ROW_BUDGET // row_bytes), BF16_ROWS)


def _params(sem, est_bytes):
    limit = int(min(VMEM_BYTES - (6 << 20), max(32 << 20, 2 * est_bytes + (8 << 20))))
    return pltpu.CompilerParams(dimension_semantics=sem, vmem_limit_bytes=limit)


def _nbytes(shape, dtype):
    return int(np.prod(shape)) * jnp.dtype(dtype).itemsize


def _mm(a, b, out_dtype, name, ta=False, tb=False, add=None):
    assert not (ta and tb)
    if ta:
        K, M = a.shape
        tm = _pick(M, MM_COL_TILE)
        tk = _pick(K, MM_TOKEN_K_TILE)
    else:
        M, K = a.shape
        tk = _pick(K, MM_K_TILE)
    N, K2 = (b.shape if tb else b.shape[::-1])
    assert K == K2
    tn = _pick(N, MM_COL_TILE)
    nk = K // tk

    def vmem_estimate(tm):
        e = 2 * (tm * tk * a.dtype.itemsize + tk * tn * b.dtype.itemsize + tm * tn * jnp.dtype(out_dtype).itemsize)
        e += tm * tn * 4 + tm * tk * 2
        return e + (tm * tn * 4 if nk > 1 else 0) + (2 * tm * tn * 4 if add is not None else 0)

    if not ta:
        tm = _pick(M, MM_ROW_TILE, BF16_ROWS)
        while vmem_estimate(tm) > MM_VMEM_BUDGET and tm > BF16_ROWS:
            tm = _pick(M, tm - BF16_ROWS, BF16_ROWS)
    dn = (((0,), (0,)), ((), ())) if ta else ((((1,), (1,)), ((), ())) if tb else (((1,), (0,)), ((), ())))

    def body(*refs):
        a_ref, b_ref = refs[:2]
        add_ref = refs[2] if add is not None else None
        o_ref = refs[2 + (add is not None)]
        d = lax.dot_general(a_ref[...].astype(BF16), b_ref[...].astype(BF16), dn, preferred_element_type=F32)

        def finish(r):
            if add is not None:
                r = r + add_ref[...].astype(F32)
            o_ref[...] = r.astype(out_dtype)

        if nk == 1:
            finish(d)
            return
        acc = refs[-1]
        k = pl.program_id(2)

        @pl.when(k == 0)
        def _():
            acc[...] = d

        @pl.when((k > 0) & (k < nk - 1))
        def _():
            acc[...] += d

        @pl.when(k == nk - 1)
        def _():
            finish(acc[...] + d)

    if ta:
        a_spec = pl.BlockSpec((tk, tm), lambda i, j, k: (k, i))
    else:
        a_spec = pl.BlockSpec((tm, tk), lambda i, j, k: (i, k))
    b_spec = pl.BlockSpec((tn, tk), lambda i, j, k: (j, k)) if tb else pl.BlockSpec((tk, tn), lambda i, j, k: (k, j))
    in_specs = [a_spec, b_spec]
    args = [a, b]
    est = vmem_estimate(tm)
    if add is not None:
        in_specs.append(pl.BlockSpec((tm, tn), lambda i, j, k: (i, j)))
        args.append(add)
    return pl.pallas_call(
        body, name=name, grid=(M // tm, N // tn, nk), in_specs=in_specs,
        out_specs=pl.BlockSpec((tm, tn), lambda i, j, k: (i, j)),
        out_shape=jax.ShapeDtypeStruct((M, N), out_dtype),
        scratch_shapes=[pltpu.VMEM((tm, tn), F32)] if nk > 1 else [],
        compiler_params=_params(("parallel", "parallel", "arbitrary"), est),
    )(*args)


def _row(arr, bw=None, cb=0, grp=False, diff=True):
    return dict(arr=arr, bw=arr.shape[1] if bw is None else bw, cb=cb, grp=grp, diff=diff)


def _out(width, dtype, bw=None, grp=False):
    return dict(width=width, dtype=dtype, bw=width if bw is None else bw, grp=grp)


def _spec_rows(d, tm):
    return pl.BlockSpec((tm, d["bw"]), lambda g, i, cb=d["cb"], gr=d["grp"]: (i, cb + (g if gr else 0)))


def _spec_const(d):
    return pl.BlockSpec((d["arr"].shape[0], d["bw"]), lambda g, i, cb=d["cb"], gr=d["grp"]: (0, cb + (g if gr else 0)))


def _rw_fwd(fn, rows, consts, outs, name, ng=1):
    Tp = rows[0]["arr"].shape[0]
    tm = _pick_rows(Tp, 4 * (sum(d["bw"] for d in rows) + 2 * sum(o["bw"] for o in outs)))
    nr, ncst = len(rows), len(consts)

    def body(*refs):
        i = pl.program_id(1)
        rowidx = i * tm + lax.broadcasted_iota(jnp.int32, (tm, 1), 0)
        rv = [r[...].astype(F32) for r in refs[:nr]]
        cv = [c[...] for c in refs[nr:nr + ncst]]
        vals = fn(rowidx, *rv, *cv)
        for o, v in zip(refs[nr + ncst:], vals):
            o[...] = v.astype(o.dtype)

    est = sum(tm * d["bw"] * 4 for d in rows) + sum(tm * o["bw"] * 4 for o in outs)
    return pl.pallas_call(
        body, name=name, grid=(ng, Tp // tm),
        in_specs=[_spec_rows(d, tm) for d in rows] + [_spec_const(d) for d in consts],
        out_specs=[pl.BlockSpec((tm, o["bw"]), lambda g, i, gr=o["grp"]: (i, g if gr else 0)) for o in outs],
        out_shape=[jax.ShapeDtypeStruct((Tp, o["width"]), o["dtype"]) for o in outs],
        compiler_params=_params(("parallel", "parallel"), 3 * est),
    )(*[d["arr"] for d in rows], *[d["arr"] for d in consts])


def _rw_bwd(fn, rows, consts, cots, drow_dtypes, name, ng=1):
    Tp = rows[0]["arr"].shape[0]
    tm = _pick_rows(Tp, 4 * (3 * sum(d["bw"] for d in rows) + 2 * sum(d["bw"] for d in cots)))
    nr, ncst, nct = len(rows), len(consts), len(cots)
    drows = [k for k, d in enumerate(rows) if d["diff"]]
    dcsts = [k for k, d in enumerate(consts) if d["diff"]]
    for k in drows:
        assert rows[k]["grp"] or ng == 1

    def body(*refs):
        g = pl.program_id(0)
        i = pl.program_id(1)
        rowidx = i * tm + lax.broadcasted_iota(jnp.int32, (tm, 1), 0)
        rv = [r[...].astype(F32) for r in refs[:nr]]
        cv = [c[...] for c in refs[nr:nr + ncst]]
        ct = tuple(r[...].astype(F32) for r in refs[nr + ncst:nr + ncst + nct])
        orefs = refs[nr + ncst + nct:]

        def f(*dargs):
            rr, cc = list(rv), list(cv)
            for k, v in zip(drows, dargs[:len(drows)]):
                rr[k] = v
            for k, v in zip(dcsts, dargs[len(drows):]):
                cc[k] = v
            return tuple(fn(rowidx, *rr, *cc))

        _, vjp = jax.vjp(f, *[rv[k] for k in drows], *[cv[k] for k in dcsts])
        grads = vjp(ct)
        for o, v in zip(orefs[:len(drows)], grads[:len(drows)]):
            o[...] = v.astype(o.dtype)
        for k, o, v in zip(dcsts, orefs[len(drows):], grads[len(drows):]):
            first = (i == 0) if consts[k]["grp"] else ((i == 0) & (g == 0))

            @pl.when(first)
            def _(o=o, v=v):
                o[...] = v

            @pl.when(jnp.logical_not(first))
            def _(o=o, v=v):
                o[...] += v

    out_specs, out_shape = [], []
    for k, dt in zip(drows, drow_dtypes):
        d = rows[k]
        out_specs.append(pl.BlockSpec((tm, d["bw"]), lambda g, i, gr=d["grp"]: (i, g if gr else 0)))
        out_shape.append(jax.ShapeDtypeStruct((Tp, d["bw"] * (ng if d["grp"] else 1)), dt))
    for k in dcsts:
        d = consts[k]
        r = d["arr"].shape[0]
        out_specs.append(pl.BlockSpec((r, d["bw"]), lambda g, i, gr=d["grp"]: (0, g if gr else 0)))
        out_shape.append(jax.ShapeDtypeStruct((r, d["bw"] * (ng if d["grp"] else 1)), F32))
    est = sum(tm * d["bw"] * 4 for d in rows) * 2 + sum(tm * d["bw"] * 4 for d in cots)
    res = pl.pallas_call(
        body, name=name, grid=(ng, Tp // tm),
        in_specs=[_spec_rows(d, tm) for d in rows] + [_spec_const(d) for d in consts] + [_spec_rows(d, tm) for d in cots],
        out_specs=out_specs, out_shape=out_shape,
        compiler_params=_params(("arbitrary", "arbitrary"), 3 * est),
    )(*[d["arr"] for d in rows], *[d["arr"] for d in consts], *[d["arr"] for d in cots])
    return list(res[:len(drows)]), list(res[len(drows):])


def _sigmoid(x):
    return 0.5 * jnp.tanh(0.5 * x) + 0.5


def _silu(x):
    return x * _sigmoid(x)


def _softplus(x):
    return jnp.maximum(x, 0.0) + jnp.log(1.0 + jnp.exp(-jnp.abs(x)))


def _layer_norm(x, g, b):
    mu = jnp.mean(x, axis=-1, keepdims=True)
    xc = x - mu
    var = jnp.mean(xc * xc, axis=-1, keepdims=True)
    return xc * lax.rsqrt(var + LN_EPS) * g + b


def _rms_norm(x, g):
    return x * lax.rsqrt(jnp.mean(x * x, axis=-1, keepdims=True) + RMS_EPS) * g


HALF_ROPE = QK_ROPE // 2


@jax.custom_vjp
def _rope(r, cos, sin_lo, sin_hi):
    return (r * cos + pltpu.roll(r, LANES - HALF_ROPE, 1) * sin_lo + pltpu.roll(r, HALF_ROPE, 1) * sin_hi)


def _rope_fwd(r, cos, sin_lo, sin_hi):
    return _rope(r, cos, sin_lo, sin_hi), (cos, sin_lo, sin_hi)


def _rope_bwd(tables, g):
    cos, sin_lo, sin_hi = tables
    dr = g * cos + pltpu.roll(g * sin_lo, HALF_ROPE, 1) + pltpu.roll(g * sin_hi, LANES - HALF_ROPE, 1)
    return dr, jnp.zeros_like(cos), jnp.zeros_like(sin_lo), jnp.zeros_like(sin_hi)


_rope.defvjp(_rope_fwd, _rope_bwd)


def _make_stage_fns(npad):
    def fn_ln_masked(rowidx, x, g, b):
        return (jnp.where(rowidx >= npad, _layer_norm(x, g, b), 0.0),)

    def fn_in_post(rowidx, ql, kvl, kpe, dtr, cos, sin_lo, sin_hi, qg, kvg, dtb):
        qn = _rms_norm(ql, qg)
        kvn = _rms_norm(kvl, kvg)
        kr = _rope(kpe, cos, sin_lo, sin_hi)
        lane = lax.broadcasted_iota(jnp.int32, (1, LANES), 1)
        dt = jnp.where((rowidx >= npad) & (lane < SSD_HEADS), _softplus(dtr + dtb), 0.0)
        return qn, kvn, jnp.concatenate([kr] * HEADS, axis=1), dt

    def fn_q_post(rowidx, q, cos, sin_lo, sin_hi):
        rr = _rope(q[:, QK_NOPE:], cos, sin_lo, sin_hi)
        return (jnp.concatenate([q[:, :QK_NOPE], rr], axis=1) * ATT_SCALE,)

    def fn_gated_norm(rowidx, y, xs, z, dskip, g):
        v = (y + xs * dskip) * _silu(z)
        return (v * lax.rsqrt(jnp.mean(v * v, axis=-1, keepdims=True) + RMS_EPS) * g,)

    def fn_mix(rowidx, ga, gs, ya, ys):
        return (_sigmoid(ga) * ya + _sigmoid(gs) * ys,)

    def fn_res_ln(rowidx, h, r, g, b):
        return (jnp.where(rowidx >= npad, _layer_norm(ALPHA * h + r, g, b), 0.0),)

    def fn_glu(rowidx, u):
        return (_silu(u[:, :D_FF]) * u[:, D_FF:],)

    return dict(ln=fn_ln_masked, in_post=fn_in_post, q_post=fn_q_post, gated=fn_gated_norm, mix=fn_mix,
                res_ln=fn_res_ln, glu=fn_glu)


def _conv_tiles(Tp, C):
    return _pick(Tp, ROW_TILE), _pick(C, MM_COL_TILE)


def _conv_fwd(x, xoff, C, w8, b, K, act, npad, name, out_dtype=F32):
    Tp = x.shape[0]
    tm, tc = _conv_tiles(Tp, C)
    assert xoff % tc == 0
    cb0 = xoff // tc
    rb = tm // HALO

    def body(prev_ref, main_ref, w_ref, b_ref, o_ref):
        i = pl.program_id(1)
        main = main_ref[...].astype(F32)
        prev = jnp.where(i > 0, prev_ref[...].astype(F32), 0.0)
        ext = jnp.concatenate([prev, main], axis=0)
        acc = b_ref[...] + w_ref[K - 1:K, :] * main
        for k in range(K - 1):
            s = K - 1 - k
            acc = acc + w_ref[k:k + 1, :] * pltpu.roll(ext, s, 0)[HALO:, :]
        if act:
            rowidx = i * tm + lax.broadcasted_iota(jnp.int32, (tm, 1), 0)
            acc = jnp.where(rowidx >= npad, _silu(acc), 0.0)
        o_ref[...] = acc.astype(o_ref.dtype)

    return pl.pallas_call(
        body, name=name, grid=(C // tc, Tp // tm),
        in_specs=[pl.BlockSpec((HALO, tc), lambda g, i: (jnp.maximum(i * rb - 1, 0), cb0 + g)),
                  pl.BlockSpec((tm, tc), lambda g, i: (i, cb0 + g)),
                  pl.BlockSpec((SUBLANES, tc), lambda g, i: (0, g)),
                  pl.BlockSpec((1, tc), lambda g, i: (0, g))],
        out_specs=pl.BlockSpec((tm, tc), lambda g, i: (i, g)),
        out_shape=jax.ShapeDtypeStruct((Tp, C), out_dtype),
        compiler_params=_params(("parallel", "parallel"), 8 * tm * tc * 4),
    )(x, x, w8, b)


def _conv_bwd(x, xoff, C, w8, b, dy, K, act, npad, name):
    Tp = x.shape[0]
    tm, tc = _conv_tiles(Tp, C)
    cb0 = xoff // tc
    rb = tm // HALO
    ni = Tp // tm
    last_rb = Tp // HALO - 1
    n = tm + 2 * HALO

    def body(xp_ref, xm_ref, xn_ref, dym_ref, dyn_ref, w_ref, b_ref, dx_ref, dw_ref, db_ref):
        i = pl.program_id(1)
        prev = jnp.where(i > 0, xp_ref[...].astype(F32), 0.0)
        ext = jnp.concatenate([prev, xm_ref[...].astype(F32), xn_ref[...].astype(F32)], axis=0)
        dyn = jnp.where(i < ni - 1, dyn_ref[...].astype(F32), 0.0)
        dpre = jnp.concatenate([jnp.zeros((HALO, tc), F32), dym_ref[...].astype(F32), dyn], axis=0)
        shifted = [ext if k == K - 1 else pltpu.roll(ext, K - 1 - k, 0) for k in range(K)]
        if act:
            pre = b_ref[...] + sum(w_ref[k:k + 1, :] * shifted[k] for k in range(K))
            rowidx = i * tm - HALO + lax.broadcasted_iota(jnp.int32, (n, 1), 0)
            sg = _sigmoid(pre)
            dpre = jnp.where(rowidx >= npad, dpre * sg * (1.0 + pre * (1.0 - sg)), 0.0)
        dx = w_ref[K - 1:K, :] * dpre
        for k in range(K - 1):
            dx = dx + w_ref[k:k + 1, :] * pltpu.roll(dpre, n - (K - 1 - k), 0)
        dx_ref[...] = dx[HALO:HALO + tm, :].astype(dx_ref.dtype)

        @pl.when(i == 0)
        def _():
            dw_ref[...] = jnp.zeros_like(dw_ref)
            db_ref[...] = jnp.zeros_like(db_ref)

        dmain = dpre[HALO:HALO + tm, :]
        for k in range(K):
            dw_ref[k:k + 1, :] += jnp.sum(dmain * shifted[k][HALO:HALO + tm, :], axis=0, keepdims=True)
        db_ref[...] += jnp.sum(dmain, axis=0, keepdims=True)

    return pl.pallas_call(
        body, name=name, grid=(C // tc, ni),
        in_specs=[pl.BlockSpec((HALO, tc), lambda g, i: (jnp.maximum(i * rb - 1, 0), cb0 + g)),
                  pl.BlockSpec((tm, tc), lambda g, i: (i, cb0 + g)),
                  pl.BlockSpec((HALO, tc), lambda g, i: (jnp.minimum((i + 1) * rb, last_rb), cb0 + g)),
                  pl.BlockSpec((tm, tc), lambda g, i: (i, g)),
                  pl.BlockSpec((HALO, tc), lambda g, i: (jnp.minimum((i + 1) * rb, last_rb), g)),
                  pl.BlockSpec((SUBLANES, tc), lambda g, i: (0, g)),
                  pl.BlockSpec((1, tc), lambda g, i: (0, g))],
        out_specs=[pl.BlockSpec((tm, tc), lambda g, i: (i, g)),
                   pl.BlockSpec((SUBLANES, tc), lambda g, i: (0, g)),
                   pl.BlockSpec((1, tc), lambda g, i: (0, g))],
        out_shape=[jax.ShapeDtypeStruct((Tp, C), BF16), jax.ShapeDtypeStruct((SUBLANES, C), F32),
                   jax.ShapeDtypeStruct((1, C), F32)],
        compiler_params=_params(("parallel", "arbitrary"), 14 * tm * tc * 4),
    )(x, x, x, dy, dy, w8, b)


def _split_refs(refs, n_in, n_out, n_scratch, nbg):
    cuts = np.cumsum([0, n_in, nbg, n_out, nbg, n_scratch])
    return tuple(refs[a:b] for a, b in zip(cuts[:-1], cuts[1:])) + (refs[cuts[-1]:],)


def _flash_fwd(q, kv, kr8, npad, name, bg=None):
    Tp = q.shape[0]
    t = _pick(Tp, ATT_TILE)
    hp = ATT_HEADS_PER_STEP
    nb = Tp // t
    ng = HEADS // hp
    nbg = bg.n if bg else 0
    nt = (((1,), (1,)), ((), ()))
    tn = (((0,), (0,)), ((), ()))

    def body(*refs):
        (q_ref, kn_ref, kr_ref, v_ref), bg_in, (o_ref, lse_ref), bg_out, (m_sc, l_sc, acc_sc), bg_sems = _split_refs(
            refs, 4, 2, 3, nbg)
        g = pl.program_id(0)
        qi = pl.program_id(1)
        ki = pl.program_id(2)
        if bg:
            @pl.when((g == 0) & (qi == 0) & (ki == 0))
            def _():
                bg.start(bg_in, bg_out, bg_sems)

        @pl.when(ki == 0)
        def _():
            m_sc[...] = jnp.full_like(m_sc, NEG_INF)
            l_sc[...] = jnp.zeros_like(l_sc)
            acc_sc[...] = jnp.zeros_like(acc_sc)

        def step(masked):
            kr = kr_ref[...]
            if masked:
                key = ki * t + lax.broadcasted_iota(jnp.int32, (t, t), 0)
                qry = qi * t + lax.broadcasted_iota(jnp.int32, (t, t), 1)
                visible = (key <= qry) & (key >= npad)
            for hh in range(hp):
                k = jnp.concatenate([kn_ref[:, hh * QK_NOPE:(hh + 1) * QK_NOPE], kr], axis=1)
                st = lax.dot_general(k, q_ref[:, hh * QHEAD:(hh + 1) * QHEAD], nt, preferred_element_type=F32)
                if masked:
                    st = jnp.where(visible, st, NEG_INF)
                vs = slice(hh * V_HEAD, (hh + 1) * V_HEAD)
                m_prev = m_sc[hh]
                m_new = jnp.maximum(m_prev, jnp.max(st, axis=0, keepdims=True))
                pt = jnp.exp(st - m_new)
                a = jnp.exp(m_prev - m_new)
                l_sc[hh] = a * l_sc[hh] + jnp.sum(pt, axis=0, keepdims=True)
                acc_sc[vs, :] = a * acc_sc[vs, :] + lax.dot_general(v_ref[:, vs], pt.astype(BF16), tn,
                                                                    preferred_element_type=F32)
                m_sc[hh] = m_new

        need_mask = (ki == qi) | (ki == 0)

        @pl.when((ki <= qi) & need_mask)
        def _():
            step(True)

        @pl.when((ki <= qi) & jnp.logical_not(need_mask))
        def _():
            step(False)

        @pl.when(ki == qi)
        def _():
            for hh in range(hp):
                vs = slice(hh * V_HEAD, (hh + 1) * V_HEAD)
                l = l_sc[hh]
                o_ref[:, vs] = (acc_sc[vs, :] / l).T.astype(o_ref.dtype)
                lse_ref[hh * SUBLANES:(hh + 1) * SUBLANES, :] = jnp.broadcast_to(m_sc[hh] + jnp.log(l), (SUBLANES, t))

        if bg:
            @pl.when((g == ng - 1) & (qi == nb - 1) & (ki == nb - 1))
            def _():
                bg.wait(bg_in, bg_out, bg_sems)

    kmin = lambda qi, ki: jnp.minimum(ki, qi)
    return pl.pallas_call(
        body, name=name, grid=(ng, nb, nb),
        in_specs=[pl.BlockSpec((t, hp * QHEAD), lambda g, qi, ki: (qi, g)),
                  pl.BlockSpec((t, hp * QK_NOPE), lambda g, qi, ki: (kmin(qi, ki), g)),
                  pl.BlockSpec((t, LANES), lambda g, qi, ki: (kmin(qi, ki), 0)),
                  pl.BlockSpec((t, hp * V_HEAD), lambda g, qi, ki: (kmin(qi, ki), ng + g))] + (bg.specs if bg else []),
        out_specs=[pl.BlockSpec((t, hp * V_HEAD), lambda g, qi, ki: (qi, g)),
                   pl.BlockSpec((hp * SUBLANES, t), lambda g, qi, ki: (g, qi))] + (bg.specs if bg else []),
        out_shape=[jax.ShapeDtypeStruct((Tp, HEADS * V_HEAD), F32), jax.ShapeDtypeStruct((HEADS * SUBLANES, Tp), F32)]
        + (bg.out_shape if bg else []),
        scratch_shapes=[pltpu.VMEM((hp, 1, t), F32), pltpu.VMEM((hp, 1, t), F32), pltpu.VMEM((hp * V_HEAD, t), F32)]
        + (bg.scratch if bg else []),
        compiler_params=_params(("arbitrary",) * 3 if bg else ("parallel", "parallel", "arbitrary"), 8 * hp * t * t * 4),
    )(q, kv, kr8, kv, *(bg.arrs if bg else []))


def _attn_delta(do, o, name):
    Tp = do.shape[0]
    tm = _pick(Tp, MM_TOKEN_K_TILE)

    def body(do_ref, o_ref, d_ref):
        prod = do_ref[...] * o_ref[...]
        ones = jnp.ones((SUBLANES, V_HEAD), F32)
        d_ref[...] = lax.dot_general(ones, prod, (((1,), (1,)), ((), ())), precision=HIGHEST,
                                     preferred_element_type=F32)

    return pl.pallas_call(
        body, name=name, grid=(HEADS, Tp // tm),
        in_specs=[pl.BlockSpec((tm, V_HEAD), lambda h, i: (i, h)), pl.BlockSpec((tm, V_HEAD), lambda h, i: (i, h))],
        out_specs=pl.BlockSpec((SUBLANES, tm), lambda h, i: (h, i)),
        out_shape=jax.ShapeDtypeStruct((HEADS * SUBLANES, Tp), F32),
        compiler_params=_params(("parallel", "parallel"), 4 * tm * V_HEAD * 4),
    )(do, o)


def _flash_bwd(q, kv, kr8, do, lse, delta, npad, name, bg=None):
    Tp = q.shape[0]
    t = _pick(Tp, ATT_TILE)
    nb = Tp // t
    nbg = bg.n if bg else 0
    nt = (((1,), (1,)), ((), ()))
    tn = (((0,), (0,)), ((), ()))

    def body(*refs):
        ((q_ref, kn_ref, kr_ref, v_ref, do_ref, lse_ref, dl_ref), bg_in, (dq_ref, dkn_ref, dkr_ref, dv_ref), bg_out,
         (dk_sc, dv_sc), bg_sems) = _split_refs(refs, 7, 4, 2, nbg)
        h = pl.program_id(0)
        ki = pl.program_id(1)
        qi = pl.program_id(2)
        if bg:
            @pl.when((h == 0) & (ki == 0) & (qi == 0))
            def _():
                bg.start(bg_in, bg_out, bg_sems)

        @pl.when(qi == 0)
        def _():
            dk_sc[...] = jnp.zeros_like(dk_sc)
            dv_sc[...] = jnp.zeros_like(dv_sc)

        def step(masked):
            qv = q_ref[...]
            k = jnp.concatenate([kn_ref[...], kr_ref[...]], axis=1)
            st = lax.dot_general(k, qv, nt, preferred_element_type=F32)
            if masked:
                key = ki * t + lax.broadcasted_iota(jnp.int32, (t, t), 0)
                qry = qi * t + lax.broadcasted_iota(jnp.int32, (t, t), 1)
                st = jnp.where((key <= qry) & (key >= npad), st, NEG_INF)
            pt = jnp.exp(st - lse_ref[0:1, :])
            dob = do_ref[...].astype(BF16)
            dv_sc[...] += jnp.dot(pt.astype(BF16), dob, preferred_element_type=F32)
            dpt = lax.dot_general(v_ref[...], dob, nt, preferred_element_type=F32)
            dst = (pt * (dpt - dl_ref[0:1, :])).astype(BF16)
            dk_sc[...] += jnp.dot(dst, qv, preferred_element_type=F32)
            dqc = lax.dot_general(dst, k, tn, preferred_element_type=F32)
            rows = pl.ds(pl.multiple_of(qi * t, t), t)

            @pl.when(ki == 0)
            def _():
                dq_ref[rows, :] = dqc

            @pl.when(ki > 0)
            def _():
                dq_ref[rows, :] += dqc

        need_mask = (ki == qi) | (ki == 0)

        @pl.when((qi >= ki) & need_mask)
        def _():
            step(True)

        @pl.when((qi >= ki) & jnp.logical_not(need_mask))
        def _():
            step(False)

        @pl.when(qi == nb - 1)
        def _():
            dkn_ref[...] = dk_sc[:, :QK_NOPE].astype(dkn_ref.dtype)
            dkr_ref[...] = dk_sc[:, QK_NOPE:].astype(dkr_ref.dtype)
            dv_ref[...] = dv_sc[...].astype(dv_ref.dtype)

        if bg:
            @pl.when((h == HEADS - 1) & (ki == nb - 1) & (qi == nb - 1))
            def _():
                bg.wait(bg_in, bg_out, bg_sems)

    qmap = lambda h, ki, qi: (jnp.maximum(qi, ki), h)
    kmap = lambda h, ki, qi: (ki, h)
    est = 2 * Tp * QHEAD * 4 + 8 * t * t * 4
    return pl.pallas_call(
        body, name=name, grid=(HEADS, nb, nb),
        in_specs=[pl.BlockSpec((t, QHEAD), qmap),
                  pl.BlockSpec((t, QK_NOPE), kmap),
                  pl.BlockSpec((t, LANES), kmap),
                  pl.BlockSpec((t, V_HEAD), lambda h, ki, qi: (ki, HEADS + h)),
                  pl.BlockSpec((t, V_HEAD), qmap),
                  pl.BlockSpec((SUBLANES, t), lambda h, ki, qi: (h, jnp.maximum(qi, ki))),
                  pl.BlockSpec((SUBLANES, t), lambda h, ki, qi: (h, jnp.maximum(qi, ki)))] + (bg.specs if bg else []),
        out_specs=[pl.BlockSpec((Tp, QHEAD), lambda h, ki, qi: (0, h)),
                   pl.BlockSpec((t, QK_NOPE), kmap),
                   pl.BlockSpec((t, LANES), kmap),
                   pl.BlockSpec((t, V_HEAD), kmap)] + (bg.specs if bg else []),
        out_shape=[jax.ShapeDtypeStruct((Tp, HEADS * QHEAD), F32),
                   jax.ShapeDtypeStruct((Tp, HEADS * QK_NOPE), BF16),
                   jax.ShapeDtypeStruct((Tp, HEADS * LANES), F32),
                   jax.ShapeDtypeStruct((Tp, HEADS * V_HEAD), BF16)] + (bg.out_shape if bg else []),
        scratch_shapes=[pltpu.VMEM((t, QHEAD), F32), pltpu.VMEM((t, V_HEAD), F32)] + (bg.scratch if bg else []),
        compiler_params=_params(("arbitrary",) * 3 if bg else ("parallel", "arbitrary", "arbitrary"), est),
    )(q, kv, kr8, kv, do, lse, delta, *(bg.arrs if bg else []))


GW = SSD_INNER // SSD_GROUPS
PAIRS_PER_GROUP = GW // LANES
XB = SSD_INNER // GW
NT_DIMS = (((1,), (1,)), ((), ()))
TN_DIMS = (((0,), (0,)), ((), ()))


def _ssd_common(xs_ref, dt_ref, alog_ref, e_ref):
    a_neg = -jnp.exp(alog_ref[...])
    dt = dt_ref[...]
    li = lax.broadcasted_iota(jnp.int32, (CHUNK, CHUNK), 0)
    si = lax.broadcasted_iota(jnp.int32, (CHUNK, CHUNK), 1)
    tril = li >= si
    tri = tril.astype(F32)
    acs = jnp.dot(tri, dt * a_neg, precision=SSD_PREC, preferred_element_type=F32)
    e = e_ref[...]
    dte = jnp.dot(dt, e, precision=SSD_PREC, preferred_element_type=F32)
    acse = jnp.dot(acs, e, precision=SSD_PREC, preferred_element_type=F32)
    x = xs_ref[...] * dte
    alast = acse[CHUNK - 1:CHUNK, :]
    return dict(a_neg=a_neg, dt=dt, tril=tril, tri=tri, acs=acs, acs_t=acs.T, e=e, dte=dte, acse=acse, x=x,
                p_e=jnp.exp(acse), w_e=jnp.exp(alast - acse), dl_e=jnp.exp(alast), li=li, si=si)


def _decay(cm, head):
    col = cm["acs"][:, head:head + 1]
    row = cm["acs_t"][head:head + 1, :]
    return jnp.exp(jnp.where(cm["tril"], col - row, -jnp.inf))


def _ssd_fwd(xbc, dt, alog, e, name):
    Tp = xbc.shape[0]
    nc = Tp // CHUNK

    def body(xs_ref, b_ref, c_ref, dt_ref, alog_ref, e_ref, y_ref, st_ref, st_sc):
        @pl.when(pl.program_id(0) == 0)
        def _():
            st_sc[...] = jnp.zeros_like(st_sc)

        cm = _ssd_common(xs_ref, dt_ref, alog_ref, e_ref)
        st_ref[0] = st_sc[...]
        lane = lax.broadcasted_iota(jnp.int32, (CHUNK, LANES), 1)
        for g in range(SSD_GROUPS):
            gs = slice(g * GW, (g + 1) * GW)
            cg = c_ref[:, g * SSD_STATE:(g + 1) * SSD_STATE].astype(BF16)
            bg = b_ref[:, g * SSD_STATE:(g + 1) * SSD_STATE].astype(BF16)
            cb = lax.dot_general(cg, bg, NT_DIMS, preferred_element_type=F32)
            stg = st_sc[:, gs]
            yoff = jnp.dot(cg, stg.astype(BF16), preferred_element_type=F32) * cm["p_e"][:, gs]
            xg = cm["x"][:, gs]
            for jp in range(PAIRS_PER_GROUP):
                j = g * PAIRS_PER_GROUP + jp
                xp = xg[:, jp * LANES:(jp + 1) * LANES].astype(BF16)
                ys = []
                for head in (2 * j, 2 * j + 1):
                    m = (cb * _decay(cm, head)).astype(BF16)
                    ys.append(jnp.dot(m, xp, preferred_element_type=F32))
                y_ref[:, j * LANES:(j + 1) * LANES] = (jnp.where(lane < SSD_HEAD_DIM, ys[0], ys[1])
                                                       + yoff[:, jp * LANES:(jp + 1) * LANES])
            snew = lax.dot_general(bg, (cm["w_e"][:, gs] * xg).astype(BF16), TN_DIMS, preferred_element_type=F32)
            st_sc[:, gs] = cm["dl_e"][:, gs] * stg + snew

    return pl.pallas_call(
        body, name=name, grid=(nc,),
        in_specs=[pl.BlockSpec((CHUNK, SSD_INNER), lambda c: (c, 0)),
                  pl.BlockSpec((CHUNK, GW), lambda c: (c, XB)),
                  pl.BlockSpec((CHUNK, GW), lambda c: (c, XB + 1)),
                  pl.BlockSpec((CHUNK, LANES), lambda c: (c, 0)),
                  pl.BlockSpec((1, LANES), lambda c: (0, 0)),
                  pl.BlockSpec((LANES, SSD_INNER), lambda c: (0, 0))],
        out_specs=[pl.BlockSpec((CHUNK, SSD_INNER), lambda c: (c, 0)),
                   pl.BlockSpec((1, SSD_STATE, SSD_INNER), lambda c: (c, 0, 0))],
        out_shape=[jax.ShapeDtypeStruct((Tp, SSD_INNER), F32), jax.ShapeDtypeStruct((nc, SSD_STATE, SSD_INNER), F32)],
        scratch_shapes=[pltpu.VMEM((SSD_STATE, SSD_INNER), F32)],
        compiler_params=_params(("arbitrary",), 24 * CHUNK * SSD_INNER * 4),
    )(xbc, xbc, xbc, dt, alog, e)


def _ssd_bwd(xbc, dt, alog, e, dy, dxs_skip, states, name):
    Tp = xbc.shape[0]
    nc = Tp // CHUNK
    rev = lambda c: nc - 1 - c

    def body(xs_ref, b_ref, c_ref, dt_ref, alog_ref, e_ref, dy_ref, skip_ref, st_ref,
             dxbc_ref, ddt_ref, dalog_ref, dst_sc, dx_sc, t_sc, tw_sc):
        @pl.when(pl.program_id(0) == 0)
        def _():
            dst_sc[...] = jnp.zeros_like(dst_sc)
            dalog_ref[...] = jnp.zeros_like(dalog_ref)

        cm = _ssd_common(xs_ref, dt_ref, alog_ref, e_ref)
        lane = lax.broadcasted_iota(jnp.int32, (CHUNK, LANES), 1)
        dacs_col = jnp.zeros((CHUNK, LANES), F32)
        dacs_row = jnp.zeros((LANES, CHUNK), F32)
        t_last = []
        for g in range(SSD_GROUPS):
            gs = slice(g * GW, (g + 1) * GW)
            cg = c_ref[:, g * SSD_STATE:(g + 1) * SSD_STATE].astype(BF16)
            bg = b_ref[:, g * SSD_STATE:(g + 1) * SSD_STATE].astype(BF16)
            stg = st_ref[0, :, gs]
            stg_b = stg.astype(BF16)
            dstg = dst_sc[:, gs]
            dstg_b = dstg.astype(BF16)
            xg = cm["x"][:, gs]
            dyg = dy_ref[:, gs]
            zg = jnp.dot(cg, stg_b, preferred_element_type=F32)
            dzg = dyg * cm["p_e"][:, gs]
            dzg_b = dzg.astype(BF16)
            dcg = lax.dot_general(dzg_b, stg_b, NT_DIMS, preferred_element_type=F32)
            dst_in = lax.dot_general(cg, dzg_b, TN_DIMS, preferred_element_type=F32)
            dst_in = dst_in + cm["dl_e"][:, gs] * dstg
            t_last.append(jnp.sum(dstg * stg * cm["dl_e"][:, gs], axis=0, keepdims=True))
            weg = cm["w_e"][:, gs]
            dbg = lax.dot_general((weg * xg).astype(BF16), dstg_b, NT_DIMS, preferred_element_type=F32)
            gg = jnp.dot(bg, dstg_b, preferred_element_type=F32)
            dxg = weg * gg
            tw_sc[:, gs] = xg * dxg
            t_sc[:, gs] = dzg * zg - xg * dxg
            cb = lax.dot_general(cg, bg, NT_DIMS, preferred_element_type=F32)
            dcb = jnp.zeros((CHUNK, CHUNK), F32)
            for jp in range(PAIRS_PER_GROUP):
                j = g * PAIRS_PER_GROUP + jp
                ps = slice(jp * LANES, (jp + 1) * LANES)
                xp = xg[:, ps].astype(BF16)
                dyp = dyg[:, ps]
                dxp = dxg[:, ps]
                for half, head in enumerate((2 * j, 2 * j + 1)):
                    lam = _decay(cm, head)
                    m32 = cb * lam
                    sel = (lane < SSD_HEAD_DIM) if half == 0 else (lane >= SSD_HEAD_DIM)
                    dye = jnp.where(sel, dyp, 0.0).astype(BF16)
                    dm = lax.dot_general(dye, xp, NT_DIMS, preferred_element_type=F32)
                    w = dm * m32
                    dacs_col = dacs_col + jnp.where(cm["si"] == head, jnp.sum(w, axis=1, keepdims=True), 0.0)
                    dacs_row = dacs_row + jnp.where(cm["li"] == head, jnp.sum(w, axis=0, keepdims=True), 0.0)
                    dcb = dcb + dm * lam
                    dxp = dxp + lax.dot_general(m32.astype(BF16), dye, TN_DIMS, preferred_element_type=F32)
                dx_sc[:, j * LANES:(j + 1) * LANES] = dxp
            dcb_b = dcb.astype(BF16)
            dcg = dcg + jnp.dot(dcb_b, bg, preferred_element_type=F32)
            dbg = dbg + lax.dot_general(dcb_b, cg, TN_DIMS, preferred_element_type=F32)
            dst_sc[:, gs] = dst_in
            dxbc_ref[:, SSD_INNER + g * SSD_STATE:SSD_INNER + (g + 1) * SSD_STATE] = dbg
            dxbc_ref[:, SSD_INNER + GW + g * SSD_STATE:SSD_INNER + GW + (g + 1) * SSD_STATE] = dcg
        e = cm["e"]
        dacs = lax.dot_general(t_sc[...], e, NT_DIMS, precision=SSD_PREC, preferred_element_type=F32)
        dacs = dacs + dacs_col - dacs_row.T
        last_lane = jnp.concatenate(t_last, axis=1) + jnp.sum(tw_sc[...], axis=0, keepdims=True)
        last_head = lax.dot_general(jnp.broadcast_to(last_lane, (SUBLANES, SSD_INNER)), e, NT_DIMS,
                                    precision=SSD_PREC, preferred_element_type=F32)[0:1, :]
        dacs = dacs + jnp.where(cm["li"] == CHUNK - 1, last_head, 0.0)
        da = lax.dot_general(cm["tri"], dacs, TN_DIMS, precision=SSD_PREC, preferred_element_type=F32)
        dx_all = dx_sc[...]
        ddt = da * cm["a_neg"] + lax.dot_general(dx_all * xs_ref[...], e, NT_DIMS, precision=SSD_PREC,
                                                 preferred_element_type=F32)
        ddt_ref[...] = ddt
        dxbc_ref[:, :SSD_INNER] = dx_all * cm["dte"] + skip_ref[...]
        dalog_ref[0:1, :] += jnp.sum(da * cm["dt"], axis=0, keepdims=True) * cm["a_neg"]

    return pl.pallas_call(
        body, name=name, grid=(nc,),
        in_specs=[pl.BlockSpec((CHUNK, SSD_INNER), lambda c: (rev(c), 0)),
                  pl.BlockSpec((CHUNK, GW), lambda c: (rev(c), XB)),
                  pl.BlockSpec((CHUNK, GW), lambda c: (rev(c), XB + 1)),
                  pl.BlockSpec((CHUNK, LANES), lambda c: (rev(c), 0)),
                  pl.BlockSpec((1, LANES), lambda c: (0, 0)),
                  pl.BlockSpec((LANES, SSD_INNER), lambda c: (0, 0)),
                  pl.BlockSpec((CHUNK, SSD_INNER), lambda c: (rev(c), 0)),
                  pl.BlockSpec((CHUNK, SSD_INNER), lambda c: (rev(c), 0)),
                  pl.BlockSpec((1, SSD_STATE, SSD_INNER), lambda c: (rev(c), 0, 0))],
        out_specs=[pl.BlockSpec((CHUNK, SSD_CONV_DIM), lambda c: (rev(c), 0)),
                   pl.BlockSpec((CHUNK, LANES), lambda c: (rev(c), 0)),
                   pl.BlockSpec((SUBLANES, LANES), lambda c: (0, 0))],
        out_shape=[jax.ShapeDtypeStruct((Tp, SSD_CONV_DIM), F32), jax.ShapeDtypeStruct((Tp, LANES), F32),
                   jax.ShapeDtypeStruct((SUBLANES, LANES), F32)],
        scratch_shapes=[pltpu.VMEM((SSD_STATE, SSD_INNER), F32), pltpu.VMEM((CHUNK, SSD_INNER), F32),
                        pltpu.VMEM((CHUNK, SSD_INNER), F32), pltpu.VMEM((CHUNK, SSD_INNER), F32)],
        compiler_params=_params(("arbitrary",), 32 * CHUNK * SSD_INNER * 4),
    )(xbc, xbc, xbc, dt, alog, e, dy, dxs_skip, states)


def _loss_head(h, target, name):
    Tp, d = h.shape
    nt = Tp // LANES

    def body(h_ref, t_ref, dh_ref, l_ref):
        real = pl.program_id(0) > 0
        err = jnp.where(real, h_ref[...] - t_ref[...], 0.0)
        dh_ref[...] = err * (1.0 / d)
        l_ref[...] = jnp.broadcast_to(0.5 * jnp.sum(err * err) * (1.0 / d), l_ref.shape)

    return pl.pallas_call(
        body, name=name, grid=(nt,),
        in_specs=[pl.BlockSpec((LANES, d), lambda i: (i, 0)),
                  pl.BlockSpec((LANES, d), lambda i: (jnp.maximum(i - 1, 0), 0))],
        out_specs=[pl.BlockSpec((LANES, d), lambda i: (i, 0)),
                   pl.BlockSpec((1, SUBLANES, LANES), lambda i: (i, 0, 0))],
        out_shape=[jax.ShapeDtypeStruct((Tp, d), F32), jax.ShapeDtypeStruct((nt, SUBLANES, LANES), F32)],
        compiler_params=_params(("parallel",), 8 * LANES * d * 4),
    )(h, target)


def _adamw(parts, w, m, v, name):
    shape = w.shape
    C = shape[-1]
    R = int(np.prod(shape[:-1]))
    npart = parts.shape[0]
    parts, w, m, v = parts.reshape(npart, R, C), w.reshape(R, C), m.reshape(R, C), v.reshape(R, C)
    lanes = -(-C // LANES) * LANES
    tr = _pick(R, max(BF16_ROWS, ADAM_ELEMS // lanes), BF16_ROWS) if R % BF16_ROWS == 0 else R
    c1 = 1.0 / (1.0 - ADAM_B1 ** ADAM_STEP)
    c2 = 1.0 / (1.0 - ADAM_B2 ** ADAM_STEP)

    def body(p_ref, w_ref, m_ref, v_ref, g_out, d_out, m_out, v_out):
        g = p_ref[0].astype(F32)
        for p in range(1, npart):
            g = g + p_ref[p].astype(F32)
        m_new = ADAM_B1 * m_ref[...] + (1.0 - ADAM_B1) * g
        v_new = ADAM_B2 * v_ref[...] + (1.0 - ADAM_B2) * (g * g)
        g_out[...] = g
        m_out[...] = m_new
        v_out[...] = v_new
        d_out[...] = -ADAM_LR * ((m_new * c1) / (jnp.sqrt(v_new * c2) + ADAM_EPS) + ADAM_WD * w_ref[...])

    spec = pl.BlockSpec((tr, C), lambda i: (i, 0))
    est = npart * tr * lanes * parts.dtype.itemsize + 7 * tr * lanes * 4
    res = pl.pallas_call(
        body, name=name, grid=(R // tr,),
        in_specs=[pl.BlockSpec((npart, tr, C), lambda i: (0, i, 0)), spec, spec, spec],
        out_specs=[spec] * 4, out_shape=[jax.ShapeDtypeStruct((R, C), F32)] * 4,
        compiler_params=_params(("parallel",), est),
    )(parts, w, m, v)
    return [r.reshape(shape) for r in res]


MESH_ID = pl.DeviceIdType.MESH
N_PEERS = N_DEV - 1


def _dev_index(p):
    return 4 * p[0] + 2 * p[1] + p[2]


class _Background:
    def __init__(self, kind, arrs):
        self.kind, self.arrs, self.n = kind, list(arrs), len(arrs)
        self.npairs = N_PEERS if kind == "gather" else N_CHIPS - 1
        lead = (N_DEV,) if kind == "gather" else ()
        self.out_shape = [jax.ShapeDtypeStruct(lead + a.shape, a.dtype) for a in self.arrs]
        self.specs = [pl.BlockSpec(memory_space=pl.ANY)] * self.n
        self.scratch = [pltpu.SemaphoreType.DMA((self.n, self.npairs)), pltpu.SemaphoreType.DMA((self.n, self.npairs)),
                        pltpu.SemaphoreType.DMA((self.n,))]

    def copies(self, in_refs, out_refs, sems):
        send_sems, recv_sems, local_sems = sems
        x, y, c = lax.axis_index("x"), lax.axis_index("y"), lax.axis_index("c")
        sends, recvs, locals_ = [], [], []

        def remote(t, k, src, dst, to):
            return pltpu.make_async_remote_copy(src_ref=src, dst_ref=dst, send_sem=send_sems.at[t, k],
                                                recv_sem=recv_sems.at[t, k], device_id=to, device_id_type=MESH_ID)

        if self.kind == "gather":
            me = _dev_index((x, y, c))
            peers = [(x, y, 1 - c), (1 - x, y, c), (x, 1 - y, c), (1 - x, 1 - y, c),
                     (1 - x, y, 1 - c), (x, 1 - y, 1 - c), (1 - x, 1 - y, 1 - c)]
            for t in range(self.n):
                locals_.append(pltpu.make_async_copy(in_refs[t], out_refs[t].at[me], local_sems.at[t]))
                for k, p in enumerate(peers):
                    sends.append(remote(t, k, in_refs[t], out_refs[t].at[me], p))
                    recvs.append(remote(t, k, in_refs[t], out_refs[t].at[_dev_index(p)], p))
        else:
            mine = 2 * x + y
            peers = [(1 - x, y), (x, 1 - y), (1 - x, 1 - y)]
            for t in range(self.n):
                locals_.append(pltpu.make_async_copy(in_refs[t].at[mine], out_refs[t].at[mine], local_sems.at[t]))
                for k, p in enumerate(peers):
                    theirs = 2 * p[0] + p[1]
                    sends.append(remote(t, k, in_refs[t].at[theirs], out_refs[t].at[mine], (*p, c)))
                    recvs.append(remote(t, k, in_refs[t].at[mine], out_refs[t].at[theirs], (*p, c)))
        return sends, recvs, locals_

    def start(self, in_refs, out_refs, sems):
        sends, _, locals_ = self.copies(in_refs, out_refs, sems)
        for cp in locals_ + sends:
            cp.start()

    def wait(self, in_refs, out_refs, sems):
        sends, recvs, locals_ = self.copies(in_refs, out_refs, sems)
        for cp in recvs:
            cp.wait_recv()
        for cp in sends:
            cp.wait_send()
        for cp in locals_:
            cp.wait()


def _comm_call(body, name, arrs, out_shape, npairs):
    n = len(arrs)
    any_spec = pl.BlockSpec(memory_space=pl.ANY)
    return pl.pallas_call(
        functools.partial(body, n), name=name, in_specs=[any_spec] * n, out_specs=[any_spec] * n, out_shape=out_shape,
        scratch_shapes=[pltpu.SemaphoreType.DMA((n, npairs)), pltpu.SemaphoreType.DMA((n, npairs)),
                        pltpu.SemaphoreType.DMA((n,))],
    )(*arrs)


def _allgather(arrs, name):
    def body(n, *refs):
        src_refs, out_refs = refs[:n], refs[n:2 * n]
        send_sems, recv_sems, local_sems = refs[2 * n:]
        x, y, c = lax.axis_index("x"), lax.axis_index("y"), lax.axis_index("c")
        me, sibling = (x, y, c), (x, y, 1 - c)
        chips = [(1 - x, y), (x, 1 - y), (1 - x, 1 - y)]

        def copy(t, k, block, to, src=None):
            slot = out_refs[t].at[_dev_index(block)]
            return pltpu.make_async_remote_copy(
                src_ref=slot if src is None else src, dst_ref=slot,
                send_sem=send_sems.at[t, k], recv_sem=recv_sems.at[t, k],
                device_id=to, device_id_type=MESH_ID)

        sends, locals_ = [], []
        for t in range(n):
            mine = pltpu.make_async_copy(src_refs[t], out_refs[t].at[_dev_index(me)], local_sems.at[t])
            mine.start()
            locals_.append(mine)
            first = [copy(t, 0, me, sibling, src=src_refs[t])]
            first += [copy(t, 1 + j, me, (*chip, c), src=src_refs[t]) for j, chip in enumerate(chips)]
            for cp in first:
                cp.start()
            sends += first
        for j, chip in enumerate(chips):
            for t in range(n):
                copy(t, 1 + j, (*chip, c), me).wait_recv()
                passed = copy(t, 4 + j, (*chip, c), sibling)
                passed.start()
                sends.append(passed)
        for t in range(n):
            copy(t, 0, sibling, me).wait_recv()
            for j, chip in enumerate(chips):
                copy(t, 4 + j, (*chip, 1 - c), me).wait_recv()
        for cp in sends:
            cp.wait_send()
        for cp in locals_:
            cp.wait()

    return _comm_call(body, name, arrs, [jax.ShapeDtypeStruct((N_DEV,) + a.shape, a.dtype) for a in arrs], N_PEERS)


N_CHIPS = N_DEV // 2
CHIPS = [(0, 0), (0, 1), (1, 0), (1, 1)]


def _sibling_exchange(arrs, name):
    def body(n, *refs):
        in_refs, out_refs = refs[:n], refs[n:2 * n]
        send_sems, recv_sems, _ = refs[2 * n:]
        x, y, c = lax.axis_index("x"), lax.axis_index("y"), lax.axis_index("c")
        sibling = (x, y, 1 - c)

        def copy(t, j):
            return pltpu.make_async_remote_copy(
                src_ref=in_refs[t].at[_dev_index((*CHIPS[j], 1 - c))], dst_ref=out_refs[t].at[j],
                send_sem=send_sems.at[t, j], recv_sem=recv_sems.at[t, j],
                device_id=sibling, device_id_type=MESH_ID)

        copies = [copy(t, j) for t in range(n) for j in range(N_CHIPS)]
        for cp in copies:
            cp.start()
        for cp in copies:
            cp.wait_recv()
        for cp in copies:
            cp.wait_send()

    return _comm_call(body, name, arrs, [jax.ShapeDtypeStruct((N_CHIPS,) + a.shape[1:], a.dtype) for a in arrs], N_CHIPS)


def _chip_exchange(arrs, name):
    def body(n, *refs):
        in_refs, out_refs = refs[:n], refs[n:2 * n]
        send_sems, recv_sems, local_sems = refs[2 * n:]
        x, y, c = lax.axis_index("x"), lax.axis_index("y"), lax.axis_index("c")
        mine = 2 * x + y
        peers = [(1 - x, y), (x, 1 - y), (1 - x, 1 - y)]

        def copy(t, k, src_chip, dst_chip, to):
            return pltpu.make_async_remote_copy(
                src_ref=in_refs[t].at[src_chip], dst_ref=out_refs[t].at[dst_chip],
                send_sem=send_sems.at[t, k], recv_sem=recv_sems.at[t, k],
                device_id=(*to, c), device_id_type=MESH_ID)

        sends, locals_ = [], []
        for t in range(n):
            own = pltpu.make_async_copy(in_refs[t].at[mine], out_refs[t].at[mine], local_sems.at[t])
            own.start()
            locals_.append(own)
            for k, p in enumerate(peers):
                cp = copy(t, k, 2 * p[0] + p[1], mine, p)
                cp.start()
                sends.append(cp)
        for t in range(n):
            for k, p in enumerate(peers):
                copy(t, k, mine, 2 * p[0] + p[1], p).wait_recv()
        for cp in sends:
            cp.wait_send()
        for cp in locals_:
            cp.wait()

    return _comm_call(body, name, arrs, [jax.ShapeDtypeStruct(a.shape, a.dtype) for a in arrs], N_CHIPS - 1)


def _add_pairs(a, b, name):
    shape = a.shape
    C = shape[-1]
    R = int(np.prod(shape[:-1]))
    lanes = -(-C // LANES) * LANES
    tr = _pick(R, max(BF16_ROWS, 2 * ADAM_ELEMS // lanes), BF16_ROWS) if R % BF16_ROWS == 0 else R

    def body(a_ref, b_ref, o_ref):
        o_ref[...] = (a_ref[...].astype(F32) + b_ref[...].astype(F32)).astype(o_ref.dtype)

    spec = pl.BlockSpec((tr, C), lambda i: (i, 0))
    return pl.pallas_call(
        body, name=name, grid=(R // tr,), in_specs=[spec, spec], out_specs=spec,
        out_shape=jax.ShapeDtypeStruct((R, C), a.dtype),
        compiler_params=_params(("parallel",), 3 * tr * lanes * 4),
    )(a.reshape(R, C), b.reshape(R, C)).reshape(shape)


WEIGHTS = ['meta_tokens', 'emb_ln_g', 'emb_ln_b', 'w_in', 'q_norm_g', 'w_q_b', 'kv_norm_g', 'w_kv_b', 'w_o_attn',
           'ssd_conv_w', 'ssd_conv_b', 'dt_bias', 'a_log', 'd_skip', 'ssd_norm_g', 'w_o_ssd', 'w_out', 'ln1_g',
           'ln1_b', 'w_up', 'ffn_conv_w', 'ffn_conv_b', 'w_down', 'ln2_g', 'ln2_b']
BIG = {'w_in': 2, 'w_q_b': 2, 'w_kv_b': 2, 'w_o_attn': 1, 'w_o_ssd': 1, 'w_out': 1, 'w_up': 2, 'w_down': 1}
SMALL_SHARDED = {'meta_tokens': 1, 'ssd_conv_w': 2, 'ffn_conv_w': 2}
REPLICATED = [n for n in WEIGHTS if n not in BIG and n not in SMALL_SHARDED]
FIRST_USED = ['w_in', 'w_q_b', 'w_kv_b']
AFTER_ATTENTION = [n for n in BIG if n not in FIRST_USED]
BIG_COLS = 1024
SMALL_COLS = LANES


def _flatten(arrs, cols, row_mult, lead=False):
    parts, offs, off = [], [], 0
    for a in arrs:
        a2 = a.reshape(N_DEV, -1) if lead else a.reshape(1, -1)
        n = a2.shape[1]
        pad = -n % cols
        parts.append(jnp.pad(a2, ((0, 0), (0, pad))))
        offs.append((off, n))
        off += n + pad
    rows = off // cols
    extra = (-rows % row_mult) * cols
    if extra:
        parts.append(jnp.zeros((parts[0].shape[0], extra), parts[0].dtype))
    flat = jnp.concatenate(parts, axis=1)
    flat = flat.reshape(flat.shape[0], -1, cols)
    return (flat if lead else flat[0]), offs


def _unflatten(flat, offs, shapes):
    f = flat.reshape(-1)
    return [f[o:o + n].reshape(s) for (o, n), s in zip(offs, shapes)]


def _to_pieces(g, axis):
    s = g.shape[axis] // N_DEV
    g = g.reshape(g.shape[:axis] + (N_DEV, s) + g.shape[axis + 1:])
    return jnp.moveaxis(g, axis, 0).reshape(N_DEV, -1)


def _from_pieces(p, shard_shape, axis):
    g = jnp.moveaxis(p.reshape((N_DEV,) + tuple(shard_shape)), 0, axis)
    sh = list(shard_shape)
    sh[axis] *= N_DEV
    return g.reshape(sh)


def _in_proj_pad(w):
    e = np.cumsum((0,) + IN_SIZES)
    ql, kvl, kpe, z, xbc, dt, ga, gs = [w[:, e[j]:e[j + 1]] for j in range(8)]
    zc = lambda n: jnp.zeros((w.shape[0], n), w.dtype)
    return jnp.concatenate([ql, kvl, z, xbc, ga, gs, kpe, zc(LANES - QK_ROPE), dt, zc(LANES - SSD_HEADS)], axis=1)


def _in_proj_unpad(d):
    seg = lambda o, n: d[:, o:o + n]
    return jnp.concatenate([seg(OQ, Q_LORA), seg(OKV, KV_LORA), seg(OKPE, QK_ROPE), seg(OZ, SSD_INNER),
                            seg(OXBC, SSD_CONV_DIM), seg(ODT, SSD_HEADS), seg(OGA, D_MODEL), seg(OGS, D_MODEL)], axis=1)


def _q_pad(w):
    w3 = w.reshape(Q_LORA, HEADS, QK_NOPE + QK_ROPE)
    return jnp.concatenate([w3, jnp.zeros((Q_LORA, HEADS, QHEAD - QK_NOPE - QK_ROPE), w.dtype)], axis=2).reshape(Q_LORA, HEADS * QHEAD)


def _q_unpad(d):
    return d.reshape(Q_LORA, HEADS, QHEAD)[:, :, :QK_NOPE + QK_ROPE].reshape(Q_LORA, HEADS * (QK_NOPE + QK_ROPE))


def _kv_perm(w):
    w3 = w.reshape(KV_LORA, HEADS, QK_NOPE + V_HEAD)
    return jnp.concatenate([w3[:, :, :QK_NOPE].reshape(KV_LORA, -1), w3[:, :, QK_NOPE:].reshape(KV_LORA, -1)], axis=1)


def _kv_unperm(d):
    kn = d[:, :HEADS * QK_NOPE].reshape(KV_LORA, HEADS, QK_NOPE)
    v = d[:, HEADS * QK_NOPE:].reshape(KV_LORA, HEADS, V_HEAD)
    return jnp.concatenate([kn, v], axis=2).reshape(KV_LORA, HEADS * (QK_NOPE + V_HEAD))


def _row_vec(v, width=None):
    v = v.reshape(1, -1).astype(F32)
    if width is not None and v.shape[1] < width:
        v = jnp.pad(v, ((0, 0), (0, width - v.shape[1])))
    return v


def _pad_rows8(w):
    return jnp.pad(w.astype(F32), ((0, SUBLANES - w.shape[0]), (0, 0)))


def _tables(Tp, npad):
    pos = jnp.maximum(jnp.arange(Tp, dtype=jnp.int32) - npad, 0).astype(F32)
    inv_freq = 1.0 / (ROPE_THETA ** (jnp.arange(0, QK_ROPE, 2, dtype=F32) / QK_ROPE))
    ang = pos[:, None] * inv_freq[None, :]
    zeros = lambda n: jnp.zeros((Tp, n), F32)
    cos = jnp.concatenate([jnp.cos(ang), jnp.cos(ang), zeros(LANES - QK_ROPE)], axis=1)
    sin_lo = jnp.concatenate([-jnp.sin(ang), zeros(LANES - HALF_ROPE)], axis=1)
    sin_hi = jnp.concatenate([zeros(HALF_ROPE), jnp.sin(ang), zeros(LANES - QK_ROPE)], axis=1)
    expand = np.zeros((LANES, SSD_INNER), np.float32)
    for hd in range(SSD_HEADS):
        expand[hd, hd * SSD_HEAD_DIM:(hd + 1) * SSD_HEAD_DIM] = 1.0
    return cos, sin_lo, sin_hi, jnp.asarray(expand)


def _layer_rows(proj, tb):
    rows_a = [_row(proj, Q_LORA, OQ // Q_LORA), _row(proj, KV_LORA, OKV // KV_LORA), _row(proj, LANES, OKPE // LANES),
              _row(proj, LANES, ODT // LANES)] + tb["rope_rows"]
    return rows_a


def _layer_fwd(h, h_bf, P, tb, fns, npad, bg=None, on_carried=None):
    both = [_out(D_MODEL, F32), _out(D_MODEL, BF16)]
    res_ln_twice = lambda *a: fns["res_ln"](*a) * 2
    proj = _mm(h_bf, P["w_in"], F32, "in_proj")
    rows_a = _layer_rows(proj, tb)
    consts_a = [_row(P["q_norm_g"]), _row(P["kv_norm_g"]), _row(P["dt_bias"])]
    qn, kvn, kr8, dt = _rw_fwd(fns["in_post"], rows_a, consts_a,
                               [_out(Q_LORA, BF16), _out(KV_LORA, BF16), _out(HEADS * LANES, BF16), _out(LANES, F32)],
                               "in_post")
    q = _mm(qn, P["w_q"], F32, "q_proj")
    rows_q = [_row(q, QHEAD, 0, grp=True)] + tb["rope_rows"]
    qr = _rw_fwd(fns["q_post"], rows_q, [], [_out(HEADS * QHEAD, BF16, QHEAD, grp=True)],
                 "q_post", ng=HEADS)[0]
    kv = _mm(kvn, P["w_kv"], BF16, "kv_proj")
    o, lse, *carried = _flash_fwd(qr, kv, kr8, npad, "attn_fwd_gather" if bg else "attn_fwd", bg=bg)
    if on_carried is not None:
        carried = on_carried(P, carried)
    ya = _mm(o, P["w_o_attn"], F32, "attn_out")
    xbc = _conv_fwd(proj, OXBC, SSD_CONV_DIM, P["ssd_conv_w"], P["ssd_conv_b"], SSD_CONV, True, npad, "ssd_conv")
    y, states = _ssd_fwd(xbc, dt, P["a_log"], tb["expand"], "ssd_fwd")
    rows_b = [_row(y, GW, 0, grp=True), _row(xbc, GW, 0, grp=True), _row(proj, GW, OZ // GW, grp=True)]
    consts_b = [_row(P["d_skip"], GW, 0, grp=True), _row(P["ssd_norm_g"], GW, 0, grp=True)]
    yn = _rw_fwd(fns["gated"], rows_b, consts_b, [_out(SSD_INNER, BF16, GW, grp=True)], "ssd_gate", ng=SSD_GROUPS)[0]
    ys = _mm(yn, P["w_o_ssd"], F32, "ssd_out")
    rows_c = [_row(proj, D_MODEL, OGA // D_MODEL), _row(proj, D_MODEL, OGS // D_MODEL), _row(ya), _row(ys)]
    mixed = _rw_fwd(fns["mix"], rows_c, [], [_out(D_MODEL, BF16)], "mix")[0]
    mo = _mm(mixed, P["w_out"], F32, "mix_out")
    consts_1 = [_row(P["ln1_g"]), _row(P["ln1_b"])]
    h1, h1_bf = _rw_fwd(res_ln_twice, [_row(h), _row(mo)], consts_1, both, "ln1")
    up = _mm(h1_bf, P["w_up"], BF16, "ffn_up")
    u = _conv_fwd(up, 0, 2 * D_FF, P["ffn_conv_w"], P["ffn_conv_b"], FFN_CONV, False, npad, "ffn_conv", BF16)
    act = _rw_fwd(fns["glu"], [_row(u)], [], [_out(D_FF, BF16)], "ffn_glu")[0]
    fo = _mm(act, P["w_down"], F32, "ffn_down")
    consts_2 = [_row(P["ln2_g"]), _row(P["ln2_b"])]
    h2, h2_bf = _rw_fwd(res_ln_twice, [_row(h1), _row(fo)], consts_2, both, "ln2")
    res = dict(h=h, h_bf=h_bf, proj=proj, qn=qn, kvn=kvn, kr8=kr8, dt=dt, q=q, qr=qr, kv=kv, o=o, lse=lse, ya=ya,
               xbc=xbc, y=y, states=states, yn=yn, ys=ys, mixed=mixed, mo=mo, h1=h1, h1_bf=h1_bf, up=up, u=u, act=act,
               fo=fo)
    return h2, h2_bf, res, carried


def _layer_bwd(dh2, r, P, tb, fns, npad, bg=None, before_attn=None):
    g = {}
    consts_2 = [_row(P["ln2_g"]), _row(P["ln2_b"])]
    (dh1_a, dfo), (g["ln2_g"], g["ln2_b"]) = _rw_bwd(fns["res_ln"], [_row(r["h1"]), _row(r["fo"])], consts_2,
                                                     [_row(dh2)], [F32, BF16], "ln2_bwd")
    g["w_down"] = _mm(r["act"], dfo, BF16, "dw_down", ta=True)
    dact = _mm(dfo, P["w_down"], BF16, "d_act", tb=True)
    (du,), _ = _rw_bwd(fns["glu"], [_row(r["u"])], [], [_row(dact)], [BF16], "glu_bwd")
    dup, g["ffn_conv_w"], g["ffn_conv_b"] = _conv_bwd(r["up"], 0, 2 * D_FF, P["ffn_conv_w"], P["ffn_conv_b"], du,
                                                      FFN_CONV, False, npad, "ffn_conv_bwd")
    g["w_up"] = _mm(r["h1_bf"], dup, BF16, "dw_up", ta=True)
    dh1 = _mm(dup, P["w_up"], F32, "d_h1", tb=True, add=dh1_a)
    consts_1 = [_row(P["ln1_g"]), _row(P["ln1_b"])]
    (dh_a, dmo), (g["ln1_g"], g["ln1_b"]) = _rw_bwd(fns["res_ln"], [_row(r["h"]), _row(r["mo"])], consts_1,
                                                    [_row(dh1)], [F32, BF16], "ln1_bwd")
    g["w_out"] = _mm(r["mixed"], dmo, BF16, "dw_out", ta=True)
    dmixed = _mm(dmo, P["w_out"], F32, "d_mixed", tb=True)
    proj = r["proj"]
    rows_c = [_row(proj, D_MODEL, OGA // D_MODEL), _row(proj, D_MODEL, OGS // D_MODEL), _row(r["ya"]), _row(r["ys"])]
    (dga, dgs, dya, dys), _ = _rw_bwd(fns["mix"], rows_c, [], [_row(dmixed)], [BF16] * 4, "mix_bwd")
    g["w_o_attn"] = _mm(r["o"], dya, BF16, "dw_o_attn", ta=True)
    do = _mm(dya, P["w_o_attn"], F32, "d_o", tb=True)
    g["w_o_ssd"] = _mm(r["yn"], dys, BF16, "dw_o_ssd", ta=True)
    dyn = _mm(dys, P["w_o_ssd"], F32, "d_yn", tb=True)
    rows_b = [_row(r["y"], GW, 0, grp=True), _row(r["xbc"], GW, 0, grp=True), _row(proj, GW, OZ // GW, grp=True)]
    consts_b = [_row(P["d_skip"], GW, 0, grp=True), _row(P["ssd_norm_g"], GW, 0, grp=True)]
    (dy, dxs_skip, dz), (g["d_skip"], g["ssd_norm_g"]) = _rw_bwd(
        fns["gated"], rows_b, consts_b, [_row(dyn, GW, 0, grp=True)], [F32, F32, BF16], "ssd_gate_bwd", ng=SSD_GROUPS)
    dxbc, ddt, g["a_log"] = _ssd_bwd(r["xbc"], r["dt"], P["a_log"], tb["expand"], dy, dxs_skip, r["states"], "ssd_bwd")
    dxbc_pre, g["ssd_conv_w"], g["ssd_conv_b"] = _conv_bwd(proj, OXBC, SSD_CONV_DIM, P["ssd_conv_w"], P["ssd_conv_b"],
                                                           dxbc, SSD_CONV, True, npad, "ssd_conv_bwd")
    delta = _attn_delta(do, r["o"], "attn_delta")
    if before_attn is not None:
        bg = before_attn(g)
    dqr, dkn, dkr8, dv, *carried = _flash_bwd(r["qr"], r["kv"], r["kr8"], do, r["lse"], delta, npad,
                                              "attn_bwd_exchange" if bg else "attn_bwd", bg=bg)
    rows_q = [_row(r["q"], QHEAD, 0, grp=True)] + tb["rope_rows"]
    (dq,), _ = _rw_bwd(fns["q_post"], rows_q, [], [_row(dqr, QHEAD, 0, grp=True)], [BF16],
                       "q_post_bwd", ng=HEADS)
    g["w_q"] = _mm(r["qn"], dq, BF16, "dw_q", ta=True)
    dqn = _mm(dq, P["w_q"], F32, "d_qn", tb=True)
    dkv = jnp.concatenate([dkn, dv], axis=1)
    g["w_kv"] = _mm(r["kvn"], dkv, BF16, "dw_kv", ta=True)
    dkvn = _mm(dkv, P["w_kv"], F32, "d_kvn", tb=True)
    rows_a = _layer_rows(proj, tb)
    consts_a = [_row(P["q_norm_g"]), _row(P["kv_norm_g"]), _row(P["dt_bias"])]
    (dql, dkvl, dkpe, ddtr), (g["q_norm_g"], g["kv_norm_g"], g["dt_bias"]) = _rw_bwd(
        fns["in_post"], rows_a, consts_a, [_row(dqn), _row(dkvn), _row(dkr8), _row(ddt)], [BF16] * 4, "in_post_bwd")
    dproj = jnp.concatenate([dql, dkvl, dz, dxbc_pre, dga, dgs, dkpe, ddtr], axis=1)
    g["w_in"] = _mm(r["h_bf"], dproj, BF16, "dw_in", ta=True)
    dh = _mm(dproj, P["w_in"], F32, "d_h", tb=True, add=dh_a)
    return dh, g, carried


def _full_weight(g, axis):
    if axis == 1:
        return g.reshape(-1, g.shape[-1])
    return jnp.concatenate([g[p] for p in range(N_DEV)], axis=1)


def _grad_pieces(d, axis):
    if axis == 1:
        return d.reshape(N_DEV, -1, d.shape[1])
    return jnp.transpose(d.reshape(d.shape[0], N_DEV, -1), (1, 0, 2))


def _big_params(gathered):
    prep = {"w_in": ("w_in", _in_proj_pad), "w_q_b": ("w_q", _q_pad), "w_kv_b": ("w_kv", _kv_perm)}
    P = {}
    for n, g in gathered.items():
        key, fn = prep.get(n, (n, lambda a: a))
        P[key] = fn(_full_weight(g, BIG[n]))
    return P


def _layer_params(gathered, small, i):
    P = _big_params(gathered)
    P["q_norm_g"] = _row_vec(small["q_norm_g"][i])
    P["kv_norm_g"] = _row_vec(small["kv_norm_g"][i])
    P["dt_bias"] = _row_vec(small["dt_bias"][i], LANES)
    P["a_log"] = _row_vec(small["a_log"][i], LANES)
    P["d_skip"] = _row_vec(jnp.repeat(small["d_skip"][i], SSD_HEAD_DIM))
    P["ssd_norm_g"] = _row_vec(small["ssd_norm_g"][i])
    P["ssd_conv_w"] = _pad_rows8(small["ssd_conv_w"][i])
    P["ssd_conv_b"] = _row_vec(small["ssd_conv_b"][i])
    P["ffn_conv_w"] = _pad_rows8(small["ffn_conv_w"][i])
    P["ffn_conv_b"] = _row_vec(small["ffn_conv_b"][i])
    for n in ("ln1_g", "ln1_b", "ln2_g", "ln2_b"):
        P[n] = _row_vec(small[n][i])
    return P


def _layer_grads_to_reference_layout(g):
    out = {}
    out["w_in"] = _in_proj_unpad(g["w_in"])
    out["w_q_b"] = _q_unpad(g["w_q"])
    out["w_kv_b"] = _kv_unperm(g["w_kv"])
    for n in ("w_o_attn", "w_o_ssd", "w_out", "w_up", "w_down"):
        out[n] = g[n]
    out["q_norm_g"] = g["q_norm_g"][0]
    out["kv_norm_g"] = g["kv_norm_g"][0]
    out["dt_bias"] = g["dt_bias"][0, :SSD_HEADS]
    out["a_log"] = g["a_log"][0, :SSD_HEADS]
    out["d_skip"] = g["d_skip"].reshape(SSD_HEADS, SSD_HEAD_DIM).sum(axis=1)
    out["ssd_norm_g"] = g["ssd_norm_g"][0]
    out["ssd_conv_w"] = g["ssd_conv_w"][:SSD_CONV]
    out["ssd_conv_b"] = g["ssd_conv_b"][0]
    out["ffn_conv_w"] = g["ffn_conv_w"][:FFN_CONV]
    out["ffn_conv_b"] = g["ffn_conv_b"][0]
    for n in ("ln1_g", "ln1_b", "ln2_g", "ln2_b"):
        out[n] = g[n][0]
    return out


def kernel(x, meta_tokens, emb_ln_g, emb_ln_b, w_in, q_norm_g, w_q_b, kv_norm_g, w_kv_b, w_o_attn, ssd_conv_w, ssd_conv_b, dt_bias, a_log, d_skip, ssd_norm_g, w_o_ssd, w_out, ln1_g, ln1_b, w_up, ffn_conv_w, ffn_conv_b, w_down, ln2_g, ln2_b, loss_target, m_meta_tokens, m_emb_ln_g, m_emb_ln_b, m_w_in, m_q_norm_g, m_w_q_b, m_kv_norm_g, m_w_kv_b, m_w_o_attn, m_ssd_conv_w, m_ssd_conv_b, m_dt_bias, m_a_log, m_d_skip, m_ssd_norm_g, m_w_o_ssd, m_w_out, m_ln1_g, m_ln1_b, m_w_up, m_ffn_conv_w, m_ffn_conv_b, m_w_down, m_ln2_g, m_ln2_b, v_meta_tokens, v_emb_ln_g, v_emb_ln_b, v_w_in, v_q_norm_g, v_w_q_b, v_kv_norm_g, v_w_kv_b, v_w_o_attn, v_ssd_conv_w, v_ssd_conv_b, v_dt_bias, v_a_log, v_d_skip, v_ssd_norm_g, v_w_o_ssd, v_w_out, v_ln1_g, v_ln1_b, v_w_up, v_ffn_conv_w, v_ffn_conv_b, v_w_down, v_ln2_g, v_ln2_b):
    given = dict(locals())
    w = {n: given[n] for n in WEIGHTS}
    m = {n: given["m_" + n] for n in WEIGHTS}
    v = {n: given["v_" + n] for n in WEIGHTS}
    seq = x.shape[1]
    assert x.shape[0] == 1 and seq % LANES == 0
    npad = LANES - N_META
    Tp = npad + N_META + seq
    depth = w_in.shape[0]

    big_names, small_names = list(BIG), list(SMALL_SHARDED)
    ws, offs_s = _flatten([w[n] for n in small_names], SMALL_COLS, SUBLANES)
    shards = [{n: w[n][i].astype(BF16) for n in big_names} for i in range(depth)]
    got = _allgather([shards[0][n] for n in FIRST_USED] + [ws], "weight_allgather")
    gathered = dict(zip(FIRST_USED, got[:-1]))
    gsm = got[-1]
    small = {n: w[n] for n in REPLICATED}
    for n, (o, sz) in zip(small_names, offs_s):
        small[n] = _from_pieces(gsm.reshape(N_DEV, -1)[:, o:o + sz], w[n].shape, SMALL_SHARDED[n])

    fns = _make_stage_fns(npad)
    cos, sin_lo, sin_hi, expand = _tables(Tp, npad)
    tb = dict(rope_rows=[_row(t, diff=False) for t in (cos, sin_lo, sin_hi)], expand=expand)
    top = jnp.pad(small["meta_tokens"], ((npad, 0), (0, 0)))
    hcat = jnp.concatenate([top, x[0]], axis=0)
    consts_e = [_row(_row_vec(w["emb_ln_g"])), _row(_row_vec(w["emb_ln_b"]))]
    h, h_bf = _rw_fwd(lambda *a: fns["ln"](*a) * 2, [_row(hcat)], consts_e, [_out(D_MODEL, F32), _out(D_MODEL, BF16)],
                      "emb_ln")
    layers, saved = [], []
    for i in range(depth):
        layers.append(_layer_params(gathered, small, i))
        late = AFTER_ATTENTION if i == 0 else []
        nxt = big_names if i + 1 < depth else []
        arrs = [shards[i][n] for n in late] + [shards[i + 1][n] for n in nxt]

        def on_carried(P, carried, late=late):
            P.update(_big_params(dict(zip(late, carried[:len(late)]))))
            return carried[len(late):]

        h, h_bf, res, carried = _layer_fwd(h, h_bf, layers[i], tb, fns, npad,
                                           bg=_Background("gather", arrs) if arrs else None, on_carried=on_carried)
        gathered = dict(zip(nxt, carried))
        saved.append(res)
    dh, lparts = _loss_head(h, loss_target[0], "loss_head")
    loss = lax.psum(jnp.sum(lparts[:, 0, 0]), ("x", "y", "c"))

    core = lax.axis_index("c")

    def chip_partials(pieces, tag):
        from_sibling = _sibling_exchange(pieces, "grad_exchange_cores_" + tag)
        sums = []
        for k, (p, r) in enumerate(zip(pieces, from_sibling)):
            own = lax.dynamic_index_in_dim(p.reshape((N_CHIPS, 2) + p.shape[1:]), core, axis=1, keepdims=False)
            sums.append(_add_pairs(own, r, "grad_chip_sum_%s_%d" % (tag, k)))
        return sums

    lg, recv_big, pending = [None] * depth, [None] * depth, []
    for i in reversed(range(depth)):
        early = AFTER_ATTENTION if i == 0 else []

        def before_attn(g, pending=pending, early=early, i=i):
            sums = pending + (chip_partials([_grad_pieces(g[n], BIG[n]) for n in early], "l%d_early" % i) if early else [])
            return _Background("chips", sums) if sums else None

        dh, gi, carried = _layer_bwd(dh, saved[i], layers[i], tb, fns, npad, before_attn=before_attn)
        if pending:
            recv_big[i + 1] = dict(zip(big_names, carried[:len(pending)]))
        recv_big[i] = dict(zip(early, carried[len(pending):]))
        lg[i] = _layer_grads_to_reference_layout(gi)
        pending = []
        if i > 0:
            pending = chip_partials([_grad_pieces(lg[i][n], BIG[n]) for n in big_names], "l%d" % i)
    (dhcat,), (d_emb_g, d_emb_b) = _rw_bwd(fns["ln"], [_row(hcat)], consts_e, [_row(dh)], [F32], "emb_ln_bwd")
    grad_x = dhcat[LANES:][None]
    local = {n: jnp.stack([lg[i][n] for i in range(depth)]) for n in lg[0] if n not in BIG}
    local["meta_tokens"] = dhcat[npad:LANES]
    local["emb_ln_g"] = d_emb_g[0]
    local["emb_ln_b"] = d_emb_b[0]

    sm_names = small_names + REPLICATED
    sm_pieces = [_to_pieces(local[n], SMALL_SHARDED[n]) for n in small_names]
    sm_pieces += [jnp.broadcast_to(local[n].reshape(1, -1), (N_DEV, local[n].size)) for n in REPLICATED]
    ps, _ = _flatten(sm_pieces, SMALL_COLS, BF16_ROWS, lead=True)
    pieces = [_grad_pieces(lg[0][n], BIG[n]) for n in FIRST_USED] + [ps]
    recv = _chip_exchange(chip_partials(pieces, "l0"), "grad_exchange_chips")
    recv_big[0].update(zip(FIRST_USED, recv[:-1]))
    outs = {}
    kinds = ("grad", "delta", "new_m", "new_v")
    for n in big_names:
        parts = jnp.stack([recv_big[i][n] for i in range(depth)], axis=1)
        for kind, a in zip(kinds, _adamw(parts, w[n], m[n], v[n], "adamw_" + n)):
            outs[kind + "_" + n] = a
    wf, offs = _flatten([w[n] for n in sm_names], SMALL_COLS, BF16_ROWS)
    mf, _ = _flatten([m[n] for n in sm_names], SMALL_COLS, BF16_ROWS)
    vf, _ = _flatten([v[n] for n in sm_names], SMALL_COLS, BF16_ROWS)
    shapes = [w[n].shape for n in sm_names]
    for kind, flat in zip(kinds, _adamw(recv[-1], wf, mf, vf, "adamw_small")):
        for n, a in zip(sm_names, _unflatten(flat, offs, shapes)):
            outs[kind + "_" + n] = a
    result = [loss, grad_x]
    for kind in ("grad", "delta", "new_m", "new_v"):
        result += [outs[kind + "_" + n] for n in WEIGHTS]
    return tuple(result)
```

```python
import functools

import jax
import jax.numpy as jnp
import numpy as np
from jax import lax
from jax.experimental import pallas as pl
from jax.experimental.pallas import tpu as pltpu

F32 = jnp.float32
BF16 = jnp.bfloat16
HIGHEST = lax.Precision.HIGHEST
SSD_PREC = lax.Precision.HIGH

D_MODEL = 1024
DEPTH = 2
N_META = 16
HEADS = 8
Q_LORA = 768
KV_LORA = 256
QK_NOPE = 128
QK_ROPE = 64
V_HEAD = 128
ROPE_THETA = 10000.0
SSD_INNER = 2048
SSD_HEAD_DIM = 64
SSD_HEADS = 32
SSD_GROUPS = 4
SSD_STATE = 128
SSD_CONV = 4
SSD_CONV_DIM = SSD_INNER + 2 * SSD_GROUPS * SSD_STATE
CHUNK = 128
D_FF = 2816
FFN_CONV = 3
LN_EPS = 1e-5
RMS_EPS = 1e-6
ALPHA = (2 * DEPTH) ** 0.25
IN_SIZES = (Q_LORA, KV_LORA, QK_ROPE, SSD_INNER, SSD_CONV_DIM, SSD_HEADS, D_MODEL, D_MODEL)
ATT_SCALE = (QK_NOPE + QK_ROPE) ** -0.5
NEG_INF = -1e30
ADAM_LR, ADAM_B1, ADAM_B2, ADAM_EPS, ADAM_WD, ADAM_STEP = 0.001, 0.9, 0.999, 1e-08, 0.01, 10

LANES = 128
SUBLANES = 8
VMEM_BYTES = 64 * 1024 * 1024
N_DEV = 8

OQ, OKV, OZ, OXBC, OGA, OGS, OKPE, ODT = 0, 768, 1024, 3072, 6144, 7168, 8192, 8320
IN_PAD = 8448
QHEAD = 256

ROW_TILE = 640
MM_COL_TILE = 1408
MM_ROW_TILE = 1664
MM_VMEM_BUDGET = 46 * 1024 * 1024
MM_K_TILE = 2816
MM_TOKEN_K_TILE = 1664
ATT_TILE = 640
ATT_HEADS_PER_STEP = 8
BF16_ROWS = 16
HALO = BF16_ROWS
ROW_BUDGET = 7 * 1024 * 1024
ADAM_ELEMS = 160 * 1024


def _pick(n, target, q=LANES):
    assert n % q == 0, (n, q)
    units = n // q
    best = q
    for d in range(1, units + 1):
        if units % d == 0 and d * q <= target:
            best = d * q
    return best


def _pick_rows(n, row_bytes):
    return _pick(n, max(BF16_ROWS, ROW_BUDGET // row_bytes), BF16_ROWS)


def _params(sem, est_bytes):
    limit = int(min(VMEM_BYTES - (6 << 20), max(32 << 20, 2 * est_bytes + (8 << 20))))
    return pltpu.CompilerParams(dimension_semantics=sem, vmem_limit_bytes=limit)


def _nbytes(shape, dtype):
    return int(np.prod(shape)) * jnp.dtype(dtype).itemsize


def _mm(a, b, out_dtype, name, ta=False, tb=False, add=None):
    assert not (ta and tb)
    if ta:
        K, M = a.shape
        tm = _pick(M, MM_COL_TILE)
        tk = _pick(K, MM_TOKEN_K_TILE)
    else:
        M, K = a.shape
        tk = _pick(K, MM_K_TILE)
    N, K2 = (b.shape if tb else b.shape[::-1])
    assert K == K2
    tn = _pick(N, MM_COL_TILE)
    nk = K // tk

    def vmem_estimate(tm):
        e = 2 * (tm * tk * a.dtype.itemsize + tk * tn * b.dtype.itemsize + tm * tn * jnp.dtype(out_dtype).itemsize)
        e += tm * tn * 4 + tm * tk * 2
        return e + (tm * tn * 4 if nk > 1 else 0) + (2 * tm * tn * 4 if add is not None else 0)

    if not ta:
        tm = _pick(M, MM_ROW_TILE, BF16_ROWS)
        while vmem_estimate(tm) > MM_VMEM_BUDGET and tm > BF16_ROWS:
            tm = _pick(M, tm - BF16_ROWS, BF16_ROWS)
    dn = (((0,), (0,)), ((), ())) if ta else ((((1,), (1,)), ((), ())) if tb else (((1,), (0,)), ((), ())))

    def body(*refs):
        a_ref, b_ref = refs[:2]
        add_ref = refs[2] if add is not None else None
        o_ref = refs[2 + (add is not None)]
        d = lax.dot_general(a_ref[...].astype(BF16), b_ref[...].astype(BF16), dn, preferred_element_type=F32)

        def finish(r):
            if add is not None:
                r = r + add_ref[...].astype(F32)
            o_ref[...] = r.astype(out_dtype)

        if nk == 1:
            finish(d)
            return
        acc = refs[-1]
        k = pl.program_id(2)

        @pl.when(k == 0)
        def _():
            acc[...] = d

        @pl.when((k > 0) & (k < nk - 1))
        def _():
            acc[...] += d

        @pl.when(k == nk - 1)
        def _():
            finish(acc[...] + d)

    if ta:
        a_spec = pl.BlockSpec((tk, tm), lambda i, j, k: (k, i))
    else:
        a_spec = pl.BlockSpec((tm, tk), lambda i, j, k: (i, k))
    b_spec = pl.BlockSpec((tn, tk), lambda i, j, k: (j, k)) if tb else pl.BlockSpec((tk, tn), lambda i, j, k: (k, j))
    in_specs = [a_spec, b_spec]
    args = [a, b]
    est = vmem_estimate(tm)
    if add is not None:
        in_specs.append(pl.BlockSpec((tm, tn), lambda i, j, k: (i, j)))
        args.append(add)
    return pl.pallas_call(
        body, name=name, grid=(M // tm, N // tn, nk), in_specs=in_specs,
        out_specs=pl.BlockSpec((tm, tn), lambda i, j, k: (i, j)),
        out_shape=jax.ShapeDtypeStruct((M, N), out_dtype),
        scratch_shapes=[pltpu.VMEM((tm, tn), F32)] if nk > 1 else [],
        compiler_params=_params(("parallel", "parallel", "arbitrary"), est),
    )(*args)


def _row(arr, bw=None, cb=0, grp=False, diff=True):
    return dict(arr=arr, bw=arr.shape[1] if bw is None else bw, cb=cb, grp=grp, diff=diff)


def _out(width, dtype, bw=None, grp=False):
    return dict(width=width, dtype=dtype, bw=width if bw is None else bw, grp=grp)


def _spec_rows(d, tm):
    return pl.BlockSpec((tm, d["bw"]), lambda g, i, cb=d["cb"], gr=d["grp"]: (i, cb + (g if gr else 0)))


def _spec_const(d):
    return pl.BlockSpec((d["arr"].shape[0], d["bw"]), lambda g, i, cb=d["cb"], gr=d["grp"]: (0, cb + (g if gr else 0)))


def _rw_fwd(fn, rows, consts, outs, name, ng=1):
    Tp = rows[0]["arr"].shape[0]
    tm = _pick_rows(Tp, 4 * (sum(d["bw"] for d in rows) + 2 * sum(o["bw"] for o in outs)))
    nr, ncst = len(rows), len(consts)

    def body(*refs):
        i = pl.program_id(1)
        rowidx = i * tm + lax.broadcasted_iota(jnp.int32, (tm, 1), 0)
        rv = [r[...].astype(F32) for r in refs[:nr]]
        cv = [c[...] for c in refs[nr:nr + ncst]]
        vals = fn(rowidx, *rv, *cv)
        for o, v in zip(refs[nr + ncst:], vals):
            o[...] = v.astype(o.dtype)

    est = sum(tm * d["bw"] * 4 for d in rows) + sum(tm * o["bw"] * 4 for o in outs)
    return pl.pallas_call(
        body, name=name, grid=(ng, Tp // tm),
        in_specs=[_spec_rows(d, tm) for d in rows] + [_spec_const(d) for d in consts],
        out_specs=[pl.BlockSpec((tm, o["bw"]), lambda g, i, gr=o["grp"]: (i, g if gr else 0)) for o in outs],
        out_shape=[jax.ShapeDtypeStruct((Tp, o["width"]), o["dtype"]) for o in outs],
        compiler_params=_params(("parallel", "parallel"), 3 * est),
    )(*[d["arr"] for d in rows], *[d["arr"] for d in consts])


def _rw_bwd(fn, rows, consts, cots, drow_dtypes, name, ng=1):
    Tp = rows[0]["arr"].shape[0]
    tm = _pick_rows(Tp, 4 * (3 * sum(d["bw"] for d in rows) + 2 * sum(d["bw"] for d in cots)))
    nr, ncst, nct = len(rows), len(consts), len(cots)
    drows = [k for k, d in enumerate(rows) if d["diff"]]
    dcsts = [k for k, d in enumerate(consts) if d["diff"]]
    for k in drows:
        assert rows[k]["grp"] or ng == 1

    def body(*refs):
        g = pl.program_id(0)
        i = pl.program_id(1)
        rowidx = i * tm + lax.broadcasted_iota(jnp.int32, (tm, 1), 0)
        rv = [r[...].astype(F32) for r in refs[:nr]]
        cv = [c[...] for c in refs[nr:nr + ncst]]
        ct = tuple(r[...].astype(F32) for r in refs[nr + ncst:nr + ncst + nct])
        orefs = refs[nr + ncst + nct:]

        def f(*dargs):
            rr, cc = list(rv), list(cv)
            for k, v in zip(drows, dargs[:len(drows)]):
                rr[k] = v
            for k, v in zip(dcsts, dargs[len(drows):]):
                cc[k] = v
            return tuple(fn(rowidx, *rr, *cc))

        _, vjp = jax.vjp(f, *[rv[k] for k in drows], *[cv[k] for k in dcsts])
        grads = vjp(ct)
        for o, v in zip(orefs[:len(drows)], grads[:len(drows)]):
            o[...] = v.astype(o.dtype)
        for k, o, v in zip(dcsts, orefs[len(drows):], grads[len(drows):]):
            first = (i == 0) if consts[k]["grp"] else ((i == 0) & (g == 0))

            @pl.when(first)
            def _(o=o, v=v):
                o[...] = v

            @pl.when(jnp.logical_not(first))
            def _(o=o, v=v):
                o[...] += v

    out_specs, out_shape = [], []
    for k, dt in zip(drows, drow_dtypes):
        d = rows[k]
        out_specs.append(pl.BlockSpec((tm, d["bw"]), lambda g, i, gr=d["grp"]: (i, g if gr else 0)))
        out_shape.append(jax.ShapeDtypeStruct((Tp, d["bw"] * (ng if d["grp"] else 1)), dt))
    for k in dcsts:
        d = consts[k]
        r = d["arr"].shape[0]
        out_specs.append(pl.BlockSpec((r, d["bw"]), lambda g, i, gr=d["grp"]: (0, g if gr else 0)))
        out_shape.append(jax.ShapeDtypeStruct((r, d["bw"] * (ng if d["grp"] else 1)), F32))
    est = sum(tm * d["bw"] * 4 for d in rows) * 2 + sum(tm * d["bw"] * 4 for d in cots)
    res = pl.pallas_call(
        body, name=name, grid=(ng, Tp // tm),
        in_specs=[_spec_rows(d, tm) for d in rows] + [_spec_const(d) for d in consts] + [_spec_rows(d, tm) for d in cots],
        out_specs=out_specs, out_shape=out_shape,
        compiler_params=_params(("arbitrary", "arbitrary"), 3 * est),
    )(*[d["arr"] for d in rows], *[d["arr"] for d in consts], *[d["arr"] for d in cots])
    return list(res[:len(drows)]), list(res[len(drows):])


def _sigmoid(x):
    return 0.5 * jnp.tanh(0.5 * x) + 0.5


def _silu(x):
    return x * _sigmoid(x)


def _softplus(x):
    return jnp.maximum(x, 0.0) + jnp.log(1.0 + jnp.exp(-jnp.abs(x)))


def _layer_norm(x, g, b):
    mu = jnp.mean(x, axis=-1, keepdims=True)
    xc = x - mu
    var = jnp.mean(xc * xc, axis=-1, keepdims=True)
    return xc * lax.rsqrt(var + LN_EPS) * g + b


def _rms_norm(x, g):
    return x * lax.rsqrt(jnp.mean(x * x, axis=-1, keepdims=True) + RMS_EPS) * g


def _rope(r, cos, sin, rot):
    return r * cos + jnp.dot(r, rot, precision=HIGHEST, preferred_element_type=F32) * sin


def _make_stage_fns(npad):
    def fn_ln_masked(rowidx, x, g, b):
        return (jnp.where(rowidx >= npad, _layer_norm(x, g, b), 0.0),)

    def fn_in_post(rowidx, ql, kvl, kpe, dtr, cos, sin, rot, qg, kvg, dtb):
        qn = _rms_norm(ql, qg)
        kvn = _rms_norm(kvl, kvg)
        kr = _rope(kpe, cos, sin, rot)
        lane = lax.broadcasted_iota(jnp.int32, (1, LANES), 1)
        dt = jnp.where((rowidx >= npad) & (lane < SSD_HEADS), _softplus(dtr + dtb), 0.0)
        return qn, kvn, jnp.concatenate([kr] * HEADS, axis=1), dt

    def fn_q_post(rowidx, q, cos, sin, rot):
        rr = _rope(q[:, QK_NOPE:], cos, sin, rot)
        return (jnp.concatenate([q[:, :QK_NOPE], rr], axis=1) * ATT_SCALE,)

    def fn_gated_norm(rowidx, y, xs, z, dskip, g):
        v = (y + xs * dskip) * _silu(z)
        return (v * lax.rsqrt(jnp.mean(v * v, axis=-1, keepdims=True) + RMS_EPS) * g,)

    def fn_mix(rowidx, ga, gs, ya, ys):
        return (_sigmoid(ga) * ya + _sigmoid(gs) * ys,)

    def fn_res_ln(rowidx, h, r, g, b):
        return (jnp.where(rowidx >= npad, _layer_norm(ALPHA * h + r, g, b), 0.0),)

    def fn_glu(rowidx, u):
        return (_silu(u[:, :D_FF]) * u[:, D_FF:],)

    return dict(ln=fn_ln_masked, in_post=fn_in_post, q_post=fn_q_post, gated=fn_gated_norm, mix=fn_mix,
                res_ln=fn_res_ln, glu=fn_glu)


def _conv_tiles(Tp, C):
    return _pick(Tp, ROW_TILE), _pick(C, MM_COL_TILE)


def _conv_fwd(x, xoff, C, w8, b, K, act, npad, name, out_dtype=F32):
    Tp = x.shape[0]
    tm, tc = _conv_tiles(Tp, C)
    assert xoff % tc == 0
    cb0 = xoff // tc
    rb = tm // HALO

    def body(prev_ref, main_ref, w_ref, b_ref, o_ref):
        i = pl.program_id(1)
        main = main_ref[...].astype(F32)
        prev = jnp.where(i > 0, prev_ref[...].astype(F32), 0.0)
        ext = jnp.concatenate([prev, main], axis=0)
        acc = b_ref[...] + w_ref[K - 1:K, :] * main
        for k in range(K - 1):
            s = K - 1 - k
            acc = acc + w_ref[k:k + 1, :] * pltpu.roll(ext, s, 0)[HALO:, :]
        if act:
            rowidx = i * tm + lax.broadcasted_iota(jnp.int32, (tm, 1), 0)
            acc = jnp.where(rowidx >= npad, _silu(acc), 0.0)
        o_ref[...] = acc.astype(o_ref.dtype)

    return pl.pallas_call(
        body, name=name, grid=(C // tc, Tp // tm),
        in_specs=[pl.BlockSpec((HALO, tc), lambda g, i: (jnp.maximum(i * rb - 1, 0), cb0 + g)),
                  pl.BlockSpec((tm, tc), lambda g, i: (i, cb0 + g)),
                  pl.BlockSpec((SUBLANES, tc), lambda g, i: (0, g)),
                  pl.BlockSpec((1, tc), lambda g, i: (0, g))],
        out_specs=pl.BlockSpec((tm, tc), lambda g, i: (i, g)),
        out_shape=jax.ShapeDtypeStruct((Tp, C), out_dtype),
        compiler_params=_params(("parallel", "parallel"), 8 * tm * tc * 4),
    )(x, x, w8, b)


def _conv_bwd(x, xoff, C, w8, b, dy, K, act, npad, name):
    Tp = x.shape[0]
    tm, tc = _conv_tiles(Tp, C)
    cb0 = xoff // tc
    rb = tm // HALO
    ni = Tp // tm
    last_rb = Tp // HALO - 1
    n = tm + 2 * HALO

    def body(xp_ref, xm_ref, xn_ref, dym_ref, dyn_ref, w_ref, b_ref, dx_ref, dw_ref, db_ref):
        i = pl.program_id(1)
        prev = jnp.where(i > 0, xp_ref[...].astype(F32), 0.0)
        ext = jnp.concatenate([prev, xm_ref[...].astype(F32), xn_ref[...].astype(F32)], axis=0)
        dyn = jnp.where(i < ni - 1, dyn_ref[...].astype(F32), 0.0)
        dpre = jnp.concatenate([jnp.zeros((HALO, tc), F32), dym_ref[...].astype(F32), dyn], axis=0)
        shifted = [ext if k == K - 1 else pltpu.roll(ext, K - 1 - k, 0) for k in range(K)]
        if act:
            pre = b_ref[...] + sum(w_ref[k:k + 1, :] * shifted[k] for k in range(K))
            rowidx = i * tm - HALO + lax.broadcasted_iota(jnp.int32, (n, 1), 0)
            sg = _sigmoid(pre)
            dpre = jnp.where(rowidx >= npad, dpre * sg * (1.0 + pre * (1.0 - sg)), 0.0)
        dx = w_ref[K - 1:K, :] * dpre
        for k in range(K - 1):
            dx = dx + w_ref[k:k + 1, :] * pltpu.roll(dpre, n - (K - 1 - k), 0)
        dx_ref[...] = dx[HALO:HALO + tm, :].astype(dx_ref.dtype)

        @pl.when(i == 0)
        def _():
            dw_ref[...] = jnp.zeros_like(dw_ref)
            db_ref[...] = jnp.zeros_like(db_ref)

        dmain = dpre[HALO:HALO + tm, :]
        for k in range(K):
            dw_ref[k:k + 1, :] += jnp.sum(dmain * shifted[k][HALO:HALO + tm, :], axis=0, keepdims=True)
        db_ref[...] += jnp.sum(dmain, axis=0, keepdims=True)

    return pl.pallas_call(
        body, name=name, grid=(C // tc, ni),
        in_specs=[pl.BlockSpec((HALO, tc), lambda g, i: (jnp.maximum(i * rb - 1, 0), cb0 + g)),
                  pl.BlockSpec((tm, tc), lambda g, i: (i, cb0 + g)),
                  pl.BlockSpec((HALO, tc), lambda g, i: (jnp.minimum((i + 1) * rb, last_rb), cb0 + g)),
                  pl.BlockSpec((tm, tc), lambda g, i: (i, g)),
                  pl.BlockSpec((HALO, tc), lambda g, i: (jnp.minimum((i + 1) * rb, last_rb), g)),
                  pl.BlockSpec((SUBLANES, tc), lambda g, i: (0, g)),
                  pl.BlockSpec((1, tc), lambda g, i: (0, g))],
        out_specs=[pl.BlockSpec((tm, tc), lambda g, i: (i, g)),
                   pl.BlockSpec((SUBLANES, tc), lambda g, i: (0, g)),
                   pl.BlockSpec((1, tc), lambda g, i: (0, g))],
        out_shape=[jax.ShapeDtypeStruct((Tp, C), BF16), jax.ShapeDtypeStruct((SUBLANES, C), F32),
                   jax.ShapeDtypeStruct((1, C), F32)],
        compiler_params=_params(("parallel", "arbitrary"), 14 * tm * tc * 4),
    )(x, x, x, dy, dy, w8, b)


def _split_refs(refs, n_in, n_out, n_scratch, nbg):
    cuts = np.cumsum([0, n_in, nbg, n_out, nbg, n_scratch])
    return tuple(refs[a:b] for a, b in zip(cuts[:-1], cuts[1:])) + (refs[cuts[-1]:],)


def _flash_fwd(q, kv, kr8, npad, name, bg=None):
    Tp = q.shape[0]
    t = _pick(Tp, ATT_TILE)
    hp = ATT_HEADS_PER_STEP
    nb = Tp // t
    ng = HEADS // hp
    nbg = bg.n if bg else 0
    nt = (((1,), (1,)), ((), ()))
    tn = (((0,), (0,)), ((), ()))

    def body(*refs):
        (q_ref, kn_ref, kr_ref, v_ref), bg_in, (o_ref, lse_ref), bg_out, (m_sc, l_sc, acc_sc), bg_sems = _split_refs(
            refs, 4, 2, 3, nbg)
        g = pl.program_id(0)
        qi = pl.program_id(1)
        ki = pl.program_id(2)
        if bg:
            @pl.when((g == 0) & (qi == 0) & (ki == 0))
            def _():
                bg.start(bg_in, bg_out, bg_sems)

        @pl.when(ki == 0)
        def _():
            m_sc[...] = jnp.full_like(m_sc, NEG_INF)
            l_sc[...] = jnp.zeros_like(l_sc)
            acc_sc[...] = jnp.zeros_like(acc_sc)

        def step(masked):
            kr = kr_ref[...]
            if masked:
                key = ki * t + lax.broadcasted_iota(jnp.int32, (t, t), 0)
                qry = qi * t + lax.broadcasted_iota(jnp.int32, (t, t), 1)
                visible = (key <= qry) & (key >= npad)
            for hh in range(hp):
                k = jnp.concatenate([kn_ref[:, hh * QK_NOPE:(hh + 1) * QK_NOPE], kr], axis=1)
                st = lax.dot_general(k, q_ref[:, hh * QHEAD:(hh + 1) * QHEAD], nt, preferred_element_type=F32)
                if masked:
                    st = jnp.where(visible, st, NEG_INF)
                vs = slice(hh * V_HEAD, (hh + 1) * V_HEAD)
                m_prev = m_sc[hh]
                m_new = jnp.maximum(m_prev, jnp.max(st, axis=0, keepdims=True))
                pt = jnp.exp(st - m_new)
                a = jnp.exp(m_prev - m_new)
                l_sc[hh] = a * l_sc[hh] + jnp.sum(pt, axis=0, keepdims=True)
                acc_sc[vs, :] = a * acc_sc[vs, :] + lax.dot_general(v_ref[:, vs], pt.astype(BF16), tn,
                                                                    preferred_element_type=F32)
                m_sc[hh] = m_new

        need_mask = (ki == qi) | (ki == 0)

        @pl.when((ki <= qi) & need_mask)
        def _():
            step(True)

        @pl.when((ki <= qi) & jnp.logical_not(need_mask))
        def _():
            step(False)

        @pl.when(ki == qi)
        def _():
            for hh in range(hp):
                vs = slice(hh * V_HEAD, (hh + 1) * V_HEAD)
                l = l_sc[hh]
                o_ref[:, vs] = (acc_sc[vs, :] / l).T.astype(o_ref.dtype)
                lse_ref[hh * SUBLANES:(hh + 1) * SUBLANES, :] = jnp.broadcast_to(m_sc[hh] + jnp.log(l), (SUBLANES, t))

        if bg:
            @pl.when((g == ng - 1) & (qi == nb - 1) & (ki == nb - 1))
            def _():
                bg.wait(bg_in, bg_out, bg_sems)

    kmin = lambda qi, ki: jnp.minimum(ki, qi)
    return pl.pallas_call(
        body, name=name, grid=(ng, nb, nb),
        in_specs=[pl.BlockSpec((t, hp * QHEAD), lambda g, qi, ki: (qi, g)),
                  pl.BlockSpec((t, hp * QK_NOPE), lambda g, qi, ki: (kmin(qi, ki), g)),
                  pl.BlockSpec((t, LANES), lambda g, qi, ki: (kmin(qi, ki), 0)),
                  pl.BlockSpec((t, hp * V_HEAD), lambda g, qi, ki: (kmin(qi, ki), ng + g))] + (bg.specs if bg else []),
        out_specs=[pl.BlockSpec((t, hp * V_HEAD), lambda g, qi, ki: (qi, g)),
                   pl.BlockSpec((hp * SUBLANES, t), lambda g, qi, ki: (g, qi))] + (bg.specs if bg else []),
        out_shape=[jax.ShapeDtypeStruct((Tp, HEADS * V_HEAD), F32), jax.ShapeDtypeStruct((HEADS * SUBLANES, Tp), F32)]
        + (bg.out_shape if bg else []),
        scratch_shapes=[pltpu.VMEM((hp, 1, t), F32), pltpu.VMEM((hp, 1, t), F32), pltpu.VMEM((hp * V_HEAD, t), F32)]
        + (bg.scratch if bg else []),
        compiler_params=_params(("arbitrary",) * 3 if bg else ("parallel", "parallel", "arbitrary"), 8 * hp * t * t * 4),
    )(q, kv, kr8, kv, *(bg.arrs if bg else []))


def _attn_delta(do, o, name):
    Tp = do.shape[0]
    tm = _pick(Tp, MM_TOKEN_K_TILE)

    def body(do_ref, o_ref, d_ref):
        prod = do_ref[...] * o_ref[...]
        ones = jnp.ones((SUBLANES, V_HEAD), F32)
        d_ref[...] = lax.dot_general(ones, prod, (((1,), (1,)), ((), ())), precision=HIGHEST,
                                     preferred_element_type=F32)

    return pl.pallas_call(
        body, name=name, grid=(HEADS, Tp // tm),
        in_specs=[pl.BlockSpec((tm, V_HEAD), lambda h, i: (i, h)), pl.BlockSpec((tm, V_HEAD), lambda h, i: (i, h))],
        out_specs=pl.BlockSpec((SUBLANES, tm), lambda h, i: (h, i)),
        out_shape=jax.ShapeDtypeStruct((HEADS * SUBLANES, Tp), F32),
        compiler_params=_params(("parallel", "parallel"), 4 * tm * V_HEAD * 4),
    )(do, o)


def _flash_bwd(q, kv, kr8, do, lse, delta, npad, name, bg=None):
    Tp = q.shape[0]
    t = _pick(Tp, ATT_TILE)
    nb = Tp // t
    nbg = bg.n if bg else 0
    nt = (((1,), (1,)), ((), ()))
    tn = (((0,), (0,)), ((), ()))

    def body(*refs):
        ((q_ref, kn_ref, kr_ref, v_ref, do_ref, lse_ref, dl_ref), bg_in, (dq_ref, dkn_ref, dkr_ref, dv_ref), bg_out,
         (dk_sc, dv_sc), bg_sems) = _split_refs(refs, 7, 4, 2, nbg)
        h = pl.program_id(0)
        ki = pl.program_id(1)
        qi = pl.program_id(2)
        if bg:
            @pl.when((h == 0) & (ki == 0) & (qi == 0))
            def _():
                bg.start(bg_in, bg_out, bg_sems)

        @pl.when(qi == 0)
        def _():
            dk_sc[...] = jnp.zeros_like(dk_sc)
            dv_sc[...] = jnp.zeros_like(dv_sc)

        def step(masked):
            qv = q_ref[...]
            k = jnp.concatenate([kn_ref[...], kr_ref[...]], axis=1)
            st = lax.dot_general(k, qv, nt, preferred_element_type=F32)
            if masked:
                key = ki * t + lax.broadcasted_iota(jnp.int32, (t, t), 0)
                qry = qi * t + lax.broadcasted_iota(jnp.int32, (t, t), 1)
                st = jnp.where((key <= qry) & (key >= npad), st, NEG_INF)
            pt = jnp.exp(st - lse_ref[0:1, :])
            dob = do_ref[...].astype(BF16)
            dv_sc[...] += jnp.dot(pt.astype(BF16), dob, preferred_element_type=F32)
            dpt = lax.dot_general(v_ref[...], dob, nt, preferred_element_type=F32)
            dst = (pt * (dpt - dl_ref[0:1, :])).astype(BF16)
            dk_sc[...] += jnp.dot(dst, qv, preferred_element_type=F32)
            dqc = lax.dot_general(dst, k, tn, preferred_element_type=F32)
            rows = pl.ds(pl.multiple_of(qi * t, t), t)

            @pl.when(ki == 0)
            def _():
                dq_ref[rows, :] = dqc

            @pl.when(ki > 0)
            def _():
                dq_ref[rows, :] += dqc

        need_mask = (ki == qi) | (ki == 0)

        @pl.when((qi >= ki) & need_mask)
        def _():
            step(True)

        @pl.when((qi >= ki) & jnp.logical_not(need_mask))
        def _():
            step(False)

        @pl.when(qi == nb - 1)
        def _():
            dkn_ref[...] = dk_sc[:, :QK_NOPE].astype(dkn_ref.dtype)
            dkr_ref[...] = dk_sc[:, QK_NOPE:].astype(dkr_ref.dtype)
            dv_ref[...] = dv_sc[...].astype(dv_ref.dtype)

        if bg:
            @pl.when((h == HEADS - 1) & (ki == nb - 1) & (qi == nb - 1))
            def _():
                bg.wait(bg_in, bg_out, bg_sems)

    qmap = lambda h, ki, qi: (jnp.maximum(qi, ki), h)
    kmap = lambda h, ki, qi: (ki, h)
    est = 2 * Tp * QHEAD * 4 + 8 * t * t * 4
    return pl.pallas_call(
        body, name=name, grid=(HEADS, nb, nb),
        in_specs=[pl.BlockSpec((t, QHEAD), qmap),
                  pl.BlockSpec((t, QK_NOPE), kmap),
                  pl.BlockSpec((t, LANES), kmap),
                  pl.BlockSpec((t, V_HEAD), lambda h, ki, qi: (ki, HEADS + h)),
                  pl.BlockSpec((t, V_HEAD), qmap),
                  pl.BlockSpec((SUBLANES, t), lambda h, ki, qi: (h, jnp.maximum(qi, ki))),
                  pl.BlockSpec((SUBLANES, t), lambda h, ki, qi: (h, jnp.maximum(qi, ki)))] + (bg.specs if bg else []),
        out_specs=[pl.BlockSpec((Tp, QHEAD), lambda h, ki, qi: (0, h)),
                   pl.BlockSpec((t, QK_NOPE), kmap),
                   pl.BlockSpec((t, LANES), kmap),
                   pl.BlockSpec((t, V_HEAD), kmap)] + (bg.specs if bg else []),
        out_shape=[jax.ShapeDtypeStruct((Tp, HEADS * QHEAD), F32),
                   jax.ShapeDtypeStruct((Tp, HEADS * QK_NOPE), BF16),
                   jax.ShapeDtypeStruct((Tp, HEADS * LANES), F32),
                   jax.ShapeDtypeStruct((Tp, HEADS * V_HEAD), BF16)] + (bg.out_shape if bg else []),
        scratch_shapes=[pltpu.VMEM((t, QHEAD), F32), pltpu.VMEM((t, V_HEAD), F32)] + (bg.scratch if bg else []),
        compiler_params=_params(("arbitrary",) * 3 if bg else ("parallel", "arbitrary", "arbitrary"), est),
    )(q, kv, kr8, kv, do, lse, delta, *(bg.arrs if bg else []))


GW = SSD_INNER // SSD_GROUPS
PAIRS_PER_GROUP = GW // LANES
XB = SSD_INNER // GW
NT_DIMS = (((1,), (1,)), ((), ()))
TN_DIMS = (((0,), (0,)), ((), ()))


def _ssd_common(xs_ref, dt_ref, alog_ref, e_ref):
    a_neg = -jnp.exp(alog_ref[...])
    dt = dt_ref[...]
    li = lax.broadcasted_iota(jnp.int32, (CHUNK, CHUNK), 0)
    si = lax.broadcasted_iota(jnp.int32, (CHUNK, CHUNK), 1)
    tril = li >= si
    tri = tril.astype(F32)
    acs = jnp.dot(tri, dt * a_neg, precision=SSD_PREC, preferred_element_type=F32)
    e = e_ref[...]
    dte = jnp.dot(dt, e, precision=SSD_PREC, preferred_element_type=F32)
    acse = jnp.dot(acs, e, precision=SSD_PREC, preferred_element_type=F32)
    x = xs_ref[...] * dte
    alast = acse[CHUNK - 1:CHUNK, :]
    return dict(a_neg=a_neg, dt=dt, tril=tril, tri=tri, acs=acs, acs_t=acs.T, e=e, dte=dte, acse=acse, x=x,
                p_e=jnp.exp(acse), w_e=jnp.exp(alast - acse), dl_e=jnp.exp(alast), li=li, si=si)


def _decay(cm, head):
    col = cm["acs"][:, head:head + 1]
    row = cm["acs_t"][head:head + 1, :]
    return jnp.exp(jnp.where(cm["tril"], col - row, -jnp.inf))


def _ssd_fwd(xbc, dt, alog, e, name):
    Tp = xbc.shape[0]
    nc = Tp // CHUNK

    def body(xs_ref, b_ref, c_ref, dt_ref, alog_ref, e_ref, y_ref, st_ref, st_sc):
        @pl.when(pl.program_id(0) == 0)
        def _():
            st_sc[...] = jnp.zeros_like(st_sc)

        cm = _ssd_common(xs_ref, dt_ref, alog_ref, e_ref)
        st_ref[0] = st_sc[...]
        lane = lax.broadcasted_iota(jnp.int32, (CHUNK, LANES), 1)
        for g in range(SSD_GROUPS):
            gs = slice(g * GW, (g + 1) * GW)
            cg = c_ref[:, g * SSD_STATE:(g + 1) * SSD_STATE].astype(BF16)
            bg = b_ref[:, g * SSD_STATE:(g + 1) * SSD_STATE].astype(BF16)
            cb = lax.dot_general(cg, bg, NT_DIMS, preferred_element_type=F32)
            stg = st_sc[:, gs]
            yoff = jnp.dot(cg, stg.astype(BF16), preferred_element_type=F32) * cm["p_e"][:, gs]
            xg = cm["x"][:, gs]
            for jp in range(PAIRS_PER_GROUP):
                j = g * PAIRS_PER_GROUP + jp
                xp = xg[:, jp * LANES:(jp + 1) * LANES].astype(BF16)
                ys = []
                for head in (2 * j, 2 * j + 1):
                    m = (cb * _decay(cm, head)).astype(BF16)
                    ys.append(jnp.dot(m, xp, preferred_element_type=F32))
                y_ref[:, j * LANES:(j + 1) * LANES] = (jnp.where(lane < SSD_HEAD_DIM, ys[0], ys[1])
                                                       + yoff[:, jp * LANES:(jp + 1) * LANES])
            snew = lax.dot_general(bg, (cm["w_e"][:, gs] * xg).astype(BF16), TN_DIMS, preferred_element_type=F32)
            st_sc[:, gs] = cm["dl_e"][:, gs] * stg + snew

    return pl.pallas_call(
        body, name=name, grid=(nc,),
        in_specs=[pl.BlockSpec((CHUNK, SSD_INNER), lambda c: (c, 0)),
                  pl.BlockSpec((CHUNK, GW), lambda c: (c, XB)),
                  pl.BlockSpec((CHUNK, GW), lambda c: (c, XB + 1)),
                  pl.BlockSpec((CHUNK, LANES), lambda c: (c, 0)),
                  pl.BlockSpec((1, LANES), lambda c: (0, 0)),
                  pl.BlockSpec((LANES, SSD_INNER), lambda c: (0, 0))],
        out_specs=[pl.BlockSpec((CHUNK, SSD_INNER), lambda c: (c, 0)),
                   pl.BlockSpec((1, SSD_STATE, SSD_INNER), lambda c: (c, 0, 0))],
        out_shape=[jax.ShapeDtypeStruct((Tp, SSD_INNER), F32), jax.ShapeDtypeStruct((nc, SSD_STATE, SSD_INNER), F32)],
        scratch_shapes=[pltpu.VMEM((SSD_STATE, SSD_INNER), F32)],
        compiler_params=_params(("arbitrary",), 24 * CHUNK * SSD_INNER * 4),
    )(xbc, xbc, xbc, dt, alog, e)


def _ssd_bwd(xbc, dt, alog, e, dy, dxs_skip, states, name):
    Tp = xbc.shape[0]
    nc = Tp // CHUNK
    rev = lambda c: nc - 1 - c

    def body(xs_ref, b_ref, c_ref, dt_ref, alog_ref, e_ref, dy_ref, skip_ref, st_ref,
             dxbc_ref, ddt_ref, dalog_ref, dst_sc, dx_sc, t_sc, tw_sc):
        @pl.when(pl.program_id(0) == 0)
        def _():
            dst_sc[...] = jnp.zeros_like(dst_sc)
            dalog_ref[...] = jnp.zeros_like(dalog_ref)

        cm = _ssd_common(xs_ref, dt_ref, alog_ref, e_ref)
        lane = lax.broadcasted_iota(jnp.int32, (CHUNK, LANES), 1)
        dacs_col = jnp.zeros((CHUNK, LANES), F32)
        dacs_row = jnp.zeros((LANES, CHUNK), F32)
        t_last = []
        for g in range(SSD_GROUPS):
            gs = slice(g * GW, (g + 1) * GW)
            cg = c_ref[:, g * SSD_STATE:(g + 1) * SSD_STATE].astype(BF16)
            bg = b_ref[:, g * SSD_STATE:(g + 1) * SSD_STATE].astype(BF16)
            stg = st_ref[0, :, gs]
            stg_b = stg.astype(BF16)
            dstg = dst_sc[:, gs]
            dstg_b = dstg.astype(BF16)
            xg = cm["x"][:, gs]
            dyg = dy_ref[:, gs]
            zg = jnp.dot(cg, stg_b, preferred_element_type=F32)
            dzg = dyg * cm["p_e"][:, gs]
            dzg_b = dzg.astype(BF16)
            dcg = lax.dot_general(dzg_b, stg_b, NT_DIMS, preferred_element_type=F32)
            dst_in = lax.dot_general(cg, dzg_b, TN_DIMS, preferred_element_type=F32)
            dst_in = dst_in + cm["dl_e"][:, gs] * dstg
            t_last.append(jnp.sum(dstg * stg * cm["dl_e"][:, gs], axis=0, keepdims=True))
            weg = cm["w_e"][:, gs]
            dbg = lax.dot_general((weg * xg).astype(BF16), dstg_b, NT_DIMS, preferred_element_type=F32)
            gg = jnp.dot(bg, dstg_b, preferred_element_type=F32)
            dxg = weg * gg
            tw_sc[:, gs] = xg * dxg
            t_sc[:, gs] = dzg * zg - xg * dxg
            cb = lax.dot_general(cg, bg, NT_DIMS, preferred_element_type=F32)
            dcb = jnp.zeros((CHUNK, CHUNK), F32)
            for jp in range(PAIRS_PER_GROUP):
                j = g * PAIRS_PER_GROUP + jp
                ps = slice(jp * LANES, (jp + 1) * LANES)
                xp = xg[:, ps].astype(BF16)
                dyp = dyg[:, ps]
                dxp = dxg[:, ps]
                for half, head in enumerate((2 * j, 2 * j + 1)):
                    lam = _decay(cm, head)
                    m32 = cb * lam
                    sel = (lane < SSD_HEAD_DIM) if half == 0 else (lane >= SSD_HEAD_DIM)
                    dye = jnp.where(sel, dyp, 0.0).astype(BF16)
                    dm = lax.dot_general(dye, xp, NT_DIMS, preferred_element_type=F32)
                    w = dm * m32
                    dacs_col = dacs_col + jnp.where(cm["si"] == head, jnp.sum(w, axis=1, keepdims=True), 0.0)
                    dacs_row = dacs_row + jnp.where(cm["li"] == head, jnp.sum(w, axis=0, keepdims=True), 0.0)
                    dcb = dcb + dm * lam
                    dxp = dxp + lax.dot_general(m32.astype(BF16), dye, TN_DIMS, preferred_element_type=F32)
                dx_sc[:, j * LANES:(j + 1) * LANES] = dxp
            dcb_b = dcb.astype(BF16)
            dcg = dcg + jnp.dot(dcb_b, bg, preferred_element_type=F32)
            dbg = dbg + lax.dot_general(dcb_b, cg, TN_DIMS, preferred_element_type=F32)
            dst_sc[:, gs] = dst_in
            dxbc_ref[:, SSD_INNER + g * SSD_STATE:SSD_INNER + (g + 1) * SSD_STATE] = dbg
            dxbc_ref[:, SSD_INNER + GW + g * SSD_STATE:SSD_INNER + GW + (g + 1) * SSD_STATE] = dcg
        e = cm["e"]
        dacs = lax.dot_general(t_sc[...], e, NT_DIMS, precision=SSD_PREC, preferred_element_type=F32)
        dacs = dacs + dacs_col - dacs_row.T
        last_lane = jnp.concatenate(t_last, axis=1) + jnp.sum(tw_sc[...], axis=0, keepdims=True)
        last_head = lax.dot_general(jnp.broadcast_to(last_lane, (SUBLANES, SSD_INNER)), e, NT_DIMS,
                                    precision=SSD_PREC, preferred_element_type=F32)[0:1, :]
        dacs = dacs + jnp.where(cm["li"] == CHUNK - 1, last_head, 0.0)
        da = lax.dot_general(cm["tri"], dacs, TN_DIMS, precision=SSD_PREC, preferred_element_type=F32)
        dx_all = dx_sc[...]
        ddt = da * cm["a_neg"] + lax.dot_general(dx_all * xs_ref[...], e, NT_DIMS, precision=SSD_PREC,
                                                 preferred_element_type=F32)
        ddt_ref[...] = ddt
        dxbc_ref[:, :SSD_INNER] = dx_all * cm["dte"] + skip_ref[...]
        dalog_ref[0:1, :] += jnp.sum(da * cm["dt"], axis=0, keepdims=True) * cm["a_neg"]

    return pl.pallas_call(
        body, name=name, grid=(nc,),
        in_specs=[pl.BlockSpec((CHUNK, SSD_INNER), lambda c: (rev(c), 0)),
                  pl.BlockSpec((CHUNK, GW), lambda c: (rev(c), XB)),
                  pl.BlockSpec((CHUNK, GW), lambda c: (rev(c), XB + 1)),
                  pl.BlockSpec((CHUNK, LANES), lambda c: (rev(c), 0)),
                  pl.BlockSpec((1, LANES), lambda c: (0, 0)),
                  pl.BlockSpec((LANES, SSD_INNER), lambda c: (0, 0)),
                  pl.BlockSpec((CHUNK, SSD_INNER), lambda c: (rev(c), 0)),
                  pl.BlockSpec((CHUNK, SSD_INNER), lambda c: (rev(c), 0)),
                  pl.BlockSpec((1, SSD_STATE, SSD_INNER), lambda c: (rev(c), 0, 0))],
        out_specs=[pl.BlockSpec((CHUNK, SSD_CONV_DIM), lambda c: (rev(c), 0)),
                   pl.BlockSpec((CHUNK, LANES), lambda c: (rev(c), 0)),
                   pl.BlockSpec((SUBLANES, LANES), lambda c: (0, 0))],
        out_shape=[jax.ShapeDtypeStruct((Tp, SSD_CONV_DIM), F32), jax.ShapeDtypeStruct((Tp, LANES), F32),
                   jax.ShapeDtypeStruct((SUBLANES, LANES), F32)],
        scratch_shapes=[pltpu.VMEM((SSD_STATE, SSD_INNER), F32), pltpu.VMEM((CHUNK, SSD_INNER), F32),
                        pltpu.VMEM((CHUNK, SSD_INNER), F32), pltpu.VMEM((CHUNK, SSD_INNER), F32)],
        compiler_params=_params(("arbitrary",), 32 * CHUNK * SSD_INNER * 4),
    )(xbc, xbc, xbc, dt, alog, e, dy, dxs_skip, states)


def _loss_head(h, target, name):
    Tp, d = h.shape
    nt = Tp // LANES

    def body(h_ref, t_ref, dh_ref, l_ref):
        real = pl.program_id(0) > 0
        err = jnp.where(real, h_ref[...] - t_ref[...], 0.0)
        dh_ref[...] = err * (1.0 / d)
        l_ref[...] = jnp.broadcast_to(0.5 * jnp.sum(err * err) * (1.0 / d), l_ref.shape)

    return pl.pallas_call(
        body, name=name, grid=(nt,),
        in_specs=[pl.BlockSpec((LANES, d), lambda i: (i, 0)),
                  pl.BlockSpec((LANES, d), lambda i: (jnp.maximum(i - 1, 0), 0))],
        out_specs=[pl.BlockSpec((LANES, d), lambda i: (i, 0)),
                   pl.BlockSpec((1, SUBLANES, LANES), lambda i: (i, 0, 0))],
        out_shape=[jax.ShapeDtypeStruct((Tp, d), F32), jax.ShapeDtypeStruct((nt, SUBLANES, LANES), F32)],
        compiler_params=_params(("parallel",), 8 * LANES * d * 4),
    )(h, target)


def _adamw(parts, w, m, v, name):
    shape = w.shape
    C = shape[-1]
    R = int(np.prod(shape[:-1]))
    npart = parts.shape[0]
    parts, w, m, v = parts.reshape(npart, R, C), w.reshape(R, C), m.reshape(R, C), v.reshape(R, C)
    lanes = -(-C // LANES) * LANES
    tr = _pick(R, max(BF16_ROWS, ADAM_ELEMS // lanes), BF16_ROWS) if R % BF16_ROWS == 0 else R
    c1 = 1.0 / (1.0 - ADAM_B1 ** ADAM_STEP)
    c2 = 1.0 / (1.0 - ADAM_B2 ** ADAM_STEP)

    def body(p_ref, w_ref, m_ref, v_ref, g_out, d_out, m_out, v_out):
        g = p_ref[0].astype(F32)
        for p in range(1, npart):
            g = g + p_ref[p].astype(F32)
        m_new = ADAM_B1 * m_ref[...] + (1.0 - ADAM_B1) * g
        v_new = ADAM_B2 * v_ref[...] + (1.0 - ADAM_B2) * (g * g)
        g_out[...] = g
        m_out[...] = m_new
        v_out[...] = v_new
        d_out[...] = -ADAM_LR * ((m_new * c1) / (jnp.sqrt(v_new * c2) + ADAM_EPS) + ADAM_WD * w_ref[...])

    spec = pl.BlockSpec((tr, C), lambda i: (i, 0))
    est = npart * tr * lanes * parts.dtype.itemsize + 7 * tr * lanes * 4
    res = pl.pallas_call(
        body, name=name, grid=(R // tr,),
        in_specs=[pl.BlockSpec((npart, tr, C), lambda i: (0, i, 0)), spec, spec, spec],
        out_specs=[spec] * 4, out_shape=[jax.ShapeDtypeStruct((R, C), F32)] * 4,
        compiler_params=_params(("parallel",), est),
    )(parts, w, m, v)
    return [r.reshape(shape) for r in res]


MESH_ID = pl.DeviceIdType.MESH
N_PEERS = N_DEV - 1


def _dev_index(p):
    return 4 * p[0] + 2 * p[1] + p[2]


class _Background:
    def __init__(self, kind, arrs):
        self.kind, self.arrs, self.n = kind, list(arrs), len(arrs)
        self.npairs = N_PEERS if kind == "gather" else N_CHIPS - 1
        lead = (N_DEV,) if kind == "gather" else ()
        self.out_shape = [jax.ShapeDtypeStruct(lead + a.shape, a.dtype) for a in self.arrs]
        self.specs = [pl.BlockSpec(memory_space=pl.ANY)] * self.n
        self.scratch = [pltpu.SemaphoreType.DMA((self.n, self.npairs)), pltpu.SemaphoreType.DMA((self.n, self.npairs)),
                        pltpu.SemaphoreType.DMA((self.n,))]

    def copies(self, in_refs, out_refs, sems):
        send_sems, recv_sems, local_sems = sems
        x, y, c = lax.axis_index("x"), lax.axis_index("y"), lax.axis_index("c")
        sends, recvs, locals_ = [], [], []

        def remote(t, k, src, dst, to):
            return pltpu.make_async_remote_copy(src_ref=src, dst_ref=dst, send_sem=send_sems.at[t, k],
                                                recv_sem=recv_sems.at[t, k], device_id=to, device_id_type=MESH_ID)

        if self.kind == "gather":
            me = _dev_index((x, y, c))
            peers = [(x, y, 1 - c), (1 - x, y, c), (x, 1 - y, c), (1 - x, 1 - y, c),
                     (1 - x, y, 1 - c), (x, 1 - y, 1 - c), (1 - x, 1 - y, 1 - c)]
            for t in range(self.n):
                locals_.append(pltpu.make_async_copy(in_refs[t], out_refs[t].at[me], local_sems.at[t]))
                for k, p in enumerate(peers):
                    sends.append(remote(t, k, in_refs[t], out_refs[t].at[me], p))
                    recvs.append(remote(t, k, in_refs[t], out_refs[t].at[_dev_index(p)], p))
        else:
            mine = 2 * x + y
            peers = [(1 - x, y), (x, 1 - y), (1 - x, 1 - y)]
            for t in range(self.n):
                locals_.append(pltpu.make_async_copy(in_refs[t].at[mine], out_refs[t].at[mine], local_sems.at[t]))
                for k, p in enumerate(peers):
                    theirs = 2 * p[0] + p[1]
                    sends.append(remote(t, k, in_refs[t].at[theirs], out_refs[t].at[mine], (*p, c)))
                    recvs.append(remote(t, k, in_refs[t].at[mine], out_refs[t].at[theirs], (*p, c)))
        return sends, recvs, locals_

    def start(self, in_refs, out_refs, sems):
        sends, _, locals_ = self.copies(in_refs, out_refs, sems)
        for cp in locals_ + sends:
            cp.start()

    def wait(self, in_refs, out_refs, sems):
        sends, recvs, locals_ = self.copies(in_refs, out_refs, sems)
        for cp in recvs:
            cp.wait_recv()
        for cp in sends:
            cp.wait_send()
        for cp in locals_:
            cp.wait()


def _comm_call(body, name, arrs, out_shape, npairs):
    n = len(arrs)
    any_spec = pl.BlockSpec(memory_space=pl.ANY)
    return pl.pallas_call(
        functools.partial(body, n), name=name, in_specs=[any_spec] * n, out_specs=[any_spec] * n, out_shape=out_shape,
        scratch_shapes=[pltpu.SemaphoreType.DMA((n, npairs)), pltpu.SemaphoreType.DMA((n, npairs)),
                        pltpu.SemaphoreType.DMA((n,))],
    )(*arrs)


def _allgather(arrs, name):
    def body(n, *refs):
        src_refs, out_refs = refs[:n], refs[n:2 * n]
        send_sems, recv_sems, local_sems = refs[2 * n:]
        x, y, c = lax.axis_index("x"), lax.axis_index("y"), lax.axis_index("c")
        me, sibling = (x, y, c), (x, y, 1 - c)
        chips = [(1 - x, y), (x, 1 - y), (1 - x, 1 - y)]

        def copy(t, k, block, to, src=None):
            slot = out_refs[t].at[_dev_index(block)]
            return pltpu.make_async_remote_copy(
                src_ref=slot if src is None else src, dst_ref=slot,
                send_sem=send_sems.at[t, k], recv_sem=recv_sems.at[t, k],
                device_id=to, device_id_type=MESH_ID)

        sends, locals_ = [], []
        for t in range(n):
            mine = pltpu.make_async_copy(src_refs[t], out_refs[t].at[_dev_index(me)], local_sems.at[t])
            mine.start()
            locals_.append(mine)
            first = [copy(t, 0, me, sibling, src=src_refs[t])]
            first += [copy(t, 1 + j, me, (*chip, c), src=src_refs[t]) for j, chip in enumerate(chips)]
            for cp in first:
                cp.start()
            sends += first
        for j, chip in enumerate(chips):
            for t in range(n):
                copy(t, 1 + j, (*chip, c), me).wait_recv()
                passed = copy(t, 4 + j, (*chip, c), sibling)
                passed.start()
                sends.append(passed)
        for t in range(n):
            copy(t, 0, sibling, me).wait_recv()
            for j, chip in enumerate(chips):
                copy(t, 4 + j, (*chip, 1 - c), me).wait_recv()
        for cp in sends:
            cp.wait_send()
        for cp in locals_:
            cp.wait()

    return _comm_call(body, name, arrs, [jax.ShapeDtypeStruct((N_DEV,) + a.shape, a.dtype) for a in arrs], N_PEERS)


N_CHIPS = N_DEV // 2
CHIPS = [(0, 0), (0, 1), (1, 0), (1, 1)]


def _sibling_exchange(arrs, name):
    def body(n, *refs):
        in_refs, out_refs = refs[:n], refs[n:2 * n]
        send_sems, recv_sems, _ = refs[2 * n:]
        x, y, c = lax.axis_index("x"), lax.axis_index("y"), lax.axis_index("c")
        sibling = (x, y, 1 - c)

        def copy(t, j):
            return pltpu.make_async_remote_copy(
                src_ref=in_refs[t].at[_dev_index((*CHIPS[j], 1 - c))], dst_ref=out_refs[t].at[j],
                send_sem=send_sems.at[t, j], recv_sem=recv_sems.at[t, j],
                device_id=sibling, device_id_type=MESH_ID)

        copies = [copy(t, j) for t in range(n) for j in range(N_CHIPS)]
        for cp in copies:
            cp.start()
        for cp in copies:
            cp.wait_recv()
        for cp in copies:
            cp.wait_send()

    return _comm_call(body, name, arrs, [jax.ShapeDtypeStruct((N_CHIPS,) + a.shape[1:], a.dtype) for a in arrs], N_CHIPS)


def _chip_exchange(arrs, name):
    def body(n, *refs):
        in_refs, out_refs = refs[:n], refs[n:2 * n]
        send_sems, recv_sems, local_sems = refs[2 * n:]
        x, y, c = lax.axis_index("x"), lax.axis_index("y"), lax.axis_index("c")
        mine = 2 * x + y
        peers = [(1 - x, y), (x, 1 - y), (1 - x, 1 - y)]

        def copy(t, k, src_chip, dst_chip, to):
            return pltpu.make_async_remote_copy(
                src_ref=in_refs[t].at[src_chip], dst_ref=out_refs[t].at[dst_chip],
                send_sem=send_sems.at[t, k], recv_sem=recv_sems.at[t, k],
                device_id=(*to, c), device_id_type=MESH_ID)

        sends, locals_ = [], []
        for t in range(n):
            own = pltpu.make_async_copy(in_refs[t].at[mine], out_refs[t].at[mine], local_sems.at[t])
            own.start()
            locals_.append(own)
            for k, p in enumerate(peers):
                cp = copy(t, k, 2 * p[0] + p[1], mine, p)
                cp.start()
                sends.append(cp)
        for t in range(n):
            for k, p in enumerate(peers):
                copy(t, k, mine, 2 * p[0] + p[1], p).wait_recv()
        for cp in sends:
            cp.wait_send()
        for cp in locals_:
            cp.wait()

    return _comm_call(body, name, arrs, [jax.ShapeDtypeStruct(a.shape, a.dtype) for a in arrs], N_CHIPS - 1)


def _add_pairs(a, b, name):
    shape = a.shape
    C = shape[-1]
    R = int(np.prod(shape[:-1]))
    lanes = -(-C // LANES) * LANES
    tr = _pick(R, max(BF16_ROWS, 2 * ADAM_ELEMS // lanes), BF16_ROWS) if R % BF16_ROWS == 0 else R

    def body(a_ref, b_ref, o_ref):
        o_ref[...] = (a_ref[...].astype(F32) + b_ref[...].astype(F32)).astype(o_ref.dtype)

    spec = pl.BlockSpec((tr, C), lambda i: (i, 0))
    return pl.pallas_call(
        body, name=name, grid=(R // tr,), in_specs=[spec, spec], out_specs=spec,
        out_shape=jax.ShapeDtypeStruct((R, C), a.dtype),
        compiler_params=_params(("parallel",), 3 * tr * lanes * 4),
    )(a.reshape(R, C), b.reshape(R, C)).reshape(shape)


WEIGHTS = ['meta_tokens', 'emb_ln_g', 'emb_ln_b', 'w_in', 'q_norm_g', 'w_q_b', 'kv_norm_g', 'w_kv_b', 'w_o_attn',
           'ssd_conv_w', 'ssd_conv_b', 'dt_bias', 'a_log', 'd_skip', 'ssd_norm_g', 'w_o_ssd', 'w_out', 'ln1_g',
           'ln1_b', 'w_up', 'ffn_conv_w', 'ffn_conv_b', 'w_down', 'ln2_g', 'ln2_b']
BIG = {'w_in': 2, 'w_q_b': 2, 'w_kv_b': 2, 'w_o_attn': 1, 'w_o_ssd': 1, 'w_out': 1, 'w_up': 2, 'w_down': 1}
SMALL_SHARDED = {'meta_tokens': 1, 'ssd_conv_w': 2, 'ffn_conv_w': 2}
REPLICATED = [n for n in WEIGHTS if n not in BIG and n not in SMALL_SHARDED]
FIRST_USED = ['w_in', 'w_q_b', 'w_kv_b']
AFTER_ATTENTION = [n for n in BIG if n not in FIRST_USED]
BIG_COLS = 1024
SMALL_COLS = LANES


def _flatten(arrs, cols, row_mult, lead=False):
    parts, offs, off = [], [], 0
    for a in arrs:
        a2 = a.reshape(N_DEV, -1) if lead else a.reshape(1, -1)
        n = a2.shape[1]
        pad = -n % cols
        parts.append(jnp.pad(a2, ((0, 0), (0, pad))))
        offs.append((off, n))
        off += n + pad
    rows = off // cols
    extra = (-rows % row_mult) * cols
    if extra:
        parts.append(jnp.zeros((parts[0].shape[0], extra), parts[0].dtype))
    flat = jnp.concatenate(parts, axis=1)
    flat = flat.reshape(flat.shape[0], -1, cols)
    return (flat if lead else flat[0]), offs


def _unflatten(flat, offs, shapes):
    f = flat.reshape(-1)
    return [f[o:o + n].reshape(s) for (o, n), s in zip(offs, shapes)]


def _to_pieces(g, axis):
    s = g.shape[axis] // N_DEV
    g = g.reshape(g.shape[:axis] + (N_DEV, s) + g.shape[axis + 1:])
    return jnp.moveaxis(g, axis, 0).reshape(N_DEV, -1)


def _from_pieces(p, shard_shape, axis):
    g = jnp.moveaxis(p.reshape((N_DEV,) + tuple(shard_shape)), 0, axis)
    sh = list(shard_shape)
    sh[axis] *= N_DEV
    return g.reshape(sh)


def _in_proj_pad(w):
    e = np.cumsum((0,) + IN_SIZES)
    ql, kvl, kpe, z, xbc, dt, ga, gs = [w[:, e[j]:e[j + 1]] for j in range(8)]
    zc = lambda n: jnp.zeros((w.shape[0], n), w.dtype)
    return jnp.concatenate([ql, kvl, z, xbc, ga, gs, kpe, zc(LANES - QK_ROPE), dt, zc(LANES - SSD_HEADS)], axis=1)


def _in_proj_unpad(d):
    seg = lambda o, n: d[:, o:o + n]
    return jnp.concatenate([seg(OQ, Q_LORA), seg(OKV, KV_LORA), seg(OKPE, QK_ROPE), seg(OZ, SSD_INNER),
                            seg(OXBC, SSD_CONV_DIM), seg(ODT, SSD_HEADS), seg(OGA, D_MODEL), seg(OGS, D_MODEL)], axis=1)


def _q_pad(w):
    w3 = w.reshape(Q_LORA, HEADS, QK_NOPE + QK_ROPE)
    return jnp.concatenate([w3, jnp.zeros((Q_LORA, HEADS, QHEAD - QK_NOPE - QK_ROPE), w.dtype)], axis=2).reshape(Q_LORA, HEADS * QHEAD)


def _q_unpad(d):
    return d.reshape(Q_LORA, HEADS, QHEAD)[:, :, :QK_NOPE + QK_ROPE].reshape(Q_LORA, HEADS * (QK_NOPE + QK_ROPE))


def _kv_perm(w):
    w3 = w.reshape(KV_LORA, HEADS, QK_NOPE + V_HEAD)
    return jnp.concatenate([w3[:, :, :QK_NOPE].reshape(KV_LORA, -1), w3[:, :, QK_NOPE:].reshape(KV_LORA, -1)], axis=1)


def _kv_unperm(d):
    kn = d[:, :HEADS * QK_NOPE].reshape(KV_LORA, HEADS, QK_NOPE)
    v = d[:, HEADS * QK_NOPE:].reshape(KV_LORA, HEADS, V_HEAD)
    return jnp.concatenate([kn, v], axis=2).reshape(KV_LORA, HEADS * (QK_NOPE + V_HEAD))


def _row_vec(v, width=None):
    v = v.reshape(1, -1).astype(F32)
    if width is not None and v.shape[1] < width:
        v = jnp.pad(v, ((0, 0), (0, width - v.shape[1])))
    return v


def _pad_rows8(w):
    return jnp.pad(w.astype(F32), ((0, SUBLANES - w.shape[0]), (0, 0)))


def _tables(Tp, npad):
    pos = jnp.maximum(jnp.arange(Tp, dtype=jnp.int32) - npad, 0).astype(F32)
    inv_freq = 1.0 / (ROPE_THETA ** (jnp.arange(0, QK_ROPE, 2, dtype=F32) / QK_ROPE))
    ang = pos[:, None] * inv_freq[None, :]
    ang = jnp.concatenate([ang, ang], axis=-1)
    zeros = jnp.zeros((Tp, LANES - QK_ROPE), F32)
    cos = jnp.concatenate([jnp.cos(ang), zeros], axis=1)
    sin = jnp.concatenate([jnp.sin(ang), zeros], axis=1)
    rot = np.zeros((LANES, LANES), np.float32)
    half = QK_ROPE // 2
    for i in range(half):
        rot[i + half, i] = -1.0
        rot[i, i + half] = 1.0
    expand = np.zeros((LANES, SSD_INNER), np.float32)
    for hd in range(SSD_HEADS):
        expand[hd, hd * SSD_HEAD_DIM:(hd + 1) * SSD_HEAD_DIM] = 1.0
    return cos, sin, jnp.asarray(rot), jnp.asarray(expand)


def _layer_rows(proj, tb):
    rows_a = [_row(proj, Q_LORA, OQ // Q_LORA), _row(proj, KV_LORA, OKV // KV_LORA), _row(proj, LANES, OKPE // LANES),
              _row(proj, LANES, ODT // LANES), _row(tb["cos"], diff=False), _row(tb["sin"], diff=False)]
    return rows_a


def _layer_fwd(h, h_bf, P, tb, fns, npad, bg=None, on_carried=None):
    both = [_out(D_MODEL, F32), _out(D_MODEL, BF16)]
    res_ln_twice = lambda *a: fns["res_ln"](*a) * 2
    proj = _mm(h_bf, P["w_in"], F32, "in_proj")
    rows_a = _layer_rows(proj, tb)
    consts_a = [_row(tb["rot"], diff=False), _row(P["q_norm_g"]), _row(P["kv_norm_g"]), _row(P["dt_bias"])]
    qn, kvn, kr8, dt = _rw_fwd(fns["in_post"], rows_a, consts_a,
                               [_out(Q_LORA, BF16), _out(KV_LORA, BF16), _out(HEADS * LANES, BF16), _out(LANES, F32)],
                               "in_post")
    q = _mm(qn, P["w_q"], F32, "q_proj")
    rows_q = [_row(q, QHEAD, 0, grp=True), _row(tb["cos"], diff=False), _row(tb["sin"], diff=False)]
    qr = _rw_fwd(fns["q_post"], rows_q, [_row(tb["rot"], diff=False)], [_out(HEADS * QHEAD, BF16, QHEAD, grp=True)],
                 "q_post", ng=HEADS)[0]
    kv = _mm(kvn, P["w_kv"], BF16, "kv_proj")
    o, lse, *carried = _flash_fwd(qr, kv, kr8, npad, "attn_fwd_gather" if bg else "attn_fwd", bg=bg)
    if on_carried is not None:
        carried = on_carried(P, carried)
    ya = _mm(o, P["w_o_attn"], F32, "attn_out")
    xbc = _conv_fwd(proj, OXBC, SSD_CONV_DIM, P["ssd_conv_w"], P["ssd_conv_b"], SSD_CONV, True, npad, "ssd_conv")
    y, states = _ssd_fwd(xbc, dt, P["a_log"], tb["expand"], "ssd_fwd")
    rows_b = [_row(y, GW, 0, grp=True), _row(xbc, GW, 0, grp=True), _row(proj, GW, OZ // GW, grp=True)]
    consts_b = [_row(P["d_skip"], GW, 0, grp=True), _row(P["ssd_norm_g"], GW, 0, grp=True)]
    yn = _rw_fwd(fns["gated"], rows_b, consts_b, [_out(SSD_INNER, BF16, GW, grp=True)], "ssd_gate", ng=SSD_GROUPS)[0]
    ys = _mm(yn, P["w_o_ssd"], F32, "ssd_out")
    rows_c = [_row(proj, D_MODEL, OGA // D_MODEL), _row(proj, D_MODEL, OGS // D_MODEL), _row(ya), _row(ys)]
    mixed = _rw_fwd(fns["mix"], rows_c, [], [_out(D_MODEL, BF16)], "mix")[0]
    mo = _mm(mixed, P["w_out"], F32, "mix_out")
    consts_1 = [_row(P["ln1_g"]), _row(P["ln1_b"])]
    h1, h1_bf = _rw_fwd(res_ln_twice, [_row(h), _row(mo)], consts_1, both, "ln1")
    up = _mm(h1_bf, P["w_up"], BF16, "ffn_up")
    u = _conv_fwd(up, 0, 2 * D_FF, P["ffn_conv_w"], P["ffn_conv_b"], FFN_CONV, False, npad, "ffn_conv", BF16)
    act = _rw_fwd(fns["glu"], [_row(u)], [], [_out(D_FF, BF16)], "ffn_glu")[0]
    fo = _mm(act, P["w_down"], F32, "ffn_down")
    consts_2 = [_row(P["ln2_g"]), _row(P["ln2_b"])]
    h2, h2_bf = _rw_fwd(res_ln_twice, [_row(h1), _row(fo)], consts_2, both, "ln2")
    res = dict(h=h, h_bf=h_bf, proj=proj, qn=qn, kvn=kvn, kr8=kr8, dt=dt, q=q, qr=qr, kv=kv, o=o, lse=lse, ya=ya,
               xbc=xbc, y=y, states=states, yn=yn, ys=ys, mixed=mixed, mo=mo, h1=h1, h1_bf=h1_bf, up=up, u=u, act=act,
               fo=fo)
    return h2, h2_bf, res, carried


def _layer_bwd(dh2, r, P, tb, fns, npad, bg=None, before_attn=None):
    g = {}
    consts_2 = [_row(P["ln2_g"]), _row(P["ln2_b"])]
    (dh1_a, dfo), (g["ln2_g"], g["ln2_b"]) = _rw_bwd(fns["res_ln"], [_row(r["h1"]), _row(r["fo"])], consts_2,
                                                     [_row(dh2)], [F32, BF16], "ln2_bwd")
    g["w_down"] = _mm(r["act"], dfo, BF16, "dw_down", ta=True)
    dact = _mm(dfo, P["w_down"], BF16, "d_act", tb=True)
    (du,), _ = _rw_bwd(fns["glu"], [_row(r["u"])], [], [_row(dact)], [BF16], "glu_bwd")
    dup, g["ffn_conv_w"], g["ffn_conv_b"] = _conv_bwd(r["up"], 0, 2 * D_FF, P["ffn_conv_w"], P["ffn_conv_b"], du,
                                                      FFN_CONV, False, npad, "ffn_conv_bwd")
    g["w_up"] = _mm(r["h1_bf"], dup, BF16, "dw_up", ta=True)
    dh1 = _mm(dup, P["w_up"], F32, "d_h1", tb=True, add=dh1_a)
    consts_1 = [_row(P["ln1_g"]), _row(P["ln1_b"])]
    (dh_a, dmo), (g["ln1_g"], g["ln1_b"]) = _rw_bwd(fns["res_ln"], [_row(r["h"]), _row(r["mo"])], consts_1,
                                                    [_row(dh1)], [F32, BF16], "ln1_bwd")
    g["w_out"] = _mm(r["mixed"], dmo, BF16, "dw_out", ta=True)
    dmixed = _mm(dmo, P["w_out"], F32, "d_mixed", tb=True)
    proj = r["proj"]
    rows_c = [_row(proj, D_MODEL, OGA // D_MODEL), _row(proj, D_MODEL, OGS // D_MODEL), _row(r["ya"]), _row(r["ys"])]
    (dga, dgs, dya, dys), _ = _rw_bwd(fns["mix"], rows_c, [], [_row(dmixed)], [BF16] * 4, "mix_bwd")
    g["w_o_attn"] = _mm(r["o"], dya, BF16, "dw_o_attn", ta=True)
    do = _mm(dya, P["w_o_attn"], F32, "d_o", tb=True)
    g["w_o_ssd"] = _mm(r["yn"], dys, BF16, "dw_o_ssd", ta=True)
    dyn = _mm(dys, P["w_o_ssd"], F32, "d_yn", tb=True)
    rows_b = [_row(r["y"], GW, 0, grp=True), _row(r["xbc"], GW, 0, grp=True), _row(proj, GW, OZ // GW, grp=True)]
    consts_b = [_row(P["d_skip"], GW, 0, grp=True), _row(P["ssd_norm_g"], GW, 0, grp=True)]
    (dy, dxs_skip, dz), (g["d_skip"], g["ssd_norm_g"]) = _rw_bwd(
        fns["gated"], rows_b, consts_b, [_row(dyn, GW, 0, grp=True)], [F32, F32, BF16], "ssd_gate_bwd", ng=SSD_GROUPS)
    dxbc, ddt, g["a_log"] = _ssd_bwd(r["xbc"], r["dt"], P["a_log"], tb["expand"], dy, dxs_skip, r["states"], "ssd_bwd")
    dxbc_pre, g["ssd_conv_w"], g["ssd_conv_b"] = _conv_bwd(proj, OXBC, SSD_CONV_DIM, P["ssd_conv_w"], P["ssd_conv_b"],
                                                           dxbc, SSD_CONV, True, npad, "ssd_conv_bwd")
    delta = _attn_delta(do, r["o"], "attn_delta")
    if before_attn is not None:
        bg = before_attn(g)
    dqr, dkn, dkr8, dv, *carried = _flash_bwd(r["qr"], r["kv"], r["kr8"], do, r["lse"], delta, npad,
                                              "attn_bwd_exchange" if bg else "attn_bwd", bg=bg)
    rows_q = [_row(r["q"], QHEAD, 0, grp=True), _row(tb["cos"], diff=False), _row(tb["sin"], diff=False)]
    (dq,), _ = _rw_bwd(fns["q_post"], rows_q, [_row(tb["rot"], diff=False)], [_row(dqr, QHEAD, 0, grp=True)], [BF16],
                       "q_post_bwd", ng=HEADS)
    g["w_q"] = _mm(r["qn"], dq, BF16, "dw_q", ta=True)
    dqn = _mm(dq, P["w_q"], F32, "d_qn", tb=True)
    dkv = jnp.concatenate([dkn, dv], axis=1)
    g["w_kv"] = _mm(r["kvn"], dkv, BF16, "dw_kv", ta=True)
    dkvn = _mm(dkv, P["w_kv"], F32, "d_kvn", tb=True)
    rows_a = _layer_rows(proj, tb)
    consts_a = [_row(tb["rot"], diff=False), _row(P["q_norm_g"]), _row(P["kv_norm_g"]), _row(P["dt_bias"])]
    (dql, dkvl, dkpe, ddtr), (g["q_norm_g"], g["kv_norm_g"], g["dt_bias"]) = _rw_bwd(
        fns["in_post"], rows_a, consts_a, [_row(dqn), _row(dkvn), _row(dkr8), _row(ddt)], [BF16] * 4, "in_post_bwd")
    dproj = jnp.concatenate([dql, dkvl, dz, dxbc_pre, dga, dgs, dkpe, ddtr], axis=1)
    g["w_in"] = _mm(r["h_bf"], dproj, BF16, "dw_in", ta=True)
    dh = _mm(dproj, P["w_in"], F32, "d_h", tb=True, add=dh_a)
    return dh, g, carried


def _full_weight(g, axis):
    if axis == 1:
        return g.reshape(-1, g.shape[-1])
    return jnp.transpose(g, (1, 0, 2)).reshape(g.shape[1], -1)


def _grad_pieces(d, axis):
    if axis == 1:
        return d.reshape(N_DEV, -1, d.shape[1])
    return jnp.transpose(d.reshape(d.shape[0], N_DEV, -1), (1, 0, 2))


def _big_params(gathered):
    prep = {"w_in": ("w_in", _in_proj_pad), "w_q_b": ("w_q", _q_pad), "w_kv_b": ("w_kv", _kv_perm)}
    P = {}
    for n, g in gathered.items():
        key, fn = prep.get(n, (n, lambda a: a))
        P[key] = fn(_full_weight(g, BIG[n]))
    return P


def _layer_params(gathered, small, i):
    P = _big_params(gathered)
    P["q_norm_g"] = _row_vec(small["q_norm_g"][i])
    P["kv_norm_g"] = _row_vec(small["kv_norm_g"][i])
    P["dt_bias"] = _row_vec(small["dt_bias"][i], LANES)
    P["a_log"] = _row_vec(small["a_log"][i], LANES)
    P["d_skip"] = _row_vec(jnp.repeat(small["d_skip"][i], SSD_HEAD_DIM))
    P["ssd_norm_g"] = _row_vec(small["ssd_norm_g"][i])
    P["ssd_conv_w"] = _pad_rows8(small["ssd_conv_w"][i])
    P["ssd_conv_b"] = _row_vec(small["ssd_conv_b"][i])
    P["ffn_conv_w"] = _pad_rows8(small["ffn_conv_w"][i])
    P["ffn_conv_b"] = _row_vec(small["ffn_conv_b"][i])
    for n in ("ln1_g", "ln1_b", "ln2_g", "ln2_b"):
        P[n] = _row_vec(small[n][i])
    return P


def _layer_grads_to_reference_layout(g):
    out = {}
    out["w_in"] = _in_proj_unpad(g["w_in"])
    out["w_q_b"] = _q_unpad(g["w_q"])
    out["w_kv_b"] = _kv_unperm(g["w_kv"])
    for n in ("w_o_attn", "w_o_ssd", "w_out", "w_up", "w_down"):
        out[n] = g[n]
    out["q_norm_g"] = g["q_norm_g"][0]
    out["kv_norm_g"] = g["kv_norm_g"][0]
    out["dt_bias"] = g["dt_bias"][0, :SSD_HEADS]
    out["a_log"] = g["a_log"][0, :SSD_HEADS]
    out["d_skip"] = g["d_skip"].reshape(SSD_HEADS, SSD_HEAD_DIM).sum(axis=1)
    out["ssd_norm_g"] = g["ssd_norm_g"][0]
    out["ssd_conv_w"] = g["ssd_conv_w"][:SSD_CONV]
    out["ssd_conv_b"] = g["ssd_conv_b"][0]
    out["ffn_conv_w"] = g["ffn_conv_w"][:FFN_CONV]
    out["ffn_conv_b"] = g["ffn_conv_b"][0]
    for n in ("ln1_g", "ln1_b", "ln2_g", "ln2_b"):
        out[n] = g[n][0]
    return out


def kernel(x, meta_tokens, emb_ln_g, emb_ln_b, w_in, q_norm_g, w_q_b, kv_norm_g, w_kv_b, w_o_attn, ssd_conv_w, ssd_conv_b, dt_bias, a_log, d_skip, ssd_norm_g, w_o_ssd, w_out, ln1_g, ln1_b, w_up, ffn_conv_w, ffn_conv_b, w_down, ln2_g, ln2_b, loss_target, m_meta_tokens, m_emb_ln_g, m_emb_ln_b, m_w_in, m_q_norm_g, m_w_q_b, m_kv_norm_g, m_w_kv_b, m_w_o_attn, m_ssd_conv_w, m_ssd_conv_b, m_dt_bias, m_a_log, m_d_skip, m_ssd_norm_g, m_w_o_ssd, m_w_out, m_ln1_g, m_ln1_b, m_w_up, m_ffn_conv_w, m_ffn_conv_b, m_w_down, m_ln2_g, m_ln2_b, v_meta_tokens, v_emb_ln_g, v_emb_ln_b, v_w_in, v_q_norm_g, v_w_q_b, v_kv_norm_g, v_w_kv_b, v_w_o_attn, v_ssd_conv_w, v_ssd_conv_b, v_dt_bias, v_a_log, v_d_skip, v_ssd_norm_g, v_w_o_ssd, v_w_out, v_ln1_g, v_ln1_b, v_w_up, v_ffn_conv_w, v_ffn_conv_b, v_w_down, v_ln2_g, v_ln2_b):
    given = dict(locals())
    w = {n: given[n] for n in WEIGHTS}
    m = {n: given["m_" + n] for n in WEIGHTS}
    v = {n: given["v_" + n] for n in WEIGHTS}
    seq = x.shape[1]
    assert x.shape[0] == 1 and seq % LANES == 0
    npad = LANES - N_META
    Tp = npad + N_META + seq
    depth = w_in.shape[0]

    big_names, small_names = list(BIG), list(SMALL_SHARDED)
    ws, offs_s = _flatten([w[n] for n in small_names], SMALL_COLS, SUBLANES)
    shards = [{n: w[n][i].astype(BF16) for n in big_names} for i in range(depth)]
    got = _allgather([shards[0][n] for n in FIRST_USED] + [ws], "weight_allgather")
    gathered = dict(zip(FIRST_USED, got[:-1]))
    gsm = got[-1]
    small = {n: w[n] for n in REPLICATED}
    for n, (o, sz) in zip(small_names, offs_s):
        small[n] = _from_pieces(gsm.reshape(N_DEV, -1)[:, o:o + sz], w[n].shape, SMALL_SHARDED[n])

    fns = _make_stage_fns(npad)
    cos, sin, rot, expand = _tables(Tp, npad)
    tb = dict(cos=cos, sin=sin, rot=rot, expand=expand)
    top = jnp.pad(small["meta_tokens"], ((npad, 0), (0, 0)))
    hcat = jnp.concatenate([top, x[0]], axis=0)
    consts_e = [_row(_row_vec(w["emb_ln_g"])), _row(_row_vec(w["emb_ln_b"]))]
    h, h_bf = _rw_fwd(lambda *a: fns["ln"](*a) * 2, [_row(hcat)], consts_e, [_out(D_MODEL, F32), _out(D_MODEL, BF16)],
                      "emb_ln")
    layers, saved = [], []
    for i in range(depth):
        layers.append(_layer_params(gathered, small, i))
        late = AFTER_ATTENTION if i == 0 else []
        nxt = big_names if i + 1 < depth else []
        arrs = [shards[i][n] for n in late] + [shards[i + 1][n] for n in nxt]

        def on_carried(P, carried, late=late):
            P.update(_big_params(dict(zip(late, carried[:len(late)]))))
            return carried[len(late):]

        h, h_bf, res, carried = _layer_fwd(h, h_bf, layers[i], tb, fns, npad,
                                           bg=_Background("gather", arrs) if arrs else None, on_carried=on_carried)
        gathered = dict(zip(nxt, carried))
        saved.append(res)
    dh, lparts = _loss_head(h, loss_target[0], "loss_head")
    loss = lax.psum(jnp.sum(lparts[:, 0, 0]), ("x", "y", "c"))

    core = lax.axis_index("c")

    def chip_partials(pieces, tag):
        from_sibling = _sibling_exchange(pieces, "grad_exchange_cores_" + tag)
        sums = []
        for k, (p, r) in enumerate(zip(pieces, from_sibling)):
            own = lax.dynamic_index_in_dim(p.reshape((N_CHIPS, 2) + p.shape[1:]), core, axis=1, keepdims=False)
            sums.append(_add_pairs(own, r, "grad_chip_sum_%s_%d" % (tag, k)))
        return sums

    lg, recv_big, pending = [None] * depth, [None] * depth, []
    for i in reversed(range(depth)):
        early = AFTER_ATTENTION if i == 0 else []

        def before_attn(g, pending=pending, early=early, i=i):
            sums = pending + (chip_partials([_grad_pieces(g[n], BIG[n]) for n in early], "l%d_early" % i) if early else [])
            return _Background("chips", sums) if sums else None

        dh, gi, carried = _layer_bwd(dh, saved[i], layers[i], tb, fns, npad, before_attn=before_attn)
        if pending:
            recv_big[i + 1] = dict(zip(big_names, carried[:len(pending)]))
        recv_big[i] = dict(zip(early, carried[len(pending):]))
        lg[i] = _layer_grads_to_reference_layout(gi)
        pending = []
        if i > 0:
            pending = chip_partials([_grad_pieces(lg[i][n], BIG[n]) for n in big_names], "l%d" % i)
    (dhcat,), (d_emb_g, d_emb_b) = _rw_bwd(fns["ln"], [_row(hcat)], consts_e, [_row(dh)], [F32], "emb_ln_bwd")
    grad_x = dhcat[LANES:][None]
    local = {n: jnp.stack([lg[i][n] for i in range(depth)]) for n in lg[0] if n not in BIG}
    local["meta_tokens"] = dhcat[npad:LANES]
    local["emb_ln_g"] = d_emb_g[0]
    local["emb_ln_b"] = d_emb_b[0]

    sm_names = small_names + REPLICATED
    sm_pieces = [_to_pieces(local[n], SMALL_SHARDED[n]) for n in small_names]
    sm_pieces += [jnp.broadcast_to(local[n].reshape(1, -1), (N_DEV, local[n].size)) for n in REPLICATED]
    ps, _ = _flatten(sm_pieces, SMALL_COLS, BF16_ROWS, lead=True)
    pieces = [_grad_pieces(lg[0][n], BIG[n]) for n in FIRST_USED] + [ps]
    recv = _chip_exchange(chip_partials(pieces, "l0"), "grad_exchange_chips")
    recv_big[0].update(zip(FIRST_USED, recv[:-1]))
    outs = {}
    kinds = ("grad", "delta", "new_m", "new_v")
    for n in big_names:
        parts = jnp.stack([recv_big[i][n] for i in range(depth)], axis=1)
        for kind, a in zip(kinds, _adamw(parts, w[n], m[n], v[n], "adamw_" + n)):
            outs[kind + "_" + n] = a
    wf, offs = _flatten([w[n] for n in sm_names], SMALL_COLS, BF16_ROWS)
    mf, _ = _flatten([m[n] for n in sm_names], SMALL_COLS, BF16_ROWS)
    vf, _ = _flatten([v[n] for n in sm_names], SMALL_COLS, BF16_ROWS)
    shapes = [w[n].shape for n in sm_names]
    for kind, flat in zip(kinds, _adamw(recv[-1], wf, mf, vf, "adamw_small")):
        for n, a in zip(sm_names, _unflatten(flat, offs, shapes)):
            outs[kind + "_" + n] = a
    result = [loss, grad_x]
    for kind in ("grad", "delta", "new_m", "new_v"):
        result += [outs[kind + "_" + n] for n in WEIGHTS]
    return tuple(result)
```

```python
import functools

import jax
import jax.numpy as jnp
import numpy as np
from jax import lax
from jax.experimental import pallas as pl
from jax.experimental.pallas import tpu as pltpu

F32 = jnp.float32
BF16 = jnp.bfloat16
HIGHEST = lax.Precision.HIGHEST
SSD_PREC = lax.Precision.HIGH

D_MODEL = 1024
DEPTH = 2
N_META = 16
HEADS = 8
Q_LORA = 768
KV_LORA = 256
QK_NOPE = 128
QK_ROPE = 64
V_HEAD = 128
ROPE_THETA = 10000.0
SSD_INNER = 2048
SSD_HEAD_DIM = 64
SSD_HEADS = 32
SSD_GROUPS = 4
SSD_STATE = 128
SSD_CONV = 4
SSD_CONV_DIM = SSD_INNER + 2 * SSD_GROUPS * SSD_STATE
CHUNK = 128
D_FF = 2816
FFN_CONV = 3
LN_EPS = 1e-5
RMS_EPS = 1e-6
ALPHA = (2 * DEPTH) ** 0.25
IN_SIZES = (Q_LORA, KV_LORA, QK_ROPE, SSD_INNER, SSD_CONV_DIM, SSD_HEADS, D_MODEL, D_MODEL)
ATT_SCALE = (QK_NOPE + QK_ROPE) ** -0.5
NEG_INF = -1e30
ADAM_LR, ADAM_B1, ADAM_B2, ADAM_EPS, ADAM_WD, ADAM_STEP = 0.001, 0.9, 0.999, 1e-08, 0.01, 10

LANES = 128
SUBLANES = 8
VMEM_BYTES = 64 * 1024 * 1024
N_DEV = 8

OQ, OKV, OZ, OXBC, OGA, OGS, OKPE, ODT = 0, 768, 1024, 3072, 6144, 7168, 8192, 8320
IN_PAD = 8448
QHEAD = 256

ROW_TILE = 640
MM_COL_TILE = 1408
MM_ROW_TILE = 1664
MM_VMEM_BUDGET = 46 * 1024 * 1024
MM_K_TILE = 2816
MM_TOKEN_K_TILE = 1664
ATT_TILE = 640
ATT_HEADS_PER_STEP = 8
BF16_ROWS = 16
HALO = BF16_ROWS
ROW_BUDGET = 12 * 1024 * 1024
ADAM_ELEMS = 160 * 1024


def _pick(n, target, q=LANES):
    assert n % q == 0, (n, q)
    units = n // q
    best = q
    for d in range(1, units + 1):
        if units % d == 0 and d * q <= target:
            best = d * q
    return best


def _pick_rows(n, row_bytes):
    return _pick(n, max(BF16_ROWS, ROW_BUDGET // row_bytes), BF16_ROWS)


def _params(sem, est_bytes):
    limit = int(min(VMEM_BYTES - (6 << 20), max(32 << 20, 2 * est_bytes + (8 << 20))))
    return pltpu.CompilerParams(dimension_semantics=sem, vmem_limit_bytes=limit)


def _nbytes(shape, dtype):
    return int(np.prod(shape)) * jnp.dtype(dtype).itemsize


def _mm(a, b, out_dtype, name, ta=False, tb=False, add=None):
    assert not (ta and tb)
    if ta:
        K, M = a.shape
        tm = _pick(M, MM_COL_TILE)
        tk = _pick(K, MM_TOKEN_K_TILE)
    else:
        M, K = a.shape
        tk = _pick(K, MM_K_TILE)
    N, K2 = (b.shape if tb else b.shape[::-1])
    assert K == K2
    tn = _pick(N, MM_COL_TILE)
    nk = K // tk

    def vmem_estimate(tm):
        e = 2 * (tm * tk * a.dtype.itemsize + tk * tn * b.dtype.itemsize + tm * tn * jnp.dtype(out_dtype).itemsize)
        e += tm * tn * 4 + tm * tk * 2
        return e + (tm * tn * 4 if nk > 1 else 0) + (2 * tm * tn * 4 if add is not None else 0)

    if not ta:
        tm = _pick(M, MM_ROW_TILE, BF16_ROWS)
        while vmem_estimate(tm) > MM_VMEM_BUDGET and tm > BF16_ROWS:
            tm = _pick(M, tm - BF16_ROWS, BF16_ROWS)
    dn = (((0,), (0,)), ((), ())) if ta else ((((1,), (1,)), ((), ())) if tb else (((1,), (0,)), ((), ())))

    def body(*refs):
        a_ref, b_ref = refs[:2]
        add_ref = refs[2] if add is not None else None
        o_ref = refs[2 + (add is not None)]
        d = lax.dot_general(a_ref[...].astype(BF16), b_ref[...].astype(BF16), dn, preferred_element_type=F32)

        def finish(r):
            if add is not None:
                r = r + add_ref[...].astype(F32)
            o_ref[...] = r.astype(out_dtype)

        if nk == 1:
            finish(d)
            return
        acc = refs[-1]
        k = pl.program_id(2)

        @pl.when(k == 0)
        def _():
            acc[...] = d

        @pl.when((k > 0) & (k < nk - 1))
        def _():
            acc[...] += d

        @pl.when(k == nk - 1)
        def _():
            finish(acc[...] + d)

    if ta:
        a_spec = pl.BlockSpec((tk, tm), lambda i, j, k: (k, i))
    else:
        a_spec = pl.BlockSpec((tm, tk), lambda i, j, k: (i, k))
    b_spec = pl.BlockSpec((tn, tk), lambda i, j, k: (j, k)) if tb else pl.BlockSpec((tk, tn), lambda i, j, k: (k, j))
    in_specs = [a_spec, b_spec]
    args = [a, b]
    est = vmem_estimate(tm)
    if add is not None:
        in_specs.append(pl.BlockSpec((tm, tn), lambda i, j, k: (i, j)))
        args.append(add)
    return pl.pallas_call(
        body, name=name, grid=(M // tm, N // tn, nk), in_specs=in_specs,
        out_specs=pl.BlockSpec((tm, tn), lambda i, j, k: (i, j)),
        out_shape=jax.ShapeDtypeStruct((M, N), out_dtype),
        scratch_shapes=[pltpu.VMEM((tm, tn), F32)] if nk > 1 else [],
        compiler_params=_params(("parallel", "parallel", "arbitrary"), est),
    )(*args)


def _row(arr, bw=None, cb=0, grp=False, diff=True):
    return dict(arr=arr, bw=arr.shape[1] if bw is None else bw, cb=cb, grp=grp, diff=diff)


def _out(width, dtype, bw=None, grp=False):
    return dict(width=width, dtype=dtype, bw=width if bw is None else bw, grp=grp)


def _spec_rows(d, tm):
    return pl.BlockSpec((tm, d["bw"]), lambda g, i, cb=d["cb"], gr=d["grp"]: (i, cb + (g if gr else 0)))


def _spec_const(d):
    return pl.BlockSpec((d["arr"].shape[0], d["bw"]), lambda g, i, cb=d["cb"], gr=d["grp"]: (0, cb + (g if gr else 0)))


def _rw_fwd(fn, rows, consts, outs, name, ng=1):
    Tp = rows[0]["arr"].shape[0]
    tm = _pick_rows(Tp, 4 * (sum(d["bw"] for d in rows) + 2 * sum(o["bw"] for o in outs)))
    nr, ncst = len(rows), len(consts)

    def body(*refs):
        i = pl.program_id(1)
        rowidx = i * tm + lax.broadcasted_iota(jnp.int32, (tm, 1), 0)
        rv = [r[...].astype(F32) for r in refs[:nr]]
        cv = [c[...] for c in refs[nr:nr + ncst]]
        vals = fn(rowidx, *rv, *cv)
        for o, v in zip(refs[nr + ncst:], vals):
            o[...] = v.astype(o.dtype)

    est = sum(tm * d["bw"] * 4 for d in rows) + sum(tm * o["bw"] * 4 for o in outs)
    return pl.pallas_call(
        body, name=name, grid=(ng, Tp // tm),
        in_specs=[_spec_rows(d, tm) for d in rows] + [_spec_const(d) for d in consts],
        out_specs=[pl.BlockSpec((tm, o["bw"]), lambda g, i, gr=o["grp"]: (i, g if gr else 0)) for o in outs],
        out_shape=[jax.ShapeDtypeStruct((Tp, o["width"]), o["dtype"]) for o in outs],
        compiler_params=_params(("parallel", "parallel"), 3 * est),
    )(*[d["arr"] for d in rows], *[d["arr"] for d in consts])


def _rw_bwd(fn, rows, consts, cots, drow_dtypes, name, ng=1):
    Tp = rows[0]["arr"].shape[0]
    tm = _pick_rows(Tp, 4 * (3 * sum(d["bw"] for d in rows) + 2 * sum(d["bw"] for d in cots)))
    nr, ncst, nct = len(rows), len(consts), len(cots)
    drows = [k for k, d in enumerate(rows) if d["diff"]]
    dcsts = [k for k, d in enumerate(consts) if d["diff"]]
    for k in drows:
        assert rows[k]["grp"] or ng == 1

    def body(*refs):
        g = pl.program_id(0)
        i = pl.program_id(1)
        rowidx = i * tm + lax.broadcasted_iota(jnp.int32, (tm, 1), 0)
        rv = [r[...].astype(F32) for r in refs[:nr]]
        cv = [c[...] for c in refs[nr:nr + ncst]]
        ct = tuple(r[...].astype(F32) for r in refs[nr + ncst:nr + ncst + nct])
        orefs = refs[nr + ncst + nct:]

        def f(*dargs):
            rr, cc = list(rv), list(cv)
            for k, v in zip(drows, dargs[:len(drows)]):
                rr[k] = v
            for k, v in zip(dcsts, dargs[len(drows):]):
                cc[k] = v
            return tuple(fn(rowidx, *rr, *cc))

        _, vjp = jax.vjp(f, *[rv[k] for k in drows], *[cv[k] for k in dcsts])
        grads = vjp(ct)
        for o, v in zip(orefs[:len(drows)], grads[:len(drows)]):
            o[...] = v.astype(o.dtype)
        for k, o, v in zip(dcsts, orefs[len(drows):], grads[len(drows):]):
            first = (i == 0) if consts[k]["grp"] else ((i == 0) & (g == 0))

            @pl.when(first)
            def _(o=o, v=v):
                o[...] = v

            @pl.when(jnp.logical_not(first))
            def _(o=o, v=v):
                o[...] += v

    out_specs, out_shape = [], []
    for k, dt in zip(drows, drow_dtypes):
        d = rows[k]
        out_specs.append(pl.BlockSpec((tm, d["bw"]), lambda g, i, gr=d["grp"]: (i, g if gr else 0)))
        out_shape.append(jax.ShapeDtypeStruct((Tp, d["bw"] * (ng if d["grp"] else 1)), dt))
    for k in dcsts:
        d = consts[k]
        r = d["arr"].shape[0]
        out_specs.append(pl.BlockSpec((r, d["bw"]), lambda g, i, gr=d["grp"]: (0, g if gr else 0)))
        out_shape.append(jax.ShapeDtypeStruct((r, d["bw"] * (ng if d["grp"] else 1)), F32))
    est = sum(tm * d["bw"] * 4 for d in rows) * 2 + sum(tm * d["bw"] * 4 for d in cots)
    res = pl.pallas_call(
        body, name=name, grid=(ng, Tp // tm),
        in_specs=[_spec_rows(d, tm) for d in rows] + [_spec_const(d) for d in consts] + [_spec_rows(d, tm) for d in cots],
        out_specs=out_specs, out_shape=out_shape,
        compiler_params=_params(("arbitrary", "arbitrary"), 3 * est),
    )(*[d["arr"] for d in rows], *[d["arr"] for d in consts], *[d["arr"] for d in cots])
    return list(res[:len(drows)]), list(res[len(drows):])


def _sigmoid(x):
    return 0.5 * jnp.tanh(0.5 * x) + 0.5


def _silu(x):
    return x * _sigmoid(x)


def _softplus(x):
    return jnp.maximum(x, 0.0) + jnp.log(1.0 + jnp.exp(-jnp.abs(x)))


def _layer_norm(x, g, b):
    mu = jnp.mean(x, axis=-1, keepdims=True)
    xc = x - mu
    var = jnp.mean(xc * xc, axis=-1, keepdims=True)
    return xc * lax.rsqrt(var + LN_EPS) * g + b


def _rms_norm(x, g):
    return x * lax.rsqrt(jnp.mean(x * x, axis=-1, keepdims=True) + RMS_EPS) * g


def _rope(r, cos, sin, rot):
    return r * cos + jnp.dot(r, rot, precision=HIGHEST, preferred_element_type=F32) * sin


def _make_stage_fns(npad):
    def fn_ln_masked(rowidx, x, g, b):
        return (jnp.where(rowidx >= npad, _layer_norm(x, g, b), 0.0),)

    def fn_in_post(rowidx, ql, kvl, kpe, dtr, cos, sin, rot, qg, kvg, dtb):
        qn = _rms_norm(ql, qg)
        kvn = _rms_norm(kvl, kvg)
        kr = _rope(kpe, cos, sin, rot)
        lane = lax.broadcasted_iota(jnp.int32, (1, LANES), 1)
        dt = jnp.where((rowidx >= npad) & (lane < SSD_HEADS), _softplus(dtr + dtb), 0.0)
        return qn, kvn, jnp.concatenate([kr] * HEADS, axis=1), dt

    def fn_q_post(rowidx, q, cos, sin, rot):
        rr = _rope(q[:, QK_NOPE:], cos, sin, rot)
        return (jnp.concatenate([q[:, :QK_NOPE], rr], axis=1) * ATT_SCALE,)

    def fn_gated_norm(rowidx, y, xs, z, dskip, g):
        v = (y + xs * dskip) * _silu(z)
        return (v * lax.rsqrt(jnp.mean(v * v, axis=-1, keepdims=True) + RMS_EPS) * g,)

    def fn_mix(rowidx, ga, gs, ya, ys):
        return (_sigmoid(ga) * ya + _sigmoid(gs) * ys,)

    def fn_res_ln(rowidx, h, r, g, b):
        return (jnp.where(rowidx >= npad, _layer_norm(ALPHA * h + r, g, b), 0.0),)

    def fn_glu(rowidx, u):
        return (_silu(u[:, :D_FF]) * u[:, D_FF:],)

    return dict(ln=fn_ln_masked, in_post=fn_in_post, q_post=fn_q_post, gated=fn_gated_norm, mix=fn_mix,
                res_ln=fn_res_ln, glu=fn_glu)


def _conv_tiles(Tp, C):
    return _pick(Tp, ROW_TILE), _pick(C, MM_COL_TILE)


def _conv_fwd(x, xoff, C, w8, b, K, act, npad, name, out_dtype=F32):
    Tp = x.shape[0]
    tm, tc = _conv_tiles(Tp, C)
    assert xoff % tc == 0
    cb0 = xoff // tc
    rb = tm // HALO

    def body(prev_ref, main_ref, w_ref, b_ref, o_ref):
        i = pl.program_id(1)
        main = main_ref[...].astype(F32)
        prev = jnp.where(i > 0, prev_ref[...].astype(F32), 0.0)
        ext = jnp.concatenate([prev, main], axis=0)
        acc = b_ref[...] + w_ref[K - 1:K, :] * main
        for k in range(K - 1):
            s = K - 1 - k
            acc = acc + w_ref[k:k + 1, :] * pltpu.roll(ext, s, 0)[HALO:, :]
        if act:
            rowidx = i * tm + lax.broadcasted_iota(jnp.int32, (tm, 1), 0)
            acc = jnp.where(rowidx >= npad, _silu(acc), 0.0)
        o_ref[...] = acc.astype(o_ref.dtype)

    return pl.pallas_call(
        body, name=name, grid=(C // tc, Tp // tm),
        in_specs=[pl.BlockSpec((HALO, tc), lambda g, i: (jnp.maximum(i * rb - 1, 0), cb0 + g)),
                  pl.BlockSpec((tm, tc), lambda g, i: (i, cb0 + g)),
                  pl.BlockSpec((SUBLANES, tc), lambda g, i: (0, g)),
                  pl.BlockSpec((1, tc), lambda g, i: (0, g))],
        out_specs=pl.BlockSpec((tm, tc), lambda g, i: (i, g)),
        out_shape=jax.ShapeDtypeStruct((Tp, C), out_dtype),
        compiler_params=_params(("parallel", "parallel"), 8 * tm * tc * 4),
    )(x, x, w8, b)


def _conv_bwd(x, xoff, C, w8, b, dy, K, act, npad, name):
    Tp = x.shape[0]
    tm, tc = _conv_tiles(Tp, C)
    cb0 = xoff // tc
    rb = tm // HALO
    ni = Tp // tm
    last_rb = Tp // HALO - 1
    n = tm + 2 * HALO

    def body(xp_ref, xm_ref, xn_ref, dym_ref, dyn_ref, w_ref, b_ref, dx_ref, dw_ref, db_ref):
        i = pl.program_id(1)
        prev = jnp.where(i > 0, xp_ref[...].astype(F32), 0.0)
        ext = jnp.concatenate([prev, xm_ref[...].astype(F32), xn_ref[...].astype(F32)], axis=0)
        dyn = jnp.where(i < ni - 1, dyn_ref[...].astype(F32), 0.0)
        dpre = jnp.concatenate([jnp.zeros((HALO, tc), F32), dym_ref[...].astype(F32), dyn], axis=0)
        shifted = [ext if k == K - 1 else pltpu.roll(ext, K - 1 - k, 0) for k in range(K)]
        if act:
            pre = b_ref[...] + sum(w_ref[k:k + 1, :] * shifted[k] for k in range(K))
            rowidx = i * tm - HALO + lax.broadcasted_iota(jnp.int32, (n, 1), 0)
            sg = _sigmoid(pre)
            dpre = jnp.where(rowidx >= npad, dpre * sg * (1.0 + pre * (1.0 - sg)), 0.0)
        dx = w_ref[K - 1:K, :] * dpre
        for k in range(K - 1):
            dx = dx + w_ref[k:k + 1, :] * pltpu.roll(dpre, n - (K - 1 - k), 0)
        dx_ref[...] = dx[HALO:HALO + tm, :].astype(dx_ref.dtype)

        @pl.when(i == 0)
        def _():
            dw_ref[...] = jnp.zeros_like(dw_ref)
            db_ref[...] = jnp.zeros_like(db_ref)

        dmain = dpre[HALO:HALO + tm, :]
        for k in range(K):
            dw_ref[k:k + 1, :] += jnp.sum(dmain * shifted[k][HALO:HALO + tm, :], axis=0, keepdims=True)
        db_ref[...] += jnp.sum(dmain, axis=0, keepdims=True)

    return pl.pallas_call(
        body, name=name, grid=(C // tc, ni),
        in_specs=[pl.BlockSpec((HALO, tc), lambda g, i: (jnp.maximum(i * rb - 1, 0), cb0 + g)),
                  pl.BlockSpec((tm, tc), lambda g, i: (i, cb0 + g)),
                  pl.BlockSpec((HALO, tc), lambda g, i: (jnp.minimum((i + 1) * rb, last_rb), cb0 + g)),
                  pl.BlockSpec((tm, tc), lambda g, i: (i, g)),
                  pl.BlockSpec((HALO, tc), lambda g, i: (jnp.minimum((i + 1) * rb, last_rb), g)),
                  pl.BlockSpec((SUBLANES, tc), lambda g, i: (0, g)),
                  pl.BlockSpec((1, tc), lambda g, i: (0, g))],
        out_specs=[pl.BlockSpec((tm, tc), lambda g, i: (i, g)),
                   pl.BlockSpec((SUBLANES, tc), lambda g, i: (0, g)),
                   pl.BlockSpec((1, tc), lambda g, i: (0, g))],
        out_shape=[jax.ShapeDtypeStruct((Tp, C), BF16), jax.ShapeDtypeStruct((SUBLANES, C), F32),
                   jax.ShapeDtypeStruct((1, C), F32)],
        compiler_params=_params(("parallel", "arbitrary"), 14 * tm * tc * 4),
    )(x, x, x, dy, dy, w8, b)


def _split_refs(refs, n_in, n_out, n_scratch, nbg):
    cuts = np.cumsum([0, n_in, nbg, n_out, nbg, n_scratch])
    return tuple(refs[a:b] for a, b in zip(cuts[:-1], cuts[1:])) + (refs[cuts[-1]:],)


def _flash_fwd(q, kv, kr8, npad, name, bg=None):
    Tp = q.shape[0]
    t = _pick(Tp, ATT_TILE)
    hp = ATT_HEADS_PER_STEP
    nb = Tp // t
    ng = HEADS // hp
    nbg = bg.n if bg else 0
    nt = (((1,), (1,)), ((), ()))
    tn = (((0,), (0,)), ((), ()))

    def body(*refs):
        (q_ref, kn_ref, kr_ref, v_ref), bg_in, (o_ref, lse_ref), bg_out, (m_sc, l_sc, acc_sc), bg_sems = _split_refs(
            refs, 4, 2, 3, nbg)
        g = pl.program_id(0)
        qi = pl.program_id(1)
        ki = pl.program_id(2)
        if bg:
            @pl.when((g == 0) & (qi == 0) & (ki == 0))
            def _():
                bg.start(bg_in, bg_out, bg_sems)

        @pl.when(ki == 0)
        def _():
            m_sc[...] = jnp.full_like(m_sc, NEG_INF)
            l_sc[...] = jnp.zeros_like(l_sc)
            acc_sc[...] = jnp.zeros_like(acc_sc)

        def step(masked):
            kr = kr_ref[...]
            if masked:
                key = ki * t + lax.broadcasted_iota(jnp.int32, (t, t), 0)
                qry = qi * t + lax.broadcasted_iota(jnp.int32, (t, t), 1)
                visible = (key <= qry) & (key >= npad)
            for hh in range(hp):
                k = jnp.concatenate([kn_ref[:, hh * QK_NOPE:(hh + 1) * QK_NOPE], kr], axis=1)
                st = lax.dot_general(k, q_ref[:, hh * QHEAD:(hh + 1) * QHEAD], nt, preferred_element_type=F32)
                if masked:
                    st = jnp.where(visible, st, NEG_INF)
                vs = slice(hh * V_HEAD, (hh + 1) * V_HEAD)
                m_prev = m_sc[hh]
                m_new = jnp.maximum(m_prev, jnp.max(st, axis=0, keepdims=True))
                pt = jnp.exp(st - m_new)
                a = jnp.exp(m_prev - m_new)
                l_sc[hh] = a * l_sc[hh] + jnp.sum(pt, axis=0, keepdims=True)
                acc_sc[vs, :] = a * acc_sc[vs, :] + lax.dot_general(v_ref[:, vs], pt.astype(BF16), tn,
                                                                    preferred_element_type=F32)
                m_sc[hh] = m_new

        need_mask = (ki == qi) | (ki == 0)

        @pl.when((ki <= qi) & need_mask)
        def _():
            step(True)

        @pl.when((ki <= qi) & jnp.logical_not(need_mask))
        def _():
            step(False)

        @pl.when(ki == qi)
        def _():
            for hh in range(hp):
                vs = slice(hh * V_HEAD, (hh + 1) * V_HEAD)
                l = l_sc[hh]
                o_ref[:, vs] = (acc_sc[vs, :] / l).T.astype(o_ref.dtype)
                lse_ref[hh * SUBLANES:(hh + 1) * SUBLANES, :] = jnp.broadcast_to(m_sc[hh] + jnp.log(l), (SUBLANES, t))

        if bg:
            @pl.when((g == ng - 1) & (qi == nb - 1) & (ki == nb - 1))
            def _():
                bg.wait(bg_in, bg_out, bg_sems)

    kmin = lambda qi, ki: jnp.minimum(ki, qi)
    return pl.pallas_call(
        body, name=name, grid=(ng, nb, nb),
        in_specs=[pl.BlockSpec((t, hp * QHEAD), lambda g, qi, ki: (qi, g)),
                  pl.BlockSpec((t, hp * QK_NOPE), lambda g, qi, ki: (kmin(qi, ki), g)),
                  pl.BlockSpec((t, LANES), lambda g, qi, ki: (kmin(qi, ki), 0)),
                  pl.BlockSpec((t, hp * V_HEAD), lambda g, qi, ki: (kmin(qi, ki), ng + g))] + (bg.specs if bg else []),
        out_specs=[pl.BlockSpec((t, hp * V_HEAD), lambda g, qi, ki: (qi, g)),
                   pl.BlockSpec((hp * SUBLANES, t), lambda g, qi, ki: (g, qi))] + (bg.specs if bg else []),
        out_shape=[jax.ShapeDtypeStruct((Tp, HEADS * V_HEAD), F32), jax.ShapeDtypeStruct((HEADS * SUBLANES, Tp), F32)]
        + (bg.out_shape if bg else []),
        scratch_shapes=[pltpu.VMEM((hp, 1, t), F32), pltpu.VMEM((hp, 1, t), F32), pltpu.VMEM((hp * V_HEAD, t), F32)]
        + (bg.scratch if bg else []),
        compiler_params=_params(("arbitrary",) * 3 if bg else ("parallel", "parallel", "arbitrary"), 8 * hp * t * t * 4),
    )(q, kv, kr8, kv, *(bg.arrs if bg else []))


def _attn_delta(do, o, name):
    Tp = do.shape[0]
    tm = _pick(Tp, MM_TOKEN_K_TILE)

    def body(do_ref, o_ref, d_ref):
        prod = do_ref[...] * o_ref[...]
        ones = jnp.ones((SUBLANES, V_HEAD), F32)
        d_ref[...] = lax.dot_general(ones, prod, (((1,), (1,)), ((), ())), precision=HIGHEST,
                                     preferred_element_type=F32)

    return pl.pallas_call(
        body, name=name, grid=(HEADS, Tp // tm),
        in_specs=[pl.BlockSpec((tm, V_HEAD), lambda h, i: (i, h)), pl.BlockSpec((tm, V_HEAD), lambda h, i: (i, h))],
        out_specs=pl.BlockSpec((SUBLANES, tm), lambda h, i: (h, i)),
        out_shape=jax.ShapeDtypeStruct((HEADS * SUBLANES, Tp), F32),
        compiler_params=_params(("parallel", "parallel"), 4 * tm * V_HEAD * 4),
    )(do, o)


def _flash_bwd(q, kv, kr8, do, lse, delta, npad, name, bg=None):
    Tp = q.shape[0]
    t = _pick(Tp, ATT_TILE)
    nb = Tp // t
    nbg = bg.n if bg else 0
    nt = (((1,), (1,)), ((), ()))
    tn = (((0,), (0,)), ((), ()))

    def body(*refs):
        ((q_ref, kn_ref, kr_ref, v_ref, do_ref, lse_ref, dl_ref), bg_in, (dq_ref, dkn_ref, dkr_ref, dv_ref), bg_out,
         (dk_sc, dv_sc), bg_sems) = _split_refs(refs, 7, 4, 2, nbg)
        h = pl.program_id(0)
        ki = pl.program_id(1)
        qi = pl.program_id(2)
        if bg:
            @pl.when((h == 0) & (ki == 0) & (qi == 0))
            def _():
                bg.start(bg_in, bg_out, bg_sems)

        @pl.when(qi == 0)
        def _():
            dk_sc[...] = jnp.zeros_like(dk_sc)
            dv_sc[...] = jnp.zeros_like(dv_sc)

        def step(masked):
            qv = q_ref[...]
            k = jnp.concatenate([kn_ref[...], kr_ref[...]], axis=1)
            st = lax.dot_general(k, qv, nt, preferred_element_type=F32)
            if masked:
                key = ki * t + lax.broadcasted_iota(jnp.int32, (t, t), 0)
                qry = qi * t + lax.broadcasted_iota(jnp.int32, (t, t), 1)
                st = jnp.where((key <= qry) & (key >= npad), st, NEG_INF)
            pt = jnp.exp(st - lse_ref[0:1, :])
            dob = do_ref[...].astype(BF16)
            dv_sc[...] += jnp.dot(pt.astype(BF16), dob, preferred_element_type=F32)
            dpt = lax.dot_general(v_ref[...], dob, nt, preferred_element_type=F32)
            dst = (pt * (dpt - dl_ref[0:1, :])).astype(BF16)
            dk_sc[...] += jnp.dot(dst, qv, preferred_element_type=F32)
            dqc = lax.dot_general(dst, k, tn, preferred_element_type=F32)
            rows = pl.ds(pl.multiple_of(qi * t, t), t)

            @pl.when(ki == 0)
            def _():
                dq_ref[rows, :] = dqc

            @pl.when(ki > 0)
            def _():
                dq_ref[rows, :] += dqc

        need_mask = (ki == qi) | (ki == 0)

        @pl.when((qi >= ki) & need_mask)
        def _():
            step(True)

        @pl.when((qi >= ki) & jnp.logical_not(need_mask))
        def _():
            step(False)

        @pl.when(qi == nb - 1)
        def _():
            dkn_ref[...] = dk_sc[:, :QK_NOPE].astype(dkn_ref.dtype)
            dkr_ref[...] = dk_sc[:, QK_NOPE:].astype(dkr_ref.dtype)
            dv_ref[...] = dv_sc[...].astype(dv_ref.dtype)

        if bg:
            @pl.when((h == HEADS - 1) & (ki == nb - 1) & (qi == nb - 1))
            def _():
                bg.wait(bg_in, bg_out, bg_sems)

    qmap = lambda h, ki, qi: (jnp.maximum(qi, ki), h)
    kmap = lambda h, ki, qi: (ki, h)
    est = 2 * Tp * QHEAD * 4 + 8 * t * t * 4
    return pl.pallas_call(
        body, name=name, grid=(HEADS, nb, nb),
        in_specs=[pl.BlockSpec((t, QHEAD), qmap),
                  pl.BlockSpec((t, QK_NOPE), kmap),
                  pl.BlockSpec((t, LANES), kmap),
                  pl.BlockSpec((t, V_HEAD), lambda h, ki, qi: (ki, HEADS + h)),
                  pl.BlockSpec((t, V_HEAD), qmap),
                  pl.BlockSpec((SUBLANES, t), lambda h, ki, qi: (h, jnp.maximum(qi, ki))),
                  pl.BlockSpec((SUBLANES, t), lambda h, ki, qi: (h, jnp.maximum(qi, ki)))] + (bg.specs if bg else []),
        out_specs=[pl.BlockSpec((Tp, QHEAD), lambda h, ki, qi: (0, h)),
                   pl.BlockSpec((t, QK_NOPE), kmap),
                   pl.BlockSpec((t, LANES), kmap),
                   pl.BlockSpec((t, V_HEAD), kmap)] + (bg.specs if bg else []),
        out_shape=[jax.ShapeDtypeStruct((Tp, HEADS * QHEAD), F32),
                   jax.ShapeDtypeStruct((Tp, HEADS * QK_NOPE), BF16),
                   jax.ShapeDtypeStruct((Tp, HEADS * LANES), F32),
                   jax.ShapeDtypeStruct((Tp, HEADS * V_HEAD), BF16)] + (bg.out_shape if bg else []),
        scratch_shapes=[pltpu.VMEM((t, QHEAD), F32), pltpu.VMEM((t, V_HEAD), F32)] + (bg.scratch if bg else []),
        compiler_params=_params(("arbitrary",) * 3 if bg else ("parallel", "arbitrary", "arbitrary"), est),
    )(q, kv, kr8, kv, do, lse, delta, *(bg.arrs if bg else []))


GW = SSD_INNER // SSD_GROUPS
PAIRS_PER_GROUP = GW // LANES
XB = SSD_INNER // GW
NT_DIMS = (((1,), (1,)), ((), ()))
TN_DIMS = (((0,), (0,)), ((), ()))


def _ssd_common(xs_ref, dt_ref, alog_ref, e_ref):
    a_neg = -jnp.exp(alog_ref[...])
    dt = dt_ref[...]
    li = lax.broadcasted_iota(jnp.int32, (CHUNK, CHUNK), 0)
    si = lax.broadcasted_iota(jnp.int32, (CHUNK, CHUNK), 1)
    tril = li >= si
    tri = tril.astype(F32)
    acs = jnp.dot(tri, dt * a_neg, precision=SSD_PREC, preferred_element_type=F32)
    e = e_ref[...]
    dte = jnp.dot(dt, e, precision=SSD_PREC, preferred_element_type=F32)
    acse = jnp.dot(acs, e, precision=SSD_PREC, preferred_element_type=F32)
    x = xs_ref[...] * dte
    alast = acse[CHUNK - 1:CHUNK, :]
    return dict(a_neg=a_neg, dt=dt, tril=tril, tri=tri, acs=acs, acs_t=acs.T, e=e, dte=dte, acse=acse, x=x,
                p_e=jnp.exp(acse), w_e=jnp.exp(alast - acse), dl_e=jnp.exp(alast), li=li, si=si)


def _decay(cm, head):
    col = cm["acs"][:, head:head + 1]
    row = cm["acs_t"][head:head + 1, :]
    return jnp.exp(jnp.where(cm["tril"], col - row, -jnp.inf))


def _ssd_fwd(xbc, dt, alog, e, name):
    Tp = xbc.shape[0]
    nc = Tp // CHUNK

    def body(xs_ref, b_ref, c_ref, dt_ref, alog_ref, e_ref, y_ref, st_ref, st_sc):
        @pl.when(pl.program_id(0) == 0)
        def _():
            st_sc[...] = jnp.zeros_like(st_sc)

        cm = _ssd_common(xs_ref, dt_ref, alog_ref, e_ref)
        st_ref[0] = st_sc[...]
        lane = lax.broadcasted_iota(jnp.int32, (CHUNK, LANES), 1)
        for g in range(SSD_GROUPS):
            gs = slice(g * GW, (g + 1) * GW)
            cg = c_ref[:, g * SSD_STATE:(g + 1) * SSD_STATE].astype(BF16)
            bg = b_ref[:, g * SSD_STATE:(g + 1) * SSD_STATE].astype(BF16)
            cb = lax.dot_general(cg, bg, NT_DIMS, preferred_element_type=F32)
            stg = st_sc[:, gs]
            yoff = jnp.dot(cg, stg.astype(BF16), preferred_element_type=F32) * cm["p_e"][:, gs]
            xg = cm["x"][:, gs]
            for jp in range(PAIRS_PER_GROUP):
                j = g * PAIRS_PER_GROUP + jp
                xp = xg[:, jp * LANES:(jp + 1) * LANES].astype(BF16)
                ys = []
                for head in (2 * j, 2 * j + 1):
                    m = (cb * _decay(cm, head)).astype(BF16)
                    ys.append(jnp.dot(m, xp, preferred_element_type=F32))
                y_ref[:, j * LANES:(j + 1) * LANES] = (jnp.where(lane < SSD_HEAD_DIM, ys[0], ys[1])
                                                       + yoff[:, jp * LANES:(jp + 1) * LANES])
            snew = lax.dot_general(bg, (cm["w_e"][:, gs] * xg).astype(BF16), TN_DIMS, preferred_element_type=F32)
            st_sc[:, gs] = cm["dl_e"][:, gs] * stg + snew

    return pl.pallas_call(
        body, name=name, grid=(nc,),
        in_specs=[pl.BlockSpec((CHUNK, SSD_INNER), lambda c: (c, 0)),
                  pl.BlockSpec((CHUNK, GW), lambda c: (c, XB)),
                  pl.BlockSpec((CHUNK, GW), lambda c: (c, XB + 1)),
                  pl.BlockSpec((CHUNK, LANES), lambda c: (c, 0)),
                  pl.BlockSpec((1, LANES), lambda c: (0, 0)),
                  pl.BlockSpec((LANES, SSD_INNER), lambda c: (0, 0))],
        out_specs=[pl.BlockSpec((CHUNK, SSD_INNER), lambda c: (c, 0)),
                   pl.BlockSpec((1, SSD_STATE, SSD_INNER), lambda c: (c, 0, 0))],
        out_shape=[jax.ShapeDtypeStruct((Tp, SSD_INNER), F32), jax.ShapeDtypeStruct((nc, SSD_STATE, SSD_INNER), F32)],
        scratch_shapes=[pltpu.VMEM((SSD_STATE, SSD_INNER), F32)],
        compiler_params=_params(("arbitrary",), 24 * CHUNK * SSD_INNER * 4),
    )(xbc, xbc, xbc, dt, alog, e)


def _ssd_bwd(xbc, dt, alog, e, dy, dxs_skip, states, name):
    Tp = xbc.shape[0]
    nc = Tp // CHUNK
    rev = lambda c: nc - 1 - c

    def body(xs_ref, b_ref, c_ref, dt_ref, alog_ref, e_ref, dy_ref, skip_ref, st_ref,
             dxbc_ref, ddt_ref, dalog_ref, dst_sc, dx_sc, t_sc, tw_sc):
        @pl.when(pl.program_id(0) == 0)
        def _():
            dst_sc[...] = jnp.zeros_like(dst_sc)
            dalog_ref[...] = jnp.zeros_like(dalog_ref)

        cm = _ssd_common(xs_ref, dt_ref, alog_ref, e_ref)
        lane = lax.broadcasted_iota(jnp.int32, (CHUNK, LANES), 1)
        dacs_col = jnp.zeros((CHUNK, LANES), F32)
        dacs_row = jnp.zeros((LANES, CHUNK), F32)
        t_last = []
        for g in range(SSD_GROUPS):
            gs = slice(g * GW, (g + 1) * GW)
            cg = c_ref[:, g * SSD_STATE:(g + 1) * SSD_STATE].astype(BF16)
            bg = b_ref[:, g * SSD_STATE:(g + 1) * SSD_STATE].astype(BF16)
            stg = st_ref[0, :, gs]
            stg_b = stg.astype(BF16)
            dstg = dst_sc[:, gs]
            dstg_b = dstg.astype(BF16)
            xg = cm["x"][:, gs]
            dyg = dy_ref[:, gs]
            zg = jnp.dot(cg, stg_b, preferred_element_type=F32)
            dzg = dyg * cm["p_e"][:, gs]
            dzg_b = dzg.astype(BF16)
            dcg = lax.dot_general(dzg_b, stg_b, NT_DIMS, preferred_element_type=F32)
            dst_in = lax.dot_general(cg, dzg_b, TN_DIMS, preferred_element_type=F32)
            dst_in = dst_in + cm["dl_e"][:, gs] * dstg
            t_last.append(jnp.sum(dstg * stg * cm["dl_e"][:, gs], axis=0, keepdims=True))
            weg = cm["w_e"][:, gs]
            dbg = lax.dot_general((weg * xg).astype(BF16), dstg_b, NT_DIMS, preferred_element_type=F32)
            gg = jnp.dot(bg, dstg_b, preferred_element_type=F32)
            dxg = weg * gg
            tw_sc[:, gs] = xg * dxg
            t_sc[:, gs] = dzg * zg - xg * dxg
            cb = lax.dot_general(cg, bg, NT_DIMS, preferred_element_type=F32)
            dcb = jnp.zeros((CHUNK, CHUNK), F32)
            for jp in range(PAIRS_PER_GROUP):
                j = g * PAIRS_PER_GROUP + jp
                ps = slice(jp * LANES, (jp + 1) * LANES)
                xp = xg[:, ps].astype(BF16)
                dyp = dyg[:, ps]
                dxp = dxg[:, ps]
                for half, head in enumerate((2 * j, 2 * j + 1)):
                    lam = _decay(cm, head)
                    m32 = cb * lam
                    sel = (lane < SSD_HEAD_DIM) if half == 0 else (lane >= SSD_HEAD_DIM)
                    dye = jnp.where(sel, dyp, 0.0).astype(BF16)
                    dm = lax.dot_general(dye, xp, NT_DIMS, preferred_element_type=F32)
                    w = dm * m32
                    dacs_col = dacs_col + jnp.where(cm["si"] == head, jnp.sum(w, axis=1, keepdims=True), 0.0)
                    dacs_row = dacs_row + jnp.where(cm["li"] == head, jnp.sum(w, axis=0, keepdims=True), 0.0)
                    dcb = dcb + dm * lam
                    dxp = dxp + lax.dot_general(m32.astype(BF16), dye, TN_DIMS, preferred_element_type=F32)
                dx_sc[:, j * LANES:(j + 1) * LANES] = dxp
            dcb_b = dcb.astype(BF16)
            dcg = dcg + jnp.dot(dcb_b, bg, preferred_element_type=F32)
            dbg = dbg + lax.dot_general(dcb_b, cg, TN_DIMS, preferred_element_type=F32)
            dst_sc[:, gs] = dst_in
            dxbc_ref[:, SSD_INNER + g * SSD_STATE:SSD_INNER + (g + 1) * SSD_STATE] = dbg
            dxbc_ref[:, SSD_INNER + GW + g * SSD_STATE:SSD_INNER + GW + (g + 1) * SSD_STATE] = dcg
        e = cm["e"]
        dacs = lax.dot_general(t_sc[...], e, NT_DIMS, precision=SSD_PREC, preferred_element_type=F32)
        dacs = dacs + dacs_col - dacs_row.T
        last_lane = jnp.concatenate(t_last, axis=1) + jnp.sum(tw_sc[...], axis=0, keepdims=True)
        last_head = lax.dot_general(jnp.broadcast_to(last_lane, (SUBLANES, SSD_INNER)), e, NT_DIMS,
                                    precision=SSD_PREC, preferred_element_type=F32)[0:1, :]
        dacs = dacs + jnp.where(cm["li"] == CHUNK - 1, last_head, 0.0)
        da = lax.dot_general(cm["tri"], dacs, TN_DIMS, precision=SSD_PREC, preferred_element_type=F32)
        dx_all = dx_sc[...]
        ddt = da * cm["a_neg"] + lax.dot_general(dx_all * xs_ref[...], e, NT_DIMS, precision=SSD_PREC,
                                                 preferred_element_type=F32)
        ddt_ref[...] = ddt
        dxbc_ref[:, :SSD_INNER] = dx_all * cm["dte"] + skip_ref[...]
        dalog_ref[0:1, :] += jnp.sum(da * cm["dt"], axis=0, keepdims=True) * cm["a_neg"]

    return pl.pallas_call(
        body, name=name, grid=(nc,),
        in_specs=[pl.BlockSpec((CHUNK, SSD_INNER), lambda c: (rev(c), 0)),
                  pl.BlockSpec((CHUNK, GW), lambda c: (rev(c), XB)),
                  pl.BlockSpec((CHUNK, GW), lambda c: (rev(c), XB + 1)),
                  pl.BlockSpec((CHUNK, LANES), lambda c: (rev(c), 0)),
                  pl.BlockSpec((1, LANES), lambda c: (0, 0)),
                  pl.BlockSpec((LANES, SSD_INNER), lambda c: (0, 0)),
                  pl.BlockSpec((CHUNK, SSD_INNER), lambda c: (rev(c), 0)),
                  pl.BlockSpec((CHUNK, SSD_INNER), lambda c: (rev(c), 0)),
                  pl.BlockSpec((1, SSD_STATE, SSD_INNER), lambda c: (rev(c), 0, 0))],
        out_specs=[pl.BlockSpec((CHUNK, SSD_CONV_DIM), lambda c: (rev(c), 0)),
                   pl.BlockSpec((CHUNK, LANES), lambda c: (rev(c), 0)),
                   pl.BlockSpec((SUBLANES, LANES), lambda c: (0, 0))],
        out_shape=[jax.ShapeDtypeStruct((Tp, SSD_CONV_DIM), F32), jax.ShapeDtypeStruct((Tp, LANES), F32),
                   jax.ShapeDtypeStruct((SUBLANES, LANES), F32)],
        scratch_shapes=[pltpu.VMEM((SSD_STATE, SSD_INNER), F32), pltpu.VMEM((CHUNK, SSD_INNER), F32),
                        pltpu.VMEM((CHUNK, SSD_INNER), F32), pltpu.VMEM((CHUNK, SSD_INNER), F32)],
        compiler_params=_params(("arbitrary",), 32 * CHUNK * SSD_INNER * 4),
    )(xbc, xbc, xbc, dt, alog, e, dy, dxs_skip, states)


def _loss_head(h, target, name):
    Tp, d = h.shape
    nt = Tp // LANES

    def body(h_ref, t_ref, dh_ref, l_ref):
        real = pl.program_id(0) > 0
        err = jnp.where(real, h_ref[...] - t_ref[...], 0.0)
        dh_ref[...] = err * (1.0 / d)
        l_ref[...] = jnp.broadcast_to(0.5 * jnp.sum(err * err) * (1.0 / d), l_ref.shape)

    return pl.pallas_call(
        body, name=name, grid=(nt,),
        in_specs=[pl.BlockSpec((LANES, d), lambda i: (i, 0)),
                  pl.BlockSpec((LANES, d), lambda i: (jnp.maximum(i - 1, 0), 0))],
        out_specs=[pl.BlockSpec((LANES, d), lambda i: (i, 0)),
                   pl.BlockSpec((1, SUBLANES, LANES), lambda i: (i, 0, 0))],
        out_shape=[jax.ShapeDtypeStruct((Tp, d), F32), jax.ShapeDtypeStruct((nt, SUBLANES, LANES), F32)],
        compiler_params=_params(("parallel",), 8 * LANES * d * 4),
    )(h, target)


def _adamw(parts, w, m, v, name):
    shape = w.shape
    C = shape[-1]
    R = int(np.prod(shape[:-1]))
    npart = parts.shape[0]
    parts, w, m, v = parts.reshape(npart, R, C), w.reshape(R, C), m.reshape(R, C), v.reshape(R, C)
    lanes = -(-C // LANES) * LANES
    tr = _pick(R, max(BF16_ROWS, ADAM_ELEMS // lanes), BF16_ROWS) if R % BF16_ROWS == 0 else R
    c1 = 1.0 / (1.0 - ADAM_B1 ** ADAM_STEP)
    c2 = 1.0 / (1.0 - ADAM_B2 ** ADAM_STEP)

    def body(p_ref, w_ref, m_ref, v_ref, g_out, d_out, m_out, v_out):
        g = p_ref[0].astype(F32)
        for p in range(1, npart):
            g = g + p_ref[p].astype(F32)
        m_new = ADAM_B1 * m_ref[...] + (1.0 - ADAM_B1) * g
        v_new = ADAM_B2 * v_ref[...] + (1.0 - ADAM_B2) * (g * g)
        g_out[...] = g
        m_out[...] = m_new
        v_out[...] = v_new
        d_out[...] = -ADAM_LR * ((m_new * c1) / (jnp.sqrt(v_new * c2) + ADAM_EPS) + ADAM_WD * w_ref[...])

    spec = pl.BlockSpec((tr, C), lambda i: (i, 0))
    est = npart * tr * lanes * parts.dtype.itemsize + 7 * tr * lanes * 4
    res = pl.pallas_call(
        body, name=name, grid=(R // tr,),
        in_specs=[pl.BlockSpec((npart, tr, C), lambda i: (0, i, 0)), spec, spec, spec],
        out_specs=[spec] * 4, out_shape=[jax.ShapeDtypeStruct((R, C), F32)] * 4,
        compiler_params=_params(("parallel",), est),
    )(parts, w, m, v)
    return [r.reshape(shape) for r in res]


MESH_ID = pl.DeviceIdType.MESH
N_PEERS = N_DEV - 1


def _dev_index(p):
    return 4 * p[0] + 2 * p[1] + p[2]


class _Background:
    def __init__(self, kind, arrs):
        self.kind, self.arrs, self.n = kind, list(arrs), len(arrs)
        self.npairs = N_PEERS if kind == "gather" else N_CHIPS - 1
        lead = (N_DEV,) if kind == "gather" else ()
        self.out_shape = [jax.ShapeDtypeStruct(lead + a.shape, a.dtype) for a in self.arrs]
        self.specs = [pl.BlockSpec(memory_space=pl.ANY)] * self.n
        self.scratch = [pltpu.SemaphoreType.DMA((self.n, self.npairs)), pltpu.SemaphoreType.DMA((self.n, self.npairs)),
                        pltpu.SemaphoreType.DMA((self.n,))]

    def copies(self, in_refs, out_refs, sems):
        send_sems, recv_sems, local_sems = sems
        x, y, c = lax.axis_index("x"), lax.axis_index("y"), lax.axis_index("c")
        sends, recvs, locals_ = [], [], []

        def remote(t, k, src, dst, to):
            return pltpu.make_async_remote_copy(src_ref=src, dst_ref=dst, send_sem=send_sems.at[t, k],
                                                recv_sem=recv_sems.at[t, k], device_id=to, device_id_type=MESH_ID)

        if self.kind == "gather":
            me = _dev_index((x, y, c))
            peers = [(x, y, 1 - c), (1 - x, y, c), (x, 1 - y, c), (1 - x, 1 - y, c),
                     (1 - x, y, 1 - c), (x, 1 - y, 1 - c), (1 - x, 1 - y, 1 - c)]
            for t in range(self.n):
                locals_.append(pltpu.make_async_copy(in_refs[t], out_refs[t].at[me], local_sems.at[t]))
                for k, p in enumerate(peers):
                    sends.append(remote(t, k, in_refs[t], out_refs[t].at[me], p))
                    recvs.append(remote(t, k, in_refs[t], out_refs[t].at[_dev_index(p)], p))
        else:
            mine = 2 * x + y
            peers = [(1 - x, y), (x, 1 - y), (1 - x, 1 - y)]
            for t in range(self.n):
                locals_.append(pltpu.make_async_copy(in_refs[t].at[mine], out_refs[t].at[mine], local_sems.at[t]))
                for k, p in enumerate(peers):
                    theirs = 2 * p[0] + p[1]
                    sends.append(remote(t, k, in_refs[t].at[theirs], out_refs[t].at[mine], (*p, c)))
                    recvs.append(remote(t, k, in_refs[t].at[mine], out_refs[t].at[theirs], (*p, c)))
        return sends, recvs, locals_

    def start(self, in_refs, out_refs, sems):
        sends, _, locals_ = self.copies(in_refs, out_refs, sems)
        for cp in locals_ + sends:
            cp.start()

    def wait(self, in_refs, out_refs, sems):
        sends, recvs, locals_ = self.copies(in_refs, out_refs, sems)
        for cp in recvs:
            cp.wait_recv()
        for cp in sends:
            cp.wait_send()
        for cp in locals_:
            cp.wait()


def _comm_call(body, name, arrs, out_shape, npairs):
    n = len(arrs)
    any_spec = pl.BlockSpec(memory_space=pl.ANY)
    return pl.pallas_call(
        functools.partial(body, n), name=name, in_specs=[any_spec] * n, out_specs=[any_spec] * n, out_shape=out_shape,
        scratch_shapes=[pltpu.SemaphoreType.DMA((n, npairs)), pltpu.SemaphoreType.DMA((n, npairs)),
                        pltpu.SemaphoreType.DMA((n,))],
    )(*arrs)


def _allgather(arrs, name):
    def body(n, *refs):
        src_refs, out_refs = refs[:n], refs[n:2 * n]
        send_sems, recv_sems, local_sems = refs[2 * n:]
        x, y, c = lax.axis_index("x"), lax.axis_index("y"), lax.axis_index("c")
        me, sibling = (x, y, c), (x, y, 1 - c)
        chips = [(1 - x, y), (x, 1 - y), (1 - x, 1 - y)]

        def copy(t, k, block, to, src=None):
            slot = out_refs[t].at[_dev_index(block)]
            return pltpu.make_async_remote_copy(
                src_ref=slot if src is None else src, dst_ref=slot,
                send_sem=send_sems.at[t, k], recv_sem=recv_sems.at[t, k],
                device_id=to, device_id_type=MESH_ID)

        sends, locals_ = [], []
        for t in range(n):
            mine = pltpu.make_async_copy(src_refs[t], out_refs[t].at[_dev_index(me)], local_sems.at[t])
            mine.start()
            locals_.append(mine)
            first = [copy(t, 0, me, sibling, src=src_refs[t])]
            first += [copy(t, 1 + j, me, (*chip, c), src=src_refs[t]) for j, chip in enumerate(chips)]
            for cp in first:
                cp.start()
            sends += first
        for j, chip in enumerate(chips):
            for t in range(n):
                copy(t, 1 + j, (*chip, c), me).wait_recv()
                passed = copy(t, 4 + j, (*chip, c), sibling)
                passed.start()
                sends.append(passed)
        for t in range(n):
            copy(t, 0, sibling, me).wait_recv()
            for j, chip in enumerate(chips):
                copy(t, 4 + j, (*chip, 1 - c), me).wait_recv()
        for cp in sends:
            cp.wait_send()
        for cp in locals_:
            cp.wait()

    return _comm_call(body, name, arrs, [jax.ShapeDtypeStruct((N_DEV,) + a.shape, a.dtype) for a in arrs], N_PEERS)


N_CHIPS = N_DEV // 2
CHIPS = [(0, 0), (0, 1), (1, 0), (1, 1)]


def _sibling_exchange(arrs, name):
    def body(n, *refs):
        in_refs, out_refs = refs[:n], refs[n:2 * n]
        send_sems, recv_sems, _ = refs[2 * n:]
        x, y, c = lax.axis_index("x"), lax.axis_index("y"), lax.axis_index("c")
        sibling = (x, y, 1 - c)

        def copy(t, j):
            return pltpu.make_async_remote_copy(
                src_ref=in_refs[t].at[_dev_index((*CHIPS[j], 1 - c))], dst_ref=out_refs[t].at[j],
                send_sem=send_sems.at[t, j], recv_sem=recv_sems.at[t, j],
                device_id=sibling, device_id_type=MESH_ID)

        copies = [copy(t, j) for t in range(n) for j in range(N_CHIPS)]
        for cp in copies:
            cp.start()
        for cp in copies:
            cp.wait_recv()
        for cp in copies:
            cp.wait_send()

    return _comm_call(body, name, arrs, [jax.ShapeDtypeStruct((N_CHIPS,) + a.shape[1:], a.dtype) for a in arrs], N_CHIPS)


def _chip_exchange(arrs, name):
    def body(n, *refs):
        in_refs, out_refs = refs[:n], refs[n:2 * n]
        send_sems, recv_sems, local_sems = refs[2 * n:]
        x, y, c = lax.axis_index("x"), lax.axis_index("y"), lax.axis_index("c")
        mine = 2 * x + y
        peers = [(1 - x, y), (x, 1 - y), (1 - x, 1 - y)]

        def copy(t, k, src_chip, dst_chip, to):
            return pltpu.make_async_remote_copy(
                src_ref=in_refs[t].at[src_chip], dst_ref=out_refs[t].at[dst_chip],
                send_sem=send_sems.at[t, k], recv_sem=recv_sems.at[t, k],
                device_id=(*to, c), device_id_type=MESH_ID)

        sends, locals_ = [], []
        for t in range(n):
            own = pltpu.make_async_copy(in_refs[t].at[mine], out_refs[t].at[mine], local_sems.at[t])
            own.start()
            locals_.append(own)
            for k, p in enumerate(peers):
                cp = copy(t, k, 2 * p[0] + p[1], mine, p)
                cp.start()
                sends.append(cp)
        for t in range(n):
            for k, p in enumerate(peers):
                copy(t, k, mine, 2 * p[0] + p[1], p).wait_recv()
        for cp in sends:
            cp.wait_send()
        for cp in locals_:
            cp.wait()

    return _comm_call(body, name, arrs, [jax.ShapeDtypeStruct(a.shape, a.dtype) for a in arrs], N_CHIPS - 1)


def _add_pairs(a, b, name):
    shape = a.shape
    C = shape[-1]
    R = int(np.prod(shape[:-1]))
    lanes = -(-C // LANES) * LANES
    tr = _pick(R, max(BF16_ROWS, 2 * ADAM_ELEMS // lanes), BF16_ROWS) if R % BF16_ROWS == 0 else R

    def body(a_ref, b_ref, o_ref):
        o_ref[...] = (a_ref[...].astype(F32) + b_ref[...].astype(F32)).astype(o_ref.dtype)

    spec = pl.BlockSpec((tr, C), lambda i: (i, 0))
    return pl.pallas_call(
        body, name=name, grid=(R // tr,), in_specs=[spec, spec], out_specs=spec,
        out_shape=jax.ShapeDtypeStruct((R, C), a.dtype),
        compiler_params=_params(("parallel",), 3 * tr * lanes * 4),
    )(a.reshape(R, C), b.reshape(R, C)).reshape(shape)


WEIGHTS = ['meta_tokens', 'emb_ln_g', 'emb_ln_b', 'w_in', 'q_norm_g', 'w_q_b', 'kv_norm_g', 'w_kv_b', 'w_o_attn',
           'ssd_conv_w', 'ssd_conv_b', 'dt_bias', 'a_log', 'd_skip', 'ssd_norm_g', 'w_o_ssd', 'w_out', 'ln1_g',
           'ln1_b', 'w_up', 'ffn_conv_w', 'ffn_conv_b', 'w_down', 'ln2_g', 'ln2_b']
BIG = {'w_in': 2, 'w_q_b': 2, 'w_kv_b': 2, 'w_o_attn': 1, 'w_o_ssd': 1, 'w_out': 1, 'w_up': 2, 'w_down': 1}
SMALL_SHARDED = {'meta_tokens': 1, 'ssd_conv_w': 2, 'ffn_conv_w': 2}
REPLICATED = [n for n in WEIGHTS if n not in BIG and n not in SMALL_SHARDED]
FIRST_USED = ['w_in', 'w_q_b', 'w_kv_b']
AFTER_ATTENTION = [n for n in BIG if n not in FIRST_USED]
SMALL_COLS = LANES


def _flatten(arrs, cols, row_mult, lead=False):
    parts, offs, off = [], [], 0
    for a in arrs:
        a2 = a.reshape(N_DEV, -1) if lead else a.reshape(1, -1)
        n = a2.shape[1]
        pad = -n % cols
        parts.append(jnp.pad(a2, ((0, 0), (0, pad))))
        offs.append((off, n))
        off += n + pad
    rows = off // cols
    extra = (-rows % row_mult) * cols
    if extra:
        parts.append(jnp.zeros((parts[0].shape[0], extra), parts[0].dtype))
    flat = jnp.concatenate(parts, axis=1)
    flat = flat.reshape(flat.shape[0], -1, cols)
    return (flat if lead else flat[0]), offs


def _unflatten(flat, offs, shapes):
    f = flat.reshape(-1)
    return [f[o:o + n].reshape(s) for (o, n), s in zip(offs, shapes)]


def _to_pieces(g, axis):
    s = g.shape[axis] // N_DEV
    g = g.reshape(g.shape[:axis] + (N_DEV, s) + g.shape[axis + 1:])
    return jnp.moveaxis(g, axis, 0).reshape(N_DEV, -1)


def _from_pieces(p, shard_shape, axis):
    g = jnp.moveaxis(p.reshape((N_DEV,) + tuple(shard_shape)), 0, axis)
    sh = list(shard_shape)
    sh[axis] *= N_DEV
    return g.reshape(sh)


def _in_proj_pad(w):
    e = np.cumsum((0,) + IN_SIZES)
    ql, kvl, kpe, z, xbc, dt, ga, gs = [w[:, e[j]:e[j + 1]] for j in range(8)]
    zc = lambda n: jnp.zeros((w.shape[0], n), w.dtype)
    return jnp.concatenate([ql, kvl, z, xbc, ga, gs, kpe, zc(LANES - QK_ROPE), dt, zc(LANES - SSD_HEADS)], axis=1)


def _in_proj_unpad(d):
    seg = lambda o, n: d[:, o:o + n]
    return jnp.concatenate([seg(OQ, Q_LORA), seg(OKV, KV_LORA), seg(OKPE, QK_ROPE), seg(OZ, SSD_INNER),
                            seg(OXBC, SSD_CONV_DIM), seg(ODT, SSD_HEADS), seg(OGA, D_MODEL), seg(OGS, D_MODEL)], axis=1)


def _q_pad(w):
    w3 = w.reshape(Q_LORA, HEADS, QK_NOPE + QK_ROPE)
    return jnp.concatenate([w3, jnp.zeros((Q_LORA, HEADS, QHEAD - QK_NOPE - QK_ROPE), w.dtype)], axis=2).reshape(Q_LORA, HEADS * QHEAD)


def _q_unpad(d):
    return d.reshape(Q_LORA, HEADS, QHEAD)[:, :, :QK_NOPE + QK_ROPE].reshape(Q_LORA, HEADS * (QK_NOPE + QK_ROPE))


def _kv_perm(w):
    w3 = w.reshape(KV_LORA, HEADS, QK_NOPE + V_HEAD)
    return jnp.concatenate([w3[:, :, :QK_NOPE].reshape(KV_LORA, -1), w3[:, :, QK_NOPE:].reshape(KV_LORA, -1)], axis=1)


def _kv_unperm(d):
    kn = d[:, :HEADS * QK_NOPE].reshape(KV_LORA, HEADS, QK_NOPE)
    v = d[:, HEADS * QK_NOPE:].reshape(KV_LORA, HEADS, V_HEAD)
    return jnp.concatenate([kn, v], axis=2).reshape(KV_LORA, HEADS * (QK_NOPE + V_HEAD))


def _row_vec(v, width=None):
    v = v.reshape(1, -1).astype(F32)
    if width is not None and v.shape[1] < width:
        v = jnp.pad(v, ((0, 0), (0, width - v.shape[1])))
    return v


def _pad_rows8(w):
    return jnp.pad(w.astype(F32), ((0, SUBLANES - w.shape[0]), (0, 0)))


def _tables(Tp, npad):
    pos = jnp.maximum(jnp.arange(Tp, dtype=jnp.int32) - npad, 0).astype(F32)
    inv_freq = 1.0 / (ROPE_THETA ** (jnp.arange(0, QK_ROPE, 2, dtype=F32) / QK_ROPE))
    ang = pos[:, None] * inv_freq[None, :]
    ang = jnp.concatenate([ang, ang], axis=-1)
    zeros = jnp.zeros((Tp, LANES - QK_ROPE), F32)
    cos = jnp.concatenate([jnp.cos(ang), zeros], axis=1)
    sin = jnp.concatenate([jnp.sin(ang), zeros], axis=1)
    rot = np.zeros((LANES, LANES), np.float32)
    half = QK_ROPE // 2
    for i in range(half):
        rot[i + half, i] = -1.0
        rot[i, i + half] = 1.0
    expand = np.zeros((LANES, SSD_INNER), np.float32)
    for hd in range(SSD_HEADS):
        expand[hd, hd * SSD_HEAD_DIM:(hd + 1) * SSD_HEAD_DIM] = 1.0
    return cos, sin, jnp.asarray(rot), jnp.asarray(expand)


def _layer_rows(proj, tb):
    rows_a = [_row(proj, Q_LORA, OQ // Q_LORA), _row(proj, KV_LORA, OKV // KV_LORA), _row(proj, LANES, OKPE // LANES),
              _row(proj, LANES, ODT // LANES), _row(tb["cos"], diff=False), _row(tb["sin"], diff=False)]
    return rows_a


def _layer_fwd(h, h_bf, P, tb, fns, npad, bg=None, on_carried=None):
    both = [_out(D_MODEL, F32), _out(D_MODEL, BF16)]
    res_ln_twice = lambda *a: fns["res_ln"](*a) * 2
    proj = _mm(h_bf, P["w_in"], F32, "in_proj")
    rows_a = _layer_rows(proj, tb)
    consts_a = [_row(tb["rot"], diff=False), _row(P["q_norm_g"]), _row(P["kv_norm_g"]), _row(P["dt_bias"])]
    qn, kvn, kr8, dt = _rw_fwd(fns["in_post"], rows_a, consts_a,
                               [_out(Q_LORA, BF16), _out(KV_LORA, BF16), _out(HEADS * LANES, BF16), _out(LANES, F32)],
                               "in_post")
    q = _mm(qn, P["w_q"], F32, "q_proj")
    rows_q = [_row(q, QHEAD, 0, grp=True), _row(tb["cos"], diff=False), _row(tb["sin"], diff=False)]
    qr = _rw_fwd(fns["q_post"], rows_q, [_row(tb["rot"], diff=False)], [_out(HEADS * QHEAD, BF16, QHEAD, grp=True)],
                 "q_post", ng=HEADS)[0]
    kv = _mm(kvn, P["w_kv"], BF16, "kv_proj")
    o, lse, *carried = _flash_fwd(qr, kv, kr8, npad, "attn_fwd_gather" if bg else "attn_fwd", bg=bg)
    if on_carried is not None:
        carried = on_carried(P, carried)
    ya = _mm(o, P["w_o_attn"], F32, "attn_out")
    xbc = _conv_fwd(proj, OXBC, SSD_CONV_DIM, P["ssd_conv_w"], P["ssd_conv_b"], SSD_CONV, True, npad, "ssd_conv")
    y, states = _ssd_fwd(xbc, dt, P["a_log"], tb["expand"], "ssd_fwd")
    rows_b = [_row(y, GW, 0, grp=True), _row(xbc, GW, 0, grp=True), _row(proj, GW, OZ // GW, grp=True)]
    consts_b = [_row(P["d_skip"], GW, 0, grp=True), _row(P["ssd_norm_g"], GW, 0, grp=True)]
    yn = _rw_fwd(fns["gated"], rows_b, consts_b, [_out(SSD_INNER, BF16, GW, grp=True)], "ssd_gate", ng=SSD_GROUPS)[0]
    ys = _mm(yn, P["w_o_ssd"], F32, "ssd_out")
    rows_c = [_row(proj, D_MODEL, OGA // D_MODEL), _row(proj, D_MODEL, OGS // D_MODEL), _row(ya), _row(ys)]
    mixed = _rw_fwd(fns["mix"], rows_c, [], [_out(D_MODEL, BF16)], "mix")[0]
    mo = _mm(mixed, P["w_out"], F32, "mix_out")
    consts_1 = [_row(P["ln1_g"]), _row(P["ln1_b"])]
    h1, h1_bf = _rw_fwd(res_ln_twice, [_row(h), _row(mo)], consts_1, both, "ln1")
    up = _mm(h1_bf, P["w_up"], BF16, "ffn_up")
    u = _conv_fwd(up, 0, 2 * D_FF, P["ffn_conv_w"], P["ffn_conv_b"], FFN_CONV, False, npad, "ffn_conv", BF16)
    act = _rw_fwd(fns["glu"], [_row(u)], [], [_out(D_FF, BF16)], "ffn_glu")[0]
    fo = _mm(act, P["w_down"], F32, "ffn_down")
    consts_2 = [_row(P["ln2_g"]), _row(P["ln2_b"])]
    h2, h2_bf = _rw_fwd(res_ln_twice, [_row(h1), _row(fo)], consts_2, both, "ln2")
    res = dict(h=h, h_bf=h_bf, proj=proj, qn=qn, kvn=kvn, kr8=kr8, dt=dt, q=q, qr=qr, kv=kv, o=o, lse=lse, ya=ya,
               xbc=xbc, y=y, states=states, yn=yn, ys=ys, mixed=mixed, mo=mo, h1=h1, h1_bf=h1_bf, up=up, u=u, act=act,
               fo=fo)
    return h2, h2_bf, res, carried


def _layer_bwd(dh2, r, P, tb, fns, npad, bg=None, before_attn=None):
    g = {}
    consts_2 = [_row(P["ln2_g"]), _row(P["ln2_b"])]
    (dh1_a, dfo), (g["ln2_g"], g["ln2_b"]) = _rw_bwd(fns["res_ln"], [_row(r["h1"]), _row(r["fo"])], consts_2,
                                                     [_row(dh2)], [F32, BF16], "ln2_bwd")
    g["w_down"] = _mm(r["act"], dfo, BF16, "dw_down", ta=True)
    dact = _mm(dfo, P["w_down"], BF16, "d_act", tb=True)
    (du,), _ = _rw_bwd(fns["glu"], [_row(r["u"])], [], [_row(dact)], [BF16], "glu_bwd")
    dup, g["ffn_conv_w"], g["ffn_conv_b"] = _conv_bwd(r["up"], 0, 2 * D_FF, P["ffn_conv_w"], P["ffn_conv_b"], du,
                                                      FFN_CONV, False, npad, "ffn_conv_bwd")
    g["w_up"] = _mm(r["h1_bf"], dup, BF16, "dw_up", ta=True)
    dh1 = _mm(dup, P["w_up"], F32, "d_h1", tb=True, add=dh1_a)
    consts_1 = [_row(P["ln1_g"]), _row(P["ln1_b"])]
    (dh_a, dmo), (g["ln1_g"], g["ln1_b"]) = _rw_bwd(fns["res_ln"], [_row(r["h"]), _row(r["mo"])], consts_1,
                                                    [_row(dh1)], [F32, BF16], "ln1_bwd")
    g["w_out"] = _mm(r["mixed"], dmo, BF16, "dw_out", ta=True)
    dmixed = _mm(dmo, P["w_out"], F32, "d_mixed", tb=True)
    proj = r["proj"]
    rows_c = [_row(proj, D_MODEL, OGA // D_MODEL), _row(proj, D_MODEL, OGS // D_MODEL), _row(r["ya"]), _row(r["ys"])]
    (dga, dgs, dya, dys), _ = _rw_bwd(fns["mix"], rows_c, [], [_row(dmixed)], [BF16] * 4, "mix_bwd")
    g["w_o_attn"] = _mm(r["o"], dya, BF16, "dw_o_attn", ta=True)
    do = _mm(dya, P["w_o_attn"], F32, "d_o", tb=True)
    g["w_o_ssd"] = _mm(r["yn"], dys, BF16, "dw_o_ssd", ta=True)
    dyn = _mm(dys, P["w_o_ssd"], F32, "d_yn", tb=True)
    rows_b = [_row(r["y"], GW, 0, grp=True), _row(r["xbc"], GW, 0, grp=True), _row(proj, GW, OZ // GW, grp=True)]
    consts_b = [_row(P["d_skip"], GW, 0, grp=True), _row(P["ssd_norm_g"], GW, 0, grp=True)]
    (dy, dxs_skip, dz), (g["d_skip"], g["ssd_norm_g"]) = _rw_bwd(
        fns["gated"], rows_b, consts_b, [_row(dyn, GW, 0, grp=True)], [F32, F32, BF16], "ssd_gate_bwd", ng=SSD_GROUPS)
    dxbc, ddt, g["a_log"] = _ssd_bwd(r["xbc"], r["dt"], P["a_log"], tb["expand"], dy, dxs_skip, r["states"], "ssd_bwd")
    dxbc_pre, g["ssd_conv_w"], g["ssd_conv_b"] = _conv_bwd(proj, OXBC, SSD_CONV_DIM, P["ssd_conv_w"], P["ssd_conv_b"],
                                                           dxbc, SSD_CONV, True, npad, "ssd_conv_bwd")
    delta = _attn_delta(do, r["o"], "attn_delta")
    if before_attn is not None:
        bg = before_attn(g)
    dqr, dkn, dkr8, dv, *carried = _flash_bwd(r["qr"], r["kv"], r["kr8"], do, r["lse"], delta, npad,
                                              "attn_bwd_exchange" if bg else "attn_bwd", bg=bg)
    rows_q = [_row(r["q"], QHEAD, 0, grp=True), _row(tb["cos"], diff=False), _row(tb["sin"], diff=False)]
    (dq,), _ = _rw_bwd(fns["q_post"], rows_q, [_row(tb["rot"], diff=False)], [_row(dqr, QHEAD, 0, grp=True)], [BF16],
                       "q_post_bwd", ng=HEADS)
    g["w_q"] = _mm(r["qn"], dq, BF16, "dw_q", ta=True)
    dqn = _mm(dq, P["w_q"], F32, "d_qn", tb=True)
    dkv = jnp.concatenate([dkn, dv], axis=1)
    g["w_kv"] = _mm(r["kvn"], dkv, BF16, "dw_kv", ta=True)
    dkvn = _mm(dkv, P["w_kv"], F32, "d_kvn", tb=True)
    rows_a = _layer_rows(proj, tb)
    consts_a = [_row(tb["rot"], diff=False), _row(P["q_norm_g"]), _row(P["kv_norm_g"]), _row(P["dt_bias"])]
    (dql, dkvl, dkpe, ddtr), (g["q_norm_g"], g["kv_norm_g"], g["dt_bias"]) = _rw_bwd(
        fns["in_post"], rows_a, consts_a, [_row(dqn), _row(dkvn), _row(dkr8), _row(ddt)], [BF16] * 4, "in_post_bwd")
    dproj = jnp.concatenate([dql, dkvl, dz, dxbc_pre, dga, dgs, dkpe, ddtr], axis=1)
    g["w_in"] = _mm(r["h_bf"], dproj, BF16, "dw_in", ta=True)
    dh = _mm(dproj, P["w_in"], F32, "d_h", tb=True, add=dh_a)
    return dh, g, carried


def _full_weight(g, axis):
    if axis == 1:
        return g.reshape(-1, g.shape[-1])
    return jnp.transpose(g, (1, 0, 2)).reshape(g.shape[1], -1)


def _grad_pieces(d, axis):
    if axis == 1:
        return d.reshape(N_DEV, -1, d.shape[1])
    return jnp.transpose(d.reshape(d.shape[0], N_DEV, -1), (1, 0, 2))


def _big_params(gathered):
    prep = {"w_in": ("w_in", _in_proj_pad), "w_q_b": ("w_q", _q_pad), "w_kv_b": ("w_kv", _kv_perm)}
    P = {}
    for n, g in gathered.items():
        key, fn = prep.get(n, (n, lambda a: a))
        P[key] = fn(_full_weight(g, BIG[n]))
    return P


def _layer_params(gathered, small, i):
    P = _big_params(gathered)
    P["q_norm_g"] = _row_vec(small["q_norm_g"][i])
    P["kv_norm_g"] = _row_vec(small["kv_norm_g"][i])
    P["dt_bias"] = _row_vec(small["dt_bias"][i], LANES)
    P["a_log"] = _row_vec(small["a_log"][i], LANES)
    P["d_skip"] = _row_vec(jnp.repeat(small["d_skip"][i], SSD_HEAD_DIM))
    P["ssd_norm_g"] = _row_vec(small["ssd_norm_g"][i])
    P["ssd_conv_w"] = _pad_rows8(small["ssd_conv_w"][i])
    P["ssd_conv_b"] = _row_vec(small["ssd_conv_b"][i])
    P["ffn_conv_w"] = _pad_rows8(small["ffn_conv_w"][i])
    P["ffn_conv_b"] = _row_vec(small["ffn_conv_b"][i])
    for n in ("ln1_g", "ln1_b", "ln2_g", "ln2_b"):
        P[n] = _row_vec(small[n][i])
    return P


def _layer_grads_to_reference_layout(g):
    out = {}
    out["w_in"] = _in_proj_unpad(g["w_in"])
    out["w_q_b"] = _q_unpad(g["w_q"])
    out["w_kv_b"] = _kv_unperm(g["w_kv"])
    for n in ("w_o_attn", "w_o_ssd", "w_out", "w_up", "w_down"):
        out[n] = g[n]
    out["q_norm_g"] = g["q_norm_g"][0]
    out["kv_norm_g"] = g["kv_norm_g"][0]
    out["dt_bias"] = g["dt_bias"][0, :SSD_HEADS]
    out["a_log"] = g["a_log"][0, :SSD_HEADS]
    out["d_skip"] = g["d_skip"].reshape(SSD_HEADS, SSD_HEAD_DIM).sum(axis=1)
    out["ssd_norm_g"] = g["ssd_norm_g"][0]
    out["ssd_conv_w"] = g["ssd_conv_w"][:SSD_CONV]
    out["ssd_conv_b"] = g["ssd_conv_b"][0]
    out["ffn_conv_w"] = g["ffn_conv_w"][:FFN_CONV]
    out["ffn_conv_b"] = g["ffn_conv_b"][0]
    for n in ("ln1_g", "ln1_b", "ln2_g", "ln2_b"):
        out[n] = g[n][0]
    return out


def kernel(x, meta_tokens, emb_ln_g, emb_ln_b, w_in, q_norm_g, w_q_b, kv_norm_g, w_kv_b, w_o_attn, ssd_conv_w, ssd_conv_b, dt_bias, a_log, d_skip, ssd_norm_g, w_o_ssd, w_out, ln1_g, ln1_b, w_up, ffn_conv_w, ffn_conv_b, w_down, ln2_g, ln2_b, loss_target, m_meta_tokens, m_emb_ln_g, m_emb_ln_b, m_w_in, m_q_norm_g, m_w_q_b, m_kv_norm_g, m_w_kv_b, m_w_o_attn, m_ssd_conv_w, m_ssd_conv_b, m_dt_bias, m_a_log, m_d_skip, m_ssd_norm_g, m_w_o_ssd, m_w_out, m_ln1_g, m_ln1_b, m_w_up, m_ffn_conv_w, m_ffn_conv_b, m_w_down, m_ln2_g, m_ln2_b, v_meta_tokens, v_emb_ln_g, v_emb_ln_b, v_w_in, v_q_norm_g, v_w_q_b, v_kv_norm_g, v_w_kv_b, v_w_o_attn, v_ssd_conv_w, v_ssd_conv_b, v_dt_bias, v_a_log, v_d_skip, v_ssd_norm_g, v_w_o_ssd, v_w_out, v_ln1_g, v_ln1_b, v_w_up, v_ffn_conv_w, v_ffn_conv_b, v_w_down, v_ln2_g, v_ln2_b):
    given = dict(locals())
    w = {n: given[n] for n in WEIGHTS}
    m = {n: given["m_" + n] for n in WEIGHTS}
    v = {n: given["v_" + n] for n in WEIGHTS}
    seq = x.shape[1]
    assert x.shape[0] == 1 and seq % LANES == 0
    npad = LANES - N_META
    Tp = npad + N_META + seq
    depth = w_in.shape[0]

    big_names, small_names = list(BIG), list(SMALL_SHARDED)
    ws, offs_s = _flatten([w[n] for n in small_names], SMALL_COLS, SUBLANES)
    shards = [{n: w[n][i].astype(BF16) for n in big_names} for i in range(depth)]
    got = _allgather([shards[0][n] for n in FIRST_USED] + [ws], "weight_allgather")
    gathered = dict(zip(FIRST_USED, got[:-1]))
    gsm = got[-1]
    small = {n: w[n] for n in REPLICATED}
    for n, (o, sz) in zip(small_names, offs_s):
        small[n] = _from_pieces(gsm.reshape(N_DEV, -1)[:, o:o + sz], w[n].shape, SMALL_SHARDED[n])

    fns = _make_stage_fns(npad)
    cos, sin, rot, expand = _tables(Tp, npad)
    tb = dict(cos=cos, sin=sin, rot=rot, expand=expand)
    top = jnp.pad(small["meta_tokens"], ((npad, 0), (0, 0)))
    hcat = jnp.concatenate([top, x[0]], axis=0)
    consts_e = [_row(_row_vec(w["emb_ln_g"])), _row(_row_vec(w["emb_ln_b"]))]
    h, h_bf = _rw_fwd(lambda *a: fns["ln"](*a) * 2, [_row(hcat)], consts_e, [_out(D_MODEL, F32), _out(D_MODEL, BF16)],
                      "emb_ln")
    layers, saved = [], []
    for i in range(depth):
        layers.append(_layer_params(gathered, small, i))
        late = AFTER_ATTENTION if i == 0 else []
        nxt = big_names if i + 1 < depth else []
        arrs = [shards[i][n] for n in late] + [shards[i + 1][n] for n in nxt]

        def on_carried(P, carried, late=late):
            P.update(_big_params(dict(zip(late, carried[:len(late)]))))
            return carried[len(late):]

        h, h_bf, res, carried = _layer_fwd(h, h_bf, layers[i], tb, fns, npad,
                                           bg=_Background("gather", arrs) if arrs else None, on_carried=on_carried)
        gathered = dict(zip(nxt, carried))
        saved.append(res)
    dh, lparts = _loss_head(h, loss_target[0], "loss_head")
    loss = lax.psum(jnp.sum(lparts[:, 0, 0]), ("x", "y", "c"))

    core = lax.axis_index("c")

    def chip_partials(pieces, tag):
        from_sibling = _sibling_exchange(pieces, "grad_exchange_cores_" + tag)
        sums = []
        for k, (p, r) in enumerate(zip(pieces, from_sibling)):
            own = lax.dynamic_index_in_dim(p.reshape((N_CHIPS, 2) + p.shape[1:]), core, axis=1, keepdims=False)
            sums.append(_add_pairs(own, r, "grad_chip_sum_%s_%d" % (tag, k)))
        return sums

    lg, recv_big, pending = [None] * depth, [None] * depth, []
    for i in reversed(range(depth)):
        early = AFTER_ATTENTION if i == 0 else []

        def before_attn(g, pending=pending, early=early, i=i):
            sums = pending + (chip_partials([_grad_pieces(g[n], BIG[n]) for n in early], "l%d_early" % i) if early else [])
            return _Background("chips", sums) if sums else None

        dh, gi, carried = _layer_bwd(dh, saved[i], layers[i], tb, fns, npad, before_attn=before_attn)
        if pending:
            recv_big[i + 1] = dict(zip(big_names, carried[:len(pending)]))
        recv_big[i] = dict(zip(early, carried[len(pending):]))
        lg[i] = _layer_grads_to_reference_layout(gi)
        pending = []
        if i > 0:
            pending = chip_partials([_grad_pieces(lg[i][n], BIG[n]) for n in big_names], "l%d" % i)
    (dhcat,), (d_emb_g, d_emb_b) = _rw_bwd(fns["ln"], [_row(hcat)], consts_e, [_row(dh)], [F32], "emb_ln_bwd")
    grad_x = dhcat[LANES:][None]
    local = {n: jnp.stack([lg[i][n] for i in range(depth)]) for n in lg[0] if n not in BIG}
    local["meta_tokens"] = dhcat[npad:LANES]
    local["emb_ln_g"] = d_emb_g[0]
    local["emb_ln_b"] = d_emb_b[0]

    sm_names = small_names + REPLICATED
    sm_pieces = [_to_pieces(local[n], SMALL_SHARDED[n]) for n in small_names]
    sm_pieces += [jnp.broadcast_to(local[n].reshape(1, -1), (N_DEV, local[n].size)) for n in REPLICATED]
    ps, _ = _flatten(sm_pieces, SMALL_COLS, BF16_ROWS, lead=True)
    pieces = [_grad_pieces(lg[0][n], BIG[n]) for n in FIRST_USED] + [ps]
    recv = _chip_exchange(chip_partials(pieces, "l0"), "grad_exchange_chips")
    recv_big[0].update(zip(FIRST_USED, recv[:-1]))
    outs = {}
    kinds = ("grad", "delta", "new_m", "new_v")
    for n in big_names:
        parts = jnp.stack([recv_big[i][n] for i in range(depth)], axis=1)
        for kind, a in zip(kinds, _adamw(parts, w[n], m[n], v[n], "adamw_" + n)):
            outs[kind + "_" + n] = a
    wf, offs = _flatten([w[n] for n in sm_names], SMALL_COLS, BF16_ROWS)
    mf, _ = _flatten([m[n] for n in sm_names], SMALL_COLS, BF16_ROWS)
    vf, _ = _flatten([v[n] for n in sm_names], SMALL_COLS, BF16_ROWS)
    shapes = [w[n].shape for n in sm_names]
    for kind, flat in zip(kinds, _adamw(recv[-1], wf, mf, vf, "adamw_small")):
        for n, a in zip(sm_names, _unflatten(flat, offs, shapes)):
            outs[kind + "_" + n] = a
    result = [loss, grad_x]
    for kind in ("grad", "delta", "new_m", "new_v"):
        result += [outs[kind + "_" + n] for n in WEIGHTS]
    return tuple(result)
```

```python
import functools

import jax
import jax.numpy as jnp
import numpy as np
from jax import lax
from jax.experimental import pallas as pl
from jax.experimental.pallas import tpu as pltpu

F32 = jnp.float32
BF16 = jnp.bfloat16
HIGHEST = lax.Precision.HIGHEST
SSD_PREC = lax.Precision.HIGH

D_MODEL = 1024
DEPTH = 2
N_META = 16
HEADS = 8
Q_LORA = 768
KV_LORA = 256
QK_NOPE = 128
QK_ROPE = 64
V_HEAD = 128
ROPE_THETA = 10000.0
SSD_INNER = 2048
SSD_HEAD_DIM = 64
SSD_HEADS = 32
SSD_GROUPS = 4
SSD_STATE = 128
SSD_CONV = 4
SSD_CONV_DIM = SSD_INNER + 2 * SSD_GROUPS * SSD_STATE
CHUNK = 128
D_FF = 2816
FFN_CONV = 3
LN_EPS = 1e-5
RMS_EPS = 1e-6
ALPHA = (2 * DEPTH) ** 0.25
IN_SIZES = (Q_LORA, KV_LORA, QK_ROPE, SSD_INNER, SSD_CONV_DIM, SSD_HEADS, D_MODEL, D_MODEL)
ATT_SCALE = (QK_NOPE + QK_ROPE) ** -0.5
NEG_INF = -1e30
ADAM_LR, ADAM_B1, ADAM_B2, ADAM_EPS, ADAM_WD, ADAM_STEP = 0.001, 0.9, 0.999, 1e-08, 0.01, 10

LANES = 128
SUBLANES = 8
VMEM_BYTES = 64 * 1024 * 1024
N_DEV = 8

OQ, OKV, OZ, OXBC, OGA, OGS, OKPE, ODT = 0, 768, 1024, 3072, 6144, 7168, 8192, 8320
IN_PAD = 8448
QHEAD = 256

ROW_TILE = 640
MM_COL_TILE = 1408
MM_ROW_TILE = 1664
MM_VMEM_BUDGET = 46 * 1024 * 1024
MM_K_TILE = 2816
MM_TOKEN_K_TILE = 1664
ATT_TILE = 640
ATT_HEADS_PER_STEP = 8
BF16_ROWS = 16
HALO = BF16_ROWS
ROW_BUDGET = 18 * 1024 * 1024
ADAM_ELEMS = 160 * 1024


def _pick(n, target, q=LANES):
    assert n % q == 0, (n, q)
    units = n // q
    best = q
    for d in range(1, units + 1):
        if units % d == 0 and d * q <= target:
            best = d * q
    return best


def _pick_rows(n, row_bytes):
    return _pick(n, max(BF16_ROWS, ROW_BUDGET // row_bytes), BF16_ROWS)


def _params(sem, est_bytes):
    limit = int(min(VMEM_BYTES - (6 << 20), max(32 << 20, 2 * est_bytes + (8 << 20))))
    return pltpu.CompilerParams(dimension_semantics=sem, vmem_limit_bytes=limit)


def _nbytes(shape, dtype):
    return int(np.prod(shape)) * jnp.dtype(dtype).itemsize


def _mm(a, b, out_dtype, name, ta=False, tb=False, add=None):
    assert not (ta and tb)
    if ta:
        K, M = a.shape
        tm = _pick(M, MM_COL_TILE)
        tk = _pick(K, MM_TOKEN_K_TILE)
    else:
        M, K = a.shape
        tk = _pick(K, MM_K_TILE)
    N, K2 = (b.shape if tb else b.shape[::-1])
    assert K == K2
    tn = _pick(N, MM_COL_TILE)
    nk = K // tk

    def vmem_estimate(tm):
        e = 2 * (tm * tk * a.dtype.itemsize + tk * tn * b.dtype.itemsize + tm * tn * jnp.dtype(out_dtype).itemsize)
        e += tm * tn * 4 + tm * tk * 2
        return e + (tm * tn * 4 if nk > 1 else 0) + (2 * tm * tn * 4 if add is not None else 0)

    if not ta:
        tm = _pick(M, MM_ROW_TILE, BF16_ROWS)
        while vmem_estimate(tm) > MM_VMEM_BUDGET and tm > BF16_ROWS:
            tm = _pick(M, tm - BF16_ROWS, BF16_ROWS)
    dn = (((0,), (0,)), ((), ())) if ta else ((((1,), (1,)), ((), ())) if tb else (((1,), (0,)), ((), ())))

    def body(*refs):
        a_ref, b_ref = refs[:2]
        add_ref = refs[2] if add is not None else None
        o_ref = refs[2 + (add is not None)]
        d = lax.dot_general(a_ref[...].astype(BF16), b_ref[...].astype(BF16), dn, preferred_element_type=F32)

        def finish(r):
            if add is not None:
                r = r + add_ref[...].astype(F32)
            o_ref[...] = r.astype(out_dtype)

        if nk == 1:
            finish(d)
            return
        acc = refs[-1]
        k = pl.program_id(2)

        @pl.when(k == 0)
        def _():
            acc[...] = d

        @pl.when((k > 0) & (k < nk - 1))
        def _():
            acc[...] += d

        @pl.when(k == nk - 1)
        def _():
            finish(acc[...] + d)

    if ta:
        a_spec = pl.BlockSpec((tk, tm), lambda i, j, k: (k, i))
    else:
        a_spec = pl.BlockSpec((tm, tk), lambda i, j, k: (i, k))
    b_spec = pl.BlockSpec((tn, tk), lambda i, j, k: (j, k)) if tb else pl.BlockSpec((tk, tn), lambda i, j, k: (k, j))
    in_specs = [a_spec, b_spec]
    args = [a, b]
    est = vmem_estimate(tm)
    if add is not None:
        in_specs.append(pl.BlockSpec((tm, tn), lambda i, j, k: (i, j)))
        args.append(add)
    return pl.pallas_call(
        body, name=name, grid=(M // tm, N // tn, nk), in_specs=in_specs,
        out_specs=pl.BlockSpec((tm, tn), lambda i, j, k: (i, j)),
        out_shape=jax.ShapeDtypeStruct((M, N), out_dtype),
        scratch_shapes=[pltpu.VMEM((tm, tn), F32)] if nk > 1 else [],
        compiler_params=_params(("parallel", "parallel", "arbitrary"), est),
    )(*args)


def _row(arr, bw=None, cb=0, grp=False, diff=True):
    return dict(arr=arr, bw=arr.shape[1] if bw is None else bw, cb=cb, grp=grp, diff=diff)


def _out(width, dtype, bw=None, grp=False):
    return dict(width=width, dtype=dtype, bw=width if bw is None else bw, grp=grp)


def _spec_rows(d, tm):
    return pl.BlockSpec((tm, d["bw"]), lambda g, i, cb=d["cb"], gr=d["grp"]: (i, cb + (g if gr else 0)))


def _spec_const(d):
    return pl.BlockSpec((d["arr"].shape[0], d["bw"]), lambda g, i, cb=d["cb"], gr=d["grp"]: (0, cb + (g if gr else 0)))


def _rw_fwd(fn, rows, consts, outs, name, ng=1):
    Tp = rows[0]["arr"].shape[0]
    tm = _pick_rows(Tp, 4 * (sum(d["bw"] for d in rows) + 2 * sum(o["bw"] for o in outs)))
    nr, ncst = len(rows), len(consts)

    def body(*refs):
        i = pl.program_id(1)
        rowidx = i * tm + lax.broadcasted_iota(jnp.int32, (tm, 1), 0)
        rv = [r[...].astype(F32) for r in refs[:nr]]
        cv = [c[...] for c in refs[nr:nr + ncst]]
        vals = fn(rowidx, *rv, *cv)
        for o, v in zip(refs[nr + ncst:], vals):
            o[...] = v.astype(o.dtype)

    est = sum(tm * d["bw"] * 4 for d in rows) + sum(tm * o["bw"] * 4 for o in outs)
    return pl.pallas_call(
        body, name=name, grid=(ng, Tp // tm),
        in_specs=[_spec_rows(d, tm) for d in rows] + [_spec_const(d) for d in consts],
        out_specs=[pl.BlockSpec((tm, o["bw"]), lambda g, i, gr=o["grp"]: (i, g if gr else 0)) for o in outs],
        out_shape=[jax.ShapeDtypeStruct((Tp, o["width"]), o["dtype"]) for o in outs],
        compiler_params=_params(("parallel", "parallel"), 3 * est),
    )(*[d["arr"] for d in rows], *[d["arr"] for d in consts])


def _rw_bwd(fn, rows, consts, cots, drow_dtypes, name, ng=1):
    Tp = rows[0]["arr"].shape[0]
    tm = _pick_rows(Tp, 4 * (3 * sum(d["bw"] for d in rows) + 2 * sum(d["bw"] for d in cots)))
    nr, ncst, nct = len(rows), len(consts), len(cots)
    drows = [k for k, d in enumerate(rows) if d["diff"]]
    dcsts = [k for k, d in enumerate(consts) if d["diff"]]
    for k in drows:
        assert rows[k]["grp"] or ng == 1

    def body(*refs):
        g = pl.program_id(0)
        i = pl.program_id(1)
        rowidx = i * tm + lax.broadcasted_iota(jnp.int32, (tm, 1), 0)
        rv = [r[...].astype(F32) for r in refs[:nr]]
        cv = [c[...] for c in refs[nr:nr + ncst]]
        ct = tuple(r[...].astype(F32) for r in refs[nr + ncst:nr + ncst + nct])
        orefs = refs[nr + ncst + nct:]

        def f(*dargs):
            rr, cc = list(rv), list(cv)
            for k, v in zip(drows, dargs[:len(drows)]):
                rr[k] = v
            for k, v in zip(dcsts, dargs[len(drows):]):
                cc[k] = v
            return tuple(fn(rowidx, *rr, *cc))

        _, vjp = jax.vjp(f, *[rv[k] for k in drows], *[cv[k] for k in dcsts])
        grads = vjp(ct)
        for o, v in zip(orefs[:len(drows)], grads[:len(drows)]):
            o[...] = v.astype(o.dtype)
        for k, o, v in zip(dcsts, orefs[len(drows):], grads[len(drows):]):
            first = (i == 0) if consts[k]["grp"] else ((i == 0) & (g == 0))

            @pl.when(first)
            def _(o=o, v=v):
                o[...] = v

            @pl.when(jnp.logical_not(first))
            def _(o=o, v=v):
                o[...] += v

    out_specs, out_shape = [], []
    for k, dt in zip(drows, drow_dtypes):
        d = rows[k]
        out_specs.append(pl.BlockSpec((tm, d["bw"]), lambda g, i, gr=d["grp"]: (i, g if gr else 0)))
        out_shape.append(jax.ShapeDtypeStruct((Tp, d["bw"] * (ng if d["grp"] else 1)), dt))
    for k in dcsts:
        d = consts[k]
        r = d["arr"].shape[0]
        out_specs.append(pl.BlockSpec((r, d["bw"]), lambda g, i, gr=d["grp"]: (0, g if gr else 0)))
        out_shape.append(jax.ShapeDtypeStruct((r, d["bw"] * (ng if d["grp"] else 1)), F32))
    est = sum(tm * d["bw"] * 4 for d in rows) * 2 + sum(tm * d["bw"] * 4 for d in cots)
    res = pl.pallas_call(
        body, name=name, grid=(ng, Tp // tm),
        in_specs=[_spec_rows(d, tm) for d in rows] + [_spec_const(d) for d in consts] + [_spec_rows(d, tm) for d in cots],
        out_specs=out_specs, out_shape=out_shape,
        compiler_params=_params(("arbitrary", "arbitrary"), 3 * est),
    )(*[d["arr"] for d in rows], *[d["arr"] for d in consts], *[d["arr"] for d in cots])
    return list(res[:len(drows)]), list(res[len(drows):])


def _sigmoid(x):
    return 0.5 * jnp.tanh(0.5 * x) + 0.5


def _silu(x):
    return x * _sigmoid(x)


def _softplus(x):
    return jnp.maximum(x, 0.0) + jnp.log(1.0 + jnp.exp(-jnp.abs(x)))


def _layer_norm(x, g, b):
    mu = jnp.mean(x, axis=-1, keepdims=True)
    xc = x - mu
    var = jnp.mean(xc * xc, axis=-1, keepdims=True)
    return xc * lax.rsqrt(var + LN_EPS) * g + b


def _rms_norm(x, g):
    return x * lax.rsqrt(jnp.mean(x * x, axis=-1, keepdims=True) + RMS_EPS) * g


def _rope(r, cos, sin, rot):
    return r * cos + jnp.dot(r, rot, precision=HIGHEST, preferred_element_type=F32) * sin


def _make_stage_fns(npad):
    def fn_ln_masked(rowidx, x, g, b):
        return (jnp.where(rowidx >= npad, _layer_norm(x, g, b), 0.0),)

    def fn_in_post(rowidx, ql, kvl, kpe, dtr, cos, sin, rot, qg, kvg, dtb):
        qn = _rms_norm(ql, qg)
        kvn = _rms_norm(kvl, kvg)
        kr = _rope(kpe, cos, sin, rot)
        lane = lax.broadcasted_iota(jnp.int32, (1, LANES), 1)
        dt = jnp.where((rowidx >= npad) & (lane < SSD_HEADS), _softplus(dtr + dtb), 0.0)
        return qn, kvn, jnp.concatenate([kr] * HEADS, axis=1), dt

    def fn_q_post(rowidx, q, cos, sin, rot):
        rr = _rope(q[:, QK_NOPE:], cos, sin, rot)
        return (jnp.concatenate([q[:, :QK_NOPE], rr], axis=1) * ATT_SCALE,)

    def fn_gated_norm(rowidx, y, xs, z, dskip, g):
        v = (y + xs * dskip) * _silu(z)
        return (v * lax.rsqrt(jnp.mean(v * v, axis=-1, keepdims=True) + RMS_EPS) * g,)

    def fn_mix(rowidx, ga, gs, ya, ys):
        return (_sigmoid(ga) * ya + _sigmoid(gs) * ys,)

    def fn_res_ln(rowidx, h, r, g, b):
        return (jnp.where(rowidx >= npad, _layer_norm(ALPHA * h + r, g, b), 0.0),)

    def fn_glu(rowidx, u):
        return (_silu(u[:, :D_FF]) * u[:, D_FF:],)

    return dict(ln=fn_ln_masked, in_post=fn_in_post, q_post=fn_q_post, gated=fn_gated_norm, mix=fn_mix,
                res_ln=fn_res_ln, glu=fn_glu)


def _conv_tiles(Tp, C):
    return _pick(Tp, ROW_TILE), _pick(C, MM_COL_TILE)


def _conv_fwd(x, xoff, C, w8, b, K, act, npad, name, out_dtype=F32):
    Tp = x.shape[0]
    tm, tc = _conv_tiles(Tp, C)
    assert xoff % tc == 0
    cb0 = xoff // tc
    rb = tm // HALO

    def body(prev_ref, main_ref, w_ref, b_ref, o_ref):
        i = pl.program_id(1)
        main = main_ref[...].astype(F32)
        prev = jnp.where(i > 0, prev_ref[...].astype(F32), 0.0)
        ext = jnp.concatenate([prev, main], axis=0)
        acc = b_ref[...] + w_ref[K - 1:K, :] * main
        for k in range(K - 1):
            s = K - 1 - k
            acc = acc + w_ref[k:k + 1, :] * pltpu.roll(ext, s, 0)[HALO:, :]
        if act:
            rowidx = i * tm + lax.broadcasted_iota(jnp.int32, (tm, 1), 0)
            acc = jnp.where(rowidx >= npad, _silu(acc), 0.0)
        o_ref[...] = acc.astype(o_ref.dtype)

    return pl.pallas_call(
        body, name=name, grid=(C // tc, Tp // tm),
        in_specs=[pl.BlockSpec((HALO, tc), lambda g, i: (jnp.maximum(i * rb - 1, 0), cb0 + g)),
                  pl.BlockSpec((tm, tc), lambda g, i: (i, cb0 + g)),
                  pl.BlockSpec((SUBLANES, tc), lambda g, i: (0, g)),
                  pl.BlockSpec((1, tc), lambda g, i: (0, g))],
        out_specs=pl.BlockSpec((tm, tc), lambda g, i: (i, g)),
        out_shape=jax.ShapeDtypeStruct((Tp, C), out_dtype),
        compiler_params=_params(("parallel", "parallel"), 8 * tm * tc * 4),
    )(x, x, w8, b)


def _conv_bwd(x, xoff, C, w8, b, dy, K, act, npad, name):
    Tp = x.shape[0]
    tm, tc = _conv_tiles(Tp, C)
    cb0 = xoff // tc
    rb = tm // HALO
    ni = Tp // tm
    last_rb = Tp // HALO - 1
    n = tm + 2 * HALO

    def body(xp_ref, xm_ref, xn_ref, dym_ref, dyn_ref, w_ref, b_ref, dx_ref, dw_ref, db_ref):
        i = pl.program_id(1)
        prev = jnp.where(i > 0, xp_ref[...].astype(F32), 0.0)
        ext = jnp.concatenate([prev, xm_ref[...].astype(F32), xn_ref[...].astype(F32)], axis=0)
        dyn = jnp.where(i < ni - 1, dyn_ref[...].astype(F32), 0.0)
        dpre = jnp.concatenate([jnp.zeros((HALO, tc), F32), dym_ref[...].astype(F32), dyn], axis=0)
        shifted = [ext if k == K - 1 else pltpu.roll(ext, K - 1 - k, 0) for k in range(K)]
        if act:
            pre = b_ref[...] + sum(w_ref[k:k + 1, :] * shifted[k] for k in range(K))
            rowidx = i * tm - HALO + lax.broadcasted_iota(jnp.int32, (n, 1), 0)
            sg = _sigmoid(pre)
            dpre = jnp.where(rowidx >= npad, dpre * sg * (1.0 + pre * (1.0 - sg)), 0.0)
        dx = w_ref[K - 1:K, :] * dpre
        for k in range(K - 1):
            dx = dx + w_ref[k:k + 1, :] * pltpu.roll(dpre, n - (K - 1 - k), 0)
        dx_ref[...] = dx[HALO:HALO + tm, :].astype(dx_ref.dtype)

        @pl.when(i == 0)
        def _():
            dw_ref[...] = jnp.zeros_like(dw_ref)
            db_ref[...] = jnp.zeros_like(db_ref)

        dmain = dpre[HALO:HALO + tm, :]
        for k in range(K):
            dw_ref[k:k + 1, :] += jnp.sum(dmain * shifted[k][HALO:HALO + tm, :], axis=0, keepdims=True)
        db_ref[...] += jnp.sum(dmain, axis=0, keepdims=True)

    return pl.pallas_call(
        body, name=name, grid=(C // tc, ni),
        in_specs=[pl.BlockSpec((HALO, tc), lambda g, i: (jnp.maximum(i * rb - 1, 0), cb0 + g)),
                  pl.BlockSpec((tm, tc), lambda g, i: (i, cb0 + g)),
                  pl.BlockSpec((HALO, tc), lambda g, i: (jnp.minimum((i + 1) * rb, last_rb), cb0 + g)),
                  pl.BlockSpec((tm, tc), lambda g, i: (i, g)),
                  pl.BlockSpec((HALO, tc), lambda g, i: (jnp.minimum((i + 1) * rb, last_rb), g)),
                  pl.BlockSpec((SUBLANES, tc), lambda g, i: (0, g)),
                  pl.BlockSpec((1, tc), lambda g, i: (0, g))],
        out_specs=[pl.BlockSpec((tm, tc), lambda g, i: (i, g)),
                   pl.BlockSpec((SUBLANES, tc), lambda g, i: (0, g)),
                   pl.BlockSpec((1, tc), lambda g, i: (0, g))],
        out_shape=[jax.ShapeDtypeStruct((Tp, C), BF16), jax.ShapeDtypeStruct((SUBLANES, C), F32),
                   jax.ShapeDtypeStruct((1, C), F32)],
        compiler_params=_params(("parallel", "arbitrary"), 14 * tm * tc * 4),
    )(x, x, x, dy, dy, w8, b)


def _split_refs(refs, n_in, n_out, n_scratch, nbg):
    cuts = np.cumsum([0, n_in, nbg, n_out, nbg, n_scratch])
    return tuple(refs[a:b] for a, b in zip(cuts[:-1], cuts[1:])) + (refs[cuts[-1]:],)


def _flash_fwd(q, kv, kr8, npad, name, bg=None):
    Tp = q.shape[0]
    t = _pick(Tp, ATT_TILE)
    hp = ATT_HEADS_PER_STEP
    nb = Tp // t
    ng = HEADS // hp
    nbg = bg.n if bg else 0
    nt = (((1,), (1,)), ((), ()))
    tn = (((0,), (0,)), ((), ()))

    def body(*refs):
        (q_ref, kn_ref, kr_ref, v_ref), bg_in, (o_ref, lse_ref), bg_out, (m_sc, l_sc, acc_sc), bg_sems = _split_refs(
            refs, 4, 2, 3, nbg)
        g = pl.program_id(0)
        qi = pl.program_id(1)
        ki = pl.program_id(2)
        if bg:
            @pl.when((g == 0) & (qi == 0) & (ki == 0))
            def _():
                bg.start(bg_in, bg_out, bg_sems)

        @pl.when(ki == 0)
        def _():
            m_sc[...] = jnp.full_like(m_sc, NEG_INF)
            l_sc[...] = jnp.zeros_like(l_sc)
            acc_sc[...] = jnp.zeros_like(acc_sc)

        def step(masked):
            kr = kr_ref[...]
            if masked:
                key = ki * t + lax.broadcasted_iota(jnp.int32, (t, t), 0)
                qry = qi * t + lax.broadcasted_iota(jnp.int32, (t, t), 1)
                visible = (key <= qry) & (key >= npad)
            for hh in range(hp):
                k = jnp.concatenate([kn_ref[:, hh * QK_NOPE:(hh + 1) * QK_NOPE], kr], axis=1)
                st = lax.dot_general(k, q_ref[:, hh * QHEAD:(hh + 1) * QHEAD], nt, preferred_element_type=F32)
                if masked:
                    st = jnp.where(visible, st, NEG_INF)
                vs = slice(hh * V_HEAD, (hh + 1) * V_HEAD)
                m_prev = m_sc[hh]
                m_new = jnp.maximum(m_prev, jnp.max(st, axis=0, keepdims=True))
                pt = jnp.exp(st - m_new)
                a = jnp.exp(m_prev - m_new)
                l_sc[hh] = a * l_sc[hh] + jnp.sum(pt, axis=0, keepdims=True)
                acc_sc[vs, :] = a * acc_sc[vs, :] + lax.dot_general(v_ref[:, vs], pt.astype(BF16), tn,
                                                                    preferred_element_type=F32)
                m_sc[hh] = m_new

        need_mask = (ki == qi) | (ki == 0)

        @pl.when((ki <= qi) & need_mask)
        def _():
            step(True)

        @pl.when((ki <= qi) & jnp.logical_not(need_mask))
        def _():
            step(False)

        @pl.when(ki == qi)
        def _():
            for hh in range(hp):
                vs = slice(hh * V_HEAD, (hh + 1) * V_HEAD)
                l = l_sc[hh]
                o_ref[:, vs] = (acc_sc[vs, :] / l).T.astype(o_ref.dtype)
                lse_ref[hh * SUBLANES:(hh + 1) * SUBLANES, :] = jnp.broadcast_to(m_sc[hh] + jnp.log(l), (SUBLANES, t))

        if bg:
            @pl.when((g == ng - 1) & (qi == nb - 1) & (ki == nb - 1))
            def _():
                bg.wait(bg_in, bg_out, bg_sems)

    kmin = lambda qi, ki: jnp.minimum(ki, qi)
    return pl.pallas_call(
        body, name=name, grid=(ng, nb, nb),
        in_specs=[pl.BlockSpec((t, hp * QHEAD), lambda g, qi, ki: (qi, g)),
                  pl.BlockSpec((t, hp * QK_NOPE), lambda g, qi, ki: (kmin(qi, ki), g)),
                  pl.BlockSpec((t, LANES), lambda g, qi, ki: (kmin(qi, ki), 0)),
                  pl.BlockSpec((t, hp * V_HEAD), lambda g, qi, ki: (kmin(qi, ki), ng + g))] + (bg.specs if bg else []),
        out_specs=[pl.BlockSpec((t, hp * V_HEAD), lambda g, qi, ki: (qi, g)),
                   pl.BlockSpec((hp * SUBLANES, t), lambda g, qi, ki: (g, qi))] + (bg.specs if bg else []),
        out_shape=[jax.ShapeDtypeStruct((Tp, HEADS * V_HEAD), F32), jax.ShapeDtypeStruct((HEADS * SUBLANES, Tp), F32)]
        + (bg.out_shape if bg else []),
        scratch_shapes=[pltpu.VMEM((hp, 1, t), F32), pltpu.VMEM((hp, 1, t), F32), pltpu.VMEM((hp * V_HEAD, t), F32)]
        + (bg.scratch if bg else []),
        compiler_params=_params(("arbitrary",) * 3 if bg else ("parallel", "parallel", "arbitrary"), 8 * hp * t * t * 4),
    )(q, kv, kr8, kv, *(bg.arrs if bg else []))


def _attn_delta(do, o, name):
    Tp = do.shape[0]
    tm = _pick(Tp, MM_TOKEN_K_TILE)

    def body(do_ref, o_ref, d_ref):
        prod = do_ref[...] * o_ref[...]
        ones = jnp.ones((SUBLANES, V_HEAD), F32)
        d_ref[...] = lax.dot_general(ones, prod, (((1,), (1,)), ((), ())), precision=HIGHEST,
                                     preferred_element_type=F32)

    return pl.pallas_call(
        body, name=name, grid=(HEADS, Tp // tm),
        in_specs=[pl.BlockSpec((tm, V_HEAD), lambda h, i: (i, h)), pl.BlockSpec((tm, V_HEAD), lambda h, i: (i, h))],
        out_specs=pl.BlockSpec((SUBLANES, tm), lambda h, i: (h, i)),
        out_shape=jax.ShapeDtypeStruct((HEADS * SUBLANES, Tp), F32),
        compiler_params=_params(("parallel", "parallel"), 4 * tm * V_HEAD * 4),
    )(do, o)


def _flash_bwd(q, kv, kr8, do, lse, delta, npad, name, bg=None):
    Tp = q.shape[0]
    t = _pick(Tp, ATT_TILE)
    nb = Tp // t
    nbg = bg.n if bg else 0
    nt = (((1,), (1,)), ((), ()))
    tn = (((0,), (0,)), ((), ()))

    def body(*refs):
        ((q_ref, kn_ref, kr_ref, v_ref, do_ref, lse_ref, dl_ref), bg_in, (dq_ref, dkn_ref, dkr_ref, dv_ref), bg_out,
         (dk_sc, dv_sc), bg_sems) = _split_refs(refs, 7, 4, 2, nbg)
        h = pl.program_id(0)
        ki = pl.program_id(1)
        qi = pl.program_id(2)
        if bg:
            @pl.when((h == 0) & (ki == 0) & (qi == 0))
            def _():
                bg.start(bg_in, bg_out, bg_sems)

        @pl.when(qi == 0)
        def _():
            dk_sc[...] = jnp.zeros_like(dk_sc)
            dv_sc[...] = jnp.zeros_like(dv_sc)

        def step(masked):
            qv = q_ref[...]
            k = jnp.concatenate([kn_ref[...], kr_ref[...]], axis=1)
            st = lax.dot_general(k, qv, nt, preferred_element_type=F32)
            if masked:
                key = ki * t + lax.broadcasted_iota(jnp.int32, (t, t), 0)
                qry = qi * t + lax.broadcasted_iota(jnp.int32, (t, t), 1)
                st = jnp.where((key <= qry) & (key >= npad), st, NEG_INF)
            pt = jnp.exp(st - lse_ref[0:1, :])
            dob = do_ref[...].astype(BF16)
            dv_sc[...] += jnp.dot(pt.astype(BF16), dob, preferred_element_type=F32)
            dpt = lax.dot_general(v_ref[...], dob, nt, preferred_element_type=F32)
            dst = (pt * (dpt - dl_ref[0:1, :])).astype(BF16)
            dk_sc[...] += jnp.dot(dst, qv, preferred_element_type=F32)
            dqc = lax.dot_general(dst, k, tn, preferred_element_type=F32)
            rows = pl.ds(pl.multiple_of(qi * t, t), t)

            @pl.when(ki == 0)
            def _():
                dq_ref[rows, :] = dqc

            @pl.when(ki > 0)
            def _():
                dq_ref[rows, :] += dqc

        need_mask = (ki == qi) | (ki == 0)

        @pl.when((qi >= ki) & need_mask)
        def _():
            step(True)

        @pl.when((qi >= ki) & jnp.logical_not(need_mask))
        def _():
            step(False)

        @pl.when(qi == nb - 1)
        def _():
            dkn_ref[...] = dk_sc[:, :QK_NOPE].astype(dkn_ref.dtype)
            dkr_ref[...] = dk_sc[:, QK_NOPE:].astype(dkr_ref.dtype)
            dv_ref[...] = dv_sc[...].astype(dv_ref.dtype)

        if bg:
            @pl.when((h == HEADS - 1) & (ki == nb - 1) & (qi == nb - 1))
            def _():
                bg.wait(bg_in, bg_out, bg_sems)

    qmap = lambda h, ki, qi: (jnp.maximum(qi, ki), h)
    kmap = lambda h, ki, qi: (ki, h)
    est = 2 * Tp * QHEAD * 4 + 8 * t * t * 4
    return pl.pallas_call(
        body, name=name, grid=(HEADS, nb, nb),
        in_specs=[pl.BlockSpec((t, QHEAD), qmap),
                  pl.BlockSpec((t, QK_NOPE), kmap),
                  pl.BlockSpec((t, LANES), kmap),
                  pl.BlockSpec((t, V_HEAD), lambda h, ki, qi: (ki, HEADS + h)),
                  pl.BlockSpec((t, V_HEAD), qmap),
                  pl.BlockSpec((SUBLANES, t), lambda h, ki, qi: (h, jnp.maximum(qi, ki))),
                  pl.BlockSpec((SUBLANES, t), lambda h, ki, qi: (h, jnp.maximum(qi, ki)))] + (bg.specs if bg else []),
        out_specs=[pl.BlockSpec((Tp, QHEAD), lambda h, ki, qi: (0, h)),
                   pl.BlockSpec((t, QK_NOPE), kmap),
                   pl.BlockSpec((t, LANES), kmap),
                   pl.BlockSpec((t, V_HEAD), kmap)] + (bg.specs if bg else []),
        out_shape=[jax.ShapeDtypeStruct((Tp, HEADS * QHEAD), F32),
                   jax.ShapeDtypeStruct((Tp, HEADS * QK_NOPE), BF16),
                   jax.ShapeDtypeStruct((Tp, HEADS * LANES), F32),
                   jax.ShapeDtypeStruct((Tp, HEADS * V_HEAD), BF16)] + (bg.out_shape if bg else []),
        scratch_shapes=[pltpu.VMEM((t, QHEAD), F32), pltpu.VMEM((t, V_HEAD), F32)] + (bg.scratch if bg else []),
        compiler_params=_params(("arbitrary",) * 3 if bg else ("parallel", "arbitrary", "arbitrary"), est),
    )(q, kv, kr8, kv, do, lse, delta, *(bg.arrs if bg else []))


GW = SSD_INNER // SSD_GROUPS
PAIRS_PER_GROUP = GW // LANES
XB = SSD_INNER // GW
NT_DIMS = (((1,), (1,)), ((), ()))
TN_DIMS = (((0,), (0,)), ((), ()))


def _ssd_common(xs_ref, dt_ref, alog_ref, e_ref):
    a_neg = -jnp.exp(alog_ref[...])
    dt = dt_ref[...]
    li = lax.broadcasted_iota(jnp.int32, (CHUNK, CHUNK), 0)
    si = lax.broadcasted_iota(jnp.int32, (CHUNK, CHUNK), 1)
    tril = li >= si
    tri = tril.astype(F32)
    acs = jnp.dot(tri, dt * a_neg, precision=SSD_PREC, preferred_element_type=F32)
    e = e_ref[...]
    dte = jnp.dot(dt, e, precision=SSD_PREC, preferred_element_type=F32)
    acse = jnp.dot(acs, e, precision=SSD_PREC, preferred_element_type=F32)
    x = xs_ref[...] * dte
    alast = acse[CHUNK - 1:CHUNK, :]
    return dict(a_neg=a_neg, dt=dt, tril=tril, tri=tri, acs=acs, acs_t=acs.T, e=e, dte=dte, acse=acse, x=x,
                p_e=jnp.exp(acse), w_e=jnp.exp(alast - acse), dl_e=jnp.exp(alast), li=li, si=si)


def _decay(cm, head):
    col = cm["acs"][:, head:head + 1]
    row = cm["acs_t"][head:head + 1, :]
    return jnp.exp(jnp.where(cm["tril"], col - row, -jnp.inf))


def _ssd_fwd(xbc, dt, alog, e, name):
    Tp = xbc.shape[0]
    nc = Tp // CHUNK

    def body(xs_ref, b_ref, c_ref, dt_ref, alog_ref, e_ref, y_ref, st_ref, st_sc):
        @pl.when(pl.program_id(0) == 0)
        def _():
            st_sc[...] = jnp.zeros_like(st_sc)

        cm = _ssd_common(xs_ref, dt_ref, alog_ref, e_ref)
        st_ref[0] = st_sc[...]
        lane = lax.broadcasted_iota(jnp.int32, (CHUNK, LANES), 1)
        for g in range(SSD_GROUPS):
            gs = slice(g * GW, (g + 1) * GW)
            cg = c_ref[:, g * SSD_STATE:(g + 1) * SSD_STATE].astype(BF16)
            bg = b_ref[:, g * SSD_STATE:(g + 1) * SSD_STATE].astype(BF16)
            cb = lax.dot_general(cg, bg, NT_DIMS, preferred_element_type=F32)
            stg = st_sc[:, gs]
            yoff = jnp.dot(cg, stg.astype(BF16), preferred_element_type=F32) * cm["p_e"][:, gs]
            xg = cm["x"][:, gs]
            for jp in range(PAIRS_PER_GROUP):
                j = g * PAIRS_PER_GROUP + jp
                xp = xg[:, jp * LANES:(jp + 1) * LANES].astype(BF16)
                ys = []
                for head in (2 * j, 2 * j + 1):
                    m = (cb * _decay(cm, head)).astype(BF16)
                    ys.append(jnp.dot(m, xp, preferred_element_type=F32))
                y_ref[:, j * LANES:(j + 1) * LANES] = (jnp.where(lane < SSD_HEAD_DIM, ys[0], ys[1])
                                                       + yoff[:, jp * LANES:(jp + 1) * LANES])
            snew = lax.dot_general(bg, (cm["w_e"][:, gs] * xg).astype(BF16), TN_DIMS, preferred_element_type=F32)
            st_sc[:, gs] = cm["dl_e"][:, gs] * stg + snew

    return pl.pallas_call(
        body, name=name, grid=(nc,),
        in_specs=[pl.BlockSpec((CHUNK, SSD_INNER), lambda c: (c, 0)),
                  pl.BlockSpec((CHUNK, GW), lambda c: (c, XB)),
                  pl.BlockSpec((CHUNK, GW), lambda c: (c, XB + 1)),
                  pl.BlockSpec((CHUNK, LANES), lambda c: (c, 0)),
                  pl.BlockSpec((1, LANES), lambda c: (0, 0)),
                  pl.BlockSpec((LANES, SSD_INNER), lambda c: (0, 0))],
        out_specs=[pl.BlockSpec((CHUNK, SSD_INNER), lambda c: (c, 0)),
                   pl.BlockSpec((1, SSD_STATE, SSD_INNER), lambda c: (c, 0, 0))],
        out_shape=[jax.ShapeDtypeStruct((Tp, SSD_INNER), F32), jax.ShapeDtypeStruct((nc, SSD_STATE, SSD_INNER), F32)],
        scratch_shapes=[pltpu.VMEM((SSD_STATE, SSD_INNER), F32)],
        compiler_params=_params(("arbitrary",), 24 * CHUNK * SSD_INNER * 4),
    )(xbc, xbc, xbc, dt, alog, e)


def _ssd_bwd(xbc, dt, alog, e, dy, dxs_skip, states, name):
    Tp = xbc.shape[0]
    nc = Tp // CHUNK
    rev = lambda c: nc - 1 - c

    def body(xs_ref, b_ref, c_ref, dt_ref, alog_ref, e_ref, dy_ref, skip_ref, st_ref,
             dxbc_ref, ddt_ref, dalog_ref, dst_sc, dx_sc, t_sc, tw_sc):
        @pl.when(pl.program_id(0) == 0)
        def _():
            dst_sc[...] = jnp.zeros_like(dst_sc)
            dalog_ref[...] = jnp.zeros_like(dalog_ref)

        cm = _ssd_common(xs_ref, dt_ref, alog_ref, e_ref)
        lane = lax.broadcasted_iota(jnp.int32, (CHUNK, LANES), 1)
        dacs_col = jnp.zeros((CHUNK, LANES), F32)
        dacs_row = jnp.zeros((LANES, CHUNK), F32)
        t_last = []
        for g in range(SSD_GROUPS):
            gs = slice(g * GW, (g + 1) * GW)
            cg = c_ref[:, g * SSD_STATE:(g + 1) * SSD_STATE].astype(BF16)
            bg = b_ref[:, g * SSD_STATE:(g + 1) * SSD_STATE].astype(BF16)
            stg = st_ref[0, :, gs]
            stg_b = stg.astype(BF16)
            dstg = dst_sc[:, gs]
            dstg_b = dstg.astype(BF16)
            xg = cm["x"][:, gs]
            dyg = dy_ref[:, gs]
            zg = jnp.dot(cg, stg_b, preferred_element_type=F32)
            dzg = dyg * cm["p_e"][:, gs]
            dzg_b = dzg.astype(BF16)
            dcg = lax.dot_general(dzg_b, stg_b, NT_DIMS, preferred_element_type=F32)
            dst_in = lax.dot_general(cg, dzg_b, TN_DIMS, preferred_element_type=F32)
            dst_in = dst_in + cm["dl_e"][:, gs] * dstg
            t_last.append(jnp.sum(dstg * stg * cm["dl_e"][:, gs], axis=0, keepdims=True))
            weg = cm["w_e"][:, gs]
            dbg = lax.dot_general((weg * xg).astype(BF16), dstg_b, NT_DIMS, preferred_element_type=F32)
            gg = jnp.dot(bg, dstg_b, preferred_element_type=F32)
            dxg = weg * gg
            tw_sc[:, gs] = xg * dxg
            t_sc[:, gs] = dzg * zg - xg * dxg
            cb = lax.dot_general(cg, bg, NT_DIMS, preferred_element_type=F32)
            dcb = jnp.zeros((CHUNK, CHUNK), F32)
            for jp in range(PAIRS_PER_GROUP):
                j = g * PAIRS_PER_GROUP + jp
                ps = slice(jp * LANES, (jp + 1) * LANES)
                xp = xg[:, ps].astype(BF16)
                dyp = dyg[:, ps]
                dxp = dxg[:, ps]
                for half, head in enumerate((2 * j, 2 * j + 1)):
                    lam = _decay(cm, head)
                    m32 = cb * lam
                    sel = (lane < SSD_HEAD_DIM) if half == 0 else (lane >= SSD_HEAD_DIM)
                    dye = jnp.where(sel, dyp, 0.0).astype(BF16)
                    dm = lax.dot_general(dye, xp, NT_DIMS, preferred_element_type=F32)
                    w = dm * m32
                    dacs_col = dacs_col + jnp.where(cm["si"] == head, jnp.sum(w, axis=1, keepdims=True), 0.0)
                    dacs_row = dacs_row + jnp.where(cm["li"] == head, jnp.sum(w, axis=0, keepdims=True), 0.0)
                    dcb = dcb + dm * lam
                    dxp = dxp + lax.dot_general(m32.astype(BF16), dye, TN_DIMS, preferred_element_type=F32)
                dx_sc[:, j * LANES:(j + 1) * LANES] = dxp
            dcb_b = dcb.astype(BF16)
            dcg = dcg + jnp.dot(dcb_b, bg, preferred_element_type=F32)
            dbg = dbg + lax.dot_general(dcb_b, cg, TN_DIMS, preferred_element_type=F32)
            dst_sc[:, gs] = dst_in
            dxbc_ref[:, SSD_INNER + g * SSD_STATE:SSD_INNER + (g + 1) * SSD_STATE] = dbg
            dxbc_ref[:, SSD_INNER + GW + g * SSD_STATE:SSD_INNER + GW + (g + 1) * SSD_STATE] = dcg
        e = cm["e"]
        dacs = lax.dot_general(t_sc[...], e, NT_DIMS, precision=SSD_PREC, preferred_element_type=F32)
        dacs = dacs + dacs_col - dacs_row.T
        last_lane = jnp.concatenate(t_last, axis=1) + jnp.sum(tw_sc[...], axis=0, keepdims=True)
        last_head = lax.dot_general(jnp.broadcast_to(last_lane, (SUBLANES, SSD_INNER)), e, NT_DIMS,
                                    precision=SSD_PREC, preferred_element_type=F32)[0:1, :]
        dacs = dacs + jnp.where(cm["li"] == CHUNK - 1, last_head, 0.0)
        da = lax.dot_general(cm["tri"], dacs, TN_DIMS, precision=SSD_PREC, preferred_element_type=F32)
        dx_all = dx_sc[...]
        ddt = da * cm["a_neg"] + lax.dot_general(dx_all * xs_ref[...], e, NT_DIMS, precision=SSD_PREC,
                                                 preferred_element_type=F32)
        ddt_ref[...] = ddt
        dxbc_ref[:, :SSD_INNER] = dx_all * cm["dte"] + skip_ref[...]
        dalog_ref[0:1, :] += jnp.sum(da * cm["dt"], axis=0, keepdims=True) * cm["a_neg"]

    return pl.pallas_call(
        body, name=name, grid=(nc,),
        in_specs=[pl.BlockSpec((CHUNK, SSD_INNER), lambda c: (rev(c), 0)),
                  pl.BlockSpec((CHUNK, GW), lambda c: (rev(c), XB)),
                  pl.BlockSpec((CHUNK, GW), lambda c: (rev(c), XB + 1)),
                  pl.BlockSpec((CHUNK, LANES), lambda c: (rev(c), 0)),
                  pl.BlockSpec((1, LANES), lambda c: (0, 0)),
                  pl.BlockSpec((LANES, SSD_INNER), lambda c: (0, 0)),
                  pl.BlockSpec((CHUNK, SSD_INNER), lambda c: (rev(c), 0)),
                  pl.BlockSpec((CHUNK, SSD_INNER), lambda c: (rev(c), 0)),
                  pl.BlockSpec((1, SSD_STATE, SSD_INNER), lambda c: (rev(c), 0, 0))],
        out_specs=[pl.BlockSpec((CHUNK, SSD_CONV_DIM), lambda c: (rev(c), 0)),
                   pl.BlockSpec((CHUNK, LANES), lambda c: (rev(c), 0)),
                   pl.BlockSpec((SUBLANES, LANES), lambda c: (0, 0))],
        out_shape=[jax.ShapeDtypeStruct((Tp, SSD_CONV_DIM), F32), jax.ShapeDtypeStruct((Tp, LANES), F32),
                   jax.ShapeDtypeStruct((SUBLANES, LANES), F32)],
        scratch_shapes=[pltpu.VMEM((SSD_STATE, SSD_INNER), F32), pltpu.VMEM((CHUNK, SSD_INNER), F32),
                        pltpu.VMEM((CHUNK, SSD_INNER), F32), pltpu.VMEM((CHUNK, SSD_INNER), F32)],
        compiler_params=_params(("arbitrary",), 32 * CHUNK * SSD_INNER * 4),
    )(xbc, xbc, xbc, dt, alog, e, dy, dxs_skip, states)


def _loss_head(h, target, name):
    Tp, d = h.shape
    nt = Tp // LANES

    def body(h_ref, t_ref, dh_ref, l_ref):
        real = pl.program_id(0) > 0
        err = jnp.where(real, h_ref[...] - t_ref[...], 0.0)
        dh_ref[...] = err * (1.0 / d)
        l_ref[...] = jnp.broadcast_to(0.5 * jnp.sum(err * err) * (1.0 / d), l_ref.shape)

    return pl.pallas_call(
        body, name=name, grid=(nt,),
        in_specs=[pl.BlockSpec((LANES, d), lambda i: (i, 0)),
                  pl.BlockSpec((LANES, d), lambda i: (jnp.maximum(i - 1, 0), 0))],
        out_specs=[pl.BlockSpec((LANES, d), lambda i: (i, 0)),
                   pl.BlockSpec((1, SUBLANES, LANES), lambda i: (i, 0, 0))],
        out_shape=[jax.ShapeDtypeStruct((Tp, d), F32), jax.ShapeDtypeStruct((nt, SUBLANES, LANES), F32)],
        compiler_params=_params(("parallel",), 8 * LANES * d * 4),
    )(h, target)


def _adamw(parts, w, m, v, name):
    shape = w.shape
    C = shape[-1]
    R = int(np.prod(shape[:-1]))
    npart = parts.shape[0]
    parts, w, m, v = parts.reshape(npart, R, C), w.reshape(R, C), m.reshape(R, C), v.reshape(R, C)
    lanes = -(-C // LANES) * LANES
    tr = _pick(R, max(BF16_ROWS, ADAM_ELEMS // lanes), BF16_ROWS) if R % BF16_ROWS == 0 else R
    c1 = 1.0 / (1.0 - ADAM_B1 ** ADAM_STEP)
    c2 = 1.0 / (1.0 - ADAM_B2 ** ADAM_STEP)

    def body(p_ref, w_ref, m_ref, v_ref, g_out, d_out, m_out, v_out):
        g = p_ref[0].astype(F32)
        for p in range(1, npart):
            g = g + p_ref[p].astype(F32)
        m_new = ADAM_B1 * m_ref[...] + (1.0 - ADAM_B1) * g
        v_new = ADAM_B2 * v_ref[...] + (1.0 - ADAM_B2) * (g * g)
        g_out[...] = g
        m_out[...] = m_new
        v_out[...] = v_new
        d_out[...] = -ADAM_LR * ((m_new * c1) / (jnp.sqrt(v_new * c2) + ADAM_EPS) + ADAM_WD * w_ref[...])

    spec = pl.BlockSpec((tr, C), lambda i: (i, 0))
    est = npart * tr * lanes * parts.dtype.itemsize + 7 * tr * lanes * 4
    res = pl.pallas_call(
        body, name=name, grid=(R // tr,),
        in_specs=[pl.BlockSpec((npart, tr, C), lambda i: (0, i, 0)), spec, spec, spec],
        out_specs=[spec] * 4, out_shape=[jax.ShapeDtypeStruct((R, C), F32)] * 4,
        compiler_params=_params(("parallel",), est),
    )(parts, w, m, v)
    return [r.reshape(shape) for r in res]


MESH_ID = pl.DeviceIdType.MESH
N_PEERS = N_DEV - 1


def _dev_index(p):
    return 4 * p[0] + 2 * p[1] + p[2]


class _Background:
    def __init__(self, kind, arrs):
        self.kind, self.arrs, self.n = kind, list(arrs), len(arrs)
        self.npairs = N_PEERS if kind == "gather" else N_CHIPS - 1
        lead = (N_DEV,) if kind == "gather" else ()
        self.out_shape = [jax.ShapeDtypeStruct(lead + a.shape, a.dtype) for a in self.arrs]
        self.specs = [pl.BlockSpec(memory_space=pl.ANY)] * self.n
        self.scratch = [pltpu.SemaphoreType.DMA((self.n, self.npairs)), pltpu.SemaphoreType.DMA((self.n, self.npairs)),
                        pltpu.SemaphoreType.DMA((self.n,))]

    def copies(self, in_refs, out_refs, sems):
        send_sems, recv_sems, local_sems = sems
        x, y, c = lax.axis_index("x"), lax.axis_index("y"), lax.axis_index("c")
        sends, recvs, locals_ = [], [], []

        def remote(t, k, src, dst, to):
            return pltpu.make_async_remote_copy(src_ref=src, dst_ref=dst, send_sem=send_sems.at[t, k],
                                                recv_sem=recv_sems.at[t, k], device_id=to, device_id_type=MESH_ID)

        if self.kind == "gather":
            me = _dev_index((x, y, c))
            peers = [(x, y, 1 - c), (1 - x, y, c), (x, 1 - y, c), (1 - x, 1 - y, c),
                     (1 - x, y, 1 - c), (x, 1 - y, 1 - c), (1 - x, 1 - y, 1 - c)]
            for t in range(self.n):
                locals_.append(pltpu.make_async_copy(in_refs[t], out_refs[t].at[me], local_sems.at[t]))
                for k, p in enumerate(peers):
                    sends.append(remote(t, k, in_refs[t], out_refs[t].at[me], p))
                    recvs.append(remote(t, k, in_refs[t], out_refs[t].at[_dev_index(p)], p))
        else:
            mine = 2 * x + y
            peers = [(1 - x, y), (x, 1 - y), (1 - x, 1 - y)]
            for t in range(self.n):
                locals_.append(pltpu.make_async_copy(in_refs[t].at[mine], out_refs[t].at[mine], local_sems.at[t]))
                for k, p in enumerate(peers):
                    theirs = 2 * p[0] + p[1]
                    sends.append(remote(t, k, in_refs[t].at[theirs], out_refs[t].at[mine], (*p, c)))
                    recvs.append(remote(t, k, in_refs[t].at[mine], out_refs[t].at[theirs], (*p, c)))
        return sends, recvs, locals_

    def start(self, in_refs, out_refs, sems):
        sends, _, locals_ = self.copies(in_refs, out_refs, sems)
        for cp in locals_ + sends:
            cp.start()

    def wait(self, in_refs, out_refs, sems):
        sends, recvs, locals_ = self.copies(in_refs, out_refs, sems)
        for cp in recvs:
            cp.wait_recv()
        for cp in sends:
            cp.wait_send()
        for cp in locals_:
            cp.wait()


def _comm_call(body, name, arrs, out_shape, npairs):
    n = len(arrs)
    any_spec = pl.BlockSpec(memory_space=pl.ANY)
    return pl.pallas_call(
        functools.partial(body, n), name=name, in_specs=[any_spec] * n, out_specs=[any_spec] * n, out_shape=out_shape,
        scratch_shapes=[pltpu.SemaphoreType.DMA((n, npairs)), pltpu.SemaphoreType.DMA((n, npairs)),
                        pltpu.SemaphoreType.DMA((n,))],
    )(*arrs)


def _allgather(arrs, name):
    def body(n, *refs):
        src_refs, out_refs = refs[:n], refs[n:2 * n]
        send_sems, recv_sems, local_sems = refs[2 * n:]
        x, y, c = lax.axis_index("x"), lax.axis_index("y"), lax.axis_index("c")
        me, sibling = (x, y, c), (x, y, 1 - c)
        chips = [(1 - x, y), (x, 1 - y), (1 - x, 1 - y)]

        def copy(t, k, block, to, src=None):
            slot = out_refs[t].at[_dev_index(block)]
            return pltpu.make_async_remote_copy(
                src_ref=slot if src is None else src, dst_ref=slot,
                send_sem=send_sems.at[t, k], recv_sem=recv_sems.at[t, k],
                device_id=to, device_id_type=MESH_ID)

        sends, locals_ = [], []
        for t in range(n):
            mine = pltpu.make_async_copy(src_refs[t], out_refs[t].at[_dev_index(me)], local_sems.at[t])
            mine.start()
            locals_.append(mine)
            first = [copy(t, 0, me, sibling, src=src_refs[t])]
            first += [copy(t, 1 + j, me, (*chip, c), src=src_refs[t]) for j, chip in enumerate(chips)]
            for cp in first:
                cp.start()
            sends += first
        for j, chip in enumerate(chips):
            for t in range(n):
                copy(t, 1 + j, (*chip, c), me).wait_recv()
                passed = copy(t, 4 + j, (*chip, c), sibling)
                passed.start()
                sends.append(passed)
        for t in range(n):
            copy(t, 0, sibling, me).wait_recv()
            for j, chip in enumerate(chips):
                copy(t, 4 + j, (*chip, 1 - c), me).wait_recv()
        for cp in sends:
            cp.wait_send()
        for cp in locals_:
            cp.wait()

    return _comm_call(body, name, arrs, [jax.ShapeDtypeStruct((N_DEV,) + a.shape, a.dtype) for a in arrs], N_PEERS)


N_CHIPS = N_DEV // 2
CHIPS = [(0, 0), (0, 1), (1, 0), (1, 1)]


def _sibling_exchange(arrs, name):
    def body(n, *refs):
        in_refs, out_refs = refs[:n], refs[n:2 * n]
        send_sems, recv_sems, _ = refs[2 * n:]
        x, y, c = lax.axis_index("x"), lax.axis_index("y"), lax.axis_index("c")
        sibling = (x, y, 1 - c)

        def copy(t, j):
            return pltpu.make_async_remote_copy(
                src_ref=in_refs[t].at[_dev_index((*CHIPS[j], 1 - c))], dst_ref=out_refs[t].at[j],
                send_sem=send_sems.at[t, j], recv_sem=recv_sems.at[t, j],
                device_id=sibling, device_id_type=MESH_ID)

        copies = [copy(t, j) for t in range(n) for j in range(N_CHIPS)]
        for cp in copies:
            cp.start()
        for cp in copies:
            cp.wait_recv()
        for cp in copies:
            cp.wait_send()

    return _comm_call(body, name, arrs, [jax.ShapeDtypeStruct((N_CHIPS,) + a.shape[1:], a.dtype) for a in arrs], N_CHIPS)


def _chip_exchange(arrs, name):
    def body(n, *refs):
        in_refs, out_refs = refs[:n], refs[n:2 * n]
        send_sems, recv_sems, local_sems = refs[2 * n:]
        x, y, c = lax.axis_index("x"), lax.axis_index("y"), lax.axis_index("c")
        mine = 2 * x + y
        peers = [(1 - x, y), (x, 1 - y), (1 - x, 1 - y)]

        def copy(t, k, src_chip, dst_chip, to):
            return pltpu.make_async_remote_copy(
                src_ref=in_refs[t].at[src_chip], dst_ref=out_refs[t].at[dst_chip],
                send_sem=send_sems.at[t, k], recv_sem=recv_sems.at[t, k],
                device_id=(*to, c), device_id_type=MESH_ID)

        sends, locals_ = [], []
        for t in range(n):
            own = pltpu.make_async_copy(in_refs[t].at[mine], out_refs[t].at[mine], local_sems.at[t])
            own.start()
            locals_.append(own)
            for k, p in enumerate(peers):
                cp = copy(t, k, 2 * p[0] + p[1], mine, p)
                cp.start()
                sends.append(cp)
        for t in range(n):
            for k, p in enumerate(peers):
                copy(t, k, mine, 2 * p[0] + p[1], p).wait_recv()
        for cp in sends:
            cp.wait_send()
        for cp in locals_:
            cp.wait()

    return _comm_call(body, name, arrs, [jax.ShapeDtypeStruct(a.shape, a.dtype) for a in arrs], N_CHIPS - 1)


def _add_pairs(a, b, name):
    shape = a.shape
    C = shape[-1]
    R = int(np.prod(shape[:-1]))
    lanes = -(-C // LANES) * LANES
    tr = _pick(R, max(BF16_ROWS, 2 * ADAM_ELEMS // lanes), BF16_ROWS) if R % BF16_ROWS == 0 else R

    def body(a_ref, b_ref, o_ref):
        o_ref[...] = (a_ref[...].astype(F32) + b_ref[...].astype(F32)).astype(o_ref.dtype)

    spec = pl.BlockSpec((tr, C), lambda i: (i, 0))
    return pl.pallas_call(
        body, name=name, grid=(R // tr,), in_specs=[spec, spec], out_specs=spec,
        out_shape=jax.ShapeDtypeStruct((R, C), a.dtype),
        compiler_params=_params(("parallel",), 3 * tr * lanes * 4),
    )(a.reshape(R, C), b.reshape(R, C)).reshape(shape)


WEIGHTS = ['meta_tokens', 'emb_ln_g', 'emb_ln_b', 'w_in', 'q_norm_g', 'w_q_b', 'kv_norm_g', 'w_kv_b', 'w_o_attn',
           'ssd_conv_w', 'ssd_conv_b', 'dt_bias', 'a_log', 'd_skip', 'ssd_norm_g', 'w_o_ssd', 'w_out', 'ln1_g',
           'ln1_b', 'w_up', 'ffn_conv_w', 'ffn_conv_b', 'w_down', 'ln2_g', 'ln2_b']
BIG = {'w_in': 2, 'w_q_b': 2, 'w_kv_b': 2, 'w_o_attn': 1, 'w_o_ssd': 1, 'w_out': 1, 'w_up': 2, 'w_down': 1}
SMALL_SHARDED = {'meta_tokens': 1, 'ssd_conv_w': 2, 'ffn_conv_w': 2}
REPLICATED = [n for n in WEIGHTS if n not in BIG and n not in SMALL_SHARDED]
FIRST_USED = ['w_in', 'w_q_b', 'w_kv_b']
AFTER_ATTENTION = [n for n in BIG if n not in FIRST_USED]
SMALL_COLS = LANES


def _flatten(arrs, cols, row_mult, lead=False):
    parts, offs, off = [], [], 0
    for a in arrs:
        a2 = a.reshape(N_DEV, -1) if lead else a.reshape(1, -1)
        n = a2.shape[1]
        pad = -n % cols
        parts.append(jnp.pad(a2, ((0, 0), (0, pad))))
        offs.append((off, n))
        off += n + pad
    rows = off // cols
    extra = (-rows % row_mult) * cols
    if extra:
        parts.append(jnp.zeros((parts[0].shape[0], extra), parts[0].dtype))
    flat = jnp.concatenate(parts, axis=1)
    flat = flat.reshape(flat.shape[0], -1, cols)
    return (flat if lead else flat[0]), offs


def _unflatten(flat, offs, shapes):
    f = flat.reshape(-1)
    return [f[o:o + n].reshape(s) for (o, n), s in zip(offs, shapes)]


def _to_pieces(g, axis):
    s = g.shape[axis] // N_DEV
    g = g.reshape(g.shape[:axis] + (N_DEV, s) + g.shape[axis + 1:])
    return jnp.moveaxis(g, axis, 0).reshape(N_DEV, -1)


def _from_pieces(p, shard_shape, axis):
    g = jnp.moveaxis(p.reshape((N_DEV,) + tuple(shard_shape)), 0, axis)
    sh = list(shard_shape)
    sh[axis] *= N_DEV
    return g.reshape(sh)


def _in_proj_pad(w):
    e = np.cumsum((0,) + IN_SIZES)
    ql, kvl, kpe, z, xbc, dt, ga, gs = [w[:, e[j]:e[j + 1]] for j in range(8)]
    zc = lambda n: jnp.zeros((w.shape[0], n), w.dtype)
    return jnp.concatenate([ql, kvl, z, xbc, ga, gs, kpe, zc(LANES - QK_ROPE), dt, zc(LANES - SSD_HEADS)], axis=1)


def _in_proj_unpad(d):
    seg = lambda o, n: d[:, o:o + n]
    return jnp.concatenate([seg(OQ, Q_LORA), seg(OKV, KV_LORA), seg(OKPE, QK_ROPE), seg(OZ, SSD_INNER),
                            seg(OXBC, SSD_CONV_DIM), seg(ODT, SSD_HEADS), seg(OGA, D_MODEL), seg(OGS, D_MODEL)], axis=1)


def _q_pad(w):
    w3 = w.reshape(Q_LORA, HEADS, QK_NOPE + QK_ROPE)
    return jnp.concatenate([w3, jnp.zeros((Q_LORA, HEADS, QHEAD - QK_NOPE - QK_ROPE), w.dtype)], axis=2).reshape(Q_LORA, HEADS * QHEAD)


def _q_unpad(d):
    return d.reshape(Q_LORA, HEADS, QHEAD)[:, :, :QK_NOPE + QK_ROPE].reshape(Q_LORA, HEADS * (QK_NOPE + QK_ROPE))


def _kv_perm(w):
    w3 = w.reshape(KV_LORA, HEADS, QK_NOPE + V_HEAD)
    return jnp.concatenate([w3[:, :, :QK_NOPE].reshape(KV_LORA, -1), w3[:, :, QK_NOPE:].reshape(KV_LORA, -1)], axis=1)


def _kv_unperm(d):
    kn = d[:, :HEADS * QK_NOPE].reshape(KV_LORA, HEADS, QK_NOPE)
    v = d[:, HEADS * QK_NOPE:].reshape(KV_LORA, HEADS, V_HEAD)
    return jnp.concatenate([kn, v], axis=2).reshape(KV_LORA, HEADS * (QK_NOPE + V_HEAD))


def _row_vec(v, width=None):
    v = v.reshape(1, -1).astype(F32)
    if width is not None and v.shape[1] < width:
        v = jnp.pad(v, ((0, 0), (0, width - v.shape[1])))
    return v


def _pad_rows8(w):
    return jnp.pad(w.astype(F32), ((0, SUBLANES - w.shape[0]), (0, 0)))


def _tables(Tp, npad):
    pos = jnp.maximum(jnp.arange(Tp, dtype=jnp.int32) - npad, 0).astype(F32)
    inv_freq = 1.0 / (ROPE_THETA ** (jnp.arange(0, QK_ROPE, 2, dtype=F32) / QK_ROPE))
    ang = pos[:, None] * inv_freq[None, :]
    ang = jnp.concatenate([ang, ang], axis=-1)
    zeros = jnp.zeros((Tp, LANES - QK_ROPE), F32)
    cos = jnp.concatenate([jnp.cos(ang), zeros], axis=1)
    sin = jnp.concatenate([jnp.sin(ang), zeros], axis=1)
    rot = np.zeros((LANES, LANES), np.float32)
    half = QK_ROPE // 2
    for i in range(half):
        rot[i + half, i] = -1.0
        rot[i, i + half] = 1.0
    expand = np.zeros((LANES, SSD_INNER), np.float32)
    for hd in range(SSD_HEADS):
        expand[hd, hd * SSD_HEAD_DIM:(hd + 1) * SSD_HEAD_DIM] = 1.0
    return cos, sin, jnp.asarray(rot), jnp.asarray(expand)


def _layer_rows(proj, tb):
    rows_a = [_row(proj, Q_LORA, OQ // Q_LORA), _row(proj, KV_LORA, OKV // KV_LORA), _row(proj, LANES, OKPE // LANES),
              _row(proj, LANES, ODT // LANES), _row(tb["cos"], diff=False), _row(tb["sin"], diff=False)]
    return rows_a


def _layer_fwd(h, h_bf, P, tb, fns, npad, bg=None, on_carried=None):
    both = [_out(D_MODEL, F32), _out(D_MODEL, BF16)]
    res_ln_twice = lambda *a: fns["res_ln"](*a) * 2
    proj = _mm(h_bf, P["w_in"], F32, "in_proj")
    rows_a = _layer_rows(proj, tb)
    consts_a = [_row(tb["rot"], diff=False), _row(P["q_norm_g"]), _row(P["kv_norm_g"]), _row(P["dt_bias"])]
    qn, kvn, kr8, dt = _rw_fwd(fns["in_post"], rows_a, consts_a,
                               [_out(Q_LORA, BF16), _out(KV_LORA, BF16), _out(HEADS * LANES, BF16), _out(LANES, F32)],
                               "in_post")
    q = _mm(qn, P["w_q"], F32, "q_proj")
    rows_q = [_row(q, QHEAD, 0, grp=True), _row(tb["cos"], diff=False), _row(tb["sin"], diff=False)]
    qr = _rw_fwd(fns["q_post"], rows_q, [_row(tb["rot"], diff=False)], [_out(HEADS * QHEAD, BF16, QHEAD, grp=True)],
                 "q_post", ng=HEADS)[0]
    kv = _mm(kvn, P["w_kv"], BF16, "kv_proj")
    o, lse, *carried = _flash_fwd(qr, kv, kr8, npad, "attn_fwd_gather" if bg else "attn_fwd", bg=bg)
    if on_carried is not None:
        carried = on_carried(P, carried)
    ya = _mm(o, P["w_o_attn"], F32, "attn_out")
    xbc = _conv_fwd(proj, OXBC, SSD_CONV_DIM, P["ssd_conv_w"], P["ssd_conv_b"], SSD_CONV, True, npad, "ssd_conv")
    y, states = _ssd_fwd(xbc, dt, P["a_log"], tb["expand"], "ssd_fwd")
    rows_b = [_row(y, GW, 0, grp=True), _row(xbc, GW, 0, grp=True), _row(proj, GW, OZ // GW, grp=True)]
    consts_b = [_row(P["d_skip"], GW, 0, grp=True), _row(P["ssd_norm_g"], GW, 0, grp=True)]
    yn = _rw_fwd(fns["gated"], rows_b, consts_b, [_out(SSD_INNER, BF16, GW, grp=True)], "ssd_gate", ng=SSD_GROUPS)[0]
    ys = _mm(yn, P["w_o_ssd"], F32, "ssd_out")
    rows_c = [_row(proj, D_MODEL, OGA // D_MODEL), _row(proj, D_MODEL, OGS // D_MODEL), _row(ya), _row(ys)]
    mixed = _rw_fwd(fns["mix"], rows_c, [], [_out(D_MODEL, BF16)], "mix")[0]
    mo = _mm(mixed, P["w_out"], F32, "mix_out")
    consts_1 = [_row(P["ln1_g"]), _row(P["ln1_b"])]
    h1, h1_bf = _rw_fwd(res_ln_twice, [_row(h), _row(mo)], consts_1, both, "ln1")
    up = _mm(h1_bf, P["w_up"], BF16, "ffn_up")
    u = _conv_fwd(up, 0, 2 * D_FF, P["ffn_conv_w"], P["ffn_conv_b"], FFN_CONV, False, npad, "ffn_conv", BF16)
    act = _rw_fwd(fns["glu"], [_row(u)], [], [_out(D_FF, BF16)], "ffn_glu")[0]
    fo = _mm(act, P["w_down"], F32, "ffn_down")
    consts_2 = [_row(P["ln2_g"]), _row(P["ln2_b"])]
    h2, h2_bf = _rw_fwd(res_ln_twice, [_row(h1), _row(fo)], consts_2, both, "ln2")
    res = dict(h=h, h_bf=h_bf, proj=proj, qn=qn, kvn=kvn, kr8=kr8, dt=dt, q=q, qr=qr, kv=kv, o=o, lse=lse, ya=ya,
               xbc=xbc, y=y, states=states, yn=yn, ys=ys, mixed=mixed, mo=mo, h1=h1, h1_bf=h1_bf, up=up, u=u, act=act,
               fo=fo)
    return h2, h2_bf, res, carried


def _layer_bwd(dh2, r, P, tb, fns, npad, bg=None, before_attn=None):
    g = {}
    consts_2 = [_row(P["ln2_g"]), _row(P["ln2_b"])]
    (dh1_a, dfo), (g["ln2_g"], g["ln2_b"]) = _rw_bwd(fns["res_ln"], [_row(r["h1"]), _row(r["fo"])], consts_2,
                                                     [_row(dh2)], [F32, BF16], "ln2_bwd")
    g["w_down"] = _mm(r["act"], dfo, BF16, "dw_down", ta=True)
    dact = _mm(dfo, P["w_down"], BF16, "d_act", tb=True)
    (du,), _ = _rw_bwd(fns["glu"], [_row(r["u"])], [], [_row(dact)], [BF16], "glu_bwd")
    dup, g["ffn_conv_w"], g["ffn_conv_b"] = _conv_bwd(r["up"], 0, 2 * D_FF, P["ffn_conv_w"], P["ffn_conv_b"], du,
                                                      FFN_CONV, False, npad, "ffn_conv_bwd")
    g["w_up"] = _mm(r["h1_bf"], dup, BF16, "dw_up", ta=True)
    dh1 = _mm(dup, P["w_up"], F32, "d_h1", tb=True, add=dh1_a)
    consts_1 = [_row(P["ln1_g"]), _row(P["ln1_b"])]
    (dh_a, dmo), (g["ln1_g"], g["ln1_b"]) = _rw_bwd(fns["res_ln"], [_row(r["h"]), _row(r["mo"])], consts_1,
                                                    [_row(dh1)], [F32, BF16], "ln1_bwd")
    g["w_out"] = _mm(r["mixed"], dmo, BF16, "dw_out", ta=True)
    dmixed = _mm(dmo, P["w_out"], F32, "d_mixed", tb=True)
    proj = r["proj"]
    rows_c = [_row(proj, D_MODEL, OGA // D_MODEL), _row(proj, D_MODEL, OGS // D_MODEL), _row(r["ya"]), _row(r["ys"])]
    (dga, dgs, dya, dys), _ = _rw_bwd(fns["mix"], rows_c, [], [_row(dmixed)], [BF16] * 4, "mix_bwd")
    g["w_o_attn"] = _mm(r["o"], dya, BF16, "dw_o_attn", ta=True)
    do = _mm(dya, P["w_o_attn"], F32, "d_o", tb=True)
    g["w_o_ssd"] = _mm(r["yn"], dys, BF16, "dw_o_ssd", ta=True)
    dyn = _mm(dys, P["w_o_ssd"], F32, "d_yn", tb=True)
    rows_b = [_row(r["y"], GW, 0, grp=True), _row(r["xbc"], GW, 0, grp=True), _row(proj, GW, OZ // GW, grp=True)]
    consts_b = [_row(P["d_skip"], GW, 0, grp=True), _row(P["ssd_norm_g"], GW, 0, grp=True)]
    (dy, dxs_skip, dz), (g["d_skip"], g["ssd_norm_g"]) = _rw_bwd(
        fns["gated"], rows_b, consts_b, [_row(dyn, GW, 0, grp=True)], [F32, F32, BF16], "ssd_gate_bwd", ng=SSD_GROUPS)
    dxbc, ddt, g["a_log"] = _ssd_bwd(r["xbc"], r["dt"], P["a_log"], tb["expand"], dy, dxs_skip, r["states"], "ssd_bwd")
    dxbc_pre, g["ssd_conv_w"], g["ssd_conv_b"] = _conv_bwd(proj, OXBC, SSD_CONV_DIM, P["ssd_conv_w"], P["ssd_conv_b"],
                                                           dxbc, SSD_CONV, True, npad, "ssd_conv_bwd")
    delta = _attn_delta(do, r["o"], "attn_delta")
    if before_attn is not None:
        bg = before_attn(g)
    dqr, dkn, dkr8, dv, *carried = _flash_bwd(r["qr"], r["kv"], r["kr8"], do, r["lse"], delta, npad,
                                              "attn_bwd_exchange" if bg else "attn_bwd", bg=bg)
    rows_q = [_row(r["q"], QHEAD, 0, grp=True), _row(tb["cos"], diff=False), _row(tb["sin"], diff=False)]
    (dq,), _ = _rw_bwd(fns["q_post"], rows_q, [_row(tb["rot"], diff=False)], [_row(dqr, QHEAD, 0, grp=True)], [BF16],
                       "q_post_bwd", ng=HEADS)
    g["w_q"] = _mm(r["qn"], dq, BF16, "dw_q", ta=True)
    dqn = _mm(dq, P["w_q"], F32, "d_qn", tb=True)
    dkv = jnp.concatenate([dkn, dv], axis=1)
    g["w_kv"] = _mm(r["kvn"], dkv, BF16, "dw_kv", ta=True)
    dkvn = _mm(dkv, P["w_kv"], F32, "d_kvn", tb=True)
    rows_a = _layer_rows(proj, tb)
    consts_a = [_row(tb["rot"], diff=False), _row(P["q_norm_g"]), _row(P["kv_norm_g"]), _row(P["dt_bias"])]
    (dql, dkvl, dkpe, ddtr), (g["q_norm_g"], g["kv_norm_g"], g["dt_bias"]) = _rw_bwd(
        fns["in_post"], rows_a, consts_a, [_row(dqn), _row(dkvn), _row(dkr8), _row(ddt)], [BF16] * 4, "in_post_bwd")
    dproj = jnp.concatenate([dql, dkvl, dz, dxbc_pre, dga, dgs, dkpe, ddtr], axis=1)
    g["w_in"] = _mm(r["h_bf"], dproj, BF16, "dw_in", ta=True)
    dh = _mm(dproj, P["w_in"], F32, "d_h", tb=True, add=dh_a)
    return dh, g, carried


def _full_weight(g, axis):
    if axis == 1:
        return g.reshape(-1, g.shape[-1])
    return jnp.transpose(g, (1, 0, 2)).reshape(g.shape[1], -1)


def _grad_pieces(d, axis):
    if axis == 1:
        return d.reshape(N_DEV, -1, d.shape[1])
    return jnp.transpose(d.reshape(d.shape[0], N_DEV, -1), (1, 0, 2))


def _big_params(gathered):
    prep = {"w_in": ("w_in", _in_proj_pad), "w_q_b": ("w_q", _q_pad), "w_kv_b": ("w_kv", _kv_perm)}
    P = {}
    for n, g in gathered.items():
        key, fn = prep.get(n, (n, lambda a: a))
        P[key] = fn(_full_weight(g, BIG[n]))
    return P


def _layer_params(gathered, small, i):
    P = _big_params(gathered)
    P["q_norm_g"] = _row_vec(small["q_norm_g"][i])
    P["kv_norm_g"] = _row_vec(small["kv_norm_g"][i])
    P["dt_bias"] = _row_vec(small["dt_bias"][i], LANES)
    P["a_log"] = _row_vec(small["a_log"][i], LANES)
    P["d_skip"] = _row_vec(jnp.repeat(small["d_skip"][i], SSD_HEAD_DIM))
    P["ssd_norm_g"] = _row_vec(small["ssd_norm_g"][i])
    P["ssd_conv_w"] = _pad_rows8(small["ssd_conv_w"][i])
    P["ssd_conv_b"] = _row_vec(small["ssd_conv_b"][i])
    P["ffn_conv_w"] = _pad_rows8(small["ffn_conv_w"][i])
    P["ffn_conv_b"] = _row_vec(small["ffn_conv_b"][i])
    for n in ("ln1_g", "ln1_b", "ln2_g", "ln2_b"):
        P[n] = _row_vec(small[n][i])
    return P


def _layer_grads_to_reference_layout(g):
    out = {}
    out["w_in"] = _in_proj_unpad(g["w_in"])
    out["w_q_b"] = _q_unpad(g["w_q"])
    out["w_kv_b"] = _kv_unperm(g["w_kv"])
    for n in ("w_o_attn", "w_o_ssd", "w_out", "w_up", "w_down"):
        out[n] = g[n]
    out["q_norm_g"] = g["q_norm_g"][0]
    out["kv_norm_g"] = g["kv_norm_g"][0]
    out["dt_bias"] = g["dt_bias"][0, :SSD_HEADS]
    out["a_log"] = g["a_log"][0, :SSD_HEADS]
    out["d_skip"] = g["d_skip"].reshape(SSD_HEADS, SSD_HEAD_DIM).sum(axis=1)
    out["ssd_norm_g"] = g["ssd_norm_g"][0]
    out["ssd_conv_w"] = g["ssd_conv_w"][:SSD_CONV]
    out["ssd_conv_b"] = g["ssd_conv_b"][0]
    out["ffn_conv_w"] = g["ffn_conv_w"][:FFN_CONV]
    out["ffn_conv_b"] = g["ffn_conv_b"][0]
    for n in ("ln1_g", "ln1_b", "ln2_g", "ln2_b"):
        out[n] = g[n][0]
    return out


def kernel(x, meta_tokens, emb_ln_g, emb_ln_b, w_in, q_norm_g, w_q_b, kv_norm_g, w_kv_b, w_o_attn, ssd_conv_w, ssd_conv_b, dt_bias, a_log, d_skip, ssd_norm_g, w_o_ssd, w_out, ln1_g, ln1_b, w_up, ffn_conv_w, ffn_conv_b, w_down, ln2_g, ln2_b, loss_target, m_meta_tokens, m_emb_ln_g, m_emb_ln_b, m_w_in, m_q_norm_g, m_w_q_b, m_kv_norm_g, m_w_kv_b, m_w_o_attn, m_ssd_conv_w, m_ssd_conv_b, m_dt_bias, m_a_log, m_d_skip, m_ssd_norm_g, m_w_o_ssd, m_w_out, m_ln1_g, m_ln1_b, m_w_up, m_ffn_conv_w, m_ffn_conv_b, m_w_down, m_ln2_g, m_ln2_b, v_meta_tokens, v_emb_ln_g, v_emb_ln_b, v_w_in, v_q_norm_g, v_w_q_b, v_kv_norm_g, v_w_kv_b, v_w_o_attn, v_ssd_conv_w, v_ssd_conv_b, v_dt_bias, v_a_log, v_d_skip, v_ssd_norm_g, v_w_o_ssd, v_w_out, v_ln1_g, v_ln1_b, v_w_up, v_ffn_conv_w, v_ffn_conv_b, v_w_down, v_ln2_g, v_ln2_b):
    given = dict(locals())
    w = {n: given[n] for n in WEIGHTS}
    m = {n: given["m_" + n] for n in WEIGHTS}
    v = {n: given["v_" + n] for n in WEIGHTS}
    seq = x.shape[1]
    assert x.shape[0] == 1 and seq % LANES == 0
    npad = LANES - N_META
    Tp = npad + N_META + seq
    depth = w_in.shape[0]

    big_names, small_names = list(BIG), list(SMALL_SHARDED)
    ws, offs_s = _flatten([w[n] for n in small_names], SMALL_COLS, SUBLANES)
    shards = [{n: w[n][i].astype(BF16) for n in big_names} for i in range(depth)]
    got = _allgather([shards[0][n] for n in FIRST_USED] + [ws], "weight_allgather")
    gathered = dict(zip(FIRST_USED, got[:-1]))
    gsm = got[-1]
    small = {n: w[n] for n in REPLICATED}
    for n, (o, sz) in zip(small_names, offs_s):
        small[n] = _from_pieces(gsm.reshape(N_DEV, -1)[:, o:o + sz], w[n].shape, SMALL_SHARDED[n])

    fns = _make_stage_fns(npad)
    cos, sin, rot, expand = _tables(Tp, npad)
    tb = dict(cos=cos, sin=sin, rot=rot, expand=expand)
    top = jnp.pad(small["meta_tokens"], ((npad, 0), (0, 0)))
    hcat = jnp.concatenate([top, x[0]], axis=0)
    consts_e = [_row(_row_vec(w["emb_ln_g"])), _row(_row_vec(w["emb_ln_b"]))]
    h, h_bf = _rw_fwd(lambda *a: fns["ln"](*a) * 2, [_row(hcat)], consts_e, [_out(D_MODEL, F32), _out(D_MODEL, BF16)],
                      "emb_ln")
    layers, saved = [], []
    for i in range(depth):
        layers.append(_layer_params(gathered, small, i))
        late = AFTER_ATTENTION if i == 0 else []
        nxt = big_names if i + 1 < depth else []
        arrs = [shards[i][n] for n in late] + [shards[i + 1][n] for n in nxt]

        def on_carried(P, carried, late=late):
            P.update(_big_params(dict(zip(late, carried[:len(late)]))))
            return carried[len(late):]

        h, h_bf, res, carried = _layer_fwd(h, h_bf, layers[i], tb, fns, npad,
                                           bg=_Background("gather", arrs) if arrs else None, on_carried=on_carried)
        gathered = dict(zip(nxt, carried))
        saved.append(res)
    dh, lparts = _loss_head(h, loss_target[0], "loss_head")
    loss = lax.psum(jnp.sum(lparts[:, 0, 0]), ("x", "y", "c"))

    core = lax.axis_index("c")

    def chip_partials(pieces, tag):
        from_sibling = _sibling_exchange(pieces, "grad_exchange_cores_" + tag)
        sums = []
        for k, (p, r) in enumerate(zip(pieces, from_sibling)):
            own = lax.dynamic_index_in_dim(p.reshape((N_CHIPS, 2) + p.shape[1:]), core, axis=1, keepdims=False)
            sums.append(_add_pairs(own, r, "grad_chip_sum_%s_%d" % (tag, k)))
        return sums

    lg, recv_big, pending = [None] * depth, [None] * depth, []
    for i in reversed(range(depth)):
        early = AFTER_ATTENTION if i == 0 else []

        def before_attn(g, pending=pending, early=early, i=i):
            sums = pending + (chip_partials([_grad_pieces(g[n], BIG[n]) for n in early], "l%d_early" % i) if early else [])
            return _Background("chips", sums) if sums else None

        dh, gi, carried = _layer_bwd(dh, saved[i], layers[i], tb, fns, npad, before_attn=before_attn)
        if pending:
            recv_big[i + 1] = dict(zip(big_names, carried[:len(pending)]))
        recv_big[i] = dict(zip(early, carried[len(pending):]))
        lg[i] = _layer_grads_to_reference_layout(gi)
        pending = []
        if i > 0:
            pending = chip_partials([_grad_pieces(lg[i][n], BIG[n]) for n in big_names], "l%d" % i)
    (dhcat,), (d_emb_g, d_emb_b) = _rw_bwd(fns["ln"], [_row(hcat)], consts_e, [_row(dh)], [F32], "emb_ln_bwd")
    grad_x = dhcat[LANES:][None]
    local = {n: jnp.stack([lg[i][n] for i in range(depth)]) for n in lg[0] if n not in BIG}
    local["meta_tokens"] = dhcat[npad:LANES]
    local["emb_ln_g"] = d_emb_g[0]
    local["emb_ln_b"] = d_emb_b[0]

    sm_names = small_names + REPLICATED
    sm_pieces = [_to_pieces(local[n], SMALL_SHARDED[n]) for n in small_names]
    sm_pieces += [jnp.broadcast_to(local[n].reshape(1, -1), (N_DEV, local[n].size)) for n in REPLICATED]
    ps, _ = _flatten(sm_pieces, SMALL_COLS, BF16_ROWS, lead=True)
    pieces = [_grad_pieces(lg[0][n], BIG[n]) for n in FIRST_USED] + [ps]
    recv = _chip_exchange(chip_partials(pieces, "l0"), "grad_exchange_chips")
    recv_big[0].update(zip(FIRST_USED, recv[:-1]))
    outs = {}
    kinds = ("grad", "delta", "new_m", "new_v")
    for n in big_names:
        parts = jnp.stack([recv_big[i][n] for i in range(depth)], axis=1)
        for kind, a in zip(kinds, _adamw(parts, w[n], m[n], v[n], "adamw_" + n)):
            outs[kind + "_" + n] = a
    wf, offs = _flatten([w[n] for n in sm_names], SMALL_COLS, BF16_ROWS)
    mf, _ = _flatten([m[n] for n in sm_names], SMALL_COLS, BF16_ROWS)
    vf, _ = _flatten([v[n] for n in sm_names], SMALL_COLS, BF16_ROWS)
    shapes = [w[n].shape for n in sm_names]
    for kind, flat in zip(kinds, _adamw(recv[-1], wf, mf, vf, "adamw_small")):
        for n, a in zip(sm_names, _unflatten(flat, offs, shapes)):
            outs[kind + "_" + n] = a
    result = [loss, grad_x]
    for kind in ("grad", "delta", "new_m", "new_v"):
        result += [outs[kind + "_" + n] for n in WEIGHTS]
    return tuple(result)
```

```python
import functools

import jax
import jax.numpy as jnp
import numpy as np
from jax import lax
from jax.experimental import pallas as pl
from jax.experimental.pallas import tpu as pltpu

F32 = jnp.float32
BF16 = jnp.bfloat16
HIGHEST = lax.Precision.HIGHEST
SSD_PREC = lax.Precision.HIGH

D_MODEL = 1024
DEPTH = 2
N_META = 16
HEADS = 8
Q_LORA = 768
KV_LORA = 256
QK_NOPE = 128
QK_ROPE = 64
V_HEAD = 128
ROPE_THETA = 10000.0
SSD_INNER = 2048
SSD_HEAD_DIM = 64
SSD_HEADS = 32
SSD_GROUPS = 4
SSD_STATE = 128
SSD_CONV = 4
SSD_CONV_DIM = SSD_INNER + 2 * SSD_GROUPS * SSD_STATE
CHUNK = 128
D_FF = 2816
FFN_CONV = 3
LN_EPS = 1e-5
RMS_EPS = 1e-6
ALPHA = (2 * DEPTH) ** 0.25
IN_SIZES = (Q_LORA, KV_LORA, QK_ROPE, SSD_INNER, SSD_CONV_DIM, SSD_HEADS, D_MODEL, D_MODEL)
ATT_SCALE = (QK_NOPE + QK_ROPE) ** -0.5
NEG_INF = -1e30
ADAM_LR, ADAM_B1, ADAM_B2, ADAM_EPS, ADAM_WD, ADAM_STEP = 0.001, 0.9, 0.999, 1e-08, 0.01, 10

LANES = 128
SUBLANES = 8
VMEM_BYTES = 64 * 1024 * 1024
N_DEV = 8

OQ, OKV, OZ, OXBC, OGA, OGS, OKPE, ODT = 0, 768, 1024, 3072, 6144, 7168, 8192, 8320
IN_PAD = 8448
QHEAD = 256

ROW_TILE = 640
MM_COL_TILE = 1408
MM_ROW_TILE = 1664
MM_VMEM_BUDGET = 46 * 1024 * 1024
MM_K_TILE = 2816
MM_TOKEN_K_TILE = 1664
ATT_TILE = 640
ATT_HEADS_PER_STEP = 8
BF16_ROWS = 16
HALO = BF16_ROWS
ROW_BUDGET = 26 * 1024 * 1024
ADAM_ELEMS = 160 * 1024


def _pick(n, target, q=LANES):
    assert n % q == 0, (n, q)
    units = n // q
    best = q
    for d in range(1, units + 1):
        if units % d == 0 and d * q <= target:
            best = d * q
    return best


def _pick_rows(n, row_bytes):
    return _pick(n, max(BF16_ROWS, ROW_BUDGET // row_bytes), BF16_ROWS)


def _params(sem, est_bytes):
    limit = int(min(VMEM_BYTES - (6 << 20), max(32 << 20, 2 * est_bytes + (8 << 20))))
    return pltpu.CompilerParams(dimension_semantics=sem, vmem_limit_bytes=limit)


def _nbytes(shape, dtype):
    return int(np.prod(shape)) * jnp.dtype(dtype).itemsize


def _mm(a, b, out_dtype, name, ta=False, tb=False, add=None):
    assert not (ta and tb)
    if ta:
        K, M = a.shape
        tm = _pick(M, MM_COL_TILE)
        tk = _pick(K, MM_TOKEN_K_TILE)
    else:
        M, K = a.shape
        tk = _pick(K, MM_K_TILE)
    N, K2 = (b.shape if tb else b.shape[::-1])
    assert K == K2
    tn = _pick(N, MM_COL_TILE)
    nk = K // tk

    def vmem_estimate(tm):
        e = 2 * (tm * tk * a.dtype.itemsize + tk * tn * b.dtype.itemsize + tm * tn * jnp.dtype(out_dtype).itemsize)
        e += tm * tn * 4 + tm * tk * 2
        return e + (tm * tn * 4 if nk > 1 else 0) + (2 * tm * tn * 4 if add is not None else 0)

    if not ta:
        tm = _pick(M, MM_ROW_TILE, BF16_ROWS)
        while vmem_estimate(tm) > MM_VMEM_BUDGET and tm > BF16_ROWS:
            tm = _pick(M, tm - BF16_ROWS, BF16_ROWS)
    dn = (((0,), (0,)), ((), ())) if ta else ((((1,), (1,)), ((), ())) if tb else (((1,), (0,)), ((), ())))

    def body(*refs):
        a_ref, b_ref = refs[:2]
        add_ref = refs[2] if add is not None else None
        o_ref = refs[2 + (add is not None)]
        d = lax.dot_general(a_ref[...].astype(BF16), b_ref[...].astype(BF16), dn, preferred_element_type=F32)

        def finish(r):
            if add is not None:
                r = r + add_ref[...].astype(F32)
            o_ref[...] = r.astype(out_dtype)

        if nk == 1:
            finish(d)
            return
        acc = refs[-1]
        k = pl.program_id(2)

        @pl.when(k == 0)
        def _():
            acc[...] = d

        @pl.when((k > 0) & (k < nk - 1))
        def _():
            acc[...] += d

        @pl.when(k == nk - 1)
        def _():
            finish(acc[...] + d)

    if ta:
        a_spec = pl.BlockSpec((tk, tm), lambda i, j, k: (k, i))
    else:
        a_spec = pl.BlockSpec((tm, tk), lambda i, j, k: (i, k))
    b_spec = pl.BlockSpec((tn, tk), lambda i, j, k: (j, k)) if tb else pl.BlockSpec((tk, tn), lambda i, j, k: (k, j))
    in_specs = [a_spec, b_spec]
    args = [a, b]
    est = vmem_estimate(tm)
    if add is not None:
        in_specs.append(pl.BlockSpec((tm, tn), lambda i, j, k: (i, j)))
        args.append(add)
    return pl.pallas_call(
        body, name=name, grid=(M // tm, N // tn, nk), in_specs=in_specs,
        out_specs=pl.BlockSpec((tm, tn), lambda i, j, k: (i, j)),
        out_shape=jax.ShapeDtypeStruct((M, N), out_dtype),
        scratch_shapes=[pltpu.VMEM((tm, tn), F32)] if nk > 1 else [],
        compiler_params=_params(("parallel", "parallel", "arbitrary"), est),
    )(*args)


def _row(arr, bw=None, cb=0, grp=False, diff=True):
    return dict(arr=arr, bw=arr.shape[1] if bw is None else bw, cb=cb, grp=grp, diff=diff)


def _out(width, dtype, bw=None, grp=False):
    return dict(width=width, dtype=dtype, bw=width if bw is None else bw, grp=grp)


def _spec_rows(d, tm):
    return pl.BlockSpec((tm, d["bw"]), lambda g, i, cb=d["cb"], gr=d["grp"]: (i, cb + (g if gr else 0)))


def _spec_const(d):
    return pl.BlockSpec((d["arr"].shape[0], d["bw"]), lambda g, i, cb=d["cb"], gr=d["grp"]: (0, cb + (g if gr else 0)))


def _rw_fwd(fn, rows, consts, outs, name, ng=1):
    Tp = rows[0]["arr"].shape[0]
    tm = _pick_rows(Tp, 4 * (sum(d["bw"] for d in rows) + 2 * sum(o["bw"] for o in outs)))
    nr, ncst = len(rows), len(consts)

    def body(*refs):
        i = pl.program_id(1)
        rowidx = i * tm + lax.broadcasted_iota(jnp.int32, (tm, 1), 0)
        rv = [r[...].astype(F32) for r in refs[:nr]]
        cv = [c[...] for c in refs[nr:nr + ncst]]
        vals = fn(rowidx, *rv, *cv)
        for o, v in zip(refs[nr + ncst:], vals):
            o[...] = v.astype(o.dtype)

    est = sum(tm * d["bw"] * 4 for d in rows) + sum(tm * o["bw"] * 4 for o in outs)
    return pl.pallas_call(
        body, name=name, grid=(ng, Tp // tm),
        in_specs=[_spec_rows(d, tm) for d in rows] + [_spec_const(d) for d in consts],
        out_specs=[pl.BlockSpec((tm, o["bw"]), lambda g, i, gr=o["grp"]: (i, g if gr else 0)) for o in outs],
        out_shape=[jax.ShapeDtypeStruct((Tp, o["width"]), o["dtype"]) for o in outs],
        compiler_params=_params(("parallel", "parallel"), 3 * est),
    )(*[d["arr"] for d in rows], *[d["arr"] for d in consts])


def _rw_bwd(fn, rows, consts, cots, drow_dtypes, name, ng=1):
    Tp = rows[0]["arr"].shape[0]
    tm = _pick_rows(Tp, 4 * (3 * sum(d["bw"] for d in rows) + 2 * sum(d["bw"] for d in cots)))
    nr, ncst, nct = len(rows), len(consts), len(cots)
    drows = [k for k, d in enumerate(rows) if d["diff"]]
    dcsts = [k for k, d in enumerate(consts) if d["diff"]]
    for k in drows:
        assert rows[k]["grp"] or ng == 1

    def body(*refs):
        g = pl.program_id(0)
        i = pl.program_id(1)
        rowidx = i * tm + lax.broadcasted_iota(jnp.int32, (tm, 1), 0)
        rv = [r[...].astype(F32) for r in refs[:nr]]
        cv = [c[...] for c in refs[nr:nr + ncst]]
        ct = tuple(r[...].astype(F32) for r in refs[nr + ncst:nr + ncst + nct])
        orefs = refs[nr + ncst + nct:]

        def f(*dargs):
            rr, cc = list(rv), list(cv)
            for k, v in zip(drows, dargs[:len(drows)]):
                rr[k] = v
            for k, v in zip(dcsts, dargs[len(drows):]):
                cc[k] = v
            return tuple(fn(rowidx, *rr, *cc))

        _, vjp = jax.vjp(f, *[rv[k] for k in drows], *[cv[k] for k in dcsts])
        grads = vjp(ct)
        for o, v in zip(orefs[:len(drows)], grads[:len(drows)]):
            o[...] = v.astype(o.dtype)
        for k, o, v in zip(dcsts, orefs[len(drows):], grads[len(drows):]):
            first = (i == 0) if consts[k]["grp"] else ((i == 0) & (g == 0))

            @pl.when(first)
            def _(o=o, v=v):
                o[...] = v

            @pl.when(jnp.logical_not(first))
            def _(o=o, v=v):
                o[...] += v

    out_specs, out_shape = [], []
    for k, dt in zip(drows, drow_dtypes):
        d = rows[k]
        out_specs.append(pl.BlockSpec((tm, d["bw"]), lambda g, i, gr=d["grp"]: (i, g if gr else 0)))
        out_shape.append(jax.ShapeDtypeStruct((Tp, d["bw"] * (ng if d["grp"] else 1)), dt))
    for k in dcsts:
        d = consts[k]
        r = d["arr"].shape[0]
        out_specs.append(pl.BlockSpec((r, d["bw"]), lambda g, i, gr=d["grp"]: (0, g if gr else 0)))
        out_shape.append(jax.ShapeDtypeStruct((r, d["bw"] * (ng if d["grp"] else 1)), F32))
    est = sum(tm * d["bw"] * 4 for d in rows) * 2 + sum(tm * d["bw"] * 4 for d in cots)
    res = pl.pallas_call(
        body, name=name, grid=(ng, Tp // tm),
        in_specs=[_spec_rows(d, tm) for d in rows] + [_spec_const(d) for d in consts] + [_spec_rows(d, tm) for d in cots],
        out_specs=out_specs, out_shape=out_shape,
        compiler_params=_params(("arbitrary", "arbitrary"), 3 * est),
    )(*[d["arr"] for d in rows], *[d["arr"] for d in consts], *[d["arr"] for d in cots])
    return list(res[:len(drows)]), list(res[len(drows):])


def _sigmoid(x):
    return 0.5 * jnp.tanh(0.5 * x) + 0.5


def _silu(x):
    return x * _sigmoid(x)


def _softplus(x):
    return jnp.maximum(x, 0.0) + jnp.log(1.0 + jnp.exp(-jnp.abs(x)))


def _layer_norm(x, g, b):
    mu = jnp.mean(x, axis=-1, keepdims=True)
    xc = x - mu
    var = jnp.mean(xc * xc, axis=-1, keepdims=True)
    return xc * lax.rsqrt(var + LN_EPS) * g + b


def _rms_norm(x, g):
    return x * lax.rsqrt(jnp.mean(x * x, axis=-1, keepdims=True) + RMS_EPS) * g


def _rope(r, cos, sin, rot):
    return r * cos + jnp.dot(r, rot, precision=HIGHEST, preferred_element_type=F32) * sin


def _make_stage_fns(npad):
    def fn_ln_masked(rowidx, x, g, b):
        return (jnp.where(rowidx >= npad, _layer_norm(x, g, b), 0.0),)

    def fn_in_post(rowidx, ql, kvl, kpe, dtr, cos, sin, rot, qg, kvg, dtb):
        qn = _rms_norm(ql, qg)
        kvn = _rms_norm(kvl, kvg)
        kr = _rope(kpe, cos, sin, rot)
        lane = lax.broadcasted_iota(jnp.int32, (1, LANES), 1)
        dt = jnp.where((rowidx >= npad) & (lane < SSD_HEADS), _softplus(dtr + dtb), 0.0)
        return qn, kvn, jnp.concatenate([kr] * HEADS, axis=1), dt

    def fn_q_post(rowidx, q, cos, sin, rot):
        rr = _rope(q[:, QK_NOPE:], cos, sin, rot)
        return (jnp.concatenate([q[:, :QK_NOPE], rr], axis=1) * ATT_SCALE,)

    def fn_gated_norm(rowidx, y, xs, z, dskip, g):
        v = (y + xs * dskip) * _silu(z)
        return (v * lax.rsqrt(jnp.mean(v * v, axis=-1, keepdims=True) + RMS_EPS) * g,)

    def fn_mix(rowidx, ga, gs, ya, ys):
        return (_sigmoid(ga) * ya + _sigmoid(gs) * ys,)

    def fn_res_ln(rowidx, h, r, g, b):
        return (jnp.where(rowidx >= npad, _layer_norm(ALPHA * h + r, g, b), 0.0),)

    def fn_glu(rowidx, u):
        return (_silu(u[:, :D_FF]) * u[:, D_FF:],)

    return dict(ln=fn_ln_masked, in_post=fn_in_post, q_post=fn_q_post, gated=fn_gated_norm, mix=fn_mix,
                res_ln=fn_res_ln, glu=fn_glu)


def _conv_tiles(Tp, C):
    return _pick(Tp, ROW_TILE), _pick(C, MM_COL_TILE)


def _conv_fwd(x, xoff, C, w8, b, K, act, npad, name, out_dtype=F32):
    Tp = x.shape[0]
    tm, tc = _conv_tiles(Tp, C)
    assert xoff % tc == 0
    cb0 = xoff // tc
    rb = tm // HALO

    def body(prev_ref, main_ref, w_ref, b_ref, o_ref):
        i = pl.program_id(1)
        main = main_ref[...].astype(F32)
        prev = jnp.where(i > 0, prev_ref[...].astype(F32), 0.0)
        ext = jnp.concatenate([prev, main], axis=0)
        acc = b_ref[...] + w_ref[K - 1:K, :] * main
        for k in range(K - 1):
            s = K - 1 - k
            acc = acc + w_ref[k:k + 1, :] * pltpu.roll(ext, s, 0)[HALO:, :]
        if act:
            rowidx = i * tm + lax.broadcasted_iota(jnp.int32, (tm, 1), 0)
            acc = jnp.where(rowidx >= npad, _silu(acc), 0.0)
        o_ref[...] = acc.astype(o_ref.dtype)

    return pl.pallas_call(
        body, name=name, grid=(C // tc, Tp // tm),
        in_specs=[pl.BlockSpec((HALO, tc), lambda g, i: (jnp.maximum(i * rb - 1, 0), cb0 + g)),
                  pl.BlockSpec((tm, tc), lambda g, i: (i, cb0 + g)),
                  pl.BlockSpec((SUBLANES, tc), lambda g, i: (0, g)),
                  pl.BlockSpec((1, tc), lambda g, i: (0, g))],
        out_specs=pl.BlockSpec((tm, tc), lambda g, i: (i, g)),
        out_shape=jax.ShapeDtypeStruct((Tp, C), out_dtype),
        compiler_params=_params(("parallel", "parallel"), 8 * tm * tc * 4),
    )(x, x, w8, b)


def _conv_bwd(x, xoff, C, w8, b, dy, K, act, npad, name):
    Tp = x.shape[0]
    tm, tc = _conv_tiles(Tp, C)
    cb0 = xoff // tc
    rb = tm // HALO
    ni = Tp // tm
    last_rb = Tp // HALO - 1
    n = tm + 2 * HALO

    def body(xp_ref, xm_ref, xn_ref, dym_ref, dyn_ref, w_ref, b_ref, dx_ref, dw_ref, db_ref):
        i = pl.program_id(1)
        prev = jnp.where(i > 0, xp_ref[...].astype(F32), 0.0)
        ext = jnp.concatenate([prev, xm_ref[...].astype(F32), xn_ref[...].astype(F32)], axis=0)
        dyn = jnp.where(i < ni - 1, dyn_ref[...].astype(F32), 0.0)
        dpre = jnp.concatenate([jnp.zeros((HALO, tc), F32), dym_ref[...].astype(F32), dyn], axis=0)
        shifted = [ext if k == K - 1 else pltpu.roll(ext, K - 1 - k, 0) for k in range(K)]
        if act:
            pre = b_ref[...] + sum(w_ref[k:k + 1, :] * shifted[k] for k in range(K))
            rowidx = i * tm - HALO + lax.broadcasted_iota(jnp.int32, (n, 1), 0)
            sg = _sigmoid(pre)
            dpre = jnp.where(rowidx >= npad, dpre * sg * (1.0 + pre * (1.0 - sg)), 0.0)
        dx = w_ref[K - 1:K, :] * dpre
        for k in range(K - 1):
            dx = dx + w_ref[k:k + 1, :] * pltpu.roll(dpre, n - (K - 1 - k), 0)
        dx_ref[...] = dx[HALO:HALO + tm, :].astype(dx_ref.dtype)

        @pl.when(i == 0)
        def _():
            dw_ref[...] = jnp.zeros_like(dw_ref)
            db_ref[...] = jnp.zeros_like(db_ref)

        dmain = dpre[HALO:HALO + tm, :]
        for k in range(K):
            dw_ref[k:k + 1, :] += jnp.sum(dmain * shifted[k][HALO:HALO + tm, :], axis=0, keepdims=True)
        db_ref[...] += jnp.sum(dmain, axis=0, keepdims=True)

    return pl.pallas_call(
        body, name=name, grid=(C // tc, ni),
        in_specs=[pl.BlockSpec((HALO, tc), lambda g, i: (jnp.maximum(i * rb - 1, 0), cb0 + g)),
                  pl.BlockSpec((tm, tc), lambda g, i: (i, cb0 + g)),
                  pl.BlockSpec((HALO, tc), lambda g, i: (jnp.minimum((i + 1) * rb, last_rb), cb0 + g)),
                  pl.BlockSpec((tm, tc), lambda g, i: (i, g)),
                  pl.BlockSpec((HALO, tc), lambda g, i: (jnp.minimum((i + 1) * rb, last_rb), g)),
                  pl.BlockSpec((SUBLANES, tc), lambda g, i: (0, g)),
                  pl.BlockSpec((1, tc), lambda g, i: (0, g))],
        out_specs=[pl.BlockSpec((tm, tc), lambda g, i: (i, g)),
                   pl.BlockSpec((SUBLANES, tc), lambda g, i: (0, g)),
                   pl.BlockSpec((1, tc), lambda g, i: (0, g))],
        out_shape=[jax.ShapeDtypeStruct((Tp, C), BF16), jax.ShapeDtypeStruct((SUBLANES, C), F32),
                   jax.ShapeDtypeStruct((1, C), F32)],
        compiler_params=_params(("parallel", "arbitrary"), 14 * tm * tc * 4),
    )(x, x, x, dy, dy, w8, b)


def _split_refs(refs, n_in, n_out, n_scratch, nbg):
    cuts = np.cumsum([0, n_in, nbg, n_out, nbg, n_scratch])
    return tuple(refs[a:b] for a, b in zip(cuts[:-1], cuts[1:])) + (refs[cuts[-1]:],)


def _flash_fwd(q, kv, kr8, npad, name, bg=None):
    Tp = q.shape[0]
    t = _pick(Tp, ATT_TILE)
    hp = ATT_HEADS_PER_STEP
    nb = Tp // t
    ng = HEADS // hp
    nbg = bg.n if bg else 0
    nt = (((1,), (1,)), ((), ()))
    tn = (((0,), (0,)), ((), ()))

    def body(*refs):
        (q_ref, kn_ref, kr_ref, v_ref), bg_in, (o_ref, lse_ref), bg_out, (m_sc, l_sc, acc_sc), bg_sems = _split_refs(
            refs, 4, 2, 3, nbg)
        g = pl.program_id(0)
        qi = pl.program_id(1)
        ki = pl.program_id(2)
        if bg:
            @pl.when((g == 0) & (qi == 0) & (ki == 0))
            def _():
                bg.start(bg_in, bg_out, bg_sems)

        @pl.when(ki == 0)
        def _():
            m_sc[...] = jnp.full_like(m_sc, NEG_INF)
            l_sc[...] = jnp.zeros_like(l_sc)
            acc_sc[...] = jnp.zeros_like(acc_sc)

        def step(masked):
            kr = kr_ref[...]
            if masked:
                key = ki * t + lax.broadcasted_iota(jnp.int32, (t, t), 0)
                qry = qi * t + lax.broadcasted_iota(jnp.int32, (t, t), 1)
                visible = (key <= qry) & (key >= npad)
            for hh in range(hp):
                k = jnp.concatenate([kn_ref[:, hh * QK_NOPE:(hh + 1) * QK_NOPE], kr], axis=1)
                st = lax.dot_general(k, q_ref[:, hh * QHEAD:(hh + 1) * QHEAD], nt, preferred_element_type=F32)
                if masked:
                    st = jnp.where(visible, st, NEG_INF)
                vs = slice(hh * V_HEAD, (hh + 1) * V_HEAD)
                m_prev = m_sc[hh]
                m_new = jnp.maximum(m_prev, jnp.max(st, axis=0, keepdims=True))
                pt = jnp.exp(st - m_new)
                a = jnp.exp(m_prev - m_new)
                l_sc[hh] = a * l_sc[hh] + jnp.sum(pt, axis=0, keepdims=True)
                acc_sc[vs, :] = a * acc_sc[vs, :] + lax.dot_general(v_ref[:, vs], pt.astype(BF16), tn,
                                                                    preferred_element_type=F32)
                m_sc[hh] = m_new

        need_mask = (ki == qi) | (ki == 0)

        @pl.when((ki <= qi) & need_mask)
        def _():
            step(True)

        @pl.when((ki <= qi) & jnp.logical_not(need_mask))
        def _():
            step(False)

        @pl.when(ki == qi)
        def _():
            for hh in range(hp):
                vs = slice(hh * V_HEAD, (hh + 1) * V_HEAD)
                l = l_sc[hh]
                o_ref[:, vs] = (acc_sc[vs, :] / l).T.astype(o_ref.dtype)
                lse_ref[hh * SUBLANES:(hh + 1) * SUBLANES, :] = jnp.broadcast_to(m_sc[hh] + jnp.log(l), (SUBLANES, t))

        if bg:
            @pl.when((g == ng - 1) & (qi == nb - 1) & (ki == nb - 1))
            def _():
                bg.wait(bg_in, bg_out, bg_sems)

    kmin = lambda qi, ki: jnp.minimum(ki, qi)
    return pl.pallas_call(
        body, name=name, grid=(ng, nb, nb),
        in_specs=[pl.BlockSpec((t, hp * QHEAD), lambda g, qi, ki: (qi, g)),
                  pl.BlockSpec((t, hp * QK_NOPE), lambda g, qi, ki: (kmin(qi, ki), g)),
                  pl.BlockSpec((t, LANES), lambda g, qi, ki: (kmin(qi, ki), 0)),
                  pl.BlockSpec((t, hp * V_HEAD), lambda g, qi, ki: (kmin(qi, ki), ng + g))] + (bg.specs if bg else []),
        out_specs=[pl.BlockSpec((t, hp * V_HEAD), lambda g, qi, ki: (qi, g)),
                   pl.BlockSpec((hp * SUBLANES, t), lambda g, qi, ki: (g, qi))] + (bg.specs if bg else []),
        out_shape=[jax.ShapeDtypeStruct((Tp, HEADS * V_HEAD), F32), jax.ShapeDtypeStruct((HEADS * SUBLANES, Tp), F32)]
        + (bg.out_shape if bg else []),
        scratch_shapes=[pltpu.VMEM((hp, 1, t), F32), pltpu.VMEM((hp, 1, t), F32), pltpu.VMEM((hp * V_HEAD, t), F32)]
        + (bg.scratch if bg else []),
        compiler_params=_params(("arbitrary",) * 3 if bg else ("parallel", "parallel", "arbitrary"), 8 * hp * t * t * 4),
    )(q, kv, kr8, kv, *(bg.arrs if bg else []))


def _attn_delta(do, o, name):
    Tp = do.shape[0]
    tm = _pick(Tp, MM_TOKEN_K_TILE)

    def body(do_ref, o_ref, d_ref):
        prod = do_ref[...] * o_ref[...]
        ones = jnp.ones((SUBLANES, V_HEAD), F32)
        d_ref[...] = lax.dot_general(ones, prod, (((1,), (1,)), ((), ())), precision=HIGHEST,
                                     preferred_element_type=F32)

    return pl.pallas_call(
        body, name=name, grid=(HEADS, Tp // tm),
        in_specs=[pl.BlockSpec((tm, V_HEAD), lambda h, i: (i, h)), pl.BlockSpec((tm, V_HEAD), lambda h, i: (i, h))],
        out_specs=pl.BlockSpec((SUBLANES, tm), lambda h, i: (h, i)),
        out_shape=jax.ShapeDtypeStruct((HEADS * SUBLANES, Tp), F32),
        compiler_params=_params(("parallel", "parallel"), 4 * tm * V_HEAD * 4),
    )(do, o)


def _flash_bwd(q, kv, kr8, do, lse, delta, npad, name, bg=None):
    Tp = q.shape[0]
    t = _pick(Tp, ATT_TILE)
    nb = Tp // t
    nbg = bg.n if bg else 0
    nt = (((1,), (1,)), ((), ()))
    tn = (((0,), (0,)), ((), ()))

    def body(*refs):
        ((q_ref, kn_ref, kr_ref, v_ref, do_ref, lse_ref, dl_ref), bg_in, (dq_ref, dkn_ref, dkr_ref, dv_ref), bg_out,
         (dk_sc, dv_sc), bg_sems) = _split_refs(refs, 7, 4, 2, nbg)
        h = pl.program_id(0)
        ki = pl.program_id(1)
        qi = pl.program_id(2)
        if bg:
            @pl.when((h == 0) & (ki == 0) & (qi == 0))
            def _():
                bg.start(bg_in, bg_out, bg_sems)

        @pl.when(qi == 0)
        def _():
            dk_sc[...] = jnp.zeros_like(dk_sc)
            dv_sc[...] = jnp.zeros_like(dv_sc)

        def step(masked):
            qv = q_ref[...]
            k = jnp.concatenate([kn_ref[...], kr_ref[...]], axis=1)
            st = lax.dot_general(k, qv, nt, preferred_element_type=F32)
            if masked:
                key = ki * t + lax.broadcasted_iota(jnp.int32, (t, t), 0)
                qry = qi * t + lax.broadcasted_iota(jnp.int32, (t, t), 1)
                st = jnp.where((key <= qry) & (key >= npad), st, NEG_INF)
            pt = jnp.exp(st - lse_ref[0:1, :])
            dob = do_ref[...].astype(BF16)
            dv_sc[...] += jnp.dot(pt.astype(BF16), dob, preferred_element_type=F32)
            dpt = lax.dot_general(v_ref[...], dob, nt, preferred_element_type=F32)
            dst = (pt * (dpt - dl_ref[0:1, :])).astype(BF16)
            dk_sc[...] += jnp.dot(dst, qv, preferred_element_type=F32)
            dqc = lax.dot_general(dst, k, tn, preferred_element_type=F32)
            rows = pl.ds(pl.multiple_of(qi * t, t), t)

            @pl.when(ki == 0)
            def _():
                dq_ref[rows, :] = dqc

            @pl.when(ki > 0)
            def _():
                dq_ref[rows, :] += dqc

        need_mask = (ki == qi) | (ki == 0)

        @pl.when((qi >= ki) & need_mask)
        def _():
            step(True)

        @pl.when((qi >= ki) & jnp.logical_not(need_mask))
        def _():
            step(False)

        @pl.when(qi == nb - 1)
        def _():
            dkn_ref[...] = dk_sc[:, :QK_NOPE].astype(dkn_ref.dtype)
            dkr_ref[...] = dk_sc[:, QK_NOPE:].astype(dkr_ref.dtype)
            dv_ref[...] = dv_sc[...].astype(dv_ref.dtype)

        if bg:
            @pl.when((h == HEADS - 1) & (ki == nb - 1) & (qi == nb - 1))
            def _():
                bg.wait(bg_in, bg_out, bg_sems)

    qmap = lambda h, ki, qi: (jnp.maximum(qi, ki), h)
    kmap = lambda h, ki, qi: (ki, h)
    est = 2 * Tp * QHEAD * 4 + 8 * t * t * 4
    return pl.pallas_call(
        body, name=name, grid=(HEADS, nb, nb),
        in_specs=[pl.BlockSpec((t, QHEAD), qmap),
                  pl.BlockSpec((t, QK_NOPE), kmap),
                  pl.BlockSpec((t, LANES), kmap),
                  pl.BlockSpec((t, V_HEAD), lambda h, ki, qi: (ki, HEADS + h)),
                  pl.BlockSpec((t, V_HEAD), qmap),
                  pl.BlockSpec((SUBLANES, t), lambda h, ki, qi: (h, jnp.maximum(qi, ki))),
                  pl.BlockSpec((SUBLANES, t), lambda h, ki, qi: (h, jnp.maximum(qi, ki)))] + (bg.specs if bg else []),
        out_specs=[pl.BlockSpec((Tp, QHEAD), lambda h, ki, qi: (0, h)),
                   pl.BlockSpec((t, QK_NOPE), kmap),
                   pl.BlockSpec((t, LANES), kmap),
                   pl.BlockSpec((t, V_HEAD), kmap)] + (bg.specs if bg else []),
        out_shape=[jax.ShapeDtypeStruct((Tp, HEADS * QHEAD), F32),
                   jax.ShapeDtypeStruct((Tp, HEADS * QK_NOPE), BF16),
                   jax.ShapeDtypeStruct((Tp, HEADS * LANES), F32),
                   jax.ShapeDtypeStruct((Tp, HEADS * V_HEAD), BF16)] + (bg.out_shape if bg else []),
        scratch_shapes=[pltpu.VMEM((t, QHEAD), F32), pltpu.VMEM((t, V_HEAD), F32)] + (bg.scratch if bg else []),
        compiler_params=_params(("arbitrary",) * 3 if bg else ("parallel", "arbitrary", "arbitrary"), est),
    )(q, kv, kr8, kv, do, lse, delta, *(bg.arrs if bg else []))


GW = SSD_INNER // SSD_GROUPS
PAIRS_PER_GROUP = GW // LANES
XB = SSD_INNER // GW
NT_DIMS = (((1,), (1,)), ((), ()))
TN_DIMS = (((0,), (0,)), ((), ()))


def _ssd_common(xs_ref, dt_ref, alog_ref, e_ref):
    a_neg = -jnp.exp(alog_ref[...])
    dt = dt_ref[...]
    li = lax.broadcasted_iota(jnp.int32, (CHUNK, CHUNK), 0)
    si = lax.broadcasted_iota(jnp.int32, (CHUNK, CHUNK), 1)
    tril = li >= si
    tri = tril.astype(F32)
    acs = jnp.dot(tri, dt * a_neg, precision=SSD_PREC, preferred_element_type=F32)
    e = e_ref[...]
    dte = jnp.dot(dt, e, precision=SSD_PREC, preferred_element_type=F32)
    acse = jnp.dot(acs, e, precision=SSD_PREC, preferred_element_type=F32)
    x = xs_ref[...] * dte
    alast = acse[CHUNK - 1:CHUNK, :]
    return dict(a_neg=a_neg, dt=dt, tril=tril, tri=tri, acs=acs, acs_t=acs.T, e=e, dte=dte, acse=acse, x=x,
                p_e=jnp.exp(acse), w_e=jnp.exp(alast - acse), dl_e=jnp.exp(alast), li=li, si=si)


def _decay(cm, head):
    col = cm["acs"][:, head:head + 1]
    row = cm["acs_t"][head:head + 1, :]
    return jnp.exp(jnp.where(cm["tril"], col - row, -jnp.inf))


def _ssd_fwd(xbc, dt, alog, e, name):
    Tp = xbc.shape[0]
    nc = Tp // CHUNK

    def body(xs_ref, b_ref, c_ref, dt_ref, alog_ref, e_ref, y_ref, st_ref, st_sc):
        @pl.when(pl.program_id(0) == 0)
        def _():
            st_sc[...] = jnp.zeros_like(st_sc)

        cm = _ssd_common(xs_ref, dt_ref, alog_ref, e_ref)
        st_ref[0] = st_sc[...]
        lane = lax.broadcasted_iota(jnp.int32, (CHUNK, LANES), 1)
        for g in range(SSD_GROUPS):
            gs = slice(g * GW, (g + 1) * GW)
            cg = c_ref[:, g * SSD_STATE:(g + 1) * SSD_STATE].astype(BF16)
            bg = b_ref[:, g * SSD_STATE:(g + 1) * SSD_STATE].astype(BF16)
            cb = lax.dot_general(cg, bg, NT_DIMS, preferred_element_type=F32)
            stg = st_sc[:, gs]
            yoff = jnp.dot(cg, stg.astype(BF16), preferred_element_type=F32) * cm["p_e"][:, gs]
            xg = cm["x"][:, gs]
            for jp in range(PAIRS_PER_GROUP):
                j = g * PAIRS_PER_GROUP + jp
                xp = xg[:, jp * LANES:(jp + 1) * LANES].astype(BF16)
                ys = []
                for head in (2 * j, 2 * j + 1):
                    m = (cb * _decay(cm, head)).astype(BF16)
                    ys.append(jnp.dot(m, xp, preferred_element_type=F32))
                y_ref[:, j * LANES:(j + 1) * LANES] = (jnp.where(lane < SSD_HEAD_DIM, ys[0], ys[1])
                                                       + yoff[:, jp * LANES:(jp + 1) * LANES])
            snew = lax.dot_general(bg, (cm["w_e"][:, gs] * xg).astype(BF16), TN_DIMS, preferred_element_type=F32)
            st_sc[:, gs] = cm["dl_e"][:, gs] * stg + snew

    return pl.pallas_call(
        body, name=name, grid=(nc,),
        in_specs=[pl.BlockSpec((CHUNK, SSD_INNER), lambda c: (c, 0)),
                  pl.BlockSpec((CHUNK, GW), lambda c: (c, XB)),
                  pl.BlockSpec((CHUNK, GW), lambda c: (c, XB + 1)),
                  pl.BlockSpec((CHUNK, LANES), lambda c: (c, 0)),
                  pl.BlockSpec((1, LANES), lambda c: (0, 0)),
                  pl.BlockSpec((LANES, SSD_INNER), lambda c: (0, 0))],
        out_specs=[pl.BlockSpec((CHUNK, SSD_INNER), lambda c: (c, 0)),
                   pl.BlockSpec((1, SSD_STATE, SSD_INNER), lambda c: (c, 0, 0))],
        out_shape=[jax.ShapeDtypeStruct((Tp, SSD_INNER), F32), jax.ShapeDtypeStruct((nc, SSD_STATE, SSD_INNER), F32)],
        scratch_shapes=[pltpu.VMEM((SSD_STATE, SSD_INNER), F32)],
        compiler_params=_params(("arbitrary",), 24 * CHUNK * SSD_INNER * 4),
    )(xbc, xbc, xbc, dt, alog, e)


def _ssd_bwd(xbc, dt, alog, e, dy, dxs_skip, states, name):
    Tp = xbc.shape[0]
    nc = Tp // CHUNK
    rev = lambda c: nc - 1 - c

    def body(xs_ref, b_ref, c_ref, dt_ref, alog_ref, e_ref, dy_ref, skip_ref, st_ref,
             dxbc_ref, ddt_ref, dalog_ref, dst_sc, dx_sc, t_sc, tw_sc):
        @pl.when(pl.program_id(0) == 0)
        def _():
            dst_sc[...] = jnp.zeros_like(dst_sc)
            dalog_ref[...] = jnp.zeros_like(dalog_ref)

        cm = _ssd_common(xs_ref, dt_ref, alog_ref, e_ref)
        lane = lax.broadcasted_iota(jnp.int32, (CHUNK, LANES), 1)
        dacs_col = jnp.zeros((CHUNK, LANES), F32)
        dacs_row = jnp.zeros((LANES, CHUNK), F32)
        t_last = []
        for g in range(SSD_GROUPS):
            gs = slice(g * GW, (g + 1) * GW)
            cg = c_ref[:, g * SSD_STATE:(g + 1) * SSD_STATE].astype(BF16)
            bg = b_ref[:, g * SSD_STATE:(g + 1) * SSD_STATE].astype(BF16)
            stg = st_ref[0, :, gs]
            stg_b = stg.astype(BF16)
            dstg = dst_sc[:, gs]
            dstg_b = dstg.astype(BF16)
            xg = cm["x"][:, gs]
            dyg = dy_ref[:, gs]
            zg = jnp.dot(cg, stg_b, preferred_element_type=F32)
            dzg = dyg * cm["p_e"][:, gs]
            dzg_b = dzg.astype(BF16)
            dcg = lax.dot_general(dzg_b, stg_b, NT_DIMS, preferred_element_type=F32)
            dst_in = lax.dot_general(cg, dzg_b, TN_DIMS, preferred_element_type=F32)
            dst_in = dst_in + cm["dl_e"][:, gs] * dstg
            t_last.append(jnp.sum(dstg * stg * cm["dl_e"][:, gs], axis=0, keepdims=True))
            weg = cm["w_e"][:, gs]
            dbg = lax.dot_general((weg * xg).astype(BF16), dstg_b, NT_DIMS, preferred_element_type=F32)
            gg = jnp.dot(bg, dstg_b, preferred_element_type=F32)
            dxg = weg * gg
            tw_sc[:, gs] = xg * dxg
            t_sc[:, gs] = dzg * zg - xg * dxg
            cb = lax.dot_general(cg, bg, NT_DIMS, preferred_element_type=F32)
            dcb = jnp.zeros((CHUNK, CHUNK), F32)
            for jp in range(PAIRS_PER_GROUP):
                j = g * PAIRS_PER_GROUP + jp
                ps = slice(jp * LANES, (jp + 1) * LANES)
                xp = xg[:, ps].astype(BF16)
                dyp = dyg[:, ps]
                dxp = dxg[:, ps]
                for half, head in enumerate((2 * j, 2 * j + 1)):
                    lam = _decay(cm, head)
                    m32 = cb * lam
                    sel = (lane < SSD_HEAD_DIM) if half == 0 else (lane >= SSD_HEAD_DIM)
                    dye = jnp.where(sel, dyp, 0.0).astype(BF16)
                    dm = lax.dot_general(dye, xp, NT_DIMS, preferred_element_type=F32)
                    w = dm * m32
                    dacs_col = dacs_col + jnp.where(cm["si"] == head, jnp.sum(w, axis=1, keepdims=True), 0.0)
                    dacs_row = dacs_row + jnp.where(cm["li"] == head, jnp.sum(w, axis=0, keepdims=True), 0.0)
                    dcb = dcb + dm * lam
                    dxp = dxp + lax.dot_general(m32.astype(BF16), dye, TN_DIMS, preferred_element_type=F32)
                dx_sc[:, j * LANES:(j + 1) * LANES] = dxp
            dcb_b = dcb.astype(BF16)
            dcg = dcg + jnp.dot(dcb_b, bg, preferred_element_type=F32)
            dbg = dbg + lax.dot_general(dcb_b, cg, TN_DIMS, preferred_element_type=F32)
            dst_sc[:, gs] = dst_in
            dxbc_ref[:, SSD_INNER + g * SSD_STATE:SSD_INNER + (g + 1) * SSD_STATE] = dbg
            dxbc_ref[:, SSD_INNER + GW + g * SSD_STATE:SSD_INNER + GW + (g + 1) * SSD_STATE] = dcg
        e = cm["e"]
        dacs = lax.dot_general(t_sc[...], e, NT_DIMS, precision=SSD_PREC, preferred_element_type=F32)
        dacs = dacs + dacs_col - dacs_row.T
        last_lane = jnp.concatenate(t_last, axis=1) + jnp.sum(tw_sc[...], axis=0, keepdims=True)
        last_head = lax.dot_general(jnp.broadcast_to(last_lane, (SUBLANES, SSD_INNER)), e, NT_DIMS,
                                    precision=SSD_PREC, preferred_element_type=F32)[0:1, :]
        dacs = dacs + jnp.where(cm["li"] == CHUNK - 1, last_head, 0.0)
        da = lax.dot_general(cm["tri"], dacs, TN_DIMS, precision=SSD_PREC, preferred_element_type=F32)
        dx_all = dx_sc[...]
        ddt = da * cm["a_neg"] + lax.dot_general(dx_all * xs_ref[...], e, NT_DIMS, precision=SSD_PREC,
                                                 preferred_element_type=F32)
        ddt_ref[...] = ddt
        dxbc_ref[:, :SSD_INNER] = dx_all * cm["dte"] + skip_ref[...]
        dalog_ref[0:1, :] += jnp.sum(da * cm["dt"], axis=0, keepdims=True) * cm["a_neg"]

    return pl.pallas_call(
        body, name=name, grid=(nc,),
        in_specs=[pl.BlockSpec((CHUNK, SSD_INNER), lambda c: (rev(c), 0)),
                  pl.BlockSpec((CHUNK, GW), lambda c: (rev(c), XB)),
                  pl.BlockSpec((CHUNK, GW), lambda c: (rev(c), XB + 1)),
                  pl.BlockSpec((CHUNK, LANES), lambda c: (rev(c), 0)),
                  pl.BlockSpec((1, LANES), lambda c: (0, 0)),
                  pl.BlockSpec((LANES, SSD_INNER), lambda c: (0, 0)),
                  pl.BlockSpec((CHUNK, SSD_INNER), lambda c: (rev(c), 0)),
                  pl.BlockSpec((CHUNK, SSD_INNER), lambda c: (rev(c), 0)),
                  pl.BlockSpec((1, SSD_STATE, SSD_INNER), lambda c: (rev(c), 0, 0))],
        out_specs=[pl.BlockSpec((CHUNK, SSD_CONV_DIM), lambda c: (rev(c), 0)),
                   pl.BlockSpec((CHUNK, LANES), lambda c: (rev(c), 0)),
                   pl.BlockSpec((SUBLANES, LANES), lambda c: (0, 0))],
        out_shape=[jax.ShapeDtypeStruct((Tp, SSD_CONV_DIM), F32), jax.ShapeDtypeStruct((Tp, LANES), F32),
                   jax.ShapeDtypeStruct((SUBLANES, LANES), F32)],
        scratch_shapes=[pltpu.VMEM((SSD_STATE, SSD_INNER), F32), pltpu.VMEM((CHUNK, SSD_INNER), F32),
                        pltpu.VMEM((CHUNK, SSD_INNER), F32), pltpu.VMEM((CHUNK, SSD_INNER), F32)],
        compiler_params=_params(("arbitrary",), 32 * CHUNK * SSD_INNER * 4),
    )(xbc, xbc, xbc, dt, alog, e, dy, dxs_skip, states)


def _loss_head(h, target, name):
    Tp, d = h.shape
    nt = Tp // LANES

    def body(h_ref, t_ref, dh_ref, l_ref):
        real = pl.program_id(0) > 0
        err = jnp.where(real, h_ref[...] - t_ref[...], 0.0)
        dh_ref[...] = err * (1.0 / d)
        l_ref[...] = jnp.broadcast_to(0.5 * jnp.sum(err * err) * (1.0 / d), l_ref.shape)

    return pl.pallas_call(
        body, name=name, grid=(nt,),
        in_specs=[pl.BlockSpec((LANES, d), lambda i: (i, 0)),
                  pl.BlockSpec((LANES, d), lambda i: (jnp.maximum(i - 1, 0), 0))],
        out_specs=[pl.BlockSpec((LANES, d), lambda i: (i, 0)),
                   pl.BlockSpec((1, SUBLANES, LANES), lambda i: (i, 0, 0))],
        out_shape=[jax.ShapeDtypeStruct((Tp, d), F32), jax.ShapeDtypeStruct((nt, SUBLANES, LANES), F32)],
        compiler_params=_params(("parallel",), 8 * LANES * d * 4),
    )(h, target)


def _adamw(parts, w, m, v, name):
    shape = w.shape
    C = shape[-1]
    R = int(np.prod(shape[:-1]))
    npart = parts.shape[0]
    parts, w, m, v = parts.reshape(npart, R, C), w.reshape(R, C), m.reshape(R, C), v.reshape(R, C)
    lanes = -(-C // LANES) * LANES
    tr = _pick(R, max(BF16_ROWS, ADAM_ELEMS // lanes), BF16_ROWS) if R % BF16_ROWS == 0 else R
    c1 = 1.0 / (1.0 - ADAM_B1 ** ADAM_STEP)
    c2 = 1.0 / (1.0 - ADAM_B2 ** ADAM_STEP)

    def body(p_ref, w_ref, m_ref, v_ref, g_out, d_out, m_out, v_out):
        g = p_ref[0].astype(F32)
        for p in range(1, npart):
            g = g + p_ref[p].astype(F32)
        m_new = ADAM_B1 * m_ref[...] + (1.0 - ADAM_B1) * g
        v_new = ADAM_B2 * v_ref[...] + (1.0 - ADAM_B2) * (g * g)
        g_out[...] = g
        m_out[...] = m_new
        v_out[...] = v_new
        d_out[...] = -ADAM_LR * ((m_new * c1) / (jnp.sqrt(v_new * c2) + ADAM_EPS) + ADAM_WD * w_ref[...])

    spec = pl.BlockSpec((tr, C), lambda i: (i, 0))
    est = npart * tr * lanes * parts.dtype.itemsize + 7 * tr * lanes * 4
    res = pl.pallas_call(
        body, name=name, grid=(R // tr,),
        in_specs=[pl.BlockSpec((npart, tr, C), lambda i: (0, i, 0)), spec, spec, spec],
        out_specs=[spec] * 4, out_shape=[jax.ShapeDtypeStruct((R, C), F32)] * 4,
        compiler_params=_params(("parallel",), est),
    )(parts, w, m, v)
    return [r.reshape(shape) for r in res]


MESH_ID = pl.DeviceIdType.MESH
N_PEERS = N_DEV - 1


def _dev_index(p):
    return 4 * p[0] + 2 * p[1] + p[2]


class _Background:
    def __init__(self, kind, arrs):
        self.kind, self.arrs, self.n = kind, list(arrs), len(arrs)
        self.npairs = N_PEERS if kind == "gather" else N_CHIPS - 1
        lead = (N_DEV,) if kind == "gather" else ()
        self.out_shape = [jax.ShapeDtypeStruct(lead + a.shape, a.dtype) for a in self.arrs]
        self.specs = [pl.BlockSpec(memory_space=pl.ANY)] * self.n
        self.scratch = [pltpu.SemaphoreType.DMA((self.n, self.npairs)), pltpu.SemaphoreType.DMA((self.n, self.npairs)),
                        pltpu.SemaphoreType.DMA((self.n,))]

    def copies(self, in_refs, out_refs, sems):
        send_sems, recv_sems, local_sems = sems
        x, y, c = lax.axis_index("x"), lax.axis_index("y"), lax.axis_index("c")
        sends, recvs, locals_ = [], [], []

        def remote(t, k, src, dst, to):
            return pltpu.make_async_remote_copy(src_ref=src, dst_ref=dst, send_sem=send_sems.at[t, k],
                                                recv_sem=recv_sems.at[t, k], device_id=to, device_id_type=MESH_ID)

        if self.kind == "gather":
            me = _dev_index((x, y, c))
            peers = [(x, y, 1 - c), (1 - x, y, c), (x, 1 - y, c), (1 - x, 1 - y, c),
                     (1 - x, y, 1 - c), (x, 1 - y, 1 - c), (1 - x, 1 - y, 1 - c)]
            for t in range(self.n):
                locals_.append(pltpu.make_async_copy(in_refs[t], out_refs[t].at[me], local_sems.at[t]))
                for k, p in enumerate(peers):
                    sends.append(remote(t, k, in_refs[t], out_refs[t].at[me], p))
                    recvs.append(remote(t, k, in_refs[t], out_refs[t].at[_dev_index(p)], p))
        else:
            mine = 2 * x + y
            peers = [(1 - x, y), (x, 1 - y), (1 - x, 1 - y)]
            for t in range(self.n):
                locals_.append(pltpu.make_async_copy(in_refs[t].at[mine], out_refs[t].at[mine], local_sems.at[t]))
                for k, p in enumerate(peers):
                    theirs = 2 * p[0] + p[1]
                    sends.append(remote(t, k, in_refs[t].at[theirs], out_refs[t].at[mine], (*p, c)))
                    recvs.append(remote(t, k, in_refs[t].at[mine], out_refs[t].at[theirs], (*p, c)))
        return sends, recvs, locals_

    def start(self, in_refs, out_refs, sems):
        sends, _, locals_ = self.copies(in_refs, out_refs, sems)
        for cp in locals_ + sends:
            cp.start()

    def wait(self, in_refs, out_refs, sems):
        sends, recvs, locals_ = self.copies(in_refs, out_refs, sems)
        for cp in recvs:
            cp.wait_recv()
        for cp in sends:
            cp.wait_send()
        for cp in locals_:
            cp.wait()


def _comm_call(body, name, arrs, out_shape, npairs):
    n = len(arrs)
    any_spec = pl.BlockSpec(memory_space=pl.ANY)
    return pl.pallas_call(
        functools.partial(body, n), name=name, in_specs=[any_spec] * n, out_specs=[any_spec] * n, out_shape=out_shape,
        scratch_shapes=[pltpu.SemaphoreType.DMA((n, npairs)), pltpu.SemaphoreType.DMA((n, npairs)),
                        pltpu.SemaphoreType.DMA((n,))],
    )(*arrs)


def _allgather(arrs, name):
    def body(n, *refs):
        src_refs, out_refs = refs[:n], refs[n:2 * n]
        send_sems, recv_sems, local_sems = refs[2 * n:]
        x, y, c = lax.axis_index("x"), lax.axis_index("y"), lax.axis_index("c")
        me, sibling = (x, y, c), (x, y, 1 - c)
        chips = [(1 - x, y), (x, 1 - y), (1 - x, 1 - y)]

        def copy(t, k, block, to, src=None):
            slot = out_refs[t].at[_dev_index(block)]
            return pltpu.make_async_remote_copy(
                src_ref=slot if src is None else src, dst_ref=slot,
                send_sem=send_sems.at[t, k], recv_sem=recv_sems.at[t, k],
                device_id=to, device_id_type=MESH_ID)

        sends, locals_ = [], []
        for t in range(n):
            mine = pltpu.make_async_copy(src_refs[t], out_refs[t].at[_dev_index(me)], local_sems.at[t])
            mine.start()
            locals_.append(mine)
            first = [copy(t, 0, me, sibling, src=src_refs[t])]
            first += [copy(t, 1 + j, me, (*chip, c), src=src_refs[t]) for j, chip in enumerate(chips)]
            for cp in first:
                cp.start()
            sends += first
        for j, chip in enumerate(chips):
            for t in range(n):
                copy(t, 1 + j, (*chip, c), me).wait_recv()
                passed = copy(t, 4 + j, (*chip, c), sibling)
                passed.start()
                sends.append(passed)
        for t in range(n):
            copy(t, 0, sibling, me).wait_recv()
            for j, chip in enumerate(chips):
                copy(t, 4 + j, (*chip, 1 - c), me).wait_recv()
        for cp in sends:
            cp.wait_send()
        for cp in locals_:
            cp.wait()

    return _comm_call(body, name, arrs, [jax.ShapeDtypeStruct((N_DEV,) + a.shape, a.dtype) for a in arrs], N_PEERS)


N_CHIPS = N_DEV // 2
CHIPS = [(0, 0), (0, 1), (1, 0), (1, 1)]


def _sibling_exchange(arrs, name):
    def body(n, *refs):
        in_refs, out_refs = refs[:n], refs[n:2 * n]
        send_sems, recv_sems, _ = refs[2 * n:]
        x, y, c = lax.axis_index("x"), lax.axis_index("y"), lax.axis_index("c")
        sibling = (x, y, 1 - c)

        def copy(t, j):
            return pltpu.make_async_remote_copy(
                src_ref=in_refs[t].at[_dev_index((*CHIPS[j], 1 - c))], dst_ref=out_refs[t].at[j],
                send_sem=send_sems.at[t, j], recv_sem=recv_sems.at[t, j],
                device_id=sibling, device_id_type=MESH_ID)

        copies = [copy(t, j) for t in range(n) for j in range(N_CHIPS)]
        for cp in copies:
            cp.start()
        for cp in copies:
            cp.wait_recv()
        for cp in copies:
            cp.wait_send()

    return _comm_call(body, name, arrs, [jax.ShapeDtypeStruct((N_CHIPS,) + a.shape[1:], a.dtype) for a in arrs], N_CHIPS)


def _chip_exchange(arrs, name):
    def body(n, *refs):
        in_refs, out_refs = refs[:n], refs[n:2 * n]
        send_sems, recv_sems, local_sems = refs[2 * n:]
        x, y, c = lax.axis_index("x"), lax.axis_index("y"), lax.axis_index("c")
        mine = 2 * x + y
        peers = [(1 - x, y), (x, 1 - y), (1 - x, 1 - y)]

        def copy(t, k, src_chip, dst_chip, to):
            return pltpu.make_async_remote_copy(
                src_ref=in_refs[t].at[src_chip], dst_ref=out_refs[t].at[dst_chip],
                send_sem=send_sems.at[t, k], recv_sem=recv_sems.at[t, k],
                device_id=(*to, c), device_id_type=MESH_ID)

        sends, locals_ = [], []
        for t in range(n):
            own = pltpu.make_async_copy(in_refs[t].at[mine], out_refs[t].at[mine], local_sems.at[t])
            own.start()
            locals_.append(own)
            for k, p in enumerate(peers):
                cp = copy(t, k, 2 * p[0] + p[1], mine, p)
                cp.start()
                sends.append(cp)
        for t in range(n):
            for k, p in enumerate(peers):
                copy(t, k, mine, 2 * p[0] + p[1], p).wait_recv()
        for cp in sends:
            cp.wait_send()
        for cp in locals_:
            cp.wait()

    return _comm_call(body, name, arrs, [jax.ShapeDtypeStruct(a.shape, a.dtype) for a in arrs], N_CHIPS - 1)


def _add_pairs(a, b, name):
    shape = a.shape
    C = shape[-1]
    R = int(np.prod(shape[:-1]))
    lanes = -(-C // LANES) * LANES
    tr = _pick(R, max(BF16_ROWS, 2 * ADAM_ELEMS // lanes), BF16_ROWS) if R % BF16_ROWS == 0 else R

    def body(a_ref, b_ref, o_ref):
        o_ref[...] = (a_ref[...].astype(F32) + b_ref[...].astype(F32)).astype(o_ref.dtype)

    spec = pl.BlockSpec((tr, C), lambda i: (i, 0))
    return pl.pallas_call(
        body, name=name, grid=(R // tr,), in_specs=[spec, spec], out_specs=spec,
        out_shape=jax.ShapeDtypeStruct((R, C), a.dtype),
        compiler_params=_params(("parallel",), 3 * tr * lanes * 4),
    )(a.reshape(R, C), b.reshape(R, C)).reshape(shape)


WEIGHTS = ['meta_tokens', 'emb_ln_g', 'emb_ln_b', 'w_in', 'q_norm_g', 'w_q_b', 'kv_norm_g', 'w_kv_b', 'w_o_attn',
           'ssd_conv_w', 'ssd_conv_b', 'dt_bias', 'a_log', 'd_skip', 'ssd_norm_g', 'w_o_ssd', 'w_out', 'ln1_g',
           'ln1_b', 'w_up', 'ffn_conv_w', 'ffn_conv_b', 'w_down', 'ln2_g', 'ln2_b']
BIG = {'w_in': 2, 'w_q_b': 2, 'w_kv_b': 2, 'w_o_attn': 1, 'w_o_ssd': 1, 'w_out': 1, 'w_up': 2, 'w_down': 1}
SMALL_SHARDED = {'meta_tokens': 1, 'ssd_conv_w': 2, 'ffn_conv_w': 2}
REPLICATED = [n for n in WEIGHTS if n not in BIG and n not in SMALL_SHARDED]
FIRST_USED = ['w_in', 'w_q_b', 'w_kv_b']
AFTER_ATTENTION = [n for n in BIG if n not in FIRST_USED]
SMALL_COLS = LANES


def _flatten(arrs, cols, row_mult, lead=False):
    parts, offs, off = [], [], 0
    for a in arrs:
        a2 = a.reshape(N_DEV, -1) if lead else a.reshape(1, -1)
        n = a2.shape[1]
        pad = -n % cols
        parts.append(jnp.pad(a2, ((0, 0), (0, pad))))
        offs.append((off, n))
        off += n + pad
    rows = off // cols
    extra = (-rows % row_mult) * cols
    if extra:
        parts.append(jnp.zeros((parts[0].shape[0], extra), parts[0].dtype))
    flat = jnp.concatenate(parts, axis=1)
    flat = flat.reshape(flat.shape[0], -1, cols)
    return (flat if lead else flat[0]), offs


def _unflatten(flat, offs, shapes):
    f = flat.reshape(-1)
    return [f[o:o + n].reshape(s) for (o, n), s in zip(offs, shapes)]


def _to_pieces(g, axis):
    s = g.shape[axis] // N_DEV
    g = g.reshape(g.shape[:axis] + (N_DEV, s) + g.shape[axis + 1:])
    return jnp.moveaxis(g, axis, 0).reshape(N_DEV, -1)


def _from_pieces(p, shard_shape, axis):
    g = jnp.moveaxis(p.reshape((N_DEV,) + tuple(shard_shape)), 0, axis)
    sh = list(shard_shape)
    sh[axis] *= N_DEV
    return g.reshape(sh)


def _in_proj_pad(w):
    e = np.cumsum((0,) + IN_SIZES)
    ql, kvl, kpe, z, xbc, dt, ga, gs = [w[:, e[j]:e[j + 1]] for j in range(8)]
    zc = lambda n: jnp.zeros((w.shape[0], n), w.dtype)
    return jnp.concatenate([ql, kvl, z, xbc, ga, gs, kpe, zc(LANES - QK_ROPE), dt, zc(LANES - SSD_HEADS)], axis=1)


def _in_proj_unpad(d):
    seg = lambda o, n: d[:, o:o + n]
    return jnp.concatenate([seg(OQ, Q_LORA), seg(OKV, KV_LORA), seg(OKPE, QK_ROPE), seg(OZ, SSD_INNER),
                            seg(OXBC, SSD_CONV_DIM), seg(ODT, SSD_HEADS), seg(OGA, D_MODEL), seg(OGS, D_MODEL)], axis=1)


def _q_pad(w):
    w3 = w.reshape(Q_LORA, HEADS, QK_NOPE + QK_ROPE)
    return jnp.concatenate([w3, jnp.zeros((Q_LORA, HEADS, QHEAD - QK_NOPE - QK_ROPE), w.dtype)], axis=2).reshape(Q_LORA, HEADS * QHEAD)


def _q_unpad(d):
    return d.reshape(Q_LORA, HEADS, QHEAD)[:, :, :QK_NOPE + QK_ROPE].reshape(Q_LORA, HEADS * (QK_NOPE + QK_ROPE))


def _kv_perm(w):
    w3 = w.reshape(KV_LORA, HEADS, QK_NOPE + V_HEAD)
    return jnp.concatenate([w3[:, :, :QK_NOPE].reshape(KV_LORA, -1), w3[:, :, QK_NOPE:].reshape(KV_LORA, -1)], axis=1)


def _kv_unperm(d):
    kn = d[:, :HEADS * QK_NOPE].reshape(KV_LORA, HEADS, QK_NOPE)
    v = d[:, HEADS * QK_NOPE:].reshape(KV_LORA, HEADS, V_HEAD)
    return jnp.concatenate([kn, v], axis=2).reshape(KV_LORA, HEADS * (QK_NOPE + V_HEAD))


def _row_vec(v, width=None):
    v = v.reshape(1, -1).astype(F32)
    if width is not None and v.shape[1] < width:
        v = jnp.pad(v, ((0, 0), (0, width - v.shape[1])))
    return v


def _pad_rows8(w):
    return jnp.pad(w.astype(F32), ((0, SUBLANES - w.shape[0]), (0, 0)))


def _tables(Tp, npad):
    pos = jnp.maximum(jnp.arange(Tp, dtype=jnp.int32) - npad, 0).astype(F32)
    inv_freq = 1.0 / (ROPE_THETA ** (jnp.arange(0, QK_ROPE, 2, dtype=F32) / QK_ROPE))
    ang = pos[:, None] * inv_freq[None, :]
    ang = jnp.concatenate([ang, ang], axis=-1)
    zeros = jnp.zeros((Tp, LANES - QK_ROPE), F32)
    cos = jnp.concatenate([jnp.cos(ang), zeros], axis=1)
    sin = jnp.concatenate([jnp.sin(ang), zeros], axis=1)
    rot = np.zeros((LANES, LANES), np.float32)
    half = QK_ROPE // 2
    for i in range(half):
        rot[i + half, i] = -1.0
        rot[i, i + half] = 1.0
    expand = np.zeros((LANES, SSD_INNER), np.float32)
    for hd in range(SSD_HEADS):
        expand[hd, hd * SSD_HEAD_DIM:(hd + 1) * SSD_HEAD_DIM] = 1.0
    return cos, sin, jnp.asarray(rot), jnp.asarray(expand)


def _layer_rows(proj, tb):
    rows_a = [_row(proj, Q_LORA, OQ // Q_LORA), _row(proj, KV_LORA, OKV // KV_LORA), _row(proj, LANES, OKPE // LANES),
              _row(proj, LANES, ODT // LANES), _row(tb["cos"], diff=False), _row(tb["sin"], diff=False)]
    return rows_a


def _layer_fwd(h, h_bf, P, tb, fns, npad, bg=None, on_carried=None):
    both = [_out(D_MODEL, F32), _out(D_MODEL, BF16)]
    res_ln_twice = lambda *a: fns["res_ln"](*a) * 2
    proj = _mm(h_bf, P["w_in"], F32, "in_proj")
    rows_a = _layer_rows(proj, tb)
    consts_a = [_row(tb["rot"], diff=False), _row(P["q_norm_g"]), _row(P["kv_norm_g"]), _row(P["dt_bias"])]
    qn, kvn, kr8, dt = _rw_fwd(fns["in_post"], rows_a, consts_a,
                               [_out(Q_LORA, BF16), _out(KV_LORA, BF16), _out(HEADS * LANES, BF16), _out(LANES, F32)],
                               "in_post")
    q = _mm(qn, P["w_q"], F32, "q_proj")
    rows_q = [_row(q, QHEAD, 0, grp=True), _row(tb["cos"], diff=False), _row(tb["sin"], diff=False)]
    qr = _rw_fwd(fns["q_post"], rows_q, [_row(tb["rot"], diff=False)], [_out(HEADS * QHEAD, BF16, QHEAD, grp=True)],
                 "q_post", ng=HEADS)[0]
    kv = _mm(kvn, P["w_kv"], BF16, "kv_proj")
    o, lse, *carried = _flash_fwd(qr, kv, kr8, npad, "attn_fwd_gather" if bg else "attn_fwd", bg=bg)
    if on_carried is not None:
        carried = on_carried(P, carried)
    ya = _mm(o, P["w_o_attn"], F32, "attn_out")
    xbc = _conv_fwd(proj, OXBC, SSD_CONV_DIM, P["ssd_conv_w"], P["ssd_conv_b"], SSD_CONV, True, npad, "ssd_conv")
    y, states = _ssd_fwd(xbc, dt, P["a_log"], tb["expand"], "ssd_fwd")
    rows_b = [_row(y, GW, 0, grp=True), _row(xbc, GW, 0, grp=True), _row(proj, GW, OZ // GW, grp=True)]
    consts_b = [_row(P["d_skip"], GW, 0, grp=True), _row(P["ssd_norm_g"], GW, 0, grp=True)]
    yn = _rw_fwd(fns["gated"], rows_b, consts_b, [_out(SSD_INNER, BF16, GW, grp=True)], "ssd_gate", ng=SSD_GROUPS)[0]
    ys = _mm(yn, P["w_o_ssd"], F32, "ssd_out")
    rows_c = [_row(proj, D_MODEL, OGA // D_MODEL), _row(proj, D_MODEL, OGS // D_MODEL), _row(ya), _row(ys)]
    mixed = _rw_fwd(fns["mix"], rows_c, [], [_out(D_MODEL, BF16)], "mix")[0]
    mo = _mm(mixed, P["w_out"], F32, "mix_out")
    consts_1 = [_row(P["ln1_g"]), _row(P["ln1_b"])]
    h1, h1_bf = _rw_fwd(res_ln_twice, [_row(h), _row(mo)], consts_1, both, "ln1")
    up = _mm(h1_bf, P["w_up"], BF16, "ffn_up")
    u = _conv_fwd(up, 0, 2 * D_FF, P["ffn_conv_w"], P["ffn_conv_b"], FFN_CONV, False, npad, "ffn_conv", BF16)
    act = _rw_fwd(fns["glu"], [_row(u)], [], [_out(D_FF, BF16)], "ffn_glu")[0]
    fo = _mm(act, P["w_down"], F32, "ffn_down")
    consts_2 = [_row(P["ln2_g"]), _row(P["ln2_b"])]
    h2, h2_bf = _rw_fwd(res_ln_twice, [_row(h1), _row(fo)], consts_2, both, "ln2")
    res = dict(h=h, h_bf=h_bf, proj=proj, qn=qn, kvn=kvn, kr8=kr8, dt=dt, q=q, qr=qr, kv=kv, o=o, lse=lse, ya=ya,
               xbc=xbc, y=y, states=states, yn=yn, ys=ys, mixed=mixed, mo=mo, h1=h1, h1_bf=h1_bf, up=up, u=u, act=act,
               fo=fo)
    return h2, h2_bf, res, carried


def _layer_bwd(dh2, r, P, tb, fns, npad, bg=None, before_attn=None):
    g = {}
    consts_2 = [_row(P["ln2_g"]), _row(P["ln2_b"])]
    (dh1_a, dfo), (g["ln2_g"], g["ln2_b"]) = _rw_bwd(fns["res_ln"], [_row(r["h1"]), _row(r["fo"])], consts_2,
                                                     [_row(dh2)], [F32, BF16], "ln2_bwd")
    g["w_down"] = _mm(r["act"], dfo, BF16, "dw_down", ta=True)
    dact = _mm(dfo, P["w_down"], BF16, "d_act", tb=True)
    (du,), _ = _rw_bwd(fns["glu"], [_row(r["u"])], [], [_row(dact)], [BF16], "glu_bwd")
    dup, g["ffn_conv_w"], g["ffn_conv_b"] = _conv_bwd(r["up"], 0, 2 * D_FF, P["ffn_conv_w"], P["ffn_conv_b"], du,
                                                      FFN_CONV, False, npad, "ffn_conv_bwd")
    g["w_up"] = _mm(r["h1_bf"], dup, BF16, "dw_up", ta=True)
    dh1 = _mm(dup, P["w_up"], F32, "d_h1", tb=True, add=dh1_a)
    consts_1 = [_row(P["ln1_g"]), _row(P["ln1_b"])]
    (dh_a, dmo), (g["ln1_g"], g["ln1_b"]) = _rw_bwd(fns["res_ln"], [_row(r["h"]), _row(r["mo"])], consts_1,
                                                    [_row(dh1)], [F32, BF16], "ln1_bwd")
    g["w_out"] = _mm(r["mixed"], dmo, BF16, "dw_out", ta=True)
    dmixed = _mm(dmo, P["w_out"], F32, "d_mixed", tb=True)
    proj = r["proj"]
    rows_c = [_row(proj, D_MODEL, OGA // D_MODEL), _row(proj, D_MODEL, OGS // D_MODEL), _row(r["ya"]), _row(r["ys"])]
    (dga, dgs, dya, dys), _ = _rw_bwd(fns["mix"], rows_c, [], [_row(dmixed)], [BF16] * 4, "mix_bwd")
    g["w_o_attn"] = _mm(r["o"], dya, BF16, "dw_o_attn", ta=True)
    do = _mm(dya, P["w_o_attn"], F32, "d_o", tb=True)
    g["w_o_ssd"] = _mm(r["yn"], dys, BF16, "dw_o_ssd", ta=True)
    dyn = _mm(dys, P["w_o_ssd"], F32, "d_yn", tb=True)
    rows_b = [_row(r["y"], GW, 0, grp=True), _row(r["xbc"], GW, 0, grp=True), _row(proj, GW, OZ // GW, grp=True)]
    consts_b = [_row(P["d_skip"], GW, 0, grp=True), _row(P["ssd_norm_g"], GW, 0, grp=True)]
    (dy, dxs_skip, dz), (g["d_skip"], g["ssd_norm_g"]) = _rw_bwd(
        fns["gated"], rows_b, consts_b, [_row(dyn, GW, 0, grp=True)], [F32, F32, BF16], "ssd_gate_bwd", ng=SSD_GROUPS)
    dxbc, ddt, g["a_log"] = _ssd_bwd(r["xbc"], r["dt"], P["a_log"], tb["expand"], dy, dxs_skip, r["states"], "ssd_bwd")
    dxbc_pre, g["ssd_conv_w"], g["ssd_conv_b"] = _conv_bwd(proj, OXBC, SSD_CONV_DIM, P["ssd_conv_w"], P["ssd_conv_b"],
                                                           dxbc, SSD_CONV, True, npad, "ssd_conv_bwd")
    delta = _attn_delta(do, r["o"], "attn_delta")
    if before_attn is not None:
        bg = before_attn(g)
    dqr, dkn, dkr8, dv, *carried = _flash_bwd(r["qr"], r["kv"], r["kr8"], do, r["lse"], delta, npad,
                                              "attn_bwd_exchange" if bg else "attn_bwd", bg=bg)
    rows_q = [_row(r["q"], QHEAD, 0, grp=True), _row(tb["cos"], diff=False), _row(tb["sin"], diff=False)]
    (dq,), _ = _rw_bwd(fns["q_post"], rows_q, [_row(tb["rot"], diff=False)], [_row(dqr, QHEAD, 0, grp=True)], [BF16],
                       "q_post_bwd", ng=HEADS)
    g["w_q"] = _mm(r["qn"], dq, BF16, "dw_q", ta=True)
    dqn = _mm(dq, P["w_q"], F32, "d_qn", tb=True)
    dkv = jnp.concatenate([dkn, dv], axis=1)
    g["w_kv"] = _mm(r["kvn"], dkv, BF16, "dw_kv", ta=True)
    dkvn = _mm(dkv, P["w_kv"], F32, "d_kvn", tb=True)
    rows_a = _layer_rows(proj, tb)
    consts_a = [_row(tb["rot"], diff=False), _row(P["q_norm_g"]), _row(P["kv_norm_g"]), _row(P["dt_bias"])]
    (dql, dkvl, dkpe, ddtr), (g["q_norm_g"], g["kv_norm_g"], g["dt_bias"]) = _rw_bwd(
        fns["in_post"], rows_a, consts_a, [_row(dqn), _row(dkvn), _row(dkr8), _row(ddt)], [BF16] * 4, "in_post_bwd")
    dproj = jnp.concatenate([dql, dkvl, dz, dxbc_pre, dga, dgs, dkpe, ddtr], axis=1)
    g["w_in"] = _mm(r["h_bf"], dproj, BF16, "dw_in", ta=True)
    dh = _mm(dproj, P["w_in"], F32, "d_h", tb=True, add=dh_a)
    return dh, g, carried


def _full_weight(g, axis):
    if axis == 1:
        return g.reshape(-1, g.shape[-1])
    return jnp.transpose(g, (1, 0, 2)).reshape(g.shape[1], -1)


def _grad_pieces(d, axis):
    if axis == 1:
        return d.reshape(N_DEV, -1, d.shape[1])
    return jnp.transpose(d.reshape(d.shape[0], N_DEV, -1), (1, 0, 2))


def _big_params(gathered):
    prep = {"w_in": ("w_in", _in_proj_pad), "w_q_b": ("w_q", _q_pad), "w_kv_b": ("w_kv", _kv_perm)}
    P = {}
    for n, g in gathered.items():
        key, fn = prep.get(n, (n, lambda a: a))
        P[key] = fn(_full_weight(g, BIG[n]))
    return P


def _layer_params(gathered, small, i):
    P = _big_params(gathered)
    P["q_norm_g"] = _row_vec(small["q_norm_g"][i])
    P["kv_norm_g"] = _row_vec(small["kv_norm_g"][i])
    P["dt_bias"] = _row_vec(small["dt_bias"][i], LANES)
    P["a_log"] = _row_vec(small["a_log"][i], LANES)
    P["d_skip"] = _row_vec(jnp.repeat(small["d_skip"][i], SSD_HEAD_DIM))
    P["ssd_norm_g"] = _row_vec(small["ssd_norm_g"][i])
    P["ssd_conv_w"] = _pad_rows8(small["ssd_conv_w"][i])
    P["ssd_conv_b"] = _row_vec(small["ssd_conv_b"][i])
    P["ffn_conv_w"] = _pad_rows8(small["ffn_conv_w"][i])
    P["ffn_conv_b"] = _row_vec(small["ffn_conv_b"][i])
    for n in ("ln1_g", "ln1_b", "ln2_g", "ln2_b"):
        P[n] = _row_vec(small[n][i])
    return P


def _layer_grads_to_reference_layout(g):
    out = {}
    out["w_in"] = _in_proj_unpad(g["w_in"])
    out["w_q_b"] = _q_unpad(g["w_q"])
    out["w_kv_b"] = _kv_unperm(g["w_kv"])
    for n in ("w_o_attn", "w_o_ssd", "w_out", "w_up", "w_down"):
        out[n] = g[n]
    out["q_norm_g"] = g["q_norm_g"][0]
    out["kv_norm_g"] = g["kv_norm_g"][0]
    out["dt_bias"] = g["dt_bias"][0, :SSD_HEADS]
    out["a_log"] = g["a_log"][0, :SSD_HEADS]
    out["d_skip"] = g["d_skip"].reshape(SSD_HEADS, SSD_HEAD_DIM).sum(axis=1)
    out["ssd_norm_g"] = g["ssd_norm_g"][0]
    out["ssd_conv_w"] = g["ssd_conv_w"][:SSD_CONV]
    out["ssd_conv_b"] = g["ssd_conv_b"][0]
    out["ffn_conv_w"] = g["ffn_conv_w"][:FFN_CONV]
    out["ffn_conv_b"] = g["ffn_conv_b"][0]
    for n in ("ln1_g", "ln1_b", "ln2_g", "ln2_b"):
        out[n] = g[n][0]
    return out


def kernel(x, meta_tokens, emb_ln_g, emb_ln_b, w_in, q_norm_g, w_q_b, kv_norm_g, w_kv_b, w_o_attn, ssd_conv_w, ssd_conv_b, dt_bias, a_log, d_skip, ssd_norm_g, w_o_ssd, w_out, ln1_g, ln1_b, w_up, ffn_conv_w, ffn_conv_b, w_down, ln2_g, ln2_b, loss_target, m_meta_tokens, m_emb_ln_g, m_emb_ln_b, m_w_in, m_q_norm_g, m_w_q_b, m_kv_norm_g, m_w_kv_b, m_w_o_attn, m_ssd_conv_w, m_ssd_conv_b, m_dt_bias, m_a_log, m_d_skip, m_ssd_norm_g, m_w_o_ssd, m_w_out, m_ln1_g, m_ln1_b, m_w_up, m_ffn_conv_w, m_ffn_conv_b, m_w_down, m_ln2_g, m_ln2_b, v_meta_tokens, v_emb_ln_g, v_emb_ln_b, v_w_in, v_q_norm_g, v_w_q_b, v_kv_norm_g, v_w_kv_b, v_w_o_attn, v_ssd_conv_w, v_ssd_conv_b, v_dt_bias, v_a_log, v_d_skip, v_ssd_norm_g, v_w_o_ssd, v_w_out, v_ln1_g, v_ln1_b, v_w_up, v_ffn_conv_w, v_ffn_conv_b, v_w_down, v_ln2_g, v_ln2_b):
    given = dict(locals())
    w = {n: given[n] for n in WEIGHTS}
    m = {n: given["m_" + n] for n in WEIGHTS}
    v = {n: given["v_" + n] for n in WEIGHTS}
    seq = x.shape[1]
    assert x.shape[0] == 1 and seq % LANES == 0
    npad = LANES - N_META
    Tp = npad + N_META + seq
    depth = w_in.shape[0]

    big_names, small_names = list(BIG), list(SMALL_SHARDED)
    ws, offs_s = _flatten([w[n] for n in small_names], SMALL_COLS, SUBLANES)
    shards = [{n: w[n][i].astype(BF16) for n in big_names} for i in range(depth)]
    got = _allgather([shards[0][n] for n in FIRST_USED] + [ws], "weight_allgather")
    gathered = dict(zip(FIRST_USED, got[:-1]))
    gsm = got[-1]
    small = {n: w[n] for n in REPLICATED}
    for n, (o, sz) in zip(small_names, offs_s):
        small[n] = _from_pieces(gsm.reshape(N_DEV, -1)[:, o:o + sz], w[n].shape, SMALL_SHARDED[n])

    fns = _make_stage_fns(npad)
    cos, sin, rot, expand = _tables(Tp, npad)
    tb = dict(cos=cos, sin=sin, rot=rot, expand=expand)
    top = jnp.pad(small["meta_tokens"], ((npad, 0), (0, 0)))
    hcat = jnp.concatenate([top, x[0]], axis=0)
    consts_e = [_row(_row_vec(w["emb_ln_g"])), _row(_row_vec(w["emb_ln_b"]))]
    h, h_bf = _rw_fwd(lambda *a: fns["ln"](*a) * 2, [_row(hcat)], consts_e, [_out(D_MODEL, F32), _out(D_MODEL, BF16)],
                      "emb_ln")
    layers, saved = [], []
    for i in range(depth):
        layers.append(_layer_params(gathered, small, i))
        late = AFTER_ATTENTION if i == 0 else []
        nxt = big_names if i + 1 < depth else []
        arrs = [shards[i][n] for n in late] + [shards[i + 1][n] for n in nxt]

        def on_carried(P, carried, late=late):
            P.update(_big_params(dict(zip(late, carried[:len(late)]))))
            return carried[len(late):]

        h, h_bf, res, carried = _layer_fwd(h, h_bf, layers[i], tb, fns, npad,
                                           bg=_Background("gather", arrs) if arrs else None, on_carried=on_carried)
        gathered = dict(zip(nxt, carried))
        saved.append(res)
    dh, lparts = _loss_head(h, loss_target[0], "loss_head")
    loss = lax.psum(jnp.sum(lparts[:, 0, 0]), ("x", "y", "c"))

    core = lax.axis_index("c")

    def chip_partials(pieces, tag):
        from_sibling = _sibling_exchange(pieces, "grad_exchange_cores_" + tag)
        sums = []
        for k, (p, r) in enumerate(zip(pieces, from_sibling)):
            own = lax.dynamic_index_in_dim(p.reshape((N_CHIPS, 2) + p.shape[1:]), core, axis=1, keepdims=False)
            sums.append(_add_pairs(own, r, "grad_chip_sum_%s_%d" % (tag, k)))
        return sums

    lg, recv_big, pending = [None] * depth, [None] * depth, []
    for i in reversed(range(depth)):
        early = AFTER_ATTENTION if i == 0 else []

        def before_attn(g, pending=pending, early=early, i=i):
            sums = pending + (chip_partials([_grad_pieces(g[n], BIG[n]) for n in early], "l%d_early" % i) if early else [])
            return _Background("chips", sums) if sums else None

        dh, gi, carried = _layer_bwd(dh, saved[i], layers[i], tb, fns, npad, before_attn=before_attn)
        if pending:
            recv_big[i + 1] = dict(zip(big_names, carried[:len(pending)]))
        recv_big[i] = dict(zip(early, carried[len(pending):]))
        lg[i] = _layer_grads_to_reference_layout(gi)
        pending = []
        if i > 0:
            pending = chip_partials([_grad_pieces(lg[i][n], BIG[n]) for n in big_names], "l%d" % i)
    (dhcat,), (d_emb_g, d_emb_b) = _rw_bwd(fns["ln"], [_row(hcat)], consts_e, [_row(dh)], [F32], "emb_ln_bwd")
    grad_x = dhcat[LANES:][None]
    local = {n: jnp.stack([lg[i][n] for i in range(depth)]) for n in lg[0] if n not in BIG}
    local["meta_tokens"] = dhcat[npad:LANES]
    local["emb_ln_g"] = d_emb_g[0]
    local["emb_ln_b"] = d_emb_b[0]

    sm_names = small_names + REPLICATED
    sm_pieces = [_to_pieces(local[n], SMALL_SHARDED[n]) for n in small_names]
    sm_pieces += [jnp.broadcast_to(local[n].reshape(1, -1), (N_DEV, local[n].size)) for n in REPLICATED]
    ps, _ = _flatten(sm_pieces, SMALL_COLS, BF16_ROWS, lead=True)
    pieces = [_grad_pieces(lg[0][n], BIG[n]) for n in FIRST_USED] + [ps]
    recv = _chip_exchange(chip_partials(pieces, "l0"), "grad_exchange_chips")
    recv_big[0].update(zip(FIRST_USED, recv[:-1]))
    outs = {}
    kinds = ("grad", "delta", "new_m", "new_v")
    for n in big_names:
        parts = jnp.stack([recv_big[i][n] for i in range(depth)], axis=1)
        for kind, a in zip(kinds, _adamw(parts, w[n], m[n], v[n], "adamw_" + n)):
            outs[kind + "_" + n] = a
    wf, offs = _flatten([w[n] for n in sm_names], SMALL_COLS, BF16_ROWS)
    mf, _ = _flatten([m[n] for n in sm_names], SMALL_COLS, BF16_ROWS)
    vf, _ = _flatten([v[n] for n in sm_names], SMALL_COLS, BF16_ROWS)
    shapes = [w[n].shape for n in sm_names]
    for kind, flat in zip(kinds, _adamw(recv[-1], wf, mf, vf, "adamw_small")):
        for n, a in zip(sm_names, _unflatten(flat, offs, shapes)):
            outs[kind + "_" + n] = a
    result = [loss, grad_x]
    for kind in ("grad", "delta", "new_m", "new_v"):
        result += [outs[kind + "_" + n] for n in WEIGHTS]
    return tuple(result)
```
